```python
import math
import jax
import jax.numpy as jnp
from jax import lax
import numpy as np

D_MODEL = 1024
BATCH = 8
SEQ = 2048
DEPTH = 2

D_MIX = 2 * D_MODEL
CONV_WIDTH = 4
EPS = 1e-6
N_ADA = 6
SSD_INNER = D_MIX // 2
SSD_HEADDIM = 64
SSD_HEADS = SSD_INNER // SSD_HEADDIM
SSD_GROUPS = 2
SSD_HPG = SSD_HEADS // SSD_GROUPS
SSD_STATE = 128
SSD_CHUNK = 128
SSD_CONV_DIM = SSD_INNER + 2 * SSD_GROUPS * SSD_STATE
HG_WIDTH = D_MIX // 4
HG_EXPAND = 128
HG_HEADS = HG_WIDTH // HG_EXPAND
HG_CHUNK = 64
LRU_WIDTH = D_MIX // 4
LRU_BLOCKS = 8
LRU_BLOCK_W = LRU_WIDTH // LRU_BLOCKS
LRU_C = 8.0
N_EXPERTS = 64
TOP_K = 8
N_EXPERT_GROUPS = 8
E_PER_GROUP = N_EXPERTS // N_EXPERT_GROUPS
TOPK_GROUPS = 4
D_EXPERT = 256
D_SHARED = 256
ROUTED_SCALE = 2.5
MASK_SCORE = -1.0e4
MOE_BLOCK = 256
IN_SIZES = (SSD_INNER, SSD_CONV_DIM, SSD_HEADS, HG_WIDTH, HG_WIDTH, HG_WIDTH, HG_WIDTH, LRU_WIDTH, LRU_WIDTH)
IN_SPLITS = tuple(int(s) for s in np.cumsum(IN_SIZES)[:-1])
D_IN = int(sum(IN_SIZES))

kernel_name = 'hybrid_ssd_hgrn2_rglru_moe_adaln'


def rms_norm(x, w):
    xf = x.astype(jnp.float32)
    y = xf * lax.rsqrt(jnp.mean(xf * xf, axis=-1, keepdims=True) + EPS)
    return (y * w.astype(jnp.float32)).astype(x.dtype)


def modulate(h, shift, scale):
    return h * (1 + scale[:, None, :]) + shift[:, None, :]


def causal_conv(x, w, b):
    y = lax.conv_general_dilated(x, w[:, None, :].astype(x.dtype), window_strides=(1,),
                                 padding=((w.shape[0] - 1, 0),),
                                 dimension_numbers=('NWC', 'WIO', 'NWC'),
                                 feature_group_count=x.shape[-1])
    return y + b.astype(x.dtype)


def segsum_exp(a):
    q = a.shape[-1]
    x = jnp.broadcast_to(a[..., None], a.shape + (q,))
    x = jnp.where(jnp.tril(jnp.ones((q, q), bool), -1), x, 0.0)
    s = jnp.cumsum(x, axis=-2)
    lower = jnp.tril(jnp.ones((q, q), bool))
    return jnp.where(lower, jnp.exp(jnp.where(lower, s, 0.0)), 0.0)


def ssd_mixer(z, xbc, dt_raw, conv_w, conv_b, dt_bias, a_log, d_skip, norm_w):
    bsz, t_len, _ = z.shape
    nc = t_len // SSD_CHUNK
    xbc = jax.nn.silu(causal_conv(xbc, conv_w, conv_b))
    xs, b_in, c_in = jnp.split(xbc, [SSD_INNER, SSD_INNER + SSD_GROUPS * SSD_STATE], axis=-1)
    dt = jax.nn.softplus(dt_raw + dt_bias.astype(jnp.float32))
    a_head = -jnp.exp(a_log.astype(jnp.float32))
    xh = xs.reshape(bsz, t_len, SSD_HEADS, SSD_HEADDIM)
    xdt = (xh * dt[..., None]).reshape(bsz, nc, SSD_CHUNK, SSD_GROUPS, SSD_HPG, SSD_HEADDIM)
    bc = b_in.reshape(bsz, nc, SSD_CHUNK, SSD_GROUPS, SSD_STATE)
    cc = c_in.reshape(bsz, nc, SSD_CHUNK, SSD_GROUPS, SSD_STATE)
    a = (dt * a_head).reshape(bsz, nc, SSD_CHUNK, SSD_GROUPS, SSD_HPG).transpose(0, 3, 4, 1, 2)
    a_cum = jnp.cumsum(a, axis=-1)
    decay_in = segsum_exp(a)
    cb = jnp.einsum('bclgn,bcsgn->bgcls', cc, bc)
    y_diag = jnp.einsum('bgrcls,bcsgrp->bclgrp', cb[:, :, None] * decay_in, xdt)
    decay_states = jnp.exp(a_cum[..., -1:] - a_cum)
    states = jnp.einsum('bclgn,bgrcl,bclgrp->bcgrpn', bc, decay_states, xdt)
    chunk_decay = jnp.pad(a_cum[..., -1], ((0, 0), (0, 0), (0, 0), (1, 0)))
    decay_chunk = segsum_exp(chunk_decay)
    states = jnp.concatenate([jnp.zeros_like(states[:, :1]), states], axis=1)
    states = jnp.einsum('bgrzc,bcgrpn->bzgrpn', decay_chunk, states)[:, :-1]
    y_off = jnp.einsum('bclgn,bcgrpn,bgrcl->bclgrp', cc, states, jnp.exp(a_cum))
    y = (y_diag + y_off).reshape(bsz, t_len, SSD_HEADS, SSD_HEADDIM) + xh * d_skip.astype(jnp.float32)[:, None]
    y = y.reshape(bsz, t_len, SSD_INNER) * jax.nn.silu(z)
    y = rms_norm(y.reshape(bsz, t_len, SSD_GROUPS, SSD_INNER // SSD_GROUPS),
                 norm_w.reshape(SSD_GROUPS, SSD_INNER // SSD_GROUPS))
    return y.reshape(bsz, t_len, SSD_INNER)


def hgrn2_mixer(q, f, v, g, lb, norm_w):
    bsz, t_len, _ = q.shape
    nc = t_len // HG_CHUNK
    q = jax.nn.silu(q)
    lb = lb.astype(jnp.float32)
    k = (1.0 - lb) * jax.nn.sigmoid(-f)
    log_f = jnp.log1p(-k)

    def to_chunks(t):
        return t.reshape(bsz, nc, HG_CHUNK, HG_HEADS, HG_EXPAND).transpose(1, 0, 3, 2, 4)

    causal = jnp.tril(jnp.ones((HG_CHUNK, HG_CHUNK), bool))[:, :, None]

    def step(state, inp):
        qc, kc, vc, lc = inp
        b = jnp.cumsum(lc, axis=2)
        o_inter = jnp.einsum('bhtk,bhkv->bhtv', qc * jnp.exp(b), state)
        diff = b[:, :, :, None, :] - b[:, :, None, :, :]
        decay = jnp.where(causal, jnp.exp(jnp.where(causal, diff, 0.0)), 0.0)
        attn = jnp.einsum('bhtsk,bhsk->bhts', decay * qc[:, :, :, None, :], kc)
        o = o_inter + jnp.einsum('bhts,bhsv->bhtv', attn, vc)
        b_last = b[:, :, -1:, :]
        state = jnp.exp(b_last[:, :, 0, :, None]) * state + jnp.einsum('bhsk,bhsv->bhkv', kc * jnp.exp(b_last - b), vc)
        return state, o

    s0 = jnp.zeros((bsz, HG_HEADS, HG_EXPAND, HG_EXPAND), jnp.float32)
    _, o = lax.scan(step, s0, (to_chunks(q), to_chunks(k), to_chunks(v), to_chunks(log_f)))
    o = o.transpose(1, 0, 3, 2, 4).reshape(bsz, t_len, HG_HEADS, HG_EXPAND)
    o = rms_norm(o, norm_w.reshape(HG_HEADS, HG_EXPAND)).reshape(bsz, t_len, HG_WIDTH)
    return o * jax.nn.silu(g)


def rglru_mixer(gate, xb, conv_w, conv_b, wa, ba, wx, bx, lam, norm_w):
    bsz, t_len, _ = xb.shape
    xb = causal_conv(xb, conv_w, conv_b)
    xblk = xb.reshape(bsz, t_len, LRU_BLOCKS, LRU_BLOCK_W)
    r = jax.nn.sigmoid(jnp.einsum('btgi,gij->btgj', xblk, wa).reshape(bsz, t_len, LRU_WIDTH) + ba)
    i = jax.nn.sigmoid(jnp.einsum('btgi,gij->btgj', xblk, wx).reshape(bsz, t_len, LRU_WIDTH) + bx)
    log_a = -LRU_C * r * jax.nn.softplus(-lam.astype(jnp.float32))
    a = jnp.exp(log_a)
    u = jnp.sqrt(-jnp.expm1(2.0 * log_a)) * (i * xb)

    def combine(lhs, rhs):
        a1, b1 = lhs
        a2, b2 = rhs
        return a1 * a2, a2 * b1 + b2

    _, h = lax.associative_scan(combine, (a, u), axis=1)
    y = h * jax.nn.gelu(gate)
    return rms_norm(y, norm_w)


def hybrid_mixer(h, w_in, ssd_conv_w, ssd_conv_b, ssd_dt_bias, ssd_a_log, ssd_d, ssd_norm_w,
                 hg_lb, hg_norm_w, lru_conv_w, lru_conv_b, lru_wa, lru_ba, lru_wx, lru_bx,
                 lru_lambda, lru_norm_w, w_out):
    u = (h @ w_in).astype(jnp.float32)
    z, xbc, dt_raw, hq, hf, hv, hg, lgate, lx = jnp.split(u, IN_SPLITS, axis=-1)
    y_ssd = ssd_mixer(z, xbc, dt_raw, ssd_conv_w, ssd_conv_b, ssd_dt_bias, ssd_a_log, ssd_d, ssd_norm_w)
    y_hg = hgrn2_mixer(hq, hf, hv, hg, hg_lb, hg_norm_w)
    y_lru = rglru_mixer(lgate, lx, lru_conv_w, lru_conv_b, lru_wa, lru_ba, lru_wx, lru_bx, lru_lambda, lru_norm_w)
    y = jnp.concatenate([y_ssd, y_hg, y_lru], axis=-1).astype(h.dtype)
    return y @ w_out


def swiglu(x, wg, wu, wd):
    return (jax.nn.silu(x @ wg) * (x @ wu)) @ wd


def moe_ffn(h, router_w, router_bias, exp_gate, exp_up, exp_down, sh_gate, sh_up, sh_down):
    bsz, t_len, d = h.shape
    tokens = h.reshape(-1, d)
    n_tok = tokens.shape[0]
    scores = jax.nn.sigmoid((tokens @ router_w).astype(jnp.float32))
    sel = scores + router_bias.astype(jnp.float32)
    grp_score = lax.top_k(sel.reshape(n_tok, N_EXPERT_GROUPS, E_PER_GROUP), 2)[0].sum(-1)
    _, top_g = lax.top_k(grp_score, TOPK_GROUPS)
    group_mask = jnp.any(top_g[:, :, None] == jnp.arange(N_EXPERT_GROUPS)[None, None, :], axis=1)
    expert_mask = jnp.repeat(group_mask, E_PER_GROUP, axis=1)
    _, idx = lax.top_k(jnp.where(expert_mask, sel, MASK_SCORE), TOP_K)
    w = jnp.take_along_axis(scores, idx, axis=1)
    w = w / jnp.sum(w, axis=-1, keepdims=True) * ROUTED_SCALE
    n_assign = n_tok * TOP_K
    flat_e = idx.reshape(-1).astype(jnp.int32)
    flat_w = w.reshape(-1)
    flat_tok = jnp.arange(n_assign, dtype=jnp.int32) // TOP_K
    order = jnp.argsort(flat_e)
    sorted_e = flat_e[order]
    counts = jnp.bincount(flat_e, length=N_EXPERTS)
    padded = (counts + MOE_BLOCK - 1) // MOE_BLOCK * MOE_BLOCK
    pad_end = jnp.cumsum(padded)
    pad_start = pad_end - padded
    start = jnp.cumsum(counts) - counts
    dest = pad_start[sorted_e] + jnp.arange(n_assign, dtype=jnp.int32) - start[sorted_e]
    n_slots = n_assign + N_EXPERTS * MOE_BLOCK
    n_blocks = n_slots // MOE_BLOCK
    slot_tok = jnp.full((n_slots,), n_tok, jnp.int32).at[dest].set(flat_tok[order])
    slot_w = jnp.zeros((n_slots,), tokens.dtype).at[dest].set(flat_w[order].astype(tokens.dtype))
    block_e = jnp.minimum(jnp.searchsorted(pad_end, jnp.arange(n_blocks) * MOE_BLOCK, side='right'),
                          N_EXPERTS - 1).astype(jnp.int32)
    tok_pad = jnp.concatenate([tokens, jnp.zeros((1, d), tokens.dtype)], axis=0)

    def run_block(args):
        t_idx, e, wt = args
        xb = tok_pad[t_idx]
        return swiglu(xb, exp_gate[e], exp_up[e], exp_down[e]) * wt[:, None]

    y_slots = lax.map(run_block, (slot_tok.reshape(n_blocks, MOE_BLOCK), block_e,
                                  slot_w.reshape(n_blocks, MOE_BLOCK)))
    routed = jax.ops.segment_sum(y_slots.reshape(n_slots, d), slot_tok, num_segments=n_tok + 1)[:n_tok]
    shared = swiglu(tokens, sh_gate, sh_up, sh_down)
    return (routed + shared).reshape(bsz, t_len, d)


def setup_inputs(seed: int = 0) -> dict:
    key = jax.random.key(seed)
    ks = iter(jax.random.split(key, 48))
    f32 = jnp.float32
    L = DEPTH

    def nrm(shape, scale):
        return jax.random.normal(next(ks), shape, f32) * scale

    def gain(shape):
        return 1.0 + nrm(shape, 0.02)

    dt = jnp.exp(jax.random.uniform(next(ks), (L, SSD_HEADS), f32, math.log(1e-3), math.log(1e-1)))
    a_c = jax.random.uniform(next(ks), (L, LRU_WIDTH), f32, 0.9, 0.999)
    sig = a_c ** (1.0 / LRU_C)
    return {
        'x': nrm((BATCH, SEQ, D_MODEL), 1.0),
        'c': nrm((BATCH, D_MODEL), 1.0),
        'ada_w': nrm((L, D_MODEL, N_ADA * D_MODEL), 0.5 * D_MODEL ** -0.5),
        'ada_b': nrm((L, N_ADA * D_MODEL), 0.02),
        'norm_mix_w': gain((L, D_MODEL)),
        'norm_ffn_w': gain((L, D_MODEL)),
        'w_in': nrm((L, D_MODEL, D_IN), D_MODEL ** -0.5),
        'ssd_conv_w': nrm((L, CONV_WIDTH, SSD_CONV_DIM), CONV_WIDTH ** -0.5),
        'ssd_conv_b': nrm((L, SSD_CONV_DIM), 0.02),
        'ssd_dt_bias': dt + jnp.log(-jnp.expm1(-dt)),
        'ssd_a_log': jnp.log(jax.random.uniform(next(ks), (L, SSD_HEADS), f32, 1.0, 16.0)),
        'ssd_d': gain((L, SSD_HEADS)),
        'ssd_norm_w': gain((L, SSD_INNER)),
        'hg_lower_bounds': nrm((L, HG_WIDTH), 0.5),
        'hg_norm_w': gain((L, HG_WIDTH)),
        'lru_conv_w': nrm((L, CONV_WIDTH, LRU_WIDTH), CONV_WIDTH ** -0.5),
        'lru_conv_b': nrm((L, LRU_WIDTH), 0.02),
        'lru_wa': nrm((L, LRU_BLOCKS, LRU_BLOCK_W, LRU_BLOCK_W), LRU_BLOCK_W ** -0.5),
        'lru_ba': nrm((L, LRU_WIDTH), 0.02),
        'lru_wx': nrm((L, LRU_BLOCKS, LRU_BLOCK_W, LRU_BLOCK_W), LRU_BLOCK_W ** -0.5),
        'lru_bx': nrm((L, LRU_WIDTH), 0.02),
        'lru_lambda': jnp.log(sig) - jnp.log1p(-sig),
        'lru_norm_w': gain((L, LRU_WIDTH)),
        'w_out': nrm((L, D_MIX, D_MODEL), D_MIX ** -0.5),
        'router_w': nrm((L, D_MODEL, N_EXPERTS), D_MODEL ** -0.5),
        'router_bias': nrm((L, N_EXPERTS), 0.01),
        'exp_gate': nrm((L, N_EXPERTS, D_MODEL, D_EXPERT), D_MODEL ** -0.5),
        'exp_up': nrm((L, N_EXPERTS, D_MODEL, D_EXPERT), D_MODEL ** -0.5),
        'exp_down': nrm((L, N_EXPERTS, D_EXPERT, D_MODEL), D_EXPERT ** -0.5),
        'sh_gate': nrm((L, D_MODEL, D_SHARED), D_MODEL ** -0.5),
        'sh_up': nrm((L, D_MODEL, D_SHARED), D_MODEL ** -0.5),
        'sh_down': nrm((L, D_SHARED, D_MODEL), D_SHARED ** -0.5),
        'final_norm_w': gain((D_MODEL,)),
    }


def reference(x, c, ada_w, ada_b, norm_mix_w, norm_ffn_w, w_in, ssd_conv_w, ssd_conv_b,
              ssd_dt_bias, ssd_a_log, ssd_d, ssd_norm_w, hg_lower_bounds, hg_norm_w,
              lru_conv_w, lru_conv_b, lru_wa, lru_ba, lru_wx, lru_bx, lru_lambda, lru_norm_w,
              w_out, router_w, router_bias, exp_gate, exp_up, exp_down, sh_gate, sh_up,
              sh_down, final_norm_w):
    lb_soft = jax.nn.softmax(hg_lower_bounds.astype(jnp.float32), axis=0)
    lower_bounds = jnp.cumsum(lb_soft, axis=0) - lb_soft[0]
    c_act = jax.nn.silu(c)
    for l in range(DEPTH):
        mod = c_act @ ada_w[l] + ada_b[l]
        sh_m, sc_m, g_m, sh_f, sc_f, g_f = jnp.split(mod, N_ADA, axis=-1)
        h = modulate(rms_norm(x, norm_mix_w[l]), sh_m, sc_m)
        mix = hybrid_mixer(h, w_in[l], ssd_conv_w[l], ssd_conv_b[l], ssd_dt_bias[l], ssd_a_log[l],
                           ssd_d[l], ssd_norm_w[l], lower_bounds[l], hg_norm_w[l], lru_conv_w[l],
                           lru_conv_b[l], lru_wa[l], lru_ba[l], lru_wx[l], lru_bx[l], lru_lambda[l],
                           lru_norm_w[l], w_out[l])
        x = x + g_m[:, None, :] * mix
        h = modulate(rms_norm(x, norm_ffn_w[l]), sh_f, sc_f)
        ffn = moe_ffn(h, router_w[l], router_bias[l], exp_gate[l], exp_up[l], exp_down[l],
                      sh_gate[l], sh_up[l], sh_down[l])
        x = x + g_f[:, None, :] * ffn
    return rms_norm(x, final_norm_w)
```

```python
import functools

import jax
import jax.numpy as jnp
import numpy as np
from jax import lax
from jax.experimental import pallas as pl
from jax.experimental.pallas import tpu as pltpu

F32 = jnp.float32
BF16 = jnp.bfloat16
HI = lax.Precision.HIGHEST

LANES = 128
SUBLANES = 8
VMEM_LIMIT_BYTES = 56 * 1024 * 1024

D_MODEL = 1024
EPS = 1e-6
N_ADA = 6
CONV_WIDTH = 4
SSD_INNER = 1024
SSD_HEADDIM = 64
SSD_HEADS = 16
SSD_GROUPS = 2
SSD_STATE = 128
SSD_CHUNK = 128
SSD_GROUP_W = SSD_INNER // SSD_GROUPS
HG_WIDTH = 512
HG_EXPAND = 128
HG_HEADS = 4
HG_CHUNK = 64
HG_LEVELS = 6
LRU_WIDTH = 512
LRU_BLOCKS = 8
LRU_BLOCK_W = 64
LRU_C = 8.0
N_EXPERTS = 64
TOP_K = 8
N_EXPERT_GROUPS = 8
E_PER_GROUP = 8
TOPK_GROUPS = 4
D_EXPERT = 256
ROUTED_SCALE = 2.5
MASK_SCORE = -1.0e4

COL_Z = 0
COL_XS = 1024
COL_BC = 2048
COL_HQ = 2560
COL_HF = 3072
COL_HV = 3584
COL_HG = 4096
COL_LG = 4608
COL_LX = 5120
COL_DT = 5632
U_WIDTH = 5760

TM_INPROJ = 256
TM_OUTPROJ = 512
TM_ROUTER = 512
TM_MOE = 1024
R_HG = 256
R_LRU = 256


def _cparams(*sem):
    return pltpu.CompilerParams(dimension_semantics=sem, vmem_limit_bytes=VMEM_LIMIT_BYTES)


def _sigmoid(x):
    return jax.nn.sigmoid(x)


def _silu(x):
    return x * jax.nn.sigmoid(x)


def _softplus(x):
    return jnp.maximum(x, 0.0) + jnp.log1p(jnp.exp(-jnp.abs(x)))


def _norm_mod(x, nw, shift, scale):
    ms = jnp.mean(x * x, axis=-1, keepdims=True)
    y = x * lax.rsqrt(ms + EPS) * nw
    return y * (1.0 + scale) + shift


def _ada_kernel(c_ref, w_ref, b_ref, o_ref):
    c = c_ref[...]
    o_ref[...] = jnp.dot(_silu(c), w_ref[...], precision=HI, preferred_element_type=F32) + b_ref[...]


def _adaln(c, ada_w, ada_b):
    depth, d, n6 = ada_w.shape
    bsz = c.shape[0]
    tn = 1536
    return pl.pallas_call(
        _ada_kernel,
        grid=(depth, n6 // tn),
        in_specs=[
            pl.BlockSpec((bsz, d), lambda l, j: (0, 0)),
            pl.BlockSpec((None, d, tn), lambda l, j: (l, 0, j)),
            pl.BlockSpec((None, 1, tn), lambda l, j: (l, 0, j)),
        ],
        out_specs=pl.BlockSpec((None, bsz, tn), lambda l, j: (l, 0, j)),
        out_shape=jax.ShapeDtypeStruct((depth, bsz, n6), F32),
        compiler_params=_cparams("arbitrary", "arbitrary"),
        name="adaln_mod",
    )(c, ada_w, ada_b.reshape(depth, 1, n6))


def _inproj_kernel(x_ref, nw_ref, sh_ref, sc_ref, w_ref, o_ref):
    h = _norm_mod(x_ref[...], nw_ref[...], sh_ref[...], sc_ref[...])
    o_ref[...] = jnp.dot(h.astype(BF16), w_ref[...], preferred_element_type=F32)


def _inproj(x2, nw, mod3, w_cat, seq):
    n, d = x2.shape
    tm = TM_INPROJ
    tpb = seq // tm
    return pl.pallas_call(
        _inproj_kernel,
        grid=(n // tm,),
        in_specs=[
            pl.BlockSpec((tm, d), lambda i: (i, 0)),
            pl.BlockSpec((1, d), lambda i: (0, 0)),
            pl.BlockSpec((None, 1, d), lambda i: ((i // tpb) * N_ADA + 0, 0, 0)),
            pl.BlockSpec((None, 1, d), lambda i: ((i // tpb) * N_ADA + 1, 0, 0)),
            pl.BlockSpec((d, U_WIDTH), lambda i: (0, 0), pipeline_mode=pl.Buffered(1)),
        ],
        out_specs=pl.BlockSpec((tm, U_WIDTH), lambda i: (i, 0)),
        out_shape=jax.ShapeDtypeStruct((n, U_WIDTH), F32),
        compiler_params=_cparams("arbitrary"),
        name="inproj",
    )(x2, nw, mod3, mod3, w_cat)


def _causal_conv(cur_ref, ext, tail, cw_ref, cb_ref):
    rows = cur_ref.shape[0]
    ext[0:SUBLANES, :] = tail[...]
    ext[SUBLANES:SUBLANES + rows, :] = cur_ref[...]
    acc = cb_ref[...]
    for w in range(CONV_WIDTH):
        start = SUBLANES - (CONV_WIDTH - 1) + w
        acc = acc + ext[start:start + rows, :] * cw_ref[w:w + 1, :]
    tail[...] = ext[rows:rows + SUBLANES, :]
    return acc


def _ssd_kernel(z_ref, xs_ref, bc_ref, dt_ref, cwx_ref, cbx_ref, cwb_ref, cbb_ref, dtb_ref, alog_ref,
                dful_ref, nw_ref, e_ref, o_ref, extx, extb, tailx, tailb, hstate):
    c = pl.program_id(1)
    q = SSD_CHUNK

    @pl.when(c == 0)
    def _():
        tailx[...] = jnp.zeros_like(tailx)
        tailb[...] = jnp.zeros_like(tailb)
        hstate[...] = jnp.zeros_like(hstate)

    xs = _silu(_causal_conv(xs_ref, extx, tailx, cwx_ref, cbx_ref))
    bc = _silu(_causal_conv(bc_ref, extb, tailb, cwb_ref, cbb_ref))

    dt = _softplus(dt_ref[...] + dtb_ref[...])
    a = dt * (-jnp.exp(alog_ref[...]))
    ri = lax.broadcasted_iota(jnp.int32, (q, q), 0)
    ci = lax.broadcasted_iota(jnp.int32, (q, q), 1)
    tril = ri >= ci
    acum = jnp.dot(tril.astype(F32), a, precision=HI, preferred_element_type=F32)
    acum_t = acum.T
    expand = e_ref[...]
    dt_full = jnp.dot(dt, expand, precision=HI, preferred_element_type=F32)
    acum_full = jnp.dot(acum, expand, precision=HI, preferred_element_type=F32)
    alast_full = acum_full[q - 1:q, :]

    xdt = xs * dt_full
    xdt_b = xdt.astype(BF16)
    exp_a = jnp.exp(acum_full)
    xd_b = (xdt * jnp.exp(alast_full - acum_full)).astype(BF16)
    state_decay = jnp.exp(alast_full)
    left = lax.broadcasted_iota(jnp.int32, (q, LANES), 1) < SSD_HEADDIM
    zero_b = jnp.zeros((q, LANES), BF16)

    ys = []
    for g in range(SSD_GROUPS):
        b_g = bc[:, g * SSD_STATE:(g + 1) * SSD_STATE]
        c_g = bc[:, (SSD_GROUPS + g) * SSD_STATE:(SSD_GROUPS + g + 1) * SSD_STATE]
        c_b = c_g.astype(BF16)
        cb = lax.dot_general(c_b, b_g.astype(BF16), (((1,), (1,)), ((), ())), preferred_element_type=F32)
        cs = slice(g * SSD_GROUP_W, (g + 1) * SSD_GROUP_W)
        h_g = hstate[:, cs]
        y_off = jnp.dot(c_b, h_g.astype(BF16), preferred_element_type=F32) * exp_a[:, cs]
        pieces = []
        for pr in range(SSD_HEADS // SSD_GROUPS // 2):
            h0 = g * (SSD_HEADS // SSD_GROUPS) + 2 * pr
            ms = []
            for h in (h0, h0 + 1):
                col = acum[:, h:h + 1]
                row = acum_t[h:h + 1, :]
                dec = jnp.where(tril, jnp.exp(jnp.where(tril, col - row, 0.0)), 0.0)
                ms.append((cb * dec).astype(BF16))
            lhs = jnp.concatenate(ms, axis=1)
            xp = xdt_b[:, h0 * SSD_HEADDIM:(h0 + 2) * SSD_HEADDIM]
            rhs = jnp.concatenate([jnp.where(left, xp, zero_b), jnp.where(left, zero_b, xp)], axis=0)
            pieces.append(jnp.dot(lhs, rhs, preferred_element_type=F32))
        ys.append(jnp.concatenate(pieces, axis=1) + y_off)
        b_t = b_g.T.astype(BF16)
        hstate[:, cs] = h_g * state_decay[:, cs] + jnp.dot(b_t, xd_b[:, cs], preferred_element_type=F32)

    y = jnp.concatenate(ys, axis=1) + xs * dful_ref[...]
    y = y * _silu(z_ref[...])
    outs = []
    for g in range(SSD_GROUPS):
        cs = slice(g * SSD_GROUP_W, (g + 1) * SSD_GROUP_W)
        yg = y[:, cs]
        ms = jnp.mean(yg * yg, axis=-1, keepdims=True)
        outs.append(yg * lax.rsqrt(ms + EPS) * nw_ref[:, cs])
    o_ref[...] = jnp.concatenate(outs, axis=1).astype(o_ref.dtype)


def _ssd(u, p, bsz, seq):
    q = SSD_CHUNK
    nc = seq // q
    n = bsz * seq

    def rows(b, c):
        return b * nc + c

    def const(shape):
        return pl.BlockSpec(shape, lambda b, c: (0,) * len(shape))

    return pl.pallas_call(
        _ssd_kernel,
        grid=(bsz, nc),
        in_specs=[
            pl.BlockSpec((q, SSD_INNER), lambda b, c: (rows(b, c), COL_Z // SSD_INNER)),
            pl.BlockSpec((q, SSD_INNER), lambda b, c: (rows(b, c), COL_XS // SSD_INNER)),
            pl.BlockSpec((q, 512), lambda b, c: (rows(b, c), COL_BC // 512)),
            pl.BlockSpec((q, LANES), lambda b, c: (rows(b, c), COL_DT // LANES)),
            const((CONV_WIDTH, SSD_INNER)), const((1, SSD_INNER)),
            const((CONV_WIDTH, 512)), const((1, 512)),
            const((1, LANES)), const((1, LANES)),
            const((1, SSD_INNER)), const((1, SSD_INNER)),
            const((LANES, SSD_INNER)),
        ],
        out_specs=pl.BlockSpec((q, SSD_INNER), lambda b, c: (rows(b, c), 0)),
        out_shape=jax.ShapeDtypeStruct((n, SSD_INNER), BF16),
        scratch_shapes=[
            pltpu.VMEM((q + SUBLANES, SSD_INNER), F32),
            pltpu.VMEM((q + SUBLANES, 512), F32),
            pltpu.VMEM((SUBLANES, SSD_INNER), F32),
            pltpu.VMEM((SUBLANES, 512), F32),
            pltpu.VMEM((SSD_STATE, SSD_INNER), F32),
        ],
        compiler_params=_cparams("arbitrary", "arbitrary"),
        name="ssd_mixer",
    )(u, u, u, u, p["cwx"], p["cbx"], p["cwb"], p["cbb"], p["dtb"], p["alog"], p["dful"], p["ssd_nw"], p["expand"])


def _hgrn2_kernel(layer, q_ref, f_ref, v_ref, g_ref, lb_ref, nw_ref, sel_ref, msk_ref, o_ref, state_t):
    t = pl.program_id(2)
    ch = HG_CHUNK

    @pl.when(t == 0)
    def _():
        state_t[...] = jnp.zeros_like(state_t)

    lrows = [lb_ref[j:j + 1, :] for j in range(lb_ref.shape[0])]
    mx = functools.reduce(jnp.maximum, lrows)
    es = [jnp.exp(r - mx) for r in lrows]
    den = functools.reduce(lambda a_, b_: a_ + b_, es)
    lb = jnp.zeros_like(mx)
    for j in range(1, layer + 1):
        lb = lb + es[j] / den
    one_minus_lb = 1.0 - lb
    nw = nw_ref[...]

    ri = lax.broadcasted_iota(jnp.int32, (ch, ch), 0)
    ci = lax.broadcasted_iota(jnp.int32, (ch, ch), 1)
    tril = (ri >= ci).astype(F32)
    rowi = lax.broadcasted_iota(jnp.int32, (ch, HG_EXPAND), 0)
    sel_all = sel_ref[...]

    def chunk(j, carry):
        r0 = pl.multiple_of(j * ch, ch)
        rs = pl.ds(r0, ch)
        qq = _silu(q_ref[rs, :])
        kk = one_minus_lb * _sigmoid(-f_ref[rs, :])
        logf = jnp.log1p(-kk)
        vv = v_ref[rs, :]
        vb = vv.astype(BF16)
        b = jnp.dot(tril, logf, precision=HI, preferred_element_type=F32)
        m_all = jnp.dot(sel_all, b, precision=HI, preferred_element_type=F32)
        st = state_t[...]
        o = lax.dot_general((qq * jnp.exp(b)).astype(BF16), st.astype(BF16), (((1,), (1,)), ((), ())),
                            preferred_element_type=F32)
        attn = jnp.zeros((ch, ch), F32)
        for lvl in range(HG_LEVELS):
            m = m_all[lvl * ch:(lvl + 1) * ch, :]
            is_tgt = ((rowi >> lvl) & 1) == 1
            qe = jnp.where(is_tgt, qq * jnp.exp(jnp.where(is_tgt, b - m, 0.0)), 0.0)
            ke = jnp.where(is_tgt, 0.0, kk * jnp.exp(jnp.where(is_tgt, 0.0, m - b)))
            prod = lax.dot_general(qe.astype(BF16), ke.astype(BF16), (((1,), (1,)), ((), ())),
                                   preferred_element_type=F32)
            attn = attn + msk_ref[lvl] * prod
        diag = jnp.sum(qq * kk, axis=-1, keepdims=True)
        o = o + jnp.dot(attn.astype(BF16), vb, preferred_element_type=F32) + diag * vv
        b_last = b[ch - 1:ch, :]
        kd = (kk * jnp.exp(b_last - b)).astype(BF16)
        state_t[...] = st * jnp.exp(b_last) + jnp.dot(vv.T.astype(BF16), kd, preferred_element_type=F32)
        ms = jnp.mean(o * o, axis=-1, keepdims=True)
        y = o * lax.rsqrt(ms + EPS) * nw
        o_ref[rs, :] = (y * _silu(g_ref[rs, :])).astype(o_ref.dtype)
        return carry

    lax.fori_loop(0, q_ref.shape[0] // ch, chunk, 0)


def _hgrn2(u, p, layer, bsz, seq):
    r = R_HG
    nt = seq // r
    n = bsz * seq

    def col(base):
        return lambda b, h, t: (b * nt + t, base // HG_EXPAND + h)

    return pl.pallas_call(
        functools.partial(_hgrn2_kernel, layer),
        grid=(bsz, HG_HEADS, nt),
        in_specs=[
            pl.BlockSpec((r, HG_EXPAND), col(COL_HQ)),
            pl.BlockSpec((r, HG_EXPAND), col(COL_HF)),
            pl.BlockSpec((r, HG_EXPAND), col(COL_HV)),
            pl.BlockSpec((r, HG_EXPAND), col(COL_HG)),
            pl.BlockSpec((p["hg_lb"].shape[0], HG_EXPAND), lambda b, h, t: (0, h)),
            pl.BlockSpec((1, HG_EXPAND), lambda b, h, t: (0, h)),
            pl.BlockSpec((HG_LEVELS * HG_CHUNK, HG_CHUNK), lambda b, h, t: (0, 0)),
            pl.BlockSpec((HG_LEVELS, HG_CHUNK, HG_CHUNK), lambda b, h, t: (0, 0, 0)),
        ],
        out_specs=pl.BlockSpec((r, HG_EXPAND), lambda b, h, t: (b * nt + t, h)),
        out_shape=jax.ShapeDtypeStruct((n, HG_WIDTH), BF16),
        scratch_shapes=[pltpu.VMEM((HG_EXPAND, HG_EXPAND), F32)],
        compiler_params=_cparams("arbitrary", "arbitrary", "arbitrary"),
        name="hgrn2_mixer",
    )(u, u, u, u, p["hg_lb"], p["hg_nw"], p["hg_sel"], p["hg_msk"])


def _lru_kernel(g_ref, x_ref, cw_ref, cb_ref, wa_ref, ba_ref, wx_ref, bx_ref, lam_ref, nw_ref, o_ref,
                ext, tail, hcarry):
    t = pl.program_id(1)
    rows = x_ref.shape[0]

    @pl.when(t == 0)
    def _():
        tail[...] = jnp.zeros_like(tail)
        hcarry[...] = jnp.zeros_like(hcarry)

    xb = _causal_conv(x_ref, ext, tail, cw_ref, cb_ref)
    xbb = xb.astype(BF16)
    npair = LRU_WIDTH // LANES
    ra = jnp.concatenate([jnp.dot(xbb[:, j * LANES:(j + 1) * LANES], wa_ref[j], preferred_element_type=F32)
                          for j in range(npair)], axis=1)
    rx = jnp.concatenate([jnp.dot(xbb[:, j * LANES:(j + 1) * LANES], wx_ref[j], preferred_element_type=F32)
                          for j in range(npair)], axis=1)
    r = _sigmoid(ra + ba_ref[...])
    i = _sigmoid(rx + bx_ref[...])
    log_a = -LRU_C * r * _softplus(-lam_ref[...])
    a = jnp.exp(log_a)
    th = jnp.tanh(log_a)
    u = jnp.sqrt(-2.0 * th / (1.0 - th)) * (i * xb)

    rowi = lax.broadcasted_iota(jnp.int32, (rows, LRU_WIDTH), 0)
    acc_a, acc_u = a, u
    d = 1
    while d < rows:
        keep = rowi >= d
        a_sh = jnp.where(keep, pltpu.roll(acc_a, d, 0), 1.0)
        u_sh = jnp.where(keep, pltpu.roll(acc_u, d, 0), 0.0)
        acc_u = acc_a * u_sh + acc_u
        acc_a = acc_a * a_sh
        d *= 2
    h = acc_a * hcarry[0:1, :] + acc_u
    hcarry[0:1, :] = h[rows - 1:rows, :]

    gate = g_ref[...]
    gelu = 0.5 * gate * (1.0 + jnp.tanh(np.sqrt(2.0 / np.pi).astype(np.float32) * (gate + 0.044715 * (gate * gate * gate))))
    y = h * gelu
    ms = jnp.mean(y * y, axis=-1, keepdims=True)
    o_ref[...] = (y * lax.rsqrt(ms + EPS) * nw_ref[...]).astype(o_ref.dtype)


def _lru(u, p, bsz, seq):
    r = R_LRU
    nt = seq // r
    n = bsz * seq
    npair = LRU_WIDTH // LANES

    def const(shape):
        return pl.BlockSpec(shape, lambda b, t: (0,) * len(shape))

    return pl.pallas_call(
        _lru_kernel,
        grid=(bsz, nt),
        in_specs=[
            pl.BlockSpec((r, LRU_WIDTH), lambda b, t: (b * nt + t, COL_LG // LRU_WIDTH)),
            pl.BlockSpec((r, LRU_WIDTH), lambda b, t: (b * nt + t, COL_LX // LRU_WIDTH)),
            const((CONV_WIDTH, LRU_WIDTH)), const((1, LRU_WIDTH)),
            const((npair, LANES, LANES)), const((1, LRU_WIDTH)),
            const((npair, LANES, LANES)), const((1, LRU_WIDTH)),
            const((1, LRU_WIDTH)), const((1, LRU_WIDTH)),
        ],
        out_specs=pl.BlockSpec((r, LRU_WIDTH), lambda b, t: (b * nt + t, 0)),
        out_shape=jax.ShapeDtypeStruct((n, LRU_WIDTH), BF16),
        scratch_shapes=[
            pltpu.VMEM((r + SUBLANES, LRU_WIDTH), F32),
            pltpu.VMEM((SUBLANES, LRU_WIDTH), F32),
            pltpu.VMEM((SUBLANES, LRU_WIDTH), F32),
        ],
        compiler_params=_cparams("arbitrary", "arbitrary"),
        name="rglru_mixer",
    )(u, u, p["lru_cw"], p["lru_cb"], p["lru_wa"], p["lru_ba"], p["lru_wx"], p["lru_bx"], p["lru_lam"], p["lru_nw"])


def _outproj_kernel(ys_ref, yh_ref, yl_ref, x_ref, w_ref, g_ref, nw_ref, sh_ref, sc_ref, xo_ref, h_ref):
    acc = jnp.dot(ys_ref[...], w_ref[0:SSD_INNER, :], preferred_element_type=F32)
    acc = acc + jnp.dot(yh_ref[...], w_ref[SSD_INNER:SSD_INNER + HG_WIDTH, :], preferred_element_type=F32)
    acc = acc + jnp.dot(yl_ref[...], w_ref[SSD_INNER + HG_WIDTH:, :], preferred_element_type=F32)
    xn = x_ref[...] + g_ref[...] * acc
    xo_ref[...] = xn
    h_ref[...] = _norm_mod(xn, nw_ref[...], sh_ref[...], sc_ref[...]).astype(h_ref.dtype)


def _outproj(y_ssd, y_hg, y_lru, x2, w_out, nw, mod3, seq):
    n, d = x2.shape
    tm = TM_OUTPROJ
    tpb = seq // tm

    def modspec(j):
        return pl.BlockSpec((None, 1, d), lambda i: ((i // tpb) * N_ADA + j, 0, 0))

    return pl.pallas_call(
        _outproj_kernel,
        grid=(n // tm,),
        in_specs=[
            pl.BlockSpec((tm, SSD_INNER), lambda i: (i, 0)),
            pl.BlockSpec((tm, HG_WIDTH), lambda i: (i, 0)),
            pl.BlockSpec((tm, LRU_WIDTH), lambda i: (i, 0)),
            pl.BlockSpec((tm, d), lambda i: (i, 0)),
            pl.BlockSpec(w_out.shape, lambda i: (0, 0)),
            modspec(2),
            pl.BlockSpec((1, d), lambda i: (0, 0)),
            modspec(3), modspec(4),
        ],
        out_specs=[pl.BlockSpec((tm, d), lambda i: (i, 0)), pl.BlockSpec((tm, d), lambda i: (i, 0))],
        out_shape=[jax.ShapeDtypeStruct((n, d), F32), jax.ShapeDtypeStruct((n, d), BF16)],
        compiler_params=_cparams("arbitrary"),
        name="outproj",
    )(y_ssd, y_hg, y_lru, x2, w_out, mod3, nw, mod3, mod3)


def _router_kernel(h_ref, rw_ref, rb_ref, wt_ref):
    tm = h_ref.shape[0]
    logit_t = lax.dot_general(rw_ref[...], h_ref[...].astype(F32), (((1,), (1,)), ((), ())),
                              precision=HI, preferred_element_type=F32)
    score = _sigmoid(logit_t)
    sel = score + rb_ref[...]
    neg_inf = jnp.float32(-jnp.inf)
    io_g = lax.broadcasted_iota(jnp.int32, (E_PER_GROUP, tm), 0)
    blocks, gscore = [], []
    for g in range(N_EXPERT_GROUPS):
        blk = sel[g * E_PER_GROUP:(g + 1) * E_PER_GROUP, :]
        m1 = jnp.max(blk, axis=0, keepdims=True)
        i1 = jnp.min(jnp.where(blk == m1, io_g, E_PER_GROUP), axis=0, keepdims=True)
        m2 = jnp.max(jnp.where(io_g == i1, neg_inf, blk), axis=0, keepdims=True)
        blocks.append(blk)
        gscore.append(m1 + m2)
    masked = []
    for g in range(N_EXPERT_GROUPS):
        rank = jnp.zeros((1, tm), jnp.int32)
        for o in range(N_EXPERT_GROUPS):
            if o == g:
                continue
            beats = (gscore[o] > gscore[g]) | ((gscore[o] == gscore[g]) & (o < g))
            rank = rank + beats.astype(jnp.int32)
        masked.append(jnp.where(rank < TOPK_GROUPS, blocks[g], MASK_SCORE))
    val = jnp.concatenate(masked, axis=0)
    io_e = lax.broadcasted_iota(jnp.int32, (N_EXPERTS, tm), 0)
    chosen = jnp.zeros((N_EXPERTS, tm), jnp.bool_)
    for _ in range(TOP_K):
        m = jnp.max(val, axis=0, keepdims=True)
        idx = jnp.min(jnp.where(val == m, io_e, N_EXPERTS), axis=0, keepdims=True)
        pick = io_e == idx
        chosen = chosen | pick
        val = jnp.where(pick, neg_inf, val)
    w = jnp.where(chosen, score, 0.0)
    w = w / jnp.sum(w, axis=0, keepdims=True) * ROUTED_SCALE
    w_pad = jnp.concatenate([w, jnp.zeros((LANES - N_EXPERTS, tm), F32)], axis=0)
    wt_ref[...] = w_pad.T


def _router(h2, rw_t, rb):
    n, d = h2.shape
    tm = TM_ROUTER
    return pl.pallas_call(
        _router_kernel,
        grid=(n // tm,),
        in_specs=[
            pl.BlockSpec((tm, d), lambda i: (i, 0)),
            pl.BlockSpec((N_EXPERTS, d), lambda i: (0, 0)),
            pl.BlockSpec((N_EXPERTS, 1), lambda i: (0, 0)),
        ],
        out_specs=pl.BlockSpec((tm, LANES), lambda i: (i, 0)),
        out_shape=jax.ShapeDtypeStruct((n, LANES), F32),
        compiler_params=_cparams("arbitrary"),
        name="router",
    )(h2, rw_t, rb)


def _moe_kernel(final, h_ref, wt_ref, wg_ref, wu_ref, wd_ref, sg_ref, su_ref, sd_ref, x_ref, gate_ref, fw_ref,
                o_ref, acc):
    e = pl.program_id(1)
    hb = h_ref[...]

    @pl.when(e == 0)
    def _():
        a = jnp.dot(hb, sg_ref[...], preferred_element_type=F32)
        u = jnp.dot(hb, su_ref[...], preferred_element_type=F32)
        acc[...] = jnp.dot((_silu(a) * u).astype(BF16), sd_ref[...], preferred_element_type=F32)

    a = jnp.dot(hb, wg_ref[...], preferred_element_type=F32)
    u = jnp.dot(hb, wu_ref[...], preferred_element_type=F32)
    onehot = (lax.broadcasted_iota(jnp.int32, (LANES, LANES), 0) == e).astype(F32)
    wcol = jnp.dot(wt_ref[...], onehot, precision=HI, preferred_element_type=F32)
    hid = _silu(a) * u * jnp.concatenate([wcol] * (D_EXPERT // LANES), axis=1)
    acc[...] += jnp.dot(hid.astype(BF16), wd_ref[...], preferred_element_type=F32)

    @pl.when(e == pl.num_programs(1) - 1)
    def _():
        xn = x_ref[...] + gate_ref[...] * acc[...]
        if final:
            ms = jnp.mean(xn * xn, axis=-1, keepdims=True)
            xn = xn * lax.rsqrt(ms + EPS) * fw_ref[...]
        o_ref[...] = xn


def _moe(h2, wt, wg, wu, wd, sg, su, sd, x2, mod3, fw, seq, final):
    n, d = x2.shape
    tm = TM_MOE
    tpb = seq // tm
    ne = wg.shape[0]
    return pl.pallas_call(
        functools.partial(_moe_kernel, final),
        grid=(n // tm, ne),
        in_specs=[
            pl.BlockSpec((tm, d), lambda i, e: (i, 0)),
            pl.BlockSpec((tm, LANES), lambda i, e: (i, 0)),
            pl.BlockSpec((None, d, D_EXPERT), lambda i, e: (e, 0, 0)),
            pl.BlockSpec((None, d, D_EXPERT), lambda i, e: (e, 0, 0)),
            pl.BlockSpec((None, D_EXPERT, d), lambda i, e: (e, 0, 0)),
            pl.BlockSpec(sg.shape, lambda i, e: (0, 0)),
            pl.BlockSpec(su.shape, lambda i, e: (0, 0)),
            pl.BlockSpec(sd.shape, lambda i, e: (0, 0)),
            pl.BlockSpec((tm, d), lambda i, e: (i, 0)),
            pl.BlockSpec((None, 1, d), lambda i, e: ((i // tpb) * N_ADA + 5, 0, 0)),
            pl.BlockSpec((1, d), lambda i, e: (0, 0)),
        ],
        out_specs=pl.BlockSpec((tm, d), lambda i, e: (i, 0)),
        out_shape=jax.ShapeDtypeStruct((n, d), F32),
        scratch_shapes=[pltpu.VMEM((tm, d), F32)],
        compiler_params=_cparams("arbitrary", "arbitrary"),
        name="moe_ffn",
    )(h2, wt, wg, wu, wd, sg, su, sd, x2, mod3, fw)


def _blockdiag_pairs(w):
    z = jnp.zeros((LRU_BLOCK_W, LRU_BLOCK_W), w.dtype)
    tiles = []
    for j in range(LRU_BLOCKS // 2):
        top = jnp.concatenate([w[2 * j], z], axis=1)
        bot = jnp.concatenate([z, w[2 * j + 1]], axis=1)
        tiles.append(jnp.concatenate([top, bot], axis=0))
    return jnp.stack(tiles).astype(BF16)


def _hg_constants():
    ch = HG_CHUNK
    sel = np.zeros((HG_LEVELS * ch, ch), np.float32)
    msk = np.zeros((HG_LEVELS, ch, ch), np.float32)
    for lvl in range(HG_LEVELS):
        half = 1 << lvl
        for t in range(ch):
            base = (t // (2 * half)) * (2 * half)
            sel[lvl * ch + t, base + half - 1] = 1.0
            if (t // half) % 2 == 1:
                msk[lvl, t, base:base + half] = 1.0
    return jnp.asarray(sel), jnp.asarray(msk)


def _ssd_expand():
    e = np.zeros((LANES, SSD_INNER), np.float32)
    for h in range(SSD_HEADS):
        e[h, h * SSD_HEADDIM:(h + 1) * SSD_HEADDIM] = 1.0
    return jnp.asarray(e)


def _pad_lanes(v, width):
    return jnp.pad(v, (0, width - v.shape[0])).reshape(1, width)


def _layer_params(l, w_in, ssd_conv_w, ssd_conv_b, ssd_dt_bias, ssd_a_log, ssd_d, ssd_norm_w, hg_lower_bounds,
                  hg_norm_w, lru_conv_w, lru_conv_b, lru_wa, lru_ba, lru_wx, lru_bx, lru_lambda, lru_norm_w):
    wi = w_in[l]
    dt0 = SSD_INNER + SSD_INNER + 2 * SSD_GROUPS * SSD_STATE
    w_cat = jnp.concatenate([wi[:, :dt0], wi[:, dt0 + SSD_HEADS:], wi[:, dt0:dt0 + SSD_HEADS]], axis=1)
    w_cat = jnp.pad(w_cat, ((0, 0), (0, U_WIDTH - w_cat.shape[1]))).astype(BF16)
    sel, msk = _hg_constants()
    return dict(
        w_cat=w_cat,
        cwx=ssd_conv_w[l][:, :SSD_INNER], cbx=ssd_conv_b[l][:SSD_INNER].reshape(1, -1),
        cwb=ssd_conv_w[l][:, SSD_INNER:], cbb=ssd_conv_b[l][SSD_INNER:].reshape(1, -1),
        dtb=_pad_lanes(ssd_dt_bias[l], LANES), alog=_pad_lanes(ssd_a_log[l], LANES),
        dful=jnp.repeat(ssd_d[l], SSD_HEADDIM).reshape(1, -1), ssd_nw=ssd_norm_w[l].reshape(1, -1),
        expand=_ssd_expand(),
        hg_lb=hg_lower_bounds, hg_nw=hg_norm_w[l].reshape(1, -1), hg_sel=sel, hg_msk=msk,
        lru_cw=lru_conv_w[l], lru_cb=lru_conv_b[l].reshape(1, -1),
        lru_wa=_blockdiag_pairs(lru_wa[l]), lru_ba=lru_ba[l].reshape(1, -1),
        lru_wx=_blockdiag_pairs(lru_wx[l]), lru_bx=lru_bx[l].reshape(1, -1),
        lru_lam=lru_lambda[l].reshape(1, -1), lru_nw=lru_norm_w[l].reshape(1, -1),
    )


def kernel(x, c, ada_w, ada_b, norm_mix_w, norm_ffn_w, w_in, ssd_conv_w, ssd_conv_b, ssd_dt_bias, ssd_a_log, ssd_d, ssd_norm_w, hg_lower_bounds, hg_norm_w, lru_conv_w, lru_conv_b, lru_wa, lru_ba, lru_wx, lru_bx, lru_lambda, lru_norm_w, w_out, router_w, router_bias, exp_gate, exp_up, exp_down, sh_gate, sh_up, sh_down, final_norm_w):
    bsz, seq, d = x.shape
    depth = ada_w.shape[0]
    assert d == D_MODEL and seq % TM_MOE == 0 and seq % SSD_CHUNK == 0
    x2 = x.reshape(bsz * seq, d)
    mod = _adaln(c, ada_w, ada_b)
    fw = final_norm_w.reshape(1, d)
    for l in range(depth):
        p = _layer_params(l, w_in, ssd_conv_w, ssd_conv_b, ssd_dt_bias, ssd_a_log, ssd_d, ssd_norm_w,
                          hg_lower_bounds, hg_norm_w, lru_conv_w, lru_conv_b, lru_wa, lru_ba, lru_wx, lru_bx,
                          lru_lambda, lru_norm_w)
        mod3 = mod[l].reshape(bsz * N_ADA, 1, d)
        u = _inproj(x2, norm_mix_w[l].reshape(1, d), mod3, p["w_cat"], seq)
        y_ssd = _ssd(u, p, bsz, seq)
        y_hg = _hgrn2(u, p, l, bsz, seq)
        y_lru = _lru(u, p, bsz, seq)
        x2, h2 = _outproj(y_ssd, y_hg, y_lru, x2, w_out[l].astype(BF16), norm_ffn_w[l].reshape(1, d), mod3, seq)
        wt = _router(h2, router_w[l].T, router_bias[l].reshape(N_EXPERTS, 1))
        x2 = _moe(h2, wt, exp_gate[l].astype(BF16), exp_up[l].astype(BF16), exp_down[l].astype(BF16),
                  sh_gate[l].astype(BF16), sh_up[l].astype(BF16), sh_down[l].astype(BF16), x2, mod3, fw, seq,
                  final=(l == depth - 1))
    return x2.reshape(bsz, seq, d)
```

```python
import functools

import jax
import jax.numpy as jnp
import numpy as np
from jax import lax
from jax.experimental import pallas as pl
from jax.experimental.pallas import tpu as pltpu

F32 = jnp.float32
BF16 = jnp.bfloat16
HI = lax.Precision.HIGHEST

LANES = 128
SUBLANES = 8
VMEM_LIMIT_BYTES = 56 * 1024 * 1024

D_MODEL = 1024
EPS = 1e-6
N_ADA = 6
CONV_WIDTH = 4
SSD_INNER = 1024
SSD_HEADDIM = 64
SSD_HEADS = 16
SSD_GROUPS = 2
SSD_STATE = 128
SSD_CHUNK = 128
SSD_GROUP_W = SSD_INNER // SSD_GROUPS
HG_WIDTH = 512
HG_EXPAND = 128
HG_HEADS = 4
HG_CHUNK = 64
HG_LEVELS = 6
LRU_WIDTH = 512
LRU_BLOCKS = 8
LRU_BLOCK_W = 64
LRU_C = 8.0
N_EXPERTS = 64
TOP_K = 8
N_EXPERT_GROUPS = 8
E_PER_GROUP = 8
TOPK_GROUPS = 4
D_EXPERT = 256
ROUTED_SCALE = 2.5
MASK_SCORE = -1.0e4

COL_Z = 0
COL_XS = 1024
COL_BC = 2048
COL_HQ = 2560
COL_HF = 3072
COL_HV = 3584
COL_HG = 4096
COL_LG = 4608
COL_LX = 5120
COL_DT = 5632
U_WIDTH = 5760

TM_INPROJ = 256
TM_OUTPROJ = 512
TM_ROUTER = 512
TM_DISPATCH = 512
TM_COMBINE = 256
MOE_BLOCK_LOG2 = 8
MOE_BLOCK = 1 << MOE_BLOCK_LOG2
R_HG = 256
R_LRU = 256


def _cparams(*sem):
    return pltpu.CompilerParams(dimension_semantics=sem, vmem_limit_bytes=VMEM_LIMIT_BYTES)


def _sigmoid(x):
    return jax.nn.sigmoid(x)


def _silu(x):
    return x * jax.nn.sigmoid(x)


def _softplus(x):
    return jnp.maximum(x, 0.0) + jnp.log1p(jnp.exp(-jnp.abs(x)))


def _norm_mod(x, nw, shift, scale):
    ms = jnp.mean(x * x, axis=-1, keepdims=True)
    y = x * lax.rsqrt(ms + EPS) * nw
    return y * (1.0 + scale) + shift


_HI16 = np.uint32(0xFFFF0000)


def _pack_bf16_pairs(x):
    half = x.shape[1] // 2
    bits = lax.bitcast_convert_type(x.astype(BF16).astype(F32), jnp.uint32)
    return (bits[:, :half] & _HI16) | (bits[:, half:] >> 16)


def _unpack_bf16_pairs(w):
    hi = lax.bitcast_convert_type(w & _HI16, F32)
    lo = lax.bitcast_convert_type(w << 16, F32)
    return jnp.concatenate([hi, lo], axis=1)


def _ada_kernel(c_ref, w_ref, b_ref, o_ref):
    c = c_ref[...]
    o_ref[...] = jnp.dot(_silu(c), w_ref[...], precision=HI, preferred_element_type=F32) + b_ref[...]


def _adaln(c, ada_w, ada_b):
    depth, d, n6 = ada_w.shape
    bsz = c.shape[0]
    tn = 1536
    return pl.pallas_call(
        _ada_kernel,
        grid=(depth, n6 // tn),
        in_specs=[
            pl.BlockSpec((bsz, d), lambda l, j: (0, 0)),
            pl.BlockSpec((None, d, tn), lambda l, j: (l, 0, j)),
            pl.BlockSpec((None, 1, tn), lambda l, j: (l, 0, j)),
        ],
        out_specs=pl.BlockSpec((None, bsz, tn), lambda l, j: (l, 0, j)),
        out_shape=jax.ShapeDtypeStruct((depth, bsz, n6), F32),
        compiler_params=_cparams("arbitrary", "arbitrary"),
        name="adaln_mod",
    )(c, ada_w, ada_b.reshape(depth, 1, n6))


def _inproj_kernel(x_ref, nw_ref, sh_ref, sc_ref, w_ref, o_ref):
    h = _norm_mod(x_ref[...], nw_ref[...], sh_ref[...], sc_ref[...])
    o_ref[...] = jnp.dot(h.astype(BF16), w_ref[...], preferred_element_type=F32)


def _inproj(x2, nw, mod3, w_cat, seq):
    n, d = x2.shape
    tm = TM_INPROJ
    tpb = seq // tm
    return pl.pallas_call(
        _inproj_kernel,
        grid=(n // tm,),
        in_specs=[
            pl.BlockSpec((tm, d), lambda i: (i, 0)),
            pl.BlockSpec((1, d), lambda i: (0, 0)),
            pl.BlockSpec((None, 1, d), lambda i: ((i // tpb) * N_ADA + 0, 0, 0)),
            pl.BlockSpec((None, 1, d), lambda i: ((i // tpb) * N_ADA + 1, 0, 0)),
            pl.BlockSpec((d, U_WIDTH), lambda i: (0, 0), pipeline_mode=pl.Buffered(1)),
        ],
        out_specs=pl.BlockSpec((tm, U_WIDTH), lambda i: (i, 0)),
        out_shape=jax.ShapeDtypeStruct((n, U_WIDTH), F32),
        compiler_params=_cparams("arbitrary"),
        name="inproj",
    )(x2, nw, mod3, mod3, w_cat)


def _causal_conv(cur_ref, ext, tail, cw_ref, cb_ref):
    rows = cur_ref.shape[0]
    ext[0:SUBLANES, :] = tail[...]
    ext[SUBLANES:SUBLANES + rows, :] = cur_ref[...]
    acc = cb_ref[...]
    for w in range(CONV_WIDTH):
        start = SUBLANES - (CONV_WIDTH - 1) + w
        acc = acc + ext[start:start + rows, :] * cw_ref[w:w + 1, :]
    tail[...] = ext[rows:rows + SUBLANES, :]
    return acc


def _ssd_kernel(z_ref, xs_ref, bc_ref, dt_ref, cwx_ref, cbx_ref, cwb_ref, cbb_ref, dtb_ref, alog_ref,
                dful_ref, nw_ref, e_ref, o_ref, extx, extb, tailx, tailb, hstate):
    c = pl.program_id(1)
    q = SSD_CHUNK

    @pl.when(c == 0)
    def _():
        tailx[...] = jnp.zeros_like(tailx)
        tailb[...] = jnp.zeros_like(tailb)
        hstate[...] = jnp.zeros_like(hstate)

    xs = _silu(_causal_conv(xs_ref, extx, tailx, cwx_ref, cbx_ref))
    bc = _silu(_causal_conv(bc_ref, extb, tailb, cwb_ref, cbb_ref))

    dt = _softplus(dt_ref[...] + dtb_ref[...])
    a = dt * (-jnp.exp(alog_ref[...]))
    ri = lax.broadcasted_iota(jnp.int32, (q, q), 0)
    ci = lax.broadcasted_iota(jnp.int32, (q, q), 1)
    tril = ri >= ci
    acum = jnp.dot(tril.astype(F32), a, precision=HI, preferred_element_type=F32)
    acum_t = acum.T
    expand = e_ref[...]
    dt_full = jnp.dot(dt, expand, precision=HI, preferred_element_type=F32)
    acum_full = jnp.dot(acum, expand, precision=HI, preferred_element_type=F32)
    alast_full = acum_full[q - 1:q, :]

    xdt = xs * dt_full
    xdt_b = xdt.astype(BF16)
    exp_a = jnp.exp(acum_full)
    xd_b = (xdt * jnp.exp(alast_full - acum_full)).astype(BF16)
    state_decay = jnp.exp(alast_full)
    left = lax.broadcasted_iota(jnp.int32, (q, LANES), 1) < SSD_HEADDIM
    zero_b = jnp.zeros((q, LANES), BF16)

    ys = []
    for g in range(SSD_GROUPS):
        b_g = bc[:, g * SSD_STATE:(g + 1) * SSD_STATE]
        c_g = bc[:, (SSD_GROUPS + g) * SSD_STATE:(SSD_GROUPS + g + 1) * SSD_STATE]
        c_b = c_g.astype(BF16)
        cb = lax.dot_general(c_b, b_g.astype(BF16), (((1,), (1,)), ((), ())), preferred_element_type=F32)
        cs = slice(g * SSD_GROUP_W, (g + 1) * SSD_GROUP_W)
        h_g = hstate[:, cs]
        y_off = jnp.dot(c_b, h_g.astype(BF16), preferred_element_type=F32) * exp_a[:, cs]
        pieces = []
        for pr in range(SSD_HEADS // SSD_GROUPS // 2):
            h0 = g * (SSD_HEADS // SSD_GROUPS) + 2 * pr
            ms = []
            for h in (h0, h0 + 1):
                col = acum[:, h:h + 1]
                row = acum_t[h:h + 1, :]
                dec = jnp.where(tril, jnp.exp(jnp.where(tril, col - row, 0.0)), 0.0)
                ms.append((cb * dec).astype(BF16))
            lhs = jnp.concatenate(ms, axis=1)
            xp = xdt_b[:, h0 * SSD_HEADDIM:(h0 + 2) * SSD_HEADDIM]
            rhs = jnp.concatenate([jnp.where(left, xp, zero_b), jnp.where(left, zero_b, xp)], axis=0)
            pieces.append(jnp.dot(lhs, rhs, preferred_element_type=F32))
        ys.append(jnp.concatenate(pieces, axis=1) + y_off)
        b_t = b_g.T.astype(BF16)
        hstate[:, cs] = h_g * state_decay[:, cs] + jnp.dot(b_t, xd_b[:, cs], preferred_element_type=F32)

    y = jnp.concatenate(ys, axis=1) + xs * dful_ref[...]
    y = y * _silu(z_ref[...])
    outs = []
    for g in range(SSD_GROUPS):
        cs = slice(g * SSD_GROUP_W, (g + 1) * SSD_GROUP_W)
        yg = y[:, cs]
        ms = jnp.mean(yg * yg, axis=-1, keepdims=True)
        outs.append(yg * lax.rsqrt(ms + EPS) * nw_ref[:, cs])
    o_ref[...] = jnp.concatenate(outs, axis=1).astype(o_ref.dtype)


def _ssd(u, p, bsz, seq):
    q = SSD_CHUNK
    nc = seq // q
    n = bsz * seq

    def rows(b, c):
        return b * nc + c

    def const(shape):
        return pl.BlockSpec(shape, lambda b, c: (0,) * len(shape))

    return pl.pallas_call(
        _ssd_kernel,
        grid=(bsz, nc),
        in_specs=[
            pl.BlockSpec((q, SSD_INNER), lambda b, c: (rows(b, c), COL_Z // SSD_INNER)),
            pl.BlockSpec((q, SSD_INNER), lambda b, c: (rows(b, c), COL_XS // SSD_INNER)),
            pl.BlockSpec((q, 512), lambda b, c: (rows(b, c), COL_BC // 512)),
            pl.BlockSpec((q, LANES), lambda b, c: (rows(b, c), COL_DT // LANES)),
            const((CONV_WIDTH, SSD_INNER)), const((1, SSD_INNER)),
            const((CONV_WIDTH, 512)), const((1, 512)),
            const((1, LANES)), const((1, LANES)),
            const((1, SSD_INNER)), const((1, SSD_INNER)),
            const((LANES, SSD_INNER)),
        ],
        out_specs=pl.BlockSpec((q, SSD_INNER), lambda b, c: (rows(b, c), 0)),
        out_shape=jax.ShapeDtypeStruct((n, SSD_INNER), BF16),
        scratch_shapes=[
            pltpu.VMEM((q + SUBLANES, SSD_INNER), F32),
            pltpu.VMEM((q + SUBLANES, 512), F32),
            pltpu.VMEM((SUBLANES, SSD_INNER), F32),
            pltpu.VMEM((SUBLANES, 512), F32),
            pltpu.VMEM((SSD_STATE, SSD_INNER), F32),
        ],
        compiler_params=_cparams("arbitrary", "arbitrary"),
        name="ssd_mixer",
    )(u, u, u, u, p["cwx"], p["cbx"], p["cwb"], p["cbb"], p["dtb"], p["alog"], p["dful"], p["ssd_nw"], p["expand"])


def _hgrn2_kernel(layer, q_ref, f_ref, v_ref, g_ref, lb_ref, nw_ref, sel_ref, msk_ref, o_ref, state_t):
    t = pl.program_id(2)
    ch = HG_CHUNK

    @pl.when(t == 0)
    def _():
        state_t[...] = jnp.zeros_like(state_t)

    lrows = [lb_ref[j:j + 1, :] for j in range(lb_ref.shape[0])]
    mx = functools.reduce(jnp.maximum, lrows)
    es = [jnp.exp(r - mx) for r in lrows]
    den = functools.reduce(lambda a_, b_: a_ + b_, es)
    lb = jnp.zeros_like(mx)
    for j in range(1, layer + 1):
        lb = lb + es[j] / den
    one_minus_lb = 1.0 - lb
    nw = nw_ref[...]

    ri = lax.broadcasted_iota(jnp.int32, (ch, ch), 0)
    ci = lax.broadcasted_iota(jnp.int32, (ch, ch), 1)
    tril = (ri >= ci).astype(F32)
    rowi = lax.broadcasted_iota(jnp.int32, (ch, HG_EXPAND), 0)
    sel_all = sel_ref[...]

    def chunk(j, carry):
        r0 = pl.multiple_of(j * ch, ch)
        rs = pl.ds(r0, ch)
        qq = _silu(q_ref[rs, :])
        kk = one_minus_lb * _sigmoid(-f_ref[rs, :])
        logf = jnp.log1p(-kk)
        vv = v_ref[rs, :]
        vb = vv.astype(BF16)
        b = jnp.dot(tril, logf, precision=HI, preferred_element_type=F32)
        m_all = jnp.dot(sel_all, b, precision=HI, preferred_element_type=F32)
        st = state_t[...]
        o = lax.dot_general((qq * jnp.exp(b)).astype(BF16), st.astype(BF16), (((1,), (1,)), ((), ())),
                            preferred_element_type=F32)
        attn = jnp.zeros((ch, ch), F32)
        for lvl in range(HG_LEVELS):
            m = m_all[lvl * ch:(lvl + 1) * ch, :]
            is_tgt = ((rowi >> lvl) & 1) == 1
            qe = jnp.where(is_tgt, qq * jnp.exp(jnp.where(is_tgt, b - m, 0.0)), 0.0)
            ke = jnp.where(is_tgt, 0.0, kk * jnp.exp(jnp.where(is_tgt, 0.0, m - b)))
            prod = lax.dot_general(qe.astype(BF16), ke.astype(BF16), (((1,), (1,)), ((), ())),
                                   preferred_element_type=F32)
            attn = attn + msk_ref[lvl] * prod
        diag = jnp.sum(qq * kk, axis=-1, keepdims=True)
        o = o + jnp.dot(attn.astype(BF16), vb, preferred_element_type=F32) + diag * vv
        b_last = b[ch - 1:ch, :]
        kd = (kk * jnp.exp(b_last - b)).astype(BF16)
        state_t[...] = st * jnp.exp(b_last) + jnp.dot(vv.T.astype(BF16), kd, preferred_element_type=F32)
        ms = jnp.mean(o * o, axis=-1, keepdims=True)
        y = o * lax.rsqrt(ms + EPS) * nw
        o_ref[rs, :] = (y * _silu(g_ref[rs, :])).astype(o_ref.dtype)
        return carry

    lax.fori_loop(0, q_ref.shape[0] // ch, chunk, 0)


def _hgrn2(u, p, layer, bsz, seq):
    r = R_HG
    nt = seq // r
    n = bsz * seq

    def col(base):
        return lambda b, h, t: (b * nt + t, base // HG_EXPAND + h)

    return pl.pallas_call(
        functools.partial(_hgrn2_kernel, layer),
        grid=(bsz, HG_HEADS, nt),
        in_specs=[
            pl.BlockSpec((r, HG_EXPAND), col(COL_HQ)),
            pl.BlockSpec((r, HG_EXPAND), col(COL_HF)),
            pl.BlockSpec((r, HG_EXPAND), col(COL_HV)),
            pl.BlockSpec((r, HG_EXPAND), col(COL_HG)),
            pl.BlockSpec((p["hg_lb"].shape[0], HG_EXPAND), lambda b, h, t: (0, h)),
            pl.BlockSpec((1, HG_EXPAND), lambda b, h, t: (0, h)),
            pl.BlockSpec((HG_LEVELS * HG_CHUNK, HG_CHUNK), lambda b, h, t: (0, 0)),
            pl.BlockSpec((HG_LEVELS, HG_CHUNK, HG_CHUNK), lambda b, h, t: (0, 0, 0)),
        ],
        out_specs=pl.BlockSpec((r, HG_EXPAND), lambda b, h, t: (b * nt + t, h)),
        out_shape=jax.ShapeDtypeStruct((n, HG_WIDTH), BF16),
        scratch_shapes=[pltpu.VMEM((HG_EXPAND, HG_EXPAND), F32)],
        compiler_params=_cparams("arbitrary", "arbitrary", "arbitrary"),
        name="hgrn2_mixer",
    )(u, u, u, u, p["hg_lb"], p["hg_nw"], p["hg_sel"], p["hg_msk"])


def _lru_kernel(g_ref, x_ref, cw_ref, cb_ref, wa_ref, ba_ref, wx_ref, bx_ref, lam_ref, nw_ref, o_ref,
                ext, tail, hcarry):
    t = pl.program_id(1)
    rows = x_ref.shape[0]

    @pl.when(t == 0)
    def _():
        tail[...] = jnp.zeros_like(tail)
        hcarry[...] = jnp.zeros_like(hcarry)

    xb = _causal_conv(x_ref, ext, tail, cw_ref, cb_ref)
    xbb = xb.astype(BF16)
    npair = LRU_WIDTH // LANES
    ra = jnp.concatenate([jnp.dot(xbb[:, j * LANES:(j + 1) * LANES], wa_ref[j], preferred_element_type=F32)
                          for j in range(npair)], axis=1)
    rx = jnp.concatenate([jnp.dot(xbb[:, j * LANES:(j + 1) * LANES], wx_ref[j], preferred_element_type=F32)
                          for j in range(npair)], axis=1)
    r = _sigmoid(ra + ba_ref[...])
    i = _sigmoid(rx + bx_ref[...])
    log_a = -LRU_C * r * _softplus(-lam_ref[...])
    a = jnp.exp(log_a)
    th = jnp.tanh(log_a)
    u = jnp.sqrt(-2.0 * th / (1.0 - th)) * (i * xb)

    rowi = lax.broadcasted_iota(jnp.int32, (rows, LRU_WIDTH), 0)
    acc_a, acc_u = a, u
    d = 1
    while d < rows:
        keep = rowi >= d
        a_sh = jnp.where(keep, pltpu.roll(acc_a, d, 0), 1.0)
        u_sh = jnp.where(keep, pltpu.roll(acc_u, d, 0), 0.0)
        acc_u = acc_a * u_sh + acc_u
        acc_a = acc_a * a_sh
        d *= 2
    h = acc_a * hcarry[0:1, :] + acc_u
    hcarry[0:1, :] = h[rows - 1:rows, :]

    gate = g_ref[...]
    gelu = 0.5 * gate * (1.0 + jnp.tanh(np.sqrt(2.0 / np.pi).astype(np.float32) * (gate + 0.044715 * (gate * gate * gate))))
    y = h * gelu
    ms = jnp.mean(y * y, axis=-1, keepdims=True)
    o_ref[...] = (y * lax.rsqrt(ms + EPS) * nw_ref[...]).astype(o_ref.dtype)


def _lru(u, p, bsz, seq):
    r = R_LRU
    nt = seq // r
    n = bsz * seq
    npair = LRU_WIDTH // LANES

    def const(shape):
        return pl.BlockSpec(shape, lambda b, t: (0,) * len(shape))

    return pl.pallas_call(
        _lru_kernel,
        grid=(bsz, nt),
        in_specs=[
            pl.BlockSpec((r, LRU_WIDTH), lambda b, t: (b * nt + t, COL_LG // LRU_WIDTH)),
            pl.BlockSpec((r, LRU_WIDTH), lambda b, t: (b * nt + t, COL_LX // LRU_WIDTH)),
            const((CONV_WIDTH, LRU_WIDTH)), const((1, LRU_WIDTH)),
            const((npair, LANES, LANES)), const((1, LRU_WIDTH)),
            const((npair, LANES, LANES)), const((1, LRU_WIDTH)),
            const((1, LRU_WIDTH)), const((1, LRU_WIDTH)),
        ],
        out_specs=pl.BlockSpec((r, LRU_WIDTH), lambda b, t: (b * nt + t, 0)),
        out_shape=jax.ShapeDtypeStruct((n, LRU_WIDTH), BF16),
        scratch_shapes=[
            pltpu.VMEM((r + SUBLANES, LRU_WIDTH), F32),
            pltpu.VMEM((SUBLANES, LRU_WIDTH), F32),
            pltpu.VMEM((SUBLANES, LRU_WIDTH), F32),
        ],
        compiler_params=_cparams("arbitrary", "arbitrary"),
        name="rglru_mixer",
    )(u, u, p["lru_cw"], p["lru_cb"], p["lru_wa"], p["lru_ba"], p["lru_wx"], p["lru_bx"], p["lru_lam"], p["lru_nw"])


def _outproj_kernel(ys_ref, yh_ref, yl_ref, x_ref, w_ref, g_ref, nw_ref, sh_ref, sc_ref, xo_ref, h_ref):
    acc = jnp.dot(ys_ref[...], w_ref[0:SSD_INNER, :], preferred_element_type=F32)
    acc = acc + jnp.dot(yh_ref[...], w_ref[SSD_INNER:SSD_INNER + HG_WIDTH, :], preferred_element_type=F32)
    acc = acc + jnp.dot(yl_ref[...], w_ref[SSD_INNER + HG_WIDTH:, :], preferred_element_type=F32)
    xn = x_ref[...] + g_ref[...] * acc
    xo_ref[...] = xn
    h_ref[...] = _pack_bf16_pairs(_norm_mod(xn, nw_ref[...], sh_ref[...], sc_ref[...]))


def _outproj(y_ssd, y_hg, y_lru, x2, w_out, nw, mod3, seq):
    n, d = x2.shape
    tm = TM_OUTPROJ
    tpb = seq // tm

    def modspec(j):
        return pl.BlockSpec((None, 1, d), lambda i: ((i // tpb) * N_ADA + j, 0, 0))

    return pl.pallas_call(
        _outproj_kernel,
        grid=(n // tm,),
        in_specs=[
            pl.BlockSpec((tm, SSD_INNER), lambda i: (i, 0)),
            pl.BlockSpec((tm, HG_WIDTH), lambda i: (i, 0)),
            pl.BlockSpec((tm, LRU_WIDTH), lambda i: (i, 0)),
            pl.BlockSpec((tm, d), lambda i: (i, 0)),
            pl.BlockSpec(w_out.shape, lambda i: (0, 0)),
            modspec(2),
            pl.BlockSpec((1, d), lambda i: (0, 0)),
            modspec(3), modspec(4),
        ],
        out_specs=[pl.BlockSpec((tm, d), lambda i: (i, 0)), pl.BlockSpec((tm, d // 2), lambda i: (i, 0))],
        out_shape=[jax.ShapeDtypeStruct((n, d), F32), jax.ShapeDtypeStruct((n, d // 2), jnp.uint32)],
        compiler_params=_cparams("arbitrary"),
        name="outproj",
    )(y_ssd, y_hg, y_lru, x2, w_out, mod3, nw, mod3, mod3)


def _router_kernel(h_ref, rw_ref, rb_ref, eid_ref, rank_ref, wt_ref, cnt_ref, carry, wscr):
    tm = h_ref.shape[0]

    @pl.when(pl.program_id(0) == 0)
    def _():
        carry[...] = jnp.zeros_like(carry)

    logit_t = lax.dot_general(rw_ref[...], _unpack_bf16_pairs(h_ref[...]), (((1,), (1,)), ((), ())),
                              precision=HI, preferred_element_type=F32)
    score = _sigmoid(logit_t)
    sel = score + rb_ref[...]
    neg_inf = jnp.float32(-jnp.inf)
    io_g = lax.broadcasted_iota(jnp.int32, (E_PER_GROUP, tm), 0)
    blocks, gscore = [], []
    for g in range(N_EXPERT_GROUPS):
        blk = sel[g * E_PER_GROUP:(g + 1) * E_PER_GROUP, :]
        m1 = jnp.max(blk, axis=0, keepdims=True)
        i1 = jnp.min(jnp.where(blk == m1, io_g, E_PER_GROUP), axis=0, keepdims=True)
        m2 = jnp.max(jnp.where(io_g == i1, neg_inf, blk), axis=0, keepdims=True)
        blocks.append(blk)
        gscore.append(m1 + m2)
    masked = []
    for g in range(N_EXPERT_GROUPS):
        rank = jnp.zeros((1, tm), jnp.int32)
        for o in range(N_EXPERT_GROUPS):
            if o == g:
                continue
            beats = (gscore[o] > gscore[g]) | ((gscore[o] == gscore[g]) & (o < g))
            rank = rank + beats.astype(jnp.int32)
        masked.append(jnp.where(rank < TOPK_GROUPS, blocks[g], MASK_SCORE))
    val = jnp.concatenate(masked, axis=0)
    io_e = lax.broadcasted_iota(jnp.int32, (N_EXPERTS, tm), 0)
    chosen = jnp.zeros((N_EXPERTS, tm), jnp.bool_)
    picks = []
    for k in range(TOP_K):
        m = jnp.max(val, axis=0, keepdims=True)
        idx = jnp.min(jnp.where(val == m, io_e, N_EXPERTS), axis=0, keepdims=True)
        pick = io_e == idx
        picks.append(pick)
        eid_ref[k:k + 1, :] = idx
        chosen = chosen | pick
        val = jnp.where(pick, neg_inf, val)
    w = jnp.where(chosen, score, 0.0)
    w = w / jnp.sum(w, axis=0, keepdims=True) * ROUTED_SCALE

    chosen_f = chosen.astype(F32)
    earlier = (lax.broadcasted_iota(jnp.int32, (tm, tm), 0) < lax.broadcasted_iota(jnp.int32, (tm, tm), 1))
    before = jnp.dot(chosen_f.astype(BF16), earlier.astype(BF16), preferred_element_type=F32)
    grank = carry[:, 0:1] + before
    wscr[...] = jnp.zeros_like(wscr)
    for k in range(TOP_K):
        rank_ref[k:k + 1, :] = jnp.sum(jnp.where(picks[k], grank, 0.0), axis=0, keepdims=True).astype(jnp.int32)
        wscr[k:k + 1, :] = jnp.sum(jnp.where(picks[k], w, 0.0), axis=0, keepdims=True)
    wt_ref[...] = wscr[...].T
    carry[...] = carry[...] + jnp.sum(chosen_f, axis=1, keepdims=True)
    cnt_ref[...] = carry[...]


def _router(h2p, rw_t, rb):
    n = h2p.shape[0]
    tm = TM_ROUTER
    return pl.pallas_call(
        _router_kernel,
        grid=(n // tm,),
        in_specs=[
            pl.BlockSpec((tm, h2p.shape[1]), lambda i: (i, 0)),
            pl.BlockSpec(rw_t.shape, lambda i: (0, 0)),
            pl.BlockSpec((N_EXPERTS, 1), lambda i: (0, 0)),
        ],
        out_specs=[
            pl.BlockSpec((TOP_K, tm), lambda i: (0, i)),
            pl.BlockSpec((TOP_K, tm), lambda i: (0, i)),
            pl.BlockSpec((tm, LANES), lambda i: (i, 0)),
            pl.BlockSpec((N_EXPERTS, LANES), lambda i: (0, 0)),
        ],
        out_shape=[
            jax.ShapeDtypeStruct((TOP_K, n), jnp.int32),
            jax.ShapeDtypeStruct((TOP_K, n), jnp.int32),
            jax.ShapeDtypeStruct((n, LANES), F32),
            jax.ShapeDtypeStruct((N_EXPERTS, LANES), F32),
        ],
        scratch_shapes=[pltpu.VMEM((N_EXPERTS, LANES), F32), pltpu.VMEM((LANES, tm), F32)],
        compiler_params=_cparams("arbitrary"),
        name="router",
    )(h2p, rw_t, rb)


def _plan_kernel(cnt_ref, eid_ref, rank_ref, dest_ref, be_ref, nb_ref):
    cnt = cnt_ref[...].astype(jnp.int32)
    padded = ((cnt + (MOE_BLOCK - 1)) >> MOE_BLOCK_LOG2) << MOE_BLOCK_LOG2
    ri = lax.broadcasted_iota(jnp.int32, (N_EXPERTS, N_EXPERTS), 0)
    ci = lax.broadcasted_iota(jnp.int32, (N_EXPERTS, N_EXPERTS), 1)
    pad_end = jnp.dot((ri >= ci).astype(F32), padded.astype(F32), precision=HI,
                      preferred_element_type=F32).astype(jnp.int32)
    pad_start = pad_end - padded
    eid = eid_ref[...]
    dest = rank_ref[...]
    for e in range(N_EXPERTS):
        dest = dest + jnp.where(eid == e, pad_start[e:e + 1, 0:1], 0)
    dest_ref[...] = dest
    nbp = be_ref.shape[1]
    jpos = lax.broadcasted_iota(jnp.int32, (N_EXPERTS, nbp), 1) * MOE_BLOCK
    be = jnp.sum((pad_end[:, 0:1] <= jpos).astype(jnp.int32), axis=0, keepdims=True)
    be_ref[...] = jnp.minimum(be, N_EXPERTS - 1)
    nb_ref[...] = pad_end[N_EXPERTS - 1:N_EXPERTS, :] >> MOE_BLOCK_LOG2


def _plan(cnt, eid, rank, n_blocks):
    n = eid.shape[1]
    nbp = -(-n_blocks // LANES) * LANES
    return pl.pallas_call(
        _plan_kernel,
        grid=(1,),
        in_specs=[
            pl.BlockSpec(cnt.shape, lambda i: (0, 0)),
            pl.BlockSpec(eid.shape, lambda i: (0, 0)),
            pl.BlockSpec(rank.shape, lambda i: (0, 0)),
        ],
        out_specs=[
            pl.BlockSpec((TOP_K, n), lambda i: (0, 0)),
            pl.BlockSpec((1, nbp), lambda i: (0, 0)),
            pl.BlockSpec((1, LANES), lambda i: (0, 0)),
        ],
        out_shape=[
            jax.ShapeDtypeStruct((TOP_K, n), jnp.int32),
            jax.ShapeDtypeStruct((1, nbp), jnp.int32),
            jax.ShapeDtypeStruct((1, LANES), jnp.int32),
        ],
        compiler_params=_cparams("arbitrary"),
        name="moe_plan",
    )(cnt, eid, rank)


def _row_copy(src_ref, src_row, dst_ref, dst_row, sem):
    return pltpu.make_async_copy(src_ref.at[pl.ds(src_row, 1)], dst_ref.at[pl.ds(dst_row, 1)], sem)


def _dispatch_kernel(dest_ref, h_ref, xs_in_ref, xs_ref, sem):
    del xs_in_ref
    tm = h_ref.shape[0]

    def issue(t, carry):
        for k in range(TOP_K):
            _row_copy(h_ref, t, xs_ref, dest_ref[k, t], sem).start(priority=k % 2)
        return carry

    lax.fori_loop(0, tm, issue, 0)
    pltpu.make_async_copy(xs_ref.at[pl.ds(0, TOP_K * tm)], xs_ref.at[pl.ds(0, TOP_K * tm)], sem).wait()


def _dispatch(dest, h2p, n_slots):
    n, w = h2p.shape
    tm = TM_DISPATCH
    xs0 = jnp.zeros((n_slots, w), jnp.uint32)
    return pl.pallas_call(
        _dispatch_kernel,
        grid=(n // tm,),
        in_specs=[
            pl.BlockSpec((TOP_K, tm), lambda i: (0, i), memory_space=pltpu.SMEM),
            pl.BlockSpec((tm, w), lambda i: (i, 0)),
            pl.BlockSpec(memory_space=pl.ANY),
        ],
        out_specs=pl.BlockSpec(memory_space=pl.ANY),
        out_shape=jax.ShapeDtypeStruct((n_slots, w), jnp.uint32),
        scratch_shapes=[pltpu.SemaphoreType.DMA],
        input_output_aliases={2: 0},
        compiler_params=_cparams("arbitrary"),
        name="moe_dispatch",
    )(dest, h2p, xs0)


def _expert_kernel(be_ref, nb_ref, xs_ref, wg_ref, wu_ref, wd_ref, ys_ref):
    del be_ref
    used = pl.program_id(0) < nb_ref[0]

    @pl.when(used)
    def _():
        x = _unpack_bf16_pairs(xs_ref[...]).astype(BF16)
        a = jnp.dot(x, wg_ref[...].astype(BF16), preferred_element_type=F32)
        u = jnp.dot(x, wu_ref[...].astype(BF16), preferred_element_type=F32)
        y = jnp.dot((_silu(a) * u).astype(BF16), wd_ref[...].astype(BF16), preferred_element_type=F32)
        ys_ref[...] = _pack_bf16_pairs(y)

    @pl.when(jnp.logical_not(used))
    def _():
        ys_ref[...] = jnp.zeros_like(ys_ref)


def _experts(be, nb, xs, wg, wu, wd):
    n_slots, w = xs.shape
    d = wg.shape[1]

    def blk(j, be_ref, nb_ref):
        return jnp.minimum(j, nb_ref[0] - 1)

    grid_spec = pltpu.PrefetchScalarGridSpec(
        num_scalar_prefetch=2,
        grid=(n_slots // MOE_BLOCK,),
        in_specs=[
            pl.BlockSpec((MOE_BLOCK, w), lambda j, be_ref, nb_ref: (blk(j, be_ref, nb_ref), 0)),
            pl.BlockSpec((None, d, D_EXPERT), lambda j, be_ref, nb_ref: (be_ref[blk(j, be_ref, nb_ref)], 0, 0)),
            pl.BlockSpec((None, d, D_EXPERT), lambda j, be_ref, nb_ref: (be_ref[blk(j, be_ref, nb_ref)], 0, 0)),
            pl.BlockSpec((None, D_EXPERT, d), lambda j, be_ref, nb_ref: (be_ref[blk(j, be_ref, nb_ref)], 0, 0)),
        ],
        out_specs=pl.BlockSpec((MOE_BLOCK, w), lambda j, be_ref, nb_ref: (j, 0)),
    )
    return pl.pallas_call(
        _expert_kernel,
        grid_spec=grid_spec,
        out_shape=jax.ShapeDtypeStruct((n_slots, w), jnp.uint32),
        compiler_params=_cparams("arbitrary"),
        name="moe_experts",
    )(be, nb, xs, wg, wu, wd)


def _combine_kernel(final, dest_ref, ys_ref, wt_ref, h_ref, sg_ref, su_ref, sd_ref, x_ref, gate_ref, fw_ref,
                    o_ref, buf, sem):
    tm = x_ref.shape[0]

    def issue(t, carry):
        for k in range(TOP_K):
            _row_copy(ys_ref, dest_ref[k, t], buf.at[k], t, sem).start(priority=k % 2)
        return carry

    lax.fori_loop(0, tm, issue, 0)
    hb = _unpack_bf16_pairs(h_ref[...]).astype(BF16)
    a = jnp.dot(hb, sg_ref[...], preferred_element_type=F32)
    u = jnp.dot(hb, su_ref[...], preferred_element_type=F32)
    acc = jnp.dot((_silu(a) * u).astype(BF16), sd_ref[...], preferred_element_type=F32)
    for k in range(TOP_K):
        pltpu.make_async_copy(ys_ref.at[pl.ds(0, tm)], buf.at[k], sem).wait()
    wt = wt_ref[...]
    for k in range(TOP_K):
        acc = acc + wt[:, k:k + 1] * _unpack_bf16_pairs(buf[k])
    xn = x_ref[...] + gate_ref[...] * acc
    if final:
        ms = jnp.mean(xn * xn, axis=-1, keepdims=True)
        xn = xn * lax.rsqrt(ms + EPS) * fw_ref[...]
    o_ref[...] = xn


def _combine(dest, ys, wt, h2p, sg, su, sd, x2, mod3, fw, seq, final):
    n, d = x2.shape
    w = h2p.shape[1]
    tm = TM_COMBINE
    tpb = seq // tm
    return pl.pallas_call(
        functools.partial(_combine_kernel, final),
        grid=(n // tm,),
        in_specs=[
            pl.BlockSpec((TOP_K, tm), lambda i: (0, i), memory_space=pltpu.SMEM),
            pl.BlockSpec(memory_space=pl.ANY),
            pl.BlockSpec((tm, LANES), lambda i: (i, 0)),
            pl.BlockSpec((tm, w), lambda i: (i, 0)),
            pl.BlockSpec(sg.shape, lambda i: (0, 0)),
            pl.BlockSpec(su.shape, lambda i: (0, 0)),
            pl.BlockSpec(sd.shape, lambda i: (0, 0)),
            pl.BlockSpec((tm, d), lambda i: (i, 0)),
            pl.BlockSpec((None, 1, d), lambda i: ((i // tpb) * N_ADA + 5, 0, 0)),
            pl.BlockSpec((1, d), lambda i: (0, 0)),
        ],
        out_specs=pl.BlockSpec((tm, d), lambda i: (i, 0)),
        out_shape=jax.ShapeDtypeStruct((n, d), F32),
        scratch_shapes=[pltpu.VMEM((TOP_K, tm, w), jnp.uint32), pltpu.SemaphoreType.DMA],
        compiler_params=_cparams("arbitrary"),
        name="moe_combine",
    )(dest, ys, wt, h2p, sg, su, sd, x2, mod3, fw)


def _blockdiag_pairs(w):
    z = jnp.zeros((LRU_BLOCK_W, LRU_BLOCK_W), w.dtype)
    tiles = []
    for j in range(LRU_BLOCKS // 2):
        top = jnp.concatenate([w[2 * j], z], axis=1)
        bot = jnp.concatenate([z, w[2 * j + 1]], axis=1)
        tiles.append(jnp.concatenate([top, bot], axis=0))
    return jnp.stack(tiles).astype(BF16)


def _hg_constants():
    ch = HG_CHUNK
    sel = np.zeros((HG_LEVELS * ch, ch), np.float32)
    msk = np.zeros((HG_LEVELS, ch, ch), np.float32)
    for lvl in range(HG_LEVELS):
        half = 1 << lvl
        for t in range(ch):
            base = (t // (2 * half)) * (2 * half)
            sel[lvl * ch + t, base + half - 1] = 1.0
            if (t // half) % 2 == 1:
                msk[lvl, t, base:base + half] = 1.0
    return jnp.asarray(sel), jnp.asarray(msk)


def _ssd_expand():
    e = np.zeros((LANES, SSD_INNER), np.float32)
    for h in range(SSD_HEADS):
        e[h, h * SSD_HEADDIM:(h + 1) * SSD_HEADDIM] = 1.0
    return jnp.asarray(e)


def _pad_lanes(v, width):
    return jnp.pad(v, (0, width - v.shape[0])).reshape(1, width)


def _layer_params(l, w_in, ssd_conv_w, ssd_conv_b, ssd_dt_bias, ssd_a_log, ssd_d, ssd_norm_w, hg_lower_bounds,
                  hg_norm_w, lru_conv_w, lru_conv_b, lru_wa, lru_ba, lru_wx, lru_bx, lru_lambda, lru_norm_w):
    wi = w_in[l]
    dt0 = SSD_INNER + SSD_INNER + 2 * SSD_GROUPS * SSD_STATE
    w_cat = jnp.concatenate([wi[:, :dt0], wi[:, dt0 + SSD_HEADS:], wi[:, dt0:dt0 + SSD_HEADS]], axis=1)
    w_cat = jnp.pad(w_cat, ((0, 0), (0, U_WIDTH - w_cat.shape[1]))).astype(BF16)
    sel, msk = _hg_constants()
    return dict(
        w_cat=w_cat,
        cwx=ssd_conv_w[l][:, :SSD_INNER], cbx=ssd_conv_b[l][:SSD_INNER].reshape(1, -1),
        cwb=ssd_conv_w[l][:, SSD_INNER:], cbb=ssd_conv_b[l][SSD_INNER:].reshape(1, -1),
        dtb=_pad_lanes(ssd_dt_bias[l], LANES), alog=_pad_lanes(ssd_a_log[l], LANES),
        dful=jnp.repeat(ssd_d[l], SSD_HEADDIM).reshape(1, -1), ssd_nw=ssd_norm_w[l].reshape(1, -1),
        expand=_ssd_expand(),
        hg_lb=hg_lower_bounds, hg_nw=hg_norm_w[l].reshape(1, -1), hg_sel=sel, hg_msk=msk,
        lru_cw=lru_conv_w[l], lru_cb=lru_conv_b[l].reshape(1, -1),
        lru_wa=_blockdiag_pairs(lru_wa[l]), lru_ba=lru_ba[l].reshape(1, -1),
        lru_wx=_blockdiag_pairs(lru_wx[l]), lru_bx=lru_bx[l].reshape(1, -1),
        lru_lam=lru_lambda[l].reshape(1, -1), lru_nw=lru_norm_w[l].reshape(1, -1),
    )


def kernel(x, c, ada_w, ada_b, norm_mix_w, norm_ffn_w, w_in, ssd_conv_w, ssd_conv_b, ssd_dt_bias, ssd_a_log, ssd_d, ssd_norm_w, hg_lower_bounds, hg_norm_w, lru_conv_w, lru_conv_b, lru_wa, lru_ba, lru_wx, lru_bx, lru_lambda, lru_norm_w, w_out, router_w, router_bias, exp_gate, exp_up, exp_down, sh_gate, sh_up, sh_down, final_norm_w):
    bsz, seq, d = x.shape
    depth = ada_w.shape[0]
    assert d == D_MODEL and seq % TM_OUTPROJ == 0 and seq % SSD_CHUNK == 0
    n = bsz * seq
    n_slots = n * TOP_K + N_EXPERTS * MOE_BLOCK
    n_blocks = n_slots // MOE_BLOCK
    x2 = x.reshape(n, d)
    mod = _adaln(c, ada_w, ada_b)
    fw = final_norm_w.reshape(1, d)
    for l in range(depth):
        p = _layer_params(l, w_in, ssd_conv_w, ssd_conv_b, ssd_dt_bias, ssd_a_log, ssd_d, ssd_norm_w,
                          hg_lower_bounds, hg_norm_w, lru_conv_w, lru_conv_b, lru_wa, lru_ba, lru_wx, lru_bx,
                          lru_lambda, lru_norm_w)
        mod3 = mod[l].reshape(bsz * N_ADA, 1, d)
        u = _inproj(x2, norm_mix_w[l].reshape(1, d), mod3, p["w_cat"], seq)
        y_ssd = _ssd(u, p, bsz, seq)
        y_hg = _hgrn2(u, p, l, bsz, seq)
        y_lru = _lru(u, p, bsz, seq)
        x2, h2p = _outproj(y_ssd, y_hg, y_lru, x2, w_out[l].astype(BF16), norm_ffn_w[l].reshape(1, d), mod3, seq)
        eid, rank, wt, cnt = _router(h2p, router_w[l].T, router_bias[l].reshape(N_EXPERTS, 1))
        dest, be, nb = _plan(cnt, eid, rank, n_blocks)
        xs = _dispatch(dest, h2p, n_slots)
        ys = _experts(be.reshape(-1), nb[0, :1], xs, exp_gate[l], exp_up[l], exp_down[l])
        x2 = _combine(dest, ys, wt, h2p, sh_gate[l].astype(BF16), sh_up[l].astype(BF16), sh_down[l].astype(BF16),
                      x2, mod3, fw, seq, final=(l == depth - 1))
    return x2.reshape(bsz, seq, d)
```

```python
import functools

import jax
import jax.numpy as jnp
import numpy as np
from jax import lax
from jax.experimental import pallas as pl
from jax.experimental.pallas import tpu as pltpu

F32 = jnp.float32
BF16 = jnp.bfloat16
HI = lax.Precision.HIGHEST

LANES = 128
SUBLANES = 8
VMEM_LIMIT_BYTES = 56 * 1024 * 1024

D_MODEL = 1024
EPS = 1e-6
N_ADA = 6
CONV_WIDTH = 4
SSD_INNER = 1024
SSD_HEADDIM = 64
SSD_HEADS = 16
SSD_GROUPS = 2
SSD_STATE = 128
SSD_CHUNK = 128
SSD_GROUP_W = SSD_INNER // SSD_GROUPS
HG_WIDTH = 512
HG_EXPAND = 128
HG_HEADS = 4
HG_CHUNK = 64
HG_LEVELS = 6
LRU_WIDTH = 512
LRU_BLOCKS = 8
LRU_BLOCK_W = 64
LRU_C = 8.0
N_EXPERTS = 64
TOP_K = 8
N_EXPERT_GROUPS = 8
E_PER_GROUP = 8
TOPK_GROUPS = 4
D_EXPERT = 256
ROUTED_SCALE = 2.5
MASK_SCORE = -1.0e4

COL_Z = 0
COL_XS = 1024
COL_BC = 2048
COL_HQ = 2560
COL_HF = 3072
COL_HV = 3584
COL_HG = 4096
COL_LG = 4608
COL_LX = 5120
COL_DT = 5632
U_WIDTH = 5760

TM_INPROJ = 256
TM_OUTPROJ = 512
TM_ROUTER = 512
TM_DISPATCH = 512
TM_COMBINE = 256
MOE_BLOCK_LOG2 = 9
MOE_BLOCK = 1 << MOE_BLOCK_LOG2
R_HG = 256
R_LRU = 256


def _cparams(*sem):
    return pltpu.CompilerParams(dimension_semantics=sem, vmem_limit_bytes=VMEM_LIMIT_BYTES)


def _sigmoid(x):
    return jax.nn.sigmoid(x)


def _silu(x):
    return x * jax.nn.sigmoid(x)


def _softplus(x):
    return jnp.maximum(x, 0.0) + jnp.log1p(jnp.exp(-jnp.abs(x)))


def _norm_mod(x, nw, shift, scale):
    ms = jnp.mean(x * x, axis=-1, keepdims=True)
    y = x * lax.rsqrt(ms + EPS) * nw
    return y * (1.0 + scale) + shift


_HI16 = np.uint32(0xFFFF0000)


def _pack_bf16_pairs(x):
    half = x.shape[1] // 2
    bits = lax.bitcast_convert_type(x.astype(BF16).astype(F32), jnp.uint32)
    return (bits[:, :half] & _HI16) | (bits[:, half:] >> 16)


def _unpack_bf16_pairs(w):
    hi = lax.bitcast_convert_type(w & _HI16, F32)
    lo = lax.bitcast_convert_type(w << 16, F32)
    return jnp.concatenate([hi, lo], axis=1)


def _ada_kernel(c_ref, w_ref, b_ref, o_ref):
    c = c_ref[...]
    o_ref[...] = jnp.dot(_silu(c), w_ref[...], precision=HI, preferred_element_type=F32) + b_ref[...]


def _adaln(c, ada_w, ada_b):
    depth, d, n6 = ada_w.shape
    bsz = c.shape[0]
    tn = 1536
    return pl.pallas_call(
        _ada_kernel,
        grid=(depth, n6 // tn),
        in_specs=[
            pl.BlockSpec((bsz, d), lambda l, j: (0, 0)),
            pl.BlockSpec((None, d, tn), lambda l, j: (l, 0, j)),
            pl.BlockSpec((None, 1, tn), lambda l, j: (l, 0, j)),
        ],
        out_specs=pl.BlockSpec((None, bsz, tn), lambda l, j: (l, 0, j)),
        out_shape=jax.ShapeDtypeStruct((depth, bsz, n6), F32),
        compiler_params=_cparams("arbitrary", "arbitrary"),
        name="adaln_mod",
    )(c, ada_w, ada_b.reshape(depth, 1, n6))


def _inproj_kernel(x_ref, nw_ref, sh_ref, sc_ref, w_ref, o_ref):
    h = _norm_mod(x_ref[...], nw_ref[...], sh_ref[...], sc_ref[...])
    o_ref[...] = jnp.dot(h.astype(BF16), w_ref[...], preferred_element_type=F32)


def _inproj(x2, nw, mod3, w_cat, seq):
    n, d = x2.shape
    tm = TM_INPROJ
    tpb = seq // tm
    return pl.pallas_call(
        _inproj_kernel,
        grid=(n // tm,),
        in_specs=[
            pl.BlockSpec((tm, d), lambda i: (i, 0)),
            pl.BlockSpec((1, d), lambda i: (0, 0)),
            pl.BlockSpec((None, 1, d), lambda i: ((i // tpb) * N_ADA + 0, 0, 0)),
            pl.BlockSpec((None, 1, d), lambda i: ((i // tpb) * N_ADA + 1, 0, 0)),
            pl.BlockSpec((d, U_WIDTH), lambda i: (0, 0), pipeline_mode=pl.Buffered(1)),
        ],
        out_specs=pl.BlockSpec((tm, U_WIDTH), lambda i: (i, 0)),
        out_shape=jax.ShapeDtypeStruct((n, U_WIDTH), F32),
        compiler_params=_cparams("arbitrary"),
        name="inproj",
    )(x2, nw, mod3, mod3, w_cat)


def _causal_conv(cur_ref, ext, tail, cw_ref, cb_ref):
    rows = cur_ref.shape[0]
    ext[0:SUBLANES, :] = tail[...]
    ext[SUBLANES:SUBLANES + rows, :] = cur_ref[...]
    acc = cb_ref[...]
    for w in range(CONV_WIDTH):
        start = SUBLANES - (CONV_WIDTH - 1) + w
        acc = acc + ext[start:start + rows, :] * cw_ref[w:w + 1, :]
    tail[...] = ext[rows:rows + SUBLANES, :]
    return acc


def _ssd_kernel(z_ref, xs_ref, bc_ref, dt_ref, cwx_ref, cbx_ref, cwb_ref, cbb_ref, dtb_ref, alog_ref,
                dful_ref, nw_ref, e_ref, o_ref, extx, extb, tailx, tailb, hstate):
    c = pl.program_id(1)
    q = SSD_CHUNK

    @pl.when(c == 0)
    def _():
        tailx[...] = jnp.zeros_like(tailx)
        tailb[...] = jnp.zeros_like(tailb)
        hstate[...] = jnp.zeros_like(hstate)

    xs = _silu(_causal_conv(xs_ref, extx, tailx, cwx_ref, cbx_ref))
    bc = _silu(_causal_conv(bc_ref, extb, tailb, cwb_ref, cbb_ref))

    dt = _softplus(dt_ref[...] + dtb_ref[...])
    a = dt * (-jnp.exp(alog_ref[...]))
    ri = lax.broadcasted_iota(jnp.int32, (q, q), 0)
    ci = lax.broadcasted_iota(jnp.int32, (q, q), 1)
    tril = ri >= ci
    acum = jnp.dot(tril.astype(F32), a, precision=HI, preferred_element_type=F32)
    acum_t = acum.T
    expand = e_ref[...]
    dt_full = jnp.dot(dt, expand, precision=HI, preferred_element_type=F32)
    acum_full = jnp.dot(acum, expand, precision=HI, preferred_element_type=F32)
    alast_full = acum_full[q - 1:q, :]

    xdt = xs * dt_full
    xdt_b = xdt.astype(BF16)
    exp_a = jnp.exp(acum_full)
    xd_b = (xdt * jnp.exp(alast_full - acum_full)).astype(BF16)
    state_decay = jnp.exp(alast_full)
    left = lax.broadcasted_iota(jnp.int32, (q, LANES), 1) < SSD_HEADDIM
    zero_b = jnp.zeros((q, LANES), BF16)

    ys = []
    for g in range(SSD_GROUPS):
        b_g = bc[:, g * SSD_STATE:(g + 1) * SSD_STATE]
        c_g = bc[:, (SSD_GROUPS + g) * SSD_STATE:(SSD_GROUPS + g + 1) * SSD_STATE]
        c_b = c_g.astype(BF16)
        cb = lax.dot_general(c_b, b_g.astype(BF16), (((1,), (1,)), ((), ())), preferred_element_type=F32)
        cs = slice(g * SSD_GROUP_W, (g + 1) * SSD_GROUP_W)
        h_g = hstate[:, cs]
        y_off = jnp.dot(c_b, h_g.astype(BF16), preferred_element_type=F32) * exp_a[:, cs]
        pieces = []
        for pr in range(SSD_HEADS // SSD_GROUPS // 2):
            h0 = g * (SSD_HEADS // SSD_GROUPS) + 2 * pr
            ms = []
            for h in (h0, h0 + 1):
                col = acum[:, h:h + 1]
                row = acum_t[h:h + 1, :]
                dec = jnp.where(tril, jnp.exp(jnp.where(tril, col - row, 0.0)), 0.0)
                ms.append((cb * dec).astype(BF16))
            lhs = jnp.concatenate(ms, axis=1)
            xp = xdt_b[:, h0 * SSD_HEADDIM:(h0 + 2) * SSD_HEADDIM]
            rhs = jnp.concatenate([jnp.where(left, xp, zero_b), jnp.where(left, zero_b, xp)], axis=0)
            pieces.append(jnp.dot(lhs, rhs, preferred_element_type=F32))
        ys.append(jnp.concatenate(pieces, axis=1) + y_off)
        b_t = b_g.T.astype(BF16)
        hstate[:, cs] = h_g * state_decay[:, cs] + jnp.dot(b_t, xd_b[:, cs], preferred_element_type=F32)

    y = jnp.concatenate(ys, axis=1) + xs * dful_ref[...]
    y = y * _silu(z_ref[...])
    outs = []
    for g in range(SSD_GROUPS):
        cs = slice(g * SSD_GROUP_W, (g + 1) * SSD_GROUP_W)
        yg = y[:, cs]
        ms = jnp.mean(yg * yg, axis=-1, keepdims=True)
        outs.append(yg * lax.rsqrt(ms + EPS) * nw_ref[:, cs])
    o_ref[...] = jnp.concatenate(outs, axis=1).astype(o_ref.dtype)


def _ssd(u, p, bsz, seq):
    q = SSD_CHUNK
    nc = seq // q
    n = bsz * seq

    def rows(b, c):
        return b * nc + c

    def const(shape):
        return pl.BlockSpec(shape, lambda b, c: (0,) * len(shape))

    return pl.pallas_call(
        _ssd_kernel,
        grid=(bsz, nc),
        in_specs=[
            pl.BlockSpec((q, SSD_INNER), lambda b, c: (rows(b, c), COL_Z // SSD_INNER)),
            pl.BlockSpec((q, SSD_INNER), lambda b, c: (rows(b, c), COL_XS // SSD_INNER)),
            pl.BlockSpec((q, 512), lambda b, c: (rows(b, c), COL_BC // 512)),
            pl.BlockSpec((q, LANES), lambda b, c: (rows(b, c), COL_DT // LANES)),
            const((CONV_WIDTH, SSD_INNER)), const((1, SSD_INNER)),
            const((CONV_WIDTH, 512)), const((1, 512)),
            const((1, LANES)), const((1, LANES)),
            const((1, SSD_INNER)), const((1, SSD_INNER)),
            const((LANES, SSD_INNER)),
        ],
        out_specs=pl.BlockSpec((q, SSD_INNER), lambda b, c: (rows(b, c), 0)),
        out_shape=jax.ShapeDtypeStruct((n, SSD_INNER), BF16),
        scratch_shapes=[
            pltpu.VMEM((q + SUBLANES, SSD_INNER), F32),
            pltpu.VMEM((q + SUBLANES, 512), F32),
            pltpu.VMEM((SUBLANES, SSD_INNER), F32),
            pltpu.VMEM((SUBLANES, 512), F32),
            pltpu.VMEM((SSD_STATE, SSD_INNER), F32),
        ],
        compiler_params=_cparams("arbitrary", "arbitrary"),
        name="ssd_mixer",
    )(u, u, u, u, p["cwx"], p["cbx"], p["cwb"], p["cbb"], p["dtb"], p["alog"], p["dful"], p["ssd_nw"], p["expand"])


def _boundary_rows(b, lvl):
    half = 1 << lvl
    parts = []
    for v in range(b.shape[0] // SUBLANES):
        r0 = v * SUBLANES
        if 2 * half >= SUBLANES:
            src = (r0 // (2 * half)) * (2 * half) + half - 1
            parts.append(jnp.broadcast_to(b[src:src + 1, :], (SUBLANES, b.shape[1])))
        else:
            sub = lax.broadcasted_iota(jnp.int32, (SUBLANES, b.shape[1]), 0)
            piece = None
            for g in range(SUBLANES // (2 * half)):
                src = r0 + g * 2 * half + half - 1
                cand = jnp.broadcast_to(b[src:src + 1, :], (SUBLANES, b.shape[1]))
                piece = cand if piece is None else jnp.where(sub >= g * 2 * half, cand, piece)
            parts.append(piece)
    return jnp.concatenate(parts, axis=0)


def _hgrn2_kernel(layer, q_ref, f_ref, v_ref, g_ref, lb_ref, nw_ref, msk_ref, o_ref, state_t):
    t = pl.program_id(1)
    ch = HG_CHUNK

    @pl.when(t == 0)
    def _():
        state_t[...] = jnp.zeros_like(state_t)

    lrows = [lb_ref[j:j + 1, :] for j in range(lb_ref.shape[0])]
    mx = functools.reduce(jnp.maximum, lrows)
    es = [jnp.exp(r - mx) for r in lrows]
    den = functools.reduce(lambda a_, b_: a_ + b_, es)
    lb = jnp.zeros_like(mx)
    for j in range(1, layer + 1):
        lb = lb + es[j] / den
    one_minus_lb = 1.0 - lb
    nw = nw_ref[...]

    ri = lax.broadcasted_iota(jnp.int32, (ch, ch), 0)
    ci = lax.broadcasted_iota(jnp.int32, (ch, ch), 1)
    tril = (ri >= ci).astype(F32)
    rowi = lax.broadcasted_iota(jnp.int32, (ch, HG_EXPAND), 0)
    tgt = [((rowi >> lvl) & 1) == 1 for lvl in range(HG_LEVELS)]

    def head_chunk(h, rs):
        cs = slice(h * HG_EXPAND, (h + 1) * HG_EXPAND)
        qq = _silu(q_ref[rs, cs])
        kk = one_minus_lb[:, cs] * _sigmoid(-f_ref[rs, cs])
        logf = jnp.log1p(-kk)
        vv = v_ref[rs, cs]
        vb = vv.astype(BF16)
        b = jnp.dot(tril, logf, precision=HI, preferred_element_type=F32)
        st = state_t[h]
        o = lax.dot_general((qq * jnp.exp(b)).astype(BF16), st.astype(BF16), (((1,), (1,)), ((), ())),
                            preferred_element_type=F32)
        attn = jnp.zeros((ch, ch), F32)
        for lvl in range(HG_LEVELS):
            if lvl == 0:
                qe = jnp.where(tgt[0], qq * (1.0 - kk), 0.0)
                ke = jnp.where(tgt[0], 0.0, kk)
            else:
                m = _boundary_rows(b, lvl)
                qe = jnp.where(tgt[lvl], qq * jnp.exp(jnp.where(tgt[lvl], b - m, 0.0)), 0.0)
                ke = jnp.where(tgt[lvl], 0.0, kk * jnp.exp(jnp.where(tgt[lvl], 0.0, m - b)))
            prod = lax.dot_general(qe.astype(BF16), ke.astype(BF16), (((1,), (1,)), ((), ())),
                                   preferred_element_type=F32)
            attn = attn + msk_ref[lvl] * prod
        diag = jnp.sum(qq * kk, axis=-1, keepdims=True)
        o = o + jnp.dot(attn.astype(BF16), vb, preferred_element_type=F32) + diag * vv
        b_last = b[ch - 1:ch, :]
        kd = (kk * jnp.exp(b_last - b)).astype(BF16)
        state_t[h] = st * jnp.exp(b_last) + jnp.dot(vv.T.astype(BF16), kd, preferred_element_type=F32)
        ms = jnp.mean(o * o, axis=-1, keepdims=True)
        y = o * lax.rsqrt(ms + EPS) * nw[:, cs]
        o_ref[rs, cs] = (y * _silu(g_ref[rs, cs])).astype(o_ref.dtype)

    def chunk(j, carry):
        rs = pl.ds(pl.multiple_of(j * ch, ch), ch)
        for h in range(HG_HEADS):
            head_chunk(h, rs)
        return carry

    lax.fori_loop(0, q_ref.shape[0] // ch, chunk, 0)


def _hgrn2(u, p, layer, bsz, seq):
    r = R_HG
    nt = seq // r
    n = bsz * seq

    def col(base):
        return lambda b, t: (b * nt + t, base // HG_WIDTH)

    return pl.pallas_call(
        functools.partial(_hgrn2_kernel, layer),
        grid=(bsz, nt),
        in_specs=[
            pl.BlockSpec((r, HG_WIDTH), col(COL_HQ)),
            pl.BlockSpec((r, HG_WIDTH), col(COL_HF)),
            pl.BlockSpec((r, HG_WIDTH), col(COL_HV)),
            pl.BlockSpec((r, HG_WIDTH), col(COL_HG)),
            pl.BlockSpec(p["hg_lb"].shape, lambda b, t: (0, 0)),
            pl.BlockSpec((1, HG_WIDTH), lambda b, t: (0, 0)),
            pl.BlockSpec((HG_LEVELS, HG_CHUNK, HG_CHUNK), lambda b, t: (0, 0, 0)),
        ],
        out_specs=pl.BlockSpec((r, HG_WIDTH), lambda b, t: (b * nt + t, 0)),
        out_shape=jax.ShapeDtypeStruct((n, HG_WIDTH), BF16),
        scratch_shapes=[pltpu.VMEM((HG_HEADS, HG_EXPAND, HG_EXPAND), F32)],
        compiler_params=_cparams("arbitrary", "arbitrary"),
        name="hgrn2_mixer",
    )(u, u, u, u, p["hg_lb"], p["hg_nw"], p["hg_msk"])


def _lru_kernel(g_ref, x_ref, cw_ref, cb_ref, wa_ref, ba_ref, wx_ref, bx_ref, lam_ref, nw_ref, o_ref,
                ext, tail, hcarry):
    t = pl.program_id(1)
    rows = x_ref.shape[0]

    @pl.when(t == 0)
    def _():
        tail[...] = jnp.zeros_like(tail)
        hcarry[...] = jnp.zeros_like(hcarry)

    xb = _causal_conv(x_ref, ext, tail, cw_ref, cb_ref)
    xbb = xb.astype(BF16)
    npair = LRU_WIDTH // LANES
    ra = jnp.concatenate([jnp.dot(xbb[:, j * LANES:(j + 1) * LANES], wa_ref[j], preferred_element_type=F32)
                          for j in range(npair)], axis=1)
    rx = jnp.concatenate([jnp.dot(xbb[:, j * LANES:(j + 1) * LANES], wx_ref[j], preferred_element_type=F32)
                          for j in range(npair)], axis=1)
    r = _sigmoid(ra + ba_ref[...])
    i = _sigmoid(rx + bx_ref[...])
    log_a = -LRU_C * r * _softplus(-lam_ref[...])
    a = jnp.exp(log_a)
    th = jnp.tanh(log_a)
    u = jnp.sqrt(-2.0 * th / (1.0 - th)) * (i * xb)

    rowi = lax.broadcasted_iota(jnp.int32, (rows, LRU_WIDTH), 0)
    acc_a, acc_u = a, u
    d = 1
    while d < rows:
        keep = rowi >= d
        a_sh = jnp.where(keep, pltpu.roll(acc_a, d, 0), 1.0)
        u_sh = jnp.where(keep, pltpu.roll(acc_u, d, 0), 0.0)
        acc_u = acc_a * u_sh + acc_u
        acc_a = acc_a * a_sh
        d *= 2
    h = acc_a * hcarry[0:1, :] + acc_u
    hcarry[0:1, :] = h[rows - 1:rows, :]

    gate = g_ref[...]
    gelu = 0.5 * gate * (1.0 + jnp.tanh(np.sqrt(2.0 / np.pi).astype(np.float32) * (gate + 0.044715 * (gate * gate * gate))))
    y = h * gelu
    ms = jnp.mean(y * y, axis=-1, keepdims=True)
    o_ref[...] = (y * lax.rsqrt(ms + EPS) * nw_ref[...]).astype(o_ref.dtype)


def _lru(u, p, bsz, seq):
    r = R_LRU
    nt = seq // r
    n = bsz * seq
    npair = LRU_WIDTH // LANES

    def const(shape):
        return pl.BlockSpec(shape, lambda b, t: (0,) * len(shape))

    return pl.pallas_call(
        _lru_kernel,
        grid=(bsz, nt),
        in_specs=[
            pl.BlockSpec((r, LRU_WIDTH), lambda b, t: (b * nt + t, COL_LG // LRU_WIDTH)),
            pl.BlockSpec((r, LRU_WIDTH), lambda b, t: (b * nt + t, COL_LX // LRU_WIDTH)),
            const((CONV_WIDTH, LRU_WIDTH)), const((1, LRU_WIDTH)),
            const((npair, LANES, LANES)), const((1, LRU_WIDTH)),
            const((npair, LANES, LANES)), const((1, LRU_WIDTH)),
            const((1, LRU_WIDTH)), const((1, LRU_WIDTH)),
        ],
        out_specs=pl.BlockSpec((r, LRU_WIDTH), lambda b, t: (b * nt + t, 0)),
        out_shape=jax.ShapeDtypeStruct((n, LRU_WIDTH), BF16),
        scratch_shapes=[
            pltpu.VMEM((r + SUBLANES, LRU_WIDTH), F32),
            pltpu.VMEM((SUBLANES, LRU_WIDTH), F32),
            pltpu.VMEM((SUBLANES, LRU_WIDTH), F32),
        ],
        compiler_params=_cparams("arbitrary", "arbitrary"),
        name="rglru_mixer",
    )(u, u, p["lru_cw"], p["lru_cb"], p["lru_wa"], p["lru_ba"], p["lru_wx"], p["lru_bx"], p["lru_lam"], p["lru_nw"])


def _outproj_kernel(ys_ref, yh_ref, yl_ref, x_ref, w_ref, g_ref, nw_ref, sh_ref, sc_ref, xo_ref, h_ref):
    acc = jnp.dot(ys_ref[...], w_ref[0:SSD_INNER, :], preferred_element_type=F32)
    acc = acc + jnp.dot(yh_ref[...], w_ref[SSD_INNER:SSD_INNER + HG_WIDTH, :], preferred_element_type=F32)
    acc = acc + jnp.dot(yl_ref[...], w_ref[SSD_INNER + HG_WIDTH:, :], preferred_element_type=F32)
    xn = x_ref[...] + g_ref[...] * acc
    xo_ref[...] = xn
    h_ref[...] = _pack_bf16_pairs(_norm_mod(xn, nw_ref[...], sh_ref[...], sc_ref[...]))


def _outproj(y_ssd, y_hg, y_lru, x2, w_out, nw, mod3, seq):
    n, d = x2.shape
    tm = TM_OUTPROJ
    tpb = seq // tm

    def modspec(j):
        return pl.BlockSpec((None, 1, d), lambda i: ((i // tpb) * N_ADA + j, 0, 0))

    return pl.pallas_call(
        _outproj_kernel,
        grid=(n // tm,),
        in_specs=[
            pl.BlockSpec((tm, SSD_INNER), lambda i: (i, 0)),
            pl.BlockSpec((tm, HG_WIDTH), lambda i: (i, 0)),
            pl.BlockSpec((tm, LRU_WIDTH), lambda i: (i, 0)),
            pl.BlockSpec((tm, d), lambda i: (i, 0)),
            pl.BlockSpec(w_out.shape, lambda i: (0, 0)),
            modspec(2),
            pl.BlockSpec((1, d), lambda i: (0, 0)),
            modspec(3), modspec(4),
        ],
        out_specs=[pl.BlockSpec((tm, d), lambda i: (i, 0)), pl.BlockSpec((tm, d // 2), lambda i: (i, 0))],
        out_shape=[jax.ShapeDtypeStruct((n, d), F32), jax.ShapeDtypeStruct((n, d // 2), jnp.uint32)],
        compiler_params=_cparams("arbitrary"),
        name="outproj",
    )(y_ssd, y_hg, y_lru, x2, w_out, mod3, nw, mod3, mod3)


def _router_kernel(h_ref, rw_ref, rb_ref, eid_ref, rank_ref, wt_ref, cnt_ref, carry, wscr):
    tm = h_ref.shape[0]

    @pl.when(pl.program_id(0) == 0)
    def _():
        carry[...] = jnp.zeros_like(carry)

    logit_t = lax.dot_general(rw_ref[...], _unpack_bf16_pairs(h_ref[...]), (((1,), (1,)), ((), ())),
                              precision=HI, preferred_element_type=F32)
    score = _sigmoid(logit_t)
    sel = score + rb_ref[...]
    neg_inf = jnp.float32(-jnp.inf)
    io_g = lax.broadcasted_iota(jnp.int32, (E_PER_GROUP, tm), 0)
    blocks, gscore = [], []
    for g in range(N_EXPERT_GROUPS):
        blk = sel[g * E_PER_GROUP:(g + 1) * E_PER_GROUP, :]
        m1 = jnp.max(blk, axis=0, keepdims=True)
        i1 = jnp.min(jnp.where(blk == m1, io_g, E_PER_GROUP), axis=0, keepdims=True)
        m2 = jnp.max(jnp.where(io_g == i1, neg_inf, blk), axis=0, keepdims=True)
        blocks.append(blk)
        gscore.append(m1 + m2)
    masked = []
    for g in range(N_EXPERT_GROUPS):
        rank = jnp.zeros((1, tm), jnp.int32)
        for o in range(N_EXPERT_GROUPS):
            if o == g:
                continue
            beats = (gscore[o] > gscore[g]) | ((gscore[o] == gscore[g]) & (o < g))
            rank = rank + beats.astype(jnp.int32)
        masked.append(jnp.where(rank < TOPK_GROUPS, blocks[g], MASK_SCORE))
    val = jnp.concatenate(masked, axis=0)
    io_e = lax.broadcasted_iota(jnp.int32, (N_EXPERTS, tm), 0)
    chosen = jnp.zeros((N_EXPERTS, tm), jnp.bool_)
    picks = []
    for k in range(TOP_K):
        m = jnp.max(val, axis=0, keepdims=True)
        idx = jnp.min(jnp.where(val == m, io_e, N_EXPERTS), axis=0, keepdims=True)
        pick = io_e == idx
        picks.append(pick)
        eid_ref[k:k + 1, :] = idx
        chosen = chosen | pick
        val = jnp.where(pick, neg_inf, val)
    w = jnp.where(chosen, score, 0.0)
    w = w / jnp.sum(w, axis=0, keepdims=True) * ROUTED_SCALE

    chosen_f = chosen.astype(F32)
    earlier = (lax.broadcasted_iota(jnp.int32, (tm, tm), 0) < lax.broadcasted_iota(jnp.int32, (tm, tm), 1))
    before = jnp.dot(chosen_f.astype(BF16), earlier.astype(BF16), preferred_element_type=F32)
    grank = carry[:, 0:1] + before
    wscr[...] = jnp.zeros_like(wscr)
    for k in range(TOP_K):
        rank_ref[k:k + 1, :] = jnp.sum(jnp.where(picks[k], grank, 0.0), axis=0, keepdims=True).astype(jnp.int32)
        wscr[k:k + 1, :] = jnp.sum(jnp.where(picks[k], w, 0.0), axis=0, keepdims=True)
    wt_ref[...] = wscr[...].T
    carry[...] = carry[...] + jnp.sum(chosen_f, axis=1, keepdims=True)
    cnt_ref[...] = carry[...]


def _router(h2p, rw_t, rb):
    n = h2p.shape[0]
    tm = TM_ROUTER
    return pl.pallas_call(
        _router_kernel,
        grid=(n // tm,),
        in_specs=[
            pl.BlockSpec((tm, h2p.shape[1]), lambda i: (i, 0)),
            pl.BlockSpec(rw_t.shape, lambda i: (0, 0)),
            pl.BlockSpec((N_EXPERTS, 1), lambda i: (0, 0)),
        ],
        out_specs=[
            pl.BlockSpec((TOP_K, tm), lambda i: (0, i)),
            pl.BlockSpec((TOP_K, tm), lambda i: (0, i)),
            pl.BlockSpec((tm, LANES), lambda i: (i, 0)),
            pl.BlockSpec((N_EXPERTS, LANES), lambda i: (0, 0)),
        ],
        out_shape=[
            jax.ShapeDtypeStruct((TOP_K, n), jnp.int32),
            jax.ShapeDtypeStruct((TOP_K, n), jnp.int32),
            jax.ShapeDtypeStruct((n, LANES), F32),
            jax.ShapeDtypeStruct((N_EXPERTS, LANES), F32),
        ],
        scratch_shapes=[pltpu.VMEM((N_EXPERTS, LANES), F32), pltpu.VMEM((LANES, tm), F32)],
        compiler_params=_cparams("arbitrary"),
        name="router",
    )(h2p, rw_t, rb)


def _plan_kernel(cnt_ref, eid_ref, rank_ref, dest_ref, be_ref, nb_ref):
    cnt = cnt_ref[...].astype(jnp.int32)
    padded = ((cnt + (MOE_BLOCK - 1)) >> MOE_BLOCK_LOG2) << MOE_BLOCK_LOG2
    ri = lax.broadcasted_iota(jnp.int32, (N_EXPERTS, N_EXPERTS), 0)
    ci = lax.broadcasted_iota(jnp.int32, (N_EXPERTS, N_EXPERTS), 1)
    pad_end = jnp.dot((ri >= ci).astype(F32), padded.astype(F32), precision=HI,
                      preferred_element_type=F32).astype(jnp.int32)
    pad_start = pad_end - padded
    eid = eid_ref[...]
    dest = rank_ref[...]
    for e in range(N_EXPERTS):
        dest = dest + jnp.where(eid == e, pad_start[e:e + 1, 0:1], 0)
    dest_ref[...] = dest
    nbp = be_ref.shape[1]
    jpos = lax.broadcasted_iota(jnp.int32, (N_EXPERTS, nbp), 1) * MOE_BLOCK
    be = jnp.sum((pad_end[:, 0:1] <= jpos).astype(jnp.int32), axis=0, keepdims=True)
    be_ref[...] = jnp.minimum(be, N_EXPERTS - 1)
    nb_ref[...] = pad_end[N_EXPERTS - 1:N_EXPERTS, :] >> MOE_BLOCK_LOG2


def _plan(cnt, eid, rank, n_blocks):
    n = eid.shape[1]
    nbp = -(-n_blocks // LANES) * LANES
    return pl.pallas_call(
        _plan_kernel,
        grid=(1,),
        in_specs=[
            pl.BlockSpec(cnt.shape, lambda i: (0, 0)),
            pl.BlockSpec(eid.shape, lambda i: (0, 0)),
            pl.BlockSpec(rank.shape, lambda i: (0, 0)),
        ],
        out_specs=[
            pl.BlockSpec((TOP_K, n), lambda i: (0, 0)),
            pl.BlockSpec((1, nbp), lambda i: (0, 0)),
            pl.BlockSpec((1, LANES), lambda i: (0, 0)),
        ],
        out_shape=[
            jax.ShapeDtypeStruct((TOP_K, n), jnp.int32),
            jax.ShapeDtypeStruct((1, nbp), jnp.int32),
            jax.ShapeDtypeStruct((1, LANES), jnp.int32),
        ],
        compiler_params=_cparams("arbitrary"),
        name="moe_plan",
    )(cnt, eid, rank)


def _row_copy(src_ref, src_row, dst_ref, dst_row, sem):
    return pltpu.make_async_copy(src_ref.at[pl.ds(src_row, 1)], dst_ref.at[pl.ds(dst_row, 1)], sem)


def _dispatch_kernel(dest_ref, h_ref, xs_in_ref, xs_ref, sem):
    del xs_in_ref
    tm = h_ref.shape[0]

    def issue(t, carry):
        for k in range(TOP_K):
            _row_copy(h_ref, t, xs_ref, dest_ref[k, t], sem).start(priority=k % 2)
        return carry

    lax.fori_loop(0, tm, issue, 0)
    pltpu.make_async_copy(xs_ref.at[pl.ds(0, TOP_K * tm)], xs_ref.at[pl.ds(0, TOP_K * tm)], sem).wait()


def _dispatch(dest, h2p, n_slots):
    n, w = h2p.shape
    tm = TM_DISPATCH
    xs0 = jnp.zeros((n_slots, w), jnp.uint32)
    return pl.pallas_call(
        _dispatch_kernel,
        grid=(n // tm,),
        in_specs=[
            pl.BlockSpec((TOP_K, tm), lambda i: (0, i), memory_space=pltpu.SMEM),
            pl.BlockSpec((tm, w), lambda i: (i, 0)),
            pl.BlockSpec(memory_space=pl.ANY),
        ],
        out_specs=pl.BlockSpec(memory_space=pl.ANY),
        out_shape=jax.ShapeDtypeStruct((n_slots, w), jnp.uint32),
        scratch_shapes=[pltpu.SemaphoreType.DMA],
        input_output_aliases={2: 0},
        compiler_params=_cparams("arbitrary"),
        name="moe_dispatch",
    )(dest, h2p, xs0)


def _expert_kernel(be_ref, nb_ref, xs_ref, wg_ref, wu_ref, wd_ref, ys_ref):
    del be_ref
    used = pl.program_id(0) < nb_ref[0]

    @pl.when(used)
    def _():
        x = _unpack_bf16_pairs(xs_ref[...]).astype(BF16)
        a = jnp.dot(x, wg_ref[...].astype(BF16), preferred_element_type=F32)
        u = jnp.dot(x, wu_ref[...].astype(BF16), preferred_element_type=F32)
        y = jnp.dot((_silu(a) * u).astype(BF16), wd_ref[...].astype(BF16), preferred_element_type=F32)
        ys_ref[...] = _pack_bf16_pairs(y)

    @pl.when(jnp.logical_not(used))
    def _():
        ys_ref[...] = jnp.zeros_like(ys_ref)


def _experts(be, nb, xs, wg, wu, wd, layer):
    n_slots, w = xs.shape
    d = wg.shape[2]

    def blk(j, be_ref, nb_ref):
        return jnp.minimum(j, jnp.maximum(nb_ref[0] - 1, 0))

    def wspec(shape):
        return pl.BlockSpec((None, None) + shape,
                            lambda j, be_ref, nb_ref: (layer, be_ref[blk(j, be_ref, nb_ref)], 0, 0))

    grid_spec = pltpu.PrefetchScalarGridSpec(
        num_scalar_prefetch=2,
        grid=(n_slots // MOE_BLOCK,),
        in_specs=[
            pl.BlockSpec((MOE_BLOCK, w), lambda j, be_ref, nb_ref: (blk(j, be_ref, nb_ref), 0)),
            wspec((d, D_EXPERT)), wspec((d, D_EXPERT)), wspec((D_EXPERT, d)),
        ],
        out_specs=pl.BlockSpec((MOE_BLOCK, w), lambda j, be_ref, nb_ref: (j, 0)),
    )
    return pl.pallas_call(
        _expert_kernel,
        grid_spec=grid_spec,
        out_shape=jax.ShapeDtypeStruct((n_slots, w), jnp.uint32),
        compiler_params=_cparams("arbitrary"),
        name="moe_experts",
    )(be, nb, xs, wg, wu, wd)


def _combine_kernel(final, dest_ref, ys_ref, wt_ref, h_ref, sg_ref, su_ref, sd_ref, x_ref, gate_ref, fw_ref,
                    o_ref, buf, sem):
    tm = x_ref.shape[0]

    def issue(t, carry):
        for k in range(TOP_K):
            _row_copy(ys_ref, dest_ref[k, t], buf.at[k], t, sem).start(priority=k % 2)
        return carry

    lax.fori_loop(0, tm, issue, 0)
    hb = _unpack_bf16_pairs(h_ref[...]).astype(BF16)
    a = jnp.dot(hb, sg_ref[...], preferred_element_type=F32)
    u = jnp.dot(hb, su_ref[...], preferred_element_type=F32)
    acc = jnp.dot((_silu(a) * u).astype(BF16), sd_ref[...], preferred_element_type=F32)
    for k in range(TOP_K):
        pltpu.make_async_copy(ys_ref.at[pl.ds(0, tm)], buf.at[k], sem).wait()
    wt = wt_ref[...]
    for k in range(TOP_K):
        acc = acc + wt[:, k:k + 1] * _unpack_bf16_pairs(buf[k])
    xn = x_ref[...] + gate_ref[...] * acc
    if final:
        ms = jnp.mean(xn * xn, axis=-1, keepdims=True)
        xn = xn * lax.rsqrt(ms + EPS) * fw_ref[...]
    o_ref[...] = xn


def _combine(dest, ys, wt, h2p, sg, su, sd, x2, mod3, fw, seq, final):
    n, d = x2.shape
    w = h2p.shape[1]
    tm = TM_COMBINE
    tpb = seq // tm
    return pl.pallas_call(
        functools.partial(_combine_kernel, final),
        grid=(n // tm,),
        in_specs=[
            pl.BlockSpec((TOP_K, tm), lambda i: (0, i), memory_space=pltpu.SMEM),
            pl.BlockSpec(memory_space=pl.ANY),
            pl.BlockSpec((tm, LANES), lambda i: (i, 0)),
            pl.BlockSpec((tm, w), lambda i: (i, 0)),
            pl.BlockSpec(sg.shape, lambda i: (0, 0)),
            pl.BlockSpec(su.shape, lambda i: (0, 0)),
            pl.BlockSpec(sd.shape, lambda i: (0, 0)),
            pl.BlockSpec((tm, d), lambda i: (i, 0)),
            pl.BlockSpec((None, 1, d), lambda i: ((i // tpb) * N_ADA + 5, 0, 0)),
            pl.BlockSpec((1, d), lambda i: (0, 0)),
        ],
        out_specs=pl.BlockSpec((tm, d), lambda i: (i, 0)),
        out_shape=jax.ShapeDtypeStruct((n, d), F32),
        scratch_shapes=[pltpu.VMEM((TOP_K, tm, w), jnp.uint32), pltpu.SemaphoreType.DMA],
        compiler_params=_cparams("arbitrary"),
        name="moe_combine",
    )(dest, ys, wt, h2p, sg, su, sd, x2, mod3, fw)


def _blockdiag_pairs(w):
    z = jnp.zeros((LRU_BLOCK_W, LRU_BLOCK_W), w.dtype)
    tiles = []
    for j in range(LRU_BLOCKS // 2):
        top = jnp.concatenate([w[2 * j], z], axis=1)
        bot = jnp.concatenate([z, w[2 * j + 1]], axis=1)
        tiles.append(jnp.concatenate([top, bot], axis=0))
    return jnp.stack(tiles).astype(BF16)


def _hg_level_masks():
    ch = HG_CHUNK
    msk = np.zeros((HG_LEVELS, ch, ch), np.float32)
    for lvl in range(HG_LEVELS):
        half = 1 << lvl
        for t in range(ch):
            base = (t // (2 * half)) * (2 * half)
            if (t // half) % 2 == 1:
                msk[lvl, t, base:base + half] = 1.0
    return jnp.asarray(msk)


def _ssd_expand():
    e = np.zeros((LANES, SSD_INNER), np.float32)
    for h in range(SSD_HEADS):
        e[h, h * SSD_HEADDIM:(h + 1) * SSD_HEADDIM] = 1.0
    return jnp.asarray(e)


def _pad_lanes(v, width):
    return jnp.pad(v, (0, width - v.shape[0])).reshape(1, width)


def _layer_params(l, w_in, ssd_conv_w, ssd_conv_b, ssd_dt_bias, ssd_a_log, ssd_d, ssd_norm_w, hg_lower_bounds,
                  hg_norm_w, lru_conv_w, lru_conv_b, lru_wa, lru_ba, lru_wx, lru_bx, lru_lambda, lru_norm_w):
    wi = w_in[l]
    dt0 = SSD_INNER + SSD_INNER + 2 * SSD_GROUPS * SSD_STATE
    w_cat = jnp.concatenate([wi[:, :dt0], wi[:, dt0 + SSD_HEADS:], wi[:, dt0:dt0 + SSD_HEADS]], axis=1)
    w_cat = jnp.pad(w_cat, ((0, 0), (0, U_WIDTH - w_cat.shape[1]))).astype(BF16)
    msk = _hg_level_masks()
    return dict(
        w_cat=w_cat,
        cwx=ssd_conv_w[l][:, :SSD_INNER], cbx=ssd_conv_b[l][:SSD_INNER].reshape(1, -1),
        cwb=ssd_conv_w[l][:, SSD_INNER:], cbb=ssd_conv_b[l][SSD_INNER:].reshape(1, -1),
        dtb=_pad_lanes(ssd_dt_bias[l], LANES), alog=_pad_lanes(ssd_a_log[l], LANES),
        dful=jnp.repeat(ssd_d[l], SSD_HEADDIM).reshape(1, -1), ssd_nw=ssd_norm_w[l].reshape(1, -1),
        expand=_ssd_expand(),
        hg_lb=hg_lower_bounds, hg_nw=hg_norm_w[l].reshape(1, -1), hg_msk=msk,
        lru_cw=lru_conv_w[l], lru_cb=lru_conv_b[l].reshape(1, -1),
        lru_wa=_blockdiag_pairs(lru_wa[l]), lru_ba=lru_ba[l].reshape(1, -1),
        lru_wx=_blockdiag_pairs(lru_wx[l]), lru_bx=lru_bx[l].reshape(1, -1),
        lru_lam=lru_lambda[l].reshape(1, -1), lru_nw=lru_norm_w[l].reshape(1, -1),
    )


def kernel(x, c, ada_w, ada_b, norm_mix_w, norm_ffn_w, w_in, ssd_conv_w, ssd_conv_b, ssd_dt_bias, ssd_a_log, ssd_d, ssd_norm_w, hg_lower_bounds, hg_norm_w, lru_conv_w, lru_conv_b, lru_wa, lru_ba, lru_wx, lru_bx, lru_lambda, lru_norm_w, w_out, router_w, router_bias, exp_gate, exp_up, exp_down, sh_gate, sh_up, sh_down, final_norm_w):
    bsz, seq, d = x.shape
    depth = ada_w.shape[0]
    assert d == D_MODEL and seq % TM_OUTPROJ == 0 and seq % SSD_CHUNK == 0
    n = bsz * seq
    n_slots = n * TOP_K + N_EXPERTS * MOE_BLOCK
    n_blocks = n_slots // MOE_BLOCK
    x2 = x.reshape(n, d)
    mod = _adaln(c, ada_w, ada_b)
    fw = final_norm_w.reshape(1, d)
    for l in range(depth):
        p = _layer_params(l, w_in, ssd_conv_w, ssd_conv_b, ssd_dt_bias, ssd_a_log, ssd_d, ssd_norm_w,
                          hg_lower_bounds, hg_norm_w, lru_conv_w, lru_conv_b, lru_wa, lru_ba, lru_wx, lru_bx,
                          lru_lambda, lru_norm_w)
        mod3 = mod[l].reshape(bsz * N_ADA, 1, d)
        u = _inproj(x2, norm_mix_w[l].reshape(1, d), mod3, p["w_cat"], seq)
        y_ssd = _ssd(u, p, bsz, seq)
        y_hg = _hgrn2(u, p, l, bsz, seq)
        y_lru = _lru(u, p, bsz, seq)
        x2, h2p = _outproj(y_ssd, y_hg, y_lru, x2, w_out[l].astype(BF16), norm_ffn_w[l].reshape(1, d), mod3, seq)
        eid, rank, wt, cnt = _router(h2p, router_w[l].T, router_bias[l].reshape(N_EXPERTS, 1))
        dest, be, nb = _plan(cnt, eid, rank, n_blocks)
        xs = _dispatch(dest, h2p, n_slots)
        ys = _experts(be.reshape(-1), nb[0, :1], xs, exp_gate, exp_up, exp_down, l)
        x2 = _combine(dest, ys, wt, h2p, sh_gate[l].astype(BF16), sh_up[l].astype(BF16), sh_down[l].astype(BF16),
                      x2, mod3, fw, seq, final=(l == depth - 1))
    return x2.reshape(bsz, seq, d)
```

```python
import functools

import jax
import jax.numpy as jnp
import numpy as np
from jax import lax
from jax.experimental import pallas as pl
from jax.experimental.pallas import tpu as pltpu
from jax.experimental.pallas import tpu_sc as plsc

F32 = jnp.float32
BF16 = jnp.bfloat16
HI = lax.Precision.HIGHEST

LANES = 128
SUBLANES = 8
VMEM_LIMIT_BYTES = 56 * 1024 * 1024

D_MODEL = 1024
EPS = 1e-6
N_ADA = 6
CONV_WIDTH = 4
SSD_INNER = 1024
SSD_HEADDIM = 64
SSD_HEADS = 16
SSD_GROUPS = 2
SSD_STATE = 128
SSD_CHUNK = 128
SSD_GROUP_W = SSD_INNER // SSD_GROUPS
HG_WIDTH = 512
HG_EXPAND = 128
HG_HEADS = 4
HG_CHUNK = 64
HG_LEVELS = 6
LRU_WIDTH = 512
LRU_BLOCKS = 8
LRU_BLOCK_W = 64
LRU_C = 8.0
N_EXPERTS = 64
TOP_K = 8
N_EXPERT_GROUPS = 8
E_PER_GROUP = 8
TOPK_GROUPS = 4
D_EXPERT = 256
ROUTED_SCALE = 2.5
MASK_SCORE = -1.0e4

COL_Z = 0
COL_XS = 1024
COL_BC = 2048
COL_HQ = 2560
COL_HF = 3072
COL_HV = 3584
COL_HG = 4096
COL_LG = 4608
COL_LX = 5120
COL_DT = 5632
U_WIDTH = 5760

TM_INPROJ = 256
TM_OUTPROJ = 512
TM_ROUTER = 512
TM_COMBINE = 256
SC_CORES = 2
SC_SUBCORES = 16
SC_WORKERS = SC_CORES * SC_SUBCORES
SC_WIN = LANES
SC_GATHER_BUFS = 4
MOE_BLOCK_LOG2 = 9
MOE_BLOCK = 1 << MOE_BLOCK_LOG2
R_HG = 256
R_LRU = 256


def _cparams(*sem):
    return pltpu.CompilerParams(dimension_semantics=sem, vmem_limit_bytes=VMEM_LIMIT_BYTES)


def _sigmoid(x):
    return jax.nn.sigmoid(x)


def _silu(x):
    return x * jax.nn.sigmoid(x)


def _softplus(x):
    return jnp.maximum(x, 0.0) + jnp.log1p(jnp.exp(-jnp.abs(x)))


def _norm_mod(x, nw, shift, scale):
    ms = jnp.mean(x * x, axis=-1, keepdims=True)
    y = x * lax.rsqrt(ms + EPS) * nw
    return y * (1.0 + scale) + shift


_HI16 = np.uint32(0xFFFF0000)


def _pack_bf16_pairs(x):
    half = x.shape[1] // 2
    bits = lax.bitcast_convert_type(x.astype(BF16).astype(F32), jnp.uint32)
    return (bits[:, :half] & _HI16) | (bits[:, half:] >> 16)


def _unpack_bf16_pairs(w):
    hi = lax.bitcast_convert_type(w & _HI16, F32)
    lo = lax.bitcast_convert_type(w << 16, F32)
    return jnp.concatenate([hi, lo], axis=1)


N_PLANES = D_MODEL // 2 // LANES


def _store_planes(refs, x):
    packed = _pack_bf16_pairs(x)
    for c, ref in enumerate(refs):
        ref[...] = packed[:, c * LANES:(c + 1) * LANES]


def _load_planes(refs):
    return _unpack_bf16_pairs(jnp.concatenate([ref[...] for ref in refs], axis=1))


def _ada_kernel(c_ref, w_ref, b_ref, o_ref):
    c = c_ref[...]
    o_ref[...] = jnp.dot(_silu(c), w_ref[...], precision=HI, preferred_element_type=F32) + b_ref[...]


def _adaln(c, ada_w, ada_b):
    depth, d, n6 = ada_w.shape
    bsz = c.shape[0]
    tn = 1536
    return pl.pallas_call(
        _ada_kernel,
        grid=(depth, n6 // tn),
        in_specs=[
            pl.BlockSpec((bsz, d), lambda l, j: (0, 0)),
            pl.BlockSpec((None, d, tn), lambda l, j: (l, 0, j)),
            pl.BlockSpec((None, 1, tn), lambda l, j: (l, 0, j)),
        ],
        out_specs=pl.BlockSpec((None, bsz, tn), lambda l, j: (l, 0, j)),
        out_shape=jax.ShapeDtypeStruct((depth, bsz, n6), F32),
        compiler_params=_cparams("arbitrary", "arbitrary"),
        name="adaln_mod",
    )(c, ada_w, ada_b.reshape(depth, 1, n6))


def _inproj_kernel(x_ref, nw_ref, sh_ref, sc_ref, w_ref, o_ref):
    h = _norm_mod(x_ref[...], nw_ref[...], sh_ref[...], sc_ref[...])
    o_ref[...] = jnp.dot(h.astype(BF16), w_ref[...], preferred_element_type=F32)


def _inproj(x2, nw, mod3, w_cat, seq):
    n, d = x2.shape
    tm = TM_INPROJ
    tpb = seq // tm
    return pl.pallas_call(
        _inproj_kernel,
        grid=(n // tm,),
        in_specs=[
            pl.BlockSpec((tm, d), lambda i: (i, 0)),
            pl.BlockSpec((1, d), lambda i: (0, 0)),
            pl.BlockSpec((None, 1, d), lambda i: ((i // tpb) * N_ADA + 0, 0, 0)),
            pl.BlockSpec((None, 1, d), lambda i: ((i // tpb) * N_ADA + 1, 0, 0)),
            pl.BlockSpec((d, U_WIDTH), lambda i: (0, 0), pipeline_mode=pl.Buffered(1)),
        ],
        out_specs=pl.BlockSpec((tm, U_WIDTH), lambda i: (i, 0)),
        out_shape=jax.ShapeDtypeStruct((n, U_WIDTH), F32),
        compiler_params=_cparams("arbitrary"),
        name="inproj",
    )(x2, nw, mod3, mod3, w_cat)


def _causal_conv(cur_ref, ext, tail, cw_ref, cb_ref):
    rows = cur_ref.shape[0]
    ext[0:SUBLANES, :] = tail[...]
    ext[SUBLANES:SUBLANES + rows, :] = cur_ref[...]
    acc = cb_ref[...]
    for w in range(CONV_WIDTH):
        start = SUBLANES - (CONV_WIDTH - 1) + w
        acc = acc + ext[start:start + rows, :] * cw_ref[w:w + 1, :]
    tail[...] = ext[rows:rows + SUBLANES, :]
    return acc


def _ssd_kernel(z_ref, xs_ref, bc_ref, dt_ref, cwx_ref, cbx_ref, cwb_ref, cbb_ref, dtb_ref, alog_ref,
                dful_ref, nw_ref, e_ref, o_ref, extx, extb, tailx, tailb, hstate):
    c = pl.program_id(1)
    q = SSD_CHUNK

    @pl.when(c == 0)
    def _():
        tailx[...] = jnp.zeros_like(tailx)
        tailb[...] = jnp.zeros_like(tailb)
        hstate[...] = jnp.zeros_like(hstate)

    xs = _silu(_causal_conv(xs_ref, extx, tailx, cwx_ref, cbx_ref))
    bc = _silu(_causal_conv(bc_ref, extb, tailb, cwb_ref, cbb_ref))

    dt = _softplus(dt_ref[...] + dtb_ref[...])
    a = dt * (-jnp.exp(alog_ref[...]))
    ri = lax.broadcasted_iota(jnp.int32, (q, q), 0)
    ci = lax.broadcasted_iota(jnp.int32, (q, q), 1)
    tril = ri >= ci
    acum = jnp.dot(tril.astype(F32), a, precision=HI, preferred_element_type=F32)
    acum_t = acum.T
    expand = e_ref[...]
    dt_full = jnp.dot(dt, expand, precision=HI, preferred_element_type=F32)
    acum_full = jnp.dot(acum, expand, precision=HI, preferred_element_type=F32)
    alast_full = acum_full[q - 1:q, :]

    xdt = xs * dt_full
    xdt_b = xdt.astype(BF16)
    exp_a = jnp.exp(acum_full)
    xd_b = (xdt * jnp.exp(alast_full - acum_full)).astype(BF16)
    state_decay = jnp.exp(alast_full)
    left = lax.broadcasted_iota(jnp.int32, (q, LANES), 1) < SSD_HEADDIM
    zero_b = jnp.zeros((q, LANES), BF16)

    ys = []
    for g in range(SSD_GROUPS):
        b_g = bc[:, g * SSD_STATE:(g + 1) * SSD_STATE]
        c_g = bc[:, (SSD_GROUPS + g) * SSD_STATE:(SSD_GROUPS + g + 1) * SSD_STATE]
        c_b = c_g.astype(BF16)
        cb = lax.dot_general(c_b, b_g.astype(BF16), (((1,), (1,)), ((), ())), preferred_element_type=F32)
        cs = slice(g * SSD_GROUP_W, (g + 1) * SSD_GROUP_W)
        h_g = hstate[:, cs]
        y_off = jnp.dot(c_b, h_g.astype(BF16), preferred_element_type=F32) * exp_a[:, cs]
        pieces = []
        for pr in range(SSD_HEADS // SSD_GROUPS // 2):
            h0 = g * (SSD_HEADS // SSD_GROUPS) + 2 * pr
            ms = []
            for h in (h0, h0 + 1):
                col = acum[:, h:h + 1]
                row = acum_t[h:h + 1, :]
                dec = jnp.where(tril, jnp.exp(jnp.where(tril, col - row, 0.0)), 0.0)
                ms.append((cb * dec).astype(BF16))
            lhs = jnp.concatenate(ms, axis=1)
            xp = xdt_b[:, h0 * SSD_HEADDIM:(h0 + 2) * SSD_HEADDIM]
            rhs = jnp.concatenate([jnp.where(left, xp, zero_b), jnp.where(left, zero_b, xp)], axis=0)
            pieces.append(jnp.dot(lhs, rhs, preferred_element_type=F32))
        ys.append(jnp.concatenate(pieces, axis=1) + y_off)
        b_t = b_g.T.astype(BF16)
        hstate[:, cs] = h_g * state_decay[:, cs] + jnp.dot(b_t, xd_b[:, cs], preferred_element_type=F32)

    y = jnp.concatenate(ys, axis=1) + xs * dful_ref[...]
    y = y * _silu(z_ref[...])
    outs = []
    for g in range(SSD_GROUPS):
        cs = slice(g * SSD_GROUP_W, (g + 1) * SSD_GROUP_W)
        yg = y[:, cs]
        ms = jnp.mean(yg * yg, axis=-1, keepdims=True)
        outs.append(yg * lax.rsqrt(ms + EPS) * nw_ref[:, cs])
    o_ref[...] = jnp.concatenate(outs, axis=1).astype(o_ref.dtype)


def _ssd(u, p, bsz, seq):
    q = SSD_CHUNK
    nc = seq // q
    n = bsz * seq

    def rows(b, c):
        return b * nc + c

    def const(shape):
        return pl.BlockSpec(shape, lambda b, c: (0,) * len(shape))

    return pl.pallas_call(
        _ssd_kernel,
        grid=(bsz, nc),
        in_specs=[
            pl.BlockSpec((q, SSD_INNER), lambda b, c: (rows(b, c), COL_Z // SSD_INNER)),
            pl.BlockSpec((q, SSD_INNER), lambda b, c: (rows(b, c), COL_XS // SSD_INNER)),
            pl.BlockSpec((q, 512), lambda b, c: (rows(b, c), COL_BC // 512)),
            pl.BlockSpec((q, LANES), lambda b, c: (rows(b, c), COL_DT // LANES)),
            const((CONV_WIDTH, SSD_INNER)), const((1, SSD_INNER)),
            const((CONV_WIDTH, 512)), const((1, 512)),
            const((1, LANES)), const((1, LANES)),
            const((1, SSD_INNER)), const((1, SSD_INNER)),
            const((LANES, SSD_INNER)),
        ],
        out_specs=pl.BlockSpec((q, SSD_INNER), lambda b, c: (rows(b, c), 0)),
        out_shape=jax.ShapeDtypeStruct((n, SSD_INNER), BF16),
        scratch_shapes=[
            pltpu.VMEM((q + SUBLANES, SSD_INNER), F32),
            pltpu.VMEM((q + SUBLANES, 512), F32),
            pltpu.VMEM((SUBLANES, SSD_INNER), F32),
            pltpu.VMEM((SUBLANES, 512), F32),
            pltpu.VMEM((SSD_STATE, SSD_INNER), F32),
        ],
        compiler_params=_cparams("arbitrary", "arbitrary"),
        name="ssd_mixer",
    )(u, u, u, u, p["cwx"], p["cbx"], p["cwb"], p["cbb"], p["dtb"], p["alog"], p["dful"], p["ssd_nw"], p["expand"])


def _boundary_rows(b, lvl):
    half = 1 << lvl
    parts = []
    for v in range(b.shape[0] // SUBLANES):
        r0 = v * SUBLANES
        if 2 * half >= SUBLANES:
            src = (r0 // (2 * half)) * (2 * half) + half - 1
            parts.append(jnp.broadcast_to(b[src:src + 1, :], (SUBLANES, b.shape[1])))
        else:
            sub = lax.broadcasted_iota(jnp.int32, (SUBLANES, b.shape[1]), 0)
            piece = None
            for g in range(SUBLANES // (2 * half)):
                src = r0 + g * 2 * half + half - 1
                cand = jnp.broadcast_to(b[src:src + 1, :], (SUBLANES, b.shape[1]))
                piece = cand if piece is None else jnp.where(sub >= g * 2 * half, cand, piece)
            parts.append(piece)
    return jnp.concatenate(parts, axis=0)


def _hgrn2_kernel(layer, q_ref, f_ref, v_ref, g_ref, lb_ref, nw_ref, msk_ref, o_ref, state_t):
    t = pl.program_id(1)
    ch = HG_CHUNK

    @pl.when(t == 0)
    def _():
        state_t[...] = jnp.zeros_like(state_t)

    lrows = [lb_ref[j:j + 1, :] for j in range(lb_ref.shape[0])]
    mx = functools.reduce(jnp.maximum, lrows)
    es = [jnp.exp(r - mx) for r in lrows]
    den = functools.reduce(lambda a_, b_: a_ + b_, es)
    lb = jnp.zeros_like(mx)
    for j in range(1, layer + 1):
        lb = lb + es[j] / den
    one_minus_lb = 1.0 - lb
    nw = nw_ref[...]

    ri = lax.broadcasted_iota(jnp.int32, (ch, ch), 0)
    ci = lax.broadcasted_iota(jnp.int32, (ch, ch), 1)
    tril = (ri >= ci).astype(F32)
    rowi = lax.broadcasted_iota(jnp.int32, (ch, HG_EXPAND), 0)
    tgt = [((rowi >> lvl) & 1) == 1 for lvl in range(HG_LEVELS)]

    def head_chunk(h, rs):
        cs = slice(h * HG_EXPAND, (h + 1) * HG_EXPAND)
        qq = _silu(q_ref[rs, cs])
        kk = one_minus_lb[:, cs] * _sigmoid(-f_ref[rs, cs])
        logf = jnp.log1p(-kk)
        vv = v_ref[rs, cs]
        vb = vv.astype(BF16)
        b = jnp.dot(tril, logf, precision=HI, preferred_element_type=F32)
        st = state_t[h]
        o = lax.dot_general((qq * jnp.exp(b)).astype(BF16), st.astype(BF16), (((1,), (1,)), ((), ())),
                            preferred_element_type=F32)
        attn = jnp.zeros((ch, ch), F32)
        for lvl in range(HG_LEVELS):
            if lvl == 0:
                qe = jnp.where(tgt[0], qq * (1.0 - kk), 0.0)
                ke = jnp.where(tgt[0], 0.0, kk)
            else:
                m = _boundary_rows(b, lvl)
                qe = jnp.where(tgt[lvl], qq * jnp.exp(jnp.where(tgt[lvl], b - m, 0.0)), 0.0)
                ke = jnp.where(tgt[lvl], 0.0, kk * jnp.exp(jnp.where(tgt[lvl], 0.0, m - b)))
            prod = lax.dot_general(qe.astype(BF16), ke.astype(BF16), (((1,), (1,)), ((), ())),
                                   preferred_element_type=F32)
            attn = attn + msk_ref[lvl] * prod
        diag = jnp.sum(qq * kk, axis=-1, keepdims=True)
        o = o + jnp.dot(attn.astype(BF16), vb, preferred_element_type=F32) + diag * vv
        b_last = b[ch - 1:ch, :]
        kd = (kk * jnp.exp(b_last - b)).astype(BF16)
        state_t[h] = st * jnp.exp(b_last) + jnp.dot(vv.T.astype(BF16), kd, preferred_element_type=F32)
        ms = jnp.mean(o * o, axis=-1, keepdims=True)
        y = o * lax.rsqrt(ms + EPS) * nw[:, cs]
        o_ref[rs, cs] = (y * _silu(g_ref[rs, cs])).astype(o_ref.dtype)

    def chunk(j, carry):
        rs = pl.ds(pl.multiple_of(j * ch, ch), ch)
        for h in range(HG_HEADS):
            head_chunk(h, rs)
        return carry

    lax.fori_loop(0, q_ref.shape[0] // ch, chunk, 0)


def _hgrn2(u, p, layer, bsz, seq):
    r = R_HG
    nt = seq // r
    n = bsz * seq

    def col(base):
        return lambda b, t: (b * nt + t, base // HG_WIDTH)

    return pl.pallas_call(
        functools.partial(_hgrn2_kernel, layer),
        grid=(bsz, nt),
        in_specs=[
            pl.BlockSpec((r, HG_WIDTH), col(COL_HQ)),
            pl.BlockSpec((r, HG_WIDTH), col(COL_HF)),
            pl.BlockSpec((r, HG_WIDTH), col(COL_HV)),
            pl.BlockSpec((r, HG_WIDTH), col(COL_HG)),
            pl.BlockSpec(p["hg_lb"].shape, lambda b, t: (0, 0)),
            pl.BlockSpec((1, HG_WIDTH), lambda b, t: (0, 0)),
            pl.BlockSpec((HG_LEVELS, HG_CHUNK, HG_CHUNK), lambda b, t: (0, 0, 0)),
        ],
        out_specs=pl.BlockSpec((r, HG_WIDTH), lambda b, t: (b * nt + t, 0)),
        out_shape=jax.ShapeDtypeStruct((n, HG_WIDTH), BF16),
        scratch_shapes=[pltpu.VMEM((HG_HEADS, HG_EXPAND, HG_EXPAND), F32)],
        compiler_params=_cparams("arbitrary", "arbitrary"),
        name="hgrn2_mixer",
    )(u, u, u, u, p["hg_lb"], p["hg_nw"], p["hg_msk"])


def _lru_kernel(g_ref, x_ref, cw_ref, cb_ref, wa_ref, ba_ref, wx_ref, bx_ref, lam_ref, nw_ref, o_ref,
                ext, tail, hcarry):
    t = pl.program_id(1)
    rows = x_ref.shape[0]

    @pl.when(t == 0)
    def _():
        tail[...] = jnp.zeros_like(tail)
        hcarry[...] = jnp.zeros_like(hcarry)

    xb = _causal_conv(x_ref, ext, tail, cw_ref, cb_ref)
    xbb = xb.astype(BF16)
    npair = LRU_WIDTH // LANES
    ra = jnp.concatenate([jnp.dot(xbb[:, j * LANES:(j + 1) * LANES], wa_ref[j], preferred_element_type=F32)
                          for j in range(npair)], axis=1)
    rx = jnp.concatenate([jnp.dot(xbb[:, j * LANES:(j + 1) * LANES], wx_ref[j], preferred_element_type=F32)
                          for j in range(npair)], axis=1)
    r = _sigmoid(ra + ba_ref[...])
    i = _sigmoid(rx + bx_ref[...])
    log_a = -LRU_C * r * _softplus(-lam_ref[...])
    a = jnp.exp(log_a)
    th = jnp.tanh(log_a)
    u = jnp.sqrt(-2.0 * th / (1.0 - th)) * (i * xb)

    rowi = lax.broadcasted_iota(jnp.int32, (rows, LRU_WIDTH), 0)
    acc_a, acc_u = a, u
    d = 1
    while d < rows:
        keep = rowi >= d
        a_sh = jnp.where(keep, pltpu.roll(acc_a, d, 0), 1.0)
        u_sh = jnp.where(keep, pltpu.roll(acc_u, d, 0), 0.0)
        acc_u = acc_a * u_sh + acc_u
        acc_a = acc_a * a_sh
        d *= 2
    h = acc_a * hcarry[0:1, :] + acc_u
    hcarry[0:1, :] = h[rows - 1:rows, :]

    gate = g_ref[...]
    gelu = 0.5 * gate * (1.0 + jnp.tanh(np.sqrt(2.0 / np.pi).astype(np.float32) * (gate + 0.044715 * (gate * gate * gate))))
    y = h * gelu
    ms = jnp.mean(y * y, axis=-1, keepdims=True)
    o_ref[...] = (y * lax.rsqrt(ms + EPS) * nw_ref[...]).astype(o_ref.dtype)


def _lru(u, p, bsz, seq):
    r = R_LRU
    nt = seq // r
    n = bsz * seq
    npair = LRU_WIDTH // LANES

    def const(shape):
        return pl.BlockSpec(shape, lambda b, t: (0,) * len(shape))

    return pl.pallas_call(
        _lru_kernel,
        grid=(bsz, nt),
        in_specs=[
            pl.BlockSpec((r, LRU_WIDTH), lambda b, t: (b * nt + t, COL_LG // LRU_WIDTH)),
            pl.BlockSpec((r, LRU_WIDTH), lambda b, t: (b * nt + t, COL_LX // LRU_WIDTH)),
            const((CONV_WIDTH, LRU_WIDTH)), const((1, LRU_WIDTH)),
            const((npair, LANES, LANES)), const((1, LRU_WIDTH)),
            const((npair, LANES, LANES)), const((1, LRU_WIDTH)),
            const((1, LRU_WIDTH)), const((1, LRU_WIDTH)),
        ],
        out_specs=pl.BlockSpec((r, LRU_WIDTH), lambda b, t: (b * nt + t, 0)),
        out_shape=jax.ShapeDtypeStruct((n, LRU_WIDTH), BF16),
        scratch_shapes=[
            pltpu.VMEM((r + SUBLANES, LRU_WIDTH), F32),
            pltpu.VMEM((SUBLANES, LRU_WIDTH), F32),
            pltpu.VMEM((SUBLANES, LRU_WIDTH), F32),
        ],
        compiler_params=_cparams("arbitrary", "arbitrary"),
        name="rglru_mixer",
    )(u, u, p["lru_cw"], p["lru_cb"], p["lru_wa"], p["lru_ba"], p["lru_wx"], p["lru_bx"], p["lru_lam"], p["lru_nw"])


def _outproj_kernel(ys_ref, yh_ref, yl_ref, x_ref, w_ref, g_ref, nw_ref, sh_ref, sc_ref, xo_ref, *h_refs):
    acc = jnp.dot(ys_ref[...], w_ref[0:SSD_INNER, :], preferred_element_type=F32)
    acc = acc + jnp.dot(yh_ref[...], w_ref[SSD_INNER:SSD_INNER + HG_WIDTH, :], preferred_element_type=F32)
    acc = acc + jnp.dot(yl_ref[...], w_ref[SSD_INNER + HG_WIDTH:, :], preferred_element_type=F32)
    xn = x_ref[...] + g_ref[...] * acc
    xo_ref[...] = xn
    _store_planes(h_refs, _norm_mod(xn, nw_ref[...], sh_ref[...], sc_ref[...]))


def _outproj(y_ssd, y_hg, y_lru, x2, w_out, nw, mod3, seq):
    n, d = x2.shape
    tm = TM_OUTPROJ
    tpb = seq // tm

    def modspec(j):
        return pl.BlockSpec((None, 1, d), lambda i: ((i // tpb) * N_ADA + j, 0, 0))

    outs = pl.pallas_call(
        _outproj_kernel,
        grid=(n // tm,),
        in_specs=[
            pl.BlockSpec((tm, SSD_INNER), lambda i: (i, 0)),
            pl.BlockSpec((tm, HG_WIDTH), lambda i: (i, 0)),
            pl.BlockSpec((tm, LRU_WIDTH), lambda i: (i, 0)),
            pl.BlockSpec((tm, d), lambda i: (i, 0)),
            pl.BlockSpec(w_out.shape, lambda i: (0, 0)),
            modspec(2),
            pl.BlockSpec((1, d), lambda i: (0, 0)),
            modspec(3), modspec(4),
        ],
        out_specs=[pl.BlockSpec((tm, d), lambda i: (i, 0))] + [_plane_spec(tm)] * N_PLANES,
        out_shape=[jax.ShapeDtypeStruct((n, d), F32)] + [jax.ShapeDtypeStruct((n, LANES), jnp.uint32)] * N_PLANES,
        compiler_params=_cparams("arbitrary"),
        name="outproj",
    )(y_ssd, y_hg, y_lru, x2, w_out, mod3, nw, mod3, mod3)
    return outs[0], tuple(outs[1:])


def _plane_spec(tm):
    return pl.BlockSpec((tm, LANES), lambda i: (i, 0))


def _router_kernel(h0, h1, h2, h3, rw_ref, rb_ref, eid_ref, rank_ref, wt_ref, cnt_ref, carry, wscr):
    tm = h0.shape[0]

    @pl.when(pl.program_id(0) == 0)
    def _():
        carry[...] = jnp.zeros_like(carry)

    logit_t = lax.dot_general(rw_ref[...], _load_planes((h0, h1, h2, h3)), (((1,), (1,)), ((), ())),
                              precision=HI, preferred_element_type=F32)
    score = _sigmoid(logit_t)
    sel = score + rb_ref[...]
    neg_inf = jnp.float32(-jnp.inf)
    io_g = lax.broadcasted_iota(jnp.int32, (E_PER_GROUP, tm), 0)
    blocks, gscore = [], []
    for g in range(N_EXPERT_GROUPS):
        blk = sel[g * E_PER_GROUP:(g + 1) * E_PER_GROUP, :]
        m1 = jnp.max(blk, axis=0, keepdims=True)
        i1 = jnp.min(jnp.where(blk == m1, io_g, E_PER_GROUP), axis=0, keepdims=True)
        m2 = jnp.max(jnp.where(io_g == i1, neg_inf, blk), axis=0, keepdims=True)
        blocks.append(blk)
        gscore.append(m1 + m2)
    masked = []
    for g in range(N_EXPERT_GROUPS):
        rank = jnp.zeros((1, tm), jnp.int32)
        for o in range(N_EXPERT_GROUPS):
            if o == g:
                continue
            beats = (gscore[o] > gscore[g]) | ((gscore[o] == gscore[g]) & (o < g))
            rank = rank + beats.astype(jnp.int32)
        masked.append(jnp.where(rank < TOPK_GROUPS, blocks[g], MASK_SCORE))
    val = jnp.concatenate(masked, axis=0)
    io_e = lax.broadcasted_iota(jnp.int32, (N_EXPERTS, tm), 0)
    chosen = jnp.zeros((N_EXPERTS, tm), jnp.bool_)
    picks = []
    for k in range(TOP_K):
        m = jnp.max(val, axis=0, keepdims=True)
        idx = jnp.min(jnp.where(val == m, io_e, N_EXPERTS), axis=0, keepdims=True)
        pick = io_e == idx
        picks.append(pick)
        eid_ref[k:k + 1, :] = idx
        chosen = chosen | pick
        val = jnp.where(pick, neg_inf, val)
    w = jnp.where(chosen, score, 0.0)
    w = w / jnp.sum(w, axis=0, keepdims=True) * ROUTED_SCALE

    chosen_f = chosen.astype(F32)
    earlier = (lax.broadcasted_iota(jnp.int32, (tm, tm), 0) < lax.broadcasted_iota(jnp.int32, (tm, tm), 1))
    before = jnp.dot(chosen_f.astype(BF16), earlier.astype(BF16), preferred_element_type=F32)
    grank = carry[:, 0:1] + before
    wscr[...] = jnp.zeros_like(wscr)
    for k in range(TOP_K):
        rank_ref[k:k + 1, :] = jnp.sum(jnp.where(picks[k], grank, 0.0), axis=0, keepdims=True).astype(jnp.int32)
        wscr[k:k + 1, :] = jnp.sum(jnp.where(picks[k], w, 0.0), axis=0, keepdims=True)
    wt_ref[...] = wscr[...].T
    carry[...] = carry[...] + jnp.sum(chosen_f, axis=1, keepdims=True)
    cnt_ref[...] = carry[...]


def _router(hp, rw_t, rb):
    n = hp[0].shape[0]
    tm = TM_ROUTER
    return pl.pallas_call(
        _router_kernel,
        grid=(n // tm,),
        in_specs=[_plane_spec(tm)] * N_PLANES + [
            pl.BlockSpec(rw_t.shape, lambda i: (0, 0)),
            pl.BlockSpec((N_EXPERTS, 1), lambda i: (0, 0)),
        ],
        out_specs=[
            pl.BlockSpec((TOP_K, tm), lambda i: (0, i)),
            pl.BlockSpec((TOP_K, tm), lambda i: (0, i)),
            pl.BlockSpec((tm, LANES), lambda i: (i, 0)),
            pl.BlockSpec((N_EXPERTS, LANES), lambda i: (0, 0)),
        ],
        out_shape=[
            jax.ShapeDtypeStruct((TOP_K, n), jnp.int32),
            jax.ShapeDtypeStruct((TOP_K, n), jnp.int32),
            jax.ShapeDtypeStruct((n, LANES), F32),
            jax.ShapeDtypeStruct((N_EXPERTS, LANES), F32),
        ],
        scratch_shapes=[pltpu.VMEM((N_EXPERTS, LANES), F32), pltpu.VMEM((LANES, tm), F32)],
        compiler_params=_cparams("arbitrary"),
        name="router",
    )(*hp, rw_t, rb)


def _plan_kernel(n_slots, cnt_ref, eid_ref, rank_ref, dest_ref, be_ref, nb_ref, pad_ref):
    cnt = cnt_ref[...].astype(jnp.int32)
    padded = ((cnt + (MOE_BLOCK - 1)) >> MOE_BLOCK_LOG2) << MOE_BLOCK_LOG2
    ri = lax.broadcasted_iota(jnp.int32, (N_EXPERTS, N_EXPERTS), 0)
    ci = lax.broadcasted_iota(jnp.int32, (N_EXPERTS, N_EXPERTS), 1)
    pad_end = jnp.dot((ri >= ci).astype(F32), padded.astype(F32), precision=HI,
                      preferred_element_type=F32).astype(jnp.int32)
    pad_start = pad_end - padded
    eid = eid_ref[...]
    dest = rank_ref[...]
    for e in range(N_EXPERTS):
        dest = dest + jnp.where(eid == e, pad_start[e:e + 1, 0:1], 0)
    dest_ref[...] = dest
    nbp = be_ref.shape[1]
    jpos = lax.broadcasted_iota(jnp.int32, (N_EXPERTS, nbp), 1) * MOE_BLOCK
    be = jnp.sum((pad_end[:, 0:1] <= jpos).astype(jnp.int32), axis=0, keepdims=True)
    be_ref[...] = jnp.minimum(be, N_EXPERTS - 1)
    nb_ref[...] = pad_end[N_EXPERTS - 1:N_EXPERTS, :] >> MOE_BLOCK_LOG2
    lane = lax.broadcasted_iota(jnp.int32, (N_EXPERTS, LANES), 1)
    for r in range(MOE_BLOCK // LANES):
        s = pad_start + cnt + (r * LANES + lane)
        pad_ref[r * N_EXPERTS:(r + 1) * N_EXPERTS, :] = jnp.where(s < pad_end, s, n_slots + lane)


def _plan(cnt, eid, rank, n_slots):
    n = eid.shape[1]
    nbp = -(-(n_slots // MOE_BLOCK) // LANES) * LANES
    pad_rows = MOE_BLOCK // LANES * N_EXPERTS
    return pl.pallas_call(
        functools.partial(_plan_kernel, n_slots),
        grid=(1,),
        in_specs=[
            pl.BlockSpec(cnt.shape, lambda i: (0, 0)),
            pl.BlockSpec(eid.shape, lambda i: (0, 0)),
            pl.BlockSpec(rank.shape, lambda i: (0, 0)),
        ],
        out_specs=[
            pl.BlockSpec((TOP_K, n), lambda i: (0, 0)),
            pl.BlockSpec((1, nbp), lambda i: (0, 0)),
            pl.BlockSpec((1, LANES), lambda i: (0, 0)),
            pl.BlockSpec((pad_rows, LANES), lambda i: (0, 0)),
        ],
        out_shape=[
            jax.ShapeDtypeStruct((TOP_K, n), jnp.int32),
            jax.ShapeDtypeStruct((1, nbp), jnp.int32),
            jax.ShapeDtypeStruct((1, LANES), jnp.int32),
            jax.ShapeDtypeStruct((pad_rows, LANES), jnp.int32),
        ],
        compiler_params=_cparams("arbitrary"),
        name="moe_plan",
    )(cnt, eid, rank)


def _sc_mesh():
    return plsc.VectorSubcoreMesh(core_axis_name="c", subcore_axis_name="s")


def _sc_worker():
    return lax.axis_index("c") * SC_SUBCORES + lax.axis_index("s")


def _sc_dispatch(hp, dest_rows, pad_rows, n_rows):
    n = hp[0].shape[0]
    tiles_per_worker = n // SC_WIN // SC_WORKERS
    pad_per_worker = pad_rows.shape[0] // SC_WORKERS
    zeros = jnp.zeros((SC_WIN, LANES), jnp.uint32)

    def body(*refs):
        h = refs[:N_PLANES]
        dest_hbm, pad_hbm, z_hbm = refs[N_PLANES:N_PLANES + 3]
        xs = refs[N_PLANES + 3:2 * N_PLANES + 3]
        bufs = refs[2 * N_PLANES + 3:3 * N_PLANES + 3]
        ibuf, pbuf, sem = refs[3 * N_PLANES + 3:]
        wid = _sc_worker()

        pltpu.sync_copy(z_hbm, bufs[0])
        pltpu.sync_copy(pad_hbm.at[pl.ds(wid * pad_per_worker, pad_per_worker)], pbuf)
        copies = [pltpu.async_copy(bufs[0], xs[c].at[pbuf.at[r]], sem)
                  for r in range(pad_per_worker) for c in range(N_PLANES)]
        for cp in copies:
            cp.wait()

        @pl.loop(0, tiles_per_worker)
        def _(i):
            tile = wid * tiles_per_worker + i
            pltpu.sync_copy(dest_hbm.at[pl.ds(tile * TOP_K, TOP_K)], ibuf)
            for c in range(N_PLANES):
                pltpu.sync_copy(h[c].at[pl.ds(tile * SC_WIN, SC_WIN)], bufs[c])
            scatters = [pltpu.async_copy(bufs[c], xs[c].at[ibuf.at[k]], sem)
                        for c in range(N_PLANES) for k in range(TOP_K)]
            for cp in scatters:
                cp.wait()

    out_type = tuple(jax.ShapeDtypeStruct((n_rows, LANES), jnp.uint32) for _ in range(N_PLANES))
    scratch = ([pltpu.VMEM((SC_WIN, LANES), jnp.uint32)] * N_PLANES
               + [pltpu.VMEM((TOP_K, LANES), jnp.int32), pltpu.VMEM((pad_per_worker, LANES), jnp.int32),
                  pltpu.SemaphoreType.DMA])
    return pl.kernel(body, out_type=out_type, mesh=_sc_mesh(), scratch_types=scratch,
                     name="moe_sc_dispatch")(*hp, dest_rows, pad_rows, zeros)


def _sc_gather(ysp, dest_rows, n):
    tiles_per_worker = n // SC_WIN // SC_WORKERS

    def body(*refs):
        ys = refs[:N_PLANES]
        dest_hbm, g_hbm = refs[N_PLANES:N_PLANES + 2]
        bufs = refs[N_PLANES + 2:N_PLANES + 2 + SC_GATHER_BUFS]
        ibuf, sem = refs[N_PLANES + 2 + SC_GATHER_BUFS:]
        wid = _sc_worker()

        @pl.loop(0, tiles_per_worker)
        def _(i):
            tile = wid * tiles_per_worker + i
            pltpu.sync_copy(dest_hbm.at[pl.ds(tile * TOP_K, TOP_K)], ibuf)
            for c in range(N_PLANES):
                for k0 in range(0, TOP_K, SC_GATHER_BUFS):
                    gathers = [pltpu.async_copy(ys[c].at[ibuf.at[k0 + j]], bufs[j], sem)
                               for j in range(SC_GATHER_BUFS)]
                    for cp in gathers:
                        cp.wait()
                    stores = [pltpu.async_copy(
                        bufs[j], g_hbm.at[pl.ds(((k0 + j) * N_PLANES + c) * n + tile * SC_WIN, SC_WIN)], sem)
                        for j in range(SC_GATHER_BUFS)]
                    for cp in stores:
                        cp.wait()

    scratch = ([pltpu.VMEM((SC_WIN, LANES), jnp.uint32)] * SC_GATHER_BUFS
               + [pltpu.VMEM((TOP_K, LANES), jnp.int32), pltpu.SemaphoreType.DMA])
    return pl.kernel(body, out_type=jax.ShapeDtypeStruct((TOP_K * N_PLANES * n, LANES), jnp.uint32),
                     mesh=_sc_mesh(), scratch_types=scratch, name="moe_sc_gather")(*ysp, dest_rows)


def _expert_kernel(be_ref, nb_ref, *refs):
    del be_ref
    xs_refs = refs[:N_PLANES]
    wg_ref, wu_ref, wd_ref = refs[N_PLANES:N_PLANES + 3]
    ys_refs = refs[N_PLANES + 3:]
    used = pl.program_id(0) < nb_ref[0]

    @pl.when(used)
    def _():
        x = _load_planes(xs_refs).astype(BF16)
        a = jnp.dot(x, wg_ref[...].astype(BF16), preferred_element_type=F32)
        u = jnp.dot(x, wu_ref[...].astype(BF16), preferred_element_type=F32)
        y = jnp.dot((_silu(a) * u).astype(BF16), wd_ref[...].astype(BF16), preferred_element_type=F32)
        _store_planes(ys_refs, y)

    @pl.when(jnp.logical_not(used))
    def _():
        for ref in ys_refs:
            ref[...] = jnp.zeros_like(ref)


def _experts(be, nb, xsp, wg, wu, wd, layer, n_slots):
    d = wg.shape[2]

    def blk(j, be_ref, nb_ref):
        return jnp.minimum(j, jnp.maximum(nb_ref[0] - 1, 0))

    def wspec(shape):
        return pl.BlockSpec((None, None) + shape,
                            lambda j, be_ref, nb_ref: (layer, be_ref[blk(j, be_ref, nb_ref)], 0, 0))

    grid_spec = pltpu.PrefetchScalarGridSpec(
        num_scalar_prefetch=2,
        grid=(n_slots // MOE_BLOCK,),
        in_specs=[pl.BlockSpec((MOE_BLOCK, LANES), lambda j, be_ref, nb_ref: (blk(j, be_ref, nb_ref), 0))] * N_PLANES
        + [wspec((d, D_EXPERT)), wspec((d, D_EXPERT)), wspec((D_EXPERT, d))],
        out_specs=[pl.BlockSpec((MOE_BLOCK, LANES), lambda j, be_ref, nb_ref: (j, 0))] * N_PLANES,
    )
    return pl.pallas_call(
        _expert_kernel,
        grid_spec=grid_spec,
        out_shape=[jax.ShapeDtypeStruct((n_slots, LANES), jnp.uint32)] * N_PLANES,
        compiler_params=_cparams("arbitrary"),
        name="moe_experts",
    )(be, nb, *xsp, wg, wu, wd)


def _combine_kernel(final, g_ref, wt_ref, h0, h1, h2, h3, sg_ref, su_ref, sd_ref, x_ref, gate_ref, fw_ref, o_ref):
    hb = _load_planes((h0, h1, h2, h3)).astype(BF16)
    a = jnp.dot(hb, sg_ref[...], preferred_element_type=F32)
    u = jnp.dot(hb, su_ref[...], preferred_element_type=F32)
    acc = jnp.dot((_silu(a) * u).astype(BF16), sd_ref[...], preferred_element_type=F32)
    wt = wt_ref[...]
    for k in range(TOP_K):
        rows = _unpack_bf16_pairs(jnp.concatenate([g_ref[k * N_PLANES + c] for c in range(N_PLANES)], axis=1))
        acc = acc + wt[:, k:k + 1] * rows
    xn = x_ref[...] + gate_ref[...] * acc
    if final:
        ms = jnp.mean(xn * xn, axis=-1, keepdims=True)
        xn = xn * lax.rsqrt(ms + EPS) * fw_ref[...]
    o_ref[...] = xn


def _combine(g, wt, hp, sg, su, sd, x2, mod3, fw, seq, final):
    n, d = x2.shape
    tm = TM_COMBINE
    tpb = seq // tm
    return pl.pallas_call(
        functools.partial(_combine_kernel, final),
        grid=(n // tm,),
        in_specs=[
            pl.BlockSpec((TOP_K * N_PLANES, tm, LANES), lambda i: (0, i, 0)),
            pl.BlockSpec((tm, LANES), lambda i: (i, 0)),
        ] + [_plane_spec(tm)] * N_PLANES + [
            pl.BlockSpec(sg.shape, lambda i: (0, 0)),
            pl.BlockSpec(su.shape, lambda i: (0, 0)),
            pl.BlockSpec(sd.shape, lambda i: (0, 0)),
            pl.BlockSpec((tm, d), lambda i: (i, 0)),
            pl.BlockSpec((None, 1, d), lambda i: ((i // tpb) * N_ADA + 5, 0, 0)),
            pl.BlockSpec((1, d), lambda i: (0, 0)),
        ],
        out_specs=pl.BlockSpec((tm, d), lambda i: (i, 0)),
        out_shape=jax.ShapeDtypeStruct((n, d), F32),
        compiler_params=_cparams("arbitrary"),
        name="moe_combine",
    )(g.reshape(TOP_K * N_PLANES, n, LANES), wt, *hp, sg, su, sd, x2, mod3, fw)


def _blockdiag_pairs(w):
    z = jnp.zeros((LRU_BLOCK_W, LRU_BLOCK_W), w.dtype)
    tiles = []
    for j in range(LRU_BLOCKS // 2):
        top = jnp.concatenate([w[2 * j], z], axis=1)
        bot = jnp.concatenate([z, w[2 * j + 1]], axis=1)
        tiles.append(jnp.concatenate([top, bot], axis=0))
    return jnp.stack(tiles).astype(BF16)


def _hg_level_masks():
    ch = HG_CHUNK
    msk = np.zeros((HG_LEVELS, ch, ch), np.float32)
    for lvl in range(HG_LEVELS):
        half = 1 << lvl
        for t in range(ch):
            base = (t // (2 * half)) * (2 * half)
            if (t // half) % 2 == 1:
                msk[lvl, t, base:base + half] = 1.0
    return jnp.asarray(msk)


def _ssd_expand():
    e = np.zeros((LANES, SSD_INNER), np.float32)
    for h in range(SSD_HEADS):
        e[h, h * SSD_HEADDIM:(h + 1) * SSD_HEADDIM] = 1.0
    return jnp.asarray(e)


def _pad_lanes(v, width):
    return jnp.pad(v, (0, width - v.shape[0])).reshape(1, width)


def _layer_params(l, w_in, ssd_conv_w, ssd_conv_b, ssd_dt_bias, ssd_a_log, ssd_d, ssd_norm_w, hg_lower_bounds,
                  hg_norm_w, lru_conv_w, lru_conv_b, lru_wa, lru_ba, lru_wx, lru_bx, lru_lambda, lru_norm_w):
    wi = w_in[l]
    dt0 = SSD_INNER + SSD_INNER + 2 * SSD_GROUPS * SSD_STATE
    w_cat = jnp.concatenate([wi[:, :dt0], wi[:, dt0 + SSD_HEADS:], wi[:, dt0:dt0 + SSD_HEADS]], axis=1)
    w_cat = jnp.pad(w_cat, ((0, 0), (0, U_WIDTH - w_cat.shape[1]))).astype(BF16)
    msk = _hg_level_masks()
    return dict(
        w_cat=w_cat,
        cwx=ssd_conv_w[l][:, :SSD_INNER], cbx=ssd_conv_b[l][:SSD_INNER].reshape(1, -1),
        cwb=ssd_conv_w[l][:, SSD_INNER:], cbb=ssd_conv_b[l][SSD_INNER:].reshape(1, -1),
        dtb=_pad_lanes(ssd_dt_bias[l], LANES), alog=_pad_lanes(ssd_a_log[l], LANES),
        dful=jnp.repeat(ssd_d[l], SSD_HEADDIM).reshape(1, -1), ssd_nw=ssd_norm_w[l].reshape(1, -1),
        expand=_ssd_expand(),
        hg_lb=hg_lower_bounds, hg_nw=hg_norm_w[l].reshape(1, -1), hg_msk=msk,
        lru_cw=lru_conv_w[l], lru_cb=lru_conv_b[l].reshape(1, -1),
        lru_wa=_blockdiag_pairs(lru_wa[l]), lru_ba=lru_ba[l].reshape(1, -1),
        lru_wx=_blockdiag_pairs(lru_wx[l]), lru_bx=lru_bx[l].reshape(1, -1),
        lru_lam=lru_lambda[l].reshape(1, -1), lru_nw=lru_norm_w[l].reshape(1, -1),
    )


def kernel(x, c, ada_w, ada_b, norm_mix_w, norm_ffn_w, w_in, ssd_conv_w, ssd_conv_b, ssd_dt_bias, ssd_a_log, ssd_d, ssd_norm_w, hg_lower_bounds, hg_norm_w, lru_conv_w, lru_conv_b, lru_wa, lru_ba, lru_wx, lru_bx, lru_lambda, lru_norm_w, w_out, router_w, router_bias, exp_gate, exp_up, exp_down, sh_gate, sh_up, sh_down, final_norm_w):
    bsz, seq, d = x.shape
    depth = ada_w.shape[0]
    assert d == D_MODEL and seq % TM_OUTPROJ == 0 and seq % SSD_CHUNK == 0
    n = bsz * seq
    n_slots = n * TOP_K + N_EXPERTS * MOE_BLOCK
    x2 = x.reshape(n, d)
    mod = _adaln(c, ada_w, ada_b)
    fw = final_norm_w.reshape(1, d)
    for l in range(depth):
        p = _layer_params(l, w_in, ssd_conv_w, ssd_conv_b, ssd_dt_bias, ssd_a_log, ssd_d, ssd_norm_w,
                          hg_lower_bounds, hg_norm_w, lru_conv_w, lru_conv_b, lru_wa, lru_ba, lru_wx, lru_bx,
                          lru_lambda, lru_norm_w)
        mod3 = mod[l].reshape(bsz * N_ADA, 1, d)
        u = _inproj(x2, norm_mix_w[l].reshape(1, d), mod3, p["w_cat"], seq)
        y_ssd = _ssd(u, p, bsz, seq)
        y_hg = _hgrn2(u, p, l, bsz, seq)
        y_lru = _lru(u, p, bsz, seq)
        x2, hp = _outproj(y_ssd, y_hg, y_lru, x2, w_out[l].astype(BF16), norm_ffn_w[l].reshape(1, d), mod3, seq)
        eid, rank, wt, cnt = _router(hp, router_w[l].T, router_bias[l].reshape(N_EXPERTS, 1))
        dest, be, nb, pad_rows = _plan(cnt, eid, rank, n_slots)
        dest_rows = dest.reshape(TOP_K, n // LANES, LANES).transpose(1, 0, 2).reshape(n // LANES * TOP_K, LANES)
        xsp = _sc_dispatch(hp, dest_rows, pad_rows, n_slots + LANES)
        ysp = _experts(be.reshape(-1), nb[0, :1], xsp, exp_gate, exp_up, exp_down, l, n_slots)
        g = _sc_gather(ysp, dest_rows, n)
        x2 = _combine(g, wt, hp, sh_gate[l].astype(BF16), sh_up[l].astype(BF16), sh_down[l].astype(BF16),
                      x2, mod3, fw, seq, final=(l == depth - 1))
    return x2.reshape(bsz, seq, d)
```

```python
import functools

import jax
import jax.numpy as jnp
import numpy as np
from jax import lax
from jax.experimental import pallas as pl
from jax.experimental.pallas import tpu as pltpu
from jax.experimental.pallas import tpu_sc as plsc

F32 = jnp.float32
BF16 = jnp.bfloat16
HI = lax.Precision.HIGHEST

LANES = 128
SUBLANES = 8
VMEM_LIMIT_BYTES = 56 * 1024 * 1024

D_MODEL = 1024
EPS = 1e-6
N_ADA = 6
CONV_WIDTH = 4
SSD_INNER = 1024
SSD_HEADDIM = 64
SSD_HEADS = 16
SSD_GROUPS = 2
SSD_STATE = 128
SSD_CHUNK = 128
SSD_GROUP_W = SSD_INNER // SSD_GROUPS
HG_WIDTH = 512
HG_EXPAND = 128
HG_HEADS = 4
HG_CHUNK = 128
HG_LEVELS = 7
LRU_WIDTH = 512
LRU_BLOCKS = 8
LRU_BLOCK_W = 64
LRU_C = 8.0
N_EXPERTS = 64
TOP_K = 8
N_EXPERT_GROUPS = 8
E_PER_GROUP = 8
TOPK_GROUPS = 4
D_EXPERT = 256
ROUTED_SCALE = 2.5
MASK_SCORE = -1.0e4

COL_Z = 0
COL_XS = 1024
COL_BC = 2048
COL_HQ = 2560
COL_HF = 3072
COL_HV = 3584
COL_HG = 4096
COL_LG = 4608
COL_LX = 5120
COL_DT = 5632
U_WIDTH = 5760

TM_INPROJ = 256
TM_OUTPROJ = 512
TM_ROUTER = 512
TM_COMBINE = 256
SC_CORES = 2
SC_SUBCORES = 16
SC_WORKERS = SC_CORES * SC_SUBCORES
SC_WIN = LANES
SC_GATHER_BUFS = 4
MOE_BLOCK_LOG2 = 9
MOE_BLOCK = 1 << MOE_BLOCK_LOG2
R_HG = 256
R_LRU = 256


def _cparams(*sem):
    return pltpu.CompilerParams(dimension_semantics=sem, vmem_limit_bytes=VMEM_LIMIT_BYTES)


def _sigmoid(x):
    return 0.5 * jnp.tanh(0.5 * x) + 0.5


def _silu(x):
    return x * _sigmoid(x)


def _split3(x):
    x1 = x.astype(BF16)
    r1 = x - x1.astype(F32)
    x2 = r1.astype(BF16)
    x3 = (r1 - x2.astype(F32)).astype(BF16)
    return x1, x2, x3


def _softplus(x):
    return jnp.maximum(x, 0.0) + jnp.log1p(jnp.exp(-jnp.abs(x)))


def _norm_mod(x, nw, shift, scale):
    ms = jnp.mean(x * x, axis=-1, keepdims=True)
    y = x * lax.rsqrt(ms + EPS) * nw
    return y * (1.0 + scale) + shift


_HI16 = np.uint32(0xFFFF0000)


def _pack_bf16_pairs(x):
    half = x.shape[1] // 2
    bits = lax.bitcast_convert_type(x.astype(BF16).astype(F32), jnp.uint32)
    return (bits[:, :half] & _HI16) | (bits[:, half:] >> 16)


def _unpack_bf16_pairs(w):
    hi = lax.bitcast_convert_type(w & _HI16, F32)
    lo = lax.bitcast_convert_type(w << 16, F32)
    return jnp.concatenate([hi, lo], axis=1)


N_PLANES = D_MODEL // 2 // LANES


def _store_planes(refs, x):
    packed = _pack_bf16_pairs(x)
    for c, ref in enumerate(refs):
        ref[...] = packed[:, c * LANES:(c + 1) * LANES]


def _load_planes(refs):
    return _unpack_bf16_pairs(jnp.concatenate([ref[...] for ref in refs], axis=1))


def _ada_kernel(c_ref, w_ref, b_ref, o_ref):
    c = c_ref[...]
    o_ref[...] = jnp.dot(_silu(c), w_ref[...], precision=HI, preferred_element_type=F32) + b_ref[...]


def _adaln(c, ada_w, ada_b):
    depth, d, n6 = ada_w.shape
    bsz = c.shape[0]
    tn = 1536
    return pl.pallas_call(
        _ada_kernel,
        grid=(depth, n6 // tn),
        in_specs=[
            pl.BlockSpec((bsz, d), lambda l, j: (0, 0)),
            pl.BlockSpec((None, d, tn), lambda l, j: (l, 0, j)),
            pl.BlockSpec((None, 1, tn), lambda l, j: (l, 0, j)),
        ],
        out_specs=pl.BlockSpec((None, bsz, tn), lambda l, j: (l, 0, j)),
        out_shape=jax.ShapeDtypeStruct((depth, bsz, n6), F32),
        compiler_params=_cparams("arbitrary", "arbitrary"),
        name="adaln_mod",
    )(c, ada_w, ada_b.reshape(depth, 1, n6))


def _inproj_kernel(x_ref, nw_ref, sh_ref, sc_ref, w_ref, o_ref):
    h = _norm_mod(x_ref[...], nw_ref[...], sh_ref[...], sc_ref[...])
    o_ref[...] = jnp.dot(h.astype(BF16), w_ref[...], preferred_element_type=F32)


def _inproj(x2, nw, mod3, w_cat, seq):
    n, d = x2.shape
    tm = TM_INPROJ
    tpb = seq // tm
    return pl.pallas_call(
        _inproj_kernel,
        grid=(n // tm,),
        in_specs=[
            pl.BlockSpec((tm, d), lambda i: (i, 0)),
            pl.BlockSpec((1, d), lambda i: (0, 0)),
            pl.BlockSpec((None, 1, d), lambda i: ((i // tpb) * N_ADA + 0, 0, 0)),
            pl.BlockSpec((None, 1, d), lambda i: ((i // tpb) * N_ADA + 1, 0, 0)),
            pl.BlockSpec((d, U_WIDTH), lambda i: (0, 0), pipeline_mode=pl.Buffered(1)),
        ],
        out_specs=pl.BlockSpec((tm, U_WIDTH), lambda i: (i, 0)),
        out_shape=jax.ShapeDtypeStruct((n, U_WIDTH), F32),
        compiler_params=_cparams("arbitrary"),
        name="inproj",
    )(x2, nw, mod3, mod3, w_cat)


def _causal_conv(cur_ref, ext, tail, cw_ref, cb_ref):
    rows = cur_ref.shape[0]
    ext[0:SUBLANES, :] = tail[...]
    ext[SUBLANES:SUBLANES + rows, :] = cur_ref[...]
    acc = cb_ref[...]
    for w in range(CONV_WIDTH):
        start = SUBLANES - (CONV_WIDTH - 1) + w
        acc = acc + ext[start:start + rows, :] * cw_ref[w:w + 1, :]
    tail[...] = ext[rows:rows + SUBLANES, :]
    return acc


def _ssd_kernel(z_ref, xs_ref, bc_ref, dt_ref, cwx_ref, cbx_ref, cwb_ref, cbb_ref, dtb_ref, alog_ref,
                dful_ref, nw_ref, e_ref, o_ref, extx, extb, tailx, tailb, hstate):
    c = pl.program_id(1)
    q = SSD_CHUNK

    @pl.when(c == 0)
    def _():
        tailx[...] = jnp.zeros_like(tailx)
        tailb[...] = jnp.zeros_like(tailb)
        hstate[...] = jnp.zeros_like(hstate)

    xs = _silu(_causal_conv(xs_ref, extx, tailx, cwx_ref, cbx_ref))
    bc = _silu(_causal_conv(bc_ref, extb, tailb, cwb_ref, cbb_ref))

    dt = _softplus(dt_ref[...] + dtb_ref[...])
    a = dt * (-jnp.exp(alog_ref[...]))
    ri = lax.broadcasted_iota(jnp.int32, (q, q), 0)
    ci = lax.broadcasted_iota(jnp.int32, (q, q), 1)
    tril = ri >= ci
    tril_f = tril.astype(F32)
    tril_b = tril.astype(BF16)
    acum = jnp.dot(jnp.concatenate([tril_b] * 3, axis=1), jnp.concatenate(_split3(a), axis=0),
                   preferred_element_type=F32)
    acum_t = acum.T
    expand3 = e_ref[...]
    dt_full = jnp.dot(jnp.concatenate(_split3(dt), axis=1), expand3, preferred_element_type=F32)
    acum_full = jnp.dot(jnp.concatenate(_split3(acum), axis=1), expand3, preferred_element_type=F32)
    alast_full = acum_full[q - 1:q, :]

    xdt = xs * dt_full
    xdt_b = xdt.astype(BF16)
    exp_a = jnp.exp(acum_full)
    xd_b = (xdt * jnp.exp(alast_full - acum_full)).astype(BF16)
    state_decay = jnp.exp(alast_full)
    left = lax.broadcasted_iota(jnp.int32, (q, LANES), 1) < SSD_HEADDIM
    zero_b = jnp.zeros((q, LANES), BF16)

    ys = []
    for g in range(SSD_GROUPS):
        b_g = bc[:, g * SSD_STATE:(g + 1) * SSD_STATE]
        c_g = bc[:, (SSD_GROUPS + g) * SSD_STATE:(SSD_GROUPS + g + 1) * SSD_STATE]
        c_b = c_g.astype(BF16)
        cb = lax.dot_general(c_b, b_g.astype(BF16), (((1,), (1,)), ((), ())), preferred_element_type=F32)
        cb = cb * tril_f
        cs = slice(g * SSD_GROUP_W, (g + 1) * SSD_GROUP_W)
        h_g = hstate[:, cs]
        y_off = jnp.dot(c_b, h_g.astype(BF16), preferred_element_type=F32) * exp_a[:, cs]
        pieces = []
        for pr in range(SSD_HEADS // SSD_GROUPS // 2):
            h0 = g * (SSD_HEADS // SSD_GROUPS) + 2 * pr
            ms = []
            for h in (h0, h0 + 1):
                col = acum[:, h:h + 1]
                row = acum_t[h:h + 1, :]
                ms.append((cb * jnp.exp(jnp.minimum(col - row, 0.0))).astype(BF16))
            lhs = jnp.concatenate(ms, axis=1)
            xp = xdt_b[:, h0 * SSD_HEADDIM:(h0 + 2) * SSD_HEADDIM]
            rhs = jnp.concatenate([jnp.where(left, xp, zero_b), jnp.where(left, zero_b, xp)], axis=0)
            pieces.append(jnp.dot(lhs, rhs, preferred_element_type=F32))
        ys.append(jnp.concatenate(pieces, axis=1) + y_off)
        b_t = b_g.T.astype(BF16)
        hstate[:, cs] = h_g * state_decay[:, cs] + jnp.dot(b_t, xd_b[:, cs], preferred_element_type=F32)

    y = jnp.concatenate(ys, axis=1) + xs * dful_ref[...]
    y = y * _silu(z_ref[...])
    outs = []
    for g in range(SSD_GROUPS):
        cs = slice(g * SSD_GROUP_W, (g + 1) * SSD_GROUP_W)
        yg = y[:, cs]
        ms = jnp.mean(yg * yg, axis=-1, keepdims=True)
        outs.append(yg * lax.rsqrt(ms + EPS) * nw_ref[:, cs])
    o_ref[...] = jnp.concatenate(outs, axis=1).astype(o_ref.dtype)


def _ssd(u, p, bsz, seq):
    q = SSD_CHUNK
    nc = seq // q
    n = bsz * seq

    def rows(b, c):
        return b * nc + c

    def const(shape):
        return pl.BlockSpec(shape, lambda b, c: (0,) * len(shape))

    return pl.pallas_call(
        _ssd_kernel,
        grid=(bsz, nc),
        in_specs=[
            pl.BlockSpec((q, SSD_INNER), lambda b, c: (rows(b, c), COL_Z // SSD_INNER)),
            pl.BlockSpec((q, SSD_INNER), lambda b, c: (rows(b, c), COL_XS // SSD_INNER)),
            pl.BlockSpec((q, 512), lambda b, c: (rows(b, c), COL_BC // 512)),
            pl.BlockSpec((q, LANES), lambda b, c: (rows(b, c), COL_DT // LANES)),
            const((CONV_WIDTH, SSD_INNER)), const((1, SSD_INNER)),
            const((CONV_WIDTH, 512)), const((1, 512)),
            const((1, LANES)), const((1, LANES)),
            const((1, SSD_INNER)), const((1, SSD_INNER)),
            const((3 * LANES, SSD_INNER)),
        ],
        out_specs=pl.BlockSpec((q, SSD_INNER), lambda b, c: (rows(b, c), 0)),
        out_shape=jax.ShapeDtypeStruct((n, SSD_INNER), BF16),
        scratch_shapes=[
            pltpu.VMEM((q + SUBLANES, SSD_INNER), F32),
            pltpu.VMEM((q + SUBLANES, 512), F32),
            pltpu.VMEM((SUBLANES, SSD_INNER), F32),
            pltpu.VMEM((SUBLANES, 512), F32),
            pltpu.VMEM((SSD_STATE, SSD_INNER), F32),
        ],
        compiler_params=_cparams("arbitrary", "arbitrary"),
        name="ssd_mixer",
    )(u, u, u, u, p["cwx"], p["cbx"], p["cwb"], p["cbb"], p["dtb"], p["alog"], p["dful"], p["ssd_nw"], p["expand"])


def _boundary_rows(b, lvl):
    half = 1 << lvl
    parts = []
    for v in range(b.shape[0] // SUBLANES):
        r0 = v * SUBLANES
        if 2 * half >= SUBLANES:
            src = (r0 // (2 * half)) * (2 * half) + half - 1
            parts.append(jnp.broadcast_to(b[src:src + 1, :], (SUBLANES, b.shape[1])))
        else:
            sub = lax.broadcasted_iota(jnp.int32, (SUBLANES, b.shape[1]), 0)
            piece = None
            for g in range(SUBLANES // (2 * half)):
                src = r0 + g * 2 * half + half - 1
                cand = jnp.broadcast_to(b[src:src + 1, :], (SUBLANES, b.shape[1]))
                piece = cand if piece is None else jnp.where(sub >= g * 2 * half, cand, piece)
            parts.append(piece)
    return jnp.concatenate(parts, axis=0)


def _hgrn2_kernel(layer, q_ref, f_ref, v_ref, g_ref, lb_ref, nw_ref, msk_ref, o_ref, state_t):
    t = pl.program_id(1)
    ch = HG_CHUNK

    @pl.when(t == 0)
    def _():
        state_t[...] = jnp.zeros_like(state_t)

    lrows = [lb_ref[j:j + 1, :] for j in range(lb_ref.shape[0])]
    mx = functools.reduce(jnp.maximum, lrows)
    es = [jnp.exp(r - mx) for r in lrows]
    den = functools.reduce(lambda a_, b_: a_ + b_, es)
    lb = jnp.zeros_like(mx)
    for j in range(1, layer + 1):
        lb = lb + es[j] / den
    one_minus_lb = 1.0 - lb
    nw = nw_ref[...]

    ri = lax.broadcasted_iota(jnp.int32, (ch, ch), 0)
    ci = lax.broadcasted_iota(jnp.int32, (ch, ch), 1)
    tril3 = jnp.concatenate([(ri >= ci).astype(BF16)] * 3, axis=1)
    rowi = lax.broadcasted_iota(jnp.int32, (ch, HG_EXPAND), 0)
    tgt = [((rowi >> lvl) & 1) == 1 for lvl in range(HG_LEVELS)]

    def head_chunk(h, rs):
        cs = slice(h * HG_EXPAND, (h + 1) * HG_EXPAND)
        qq = _silu(q_ref[rs, cs])
        kk = one_minus_lb[:, cs] * _sigmoid(-f_ref[rs, cs])
        logf = jnp.log1p(-kk)
        vv = v_ref[rs, cs]
        vb = vv.astype(BF16)
        b = jnp.dot(tril3, jnp.concatenate(_split3(logf), axis=0),
                    preferred_element_type=F32)
        st = state_t[h]
        o = lax.dot_general((qq * jnp.exp(b)).astype(BF16), st.astype(BF16), (((1,), (1,)), ((), ())),
                            preferred_element_type=F32)
        attn = jnp.zeros((ch, ch), F32)
        for lvl in range(HG_LEVELS):
            if lvl == 0:
                qe = jnp.where(tgt[0], qq * (1.0 - kk), 0.0)
                ke = jnp.where(tgt[0], 0.0, kk)
            else:
                m = _boundary_rows(b, lvl)
                qe = jnp.where(tgt[lvl], qq * jnp.exp(jnp.where(tgt[lvl], b - m, 0.0)), 0.0)
                ke = jnp.where(tgt[lvl], 0.0, kk * jnp.exp(jnp.where(tgt[lvl], 0.0, m - b)))
            prod = lax.dot_general(qe.astype(BF16), ke.astype(BF16), (((1,), (1,)), ((), ())),
                                   preferred_element_type=F32)
            attn = attn + msk_ref[lvl] * prod
        diag = jnp.sum(qq * kk, axis=-1, keepdims=True)
        o = o + jnp.dot(attn.astype(BF16), vb, preferred_element_type=F32) + diag * vv
        b_last = b[ch - 1:ch, :]
        kd = (kk * jnp.exp(b_last - b)).astype(BF16)
        state_t[h] = st * jnp.exp(b_last) + jnp.dot(vv.T.astype(BF16), kd, preferred_element_type=F32)
        ms = jnp.mean(o * o, axis=-1, keepdims=True)
        y = o * lax.rsqrt(ms + EPS) * nw[:, cs]
        o_ref[rs, cs] = (y * _silu(g_ref[rs, cs])).astype(o_ref.dtype)

    def chunk(j, carry):
        rs = pl.ds(pl.multiple_of(j * ch, ch), ch)
        for h in range(HG_HEADS):
            head_chunk(h, rs)
        return carry

    lax.fori_loop(0, q_ref.shape[0] // ch, chunk, 0)


def _hgrn2(u, p, layer, bsz, seq):
    r = R_HG
    nt = seq // r
    n = bsz * seq

    def col(base):
        return lambda b, t: (b * nt + t, base // HG_WIDTH)

    return pl.pallas_call(
        functools.partial(_hgrn2_kernel, layer),
        grid=(bsz, nt),
        in_specs=[
            pl.BlockSpec((r, HG_WIDTH), col(COL_HQ)),
            pl.BlockSpec((r, HG_WIDTH), col(COL_HF)),
            pl.BlockSpec((r, HG_WIDTH), col(COL_HV)),
            pl.BlockSpec((r, HG_WIDTH), col(COL_HG)),
            pl.BlockSpec(p["hg_lb"].shape, lambda b, t: (0, 0)),
            pl.BlockSpec((1, HG_WIDTH), lambda b, t: (0, 0)),
            pl.BlockSpec((HG_LEVELS, HG_CHUNK, HG_CHUNK), lambda b, t: (0, 0, 0)),
        ],
        out_specs=pl.BlockSpec((r, HG_WIDTH), lambda b, t: (b * nt + t, 0)),
        out_shape=jax.ShapeDtypeStruct((n, HG_WIDTH), BF16),
        scratch_shapes=[pltpu.VMEM((HG_HEADS, HG_EXPAND, HG_EXPAND), F32)],
        compiler_params=_cparams("arbitrary", "arbitrary"),
        name="hgrn2_mixer",
    )(u, u, u, u, p["hg_lb"], p["hg_nw"], p["hg_msk"])


def _lru_kernel(g_ref, x_ref, cw_ref, cb_ref, wa_ref, ba_ref, wx_ref, bx_ref, lam_ref, nw_ref, o_ref,
                ext, tail, hcarry):
    t = pl.program_id(1)
    rows = x_ref.shape[0]

    @pl.when(t == 0)
    def _():
        tail[...] = jnp.zeros_like(tail)
        hcarry[...] = jnp.zeros_like(hcarry)

    xb = _causal_conv(x_ref, ext, tail, cw_ref, cb_ref)
    xbb = xb.astype(BF16)
    npair = LRU_WIDTH // LANES
    ra = jnp.concatenate([jnp.dot(xbb[:, j * LANES:(j + 1) * LANES], wa_ref[j], preferred_element_type=F32)
                          for j in range(npair)], axis=1)
    rx = jnp.concatenate([jnp.dot(xbb[:, j * LANES:(j + 1) * LANES], wx_ref[j], preferred_element_type=F32)
                          for j in range(npair)], axis=1)
    r = _sigmoid(ra + ba_ref[...])
    i = _sigmoid(rx + bx_ref[...])
    log_a = -LRU_C * r * _softplus(-lam_ref[...])
    a = jnp.exp(log_a)
    th = jnp.tanh(log_a)
    u = jnp.sqrt(-2.0 * th / (1.0 - th)) * (i * xb)

    rowi = lax.broadcasted_iota(jnp.int32, (rows, LRU_WIDTH), 0)
    acc_a, acc_u = a, u
    d = 1
    while d < rows:
        keep = rowi >= d
        a_sh = jnp.where(keep, pltpu.roll(acc_a, d, 0), 1.0)
        u_sh = jnp.where(keep, pltpu.roll(acc_u, d, 0), 0.0)
        acc_u = acc_a * u_sh + acc_u
        acc_a = acc_a * a_sh
        d *= 2
    h = acc_a * hcarry[0:1, :] + acc_u
    hcarry[0:1, :] = h[rows - 1:rows, :]

    gate = g_ref[...]
    gelu = 0.5 * gate * (1.0 + jnp.tanh(np.sqrt(2.0 / np.pi).astype(np.float32) * (gate + 0.044715 * (gate * gate * gate))))
    y = h * gelu
    ms = jnp.mean(y * y, axis=-1, keepdims=True)
    o_ref[...] = (y * lax.rsqrt(ms + EPS) * nw_ref[...]).astype(o_ref.dtype)


def _lru(u, p, bsz, seq):
    r = R_LRU
    nt = seq // r
    n = bsz * seq
    npair = LRU_WIDTH // LANES

    def const(shape):
        return pl.BlockSpec(shape, lambda b, t: (0,) * len(shape))

    return pl.pallas_call(
        _lru_kernel,
        grid=(bsz, nt),
        in_specs=[
            pl.BlockSpec((r, LRU_WIDTH), lambda b, t: (b * nt + t, COL_LG // LRU_WIDTH)),
            pl.BlockSpec((r, LRU_WIDTH), lambda b, t: (b * nt + t, COL_LX // LRU_WIDTH)),
            const((CONV_WIDTH, LRU_WIDTH)), const((1, LRU_WIDTH)),
            const((npair, LANES, LANES)), const((1, LRU_WIDTH)),
            const((npair, LANES, LANES)), const((1, LRU_WIDTH)),
            const((1, LRU_WIDTH)), const((1, LRU_WIDTH)),
        ],
        out_specs=pl.BlockSpec((r, LRU_WIDTH), lambda b, t: (b * nt + t, 0)),
        out_shape=jax.ShapeDtypeStruct((n, LRU_WIDTH), BF16),
        scratch_shapes=[
            pltpu.VMEM((r + SUBLANES, LRU_WIDTH), F32),
            pltpu.VMEM((SUBLANES, LRU_WIDTH), F32),
            pltpu.VMEM((SUBLANES, LRU_WIDTH), F32),
        ],
        compiler_params=_cparams("arbitrary", "arbitrary"),
        name="rglru_mixer",
    )(u, u, p["lru_cw"], p["lru_cb"], p["lru_wa"], p["lru_ba"], p["lru_wx"], p["lru_bx"], p["lru_lam"], p["lru_nw"])


def _outproj_kernel(ys_ref, yh_ref, yl_ref, x_ref, w_ref, g_ref, nw_ref, sh_ref, sc_ref, xo_ref, *h_refs):
    acc = jnp.dot(ys_ref[...], w_ref[0:SSD_INNER, :], preferred_element_type=F32)
    acc = acc + jnp.dot(yh_ref[...], w_ref[SSD_INNER:SSD_INNER + HG_WIDTH, :], preferred_element_type=F32)
    acc = acc + jnp.dot(yl_ref[...], w_ref[SSD_INNER + HG_WIDTH:, :], preferred_element_type=F32)
    xn = x_ref[...] + g_ref[...] * acc
    xo_ref[...] = xn
    _store_planes(h_refs, _norm_mod(xn, nw_ref[...], sh_ref[...], sc_ref[...]))


def _outproj(y_ssd, y_hg, y_lru, x2, w_out, nw, mod3, seq):
    n, d = x2.shape
    tm = TM_OUTPROJ
    tpb = seq // tm

    def modspec(j):
        return pl.BlockSpec((None, 1, d), lambda i: ((i // tpb) * N_ADA + j, 0, 0))

    outs = pl.pallas_call(
        _outproj_kernel,
        grid=(n // tm,),
        in_specs=[
            pl.BlockSpec((tm, SSD_INNER), lambda i: (i, 0)),
            pl.BlockSpec((tm, HG_WIDTH), lambda i: (i, 0)),
            pl.BlockSpec((tm, LRU_WIDTH), lambda i: (i, 0)),
            pl.BlockSpec((tm, d), lambda i: (i, 0)),
            pl.BlockSpec(w_out.shape, lambda i: (0, 0)),
            modspec(2),
            pl.BlockSpec((1, d), lambda i: (0, 0)),
            modspec(3), modspec(4),
        ],
        out_specs=[pl.BlockSpec((tm, d), lambda i: (i, 0))] + [_plane_spec(tm)] * N_PLANES,
        out_shape=[jax.ShapeDtypeStruct((n, d), F32)] + [jax.ShapeDtypeStruct((n, LANES), jnp.uint32)] * N_PLANES,
        compiler_params=_cparams("arbitrary"),
        name="outproj",
    )(y_ssd, y_hg, y_lru, x2, w_out, mod3, nw, mod3, mod3)
    return outs[0], tuple(outs[1:])


def _plane_spec(tm):
    return pl.BlockSpec((tm, LANES), lambda i: (i, 0))


def _router_kernel(h0, h1, h2, h3, rw_ref, rb_ref, eid_ref, rank_ref, wt_ref, cnt_ref, carry, wscr):
    tm = h0.shape[0]

    @pl.when(pl.program_id(0) == 0)
    def _():
        carry[...] = jnp.zeros_like(carry)

    hb = _load_planes((h0, h1, h2, h3)).astype(BF16)
    logit_t = sum(lax.dot_general(part, hb, (((1,), (1,)), ((), ())), preferred_element_type=F32)
                  for part in _split3(rw_ref[...]))
    score = _sigmoid(logit_t)
    sel = score + rb_ref[...]
    neg_inf = jnp.float32(-jnp.inf)
    io_g = lax.broadcasted_iota(jnp.int32, (E_PER_GROUP, tm), 0)
    blocks, gscore = [], []
    for g in range(N_EXPERT_GROUPS):
        blk = sel[g * E_PER_GROUP:(g + 1) * E_PER_GROUP, :]
        m1 = jnp.max(blk, axis=0, keepdims=True)
        i1 = jnp.min(jnp.where(blk == m1, io_g, E_PER_GROUP), axis=0, keepdims=True)
        m2 = jnp.max(jnp.where(io_g == i1, neg_inf, blk), axis=0, keepdims=True)
        blocks.append(blk)
        gscore.append(m1 + m2)
    masked = []
    for g in range(N_EXPERT_GROUPS):
        rank = jnp.zeros((1, tm), jnp.int32)
        for o in range(N_EXPERT_GROUPS):
            if o == g:
                continue
            beats = (gscore[o] > gscore[g]) | ((gscore[o] == gscore[g]) & (o < g))
            rank = rank + beats.astype(jnp.int32)
        masked.append(jnp.where(rank < TOPK_GROUPS, blocks[g], MASK_SCORE))
    val = jnp.concatenate(masked, axis=0)
    io_e = lax.broadcasted_iota(jnp.int32, (N_EXPERTS, tm), 0)
    chosen = jnp.zeros((N_EXPERTS, tm), jnp.bool_)
    picks = []
    for k in range(TOP_K):
        m = jnp.max(val, axis=0, keepdims=True)
        idx = jnp.min(jnp.where(val == m, io_e, N_EXPERTS), axis=0, keepdims=True)
        pick = io_e == idx
        picks.append(pick)
        eid_ref[k:k + 1, :] = idx
        chosen = chosen | pick
        val = jnp.where(pick, neg_inf, val)
    w = jnp.where(chosen, score, 0.0)
    w = w / jnp.sum(w, axis=0, keepdims=True) * ROUTED_SCALE

    chosen_f = chosen.astype(F32)
    earlier = (lax.broadcasted_iota(jnp.int32, (tm, tm), 0) < lax.broadcasted_iota(jnp.int32, (tm, tm), 1))
    before = jnp.dot(chosen_f.astype(BF16), earlier.astype(BF16), preferred_element_type=F32)
    grank = carry[:, 0:1] + before
    wscr[...] = jnp.zeros_like(wscr)
    for k in range(TOP_K):
        rank_ref[k:k + 1, :] = jnp.sum(jnp.where(picks[k], grank, 0.0), axis=0, keepdims=True).astype(jnp.int32)
        wscr[k:k + 1, :] = jnp.sum(jnp.where(picks[k], w, 0.0), axis=0, keepdims=True)
    wt_ref[...] = wscr[...].T
    carry[...] = carry[...] + jnp.sum(chosen_f, axis=1, keepdims=True)
    cnt_ref[...] = carry[...]


def _router(hp, rw_t, rb):
    n = hp[0].shape[0]
    tm = TM_ROUTER
    return pl.pallas_call(
        _router_kernel,
        grid=(n // tm,),
        in_specs=[_plane_spec(tm)] * N_PLANES + [
            pl.BlockSpec(rw_t.shape, lambda i: (0, 0)),
            pl.BlockSpec((N_EXPERTS, 1), lambda i: (0, 0)),
        ],
        out_specs=[
            pl.BlockSpec((TOP_K, tm), lambda i: (0, i)),
            pl.BlockSpec((TOP_K, tm), lambda i: (0, i)),
            pl.BlockSpec((tm, LANES), lambda i: (i, 0)),
            pl.BlockSpec((N_EXPERTS, LANES), lambda i: (0, 0)),
        ],
        out_shape=[
            jax.ShapeDtypeStruct((TOP_K, n), jnp.int32),
            jax.ShapeDtypeStruct((TOP_K, n), jnp.int32),
            jax.ShapeDtypeStruct((n, LANES), F32),
            jax.ShapeDtypeStruct((N_EXPERTS, LANES), F32),
        ],
        scratch_shapes=[pltpu.VMEM((N_EXPERTS, LANES), F32), pltpu.VMEM((LANES, tm), F32)],
        compiler_params=_cparams("arbitrary"),
        name="router",
    )(*hp, rw_t, rb)


def _plan_kernel(n_slots, cnt_ref, eid_ref, rank_ref, dest_ref, be_ref, nb_ref, pad_ref):
    cnt = cnt_ref[...].astype(jnp.int32)
    padded = ((cnt + (MOE_BLOCK - 1)) >> MOE_BLOCK_LOG2) << MOE_BLOCK_LOG2
    ri = lax.broadcasted_iota(jnp.int32, (N_EXPERTS, N_EXPERTS), 0)
    ci = lax.broadcasted_iota(jnp.int32, (N_EXPERTS, N_EXPERTS), 1)
    pad_end = jnp.dot((ri >= ci).astype(F32), padded.astype(F32), precision=HI,
                      preferred_element_type=F32).astype(jnp.int32)
    pad_start = pad_end - padded
    eid = eid_ref[...]
    dest = rank_ref[...]
    for e in range(N_EXPERTS):
        dest = dest + jnp.where(eid == e, pad_start[e:e + 1, 0:1], 0)
    dest_ref[...] = dest
    nbp = be_ref.shape[1]
    jpos = lax.broadcasted_iota(jnp.int32, (N_EXPERTS, nbp), 1) * MOE_BLOCK
    be = jnp.sum((pad_end[:, 0:1] <= jpos).astype(jnp.int32), axis=0, keepdims=True)
    be_ref[...] = jnp.minimum(be, N_EXPERTS - 1)
    nb_ref[...] = pad_end[N_EXPERTS - 1:N_EXPERTS, :] >> MOE_BLOCK_LOG2
    lane = lax.broadcasted_iota(jnp.int32, (N_EXPERTS, LANES), 1)
    for r in range(MOE_BLOCK // LANES):
        s = pad_start + cnt + (r * LANES + lane)
        pad_ref[r * N_EXPERTS:(r + 1) * N_EXPERTS, :] = jnp.where(s < pad_end, s, n_slots + lane)


def _plan(cnt, eid, rank, n_slots):
    n = eid.shape[1]
    nbp = -(-(n_slots // MOE_BLOCK) // LANES) * LANES
    pad_rows = MOE_BLOCK // LANES * N_EXPERTS
    return pl.pallas_call(
        functools.partial(_plan_kernel, n_slots),
        grid=(1,),
        in_specs=[
            pl.BlockSpec(cnt.shape, lambda i: (0, 0)),
            pl.BlockSpec(eid.shape, lambda i: (0, 0)),
            pl.BlockSpec(rank.shape, lambda i: (0, 0)),
        ],
        out_specs=[
            pl.BlockSpec((TOP_K, n), lambda i: (0, 0)),
            pl.BlockSpec((1, nbp), lambda i: (0, 0)),
            pl.BlockSpec((1, LANES), lambda i: (0, 0)),
            pl.BlockSpec((pad_rows, LANES), lambda i: (0, 0)),
        ],
        out_shape=[
            jax.ShapeDtypeStruct((TOP_K, n), jnp.int32),
            jax.ShapeDtypeStruct((1, nbp), jnp.int32),
            jax.ShapeDtypeStruct((1, LANES), jnp.int32),
            jax.ShapeDtypeStruct((pad_rows, LANES), jnp.int32),
        ],
        compiler_params=_cparams("arbitrary"),
        name="moe_plan",
    )(cnt, eid, rank)


def _sc_mesh():
    return plsc.VectorSubcoreMesh(core_axis_name="c", subcore_axis_name="s")


def _sc_worker():
    return lax.axis_index("c") * SC_SUBCORES + lax.axis_index("s")


def _sc_dispatch(hp, dest_rows, pad_rows, n_rows):
    n = hp[0].shape[0]
    tiles_per_worker = n // SC_WIN // SC_WORKERS
    pad_per_worker = pad_rows.shape[0] // SC_WORKERS
    zeros = jnp.zeros((SC_WIN, LANES), jnp.uint32)

    def body(*refs):
        h = refs[:N_PLANES]
        dest_hbm, pad_hbm, z_hbm = refs[N_PLANES:N_PLANES + 3]
        xs = refs[N_PLANES + 3:2 * N_PLANES + 3]
        bufs = refs[2 * N_PLANES + 3:3 * N_PLANES + 3]
        ibuf, pbuf, sem = refs[3 * N_PLANES + 3:]
        wid = _sc_worker()

        pltpu.sync_copy(z_hbm, bufs[0])
        pltpu.sync_copy(pad_hbm.at[pl.ds(wid * pad_per_worker, pad_per_worker)], pbuf)
        copies = [pltpu.async_copy(bufs[0], xs[c].at[pbuf.at[r]], sem)
                  for r in range(pad_per_worker) for c in range(N_PLANES)]
        for cp in copies:
            cp.wait()

        @pl.loop(0, tiles_per_worker)
        def _(i):
            tile = wid * tiles_per_worker + i
            pltpu.sync_copy(dest_hbm.at[pl.ds(tile * TOP_K, TOP_K)], ibuf)
            for c in range(N_PLANES):
                pltpu.sync_copy(h[c].at[pl.ds(tile * SC_WIN, SC_WIN)], bufs[c])
            scatters = [pltpu.async_copy(bufs[c], xs[c].at[ibuf.at[k]], sem)
                        for c in range(N_PLANES) for k in range(TOP_K)]
            for cp in scatters:
                cp.wait()

    out_type = tuple(jax.ShapeDtypeStruct((n_rows, LANES), jnp.uint32) for _ in range(N_PLANES))
    scratch = ([pltpu.VMEM((SC_WIN, LANES), jnp.uint32)] * N_PLANES
               + [pltpu.VMEM((TOP_K, LANES), jnp.int32), pltpu.VMEM((pad_per_worker, LANES), jnp.int32),
                  pltpu.SemaphoreType.DMA])
    return pl.kernel(body, out_type=out_type, mesh=_sc_mesh(), scratch_types=scratch,
                     name="moe_sc_dispatch")(*hp, dest_rows, pad_rows, zeros)


def _sc_gather(ysp, dest_rows, n):
    tiles_per_worker = n // SC_WIN // SC_WORKERS

    def body(*refs):
        ys = refs[:N_PLANES]
        dest_hbm, g_hbm = refs[N_PLANES:N_PLANES + 2]
        bufs = refs[N_PLANES + 2:N_PLANES + 2 + SC_GATHER_BUFS]
        ibuf, sem = refs[N_PLANES + 2 + SC_GATHER_BUFS:]
        wid = _sc_worker()

        @pl.loop(0, tiles_per_worker)
        def _(i):
            tile = wid * tiles_per_worker + i
            pltpu.sync_copy(dest_hbm.at[pl.ds(tile * TOP_K, TOP_K)], ibuf)
            for c in range(N_PLANES):
                for k0 in range(0, TOP_K, SC_GATHER_BUFS):
                    gathers = [pltpu.async_copy(ys[c].at[ibuf.at[k0 + j]], bufs[j], sem)
                               for j in range(SC_GATHER_BUFS)]
                    for cp in gathers:
                        cp.wait()
                    stores = [pltpu.async_copy(
                        bufs[j], g_hbm.at[pl.ds(((k0 + j) * N_PLANES + c) * n + tile * SC_WIN, SC_WIN)], sem)
                        for j in range(SC_GATHER_BUFS)]
                    for cp in stores:
                        cp.wait()

    scratch = ([pltpu.VMEM((SC_WIN, LANES), jnp.uint32)] * SC_GATHER_BUFS
               + [pltpu.VMEM((TOP_K, LANES), jnp.int32), pltpu.SemaphoreType.DMA])
    return pl.kernel(body, out_type=jax.ShapeDtypeStruct((TOP_K * N_PLANES * n, LANES), jnp.uint32),
                     mesh=_sc_mesh(), scratch_types=scratch, name="moe_sc_gather")(*ysp, dest_rows)


def _expert_kernel(be_ref, nb_ref, *refs):
    xs_refs = refs[:N_PLANES]
    wg_ref, wu_ref, wd_ref = refs[N_PLANES:N_PLANES + 3]
    ys_refs = refs[N_PLANES + 3:2 * N_PLANES + 3]
    wg_b, wu_b, wd_b = refs[2 * N_PLANES + 3:]
    j = pl.program_id(0)
    used = j < nb_ref[0]
    new_expert = (j == 0) | (be_ref[j] != be_ref[jnp.maximum(j - 1, 0)])

    @pl.when(used & new_expert)
    def _():
        wg_b[...] = wg_ref[...].astype(BF16)
        wu_b[...] = wu_ref[...].astype(BF16)
        wd_b[...] = wd_ref[...].astype(BF16)

    @pl.when(used)
    def _():
        x = _load_planes(xs_refs).astype(BF16)
        a = jnp.dot(x, wg_b[...], preferred_element_type=F32)
        u = jnp.dot(x, wu_b[...], preferred_element_type=F32)
        y = jnp.dot((_silu(a) * u).astype(BF16), wd_b[...], preferred_element_type=F32)
        _store_planes(ys_refs, y)

    @pl.when(jnp.logical_not(used))
    def _():
        for ref in ys_refs:
            ref[...] = jnp.zeros_like(ref)


def _experts(be, nb, xsp, wg, wu, wd, layer, n_slots):
    d = wg.shape[2]

    def blk(j, be_ref, nb_ref):
        return jnp.minimum(j, jnp.maximum(nb_ref[0] - 1, 0))

    def wspec(shape):
        return pl.BlockSpec((None, None) + shape,
                            lambda j, be_ref, nb_ref: (layer, be_ref[blk(j, be_ref, nb_ref)], 0, 0))

    grid_spec = pltpu.PrefetchScalarGridSpec(
        num_scalar_prefetch=2,
        grid=(n_slots // MOE_BLOCK,),
        in_specs=[pl.BlockSpec((MOE_BLOCK, LANES), lambda j, be_ref, nb_ref: (blk(j, be_ref, nb_ref), 0))] * N_PLANES
        + [wspec((d, D_EXPERT)), wspec((d, D_EXPERT)), wspec((D_EXPERT, d))],
        out_specs=[pl.BlockSpec((MOE_BLOCK, LANES), lambda j, be_ref, nb_ref: (j, 0))] * N_PLANES,
        scratch_shapes=[pltpu.VMEM((d, D_EXPERT), BF16), pltpu.VMEM((d, D_EXPERT), BF16),
                        pltpu.VMEM((D_EXPERT, d), BF16)],
    )
    return pl.pallas_call(
        _expert_kernel,
        grid_spec=grid_spec,
        out_shape=[jax.ShapeDtypeStruct((n_slots, LANES), jnp.uint32)] * N_PLANES,
        compiler_params=_cparams("arbitrary"),
        name="moe_experts",
    )(be, nb, *xsp, wg, wu, wd)


def _combine_kernel(final, g_ref, wt_ref, h0, h1, h2, h3, sg_ref, su_ref, sd_ref, x_ref, gate_ref, fw_ref, o_ref):
    hb = _load_planes((h0, h1, h2, h3)).astype(BF16)
    a = jnp.dot(hb, sg_ref[...], preferred_element_type=F32)
    u = jnp.dot(hb, su_ref[...], preferred_element_type=F32)
    acc = jnp.dot((_silu(a) * u).astype(BF16), sd_ref[...], preferred_element_type=F32)
    wt = wt_ref[...]
    for k in range(TOP_K):
        rows = _unpack_bf16_pairs(jnp.concatenate([g_ref[k * N_PLANES + c] for c in range(N_PLANES)], axis=1))
        acc = acc + wt[:, k:k + 1] * rows
    xn = x_ref[...] + gate_ref[...] * acc
    if final:
        ms = jnp.mean(xn * xn, axis=-1, keepdims=True)
        xn = xn * lax.rsqrt(ms + EPS) * fw_ref[...]
    o_ref[...] = xn


def _combine(g, wt, hp, sg, su, sd, x2, mod3, fw, seq, final):
    n, d = x2.shape
    tm = TM_COMBINE
    tpb = seq // tm
    return pl.pallas_call(
        functools.partial(_combine_kernel, final),
        grid=(n // tm,),
        in_specs=[
            pl.BlockSpec((TOP_K * N_PLANES, tm, LANES), lambda i: (0, i, 0)),
            pl.BlockSpec((tm, LANES), lambda i: (i, 0)),
        ] + [_plane_spec(tm)] * N_PLANES + [
            pl.BlockSpec(sg.shape, lambda i: (0, 0)),
            pl.BlockSpec(su.shape, lambda i: (0, 0)),
            pl.BlockSpec(sd.shape, lambda i: (0, 0)),
            pl.BlockSpec((tm, d), lambda i: (i, 0)),
            pl.BlockSpec((None, 1, d), lambda i: ((i // tpb) * N_ADA + 5, 0, 0)),
            pl.BlockSpec((1, d), lambda i: (0, 0)),
        ],
        out_specs=pl.BlockSpec((tm, d), lambda i: (i, 0)),
        out_shape=jax.ShapeDtypeStruct((n, d), F32),
        compiler_params=_cparams("arbitrary"),
        name="moe_combine",
    )(g.reshape(TOP_K * N_PLANES, n, LANES), wt, *hp, sg, su, sd, x2, mod3, fw)


def _blockdiag_pairs(w):
    z = jnp.zeros((LRU_BLOCK_W, LRU_BLOCK_W), w.dtype)
    tiles = []
    for j in range(LRU_BLOCKS // 2):
        top = jnp.concatenate([w[2 * j], z], axis=1)
        bot = jnp.concatenate([z, w[2 * j + 1]], axis=1)
        tiles.append(jnp.concatenate([top, bot], axis=0))
    return jnp.stack(tiles).astype(BF16)


def _hg_level_masks():
    ch = HG_CHUNK
    msk = np.zeros((HG_LEVELS, ch, ch), np.float32)
    for lvl in range(HG_LEVELS):
        half = 1 << lvl
        for t in range(ch):
            base = (t // (2 * half)) * (2 * half)
            if (t // half) % 2 == 1:
                msk[lvl, t, base:base + half] = 1.0
    return jnp.asarray(msk)


def _ssd_expand():
    e = np.zeros((LANES, SSD_INNER), np.float32)
    for h in range(SSD_HEADS):
        e[h, h * SSD_HEADDIM:(h + 1) * SSD_HEADDIM] = 1.0
    return jnp.asarray(np.concatenate([e] * 3, axis=0), dtype=BF16)


def _pad_lanes(v, width):
    return jnp.pad(v, (0, width - v.shape[0])).reshape(1, width)


def _layer_params(l, w_in, ssd_conv_w, ssd_conv_b, ssd_dt_bias, ssd_a_log, ssd_d, ssd_norm_w, hg_lower_bounds,
                  hg_norm_w, lru_conv_w, lru_conv_b, lru_wa, lru_ba, lru_wx, lru_bx, lru_lambda, lru_norm_w):
    wi = w_in[l]
    dt0 = SSD_INNER + SSD_INNER + 2 * SSD_GROUPS * SSD_STATE
    w_cat = jnp.concatenate([wi[:, :dt0], wi[:, dt0 + SSD_HEADS:], wi[:, dt0:dt0 + SSD_HEADS]], axis=1)
    w_cat = jnp.pad(w_cat, ((0, 0), (0, U_WIDTH - w_cat.shape[1]))).astype(BF16)
    msk = _hg_level_masks()
    return dict(
        w_cat=w_cat,
        cwx=ssd_conv_w[l][:, :SSD_INNER], cbx=ssd_conv_b[l][:SSD_INNER].reshape(1, -1),
        cwb=ssd_conv_w[l][:, SSD_INNER:], cbb=ssd_conv_b[l][SSD_INNER:].reshape(1, -1),
        dtb=_pad_lanes(ssd_dt_bias[l], LANES), alog=_pad_lanes(ssd_a_log[l], LANES),
        dful=jnp.repeat(ssd_d[l], SSD_HEADDIM).reshape(1, -1), ssd_nw=ssd_norm_w[l].reshape(1, -1),
        expand=_ssd_expand(),
        hg_lb=hg_lower_bounds, hg_nw=hg_norm_w[l].reshape(1, -1), hg_msk=msk,
        lru_cw=lru_conv_w[l], lru_cb=lru_conv_b[l].reshape(1, -1),
        lru_wa=_blockdiag_pairs(lru_wa[l]), lru_ba=lru_ba[l].reshape(1, -1),
        lru_wx=_blockdiag_pairs(lru_wx[l]), lru_bx=lru_bx[l].reshape(1, -1),
        lru_lam=lru_lambda[l].reshape(1, -1), lru_nw=lru_norm_w[l].reshape(1, -1),
    )


def kernel(x, c, ada_w, ada_b, norm_mix_w, norm_ffn_w, w_in, ssd_conv_w, ssd_conv_b, ssd_dt_bias, ssd_a_log, ssd_d, ssd_norm_w, hg_lower_bounds, hg_norm_w, lru_conv_w, lru_conv_b, lru_wa, lru_ba, lru_wx, lru_bx, lru_lambda, lru_norm_w, w_out, router_w, router_bias, exp_gate, exp_up, exp_down, sh_gate, sh_up, sh_down, final_norm_w):
    bsz, seq, d = x.shape
    depth = ada_w.shape[0]
    assert d == D_MODEL and seq % TM_OUTPROJ == 0 and seq % SSD_CHUNK == 0
    n = bsz * seq
    n_slots = n * TOP_K + N_EXPERTS * MOE_BLOCK
    x2 = x.reshape(n, d)
    mod = _adaln(c, ada_w, ada_b)
    fw = final_norm_w.reshape(1, d)
    for l in range(depth):
        p = _layer_params(l, w_in, ssd_conv_w, ssd_conv_b, ssd_dt_bias, ssd_a_log, ssd_d, ssd_norm_w,
                          hg_lower_bounds, hg_norm_w, lru_conv_w, lru_conv_b, lru_wa, lru_ba, lru_wx, lru_bx,
                          lru_lambda, lru_norm_w)
        mod3 = mod[l].reshape(bsz * N_ADA, 1, d)
        u = _inproj(x2, norm_mix_w[l].reshape(1, d), mod3, p["w_cat"], seq)
        y_ssd = _ssd(u, p, bsz, seq)
        y_hg = _hgrn2(u, p, l, bsz, seq)
        y_lru = _lru(u, p, bsz, seq)
        x2, hp = _outproj(y_ssd, y_hg, y_lru, x2, w_out[l].astype(BF16), norm_ffn_w[l].reshape(1, d), mod3, seq)
        eid, rank, wt, cnt = _router(hp, router_w[l].T, router_bias[l].reshape(N_EXPERTS, 1))
        dest, be, nb, pad_rows = _plan(cnt, eid, rank, n_slots)
        dest_rows = dest.reshape(TOP_K, n // LANES, LANES).transpose(1, 0, 2).reshape(n // LANES * TOP_K, LANES)
        xsp = _sc_dispatch(hp, dest_rows, pad_rows, n_slots + LANES)
        ysp = _experts(be.reshape(-1), nb[0, :1], xsp, exp_gate, exp_up, exp_down, l, n_slots)
        g = _sc_gather(ysp, dest_rows, n)
        x2 = _combine(g, wt, hp, sh_gate[l].astype(BF16), sh_up[l].astype(BF16), sh_down[l].astype(BF16),
                      x2, mod3, fw, seq, final=(l == depth - 1))
    return x2.reshape(bsz, seq, d)
```

```python
import functools

import jax
import jax.numpy as jnp
import numpy as np
from jax import lax
from jax.experimental import pallas as pl
from jax.experimental.pallas import tpu as pltpu
from jax.experimental.pallas import tpu_sc as plsc

F32 = jnp.float32
BF16 = jnp.bfloat16
HI = lax.Precision.HIGHEST

LANES = 128
SUBLANES = 8
VMEM_LIMIT_BYTES = 56 * 1024 * 1024

D_MODEL = 1024
EPS = 1e-6
N_ADA = 6
CONV_WIDTH = 4
SSD_INNER = 1024
SSD_HEADDIM = 64
SSD_HEADS = 16
SSD_GROUPS = 2
SSD_STATE = 128
SSD_CHUNK = 128
SSD_GROUP_W = SSD_INNER // SSD_GROUPS
HG_WIDTH = 512
HG_EXPAND = 128
HG_HEADS = 4
HG_CHUNK = 128
HG_LEVELS = 7
LRU_WIDTH = 512
LRU_BLOCKS = 8
LRU_BLOCK_W = 64
LRU_C = 8.0
N_EXPERTS = 64
TOP_K = 8
N_EXPERT_GROUPS = 8
E_PER_GROUP = 8
TOPK_GROUPS = 4
D_EXPERT = 256
ROUTED_SCALE = 2.5
MASK_SCORE = -1.0e4

COL_Z = 0
COL_XS = 1024
COL_BC = 2048
COL_HQ = 2560
COL_HF = 3072
COL_HV = 3584
COL_HG = 4096
COL_LG = 4608
COL_LX = 5120
COL_DT = 5632
U_WIDTH = 5760

TM_INPROJ = 256
TM_OUTPROJ = 512
TM_ROUTER = 512
TM_COMBINE = 256
SC_CORES = 2
SC_SUBCORES = 16
SC_WORKERS = SC_CORES * SC_SUBCORES
SC_WIN = LANES
SC_GATHER_BUFS = 4
MOE_BLOCK_LOG2 = 10
MOE_BLOCK = 1 << MOE_BLOCK_LOG2
R_HG = 256
R_LRU = 256


def _cparams(*sem):
    return pltpu.CompilerParams(dimension_semantics=sem, vmem_limit_bytes=VMEM_LIMIT_BYTES)


def _sigmoid(x):
    return 0.5 * jnp.tanh(0.5 * x) + 0.5


def _silu(x):
    return x * _sigmoid(x)


def _split3(x):
    x1 = x.astype(BF16)
    r1 = x - x1.astype(F32)
    x2 = r1.astype(BF16)
    x3 = (r1 - x2.astype(F32)).astype(BF16)
    return x1, x2, x3


def _softplus(x):
    return jnp.maximum(x, 0.0) + jnp.log1p(jnp.exp(-jnp.abs(x)))


def _norm_mod(x, nw, shift, scale):
    ms = jnp.mean(x * x, axis=-1, keepdims=True)
    y = x * lax.rsqrt(ms + EPS) * nw
    return y * (1.0 + scale) + shift


_HI16 = np.uint32(0xFFFF0000)


def _pack_bf16_pairs(x):
    half = x.shape[1] // 2
    bits = lax.bitcast_convert_type(x.astype(BF16).astype(F32), jnp.uint32)
    return (bits[:, :half] & _HI16) | (bits[:, half:] >> 16)


def _unpack_bf16_pairs(w):
    hi = lax.bitcast_convert_type(w & _HI16, F32)
    lo = lax.bitcast_convert_type(w << 16, F32)
    return jnp.concatenate([hi, lo], axis=1)


N_PLANES = D_MODEL // 2 // LANES


def _store_planes(refs, x):
    packed = _pack_bf16_pairs(x)
    for c, ref in enumerate(refs):
        ref[...] = packed[:, c * LANES:(c + 1) * LANES]


def _load_planes(refs):
    return _unpack_bf16_pairs(jnp.concatenate([ref[...] for ref in refs], axis=1))


def _ada_kernel(c_ref, w_ref, b_ref, o_ref):
    c = c_ref[...]
    o_ref[...] = jnp.dot(_silu(c), w_ref[...], precision=HI, preferred_element_type=F32) + b_ref[...]


def _adaln(c, ada_w, ada_b):
    depth, d, n6 = ada_w.shape
    bsz = c.shape[0]
    tn = 1536
    return pl.pallas_call(
        _ada_kernel,
        grid=(depth, n6 // tn),
        in_specs=[
            pl.BlockSpec((bsz, d), lambda l, j: (0, 0)),
            pl.BlockSpec((None, d, tn), lambda l, j: (l, 0, j)),
            pl.BlockSpec((None, 1, tn), lambda l, j: (l, 0, j)),
        ],
        out_specs=pl.BlockSpec((None, bsz, tn), lambda l, j: (l, 0, j)),
        out_shape=jax.ShapeDtypeStruct((depth, bsz, n6), F32),
        compiler_params=_cparams("arbitrary", "arbitrary"),
        name="adaln_mod",
    )(c, ada_w, ada_b.reshape(depth, 1, n6))


def _inproj_kernel(x_ref, nw_ref, sh_ref, sc_ref, w_ref, o_ref):
    h = _norm_mod(x_ref[...], nw_ref[...], sh_ref[...], sc_ref[...])
    o_ref[...] = jnp.dot(h.astype(BF16), w_ref[...], preferred_element_type=F32)


def _inproj(x2, nw, mod3, w_cat, seq):
    n, d = x2.shape
    tm = TM_INPROJ
    tpb = seq // tm
    return pl.pallas_call(
        _inproj_kernel,
        grid=(n // tm,),
        in_specs=[
            pl.BlockSpec((tm, d), lambda i: (i, 0)),
            pl.BlockSpec((1, d), lambda i: (0, 0)),
            pl.BlockSpec((None, 1, d), lambda i: ((i // tpb) * N_ADA + 0, 0, 0)),
            pl.BlockSpec((None, 1, d), lambda i: ((i // tpb) * N_ADA + 1, 0, 0)),
            pl.BlockSpec((d, U_WIDTH), lambda i: (0, 0), pipeline_mode=pl.Buffered(1)),
        ],
        out_specs=pl.BlockSpec((tm, U_WIDTH), lambda i: (i, 0)),
        out_shape=jax.ShapeDtypeStruct((n, U_WIDTH), F32),
        compiler_params=_cparams("arbitrary"),
        name="inproj",
    )(x2, nw, mod3, mod3, w_cat)


def _causal_conv(cur_ref, ext, tail, cw_ref, cb_ref):
    rows = cur_ref.shape[0]
    ext[0:SUBLANES, :] = tail[...]
    ext[SUBLANES:SUBLANES + rows, :] = cur_ref[...]
    acc = cb_ref[...]
    for w in range(CONV_WIDTH):
        start = SUBLANES - (CONV_WIDTH - 1) + w
        acc = acc + ext[start:start + rows, :] * cw_ref[w:w + 1, :]
    tail[...] = ext[rows:rows + SUBLANES, :]
    return acc


def _ssd_kernel(z_ref, xs_ref, bc_ref, dt_ref, cwx_ref, cbx_ref, cwb_ref, cbb_ref, dtb_ref, alog_ref,
                dful_ref, nw_ref, e_ref, o_ref, extx, extb, tailx, tailb, hstate):
    c = pl.program_id(1)
    q = SSD_CHUNK

    @pl.when(c == 0)
    def _():
        tailx[...] = jnp.zeros_like(tailx)
        tailb[...] = jnp.zeros_like(tailb)
        hstate[...] = jnp.zeros_like(hstate)

    xs = _silu(_causal_conv(xs_ref, extx, tailx, cwx_ref, cbx_ref))
    bc = _silu(_causal_conv(bc_ref, extb, tailb, cwb_ref, cbb_ref))

    dt = _softplus(dt_ref[...] + dtb_ref[...])
    a = dt * (-jnp.exp(alog_ref[...]))
    ri = lax.broadcasted_iota(jnp.int32, (q, q), 0)
    ci = lax.broadcasted_iota(jnp.int32, (q, q), 1)
    tril = ri >= ci
    tril_f = tril.astype(F32)
    tril_b = tril.astype(BF16)
    acum = jnp.dot(jnp.concatenate([tril_b] * 3, axis=1), jnp.concatenate(_split3(a), axis=0),
                   preferred_element_type=F32)
    acum_t = acum.T
    expand3 = e_ref[...]
    dt_full = jnp.dot(jnp.concatenate(_split3(dt), axis=1), expand3, preferred_element_type=F32)
    acum_full = jnp.dot(jnp.concatenate(_split3(acum), axis=1), expand3, preferred_element_type=F32)
    alast_full = acum_full[q - 1:q, :]

    xdt = xs * dt_full
    xdt_b = xdt.astype(BF16)
    exp_a = jnp.exp(acum_full)
    xd_b = (xdt * jnp.exp(alast_full - acum_full)).astype(BF16)
    state_decay = jnp.exp(alast_full)
    left = lax.broadcasted_iota(jnp.int32, (q, LANES), 1) < SSD_HEADDIM
    zero_b = jnp.zeros((q, LANES), BF16)

    ys = []
    for g in range(SSD_GROUPS):
        b_g = bc[:, g * SSD_STATE:(g + 1) * SSD_STATE]
        c_g = bc[:, (SSD_GROUPS + g) * SSD_STATE:(SSD_GROUPS + g + 1) * SSD_STATE]
        c_b = c_g.astype(BF16)
        cb = lax.dot_general(c_b, b_g.astype(BF16), (((1,), (1,)), ((), ())), preferred_element_type=F32)
        cb = cb * tril_f
        cs = slice(g * SSD_GROUP_W, (g + 1) * SSD_GROUP_W)
        h_g = hstate[:, cs]
        y_off = jnp.dot(c_b, h_g.astype(BF16), preferred_element_type=F32) * exp_a[:, cs]
        pieces = []
        for pr in range(SSD_HEADS // SSD_GROUPS // 2):
            h0 = g * (SSD_HEADS // SSD_GROUPS) + 2 * pr
            ms = []
            for h in (h0, h0 + 1):
                col = acum[:, h:h + 1]
                row = acum_t[h:h + 1, :]
                ms.append((cb * jnp.exp(jnp.minimum(col - row, 0.0))).astype(BF16))
            lhs = jnp.concatenate(ms, axis=1)
            xp = xdt_b[:, h0 * SSD_HEADDIM:(h0 + 2) * SSD_HEADDIM]
            rhs = jnp.concatenate([jnp.where(left, xp, zero_b), jnp.where(left, zero_b, xp)], axis=0)
            pieces.append(jnp.dot(lhs, rhs, preferred_element_type=F32))
        ys.append(jnp.concatenate(pieces, axis=1) + y_off)
        b_t = b_g.T.astype(BF16)
        hstate[:, cs] = h_g * state_decay[:, cs] + jnp.dot(b_t, xd_b[:, cs], preferred_element_type=F32)

    y = jnp.concatenate(ys, axis=1) + xs * dful_ref[...]
    y = y * _silu(z_ref[...])
    outs = []
    for g in range(SSD_GROUPS):
        cs = slice(g * SSD_GROUP_W, (g + 1) * SSD_GROUP_W)
        yg = y[:, cs]
        ms = jnp.mean(yg * yg, axis=-1, keepdims=True)
        outs.append(yg * lax.rsqrt(ms + EPS) * nw_ref[:, cs])
    o_ref[...] = jnp.concatenate(outs, axis=1).astype(o_ref.dtype)


def _ssd(u, p, bsz, seq):
    q = SSD_CHUNK
    nc = seq // q
    n = bsz * seq

    def rows(b, c):
        return b * nc + c

    def const(shape):
        return pl.BlockSpec(shape, lambda b, c: (0,) * len(shape))

    return pl.pallas_call(
        _ssd_kernel,
        grid=(bsz, nc),
        in_specs=[
            pl.BlockSpec((q, SSD_INNER), lambda b, c: (rows(b, c), COL_Z // SSD_INNER)),
            pl.BlockSpec((q, SSD_INNER), lambda b, c: (rows(b, c), COL_XS // SSD_INNER)),
            pl.BlockSpec((q, 512), lambda b, c: (rows(b, c), COL_BC // 512)),
            pl.BlockSpec((q, LANES), lambda b, c: (rows(b, c), COL_DT // LANES)),
            const((CONV_WIDTH, SSD_INNER)), const((1, SSD_INNER)),
            const((CONV_WIDTH, 512)), const((1, 512)),
            const((1, LANES)), const((1, LANES)),
            const((1, SSD_INNER)), const((1, SSD_INNER)),
            const((3 * LANES, SSD_INNER)),
        ],
        out_specs=pl.BlockSpec((q, SSD_INNER), lambda b, c: (rows(b, c), 0)),
        out_shape=jax.ShapeDtypeStruct((n, SSD_INNER), BF16),
        scratch_shapes=[
            pltpu.VMEM((q + SUBLANES, SSD_INNER), F32),
            pltpu.VMEM((q + SUBLANES, 512), F32),
            pltpu.VMEM((SUBLANES, SSD_INNER), F32),
            pltpu.VMEM((SUBLANES, 512), F32),
            pltpu.VMEM((SSD_STATE, SSD_INNER), F32),
        ],
        compiler_params=_cparams("arbitrary", "arbitrary"),
        name="ssd_mixer",
    )(u, u, u, u, p["cwx"], p["cbx"], p["cwb"], p["cbb"], p["dtb"], p["alog"], p["dful"], p["ssd_nw"], p["expand"])


def _boundary_rows(b, lvl):
    half = 1 << lvl
    parts = []
    for v in range(b.shape[0] // SUBLANES):
        r0 = v * SUBLANES
        if 2 * half >= SUBLANES:
            src = (r0 // (2 * half)) * (2 * half) + half - 1
            parts.append(jnp.broadcast_to(b[src:src + 1, :], (SUBLANES, b.shape[1])))
        else:
            sub = lax.broadcasted_iota(jnp.int32, (SUBLANES, b.shape[1]), 0)
            piece = None
            for g in range(SUBLANES // (2 * half)):
                src = r0 + g * 2 * half + half - 1
                cand = jnp.broadcast_to(b[src:src + 1, :], (SUBLANES, b.shape[1]))
                piece = cand if piece is None else jnp.where(sub >= g * 2 * half, cand, piece)
            parts.append(piece)
    return jnp.concatenate(parts, axis=0)


def _hgrn2_kernel(layer, q_ref, f_ref, v_ref, g_ref, lb_ref, nw_ref, msk_ref, o_ref, state_t):
    t = pl.program_id(1)
    ch = HG_CHUNK

    @pl.when(t == 0)
    def _():
        state_t[...] = jnp.zeros_like(state_t)

    lrows = [lb_ref[j:j + 1, :] for j in range(lb_ref.shape[0])]
    mx = functools.reduce(jnp.maximum, lrows)
    es = [jnp.exp(r - mx) for r in lrows]
    den = functools.reduce(lambda a_, b_: a_ + b_, es)
    lb = jnp.zeros_like(mx)
    for j in range(1, layer + 1):
        lb = lb + es[j] / den
    one_minus_lb = 1.0 - lb
    nw = nw_ref[...]

    ri = lax.broadcasted_iota(jnp.int32, (ch, ch), 0)
    ci = lax.broadcasted_iota(jnp.int32, (ch, ch), 1)
    tril3 = jnp.concatenate([(ri >= ci).astype(BF16)] * 3, axis=1)
    rowi = lax.broadcasted_iota(jnp.int32, (ch, HG_EXPAND), 0)
    tgt = [((rowi >> lvl) & 1) == 1 for lvl in range(HG_LEVELS)]

    def head_chunk(h, rs):
        cs = slice(h * HG_EXPAND, (h + 1) * HG_EXPAND)
        qq = _silu(q_ref[rs, cs])
        kk = one_minus_lb[:, cs] * _sigmoid(-f_ref[rs, cs])
        logf = jnp.log1p(-kk)
        vv = v_ref[rs, cs]
        vb = vv.astype(BF16)
        b = jnp.dot(tril3, jnp.concatenate(_split3(logf), axis=0),
                    preferred_element_type=F32)
        st = state_t[h]
        o = lax.dot_general((qq * jnp.exp(b)).astype(BF16), st.astype(BF16), (((1,), (1,)), ((), ())),
                            preferred_element_type=F32)
        attn = jnp.zeros((ch, ch), F32)
        for lvl in range(HG_LEVELS):
            if lvl == 0:
                qe = jnp.where(tgt[0], qq * (1.0 - kk), 0.0)
                ke = jnp.where(tgt[0], 0.0, kk)
            else:
                m = _boundary_rows(b, lvl)
                qe = jnp.where(tgt[lvl], qq * jnp.exp(jnp.where(tgt[lvl], b - m, 0.0)), 0.0)
                ke = jnp.where(tgt[lvl], 0.0, kk * jnp.exp(jnp.where(tgt[lvl], 0.0, m - b)))
            prod = lax.dot_general(qe.astype(BF16), ke.astype(BF16), (((1,), (1,)), ((), ())),
                                   preferred_element_type=F32)
            attn = attn + msk_ref[lvl] * prod
        diag = jnp.sum(qq * kk, axis=-1, keepdims=True)
        o = o + jnp.dot(attn.astype(BF16), vb, preferred_element_type=F32) + diag * vv
        b_last = b[ch - 1:ch, :]
        kd = (kk * jnp.exp(b_last - b)).astype(BF16)
        state_t[h] = st * jnp.exp(b_last) + jnp.dot(vv.T.astype(BF16), kd, preferred_element_type=F32)
        ms = jnp.mean(o * o, axis=-1, keepdims=True)
        y = o * lax.rsqrt(ms + EPS) * nw[:, cs]
        o_ref[rs, cs] = (y * _silu(g_ref[rs, cs])).astype(o_ref.dtype)

    def chunk(j, carry):
        rs = pl.ds(pl.multiple_of(j * ch, ch), ch)
        for h in range(HG_HEADS):
            head_chunk(h, rs)
        return carry

    lax.fori_loop(0, q_ref.shape[0] // ch, chunk, 0)


def _hgrn2(u, p, layer, bsz, seq):
    r = R_HG
    nt = seq // r
    n = bsz * seq

    def col(base):
        return lambda b, t: (b * nt + t, base // HG_WIDTH)

    return pl.pallas_call(
        functools.partial(_hgrn2_kernel, layer),
        grid=(bsz, nt),
        in_specs=[
            pl.BlockSpec((r, HG_WIDTH), col(COL_HQ)),
            pl.BlockSpec((r, HG_WIDTH), col(COL_HF)),
            pl.BlockSpec((r, HG_WIDTH), col(COL_HV)),
            pl.BlockSpec((r, HG_WIDTH), col(COL_HG)),
            pl.BlockSpec(p["hg_lb"].shape, lambda b, t: (0, 0)),
            pl.BlockSpec((1, HG_WIDTH), lambda b, t: (0, 0)),
            pl.BlockSpec((HG_LEVELS, HG_CHUNK, HG_CHUNK), lambda b, t: (0, 0, 0)),
        ],
        out_specs=pl.BlockSpec((r, HG_WIDTH), lambda b, t: (b * nt + t, 0)),
        out_shape=jax.ShapeDtypeStruct((n, HG_WIDTH), BF16),
        scratch_shapes=[pltpu.VMEM((HG_HEADS, HG_EXPAND, HG_EXPAND), F32)],
        compiler_params=_cparams("arbitrary", "arbitrary"),
        name="hgrn2_mixer",
    )(u, u, u, u, p["hg_lb"], p["hg_nw"], p["hg_msk"])


def _lru_kernel(g_ref, x_ref, cw_ref, cb_ref, wa_ref, ba_ref, wx_ref, bx_ref, lam_ref, nw_ref, o_ref,
                ext, tail, hcarry):
    t = pl.program_id(1)
    rows = x_ref.shape[0]

    @pl.when(t == 0)
    def _():
        tail[...] = jnp.zeros_like(tail)
        hcarry[...] = jnp.zeros_like(hcarry)

    xb = _causal_conv(x_ref, ext, tail, cw_ref, cb_ref)
    xbb = xb.astype(BF16)
    npair = LRU_WIDTH // LANES
    ra = jnp.concatenate([jnp.dot(xbb[:, j * LANES:(j + 1) * LANES], wa_ref[j], preferred_element_type=F32)
                          for j in range(npair)], axis=1)
    rx = jnp.concatenate([jnp.dot(xbb[:, j * LANES:(j + 1) * LANES], wx_ref[j], preferred_element_type=F32)
                          for j in range(npair)], axis=1)
    r = _sigmoid(ra + ba_ref[...])
    i = _sigmoid(rx + bx_ref[...])
    log_a = -LRU_C * r * _softplus(-lam_ref[...])
    a = jnp.exp(log_a)
    th = jnp.tanh(log_a)
    u = jnp.sqrt(-2.0 * th / (1.0 - th)) * (i * xb)

    rowi = lax.broadcasted_iota(jnp.int32, (rows, LRU_WIDTH), 0)
    acc_a, acc_u = a, u
    d = 1
    while d < rows:
        keep = rowi >= d
        a_sh = jnp.where(keep, pltpu.roll(acc_a, d, 0), 1.0)
        u_sh = jnp.where(keep, pltpu.roll(acc_u, d, 0), 0.0)
        acc_u = acc_a * u_sh + acc_u
        acc_a = acc_a * a_sh
        d *= 2
    h = acc_a * hcarry[0:1, :] + acc_u
    hcarry[0:1, :] = h[rows - 1:rows, :]

    gate = g_ref[...]
    gelu = 0.5 * gate * (1.0 + jnp.tanh(np.sqrt(2.0 / np.pi).astype(np.float32) * (gate + 0.044715 * (gate * gate * gate))))
    y = h * gelu
    ms = jnp.mean(y * y, axis=-1, keepdims=True)
    o_ref[...] = (y * lax.rsqrt(ms + EPS) * nw_ref[...]).astype(o_ref.dtype)


def _lru(u, p, bsz, seq):
    r = R_LRU
    nt = seq // r
    n = bsz * seq
    npair = LRU_WIDTH // LANES

    def const(shape):
        return pl.BlockSpec(shape, lambda b, t: (0,) * len(shape))

    return pl.pallas_call(
        _lru_kernel,
        grid=(bsz, nt),
        in_specs=[
            pl.BlockSpec((r, LRU_WIDTH), lambda b, t: (b * nt + t, COL_LG // LRU_WIDTH)),
            pl.BlockSpec((r, LRU_WIDTH), lambda b, t: (b * nt + t, COL_LX // LRU_WIDTH)),
            const((CONV_WIDTH, LRU_WIDTH)), const((1, LRU_WIDTH)),
            const((npair, LANES, LANES)), const((1, LRU_WIDTH)),
            const((npair, LANES, LANES)), const((1, LRU_WIDTH)),
            const((1, LRU_WIDTH)), const((1, LRU_WIDTH)),
        ],
        out_specs=pl.BlockSpec((r, LRU_WIDTH), lambda b, t: (b * nt + t, 0)),
        out_shape=jax.ShapeDtypeStruct((n, LRU_WIDTH), BF16),
        scratch_shapes=[
            pltpu.VMEM((r + SUBLANES, LRU_WIDTH), F32),
            pltpu.VMEM((SUBLANES, LRU_WIDTH), F32),
            pltpu.VMEM((SUBLANES, LRU_WIDTH), F32),
        ],
        compiler_params=_cparams("arbitrary", "arbitrary"),
        name="rglru_mixer",
    )(u, u, p["lru_cw"], p["lru_cb"], p["lru_wa"], p["lru_ba"], p["lru_wx"], p["lru_bx"], p["lru_lam"], p["lru_nw"])


def _outproj_kernel(ys_ref, yh_ref, yl_ref, x_ref, w_ref, g_ref, nw_ref, sh_ref, sc_ref, xo_ref, *h_refs):
    acc = jnp.dot(ys_ref[...], w_ref[0:SSD_INNER, :], preferred_element_type=F32)
    acc = acc + jnp.dot(yh_ref[...], w_ref[SSD_INNER:SSD_INNER + HG_WIDTH, :], preferred_element_type=F32)
    acc = acc + jnp.dot(yl_ref[...], w_ref[SSD_INNER + HG_WIDTH:, :], preferred_element_type=F32)
    xn = x_ref[...] + g_ref[...] * acc
    xo_ref[...] = xn
    _store_planes(h_refs, _norm_mod(xn, nw_ref[...], sh_ref[...], sc_ref[...]))


def _outproj(y_ssd, y_hg, y_lru, x2, w_out, nw, mod3, seq):
    n, d = x2.shape
    tm = TM_OUTPROJ
    tpb = seq // tm

    def modspec(j):
        return pl.BlockSpec((None, 1, d), lambda i: ((i // tpb) * N_ADA + j, 0, 0))

    outs = pl.pallas_call(
        _outproj_kernel,
        grid=(n // tm,),
        in_specs=[
            pl.BlockSpec((tm, SSD_INNER), lambda i: (i, 0)),
            pl.BlockSpec((tm, HG_WIDTH), lambda i: (i, 0)),
            pl.BlockSpec((tm, LRU_WIDTH), lambda i: (i, 0)),
            pl.BlockSpec((tm, d), lambda i: (i, 0)),
            pl.BlockSpec(w_out.shape, lambda i: (0, 0)),
            modspec(2),
            pl.BlockSpec((1, d), lambda i: (0, 0)),
            modspec(3), modspec(4),
        ],
        out_specs=[pl.BlockSpec((tm, d), lambda i: (i, 0))] + [_plane_spec(tm)] * N_PLANES,
        out_shape=[jax.ShapeDtypeStruct((n, d), F32)] + [jax.ShapeDtypeStruct((n, LANES), jnp.uint32)] * N_PLANES,
        compiler_params=_cparams("arbitrary"),
        name="outproj",
    )(y_ssd, y_hg, y_lru, x2, w_out, mod3, nw, mod3, mod3)
    return outs[0], tuple(outs[1:])


def _plane_spec(tm):
    return pl.BlockSpec((tm, LANES), lambda i: (i, 0))


def _router_kernel(h0, h1, h2, h3, rw_ref, rb_ref, eid_ref, rank_ref, wt_ref, cnt_ref, carry, wscr):
    tm = h0.shape[0]

    @pl.when(pl.program_id(0) == 0)
    def _():
        carry[...] = jnp.zeros_like(carry)

    hb = _load_planes((h0, h1, h2, h3)).astype(BF16)
    logit_t = sum(lax.dot_general(part, hb, (((1,), (1,)), ((), ())), preferred_element_type=F32)
                  for part in _split3(rw_ref[...]))
    score = _sigmoid(logit_t)
    sel = score + rb_ref[...]
    neg_inf = jnp.float32(-jnp.inf)
    io_g = lax.broadcasted_iota(jnp.int32, (E_PER_GROUP, tm), 0)
    blocks, gscore = [], []
    for g in range(N_EXPERT_GROUPS):
        blk = sel[g * E_PER_GROUP:(g + 1) * E_PER_GROUP, :]
        m1 = jnp.max(blk, axis=0, keepdims=True)
        i1 = jnp.min(jnp.where(blk == m1, io_g, E_PER_GROUP), axis=0, keepdims=True)
        m2 = jnp.max(jnp.where(io_g == i1, neg_inf, blk), axis=0, keepdims=True)
        blocks.append(blk)
        gscore.append(m1 + m2)
    masked = []
    for g in range(N_EXPERT_GROUPS):
        rank = jnp.zeros((1, tm), jnp.int32)
        for o in range(N_EXPERT_GROUPS):
            if o == g:
                continue
            beats = (gscore[o] > gscore[g]) | ((gscore[o] == gscore[g]) & (o < g))
            rank = rank + beats.astype(jnp.int32)
        masked.append(jnp.where(rank < TOPK_GROUPS, blocks[g], MASK_SCORE))
    val = jnp.concatenate(masked, axis=0)
    io_e = lax.broadcasted_iota(jnp.int32, (N_EXPERTS, tm), 0)
    chosen = jnp.zeros((N_EXPERTS, tm), jnp.bool_)
    picks = []
    for k in range(TOP_K):
        m = jnp.max(val, axis=0, keepdims=True)
        idx = jnp.min(jnp.where(val == m, io_e, N_EXPERTS), axis=0, keepdims=True)
        pick = io_e == idx
        picks.append(pick)
        eid_ref[k:k + 1, :] = idx
        chosen = chosen | pick
        val = jnp.where(pick, neg_inf, val)
    w = jnp.where(chosen, score, 0.0)
    w = w / jnp.sum(w, axis=0, keepdims=True) * ROUTED_SCALE

    chosen_f = chosen.astype(F32)
    earlier = (lax.broadcasted_iota(jnp.int32, (tm, tm), 0) < lax.broadcasted_iota(jnp.int32, (tm, tm), 1))
    before = jnp.dot(chosen_f.astype(BF16), earlier.astype(BF16), preferred_element_type=F32)
    grank = carry[:, 0:1] + before
    wscr[...] = jnp.zeros_like(wscr)
    for k in range(TOP_K):
        rank_ref[k:k + 1, :] = jnp.sum(jnp.where(picks[k], grank, 0.0), axis=0, keepdims=True).astype(jnp.int32)
        wscr[k:k + 1, :] = jnp.sum(jnp.where(picks[k], w, 0.0), axis=0, keepdims=True)
    wt_ref[...] = wscr[...].T
    carry[...] = carry[...] + jnp.sum(chosen_f, axis=1, keepdims=True)
    cnt_ref[...] = carry[...]


def _router(hp, rw_t, rb):
    n = hp[0].shape[0]
    tm = TM_ROUTER
    return pl.pallas_call(
        _router_kernel,
        grid=(n // tm,),
        in_specs=[_plane_spec(tm)] * N_PLANES + [
            pl.BlockSpec(rw_t.shape, lambda i: (0, 0)),
            pl.BlockSpec((N_EXPERTS, 1), lambda i: (0, 0)),
        ],
        out_specs=[
            pl.BlockSpec((TOP_K, tm), lambda i: (0, i)),
            pl.BlockSpec((TOP_K, tm), lambda i: (0, i)),
            pl.BlockSpec((tm, LANES), lambda i: (i, 0)),
            pl.BlockSpec((N_EXPERTS, LANES), lambda i: (0, 0)),
        ],
        out_shape=[
            jax.ShapeDtypeStruct((TOP_K, n), jnp.int32),
            jax.ShapeDtypeStruct((TOP_K, n), jnp.int32),
            jax.ShapeDtypeStruct((n, LANES), F32),
            jax.ShapeDtypeStruct((N_EXPERTS, LANES), F32),
        ],
        scratch_shapes=[pltpu.VMEM((N_EXPERTS, LANES), F32), pltpu.VMEM((LANES, tm), F32)],
        compiler_params=_cparams("arbitrary"),
        name="router",
    )(*hp, rw_t, rb)


def _plan_kernel(n_slots, cnt_ref, eid_ref, rank_ref, dest_ref, be_ref, nb_ref, pad_ref):
    cnt = cnt_ref[...].astype(jnp.int32)
    padded = ((cnt + (MOE_BLOCK - 1)) >> MOE_BLOCK_LOG2) << MOE_BLOCK_LOG2
    ri = lax.broadcasted_iota(jnp.int32, (N_EXPERTS, N_EXPERTS), 0)
    ci = lax.broadcasted_iota(jnp.int32, (N_EXPERTS, N_EXPERTS), 1)
    pad_end = jnp.dot((ri >= ci).astype(F32), padded.astype(F32), precision=HI,
                      preferred_element_type=F32).astype(jnp.int32)
    pad_start = pad_end - padded
    eid = eid_ref[...]
    dest = rank_ref[...]
    for e in range(N_EXPERTS):
        dest = dest + jnp.where(eid == e, pad_start[e:e + 1, 0:1], 0)
    dest_ref[...] = dest
    nbp = be_ref.shape[1]
    jpos = lax.broadcasted_iota(jnp.int32, (N_EXPERTS, nbp), 1) * MOE_BLOCK
    be = jnp.sum((pad_end[:, 0:1] <= jpos).astype(jnp.int32), axis=0, keepdims=True)
    be_ref[...] = jnp.minimum(be, N_EXPERTS - 1)
    nb_ref[...] = pad_end[N_EXPERTS - 1:N_EXPERTS, :] >> MOE_BLOCK_LOG2
    lane = lax.broadcasted_iota(jnp.int32, (N_EXPERTS, LANES), 1)
    for r in range(MOE_BLOCK // LANES):
        s = pad_start + cnt + (r * LANES + lane)
        pad_ref[r * N_EXPERTS:(r + 1) * N_EXPERTS, :] = jnp.where(s < pad_end, s, n_slots + lane)


def _plan(cnt, eid, rank, n_slots):
    n = eid.shape[1]
    nbp = -(-(n_slots // MOE_BLOCK) // LANES) * LANES
    pad_rows = MOE_BLOCK // LANES * N_EXPERTS
    return pl.pallas_call(
        functools.partial(_plan_kernel, n_slots),
        grid=(1,),
        in_specs=[
            pl.BlockSpec(cnt.shape, lambda i: (0, 0)),
            pl.BlockSpec(eid.shape, lambda i: (0, 0)),
            pl.BlockSpec(rank.shape, lambda i: (0, 0)),
        ],
        out_specs=[
            pl.BlockSpec((TOP_K, n), lambda i: (0, 0)),
            pl.BlockSpec((1, nbp), lambda i: (0, 0)),
            pl.BlockSpec((1, LANES), lambda i: (0, 0)),
            pl.BlockSpec((pad_rows, LANES), lambda i: (0, 0)),
        ],
        out_shape=[
            jax.ShapeDtypeStruct((TOP_K, n), jnp.int32),
            jax.ShapeDtypeStruct((1, nbp), jnp.int32),
            jax.ShapeDtypeStruct((1, LANES), jnp.int32),
            jax.ShapeDtypeStruct((pad_rows, LANES), jnp.int32),
        ],
        compiler_params=_cparams("arbitrary"),
        name="moe_plan",
    )(cnt, eid, rank)


def _sc_mesh():
    return plsc.VectorSubcoreMesh(core_axis_name="c", subcore_axis_name="s")


def _sc_worker():
    return lax.axis_index("c") * SC_SUBCORES + lax.axis_index("s")


def _sc_dispatch(hp, dest_rows, pad_rows, n_rows):
    n = hp[0].shape[0]
    tiles_per_worker = n // SC_WIN // SC_WORKERS
    pad_per_worker = pad_rows.shape[0] // SC_WORKERS
    zeros = jnp.zeros((SC_WIN, LANES), jnp.uint32)

    def body(*refs):
        h = refs[:N_PLANES]
        dest_hbm, pad_hbm, z_hbm = refs[N_PLANES:N_PLANES + 3]
        xs = refs[N_PLANES + 3:2 * N_PLANES + 3]
        bufs = refs[2 * N_PLANES + 3:3 * N_PLANES + 3]
        ibuf, pbuf, sem = refs[3 * N_PLANES + 3:]
        wid = _sc_worker()

        pltpu.sync_copy(z_hbm, bufs[0])
        pltpu.sync_copy(pad_hbm.at[pl.ds(wid * pad_per_worker, pad_per_worker)], pbuf)
        copies = [pltpu.async_copy(bufs[0], xs[c].at[pbuf.at[r]], sem)
                  for r in range(pad_per_worker) for c in range(N_PLANES)]
        for cp in copies:
            cp.wait()

        @pl.loop(0, tiles_per_worker)
        def _(i):
            tile = wid * tiles_per_worker + i
            pltpu.sync_copy(dest_hbm.at[pl.ds(tile * TOP_K, TOP_K)], ibuf)
            for c in range(N_PLANES):
                pltpu.sync_copy(h[c].at[pl.ds(tile * SC_WIN, SC_WIN)], bufs[c])
            scatters = [pltpu.async_copy(bufs[c], xs[c].at[ibuf.at[k]], sem)
                        for c in range(N_PLANES) for k in range(TOP_K)]
            for cp in scatters:
                cp.wait()

    out_type = tuple(jax.ShapeDtypeStruct((n_rows, LANES), jnp.uint32) for _ in range(N_PLANES))
    scratch = ([pltpu.VMEM((SC_WIN, LANES), jnp.uint32)] * N_PLANES
               + [pltpu.VMEM((TOP_K, LANES), jnp.int32), pltpu.VMEM((pad_per_worker, LANES), jnp.int32),
                  pltpu.SemaphoreType.DMA])
    return pl.kernel(body, out_type=out_type, mesh=_sc_mesh(), scratch_types=scratch,
                     name="moe_sc_dispatch")(*hp, dest_rows, pad_rows, zeros)


def _sc_gather(ysp, dest_rows, n):
    tiles_per_worker = n // SC_WIN // SC_WORKERS

    def body(*refs):
        ys = refs[:N_PLANES]
        dest_hbm, g_hbm = refs[N_PLANES:N_PLANES + 2]
        bufs = refs[N_PLANES + 2:N_PLANES + 2 + SC_GATHER_BUFS]
        ibuf, sem = refs[N_PLANES + 2 + SC_GATHER_BUFS:]
        wid = _sc_worker()

        @pl.loop(0, tiles_per_worker)
        def _(i):
            tile = wid * tiles_per_worker + i
            pltpu.sync_copy(dest_hbm.at[pl.ds(tile * TOP_K, TOP_K)], ibuf)
            for c in range(N_PLANES):
                for k0 in range(0, TOP_K, SC_GATHER_BUFS):
                    gathers = [pltpu.async_copy(ys[c].at[ibuf.at[k0 + j]], bufs[j], sem)
                               for j in range(SC_GATHER_BUFS)]
                    for cp in gathers:
                        cp.wait()
                    stores = [pltpu.async_copy(
                        bufs[j], g_hbm.at[pl.ds(((k0 + j) * N_PLANES + c) * n + tile * SC_WIN, SC_WIN)], sem)
                        for j in range(SC_GATHER_BUFS)]
                    for cp in stores:
                        cp.wait()

    scratch = ([pltpu.VMEM((SC_WIN, LANES), jnp.uint32)] * SC_GATHER_BUFS
               + [pltpu.VMEM((TOP_K, LANES), jnp.int32), pltpu.SemaphoreType.DMA])
    return pl.kernel(body, out_type=jax.ShapeDtypeStruct((TOP_K * N_PLANES * n, LANES), jnp.uint32),
                     mesh=_sc_mesh(), scratch_types=scratch, name="moe_sc_gather")(*ysp, dest_rows)


def _expert_kernel(be_ref, nb_ref, *refs):
    xs_refs = refs[:N_PLANES]
    wg_ref, wu_ref, wd_ref = refs[N_PLANES:N_PLANES + 3]
    ys_refs = refs[N_PLANES + 3:2 * N_PLANES + 3]
    wg_b, wu_b, wd_b = refs[2 * N_PLANES + 3:]
    j = pl.program_id(0)
    used = j < nb_ref[0]
    new_expert = (j == 0) | (be_ref[j] != be_ref[jnp.maximum(j - 1, 0)])

    @pl.when(used & new_expert)
    def _():
        wg_b[...] = wg_ref[...].astype(BF16)
        wu_b[...] = wu_ref[...].astype(BF16)
        wd_b[...] = wd_ref[...].astype(BF16)

    @pl.when(used)
    def _():
        x = _load_planes(xs_refs).astype(BF16)
        a = jnp.dot(x, wg_b[...], preferred_element_type=F32)
        u = jnp.dot(x, wu_b[...], preferred_element_type=F32)
        y = jnp.dot((_silu(a) * u).astype(BF16), wd_b[...], preferred_element_type=F32)
        _store_planes(ys_refs, y)

    @pl.when(jnp.logical_not(used))
    def _():
        for ref in ys_refs:
            ref[...] = jnp.zeros_like(ref)


def _experts(be, nb, xsp, wg, wu, wd, layer, n_slots):
    d = wg.shape[2]

    def blk(j, be_ref, nb_ref):
        return jnp.minimum(j, jnp.maximum(nb_ref[0] - 1, 0))

    def wspec(shape):
        return pl.BlockSpec((None, None) + shape,
                            lambda j, be_ref, nb_ref: (layer, be_ref[blk(j, be_ref, nb_ref)], 0, 0))

    grid_spec = pltpu.PrefetchScalarGridSpec(
        num_scalar_prefetch=2,
        grid=(n_slots // MOE_BLOCK,),
        in_specs=[pl.BlockSpec((MOE_BLOCK, LANES), lambda j, be_ref, nb_ref: (blk(j, be_ref, nb_ref), 0))] * N_PLANES
        + [wspec((d, D_EXPERT)), wspec((d, D_EXPERT)), wspec((D_EXPERT, d))],
        out_specs=[pl.BlockSpec((MOE_BLOCK, LANES), lambda j, be_ref, nb_ref: (j, 0))] * N_PLANES,
        scratch_shapes=[pltpu.VMEM((d, D_EXPERT), BF16), pltpu.VMEM((d, D_EXPERT), BF16),
                        pltpu.VMEM((D_EXPERT, d), BF16)],
    )
    return pl.pallas_call(
        _expert_kernel,
        grid_spec=grid_spec,
        out_shape=[jax.ShapeDtypeStruct((n_slots, LANES), jnp.uint32)] * N_PLANES,
        compiler_params=_cparams("arbitrary"),
        name="moe_experts",
    )(be, nb, *xsp, wg, wu, wd)


def _combine_kernel(final, g_ref, wt_ref, h0, h1, h2, h3, sg_ref, su_ref, sd_ref, x_ref, gate_ref, fw_ref, o_ref):
    hb = _load_planes((h0, h1, h2, h3)).astype(BF16)
    a = jnp.dot(hb, sg_ref[...], preferred_element_type=F32)
    u = jnp.dot(hb, su_ref[...], preferred_element_type=F32)
    acc = jnp.dot((_silu(a) * u).astype(BF16), sd_ref[...], preferred_element_type=F32)
    wt = wt_ref[...]
    for k in range(TOP_K):
        rows = _unpack_bf16_pairs(jnp.concatenate([g_ref[k * N_PLANES + c] for c in range(N_PLANES)], axis=1))
        acc = acc + wt[:, k:k + 1] * rows
    xn = x_ref[...] + gate_ref[...] * acc
    if final:
        ms = jnp.mean(xn * xn, axis=-1, keepdims=True)
        xn = xn * lax.rsqrt(ms + EPS) * fw_ref[...]
    o_ref[...] = xn


def _combine(g, wt, hp, sg, su, sd, x2, mod3, fw, seq, final):
    n, d = x2.shape
    tm = TM_COMBINE
    tpb = seq // tm
    return pl.pallas_call(
        functools.partial(_combine_kernel, final),
        grid=(n // tm,),
        in_specs=[
            pl.BlockSpec((TOP_K * N_PLANES, tm, LANES), lambda i: (0, i, 0)),
            pl.BlockSpec((tm, LANES), lambda i: (i, 0)),
        ] + [_plane_spec(tm)] * N_PLANES + [
            pl.BlockSpec(sg.shape, lambda i: (0, 0)),
            pl.BlockSpec(su.shape, lambda i: (0, 0)),
            pl.BlockSpec(sd.shape, lambda i: (0, 0)),
            pl.BlockSpec((tm, d), lambda i: (i, 0)),
            pl.BlockSpec((None, 1, d), lambda i: ((i // tpb) * N_ADA + 5, 0, 0)),
            pl.BlockSpec((1, d), lambda i: (0, 0)),
        ],
        out_specs=pl.BlockSpec((tm, d), lambda i: (i, 0)),
        out_shape=jax.ShapeDtypeStruct((n, d), F32),
        compiler_params=_cparams("arbitrary"),
        name="moe_combine",
    )(g.reshape(TOP_K * N_PLANES, n, LANES), wt, *hp, sg, su, sd, x2, mod3, fw)


def _blockdiag_pairs(w):
    z = jnp.zeros((LRU_BLOCK_W, LRU_BLOCK_W), w.dtype)
    tiles = []
    for j in range(LRU_BLOCKS // 2):
        top = jnp.concatenate([w[2 * j], z], axis=1)
        bot = jnp.concatenate([z, w[2 * j + 1]], axis=1)
        tiles.append(jnp.concatenate([top, bot], axis=0))
    return jnp.stack(tiles).astype(BF16)


def _hg_level_masks():
    ch = HG_CHUNK
    msk = np.zeros((HG_LEVELS, ch, ch), np.float32)
    for lvl in range(HG_LEVELS):
        half = 1 << lvl
        for t in range(ch):
            base = (t // (2 * half)) * (2 * half)
            if (t // half) % 2 == 1:
                msk[lvl, t, base:base + half] = 1.0
    return jnp.asarray(msk)


def _ssd_expand():
    e = np.zeros((LANES, SSD_INNER), np.float32)
    for h in range(SSD_HEADS):
        e[h, h * SSD_HEADDIM:(h + 1) * SSD_HEADDIM] = 1.0
    return jnp.asarray(np.concatenate([e] * 3, axis=0), dtype=BF16)


def _pad_lanes(v, width):
    return jnp.pad(v, (0, width - v.shape[0])).reshape(1, width)


def _layer_params(l, w_in, ssd_conv_w, ssd_conv_b, ssd_dt_bias, ssd_a_log, ssd_d, ssd_norm_w, hg_lower_bounds,
                  hg_norm_w, lru_conv_w, lru_conv_b, lru_wa, lru_ba, lru_wx, lru_bx, lru_lambda, lru_norm_w):
    wi = w_in[l]
    dt0 = SSD_INNER + SSD_INNER + 2 * SSD_GROUPS * SSD_STATE
    w_cat = jnp.concatenate([wi[:, :dt0], wi[:, dt0 + SSD_HEADS:], wi[:, dt0:dt0 + SSD_HEADS]], axis=1)
    w_cat = jnp.pad(w_cat, ((0, 0), (0, U_WIDTH - w_cat.shape[1]))).astype(BF16)
    msk = _hg_level_masks()
    return dict(
        w_cat=w_cat,
        cwx=ssd_conv_w[l][:, :SSD_INNER], cbx=ssd_conv_b[l][:SSD_INNER].reshape(1, -1),
        cwb=ssd_conv_w[l][:, SSD_INNER:], cbb=ssd_conv_b[l][SSD_INNER:].reshape(1, -1),
        dtb=_pad_lanes(ssd_dt_bias[l], LANES), alog=_pad_lanes(ssd_a_log[l], LANES),
        dful=jnp.repeat(ssd_d[l], SSD_HEADDIM).reshape(1, -1), ssd_nw=ssd_norm_w[l].reshape(1, -1),
        expand=_ssd_expand(),
        hg_lb=hg_lower_bounds, hg_nw=hg_norm_w[l].reshape(1, -1), hg_msk=msk,
        lru_cw=lru_conv_w[l], lru_cb=lru_conv_b[l].reshape(1, -1),
        lru_wa=_blockdiag_pairs(lru_wa[l]), lru_ba=lru_ba[l].reshape(1, -1),
        lru_wx=_blockdiag_pairs(lru_wx[l]), lru_bx=lru_bx[l].reshape(1, -1),
        lru_lam=lru_lambda[l].reshape(1, -1), lru_nw=lru_norm_w[l].reshape(1, -1),
    )


def kernel(x, c, ada_w, ada_b, norm_mix_w, norm_ffn_w, w_in, ssd_conv_w, ssd_conv_b, ssd_dt_bias, ssd_a_log, ssd_d, ssd_norm_w, hg_lower_bounds, hg_norm_w, lru_conv_w, lru_conv_b, lru_wa, lru_ba, lru_wx, lru_bx, lru_lambda, lru_norm_w, w_out, router_w, router_bias, exp_gate, exp_up, exp_down, sh_gate, sh_up, sh_down, final_norm_w):
    bsz, seq, d = x.shape
    depth = ada_w.shape[0]
    assert d == D_MODEL and seq % TM_OUTPROJ == 0 and seq % SSD_CHUNK == 0
    n = bsz * seq
    n_slots = n * TOP_K + N_EXPERTS * MOE_BLOCK
    x2 = x.reshape(n, d)
    mod = _adaln(c, ada_w, ada_b)
    fw = final_norm_w.reshape(1, d)
    for l in range(depth):
        p = _layer_params(l, w_in, ssd_conv_w, ssd_conv_b, ssd_dt_bias, ssd_a_log, ssd_d, ssd_norm_w,
                          hg_lower_bounds, hg_norm_w, lru_conv_w, lru_conv_b, lru_wa, lru_ba, lru_wx, lru_bx,
                          lru_lambda, lru_norm_w)
        mod3 = mod[l].reshape(bsz * N_ADA, 1, d)
        u = _inproj(x2, norm_mix_w[l].reshape(1, d), mod3, p["w_cat"], seq)
        y_ssd = _ssd(u, p, bsz, seq)
        y_hg = _hgrn2(u, p, l, bsz, seq)
        y_lru = _lru(u, p, bsz, seq)
        x2, hp = _outproj(y_ssd, y_hg, y_lru, x2, w_out[l].astype(BF16), norm_ffn_w[l].reshape(1, d), mod3, seq)
        eid, rank, wt, cnt = _router(hp, router_w[l].T, router_bias[l].reshape(N_EXPERTS, 1))
        dest, be, nb, pad_rows = _plan(cnt, eid, rank, n_slots)
        dest_rows = dest.reshape(TOP_K, n // LANES, LANES).transpose(1, 0, 2).reshape(n // LANES * TOP_K, LANES)
        xsp = _sc_dispatch(hp, dest_rows, pad_rows, n_slots + LANES)
        ysp = _experts(be.reshape(-1), nb[0, :1], xsp, exp_gate, exp_up, exp_down, l, n_slots)
        g = _sc_gather(ysp, dest_rows, n)
        x2 = _combine(g, wt, hp, sh_gate[l].astype(BF16), sh_up[l].astype(BF16), sh_down[l].astype(BF16),
                      x2, mod3, fw, seq, final=(l == depth - 1))
    return x2.reshape(bsz, seq, d)
```

```python
import functools

import jax
import jax.numpy as jnp
import numpy as np
from jax import lax
from jax.experimental import pallas as pl
from jax.experimental.pallas import tpu as pltpu
from jax.experimental.pallas import tpu_sc as plsc

F32 = jnp.float32
BF16 = jnp.bfloat16
HI = lax.Precision.HIGHEST

LANES = 128
SUBLANES = 8
VMEM_LIMIT_BYTES = 56 * 1024 * 1024

D_MODEL = 1024
EPS = 1e-6
N_ADA = 6
CONV_WIDTH = 4
SSD_INNER = 1024
SSD_HEADDIM = 64
SSD_HEADS = 16
SSD_GROUPS = 2
SSD_STATE = 128
SSD_CHUNK = 128
SSD_GROUP_W = SSD_INNER // SSD_GROUPS
HG_WIDTH = 512
HG_EXPAND = 128
HG_HEADS = 4
HG_CHUNK = 128
HG_LEVELS = 7
LRU_WIDTH = 512
LRU_BLOCKS = 8
LRU_BLOCK_W = 64
LRU_C = 8.0
N_EXPERTS = 64
TOP_K = 8
N_EXPERT_GROUPS = 8
E_PER_GROUP = 8
TOPK_GROUPS = 4
D_EXPERT = 256
ROUTED_SCALE = 2.5
MASK_SCORE = -1.0e4

COL_Z = 0
COL_XS = 1024
COL_BC = 2048
COL_HQ = 2560
COL_HF = 3072
COL_HV = 3584
COL_HG = 4096
COL_LG = 4608
COL_LX = 5120
COL_DT = 5632
U_WIDTH = 5760

TM_INPROJ = 256
TM_OUTPROJ = 512
TM_ROUTER = 512
TM_COMBINE = 256
SC_CORES = 2
SC_SUBCORES = 16
SC_WORKERS = SC_CORES * SC_SUBCORES
SC_WIN = LANES
SC_GATHER_BUFS = 4
BATCH_CHAINS = 2
MOE_BLOCK_LOG2 = 9
MOE_BLOCK = 1 << MOE_BLOCK_LOG2
R_HG = 256
R_LRU = 256


def _cparams(*sem):
    return pltpu.CompilerParams(dimension_semantics=sem, vmem_limit_bytes=VMEM_LIMIT_BYTES)


def _sigmoid(x):
    return 0.5 * jnp.tanh(0.5 * x) + 0.5


def _silu(x):
    return x * _sigmoid(x)


def _split3(x):
    x1 = x.astype(BF16)
    r1 = x - x1.astype(F32)
    x2 = r1.astype(BF16)
    x3 = (r1 - x2.astype(F32)).astype(BF16)
    return x1, x2, x3


def _softplus(x):
    return jnp.maximum(x, 0.0) + jnp.log1p(jnp.exp(-jnp.abs(x)))


def _norm_mod(x, nw, shift, scale):
    ms = jnp.mean(x * x, axis=-1, keepdims=True)
    y = x * lax.rsqrt(ms + EPS) * nw
    return y * (1.0 + scale) + shift


_HI16 = np.uint32(0xFFFF0000)


def _pack_bf16_pairs(x):
    half = x.shape[1] // 2
    bits = lax.bitcast_convert_type(x.astype(BF16).astype(F32), jnp.uint32)
    return (bits[:, :half] & _HI16) | (bits[:, half:] >> 16)


def _unpack_bf16_pairs(w):
    hi = lax.bitcast_convert_type(w & _HI16, F32)
    lo = lax.bitcast_convert_type(w << 16, F32)
    return jnp.concatenate([hi, lo], axis=1)


N_PLANES = D_MODEL // 2 // LANES


def _store_planes(refs, x):
    packed = _pack_bf16_pairs(x)
    for c, ref in enumerate(refs):
        ref[...] = packed[:, c * LANES:(c + 1) * LANES]


def _load_planes(refs):
    return _unpack_bf16_pairs(jnp.concatenate([ref[...] for ref in refs], axis=1))


def _ada_kernel(c_ref, w_ref, b_ref, o_ref):
    c = c_ref[...]
    o_ref[...] = jnp.dot(_silu(c), w_ref[...], precision=HI, preferred_element_type=F32) + b_ref[...]


def _adaln(c, ada_w, ada_b):
    depth, d, n6 = ada_w.shape
    bsz = c.shape[0]
    tn = 1536
    return pl.pallas_call(
        _ada_kernel,
        grid=(depth, n6 // tn),
        in_specs=[
            pl.BlockSpec((bsz, d), lambda l, j: (0, 0)),
            pl.BlockSpec((None, d, tn), lambda l, j: (l, 0, j)),
            pl.BlockSpec((None, 1, tn), lambda l, j: (l, 0, j)),
        ],
        out_specs=pl.BlockSpec((None, bsz, tn), lambda l, j: (l, 0, j)),
        out_shape=jax.ShapeDtypeStruct((depth, bsz, n6), F32),
        compiler_params=_cparams("arbitrary", "arbitrary"),
        name="adaln_mod",
    )(c, ada_w, ada_b.reshape(depth, 1, n6))


def _inproj_kernel(x_ref, nw_ref, sh_ref, sc_ref, w_ref, o_ref):
    h = _norm_mod(x_ref[...], nw_ref[...], sh_ref[...], sc_ref[...])
    o_ref[...] = jnp.dot(h.astype(BF16), w_ref[...], preferred_element_type=F32)


def _inproj(x2, nw, mod3, w_cat, seq):
    n, d = x2.shape
    tm = TM_INPROJ
    tpb = seq // tm
    return pl.pallas_call(
        _inproj_kernel,
        grid=(n // tm,),
        in_specs=[
            pl.BlockSpec((tm, d), lambda i: (i, 0)),
            pl.BlockSpec((1, d), lambda i: (0, 0)),
            pl.BlockSpec((None, 1, d), lambda i: ((i // tpb) * N_ADA + 0, 0, 0)),
            pl.BlockSpec((None, 1, d), lambda i: ((i // tpb) * N_ADA + 1, 0, 0)),
            pl.BlockSpec((d, U_WIDTH), lambda i: (0, 0), pipeline_mode=pl.Buffered(1)),
        ],
        out_specs=pl.BlockSpec((tm, U_WIDTH), lambda i: (i, 0)),
        out_shape=jax.ShapeDtypeStruct((n, U_WIDTH), F32),
        compiler_params=_cparams("arbitrary"),
        name="inproj",
    )(x2, nw, mod3, mod3, w_cat)


def _causal_conv(cur_ref, ext, tail, cw_ref, cb_ref):
    rows = cur_ref.shape[0]
    ext[0:SUBLANES, :] = tail[...]
    ext[SUBLANES:SUBLANES + rows, :] = cur_ref[...]
    acc = cb_ref[...]
    for w in range(CONV_WIDTH):
        start = SUBLANES - (CONV_WIDTH - 1) + w
        acc = acc + ext[start:start + rows, :] * cw_ref[w:w + 1, :]
    tail[...] = ext[rows:rows + SUBLANES, :]
    return acc


def _ssd_kernel(z_ref, xs_ref, bc_ref, dt_ref, cwx_ref, cbx_ref, cwb_ref, cbb_ref, dtb_ref, alog_ref,
                dful_ref, nw_ref, e_ref, o_ref, extx, extb, tailx, tailb, hstate):
    c = pl.program_id(1)
    q = SSD_CHUNK

    @pl.when(c == 0)
    def _():
        tailx[...] = jnp.zeros_like(tailx)
        tailb[...] = jnp.zeros_like(tailb)
        hstate[...] = jnp.zeros_like(hstate)

    xs = _silu(_causal_conv(xs_ref, extx, tailx, cwx_ref, cbx_ref))
    bc = _silu(_causal_conv(bc_ref, extb, tailb, cwb_ref, cbb_ref))

    dt = _softplus(dt_ref[...] + dtb_ref[...])
    a = dt * (-jnp.exp(alog_ref[...]))
    ri = lax.broadcasted_iota(jnp.int32, (q, q), 0)
    ci = lax.broadcasted_iota(jnp.int32, (q, q), 1)
    tril = ri >= ci
    tril_f = tril.astype(F32)
    tril_b = tril.astype(BF16)
    acum = jnp.dot(jnp.concatenate([tril_b] * 3, axis=1), jnp.concatenate(_split3(a), axis=0),
                   preferred_element_type=F32)
    acum_t = acum.T
    expand3 = e_ref[...]
    dt_full = jnp.dot(jnp.concatenate(_split3(dt), axis=1), expand3, preferred_element_type=F32)
    acum_full = jnp.dot(jnp.concatenate(_split3(acum), axis=1), expand3, preferred_element_type=F32)
    alast_full = acum_full[q - 1:q, :]

    xdt = xs * dt_full
    xdt_b = xdt.astype(BF16)
    exp_a = jnp.exp(acum_full)
    xd_b = (xdt * jnp.exp(alast_full - acum_full)).astype(BF16)
    state_decay = jnp.exp(alast_full)
    left = lax.broadcasted_iota(jnp.int32, (q, LANES), 1) < SSD_HEADDIM
    zero_b = jnp.zeros((q, LANES), BF16)

    ys = []
    for g in range(SSD_GROUPS):
        b_g = bc[:, g * SSD_STATE:(g + 1) * SSD_STATE]
        c_g = bc[:, (SSD_GROUPS + g) * SSD_STATE:(SSD_GROUPS + g + 1) * SSD_STATE]
        c_b = c_g.astype(BF16)
        cb = lax.dot_general(c_b, b_g.astype(BF16), (((1,), (1,)), ((), ())), preferred_element_type=F32)
        cb = cb * tril_f
        cs = slice(g * SSD_GROUP_W, (g + 1) * SSD_GROUP_W)
        h_g = hstate[:, cs]
        y_off = jnp.dot(c_b, h_g.astype(BF16), preferred_element_type=F32) * exp_a[:, cs]
        pieces = []
        for pr in range(SSD_HEADS // SSD_GROUPS // 2):
            h0 = g * (SSD_HEADS // SSD_GROUPS) + 2 * pr
            ms = []
            for h in (h0, h0 + 1):
                col = acum[:, h:h + 1]
                row = acum_t[h:h + 1, :]
                ms.append((cb * jnp.exp(jnp.minimum(col - row, 0.0))).astype(BF16))
            lhs = jnp.concatenate(ms, axis=1)
            xp = xdt_b[:, h0 * SSD_HEADDIM:(h0 + 2) * SSD_HEADDIM]
            rhs = jnp.concatenate([jnp.where(left, xp, zero_b), jnp.where(left, zero_b, xp)], axis=0)
            pieces.append(jnp.dot(lhs, rhs, preferred_element_type=F32))
        ys.append(jnp.concatenate(pieces, axis=1) + y_off)
        b_t = b_g.T.astype(BF16)
        hstate[:, cs] = h_g * state_decay[:, cs] + jnp.dot(b_t, xd_b[:, cs], preferred_element_type=F32)

    y = jnp.concatenate(ys, axis=1) + xs * dful_ref[...]
    y = y * _silu(z_ref[...])
    outs = []
    for g in range(SSD_GROUPS):
        cs = slice(g * SSD_GROUP_W, (g + 1) * SSD_GROUP_W)
        yg = y[:, cs]
        ms = jnp.mean(yg * yg, axis=-1, keepdims=True)
        outs.append(yg * lax.rsqrt(ms + EPS) * nw_ref[:, cs])
    o_ref[...] = jnp.concatenate(outs, axis=1).astype(o_ref.dtype)


def _ssd(u, p, bsz, seq):
    q = SSD_CHUNK
    nc = seq // q
    n = bsz * seq

    def rows(b, c):
        return b * nc + c

    def const(shape):
        return pl.BlockSpec(shape, lambda b, c: (0,) * len(shape))

    return pl.pallas_call(
        _ssd_kernel,
        grid=(bsz, nc),
        in_specs=[
            pl.BlockSpec((q, SSD_INNER), lambda b, c: (rows(b, c), COL_Z // SSD_INNER)),
            pl.BlockSpec((q, SSD_INNER), lambda b, c: (rows(b, c), COL_XS // SSD_INNER)),
            pl.BlockSpec((q, 512), lambda b, c: (rows(b, c), COL_BC // 512)),
            pl.BlockSpec((q, LANES), lambda b, c: (rows(b, c), COL_DT // LANES)),
            const((CONV_WIDTH, SSD_INNER)), const((1, SSD_INNER)),
            const((CONV_WIDTH, 512)), const((1, 512)),
            const((1, LANES)), const((1, LANES)),
            const((1, SSD_INNER)), const((1, SSD_INNER)),
            const((3 * LANES, SSD_INNER)),
        ],
        out_specs=pl.BlockSpec((q, SSD_INNER), lambda b, c: (rows(b, c), 0)),
        out_shape=jax.ShapeDtypeStruct((n, SSD_INNER), BF16),
        scratch_shapes=[
            pltpu.VMEM((q + SUBLANES, SSD_INNER), F32),
            pltpu.VMEM((q + SUBLANES, 512), F32),
            pltpu.VMEM((SUBLANES, SSD_INNER), F32),
            pltpu.VMEM((SUBLANES, 512), F32),
            pltpu.VMEM((SSD_STATE, SSD_INNER), F32),
        ],
        compiler_params=_cparams("arbitrary", "arbitrary"),
        name="ssd_mixer",
    )(u, u, u, u, p["cwx"], p["cbx"], p["cwb"], p["cbb"], p["dtb"], p["alog"], p["dful"], p["ssd_nw"], p["expand"])


def _boundary_rows(b, lvl):
    half = 1 << lvl
    parts = []
    for v in range(b.shape[0] // SUBLANES):
        r0 = v * SUBLANES
        if 2 * half >= SUBLANES:
            src = (r0 // (2 * half)) * (2 * half) + half - 1
            parts.append(jnp.broadcast_to(b[src:src + 1, :], (SUBLANES, b.shape[1])))
        else:
            sub = lax.broadcasted_iota(jnp.int32, (SUBLANES, b.shape[1]), 0)
            piece = None
            for g in range(SUBLANES // (2 * half)):
                src = r0 + g * 2 * half + half - 1
                cand = jnp.broadcast_to(b[src:src + 1, :], (SUBLANES, b.shape[1]))
                piece = cand if piece is None else jnp.where(sub >= g * 2 * half, cand, piece)
            parts.append(piece)
    return jnp.concatenate(parts, axis=0)


def _hgrn2_kernel(layer, q_ref, f_ref, v_ref, g_ref, lb_ref, nw_ref, msk_ref, o_ref, state_t):
    t = pl.program_id(1)
    ch = HG_CHUNK

    @pl.when(t == 0)
    def _():
        state_t[...] = jnp.zeros_like(state_t)

    lrows = [lb_ref[j:j + 1, :] for j in range(lb_ref.shape[0])]
    mx = functools.reduce(jnp.maximum, lrows)
    es = [jnp.exp(r - mx) for r in lrows]
    den = functools.reduce(lambda a_, b_: a_ + b_, es)
    lb = jnp.zeros_like(mx)
    for j in range(1, layer + 1):
        lb = lb + es[j] / den
    one_minus_lb = 1.0 - lb
    nw = nw_ref[...]

    ri = lax.broadcasted_iota(jnp.int32, (ch, ch), 0)
    ci = lax.broadcasted_iota(jnp.int32, (ch, ch), 1)
    tril3 = jnp.concatenate([(ri >= ci).astype(BF16)] * 3, axis=1)
    rowi = lax.broadcasted_iota(jnp.int32, (ch, HG_EXPAND), 0)
    tgt = [((rowi >> lvl) & 1) == 1 for lvl in range(HG_LEVELS)]

    def head_chunk(h, rs):
        cs = slice(h * HG_EXPAND, (h + 1) * HG_EXPAND)
        qq = _silu(q_ref[rs, cs])
        kk = one_minus_lb[:, cs] * _sigmoid(-f_ref[rs, cs])
        logf = jnp.log1p(-kk)
        vv = v_ref[rs, cs]
        vb = vv.astype(BF16)
        b = jnp.dot(tril3, jnp.concatenate(_split3(logf), axis=0),
                    preferred_element_type=F32)
        st = state_t[h]
        o = lax.dot_general((qq * jnp.exp(b)).astype(BF16), st.astype(BF16), (((1,), (1,)), ((), ())),
                            preferred_element_type=F32)
        attn = jnp.zeros((ch, ch), F32)
        for lvl in range(HG_LEVELS):
            if lvl == 0:
                qe = jnp.where(tgt[0], qq * (1.0 - kk), 0.0)
                ke = jnp.where(tgt[0], 0.0, kk)
            else:
                m = _boundary_rows(b, lvl)
                qe = jnp.where(tgt[lvl], qq * jnp.exp(jnp.where(tgt[lvl], b - m, 0.0)), 0.0)
                ke = jnp.where(tgt[lvl], 0.0, kk * jnp.exp(jnp.where(tgt[lvl], 0.0, m - b)))
            prod = lax.dot_general(qe.astype(BF16), ke.astype(BF16), (((1,), (1,)), ((), ())),
                                   preferred_element_type=F32)
            attn = attn + msk_ref[lvl] * prod
        diag = jnp.sum(qq * kk, axis=-1, keepdims=True)
        o = o + jnp.dot(attn.astype(BF16), vb, preferred_element_type=F32) + diag * vv
        b_last = b[ch - 1:ch, :]
        kd = (kk * jnp.exp(b_last - b)).astype(BF16)
        state_t[h] = st * jnp.exp(b_last) + jnp.dot(vv.T.astype(BF16), kd, preferred_element_type=F32)
        ms = jnp.mean(o * o, axis=-1, keepdims=True)
        y = o * lax.rsqrt(ms + EPS) * nw[:, cs]
        o_ref[rs, cs] = (y * _silu(g_ref[rs, cs])).astype(o_ref.dtype)

    def chunk(j, carry):
        rs = pl.ds(pl.multiple_of(j * ch, ch), ch)
        for h in range(HG_HEADS):
            head_chunk(h, rs)
        return carry

    lax.fori_loop(0, q_ref.shape[0] // ch, chunk, 0)


def _hgrn2(u, p, layer, bsz, seq):
    r = R_HG
    nt = seq // r
    n = bsz * seq

    def col(base):
        return lambda b, t: (b * nt + t, base // HG_WIDTH)

    return pl.pallas_call(
        functools.partial(_hgrn2_kernel, layer),
        grid=(bsz, nt),
        in_specs=[
            pl.BlockSpec((r, HG_WIDTH), col(COL_HQ)),
            pl.BlockSpec((r, HG_WIDTH), col(COL_HF)),
            pl.BlockSpec((r, HG_WIDTH), col(COL_HV)),
            pl.BlockSpec((r, HG_WIDTH), col(COL_HG)),
            pl.BlockSpec(p["hg_lb"].shape, lambda b, t: (0, 0)),
            pl.BlockSpec((1, HG_WIDTH), lambda b, t: (0, 0)),
            pl.BlockSpec((HG_LEVELS, HG_CHUNK, HG_CHUNK), lambda b, t: (0, 0, 0)),
        ],
        out_specs=pl.BlockSpec((r, HG_WIDTH), lambda b, t: (b * nt + t, 0)),
        out_shape=jax.ShapeDtypeStruct((n, HG_WIDTH), BF16),
        scratch_shapes=[pltpu.VMEM((HG_HEADS, HG_EXPAND, HG_EXPAND), F32)],
        compiler_params=_cparams("arbitrary", "arbitrary"),
        name="hgrn2_mixer",
    )(u, u, u, u, p["hg_lb"], p["hg_nw"], p["hg_msk"])


def _lru_kernel(g_ref, x_ref, cw_ref, cb_ref, wa_ref, ba_ref, wx_ref, bx_ref, lam_ref, nw_ref, o_ref,
                ext, tail, hcarry):
    t = pl.program_id(1)
    rows = x_ref.shape[0]

    @pl.when(t == 0)
    def _():
        tail[...] = jnp.zeros_like(tail)
        hcarry[...] = jnp.zeros_like(hcarry)

    xb = _causal_conv(x_ref, ext, tail, cw_ref, cb_ref)
    xbb = xb.astype(BF16)
    npair = LRU_WIDTH // LANES
    ra = jnp.concatenate([jnp.dot(xbb[:, j * LANES:(j + 1) * LANES], wa_ref[j], preferred_element_type=F32)
                          for j in range(npair)], axis=1)
    rx = jnp.concatenate([jnp.dot(xbb[:, j * LANES:(j + 1) * LANES], wx_ref[j], preferred_element_type=F32)
                          for j in range(npair)], axis=1)
    r = _sigmoid(ra + ba_ref[...])
    i = _sigmoid(rx + bx_ref[...])
    log_a = -LRU_C * r * _softplus(-lam_ref[...])
    a = jnp.exp(log_a)
    th = jnp.tanh(log_a)
    u = jnp.sqrt(-2.0 * th / (1.0 - th)) * (i * xb)

    rowi = lax.broadcasted_iota(jnp.int32, (rows, LRU_WIDTH), 0)
    acc_a, acc_u = a, u
    d = 1
    while d < rows:
        keep = rowi >= d
        a_sh = jnp.where(keep, pltpu.roll(acc_a, d, 0), 1.0)
        u_sh = jnp.where(keep, pltpu.roll(acc_u, d, 0), 0.0)
        acc_u = acc_a * u_sh + acc_u
        acc_a = acc_a * a_sh
        d *= 2
    h = acc_a * hcarry[0:1, :] + acc_u
    hcarry[0:1, :] = h[rows - 1:rows, :]

    gate = g_ref[...]
    gelu = 0.5 * gate * (1.0 + jnp.tanh(np.sqrt(2.0 / np.pi).astype(np.float32) * (gate + 0.044715 * (gate * gate * gate))))
    y = h * gelu
    ms = jnp.mean(y * y, axis=-1, keepdims=True)
    o_ref[...] = (y * lax.rsqrt(ms + EPS) * nw_ref[...]).astype(o_ref.dtype)


def _lru(u, p, bsz, seq):
    r = R_LRU
    nt = seq // r
    n = bsz * seq
    npair = LRU_WIDTH // LANES

    def const(shape):
        return pl.BlockSpec(shape, lambda b, t: (0,) * len(shape))

    return pl.pallas_call(
        _lru_kernel,
        grid=(bsz, nt),
        in_specs=[
            pl.BlockSpec((r, LRU_WIDTH), lambda b, t: (b * nt + t, COL_LG // LRU_WIDTH)),
            pl.BlockSpec((r, LRU_WIDTH), lambda b, t: (b * nt + t, COL_LX // LRU_WIDTH)),
            const((CONV_WIDTH, LRU_WIDTH)), const((1, LRU_WIDTH)),
            const((npair, LANES, LANES)), const((1, LRU_WIDTH)),
            const((npair, LANES, LANES)), const((1, LRU_WIDTH)),
            const((1, LRU_WIDTH)), const((1, LRU_WIDTH)),
        ],
        out_specs=pl.BlockSpec((r, LRU_WIDTH), lambda b, t: (b * nt + t, 0)),
        out_shape=jax.ShapeDtypeStruct((n, LRU_WIDTH), BF16),
        scratch_shapes=[
            pltpu.VMEM((r + SUBLANES, LRU_WIDTH), F32),
            pltpu.VMEM((SUBLANES, LRU_WIDTH), F32),
            pltpu.VMEM((SUBLANES, LRU_WIDTH), F32),
        ],
        compiler_params=_cparams("arbitrary", "arbitrary"),
        name="rglru_mixer",
    )(u, u, p["lru_cw"], p["lru_cb"], p["lru_wa"], p["lru_ba"], p["lru_wx"], p["lru_bx"], p["lru_lam"], p["lru_nw"])


def _outproj_kernel(ys_ref, yh_ref, yl_ref, x_ref, w_ref, g_ref, nw_ref, sh_ref, sc_ref, xo_ref, *h_refs):
    acc = jnp.dot(ys_ref[...], w_ref[0:SSD_INNER, :], preferred_element_type=F32)
    acc = acc + jnp.dot(yh_ref[...], w_ref[SSD_INNER:SSD_INNER + HG_WIDTH, :], preferred_element_type=F32)
    acc = acc + jnp.dot(yl_ref[...], w_ref[SSD_INNER + HG_WIDTH:, :], preferred_element_type=F32)
    xn = x_ref[...] + g_ref[...] * acc
    xo_ref[...] = xn
    _store_planes(h_refs, _norm_mod(xn, nw_ref[...], sh_ref[...], sc_ref[...]))


def _outproj(y_ssd, y_hg, y_lru, x2, w_out, nw, mod3, seq):
    n, d = x2.shape
    tm = TM_OUTPROJ
    tpb = seq // tm

    def modspec(j):
        return pl.BlockSpec((None, 1, d), lambda i: ((i // tpb) * N_ADA + j, 0, 0))

    outs = pl.pallas_call(
        _outproj_kernel,
        grid=(n // tm,),
        in_specs=[
            pl.BlockSpec((tm, SSD_INNER), lambda i: (i, 0)),
            pl.BlockSpec((tm, HG_WIDTH), lambda i: (i, 0)),
            pl.BlockSpec((tm, LRU_WIDTH), lambda i: (i, 0)),
            pl.BlockSpec((tm, d), lambda i: (i, 0)),
            pl.BlockSpec(w_out.shape, lambda i: (0, 0)),
            modspec(2),
            pl.BlockSpec((1, d), lambda i: (0, 0)),
            modspec(3), modspec(4),
        ],
        out_specs=[pl.BlockSpec((tm, d), lambda i: (i, 0))] + [_plane_spec(tm)] * N_PLANES,
        out_shape=[jax.ShapeDtypeStruct((n, d), F32)] + [jax.ShapeDtypeStruct((n, LANES), jnp.uint32)] * N_PLANES,
        compiler_params=_cparams("arbitrary"),
        name="outproj",
    )(y_ssd, y_hg, y_lru, x2, w_out, mod3, nw, mod3, mod3)
    return outs[0], tuple(outs[1:])


def _plane_spec(tm):
    return pl.BlockSpec((tm, LANES), lambda i: (i, 0))


def _router_kernel(h0, h1, h2, h3, rw_ref, rb_ref, eid_ref, rank_ref, wt_ref, cnt_ref, carry, wscr):
    tm = h0.shape[0]

    @pl.when(pl.program_id(0) == 0)
    def _():
        carry[...] = jnp.zeros_like(carry)

    hb = _load_planes((h0, h1, h2, h3)).astype(BF16)
    logit_t = sum(lax.dot_general(part, hb, (((1,), (1,)), ((), ())), preferred_element_type=F32)
                  for part in _split3(rw_ref[...]))
    score = _sigmoid(logit_t)
    sel = score + rb_ref[...]
    neg_inf = jnp.float32(-jnp.inf)
    io_g = lax.broadcasted_iota(jnp.int32, (E_PER_GROUP, tm), 0)
    blocks, gscore = [], []
    for g in range(N_EXPERT_GROUPS):
        blk = sel[g * E_PER_GROUP:(g + 1) * E_PER_GROUP, :]
        m1 = jnp.max(blk, axis=0, keepdims=True)
        i1 = jnp.min(jnp.where(blk == m1, io_g, E_PER_GROUP), axis=0, keepdims=True)
        m2 = jnp.max(jnp.where(io_g == i1, neg_inf, blk), axis=0, keepdims=True)
        blocks.append(blk)
        gscore.append(m1 + m2)
    masked = []
    for g in range(N_EXPERT_GROUPS):
        rank = jnp.zeros((1, tm), jnp.int32)
        for o in range(N_EXPERT_GROUPS):
            if o == g:
                continue
            beats = (gscore[o] > gscore[g]) | ((gscore[o] == gscore[g]) & (o < g))
            rank = rank + beats.astype(jnp.int32)
        masked.append(jnp.where(rank < TOPK_GROUPS, blocks[g], MASK_SCORE))
    val = jnp.concatenate(masked, axis=0)
    io_e = lax.broadcasted_iota(jnp.int32, (N_EXPERTS, tm), 0)
    chosen = jnp.zeros((N_EXPERTS, tm), jnp.bool_)
    picks = []
    for k in range(TOP_K):
        m = jnp.max(val, axis=0, keepdims=True)
        idx = jnp.min(jnp.where(val == m, io_e, N_EXPERTS), axis=0, keepdims=True)
        pick = io_e == idx
        picks.append(pick)
        eid_ref[k:k + 1, :] = idx
        chosen = chosen | pick
        val = jnp.where(pick, neg_inf, val)
    w = jnp.where(chosen, score, 0.0)
    w = w / jnp.sum(w, axis=0, keepdims=True) * ROUTED_SCALE

    chosen_f = chosen.astype(F32)
    earlier = (lax.broadcasted_iota(jnp.int32, (tm, tm), 0) < lax.broadcasted_iota(jnp.int32, (tm, tm), 1))
    before = jnp.dot(chosen_f.astype(BF16), earlier.astype(BF16), preferred_element_type=F32)
    grank = carry[:, 0:1] + before
    wscr[...] = jnp.zeros_like(wscr)
    for k in range(TOP_K):
        rank_ref[k:k + 1, :] = jnp.sum(jnp.where(picks[k], grank, 0.0), axis=0, keepdims=True).astype(jnp.int32)
        wscr[k:k + 1, :] = jnp.sum(jnp.where(picks[k], w, 0.0), axis=0, keepdims=True)
    wt_ref[...] = wscr[...].T
    carry[...] = carry[...] + jnp.sum(chosen_f, axis=1, keepdims=True)
    cnt_ref[...] = carry[...]


def _router(hp, rw_t, rb):
    n = hp[0].shape[0]
    tm = TM_ROUTER
    return pl.pallas_call(
        _router_kernel,
        grid=(n // tm,),
        in_specs=[_plane_spec(tm)] * N_PLANES + [
            pl.BlockSpec(rw_t.shape, lambda i: (0, 0)),
            pl.BlockSpec((N_EXPERTS, 1), lambda i: (0, 0)),
        ],
        out_specs=[
            pl.BlockSpec((TOP_K, tm), lambda i: (0, i)),
            pl.BlockSpec((TOP_K, tm), lambda i: (0, i)),
            pl.BlockSpec((tm, LANES), lambda i: (i, 0)),
            pl.BlockSpec((N_EXPERTS, LANES), lambda i: (0, 0)),
        ],
        out_shape=[
            jax.ShapeDtypeStruct((TOP_K, n), jnp.int32),
            jax.ShapeDtypeStruct((TOP_K, n), jnp.int32),
            jax.ShapeDtypeStruct((n, LANES), F32),
            jax.ShapeDtypeStruct((N_EXPERTS, LANES), F32),
        ],
        scratch_shapes=[pltpu.VMEM((N_EXPERTS, LANES), F32), pltpu.VMEM((LANES, tm), F32)],
        compiler_params=_cparams("arbitrary"),
        name="router",
    )(*hp, rw_t, rb)


def _plan_kernel(n_slots, cnt_ref, eid_ref, rank_ref, dest_ref, be_ref, nb_ref, pad_ref):
    cnt = cnt_ref[...].astype(jnp.int32)
    padded = ((cnt + (MOE_BLOCK - 1)) >> MOE_BLOCK_LOG2) << MOE_BLOCK_LOG2
    ri = lax.broadcasted_iota(jnp.int32, (N_EXPERTS, N_EXPERTS), 0)
    ci = lax.broadcasted_iota(jnp.int32, (N_EXPERTS, N_EXPERTS), 1)
    pad_end = jnp.dot((ri >= ci).astype(F32), padded.astype(F32), precision=HI,
                      preferred_element_type=F32).astype(jnp.int32)
    pad_start = pad_end - padded
    eid = eid_ref[...]
    dest = rank_ref[...]
    for e in range(N_EXPERTS):
        dest = dest + jnp.where(eid == e, pad_start[e:e + 1, 0:1], 0)
    dest_ref[...] = dest
    nbp = be_ref.shape[1]
    jpos = lax.broadcasted_iota(jnp.int32, (N_EXPERTS, nbp), 1) * MOE_BLOCK
    be = jnp.sum((pad_end[:, 0:1] <= jpos).astype(jnp.int32), axis=0, keepdims=True)
    be_ref[...] = jnp.minimum(be, N_EXPERTS - 1)
    nb_ref[...] = pad_end[N_EXPERTS - 1:N_EXPERTS, :] >> MOE_BLOCK_LOG2
    lane = lax.broadcasted_iota(jnp.int32, (N_EXPERTS, LANES), 1)
    for r in range(MOE_BLOCK // LANES):
        s = pad_start + cnt + (r * LANES + lane)
        pad_ref[r * N_EXPERTS:(r + 1) * N_EXPERTS, :] = jnp.where(s < pad_end, s, n_slots + lane)


def _plan(cnt, eid, rank, n_slots):
    n = eid.shape[1]
    nbp = -(-(n_slots // MOE_BLOCK) // LANES) * LANES
    pad_rows = MOE_BLOCK // LANES * N_EXPERTS
    return pl.pallas_call(
        functools.partial(_plan_kernel, n_slots),
        grid=(1,),
        in_specs=[
            pl.BlockSpec(cnt.shape, lambda i: (0, 0)),
            pl.BlockSpec(eid.shape, lambda i: (0, 0)),
            pl.BlockSpec(rank.shape, lambda i: (0, 0)),
        ],
        out_specs=[
            pl.BlockSpec((TOP_K, n), lambda i: (0, 0)),
            pl.BlockSpec((1, nbp), lambda i: (0, 0)),
            pl.BlockSpec((1, LANES), lambda i: (0, 0)),
            pl.BlockSpec((pad_rows, LANES), lambda i: (0, 0)),
        ],
        out_shape=[
            jax.ShapeDtypeStruct((TOP_K, n), jnp.int32),
            jax.ShapeDtypeStruct((1, nbp), jnp.int32),
            jax.ShapeDtypeStruct((1, LANES), jnp.int32),
            jax.ShapeDtypeStruct((pad_rows, LANES), jnp.int32),
        ],
        compiler_params=_cparams("arbitrary"),
        name="moe_plan",
    )(cnt, eid, rank)


def _sc_mesh():
    return plsc.VectorSubcoreMesh(core_axis_name="c", subcore_axis_name="s")


def _sc_worker():
    return lax.axis_index("c") * SC_SUBCORES + lax.axis_index("s")


def _sc_dispatch(hp, dest_rows, pad_rows, n_rows):
    n = hp[0].shape[0]
    tiles_per_worker = n // SC_WIN // SC_WORKERS
    pad_per_worker = pad_rows.shape[0] // SC_WORKERS
    zeros = jnp.zeros((SC_WIN, LANES), jnp.uint32)

    def body(*refs):
        h = refs[:N_PLANES]
        dest_hbm, pad_hbm, z_hbm = refs[N_PLANES:N_PLANES + 3]
        xs = refs[N_PLANES + 3:2 * N_PLANES + 3]
        bufs = refs[2 * N_PLANES + 3:3 * N_PLANES + 3]
        ibuf, pbuf, sem = refs[3 * N_PLANES + 3:]
        wid = _sc_worker()

        pltpu.sync_copy(z_hbm, bufs[0])
        pltpu.sync_copy(pad_hbm.at[pl.ds(wid * pad_per_worker, pad_per_worker)], pbuf)
        copies = [pltpu.async_copy(bufs[0], xs[c].at[pbuf.at[r]], sem)
                  for r in range(pad_per_worker) for c in range(N_PLANES)]
        for cp in copies:
            cp.wait()

        @pl.loop(0, tiles_per_worker)
        def _(i):
            tile = wid * tiles_per_worker + i
            pltpu.sync_copy(dest_hbm.at[pl.ds(tile * TOP_K, TOP_K)], ibuf)
            for c in range(N_PLANES):
                pltpu.sync_copy(h[c].at[pl.ds(tile * SC_WIN, SC_WIN)], bufs[c])
            scatters = [pltpu.async_copy(bufs[c], xs[c].at[ibuf.at[k]], sem)
                        for c in range(N_PLANES) for k in range(TOP_K)]
            for cp in scatters:
                cp.wait()

    out_type = tuple(jax.ShapeDtypeStruct((n_rows, LANES), jnp.uint32) for _ in range(N_PLANES))
    scratch = ([pltpu.VMEM((SC_WIN, LANES), jnp.uint32)] * N_PLANES
               + [pltpu.VMEM((TOP_K, LANES), jnp.int32), pltpu.VMEM((pad_per_worker, LANES), jnp.int32),
                  pltpu.SemaphoreType.DMA])
    return pl.kernel(body, out_type=out_type, mesh=_sc_mesh(), scratch_types=scratch,
                     name="moe_sc_dispatch")(*hp, dest_rows, pad_rows, zeros)


def _sc_gather(ysp, dest_rows, n):
    tiles_per_worker = n // SC_WIN // SC_WORKERS

    def body(*refs):
        ys = refs[:N_PLANES]
        dest_hbm, g_hbm = refs[N_PLANES:N_PLANES + 2]
        bufs = refs[N_PLANES + 2:N_PLANES + 2 + SC_GATHER_BUFS]
        ibuf, sem = refs[N_PLANES + 2 + SC_GATHER_BUFS:]
        wid = _sc_worker()

        @pl.loop(0, tiles_per_worker)
        def _(i):
            tile = wid * tiles_per_worker + i
            pltpu.sync_copy(dest_hbm.at[pl.ds(tile * TOP_K, TOP_K)], ibuf)
            for c in range(N_PLANES):
                for k0 in range(0, TOP_K, SC_GATHER_BUFS):
                    gathers = [pltpu.async_copy(ys[c].at[ibuf.at[k0 + j]], bufs[j], sem)
                               for j in range(SC_GATHER_BUFS)]
                    for cp in gathers:
                        cp.wait()
                    stores = [pltpu.async_copy(
                        bufs[j], g_hbm.at[pl.ds(((k0 + j) * N_PLANES + c) * n + tile * SC_WIN, SC_WIN)], sem)
                        for j in range(SC_GATHER_BUFS)]
                    for cp in stores:
                        cp.wait()

    scratch = ([pltpu.VMEM((SC_WIN, LANES), jnp.uint32)] * SC_GATHER_BUFS
               + [pltpu.VMEM((TOP_K, LANES), jnp.int32), pltpu.SemaphoreType.DMA])
    return pl.kernel(body, out_type=jax.ShapeDtypeStruct((TOP_K * N_PLANES * n, LANES), jnp.uint32),
                     mesh=_sc_mesh(), scratch_types=scratch, name="moe_sc_gather")(*ysp, dest_rows)


def _expert_kernel(be_ref, nb_ref, *refs):
    xs_refs = refs[:N_PLANES]
    wg_ref, wu_ref, wd_ref = refs[N_PLANES:N_PLANES + 3]
    ys_refs = refs[N_PLANES + 3:2 * N_PLANES + 3]
    wg_b, wu_b, wd_b = refs[2 * N_PLANES + 3:]
    j = pl.program_id(0)
    used = j < nb_ref[0]
    new_expert = (j == 0) | (be_ref[j] != be_ref[jnp.maximum(j - 1, 0)])

    @pl.when(used & new_expert)
    def _():
        wg_b[...] = wg_ref[...].astype(BF16)
        wu_b[...] = wu_ref[...].astype(BF16)
        wd_b[...] = wd_ref[...].astype(BF16)

    @pl.when(used)
    def _():
        x = _load_planes(xs_refs).astype(BF16)
        a = jnp.dot(x, wg_b[...], preferred_element_type=F32)
        u = jnp.dot(x, wu_b[...], preferred_element_type=F32)
        y = jnp.dot((_silu(a) * u).astype(BF16), wd_b[...], preferred_element_type=F32)
        _store_planes(ys_refs, y)

    @pl.when(jnp.logical_not(used))
    def _():
        for ref in ys_refs:
            ref[...] = jnp.zeros_like(ref)


def _experts(be, nb, xsp, wg, wu, wd, layer, n_slots):
    d = wg.shape[2]

    def blk(j, be_ref, nb_ref):
        return jnp.minimum(j, jnp.maximum(nb_ref[0] - 1, 0))

    def wspec(shape):
        return pl.BlockSpec((None, None) + shape,
                            lambda j, be_ref, nb_ref: (layer, be_ref[blk(j, be_ref, nb_ref)], 0, 0))

    grid_spec = pltpu.PrefetchScalarGridSpec(
        num_scalar_prefetch=2,
        grid=(n_slots // MOE_BLOCK,),
        in_specs=[pl.BlockSpec((MOE_BLOCK, LANES), lambda j, be_ref, nb_ref: (blk(j, be_ref, nb_ref), 0))] * N_PLANES
        + [wspec((d, D_EXPERT)), wspec((d, D_EXPERT)), wspec((D_EXPERT, d))],
        out_specs=[pl.BlockSpec((MOE_BLOCK, LANES), lambda j, be_ref, nb_ref: (j, 0))] * N_PLANES,
        scratch_shapes=[pltpu.VMEM((d, D_EXPERT), BF16), pltpu.VMEM((d, D_EXPERT), BF16),
                        pltpu.VMEM((D_EXPERT, d), BF16)],
    )
    return pl.pallas_call(
        _expert_kernel,
        grid_spec=grid_spec,
        out_shape=[jax.ShapeDtypeStruct((n_slots, LANES), jnp.uint32)] * N_PLANES,
        compiler_params=_cparams("arbitrary"),
        name="moe_experts",
    )(be, nb, *xsp, wg, wu, wd)


def _combine_kernel(final, g_ref, wt_ref, h0, h1, h2, h3, sg_ref, su_ref, sd_ref, x_ref, gate_ref, fw_ref, o_ref):
    hb = _load_planes((h0, h1, h2, h3)).astype(BF16)
    a = jnp.dot(hb, sg_ref[...], preferred_element_type=F32)
    u = jnp.dot(hb, su_ref[...], preferred_element_type=F32)
    acc = jnp.dot((_silu(a) * u).astype(BF16), sd_ref[...], preferred_element_type=F32)
    wt = wt_ref[...]
    for k in range(TOP_K):
        rows = _unpack_bf16_pairs(jnp.concatenate([g_ref[k * N_PLANES + c] for c in range(N_PLANES)], axis=1))
        acc = acc + wt[:, k:k + 1] * rows
    xn = x_ref[...] + gate_ref[...] * acc
    if final:
        ms = jnp.mean(xn * xn, axis=-1, keepdims=True)
        xn = xn * lax.rsqrt(ms + EPS) * fw_ref[...]
    o_ref[...] = xn


def _combine(g, wt, hp, sg, su, sd, x2, mod3, fw, seq, final):
    n, d = x2.shape
    tm = TM_COMBINE
    tpb = seq // tm
    return pl.pallas_call(
        functools.partial(_combine_kernel, final),
        grid=(n // tm,),
        in_specs=[
            pl.BlockSpec((TOP_K * N_PLANES, tm, LANES), lambda i: (0, i, 0)),
            pl.BlockSpec((tm, LANES), lambda i: (i, 0)),
        ] + [_plane_spec(tm)] * N_PLANES + [
            pl.BlockSpec(sg.shape, lambda i: (0, 0)),
            pl.BlockSpec(su.shape, lambda i: (0, 0)),
            pl.BlockSpec(sd.shape, lambda i: (0, 0)),
            pl.BlockSpec((tm, d), lambda i: (i, 0)),
            pl.BlockSpec((None, 1, d), lambda i: ((i // tpb) * N_ADA + 5, 0, 0)),
            pl.BlockSpec((1, d), lambda i: (0, 0)),
        ],
        out_specs=pl.BlockSpec((tm, d), lambda i: (i, 0)),
        out_shape=jax.ShapeDtypeStruct((n, d), F32),
        compiler_params=_cparams("arbitrary"),
        name="moe_combine",
    )(g.reshape(TOP_K * N_PLANES, n, LANES), wt, *hp, sg, su, sd, x2, mod3, fw)


def _blockdiag_pairs(w):
    z = jnp.zeros((LRU_BLOCK_W, LRU_BLOCK_W), w.dtype)
    tiles = []
    for j in range(LRU_BLOCKS // 2):
        top = jnp.concatenate([w[2 * j], z], axis=1)
        bot = jnp.concatenate([z, w[2 * j + 1]], axis=1)
        tiles.append(jnp.concatenate([top, bot], axis=0))
    return jnp.stack(tiles).astype(BF16)


def _hg_level_masks():
    ch = HG_CHUNK
    msk = np.zeros((HG_LEVELS, ch, ch), np.float32)
    for lvl in range(HG_LEVELS):
        half = 1 << lvl
        for t in range(ch):
            base = (t // (2 * half)) * (2 * half)
            if (t // half) % 2 == 1:
                msk[lvl, t, base:base + half] = 1.0
    return jnp.asarray(msk)


def _ssd_expand():
    e = np.zeros((LANES, SSD_INNER), np.float32)
    for h in range(SSD_HEADS):
        e[h, h * SSD_HEADDIM:(h + 1) * SSD_HEADDIM] = 1.0
    return jnp.asarray(np.concatenate([e] * 3, axis=0), dtype=BF16)


def _pad_lanes(v, width):
    return jnp.pad(v, (0, width - v.shape[0])).reshape(1, width)


def _layer_params(l, w_in, ssd_conv_w, ssd_conv_b, ssd_dt_bias, ssd_a_log, ssd_d, ssd_norm_w, hg_lower_bounds,
                  hg_norm_w, lru_conv_w, lru_conv_b, lru_wa, lru_ba, lru_wx, lru_bx, lru_lambda, lru_norm_w):
    wi = w_in[l]
    dt0 = SSD_INNER + SSD_INNER + 2 * SSD_GROUPS * SSD_STATE
    w_cat = jnp.concatenate([wi[:, :dt0], wi[:, dt0 + SSD_HEADS:], wi[:, dt0:dt0 + SSD_HEADS]], axis=1)
    w_cat = jnp.pad(w_cat, ((0, 0), (0, U_WIDTH - w_cat.shape[1]))).astype(BF16)
    msk = _hg_level_masks()
    return dict(
        w_cat=w_cat,
        cwx=ssd_conv_w[l][:, :SSD_INNER], cbx=ssd_conv_b[l][:SSD_INNER].reshape(1, -1),
        cwb=ssd_conv_w[l][:, SSD_INNER:], cbb=ssd_conv_b[l][SSD_INNER:].reshape(1, -1),
        dtb=_pad_lanes(ssd_dt_bias[l], LANES), alog=_pad_lanes(ssd_a_log[l], LANES),
        dful=jnp.repeat(ssd_d[l], SSD_HEADDIM).reshape(1, -1), ssd_nw=ssd_norm_w[l].reshape(1, -1),
        expand=_ssd_expand(),
        hg_lb=hg_lower_bounds, hg_nw=hg_norm_w[l].reshape(1, -1), hg_msk=msk,
        lru_cw=lru_conv_w[l], lru_cb=lru_conv_b[l].reshape(1, -1),
        lru_wa=_blockdiag_pairs(lru_wa[l]), lru_ba=lru_ba[l].reshape(1, -1),
        lru_wx=_blockdiag_pairs(lru_wx[l]), lru_bx=lru_bx[l].reshape(1, -1),
        lru_lam=lru_lambda[l].reshape(1, -1), lru_nw=lru_norm_w[l].reshape(1, -1),
    )


def kernel(x, c, ada_w, ada_b, norm_mix_w, norm_ffn_w, w_in, ssd_conv_w, ssd_conv_b, ssd_dt_bias, ssd_a_log, ssd_d, ssd_norm_w, hg_lower_bounds, hg_norm_w, lru_conv_w, lru_conv_b, lru_wa, lru_ba, lru_wx, lru_bx, lru_lambda, lru_norm_w, w_out, router_w, router_bias, exp_gate, exp_up, exp_down, sh_gate, sh_up, sh_down, final_norm_w):
    bsz_all, seq, d = x.shape
    depth = ada_w.shape[0]
    assert d == D_MODEL and seq % TM_OUTPROJ == 0 and seq % SSD_CHUNK == 0 and bsz_all % BATCH_CHAINS == 0
    bsz = bsz_all // BATCH_CHAINS
    n = bsz * seq
    n_slots = n * TOP_K + N_EXPERTS * MOE_BLOCK
    fw = final_norm_w.reshape(1, d)
    mod = _adaln(c, ada_w, ada_b)
    xs2 = [x[j * bsz:(j + 1) * bsz].reshape(n, d) for j in range(BATCH_CHAINS)]
    for l in range(depth):
        p = _layer_params(l, w_in, ssd_conv_w, ssd_conv_b, ssd_dt_bias, ssd_a_log, ssd_d, ssd_norm_w,
                          hg_lower_bounds, hg_norm_w, lru_conv_w, lru_conv_b, lru_wa, lru_ba, lru_wx, lru_bx,
                          lru_lambda, lru_norm_w)
        w_out_b = w_out[l].astype(BF16)
        shared = (sh_gate[l].astype(BF16), sh_up[l].astype(BF16), sh_down[l].astype(BF16))
        mods = [mod[l, j * bsz:(j + 1) * bsz].reshape(bsz * N_ADA, 1, d) for j in range(BATCH_CHAINS)]
        routed = []
        for j in range(BATCH_CHAINS):
            u = _inproj(xs2[j], norm_mix_w[l].reshape(1, d), mods[j], p["w_cat"], seq)
            y_ssd = _ssd(u, p, bsz, seq)
            y_hg = _hgrn2(u, p, l, bsz, seq)
            y_lru = _lru(u, p, bsz, seq)
            xs2[j], hp = _outproj(y_ssd, y_hg, y_lru, xs2[j], w_out_b, norm_ffn_w[l].reshape(1, d), mods[j], seq)
            eid, rank, wt, cnt = _router(hp, router_w[l].T, router_bias[l].reshape(N_EXPERTS, 1))
            dest, be, nb, pad_rows = _plan(cnt, eid, rank, n_slots)
            dest_rows = dest.reshape(TOP_K, n // LANES, LANES).transpose(1, 0, 2).reshape(n // LANES * TOP_K, LANES)
            xsp = _sc_dispatch(hp, dest_rows, pad_rows, n_slots + LANES)
            routed.append((hp, wt, be, nb, dest_rows, xsp))
        gathered = []
        for j in range(BATCH_CHAINS):
            hp, wt, be, nb, dest_rows, xsp = routed[j]
            ysp = _experts(be.reshape(-1), nb[0, :1], xsp, exp_gate, exp_up, exp_down, l, n_slots)
            gathered.append(_sc_gather(ysp, dest_rows, n))
        for j in range(BATCH_CHAINS):
            hp, wt = routed[j][:2]
            xs2[j] = _combine(gathered[j], wt, hp, *shared, xs2[j], mods[j], fw, seq, final=(l == depth - 1))
    return jnp.concatenate(xs2, axis=0).reshape(bsz_all, seq, d)
```

```python
import functools

import jax
import jax.numpy as jnp
import numpy as np
from jax import lax
from jax.experimental import pallas as pl
from jax.experimental.pallas import tpu as pltpu
from jax.experimental.pallas import tpu_sc as plsc

F32 = jnp.float32
BF16 = jnp.bfloat16
HI = lax.Precision.HIGHEST
F32_TINY = float(np.finfo(np.float32).tiny)
LOG2E = float(np.log2(np.e))

LANES = 128
SUBLANES = 8
VMEM_LIMIT_BYTES = 56 * 1024 * 1024

D_MODEL = 1024
EPS = 1e-6
N_ADA = 6
CONV_WIDTH = 4
SSD_INNER = 1024
SSD_HEADDIM = 64
SSD_HEADS = 16
SSD_GROUPS = 2
SSD_STATE = 128
SSD_CHUNK = 128
SSD_GROUP_W = SSD_INNER // SSD_GROUPS
HG_WIDTH = 512
HG_EXPAND = 128
HG_HEADS = 4
HG_CHUNK = 128
HG_LEVELS = 7
LRU_WIDTH = 512
LRU_BLOCKS = 8
LRU_BLOCK_W = 64
LRU_C = 8.0
N_EXPERTS = 64
TOP_K = 8
N_EXPERT_GROUPS = 8
E_PER_GROUP = 8
TOPK_GROUPS = 4
D_EXPERT = 256
ROUTED_SCALE = 2.5
MASK_SCORE = -1.0e4

COL_Z = 0
COL_XS = 1024
COL_BC = 2048
COL_HQ = 2560
COL_HF = 3072
COL_HV = 3584
COL_HG = 4096
COL_LG = 4608
COL_LX = 5120
COL_DT = 5632
U_WIDTH = 5760

TM_INPROJ = 256
TM_OUTPROJ = 512
TM_ROUTER = 512
TM_COMBINE = 256
SC_CORES = 2
SC_SUBCORES = 16
SC_WORKERS = SC_CORES * SC_SUBCORES
SC_WIN = LANES
SC_GATHER_BUFS = 4
MOE_BLOCK_LOG2 = 10
MOE_BLOCK = 1 << MOE_BLOCK_LOG2
R_HG = 256


def _cparams(*sem):
    return pltpu.CompilerParams(dimension_semantics=sem, vmem_limit_bytes=VMEM_LIMIT_BYTES)


def _sigmoid(x):
    return 0.5 * jnp.tanh(0.5 * x) + 0.5


def _silu(x):
    return x * _sigmoid(x)


def _split3(x):
    x1 = x.astype(BF16)
    r1 = x - x1.astype(F32)
    x2 = r1.astype(BF16)
    x3 = (r1 - x2.astype(F32)).astype(BF16)
    return x1, x2, x3


def _softplus(x):
    return jnp.maximum(x, 0.0) + jnp.log1p(jnp.exp(-jnp.abs(x)))


def _norm_mod(x, nw, shift, scale):
    ms = jnp.mean(x * x, axis=-1, keepdims=True)
    y = x * lax.rsqrt(ms + EPS) * nw
    return y * (1.0 + scale) + shift


_HI16 = np.uint32(0xFFFF0000)


def _pack_bf16_pairs(x):
    half = x.shape[1] // 2
    bits = lax.bitcast_convert_type(x.astype(BF16).astype(F32), jnp.uint32)
    return (bits[:, :half] & _HI16) | (bits[:, half:] >> 16)


def _unpack_bf16_pairs(w):
    hi = lax.bitcast_convert_type(w & _HI16, F32)
    lo = lax.bitcast_convert_type(w << 16, F32)
    return jnp.concatenate([hi, lo], axis=1)


N_PLANES = D_MODEL // 2 // LANES


def _store_planes(refs, x):
    packed = _pack_bf16_pairs(x)
    for c, ref in enumerate(refs):
        ref[...] = packed[:, c * LANES:(c + 1) * LANES]


def _load_planes(refs):
    return _unpack_bf16_pairs(jnp.concatenate([ref[...] for ref in refs], axis=1))


def _ada_kernel(c_ref, w_ref, b_ref, o_ref):
    c = c_ref[...]
    o_ref[...] = jnp.dot(_silu(c), w_ref[...], precision=HI, preferred_element_type=F32) + b_ref[...]


def _adaln(c, ada_w, ada_b):
    depth, d, n6 = ada_w.shape
    bsz = c.shape[0]
    tn = 1536
    return pl.pallas_call(
        _ada_kernel,
        grid=(depth, n6 // tn),
        in_specs=[
            pl.BlockSpec((bsz, d), lambda l, j: (0, 0)),
            pl.BlockSpec((None, d, tn), lambda l, j: (l, 0, j)),
            pl.BlockSpec((None, 1, tn), lambda l, j: (l, 0, j)),
        ],
        out_specs=pl.BlockSpec((None, bsz, tn), lambda l, j: (l, 0, j)),
        out_shape=jax.ShapeDtypeStruct((depth, bsz, n6), F32),
        compiler_params=_cparams("arbitrary", "arbitrary"),
        name="adaln_mod",
    )(c, ada_w, ada_b.reshape(depth, 1, n6))


def _inproj_kernel(x_ref, nw_ref, sh_ref, sc_ref, w_ref, o_ref):
    h = _norm_mod(x_ref[...], nw_ref[...], sh_ref[...], sc_ref[...])
    o_ref[...] = jnp.dot(h.astype(BF16), w_ref[...], preferred_element_type=F32)


def _inproj(x2, nw, mod3, w_cat, seq):
    n, d = x2.shape
    tm = TM_INPROJ
    tpb = seq // tm
    return pl.pallas_call(
        _inproj_kernel,
        grid=(n // tm,),
        in_specs=[
            pl.BlockSpec((tm, d), lambda i: (i, 0)),
            pl.BlockSpec((1, d), lambda i: (0, 0)),
            pl.BlockSpec((None, 1, d), lambda i: ((i // tpb) * N_ADA + 0, 0, 0)),
            pl.BlockSpec((None, 1, d), lambda i: ((i // tpb) * N_ADA + 1, 0, 0)),
            pl.BlockSpec((d, U_WIDTH), lambda i: (0, 0), pipeline_mode=pl.Buffered(1)),
        ],
        out_specs=pl.BlockSpec((tm, U_WIDTH), lambda i: (i, 0)),
        out_shape=jax.ShapeDtypeStruct((n, U_WIDTH), F32),
        compiler_params=_cparams("arbitrary"),
        name="inproj",
    )(x2, nw, mod3, mod3, w_cat)


def _causal_conv(cur_ref, ext, tail, cw_ref, cb_ref):
    rows = cur_ref.shape[0]
    ext[0:SUBLANES, :] = tail[...]
    ext[SUBLANES:SUBLANES + rows, :] = cur_ref[...]
    acc = cb_ref[...]
    for w in range(CONV_WIDTH):
        start = SUBLANES - (CONV_WIDTH - 1) + w
        acc = acc + ext[start:start + rows, :] * cw_ref[w:w + 1, :]
    tail[...] = ext[rows:rows + SUBLANES, :]
    return acc


def _ssd_kernel(z_ref, xs_ref, bc_ref, dt_ref, cwx_ref, cbx_ref, cwb_ref, cbb_ref, dtb_ref, alog_ref,
                dful_ref, nw_ref, e_ref, o_ref, extx, extb, tailx, tailb, hstate):
    c = pl.program_id(1)
    q = SSD_CHUNK

    @pl.when(c == 0)
    def _():
        tailx[...] = jnp.zeros_like(tailx)
        tailb[...] = jnp.zeros_like(tailb)
        hstate[...] = jnp.zeros_like(hstate)

    xs = _silu(_causal_conv(xs_ref, extx, tailx, cwx_ref, cbx_ref))
    bc = _silu(_causal_conv(bc_ref, extb, tailb, cwb_ref, cbb_ref))

    dt = _softplus(dt_ref[...] + dtb_ref[...])
    a = dt * (-jnp.exp(alog_ref[...]))
    ri = lax.broadcasted_iota(jnp.int32, (q, q), 0)
    ci = lax.broadcasted_iota(jnp.int32, (q, q), 1)
    tril = ri >= ci
    tril_f = tril.astype(F32)
    tril_b = tril.astype(BF16)
    acum = jnp.dot(jnp.concatenate([tril_b] * 3, axis=1), jnp.concatenate(_split3(a), axis=0),
                   preferred_element_type=F32)
    acum_t = acum.T
    expand3 = e_ref[...]
    dt_full = jnp.dot(jnp.concatenate(_split3(dt), axis=1), expand3, preferred_element_type=F32)
    acum_full = jnp.dot(jnp.concatenate(_split3(acum), axis=1), expand3, preferred_element_type=F32)
    alast_full = acum_full[q - 1:q, :]

    xdt = xs * dt_full
    xdt_b = xdt.astype(BF16)
    exp_a = jnp.exp(acum_full)
    xd_b = (xdt * jnp.exp(alast_full - acum_full)).astype(BF16)
    state_decay = jnp.exp(alast_full)
    left = lax.broadcasted_iota(jnp.int32, (q, LANES), 1) < SSD_HEADDIM
    zero_b = jnp.zeros((q, LANES), BF16)

    ys = []
    for g in range(SSD_GROUPS):
        b_g = bc[:, g * SSD_STATE:(g + 1) * SSD_STATE]
        c_g = bc[:, (SSD_GROUPS + g) * SSD_STATE:(SSD_GROUPS + g + 1) * SSD_STATE]
        c_b = c_g.astype(BF16)
        cb = lax.dot_general(c_b, b_g.astype(BF16), (((1,), (1,)), ((), ())), preferred_element_type=F32)
        cb = cb * tril_f
        cs = slice(g * SSD_GROUP_W, (g + 1) * SSD_GROUP_W)
        h_g = hstate[:, cs]
        y_off = jnp.dot(c_b, h_g.astype(BF16), preferred_element_type=F32) * exp_a[:, cs]
        pieces = []
        for pr in range(SSD_HEADS // SSD_GROUPS // 2):
            h0 = g * (SSD_HEADS // SSD_GROUPS) + 2 * pr
            ms = []
            for h in (h0, h0 + 1):
                col = acum[:, h:h + 1]
                row = acum_t[h:h + 1, :]
                ms.append((cb * jnp.exp(jnp.minimum(col - row, 0.0))).astype(BF16))
            lhs = jnp.concatenate(ms, axis=1)
            xp = xdt_b[:, h0 * SSD_HEADDIM:(h0 + 2) * SSD_HEADDIM]
            rhs = jnp.concatenate([jnp.where(left, xp, zero_b), jnp.where(left, zero_b, xp)], axis=0)
            pieces.append(jnp.dot(lhs, rhs, preferred_element_type=F32))
        ys.append(jnp.concatenate(pieces, axis=1) + y_off)
        b_t = b_g.T.astype(BF16)
        hstate[:, cs] = h_g * state_decay[:, cs] + jnp.dot(b_t, xd_b[:, cs], preferred_element_type=F32)

    y = jnp.concatenate(ys, axis=1) + xs * dful_ref[...]
    y = y * _silu(z_ref[...])
    outs = []
    for g in range(SSD_GROUPS):
        cs = slice(g * SSD_GROUP_W, (g + 1) * SSD_GROUP_W)
        yg = y[:, cs]
        ms = jnp.mean(yg * yg, axis=-1, keepdims=True)
        outs.append(yg * lax.rsqrt(ms + EPS) * nw_ref[:, cs])
    o_ref[...] = jnp.concatenate(outs, axis=1).astype(o_ref.dtype)


def _ssd(u, p, bsz, seq):
    q = SSD_CHUNK
    nc = seq // q
    n = bsz * seq

    def rows(b, c):
        return b * nc + c

    def const(shape):
        return pl.BlockSpec(shape, lambda b, c: (0,) * len(shape))

    return pl.pallas_call(
        _ssd_kernel,
        grid=(bsz, nc),
        in_specs=[
            pl.BlockSpec((q, SSD_INNER), lambda b, c: (rows(b, c), COL_Z // SSD_INNER)),
            pl.BlockSpec((q, SSD_INNER), lambda b, c: (rows(b, c), COL_XS // SSD_INNER)),
            pl.BlockSpec((q, 512), lambda b, c: (rows(b, c), COL_BC // 512)),
            pl.BlockSpec((q, LANES), lambda b, c: (rows(b, c), COL_DT // LANES)),
            const((CONV_WIDTH, SSD_INNER)), const((1, SSD_INNER)),
            const((CONV_WIDTH, 512)), const((1, 512)),
            const((1, LANES)), const((1, LANES)),
            const((1, SSD_INNER)), const((1, SSD_INNER)),
            const((3 * LANES, SSD_INNER)),
        ],
        out_specs=pl.BlockSpec((q, SSD_INNER), lambda b, c: (rows(b, c), 0)),
        out_shape=jax.ShapeDtypeStruct((n, SSD_INNER), BF16),
        scratch_shapes=[
            pltpu.VMEM((q + SUBLANES, SSD_INNER), F32),
            pltpu.VMEM((q + SUBLANES, 512), F32),
            pltpu.VMEM((SUBLANES, SSD_INNER), F32),
            pltpu.VMEM((SUBLANES, 512), F32),
            pltpu.VMEM((SSD_STATE, SSD_INNER), F32),
        ],
        compiler_params=_cparams("arbitrary", "arbitrary"),
        name="ssd_mixer",
    )(u, u, u, u, p["cwx"], p["cbx"], p["cwb"], p["cbb"], p["dtb"], p["alog"], p["dful"], p["ssd_nw"], p["expand"])


def _boundary_rows(b, lvl):
    half = 1 << lvl
    parts = []
    for v in range(b.shape[0] // SUBLANES):
        r0 = v * SUBLANES
        if 2 * half >= SUBLANES:
            src = (r0 // (2 * half)) * (2 * half) + half - 1
            parts.append(jnp.broadcast_to(b[src:src + 1, :], (SUBLANES, b.shape[1])))
        else:
            sub = lax.broadcasted_iota(jnp.int32, (SUBLANES, b.shape[1]), 0)
            piece = None
            for g in range(SUBLANES // (2 * half)):
                src = r0 + g * 2 * half + half - 1
                cand = jnp.broadcast_to(b[src:src + 1, :], (SUBLANES, b.shape[1]))
                piece = cand if piece is None else jnp.where(sub >= g * 2 * half, cand, piece)
            parts.append(piece)
    return jnp.concatenate(parts, axis=0)


def _hgrn2_lru_kernel(layer, q_ref, f_ref, v_ref, g_ref, lb_ref, nw_ref, msk_ref,
                      lg_ref, lx_ref, cw_ref, cb_ref, wa_ref, ba_ref, wx_ref, bx_ref, lam_ref, lnw_ref,
                      o_ref, ol_ref, state_t, ext, tail, hcarry):
    t = pl.program_id(1)
    ch = HG_CHUNK

    @pl.when(t == 0)
    def _():
        state_t[...] = jnp.zeros_like(state_t)
        tail[...] = jnp.zeros_like(tail)
        hcarry[...] = jnp.zeros_like(hcarry)

    _lru_tile(lg_ref, lx_ref, cw_ref, cb_ref, wa_ref, ba_ref, wx_ref, bx_ref, lam_ref, lnw_ref, ol_ref,
              ext, tail, hcarry)

    lrows = [lb_ref[j:j + 1, :] for j in range(lb_ref.shape[0])]
    mx = functools.reduce(jnp.maximum, lrows)
    es = [jnp.exp(r - mx) for r in lrows]
    den = functools.reduce(lambda a_, b_: a_ + b_, es)
    lb = jnp.zeros_like(mx)
    for j in range(1, layer + 1):
        lb = lb + es[j] / den
    one_minus_lb = 1.0 - lb
    nw = nw_ref[...]

    ri = lax.broadcasted_iota(jnp.int32, (ch, ch), 0)
    ci = lax.broadcasted_iota(jnp.int32, (ch, ch), 1)
    tril3 = jnp.concatenate([(ri >= ci).astype(BF16)] * 3, axis=1)
    rowi = lax.broadcasted_iota(jnp.int32, (ch, HG_EXPAND), 0)
    tgt = [((rowi >> lvl) & 1) == 1 for lvl in range(HG_LEVELS)]

    def head_chunk(h, rs):
        cs = slice(h * HG_EXPAND, (h + 1) * HG_EXPAND)
        qq = _silu(q_ref[rs, cs])
        kk = one_minus_lb[:, cs] * _sigmoid(-f_ref[rs, cs])
        logf = jnp.log1p(-kk)
        vv = v_ref[rs, cs]
        vb = vv.astype(BF16)
        b = jnp.dot(tril3, jnp.concatenate(_split3(logf), axis=0),
                    preferred_element_type=F32)
        st = state_t[h]
        o = lax.dot_general((qq * jnp.exp(b)).astype(BF16), st.astype(BF16), (((1,), (1,)), ((), ())),
                            preferred_element_type=F32)
        attn = jnp.zeros((ch, ch), F32)
        for lvl in range(HG_LEVELS):
            if lvl == 0:
                qe = jnp.where(tgt[0], qq * (1.0 - kk), 0.0)
                ke = jnp.where(tgt[0], 0.0, kk)
            else:
                m = _boundary_rows(b, lvl)
                e = jnp.exp2((b - m) * jnp.where(tgt[lvl], LOG2E, -LOG2E))
                prod = jnp.where(tgt[lvl], qq, kk) * e
                qe = jnp.where(tgt[lvl], prod, 0.0)
                ke = jnp.where(tgt[lvl], 0.0, prod)
            prod = lax.dot_general(qe.astype(BF16), ke.astype(BF16), (((1,), (1,)), ((), ())),
                                   preferred_element_type=F32)
            attn = attn + msk_ref[lvl] * prod
        diag = jnp.sum(qq * kk, axis=-1, keepdims=True)
        o = o + jnp.dot(attn.astype(BF16), vb, preferred_element_type=F32) + diag * vv
        b_last = b[ch - 1:ch, :]
        kd = (kk * jnp.exp(b_last - b)).astype(BF16)
        state_t[h] = st * jnp.exp(b_last) + jnp.dot(vv.T.astype(BF16), kd, preferred_element_type=F32)
        ms = jnp.mean(o * o, axis=-1, keepdims=True)
        y = o * lax.rsqrt(ms + EPS) * nw[:, cs]
        o_ref[rs, cs] = (y * _silu(g_ref[rs, cs])).astype(o_ref.dtype)

    for j in range(q_ref.shape[0] // ch):
        for h in range(HG_HEADS):
            head_chunk(h, slice(j * ch, (j + 1) * ch))


def _lru_tile(g_ref, x_ref, cw_ref, cb_ref, wa_ref, ba_ref, wx_ref, bx_ref, lam_ref, nw_ref, o_ref,
              ext, tail, hcarry):
    rows = x_ref.shape[0]
    xb = _causal_conv(x_ref, ext, tail, cw_ref, cb_ref)
    xbb = xb.astype(BF16)
    npair = LRU_WIDTH // LANES
    ra = jnp.concatenate([jnp.dot(xbb[:, j * LANES:(j + 1) * LANES], wa_ref[j], preferred_element_type=F32)
                          for j in range(npair)], axis=1)
    rx = jnp.concatenate([jnp.dot(xbb[:, j * LANES:(j + 1) * LANES], wx_ref[j], preferred_element_type=F32)
                          for j in range(npair)], axis=1)
    r = _sigmoid(ra + ba_ref[...])
    i = _sigmoid(rx + bx_ref[...])
    log_a = -LRU_C * r * _softplus(-lam_ref[...])
    a = jnp.exp(log_a)
    th = jnp.tanh(log_a)
    s = -2.0 * th
    root = s * lax.rsqrt(jnp.maximum(s * (1.0 - th), F32_TINY))
    u = root * (i * xb)

    sub = lax.broadcasted_iota(jnp.int32, (rows, LRU_WIDTH), 0) & (SUBLANES - 1)
    acc_a, acc_u = a, u
    d = 1
    while d < SUBLANES:
        keep = sub >= d
        a_sh = jnp.where(keep, pltpu.roll(acc_a, d, 0), 1.0)
        u_sh = jnp.where(keep, pltpu.roll(acc_u, d, 0), 0.0)
        acc_u = acc_a * u_sh + acc_u
        acc_a = acc_a * a_sh
        d *= 2
    carry = hcarry[0:1, :]
    groups = []
    for g in range(rows // SUBLANES):
        gs = slice(g * SUBLANES, (g + 1) * SUBLANES)
        hg = acc_a[gs, :] * carry + acc_u[gs, :]
        groups.append(hg)
        carry = hg[SUBLANES - 1:SUBLANES, :]
    h = jnp.concatenate(groups, axis=0)
    hcarry[0:1, :] = carry

    gate = g_ref[...]
    gelu = 0.5 * gate * (1.0 + jnp.tanh(np.sqrt(2.0 / np.pi).astype(np.float32) * (gate + 0.044715 * (gate * gate * gate))))
    y = h * gelu
    ms = jnp.mean(y * y, axis=-1, keepdims=True)
    o_ref[...] = (y * lax.rsqrt(ms + EPS) * nw_ref[...]).astype(o_ref.dtype)


def _hgrn2_lru(u, p, layer, bsz, seq):
    r = R_HG
    nt = seq // r
    n = bsz * seq
    npair = LRU_WIDTH // LANES

    def col(base, width):
        return pl.BlockSpec((r, width), lambda b, t: (b * nt + t, base // width))

    def const(shape):
        return pl.BlockSpec(shape, lambda b, t: (0,) * len(shape))

    def out(width):
        return pl.BlockSpec((r, width), lambda b, t: (b * nt + t, 0))

    return pl.pallas_call(
        functools.partial(_hgrn2_lru_kernel, layer),
        grid=(bsz, nt),
        in_specs=[
            col(COL_HQ, HG_WIDTH), col(COL_HF, HG_WIDTH), col(COL_HV, HG_WIDTH), col(COL_HG, HG_WIDTH),
            const(p["hg_lb"].shape), const((1, HG_WIDTH)), const((HG_LEVELS, HG_CHUNK, HG_CHUNK)),
            col(COL_LG, LRU_WIDTH), col(COL_LX, LRU_WIDTH),
            const((CONV_WIDTH, LRU_WIDTH)), const((1, LRU_WIDTH)),
            const((npair, LANES, LANES)), const((1, LRU_WIDTH)),
            const((npair, LANES, LANES)), const((1, LRU_WIDTH)),
            const((1, LRU_WIDTH)), const((1, LRU_WIDTH)),
        ],
        out_specs=[out(HG_WIDTH), out(LRU_WIDTH)],
        out_shape=[jax.ShapeDtypeStruct((n, HG_WIDTH), BF16), jax.ShapeDtypeStruct((n, LRU_WIDTH), BF16)],
        scratch_shapes=[
            pltpu.VMEM((HG_HEADS, HG_EXPAND, HG_EXPAND), F32),
            pltpu.VMEM((r + SUBLANES, LRU_WIDTH), F32),
            pltpu.VMEM((SUBLANES, LRU_WIDTH), F32),
            pltpu.VMEM((SUBLANES, LRU_WIDTH), F32),
        ],
        compiler_params=_cparams("arbitrary", "arbitrary"),
        name="hgrn2_lru_mixer",
    )(u, u, u, u, p["hg_lb"], p["hg_nw"], p["hg_msk"],
      u, u, p["lru_cw"], p["lru_cb"], p["lru_wa"], p["lru_ba"], p["lru_wx"], p["lru_bx"], p["lru_lam"], p["lru_nw"])


def _outproj_kernel(ys_ref, yh_ref, yl_ref, x_ref, w_ref, g_ref, nw_ref, sh_ref, sc_ref, xo_ref, *h_refs):
    acc = jnp.dot(ys_ref[...], w_ref[0:SSD_INNER, :], preferred_element_type=F32)
    acc = acc + jnp.dot(yh_ref[...], w_ref[SSD_INNER:SSD_INNER + HG_WIDTH, :], preferred_element_type=F32)
    acc = acc + jnp.dot(yl_ref[...], w_ref[SSD_INNER + HG_WIDTH:, :], preferred_element_type=F32)
    xn = x_ref[...] + g_ref[...] * acc
    xo_ref[...] = xn
    _store_planes(h_refs, _norm_mod(xn, nw_ref[...], sh_ref[...], sc_ref[...]))


def _outproj(y_ssd, y_hg, y_lru, x2, w_out, nw, mod3, seq):
    n, d = x2.shape
    tm = TM_OUTPROJ
    tpb = seq // tm

    def modspec(j):
        return pl.BlockSpec((None, 1, d), lambda i: ((i // tpb) * N_ADA + j, 0, 0))

    outs = pl.pallas_call(
        _outproj_kernel,
        grid=(n // tm,),
        in_specs=[
            pl.BlockSpec((tm, SSD_INNER), lambda i: (i, 0)),
            pl.BlockSpec((tm, HG_WIDTH), lambda i: (i, 0)),
            pl.BlockSpec((tm, LRU_WIDTH), lambda i: (i, 0)),
            pl.BlockSpec((tm, d), lambda i: (i, 0)),
            pl.BlockSpec(w_out.shape, lambda i: (0, 0)),
            modspec(2),
            pl.BlockSpec((1, d), lambda i: (0, 0)),
            modspec(3), modspec(4),
        ],
        out_specs=[pl.BlockSpec((tm, d), lambda i: (i, 0))] + [_plane_spec(tm)] * N_PLANES,
        out_shape=[jax.ShapeDtypeStruct((n, d), F32)] + [jax.ShapeDtypeStruct((n, LANES), jnp.uint32)] * N_PLANES,
        compiler_params=_cparams("arbitrary"),
        name="outproj",
    )(y_ssd, y_hg, y_lru, x2, w_out, mod3, nw, mod3, mod3)
    return outs[0], tuple(outs[1:])


def _plane_spec(tm):
    return pl.BlockSpec((tm, LANES), lambda i: (i, 0))


def _router_kernel(h0, h1, h2, h3, rw_ref, rb_ref, eid_ref, rank_ref, wt_ref, cnt_ref, carry, wscr):
    tm = h0.shape[0]

    @pl.when(pl.program_id(0) == 0)
    def _():
        carry[...] = jnp.zeros_like(carry)

    hb = _load_planes((h0, h1, h2, h3)).astype(BF16)
    logit_t = sum(lax.dot_general(part, hb, (((1,), (1,)), ((), ())), preferred_element_type=F32)
                  for part in _split3(rw_ref[...]))
    score = _sigmoid(logit_t)
    sel = score + rb_ref[...]
    neg_inf = jnp.float32(-jnp.inf)
    io_g = lax.broadcasted_iota(jnp.int32, (E_PER_GROUP, tm), 0)
    blocks, gscore = [], []
    for g in range(N_EXPERT_GROUPS):
        blk = sel[g * E_PER_GROUP:(g + 1) * E_PER_GROUP, :]
        m1 = jnp.max(blk, axis=0, keepdims=True)
        i1 = jnp.min(jnp.where(blk == m1, io_g, E_PER_GROUP), axis=0, keepdims=True)
        m2 = jnp.max(jnp.where(io_g == i1, neg_inf, blk), axis=0, keepdims=True)
        blocks.append(blk)
        gscore.append(m1 + m2)
    masked = []
    for g in range(N_EXPERT_GROUPS):
        rank = jnp.zeros((1, tm), jnp.int32)
        for o in range(N_EXPERT_GROUPS):
            if o == g:
                continue
            beats = (gscore[o] > gscore[g]) | ((gscore[o] == gscore[g]) & (o < g))
            rank = rank + beats.astype(jnp.int32)
        masked.append(jnp.where(rank < TOPK_GROUPS, blocks[g], MASK_SCORE))
    val = jnp.concatenate(masked, axis=0)
    io_e = lax.broadcasted_iota(jnp.int32, (N_EXPERTS, tm), 0)
    chosen = jnp.zeros((N_EXPERTS, tm), jnp.bool_)
    picks = []
    for k in range(TOP_K):
        m = jnp.max(val, axis=0, keepdims=True)
        idx = jnp.min(jnp.where(val == m, io_e, N_EXPERTS), axis=0, keepdims=True)
        pick = io_e == idx
        picks.append(pick)
        eid_ref[k:k + 1, :] = idx
        chosen = chosen | pick
        val = jnp.where(pick, neg_inf, val)
    w = jnp.where(chosen, score, 0.0)
    w = w / jnp.sum(w, axis=0, keepdims=True) * ROUTED_SCALE

    chosen_f = chosen.astype(F32)
    earlier = (lax.broadcasted_iota(jnp.int32, (tm, tm), 0) < lax.broadcasted_iota(jnp.int32, (tm, tm), 1))
    before = jnp.dot(chosen_f.astype(BF16), earlier.astype(BF16), preferred_element_type=F32)
    grank = carry[:, 0:1] + before
    wscr[...] = jnp.zeros_like(wscr)
    for k in range(TOP_K):
        rank_ref[k:k + 1, :] = jnp.sum(jnp.where(picks[k], grank, 0.0), axis=0, keepdims=True).astype(jnp.int32)
        wscr[k:k + 1, :] = jnp.sum(jnp.where(picks[k], w, 0.0), axis=0, keepdims=True)
    wt_ref[...] = wscr[...].T
    carry[...] = carry[...] + jnp.sum(chosen_f, axis=1, keepdims=True)
    cnt_ref[...] = carry[...]


def _router(hp, rw_t, rb):
    n = hp[0].shape[0]
    tm = TM_ROUTER
    return pl.pallas_call(
        _router_kernel,
        grid=(n // tm,),
        in_specs=[_plane_spec(tm)] * N_PLANES + [
            pl.BlockSpec(rw_t.shape, lambda i: (0, 0)),
            pl.BlockSpec((N_EXPERTS, 1), lambda i: (0, 0)),
        ],
        out_specs=[
            pl.BlockSpec((TOP_K, tm), lambda i: (0, i)),
            pl.BlockSpec((TOP_K, tm), lambda i: (0, i)),
            pl.BlockSpec((tm, LANES), lambda i: (i, 0)),
            pl.BlockSpec((N_EXPERTS, LANES), lambda i: (0, 0)),
        ],
        out_shape=[
            jax.ShapeDtypeStruct((TOP_K, n), jnp.int32),
            jax.ShapeDtypeStruct((TOP_K, n), jnp.int32),
            jax.ShapeDtypeStruct((n, LANES), F32),
            jax.ShapeDtypeStruct((N_EXPERTS, LANES), F32),
        ],
        scratch_shapes=[pltpu.VMEM((N_EXPERTS, LANES), F32), pltpu.VMEM((LANES, tm), F32)],
        compiler_params=_cparams("arbitrary"),
        name="router",
    )(*hp, rw_t, rb)


def _plan_kernel(n_slots, cnt_ref, eid_ref, rank_ref, dest_ref, be_ref, nb_ref, pad_ref):
    cnt = cnt_ref[...].astype(jnp.int32)
    padded = ((cnt + (MOE_BLOCK - 1)) >> MOE_BLOCK_LOG2) << MOE_BLOCK_LOG2
    ri = lax.broadcasted_iota(jnp.int32, (N_EXPERTS, N_EXPERTS), 0)
    ci = lax.broadcasted_iota(jnp.int32, (N_EXPERTS, N_EXPERTS), 1)
    pad_end = jnp.dot((ri >= ci).astype(F32), padded.astype(F32), precision=HI,
                      preferred_element_type=F32).astype(jnp.int32)
    pad_start = pad_end - padded
    eid = eid_ref[...]
    dest = rank_ref[...]
    for e in range(N_EXPERTS):
        dest = dest + jnp.where(eid == e, pad_start[e:e + 1, 0:1], 0)
    dest_ref[...] = dest
    nbp = be_ref.shape[1]
    jpos = lax.broadcasted_iota(jnp.int32, (N_EXPERTS, nbp), 1) * MOE_BLOCK
    be = jnp.sum((pad_end[:, 0:1] <= jpos).astype(jnp.int32), axis=0, keepdims=True)
    be_ref[...] = jnp.minimum(be, N_EXPERTS - 1)
    nb_ref[...] = pad_end[N_EXPERTS - 1:N_EXPERTS, :] >> MOE_BLOCK_LOG2
    lane = lax.broadcasted_iota(jnp.int32, (N_EXPERTS, LANES), 1)
    for r in range(MOE_BLOCK // LANES):
        s = pad_start + cnt + (r * LANES + lane)
        pad_ref[r * N_EXPERTS:(r + 1) * N_EXPERTS, :] = jnp.where(s < pad_end, s, n_slots + lane)


def _plan(cnt, eid, rank, n_slots):
    n = eid.shape[1]
    nbp = -(-(n_slots // MOE_BLOCK) // LANES) * LANES
    pad_rows = MOE_BLOCK // LANES * N_EXPERTS
    return pl.pallas_call(
        functools.partial(_plan_kernel, n_slots),
        grid=(1,),
        in_specs=[
            pl.BlockSpec(cnt.shape, lambda i: (0, 0)),
            pl.BlockSpec(eid.shape, lambda i: (0, 0)),
            pl.BlockSpec(rank.shape, lambda i: (0, 0)),
        ],
        out_specs=[
            pl.BlockSpec((TOP_K, n), lambda i: (0, 0)),
            pl.BlockSpec((1, nbp), lambda i: (0, 0)),
            pl.BlockSpec((1, LANES), lambda i: (0, 0)),
            pl.BlockSpec((pad_rows, LANES), lambda i: (0, 0)),
        ],
        out_shape=[
            jax.ShapeDtypeStruct((TOP_K, n), jnp.int32),
            jax.ShapeDtypeStruct((1, nbp), jnp.int32),
            jax.ShapeDtypeStruct((1, LANES), jnp.int32),
            jax.ShapeDtypeStruct((pad_rows, LANES), jnp.int32),
        ],
        compiler_params=_cparams("arbitrary"),
        name="moe_plan",
    )(cnt, eid, rank)


def _sc_mesh():
    return plsc.VectorSubcoreMesh(core_axis_name="c", subcore_axis_name="s")


def _sc_worker():
    return lax.axis_index("c") * SC_SUBCORES + lax.axis_index("s")


def _sc_dispatch(hp, dest_rows, pad_rows, n_rows):
    n = hp[0].shape[0]
    tiles_per_worker = n // SC_WIN // SC_WORKERS
    pad_per_worker = pad_rows.shape[0] // SC_WORKERS
    zeros = jnp.zeros((SC_WIN, LANES), jnp.uint32)

    def body(*refs):
        h = refs[:N_PLANES]
        dest_hbm, pad_hbm, z_hbm = refs[N_PLANES:N_PLANES + 3]
        xs = refs[N_PLANES + 3:2 * N_PLANES + 3]
        bufs = refs[2 * N_PLANES + 3:3 * N_PLANES + 3]
        ibuf, pbuf, sem = refs[3 * N_PLANES + 3:]
        wid = _sc_worker()

        pltpu.sync_copy(z_hbm, bufs[0])
        pltpu.sync_copy(pad_hbm.at[pl.ds(wid * pad_per_worker, pad_per_worker)], pbuf)
        copies = [pltpu.async_copy(bufs[0], xs[c].at[pbuf.at[r]], sem)
                  for r in range(pad_per_worker) for c in range(N_PLANES)]
        for cp in copies:
            cp.wait()

        @pl.loop(0, tiles_per_worker)
        def _(i):
            tile = wid * tiles_per_worker + i
            pltpu.sync_copy(dest_hbm.at[pl.ds(tile * TOP_K, TOP_K)], ibuf)
            for c in range(N_PLANES):
                pltpu.sync_copy(h[c].at[pl.ds(tile * SC_WIN, SC_WIN)], bufs[c])
            scatters = [pltpu.async_copy(bufs[c], xs[c].at[ibuf.at[k]], sem)
                        for c in range(N_PLANES) for k in range(TOP_K)]
            for cp in scatters:
                cp.wait()

    out_type = tuple(jax.ShapeDtypeStruct((n_rows, LANES), jnp.uint32) for _ in range(N_PLANES))
    scratch = ([pltpu.VMEM((SC_WIN, LANES), jnp.uint32)] * N_PLANES
               + [pltpu.VMEM((TOP_K, LANES), jnp.int32), pltpu.VMEM((pad_per_worker, LANES), jnp.int32),
                  pltpu.SemaphoreType.DMA])
    return pl.kernel(body, out_type=out_type, mesh=_sc_mesh(), scratch_types=scratch,
                     name="moe_sc_dispatch")(*hp, dest_rows, pad_rows, zeros)


def _sc_gather(ysp, dest_rows, n):
    tiles_per_worker = n // SC_WIN // SC_WORKERS

    def body(*refs):
        ys = refs[:N_PLANES]
        dest_hbm, g_hbm = refs[N_PLANES:N_PLANES + 2]
        bufs = refs[N_PLANES + 2:N_PLANES + 2 + SC_GATHER_BUFS]
        ibuf, sem = refs[N_PLANES + 2 + SC_GATHER_BUFS:]
        wid = _sc_worker()

        @pl.loop(0, tiles_per_worker)
        def _(i):
            tile = wid * tiles_per_worker + i
            pltpu.sync_copy(dest_hbm.at[pl.ds(tile * TOP_K, TOP_K)], ibuf)
            for c in range(N_PLANES):
                for k0 in range(0, TOP_K, SC_GATHER_BUFS):
                    gathers = [pltpu.async_copy(ys[c].at[ibuf.at[k0 + j]], bufs[j], sem)
                               for j in range(SC_GATHER_BUFS)]
                    for cp in gathers:
                        cp.wait()
                    stores = [pltpu.async_copy(
                        bufs[j], g_hbm.at[pl.ds(((k0 + j) * N_PLANES + c) * n + tile * SC_WIN, SC_WIN)], sem)
                        for j in range(SC_GATHER_BUFS)]
                    for cp in stores:
                        cp.wait()

    scratch = ([pltpu.VMEM((SC_WIN, LANES), jnp.uint32)] * SC_GATHER_BUFS
               + [pltpu.VMEM((TOP_K, LANES), jnp.int32), pltpu.SemaphoreType.DMA])
    return pl.kernel(body, out_type=jax.ShapeDtypeStruct((TOP_K * N_PLANES * n, LANES), jnp.uint32),
                     mesh=_sc_mesh(), scratch_types=scratch, name="moe_sc_gather")(*ysp, dest_rows)


def _expert_kernel(be_ref, nb_ref, *refs):
    xs_refs = refs[:N_PLANES]
    wg_ref, wu_ref, wd_ref = refs[N_PLANES:N_PLANES + 3]
    ys_refs = refs[N_PLANES + 3:2 * N_PLANES + 3]
    wg_b, wu_b, wd_b = refs[2 * N_PLANES + 3:]
    j = pl.program_id(0)
    used = j < nb_ref[0]
    new_expert = (j == 0) | (be_ref[j] != be_ref[jnp.maximum(j - 1, 0)])

    @pl.when(used & new_expert)
    def _():
        wg_b[...] = wg_ref[...].astype(BF16)
        wu_b[...] = wu_ref[...].astype(BF16)
        wd_b[...] = wd_ref[...].astype(BF16)

    @pl.when(used)
    def _():
        x = _load_planes(xs_refs).astype(BF16)
        a = jnp.dot(x, wg_b[...], preferred_element_type=F32)
        u = jnp.dot(x, wu_b[...], preferred_element_type=F32)
        y = jnp.dot((_silu(a) * u).astype(BF16), wd_b[...], preferred_element_type=F32)
        _store_planes(ys_refs, y)

    @pl.when(jnp.logical_not(used))
    def _():
        for ref in ys_refs:
            ref[...] = jnp.zeros_like(ref)


def _experts(be, nb, xsp, wg, wu, wd, layer, n_slots):
    d = wg.shape[2]

    def blk(j, be_ref, nb_ref):
        return jnp.minimum(j, jnp.maximum(nb_ref[0] - 1, 0))

    def wspec(shape):
        return pl.BlockSpec((None, None) + shape,
                            lambda j, be_ref, nb_ref: (layer, be_ref[blk(j, be_ref, nb_ref)], 0, 0))

    grid_spec = pltpu.PrefetchScalarGridSpec(
        num_scalar_prefetch=2,
        grid=(n_slots // MOE_BLOCK,),
        in_specs=[pl.BlockSpec((MOE_BLOCK, LANES), lambda j, be_ref, nb_ref: (blk(j, be_ref, nb_ref), 0))] * N_PLANES
        + [wspec((d, D_EXPERT)), wspec((d, D_EXPERT)), wspec((D_EXPERT, d))],
        out_specs=[pl.BlockSpec((MOE_BLOCK, LANES), lambda j, be_ref, nb_ref: (j, 0))] * N_PLANES,
        scratch_shapes=[pltpu.VMEM((d, D_EXPERT), BF16), pltpu.VMEM((d, D_EXPERT), BF16),
                        pltpu.VMEM((D_EXPERT, d), BF16)],
    )
    return pl.pallas_call(
        _expert_kernel,
        grid_spec=grid_spec,
        out_shape=[jax.ShapeDtypeStruct((n_slots, LANES), jnp.uint32)] * N_PLANES,
        compiler_params=_cparams("arbitrary"),
        name="moe_experts",
    )(be, nb, *xsp, wg, wu, wd)


def _combine_kernel(final, g_ref, wt_ref, h0, h1, h2, h3, sg_ref, su_ref, sd_ref, x_ref, gate_ref, fw_ref, o_ref):
    hb = _load_planes((h0, h1, h2, h3)).astype(BF16)
    a = jnp.dot(hb, sg_ref[...], preferred_element_type=F32)
    u = jnp.dot(hb, su_ref[...], preferred_element_type=F32)
    acc = jnp.dot((_silu(a) * u).astype(BF16), sd_ref[...], preferred_element_type=F32)
    wt = wt_ref[...]
    for k in range(TOP_K):
        rows = _unpack_bf16_pairs(jnp.concatenate([g_ref[k * N_PLANES + c] for c in range(N_PLANES)], axis=1))
        acc = acc + wt[:, k:k + 1] * rows
    xn = x_ref[...] + gate_ref[...] * acc
    if final:
        ms = jnp.mean(xn * xn, axis=-1, keepdims=True)
        xn = xn * lax.rsqrt(ms + EPS) * fw_ref[...]
    o_ref[...] = xn


def _combine(g, wt, hp, sg, su, sd, x2, mod3, fw, seq, final):
    n, d = x2.shape
    tm = TM_COMBINE
    tpb = seq // tm
    return pl.pallas_call(
        functools.partial(_combine_kernel, final),
        grid=(n // tm,),
        in_specs=[
            pl.BlockSpec((TOP_K * N_PLANES, tm, LANES), lambda i: (0, i, 0)),
            pl.BlockSpec((tm, LANES), lambda i: (i, 0)),
        ] + [_plane_spec(tm)] * N_PLANES + [
            pl.BlockSpec(sg.shape, lambda i: (0, 0)),
            pl.BlockSpec(su.shape, lambda i: (0, 0)),
            pl.BlockSpec(sd.shape, lambda i: (0, 0)),
            pl.BlockSpec((tm, d), lambda i: (i, 0)),
            pl.BlockSpec((None, 1, d), lambda i: ((i // tpb) * N_ADA + 5, 0, 0)),
            pl.BlockSpec((1, d), lambda i: (0, 0)),
        ],
        out_specs=pl.BlockSpec((tm, d), lambda i: (i, 0)),
        out_shape=jax.ShapeDtypeStruct((n, d), F32),
        compiler_params=_cparams("arbitrary"),
        name="moe_combine",
    )(g.reshape(TOP_K * N_PLANES, n, LANES), wt, *hp, sg, su, sd, x2, mod3, fw)


def _blockdiag_pairs(w):
    z = jnp.zeros((LRU_BLOCK_W, LRU_BLOCK_W), w.dtype)
    tiles = []
    for j in range(LRU_BLOCKS // 2):
        top = jnp.concatenate([w[2 * j], z], axis=1)
        bot = jnp.concatenate([z, w[2 * j + 1]], axis=1)
        tiles.append(jnp.concatenate([top, bot], axis=0))
    return jnp.stack(tiles).astype(BF16)


def _hg_level_masks():
    ch = HG_CHUNK
    msk = np.zeros((HG_LEVELS, ch, ch), np.float32)
    for lvl in range(HG_LEVELS):
        half = 1 << lvl
        for t in range(ch):
            base = (t // (2 * half)) * (2 * half)
            if (t // half) % 2 == 1:
                msk[lvl, t, base:base + half] = 1.0
    return jnp.asarray(msk)


def _ssd_expand():
    e = np.zeros((LANES, SSD_INNER), np.float32)
    for h in range(SSD_HEADS):
        e[h, h * SSD_HEADDIM:(h + 1) * SSD_HEADDIM] = 1.0
    return jnp.asarray(np.concatenate([e] * 3, axis=0), dtype=BF16)


def _pad_lanes(v, width):
    return jnp.pad(v, (0, width - v.shape[0])).reshape(1, width)


def _layer_params(l, w_in, ssd_conv_w, ssd_conv_b, ssd_dt_bias, ssd_a_log, ssd_d, ssd_norm_w, hg_lower_bounds,
                  hg_norm_w, lru_conv_w, lru_conv_b, lru_wa, lru_ba, lru_wx, lru_bx, lru_lambda, lru_norm_w):
    wi = w_in[l]
    dt0 = SSD_INNER + SSD_INNER + 2 * SSD_GROUPS * SSD_STATE
    w_cat = jnp.concatenate([wi[:, :dt0], wi[:, dt0 + SSD_HEADS:], wi[:, dt0:dt0 + SSD_HEADS]], axis=1)
    w_cat = jnp.pad(w_cat, ((0, 0), (0, U_WIDTH - w_cat.shape[1]))).astype(BF16)
    msk = _hg_level_masks()
    return dict(
        w_cat=w_cat,
        cwx=ssd_conv_w[l][:, :SSD_INNER], cbx=ssd_conv_b[l][:SSD_INNER].reshape(1, -1),
        cwb=ssd_conv_w[l][:, SSD_INNER:], cbb=ssd_conv_b[l][SSD_INNER:].reshape(1, -1),
        dtb=_pad_lanes(ssd_dt_bias[l], LANES), alog=_pad_lanes(ssd_a_log[l], LANES),
        dful=jnp.repeat(ssd_d[l], SSD_HEADDIM).reshape(1, -1), ssd_nw=ssd_norm_w[l].reshape(1, -1),
        expand=_ssd_expand(),
        hg_lb=hg_lower_bounds, hg_nw=hg_norm_w[l].reshape(1, -1), hg_msk=msk,
        lru_cw=lru_conv_w[l], lru_cb=lru_conv_b[l].reshape(1, -1),
        lru_wa=_blockdiag_pairs(lru_wa[l]), lru_ba=lru_ba[l].reshape(1, -1),
        lru_wx=_blockdiag_pairs(lru_wx[l]), lru_bx=lru_bx[l].reshape(1, -1),
        lru_lam=lru_lambda[l].reshape(1, -1), lru_nw=lru_norm_w[l].reshape(1, -1),
    )


def kernel(x, c, ada_w, ada_b, norm_mix_w, norm_ffn_w, w_in, ssd_conv_w, ssd_conv_b, ssd_dt_bias, ssd_a_log, ssd_d, ssd_norm_w, hg_lower_bounds, hg_norm_w, lru_conv_w, lru_conv_b, lru_wa, lru_ba, lru_wx, lru_bx, lru_lambda, lru_norm_w, w_out, router_w, router_bias, exp_gate, exp_up, exp_down, sh_gate, sh_up, sh_down, final_norm_w):
    bsz, seq, d = x.shape
    depth = ada_w.shape[0]
    assert d == D_MODEL and seq % TM_OUTPROJ == 0 and seq % SSD_CHUNK == 0
    n = bsz * seq
    n_slots = n * TOP_K + N_EXPERTS * MOE_BLOCK
    x2 = x.reshape(n, d)
    mod = _adaln(c, ada_w, ada_b)
    fw = final_norm_w.reshape(1, d)
    for l in range(depth):
        p = _layer_params(l, w_in, ssd_conv_w, ssd_conv_b, ssd_dt_bias, ssd_a_log, ssd_d, ssd_norm_w,
                          hg_lower_bounds, hg_norm_w, lru_conv_w, lru_conv_b, lru_wa, lru_ba, lru_wx, lru_bx,
                          lru_lambda, lru_norm_w)
        mod3 = mod[l].reshape(bsz * N_ADA, 1, d)
        u = _inproj(x2, norm_mix_w[l].reshape(1, d), mod3, p["w_cat"], seq)
        y_ssd = _ssd(u, p, bsz, seq)
        y_hg, y_lru = _hgrn2_lru(u, p, l, bsz, seq)
        x2, hp = _outproj(y_ssd, y_hg, y_lru, x2, w_out[l].astype(BF16), norm_ffn_w[l].reshape(1, d), mod3, seq)
        eid, rank, wt, cnt = _router(hp, router_w[l].T, router_bias[l].reshape(N_EXPERTS, 1))
        dest, be, nb, pad_rows = _plan(cnt, eid, rank, n_slots)
        dest_rows = dest.reshape(TOP_K, n // LANES, LANES).transpose(1, 0, 2).reshape(n // LANES * TOP_K, LANES)
        xsp = _sc_dispatch(hp, dest_rows, pad_rows, n_slots + LANES)
        ysp = _experts(be.reshape(-1), nb[0, :1], xsp, exp_gate, exp_up, exp_down, l, n_slots)
        g = _sc_gather(ysp, dest_rows, n)
        x2 = _combine(g, wt, hp, sh_gate[l].astype(BF16), sh_up[l].astype(BF16), sh_down[l].astype(BF16),
                      x2, mod3, fw, seq, final=(l == depth - 1))
    return x2.reshape(bsz, seq, d)
```

```python
import functools

import jax
import jax.numpy as jnp
import numpy as np
from jax import lax
from jax.experimental import pallas as pl
from jax.experimental.pallas import tpu as pltpu
from jax.experimental.pallas import tpu_sc as plsc

F32 = jnp.float32
BF16 = jnp.bfloat16
HI = lax.Precision.HIGHEST
F32_TINY = float(np.finfo(np.float32).tiny)
LOG2E = float(np.log2(np.e))

LANES = 128
SUBLANES = 8
VMEM_LIMIT_BYTES = 56 * 1024 * 1024

D_MODEL = 1024
EPS = 1e-6
N_ADA = 6
CONV_WIDTH = 4
SSD_INNER = 1024
SSD_HEADDIM = 64
SSD_HEADS = 16
SSD_GROUPS = 2
SSD_STATE = 128
SSD_CHUNK = 128
SSD_GROUP_W = SSD_INNER // SSD_GROUPS
HG_WIDTH = 512
HG_EXPAND = 128
HG_HEADS = 4
HG_CHUNK = 128
HG_LEVELS = 7
LRU_WIDTH = 512
LRU_BLOCKS = 8
LRU_BLOCK_W = 64
LRU_C = 8.0
N_EXPERTS = 64
TOP_K = 8
N_EXPERT_GROUPS = 8
E_PER_GROUP = 8
TOPK_GROUPS = 4
D_EXPERT = 256
ROUTED_SCALE = 2.5
MASK_SCORE = -1.0e4

COL_Z = 0
COL_XS = 1024
COL_BC = 2048
COL_HQ = 2560
COL_HF = 3072
COL_HV = 3584
COL_HG = 4096
COL_LG = 4608
COL_LX = 5120
COL_DT = 5632
U_WIDTH = 5760

TM_INPROJ = 256
TM_OUTPROJ = 512
TM_ROUTER = 512
TM_COMBINE = 256
SC_CORES = 2
SC_SUBCORES = 16
SC_WORKERS = SC_CORES * SC_SUBCORES
SC_WIN = LANES
SC_GATHER_BUFS = 4
MOE_BLOCK_LOG2 = 10
MOE_BLOCK = 1 << MOE_BLOCK_LOG2
R_HG = 256


def _cparams(*sem):
    return pltpu.CompilerParams(dimension_semantics=sem, vmem_limit_bytes=VMEM_LIMIT_BYTES)


def _sigmoid(x):
    return 0.5 * jnp.tanh(0.5 * x) + 0.5


def _silu(x):
    return x * _sigmoid(x)


def _split3(x):
    x1 = x.astype(BF16)
    r1 = x - x1.astype(F32)
    x2 = r1.astype(BF16)
    x3 = (r1 - x2.astype(F32)).astype(BF16)
    return x1, x2, x3


def _softplus(x):
    return jnp.maximum(x, 0.0) + jnp.log1p(jnp.exp(-jnp.abs(x)))


def _norm_mod(x, nw, shift, scale):
    ms = jnp.mean(x * x, axis=-1, keepdims=True)
    y = x * lax.rsqrt(ms + EPS) * nw
    return y * (1.0 + scale) + shift


_HI16 = np.uint32(0xFFFF0000)


def _pack_bf16_pairs(x):
    half = x.shape[1] // 2
    bits = lax.bitcast_convert_type(x.astype(BF16).astype(F32), jnp.uint32)
    return (bits[:, :half] & _HI16) | (bits[:, half:] >> 16)


def _unpack_bf16_pairs(w):
    hi = lax.bitcast_convert_type(w & _HI16, F32)
    lo = lax.bitcast_convert_type(w << 16, F32)
    return jnp.concatenate([hi, lo], axis=1)


N_PLANES = D_MODEL // 2 // LANES


def _store_planes(refs, x):
    packed = _pack_bf16_pairs(x)
    for c, ref in enumerate(refs):
        ref[...] = packed[:, c * LANES:(c + 1) * LANES]


def _load_planes(refs):
    return _unpack_bf16_pairs(jnp.concatenate([ref[...] for ref in refs], axis=1))


def _ada_kernel(c_ref, w_ref, b_ref, o_ref):
    c = c_ref[...]
    o_ref[...] = jnp.dot(_silu(c), w_ref[...], precision=HI, preferred_element_type=F32) + b_ref[...]


def _adaln(c, ada_w, ada_b):
    depth, d, n6 = ada_w.shape
    bsz = c.shape[0]
    tn = 1536
    return pl.pallas_call(
        _ada_kernel,
        grid=(depth, n6 // tn),
        in_specs=[
            pl.BlockSpec((bsz, d), lambda l, j: (0, 0)),
            pl.BlockSpec((None, d, tn), lambda l, j: (l, 0, j)),
            pl.BlockSpec((None, 1, tn), lambda l, j: (l, 0, j)),
        ],
        out_specs=pl.BlockSpec((None, bsz, tn), lambda l, j: (l, 0, j)),
        out_shape=jax.ShapeDtypeStruct((depth, bsz, n6), F32),
        compiler_params=_cparams("arbitrary", "arbitrary"),
        name="adaln_mod",
    )(c, ada_w, ada_b.reshape(depth, 1, n6))


def _inproj_kernel(x_ref, nw_ref, sh_ref, sc_ref, w_ref, o_ref):
    h = _norm_mod(x_ref[...], nw_ref[...], sh_ref[...], sc_ref[...])
    o_ref[...] = jnp.dot(h.astype(BF16), w_ref[...], preferred_element_type=F32)


def _inproj(x2, nw, mod3, w_cat, seq):
    n, d = x2.shape
    tm = TM_INPROJ
    tpb = seq // tm
    return pl.pallas_call(
        _inproj_kernel,
        grid=(n // tm,),
        in_specs=[
            pl.BlockSpec((tm, d), lambda i: (i, 0)),
            pl.BlockSpec((1, d), lambda i: (0, 0)),
            pl.BlockSpec((None, 1, d), lambda i: ((i // tpb) * N_ADA + 0, 0, 0)),
            pl.BlockSpec((None, 1, d), lambda i: ((i // tpb) * N_ADA + 1, 0, 0)),
            pl.BlockSpec((d, U_WIDTH), lambda i: (0, 0), pipeline_mode=pl.Buffered(1)),
        ],
        out_specs=pl.BlockSpec((tm, U_WIDTH), lambda i: (i, 0)),
        out_shape=jax.ShapeDtypeStruct((n, U_WIDTH), F32),
        compiler_params=_cparams("arbitrary"),
        name="inproj",
    )(x2, nw, mod3, mod3, w_cat)


def _causal_conv(cur_ref, ext, tail, cw_ref, cb_ref):
    rows, width = cur_ref.shape
    cur = cur_ref[...]
    ext[0:SUBLANES, :] = tail[...]
    ext[SUBLANES:SUBLANES + rows, :] = cur
    tail[...] = cur[rows - SUBLANES:rows, :]
    groups = ext[...].reshape(rows // SUBLANES + 1, SUBLANES, width)
    sub = lax.broadcasted_iota(jnp.int32, (1, SUBLANES, width), 1)
    acc = cb_ref[...] + cur * cw_ref[CONV_WIDTH - 1:CONV_WIDTH, :]
    for d in range(1, CONV_WIDTH):
        rot = pltpu.roll(groups, d, 1)
        back = jnp.where(sub < d, rot[:-1], rot[1:]).reshape(rows, width)
        acc = acc + back * cw_ref[CONV_WIDTH - 1 - d:CONV_WIDTH - d, :]
    return acc


def _ssd_kernel(z_ref, xs_ref, bc_ref, dt_ref, cwx_ref, cbx_ref, cwb_ref, cbb_ref, dtb_ref, alog_ref,
                dful_ref, nw_ref, e_ref, o_ref, extx, extb, tailx, tailb, hstate):
    c = pl.program_id(1)
    q = SSD_CHUNK

    @pl.when(c == 0)
    def _():
        tailx[...] = jnp.zeros_like(tailx)
        tailb[...] = jnp.zeros_like(tailb)
        hstate[...] = jnp.zeros_like(hstate)

    xs = _silu(_causal_conv(xs_ref, extx, tailx, cwx_ref, cbx_ref))
    bc = _silu(_causal_conv(bc_ref, extb, tailb, cwb_ref, cbb_ref))

    dt = _softplus(dt_ref[...] + dtb_ref[...])
    a = dt * (-jnp.exp(alog_ref[...]))
    ri = lax.broadcasted_iota(jnp.int32, (q, q), 0)
    ci = lax.broadcasted_iota(jnp.int32, (q, q), 1)
    tril = ri >= ci
    tril_f = tril.astype(F32)
    tril_b = tril.astype(BF16)
    acum = jnp.dot(jnp.concatenate([tril_b] * 3, axis=1), jnp.concatenate(_split3(a), axis=0),
                   preferred_element_type=F32)
    acum_t = acum.T
    expand3 = e_ref[...]
    dt_full = jnp.dot(jnp.concatenate(_split3(dt), axis=1), expand3, preferred_element_type=F32)
    acum_full = jnp.dot(jnp.concatenate(_split3(acum), axis=1), expand3, preferred_element_type=F32)
    alast_full = acum_full[q - 1:q, :]

    xdt = xs * dt_full
    xdt_b = xdt.astype(BF16)
    exp_a = jnp.exp(acum_full)
    xd_b = (xdt * jnp.exp(alast_full - acum_full)).astype(BF16)
    state_decay = jnp.exp(alast_full)
    left = lax.broadcasted_iota(jnp.int32, (q, LANES), 1) < SSD_HEADDIM
    zero_b = jnp.zeros((q, LANES), BF16)

    ys = []
    for g in range(SSD_GROUPS):
        b_g = bc[:, g * SSD_STATE:(g + 1) * SSD_STATE]
        c_g = bc[:, (SSD_GROUPS + g) * SSD_STATE:(SSD_GROUPS + g + 1) * SSD_STATE]
        c_b = c_g.astype(BF16)
        cb = lax.dot_general(c_b, b_g.astype(BF16), (((1,), (1,)), ((), ())), preferred_element_type=F32)
        cb = cb * tril_f
        cs = slice(g * SSD_GROUP_W, (g + 1) * SSD_GROUP_W)
        h_g = hstate[:, cs]
        y_off = jnp.dot(c_b, h_g.astype(BF16), preferred_element_type=F32) * exp_a[:, cs]
        pieces = []
        for pr in range(SSD_HEADS // SSD_GROUPS // 2):
            h0 = g * (SSD_HEADS // SSD_GROUPS) + 2 * pr
            ms = []
            for h in (h0, h0 + 1):
                col = acum[:, h:h + 1]
                row = acum_t[h:h + 1, :]
                ms.append((cb * jnp.exp(jnp.minimum(col - row, 0.0))).astype(BF16))
            lhs = jnp.concatenate(ms, axis=1)
            xp = xdt_b[:, h0 * SSD_HEADDIM:(h0 + 2) * SSD_HEADDIM]
            rhs = jnp.concatenate([jnp.where(left, xp, zero_b), jnp.where(left, zero_b, xp)], axis=0)
            pieces.append(jnp.dot(lhs, rhs, preferred_element_type=F32))
        ys.append(jnp.concatenate(pieces, axis=1) + y_off)
        b_t = b_g.T.astype(BF16)
        hstate[:, cs] = h_g * state_decay[:, cs] + jnp.dot(b_t, xd_b[:, cs], preferred_element_type=F32)

    y = jnp.concatenate(ys, axis=1) + xs * dful_ref[...]
    y = y * _silu(z_ref[...])
    outs = []
    for g in range(SSD_GROUPS):
        cs = slice(g * SSD_GROUP_W, (g + 1) * SSD_GROUP_W)
        yg = y[:, cs]
        ms = jnp.mean(yg * yg, axis=-1, keepdims=True)
        outs.append(yg * lax.rsqrt(ms + EPS) * nw_ref[:, cs])
    o_ref[...] = jnp.concatenate(outs, axis=1).astype(o_ref.dtype)


def _ssd(u, p, bsz, seq):
    q = SSD_CHUNK
    nc = seq // q
    n = bsz * seq

    def rows(b, c):
        return b * nc + c

    def const(shape):
        return pl.BlockSpec(shape, lambda b, c: (0,) * len(shape))

    return pl.pallas_call(
        _ssd_kernel,
        grid=(bsz, nc),
        in_specs=[
            pl.BlockSpec((q, SSD_INNER), lambda b, c: (rows(b, c), COL_Z // SSD_INNER)),
            pl.BlockSpec((q, SSD_INNER), lambda b, c: (rows(b, c), COL_XS // SSD_INNER)),
            pl.BlockSpec((q, 512), lambda b, c: (rows(b, c), COL_BC // 512)),
            pl.BlockSpec((q, LANES), lambda b, c: (rows(b, c), COL_DT // LANES)),
            const((CONV_WIDTH, SSD_INNER)), const((1, SSD_INNER)),
            const((CONV_WIDTH, 512)), const((1, 512)),
            const((1, LANES)), const((1, LANES)),
            const((1, SSD_INNER)), const((1, SSD_INNER)),
            const((3 * LANES, SSD_INNER)),
        ],
        out_specs=pl.BlockSpec((q, SSD_INNER), lambda b, c: (rows(b, c), 0)),
        out_shape=jax.ShapeDtypeStruct((n, SSD_INNER), BF16),
        scratch_shapes=[
            pltpu.VMEM((q + SUBLANES, SSD_INNER), F32),
            pltpu.VMEM((q + SUBLANES, 512), F32),
            pltpu.VMEM((SUBLANES, SSD_INNER), F32),
            pltpu.VMEM((SUBLANES, 512), F32),
            pltpu.VMEM((SSD_STATE, SSD_INNER), F32),
        ],
        compiler_params=_cparams("arbitrary", "arbitrary"),
        name="ssd_mixer",
    )(u, u, u, u, p["cwx"], p["cbx"], p["cwb"], p["cbb"], p["dtb"], p["alog"], p["dful"], p["ssd_nw"], p["expand"])


def _boundary_rows(b, lvl):
    half = 1 << lvl
    parts = []
    for v in range(b.shape[0] // SUBLANES):
        r0 = v * SUBLANES
        if 2 * half >= SUBLANES:
            src = (r0 // (2 * half)) * (2 * half) + half - 1
            parts.append(jnp.broadcast_to(b[src:src + 1, :], (SUBLANES, b.shape[1])))
        else:
            sub = lax.broadcasted_iota(jnp.int32, (SUBLANES, b.shape[1]), 0)
            piece = None
            for g in range(SUBLANES // (2 * half)):
                src = r0 + g * 2 * half + half - 1
                cand = jnp.broadcast_to(b[src:src + 1, :], (SUBLANES, b.shape[1]))
                piece = cand if piece is None else jnp.where(sub >= g * 2 * half, cand, piece)
            parts.append(piece)
    return jnp.concatenate(parts, axis=0)


def _hgrn2_lru_kernel(layer, q_ref, f_ref, v_ref, g_ref, lb_ref, nw_ref, msk_ref,
                      lg_ref, lx_ref, cw_ref, cb_ref, wa_ref, ba_ref, wx_ref, bx_ref, lam_ref, lnw_ref,
                      o_ref, ol_ref, state_t, ext, tail, hcarry):
    t = pl.program_id(1)
    ch = HG_CHUNK

    @pl.when(t == 0)
    def _():
        state_t[...] = jnp.zeros_like(state_t)
        tail[...] = jnp.zeros_like(tail)
        hcarry[...] = jnp.zeros_like(hcarry)

    _lru_tile(lg_ref, lx_ref, cw_ref, cb_ref, wa_ref, ba_ref, wx_ref, bx_ref, lam_ref, lnw_ref, ol_ref,
              ext, tail, hcarry)

    lrows = [lb_ref[j:j + 1, :] for j in range(lb_ref.shape[0])]
    mx = functools.reduce(jnp.maximum, lrows)
    es = [jnp.exp(r - mx) for r in lrows]
    den = functools.reduce(lambda a_, b_: a_ + b_, es)
    lb = jnp.zeros_like(mx)
    for j in range(1, layer + 1):
        lb = lb + es[j] / den
    one_minus_lb = 1.0 - lb
    nw = nw_ref[...]

    ri = lax.broadcasted_iota(jnp.int32, (ch, ch), 0)
    ci = lax.broadcasted_iota(jnp.int32, (ch, ch), 1)
    tril3 = jnp.concatenate([(ri >= ci).astype(BF16)] * 3, axis=1)
    rowi = lax.broadcasted_iota(jnp.int32, (ch, HG_EXPAND), 0)
    tgt = [((rowi >> lvl) & 1) == 1 for lvl in range(HG_LEVELS)]

    def head_chunk(h, rs):
        cs = slice(h * HG_EXPAND, (h + 1) * HG_EXPAND)
        qq = _silu(q_ref[rs, cs])
        kk = one_minus_lb[:, cs] * _sigmoid(-f_ref[rs, cs])
        logf = jnp.log1p(-kk)
        vv = v_ref[rs, cs]
        vb = vv.astype(BF16)
        b = jnp.dot(tril3, jnp.concatenate(_split3(logf), axis=0),
                    preferred_element_type=F32)
        st = state_t[h]
        o = lax.dot_general((qq * jnp.exp(b)).astype(BF16), st.astype(BF16), (((1,), (1,)), ((), ())),
                            preferred_element_type=F32)
        attn = jnp.zeros((ch, ch), F32)
        for lvl in range(HG_LEVELS):
            if lvl == 0:
                qe = jnp.where(tgt[0], qq * (1.0 - kk), 0.0)
                ke = jnp.where(tgt[0], 0.0, kk)
            else:
                m = _boundary_rows(b, lvl)
                e = jnp.exp2((b - m) * jnp.where(tgt[lvl], LOG2E, -LOG2E))
                prod = jnp.where(tgt[lvl], qq, kk) * e
                qe = jnp.where(tgt[lvl], prod, 0.0)
                ke = jnp.where(tgt[lvl], 0.0, prod)
            prod = lax.dot_general(qe.astype(BF16), ke.astype(BF16), (((1,), (1,)), ((), ())),
                                   preferred_element_type=F32)
            attn = attn + msk_ref[lvl] * prod
        diag = jnp.sum(qq * kk, axis=-1, keepdims=True)
        o = o + jnp.dot(attn.astype(BF16), vb, preferred_element_type=F32) + diag * vv
        b_last = b[ch - 1:ch, :]
        kd = (kk * jnp.exp(b_last - b)).astype(BF16)
        state_t[h] = st * jnp.exp(b_last) + jnp.dot(vv.T.astype(BF16), kd, preferred_element_type=F32)
        ms = jnp.mean(o * o, axis=-1, keepdims=True)
        y = o * lax.rsqrt(ms + EPS) * nw[:, cs]
        o_ref[rs, cs] = (y * _silu(g_ref[rs, cs])).astype(o_ref.dtype)

    for j in range(q_ref.shape[0] // ch):
        for h in range(HG_HEADS):
            head_chunk(h, slice(j * ch, (j + 1) * ch))


def _lru_tile(g_ref, x_ref, cw_ref, cb_ref, wa_ref, ba_ref, wx_ref, bx_ref, lam_ref, nw_ref, o_ref,
              ext, tail, hcarry):
    rows = x_ref.shape[0]
    xb = _causal_conv(x_ref, ext, tail, cw_ref, cb_ref)
    xbb = xb.astype(BF16)
    npair = LRU_WIDTH // LANES
    ra = jnp.concatenate([jnp.dot(xbb[:, j * LANES:(j + 1) * LANES], wa_ref[j], preferred_element_type=F32)
                          for j in range(npair)], axis=1)
    rx = jnp.concatenate([jnp.dot(xbb[:, j * LANES:(j + 1) * LANES], wx_ref[j], preferred_element_type=F32)
                          for j in range(npair)], axis=1)
    r = _sigmoid(ra + ba_ref[...])
    i = _sigmoid(rx + bx_ref[...])
    log_a = -LRU_C * r * _softplus(-lam_ref[...])
    a = jnp.exp(log_a)
    th = jnp.tanh(log_a)
    s = -2.0 * th
    root = s * lax.rsqrt(jnp.maximum(s * (1.0 - th), F32_TINY))
    u = root * (i * xb)

    sub = lax.broadcasted_iota(jnp.int32, (rows, LRU_WIDTH), 0) & (SUBLANES - 1)
    acc_a, acc_u = a, u
    d = 1
    while d < SUBLANES:
        keep = sub >= d
        a_sh = jnp.where(keep, pltpu.roll(acc_a, d, 0), 1.0)
        u_sh = jnp.where(keep, pltpu.roll(acc_u, d, 0), 0.0)
        acc_u = acc_a * u_sh + acc_u
        acc_a = acc_a * a_sh
        d *= 2
    carry = hcarry[0:1, :]
    groups = []
    for g in range(rows // SUBLANES):
        gs = slice(g * SUBLANES, (g + 1) * SUBLANES)
        hg = acc_a[gs, :] * carry + acc_u[gs, :]
        groups.append(hg)
        carry = hg[SUBLANES - 1:SUBLANES, :]
    h = jnp.concatenate(groups, axis=0)
    hcarry[0:1, :] = carry

    gate = g_ref[...]
    gelu = 0.5 * gate * (1.0 + jnp.tanh(np.sqrt(2.0 / np.pi).astype(np.float32) * (gate + 0.044715 * (gate * gate * gate))))
    y = h * gelu
    ms = jnp.mean(y * y, axis=-1, keepdims=True)
    o_ref[...] = (y * lax.rsqrt(ms + EPS) * nw_ref[...]).astype(o_ref.dtype)


def _hgrn2_lru(u, p, layer, bsz, seq):
    r = R_HG
    nt = seq // r
    n = bsz * seq
    npair = LRU_WIDTH // LANES

    def col(base, width):
        return pl.BlockSpec((r, width), lambda b, t: (b * nt + t, base // width))

    def const(shape):
        return pl.BlockSpec(shape, lambda b, t: (0,) * len(shape))

    def out(width):
        return pl.BlockSpec((r, width), lambda b, t: (b * nt + t, 0))

    return pl.pallas_call(
        functools.partial(_hgrn2_lru_kernel, layer),
        grid=(bsz, nt),
        in_specs=[
            col(COL_HQ, HG_WIDTH), col(COL_HF, HG_WIDTH), col(COL_HV, HG_WIDTH), col(COL_HG, HG_WIDTH),
            const(p["hg_lb"].shape), const((1, HG_WIDTH)), const((HG_LEVELS, HG_CHUNK, HG_CHUNK)),
            col(COL_LG, LRU_WIDTH), col(COL_LX, LRU_WIDTH),
            const((CONV_WIDTH, LRU_WIDTH)), const((1, LRU_WIDTH)),
            const((npair, LANES, LANES)), const((1, LRU_WIDTH)),
            const((npair, LANES, LANES)), const((1, LRU_WIDTH)),
            const((1, LRU_WIDTH)), const((1, LRU_WIDTH)),
        ],
        out_specs=[out(HG_WIDTH), out(LRU_WIDTH)],
        out_shape=[jax.ShapeDtypeStruct((n, HG_WIDTH), BF16), jax.ShapeDtypeStruct((n, LRU_WIDTH), BF16)],
        scratch_shapes=[
            pltpu.VMEM((HG_HEADS, HG_EXPAND, HG_EXPAND), F32),
            pltpu.VMEM((r + SUBLANES, LRU_WIDTH), F32),
            pltpu.VMEM((SUBLANES, LRU_WIDTH), F32),
            pltpu.VMEM((SUBLANES, LRU_WIDTH), F32),
        ],
        compiler_params=_cparams("arbitrary", "arbitrary"),
        name="hgrn2_lru_mixer",
    )(u, u, u, u, p["hg_lb"], p["hg_nw"], p["hg_msk"],
      u, u, p["lru_cw"], p["lru_cb"], p["lru_wa"], p["lru_ba"], p["lru_wx"], p["lru_bx"], p["lru_lam"], p["lru_nw"])


def _outproj_kernel(ys_ref, yh_ref, yl_ref, x_ref, w_ref, g_ref, nw_ref, sh_ref, sc_ref, xo_ref, *h_refs):
    acc = jnp.dot(ys_ref[...], w_ref[0:SSD_INNER, :], preferred_element_type=F32)
    acc = acc + jnp.dot(yh_ref[...], w_ref[SSD_INNER:SSD_INNER + HG_WIDTH, :], preferred_element_type=F32)
    acc = acc + jnp.dot(yl_ref[...], w_ref[SSD_INNER + HG_WIDTH:, :], preferred_element_type=F32)
    xn = x_ref[...] + g_ref[...] * acc
    xo_ref[...] = xn
    _store_planes(h_refs, _norm_mod(xn, nw_ref[...], sh_ref[...], sc_ref[...]))


def _outproj(y_ssd, y_hg, y_lru, x2, w_out, nw, mod3, seq):
    n, d = x2.shape
    tm = TM_OUTPROJ
    tpb = seq // tm

    def modspec(j):
        return pl.BlockSpec((None, 1, d), lambda i: ((i // tpb) * N_ADA + j, 0, 0))

    outs = pl.pallas_call(
        _outproj_kernel,
        grid=(n // tm,),
        in_specs=[
            pl.BlockSpec((tm, SSD_INNER), lambda i: (i, 0)),
            pl.BlockSpec((tm, HG_WIDTH), lambda i: (i, 0)),
            pl.BlockSpec((tm, LRU_WIDTH), lambda i: (i, 0)),
            pl.BlockSpec((tm, d), lambda i: (i, 0)),
            pl.BlockSpec(w_out.shape, lambda i: (0, 0)),
            modspec(2),
            pl.BlockSpec((1, d), lambda i: (0, 0)),
            modspec(3), modspec(4),
        ],
        out_specs=[pl.BlockSpec((tm, d), lambda i: (i, 0))] + [_plane_spec(tm)] * N_PLANES,
        out_shape=[jax.ShapeDtypeStruct((n, d), F32)] + [jax.ShapeDtypeStruct((n, LANES), jnp.uint32)] * N_PLANES,
        compiler_params=_cparams("arbitrary"),
        name="outproj",
    )(y_ssd, y_hg, y_lru, x2, w_out, mod3, nw, mod3, mod3)
    return outs[0], tuple(outs[1:])


def _plane_spec(tm):
    return pl.BlockSpec((tm, LANES), lambda i: (i, 0))


def _router_kernel(h0, h1, h2, h3, rw_ref, rb_ref, eid_ref, rank_ref, wt_ref, cnt_ref, carry, wscr):
    tm = h0.shape[0]

    @pl.when(pl.program_id(0) == 0)
    def _():
        carry[...] = jnp.zeros_like(carry)

    hb = _load_planes((h0, h1, h2, h3)).astype(BF16)
    logit_t = sum(lax.dot_general(part, hb, (((1,), (1,)), ((), ())), preferred_element_type=F32)
                  for part in _split3(rw_ref[...]))
    score = _sigmoid(logit_t)
    sel = score + rb_ref[...]
    neg_inf = jnp.float32(-jnp.inf)
    io_g = lax.broadcasted_iota(jnp.int32, (E_PER_GROUP, tm), 0)
    blocks, gscore = [], []
    for g in range(N_EXPERT_GROUPS):
        blk = sel[g * E_PER_GROUP:(g + 1) * E_PER_GROUP, :]
        m1 = jnp.max(blk, axis=0, keepdims=True)
        i1 = jnp.min(jnp.where(blk == m1, io_g, E_PER_GROUP), axis=0, keepdims=True)
        m2 = jnp.max(jnp.where(io_g == i1, neg_inf, blk), axis=0, keepdims=True)
        blocks.append(blk)
        gscore.append(m1 + m2)
    masked = []
    for g in range(N_EXPERT_GROUPS):
        rank = jnp.zeros((1, tm), jnp.int32)
        for o in range(N_EXPERT_GROUPS):
            if o == g:
                continue
            beats = (gscore[o] > gscore[g]) | ((gscore[o] == gscore[g]) & (o < g))
            rank = rank + beats.astype(jnp.int32)
        masked.append(jnp.where(rank < TOPK_GROUPS, blocks[g], MASK_SCORE))
    val = jnp.concatenate(masked, axis=0)
    io_e = lax.broadcasted_iota(jnp.int32, (N_EXPERTS, tm), 0)
    chosen = jnp.zeros((N_EXPERTS, tm), jnp.bool_)
    picks = []
    for k in range(TOP_K):
        m = jnp.max(val, axis=0, keepdims=True)
        idx = jnp.min(jnp.where(val == m, io_e, N_EXPERTS), axis=0, keepdims=True)
        pick = io_e == idx
        picks.append(pick)
        eid_ref[k:k + 1, :] = idx
        chosen = chosen | pick
        val = jnp.where(pick, neg_inf, val)
    w = jnp.where(chosen, score, 0.0)
    w = w / jnp.sum(w, axis=0, keepdims=True) * ROUTED_SCALE

    chosen_f = chosen.astype(F32)
    earlier = (lax.broadcasted_iota(jnp.int32, (tm, tm), 0) < lax.broadcasted_iota(jnp.int32, (tm, tm), 1))
    before = jnp.dot(chosen_f.astype(BF16), earlier.astype(BF16), preferred_element_type=F32)
    grank = carry[:, 0:1] + before
    wscr[...] = jnp.zeros_like(wscr)
    for k in range(TOP_K):
        rank_ref[k:k + 1, :] = jnp.sum(jnp.where(picks[k], grank, 0.0), axis=0, keepdims=True).astype(jnp.int32)
        wscr[k:k + 1, :] = jnp.sum(jnp.where(picks[k], w, 0.0), axis=0, keepdims=True)
    wt_ref[...] = wscr[...].T
    carry[...] = carry[...] + jnp.sum(chosen_f, axis=1, keepdims=True)
    cnt_ref[...] = carry[...]


def _router(hp, rw_t, rb):
    n = hp[0].shape[0]
    tm = TM_ROUTER
    return pl.pallas_call(
        _router_kernel,
        grid=(n // tm,),
        in_specs=[_plane_spec(tm)] * N_PLANES + [
            pl.BlockSpec(rw_t.shape, lambda i: (0, 0)),
            pl.BlockSpec((N_EXPERTS, 1), lambda i: (0, 0)),
        ],
        out_specs=[
            pl.BlockSpec((TOP_K, tm), lambda i: (0, i)),
            pl.BlockSpec((TOP_K, tm), lambda i: (0, i)),
            pl.BlockSpec((tm, LANES), lambda i: (i, 0)),
            pl.BlockSpec((N_EXPERTS, LANES), lambda i: (0, 0)),
        ],
        out_shape=[
            jax.ShapeDtypeStruct((TOP_K, n), jnp.int32),
            jax.ShapeDtypeStruct((TOP_K, n), jnp.int32),
            jax.ShapeDtypeStruct((n, LANES), F32),
            jax.ShapeDtypeStruct((N_EXPERTS, LANES), F32),
        ],
        scratch_shapes=[pltpu.VMEM((N_EXPERTS, LANES), F32), pltpu.VMEM((LANES, tm), F32)],
        compiler_params=_cparams("arbitrary"),
        name="router",
    )(*hp, rw_t, rb)


def _plan_kernel(n_slots, cnt_ref, eid_ref, rank_ref, dest_ref, be_ref, nb_ref, pad_ref):
    cnt = cnt_ref[...].astype(jnp.int32)
    padded = ((cnt + (MOE_BLOCK - 1)) >> MOE_BLOCK_LOG2) << MOE_BLOCK_LOG2
    ri = lax.broadcasted_iota(jnp.int32, (N_EXPERTS, N_EXPERTS), 0)
    ci = lax.broadcasted_iota(jnp.int32, (N_EXPERTS, N_EXPERTS), 1)
    pad_end = jnp.dot((ri >= ci).astype(F32), padded.astype(F32), precision=HI,
                      preferred_element_type=F32).astype(jnp.int32)
    pad_start = pad_end - padded
    eid = eid_ref[...]
    dest = rank_ref[...]
    for e in range(N_EXPERTS):
        dest = dest + jnp.where(eid == e, pad_start[e:e + 1, 0:1], 0)
    dest_ref[...] = dest
    nbp = be_ref.shape[1]
    jpos = lax.broadcasted_iota(jnp.int32, (N_EXPERTS, nbp), 1) * MOE_BLOCK
    be = jnp.sum((pad_end[:, 0:1] <= jpos).astype(jnp.int32), axis=0, keepdims=True)
    be_ref[...] = jnp.minimum(be, N_EXPERTS - 1)
    nb_ref[...] = pad_end[N_EXPERTS - 1:N_EXPERTS, :] >> MOE_BLOCK_LOG2
    lane = lax.broadcasted_iota(jnp.int32, (N_EXPERTS, LANES), 1)
    for r in range(MOE_BLOCK // LANES):
        s = pad_start + cnt + (r * LANES + lane)
        pad_ref[r * N_EXPERTS:(r + 1) * N_EXPERTS, :] = jnp.where(s < pad_end, s, n_slots + lane)


def _plan(cnt, eid, rank, n_slots):
    n = eid.shape[1]
    nbp = -(-(n_slots // MOE_BLOCK) // LANES) * LANES
    pad_rows = MOE_BLOCK // LANES * N_EXPERTS
    return pl.pallas_call(
        functools.partial(_plan_kernel, n_slots),
        grid=(1,),
        in_specs=[
            pl.BlockSpec(cnt.shape, lambda i: (0, 0)),
            pl.BlockSpec(eid.shape, lambda i: (0, 0)),
            pl.BlockSpec(rank.shape, lambda i: (0, 0)),
        ],
        out_specs=[
            pl.BlockSpec((TOP_K, n), lambda i: (0, 0)),
            pl.BlockSpec((1, nbp), lambda i: (0, 0)),
            pl.BlockSpec((1, LANES), lambda i: (0, 0)),
            pl.BlockSpec((pad_rows, LANES), lambda i: (0, 0)),
        ],
        out_shape=[
            jax.ShapeDtypeStruct((TOP_K, n), jnp.int32),
            jax.ShapeDtypeStruct((1, nbp), jnp.int32),
            jax.ShapeDtypeStruct((1, LANES), jnp.int32),
            jax.ShapeDtypeStruct((pad_rows, LANES), jnp.int32),
        ],
        compiler_params=_cparams("arbitrary"),
        name="moe_plan",
    )(cnt, eid, rank)


def _sc_mesh():
    return plsc.VectorSubcoreMesh(core_axis_name="c", subcore_axis_name="s")


def _sc_worker():
    return lax.axis_index("c") * SC_SUBCORES + lax.axis_index("s")


def _sc_dispatch(hp, dest_rows, pad_rows, n_rows):
    n = hp[0].shape[0]
    tiles_per_worker = n // SC_WIN // SC_WORKERS
    pad_per_worker = pad_rows.shape[0] // SC_WORKERS
    zeros = jnp.zeros((SC_WIN, LANES), jnp.uint32)

    def body(*refs):
        h = refs[:N_PLANES]
        dest_hbm, pad_hbm, z_hbm = refs[N_PLANES:N_PLANES + 3]
        xs = refs[N_PLANES + 3:2 * N_PLANES + 3]
        bufs = refs[2 * N_PLANES + 3:3 * N_PLANES + 3]
        ibuf, pbuf, sem = refs[3 * N_PLANES + 3:]
        wid = _sc_worker()

        pltpu.sync_copy(z_hbm, bufs[0])
        pltpu.sync_copy(pad_hbm.at[pl.ds(wid * pad_per_worker, pad_per_worker)], pbuf)
        copies = [pltpu.async_copy(bufs[0], xs[c].at[pbuf.at[r]], sem)
                  for r in range(pad_per_worker) for c in range(N_PLANES)]
        for cp in copies:
            cp.wait()

        @pl.loop(0, tiles_per_worker)
        def _(i):
            tile = wid * tiles_per_worker + i
            pltpu.sync_copy(dest_hbm.at[pl.ds(tile * TOP_K, TOP_K)], ibuf)
            for c in range(N_PLANES):
                pltpu.sync_copy(h[c].at[pl.ds(tile * SC_WIN, SC_WIN)], bufs[c])
            scatters = [pltpu.async_copy(bufs[c], xs[c].at[ibuf.at[k]], sem)
                        for c in range(N_PLANES) for k in range(TOP_K)]
            for cp in scatters:
                cp.wait()

    out_type = tuple(jax.ShapeDtypeStruct((n_rows, LANES), jnp.uint32) for _ in range(N_PLANES))
    scratch = ([pltpu.VMEM((SC_WIN, LANES), jnp.uint32)] * N_PLANES
               + [pltpu.VMEM((TOP_K, LANES), jnp.int32), pltpu.VMEM((pad_per_worker, LANES), jnp.int32),
                  pltpu.SemaphoreType.DMA])
    return pl.kernel(body, out_type=out_type, mesh=_sc_mesh(), scratch_types=scratch,
                     name="moe_sc_dispatch")(*hp, dest_rows, pad_rows, zeros)


def _sc_gather(ysp, dest_rows, n):
    tiles_per_worker = n // SC_WIN // SC_WORKERS

    def body(*refs):
        ys = refs[:N_PLANES]
        dest_hbm, g_hbm = refs[N_PLANES:N_PLANES + 2]
        bufs = refs[N_PLANES + 2:N_PLANES + 2 + SC_GATHER_BUFS]
        ibuf, sem = refs[N_PLANES + 2 + SC_GATHER_BUFS:]
        wid = _sc_worker()

        @pl.loop(0, tiles_per_worker)
        def _(i):
            tile = wid * tiles_per_worker + i
            pltpu.sync_copy(dest_hbm.at[pl.ds(tile * TOP_K, TOP_K)], ibuf)
            for c in range(N_PLANES):
                for k0 in range(0, TOP_K, SC_GATHER_BUFS):
                    gathers = [pltpu.async_copy(ys[c].at[ibuf.at[k0 + j]], bufs[j], sem)
                               for j in range(SC_GATHER_BUFS)]
                    for cp in gathers:
                        cp.wait()
                    stores = [pltpu.async_copy(
                        bufs[j], g_hbm.at[pl.ds(((k0 + j) * N_PLANES + c) * n + tile * SC_WIN, SC_WIN)], sem)
                        for j in range(SC_GATHER_BUFS)]
                    for cp in stores:
                        cp.wait()

    scratch = ([pltpu.VMEM((SC_WIN, LANES), jnp.uint32)] * SC_GATHER_BUFS
               + [pltpu.VMEM((TOP_K, LANES), jnp.int32), pltpu.SemaphoreType.DMA])
    return pl.kernel(body, out_type=jax.ShapeDtypeStruct((TOP_K * N_PLANES * n, LANES), jnp.uint32),
                     mesh=_sc_mesh(), scratch_types=scratch, name="moe_sc_gather")(*ysp, dest_rows)


def _expert_kernel(be_ref, nb_ref, *refs):
    xs_refs = refs[:N_PLANES]
    wg_ref, wu_ref, wd_ref = refs[N_PLANES:N_PLANES + 3]
    ys_refs = refs[N_PLANES + 3:2 * N_PLANES + 3]
    wg_b, wu_b, wd_b = refs[2 * N_PLANES + 3:]
    j = pl.program_id(0)
    used = j < nb_ref[0]
    new_expert = (j == 0) | (be_ref[j] != be_ref[jnp.maximum(j - 1, 0)])

    @pl.when(used & new_expert)
    def _():
        wg_b[...] = wg_ref[...].astype(BF16)
        wu_b[...] = wu_ref[...].astype(BF16)
        wd_b[...] = wd_ref[...].astype(BF16)

    @pl.when(used)
    def _():
        x = _load_planes(xs_refs).astype(BF16)
        a = jnp.dot(x, wg_b[...], preferred_element_type=F32)
        u = jnp.dot(x, wu_b[...], preferred_element_type=F32)
        y = jnp.dot((_silu(a) * u).astype(BF16), wd_b[...], preferred_element_type=F32)
        _store_planes(ys_refs, y)

    @pl.when(jnp.logical_not(used))
    def _():
        for ref in ys_refs:
            ref[...] = jnp.zeros_like(ref)


def _experts(be, nb, xsp, wg, wu, wd, layer, n_slots):
    d = wg.shape[2]

    def blk(j, be_ref, nb_ref):
        return jnp.minimum(j, jnp.maximum(nb_ref[0] - 1, 0))

    def wspec(shape):
        return pl.BlockSpec((None, None) + shape,
                            lambda j, be_ref, nb_ref: (layer, be_ref[blk(j, be_ref, nb_ref)], 0, 0))

    grid_spec = pltpu.PrefetchScalarGridSpec(
        num_scalar_prefetch=2,
        grid=(n_slots // MOE_BLOCK,),
        in_specs=[pl.BlockSpec((MOE_BLOCK, LANES), lambda j, be_ref, nb_ref: (blk(j, be_ref, nb_ref), 0))] * N_PLANES
        + [wspec((d, D_EXPERT)), wspec((d, D_EXPERT)), wspec((D_EXPERT, d))],
        out_specs=[pl.BlockSpec((MOE_BLOCK, LANES), lambda j, be_ref, nb_ref: (j, 0))] * N_PLANES,
        scratch_shapes=[pltpu.VMEM((d, D_EXPERT), BF16), pltpu.VMEM((d, D_EXPERT), BF16),
                        pltpu.VMEM((D_EXPERT, d), BF16)],
    )
    return pl.pallas_call(
        _expert_kernel,
        grid_spec=grid_spec,
        out_shape=[jax.ShapeDtypeStruct((n_slots, LANES), jnp.uint32)] * N_PLANES,
        compiler_params=_cparams("arbitrary"),
        name="moe_experts",
    )(be, nb, *xsp, wg, wu, wd)


def _combine_kernel(final, g_ref, wt_ref, h0, h1, h2, h3, sg_ref, su_ref, sd_ref, x_ref, gate_ref, fw_ref, o_ref):
    hb = _load_planes((h0, h1, h2, h3)).astype(BF16)
    a = jnp.dot(hb, sg_ref[...], preferred_element_type=F32)
    u = jnp.dot(hb, su_ref[...], preferred_element_type=F32)
    acc = jnp.dot((_silu(a) * u).astype(BF16), sd_ref[...], preferred_element_type=F32)
    wt = wt_ref[...]
    for k in range(TOP_K):
        rows = _unpack_bf16_pairs(jnp.concatenate([g_ref[k * N_PLANES + c] for c in range(N_PLANES)], axis=1))
        acc = acc + wt[:, k:k + 1] * rows
    xn = x_ref[...] + gate_ref[...] * acc
    if final:
        ms = jnp.mean(xn * xn, axis=-1, keepdims=True)
        xn = xn * lax.rsqrt(ms + EPS) * fw_ref[...]
    o_ref[...] = xn


def _combine(g, wt, hp, sg, su, sd, x2, mod3, fw, seq, final):
    n, d = x2.shape
    tm = TM_COMBINE
    tpb = seq // tm
    return pl.pallas_call(
        functools.partial(_combine_kernel, final),
        grid=(n // tm,),
        in_specs=[
            pl.BlockSpec((TOP_K * N_PLANES, tm, LANES), lambda i: (0, i, 0)),
            pl.BlockSpec((tm, LANES), lambda i: (i, 0)),
        ] + [_plane_spec(tm)] * N_PLANES + [
            pl.BlockSpec(sg.shape, lambda i: (0, 0)),
            pl.BlockSpec(su.shape, lambda i: (0, 0)),
            pl.BlockSpec(sd.shape, lambda i: (0, 0)),
            pl.BlockSpec((tm, d), lambda i: (i, 0)),
            pl.BlockSpec((None, 1, d), lambda i: ((i // tpb) * N_ADA + 5, 0, 0)),
            pl.BlockSpec((1, d), lambda i: (0, 0)),
        ],
        out_specs=pl.BlockSpec((tm, d), lambda i: (i, 0)),
        out_shape=jax.ShapeDtypeStruct((n, d), F32),
        compiler_params=_cparams("arbitrary"),
        name="moe_combine",
    )(g.reshape(TOP_K * N_PLANES, n, LANES), wt, *hp, sg, su, sd, x2, mod3, fw)


def _blockdiag_pairs(w):
    z = jnp.zeros((LRU_BLOCK_W, LRU_BLOCK_W), w.dtype)
    tiles = []
    for j in range(LRU_BLOCKS // 2):
        top = jnp.concatenate([w[2 * j], z], axis=1)
        bot = jnp.concatenate([z, w[2 * j + 1]], axis=1)
        tiles.append(jnp.concatenate([top, bot], axis=0))
    return jnp.stack(tiles).astype(BF16)


def _hg_level_masks():
    ch = HG_CHUNK
    msk = np.zeros((HG_LEVELS, ch, ch), np.float32)
    for lvl in range(HG_LEVELS):
        half = 1 << lvl
        for t in range(ch):
            base = (t // (2 * half)) * (2 * half)
            if (t // half) % 2 == 1:
                msk[lvl, t, base:base + half] = 1.0
    return jnp.asarray(msk)


def _ssd_expand():
    e = np.zeros((LANES, SSD_INNER), np.float32)
    for h in range(SSD_HEADS):
        e[h, h * SSD_HEADDIM:(h + 1) * SSD_HEADDIM] = 1.0
    return jnp.asarray(np.concatenate([e] * 3, axis=0), dtype=BF16)


def _pad_lanes(v, width):
    return jnp.pad(v, (0, width - v.shape[0])).reshape(1, width)


def _layer_params(l, w_in, ssd_conv_w, ssd_conv_b, ssd_dt_bias, ssd_a_log, ssd_d, ssd_norm_w, hg_lower_bounds,
                  hg_norm_w, lru_conv_w, lru_conv_b, lru_wa, lru_ba, lru_wx, lru_bx, lru_lambda, lru_norm_w):
    wi = w_in[l]
    dt0 = SSD_INNER + SSD_INNER + 2 * SSD_GROUPS * SSD_STATE
    w_cat = jnp.concatenate([wi[:, :dt0], wi[:, dt0 + SSD_HEADS:], wi[:, dt0:dt0 + SSD_HEADS]], axis=1)
    w_cat = jnp.pad(w_cat, ((0, 0), (0, U_WIDTH - w_cat.shape[1]))).astype(BF16)
    msk = _hg_level_masks()
    return dict(
        w_cat=w_cat,
        cwx=ssd_conv_w[l][:, :SSD_INNER], cbx=ssd_conv_b[l][:SSD_INNER].reshape(1, -1),
        cwb=ssd_conv_w[l][:, SSD_INNER:], cbb=ssd_conv_b[l][SSD_INNER:].reshape(1, -1),
        dtb=_pad_lanes(ssd_dt_bias[l], LANES), alog=_pad_lanes(ssd_a_log[l], LANES),
        dful=jnp.repeat(ssd_d[l], SSD_HEADDIM).reshape(1, -1), ssd_nw=ssd_norm_w[l].reshape(1, -1),
        expand=_ssd_expand(),
        hg_lb=hg_lower_bounds, hg_nw=hg_norm_w[l].reshape(1, -1), hg_msk=msk,
        lru_cw=lru_conv_w[l], lru_cb=lru_conv_b[l].reshape(1, -1),
        lru_wa=_blockdiag_pairs(lru_wa[l]), lru_ba=lru_ba[l].reshape(1, -1),
        lru_wx=_blockdiag_pairs(lru_wx[l]), lru_bx=lru_bx[l].reshape(1, -1),
        lru_lam=lru_lambda[l].reshape(1, -1), lru_nw=lru_norm_w[l].reshape(1, -1),
    )


def kernel(x, c, ada_w, ada_b, norm_mix_w, norm_ffn_w, w_in, ssd_conv_w, ssd_conv_b, ssd_dt_bias, ssd_a_log, ssd_d, ssd_norm_w, hg_lower_bounds, hg_norm_w, lru_conv_w, lru_conv_b, lru_wa, lru_ba, lru_wx, lru_bx, lru_lambda, lru_norm_w, w_out, router_w, router_bias, exp_gate, exp_up, exp_down, sh_gate, sh_up, sh_down, final_norm_w):
    bsz, seq, d = x.shape
    depth = ada_w.shape[0]
    assert d == D_MODEL and seq % TM_OUTPROJ == 0 and seq % SSD_CHUNK == 0
    n = bsz * seq
    n_slots = n * TOP_K + N_EXPERTS * MOE_BLOCK
    x2 = x.reshape(n, d)
    mod = _adaln(c, ada_w, ada_b)
    fw = final_norm_w.reshape(1, d)
    for l in range(depth):
        p = _layer_params(l, w_in, ssd_conv_w, ssd_conv_b, ssd_dt_bias, ssd_a_log, ssd_d, ssd_norm_w,
                          hg_lower_bounds, hg_norm_w, lru_conv_w, lru_conv_b, lru_wa, lru_ba, lru_wx, lru_bx,
                          lru_lambda, lru_norm_w)
        mod3 = mod[l].reshape(bsz * N_ADA, 1, d)
        u = _inproj(x2, norm_mix_w[l].reshape(1, d), mod3, p["w_cat"], seq)
        y_ssd = _ssd(u, p, bsz, seq)
        y_hg, y_lru = _hgrn2_lru(u, p, l, bsz, seq)
        x2, hp = _outproj(y_ssd, y_hg, y_lru, x2, w_out[l].astype(BF16), norm_ffn_w[l].reshape(1, d), mod3, seq)
        eid, rank, wt, cnt = _router(hp, router_w[l].T, router_bias[l].reshape(N_EXPERTS, 1))
        dest, be, nb, pad_rows = _plan(cnt, eid, rank, n_slots)
        dest_rows = dest.reshape(TOP_K, n // LANES, LANES).transpose(1, 0, 2).reshape(n // LANES * TOP_K, LANES)
        xsp = _sc_dispatch(hp, dest_rows, pad_rows, n_slots + LANES)
        ysp = _experts(be.reshape(-1), nb[0, :1], xsp, exp_gate, exp_up, exp_down, l, n_slots)
        g = _sc_gather(ysp, dest_rows, n)
        x2 = _combine(g, wt, hp, sh_gate[l].astype(BF16), sh_up[l].astype(BF16), sh_down[l].astype(BF16),
                      x2, mod3, fw, seq, final=(l == depth - 1))
    return x2.reshape(bsz, seq, d)
```

```python
import functools

import jax
import jax.numpy as jnp
import numpy as np
from jax import lax
from jax.experimental import pallas as pl
from jax.experimental.pallas import tpu as pltpu
from jax.experimental.pallas import tpu_sc as plsc

F32 = jnp.float32
BF16 = jnp.bfloat16
HI = lax.Precision.HIGHEST
F32_TINY = float(np.finfo(np.float32).tiny)
LOG2E = float(np.log2(np.e))

LANES = 128
SUBLANES = 8
VMEM_LIMIT_BYTES = 56 * 1024 * 1024

D_MODEL = 1024
EPS = 1e-6
N_ADA = 6
CONV_WIDTH = 4
SSD_INNER = 1024
SSD_HEADDIM = 64
SSD_HEADS = 16
SSD_GROUPS = 2
SSD_STATE = 128
SSD_CHUNK = 128
SSD_GROUP_W = SSD_INNER // SSD_GROUPS
HG_WIDTH = 512
HG_EXPAND = 128
HG_HEADS = 4
HG_CHUNK = 128
HG_LEVELS = 7
LRU_WIDTH = 512
LRU_BLOCKS = 8
LRU_BLOCK_W = 64
LRU_C = 8.0
N_EXPERTS = 64
TOP_K = 8
N_EXPERT_GROUPS = 8
E_PER_GROUP = 8
TOPK_GROUPS = 4
D_EXPERT = 256
ROUTED_SCALE = 2.5
MASK_SCORE = -1.0e4

COL_Z = 0
COL_XS = 1024
COL_BC = 2048
COL_HQ = 2560
COL_HF = 3072
COL_HV = 3584
COL_HG = 4096
COL_LG = 4608
COL_LX = 5120
COL_DT = 5632
U_WIDTH = 5760

TM_INPROJ = 512
TM_OUTPROJ = 1024
TM_ROUTER = 512
TM_COMBINE = 512
SC_CORES = 2
SC_SUBCORES = 16
SC_WORKERS = SC_CORES * SC_SUBCORES
SC_WIN = LANES
SC_GATHER_BUFS = 4
MOE_BLOCK_LOG2 = 10
MOE_BLOCK = 1 << MOE_BLOCK_LOG2
R_HG = 256


def _cparams(*sem):
    return pltpu.CompilerParams(dimension_semantics=sem, vmem_limit_bytes=VMEM_LIMIT_BYTES)


def _sigmoid(x):
    return 0.5 * jnp.tanh(0.5 * x) + 0.5


def _silu(x):
    h = 0.5 * x
    return h + h * jnp.tanh(h)


def _split3(x):
    x1 = x.astype(BF16)
    r1 = x - x1.astype(F32)
    x2 = r1.astype(BF16)
    x3 = (r1 - x2.astype(F32)).astype(BF16)
    return x1, x2, x3


def _softplus(x):
    return jnp.maximum(x, 0.0) + jnp.log1p(jnp.exp(-jnp.abs(x)))


def _norm_mod(x, nw, shift, scale):
    ms = jnp.mean(x * x, axis=-1, keepdims=True)
    y = x * lax.rsqrt(ms + EPS) * nw
    return y * (1.0 + scale) + shift


_HI16 = np.uint32(0xFFFF0000)


def _pack_bf16_pairs(x):
    half = x.shape[1] // 2
    bits = lax.bitcast_convert_type(x.astype(BF16).astype(F32), jnp.uint32)
    return (bits[:, :half] & _HI16) | (bits[:, half:] >> 16)


def _unpack_bf16_pairs(w):
    hi = lax.bitcast_convert_type(w & _HI16, F32)
    lo = lax.bitcast_convert_type(w << 16, F32)
    return jnp.concatenate([hi, lo], axis=1)


N_PLANES = D_MODEL // 2 // LANES


def _store_planes(refs, x):
    packed = _pack_bf16_pairs(x)
    for c, ref in enumerate(refs):
        ref[...] = packed[:, c * LANES:(c + 1) * LANES]


def _load_planes(refs):
    return _unpack_bf16_pairs(jnp.concatenate([ref[...] for ref in refs], axis=1))


def _ada_kernel(c_ref, w_ref, b_ref, o_ref):
    c = c_ref[...]
    o_ref[...] = jnp.dot(_silu(c), w_ref[...], precision=HI, preferred_element_type=F32) + b_ref[...]


def _adaln(c, ada_w, ada_b):
    depth, d, n6 = ada_w.shape
    bsz = c.shape[0]
    tn = 1536
    return pl.pallas_call(
        _ada_kernel,
        grid=(depth, n6 // tn),
        in_specs=[
            pl.BlockSpec((bsz, d), lambda l, j: (0, 0)),
            pl.BlockSpec((None, d, tn), lambda l, j: (l, 0, j)),
            pl.BlockSpec((None, 1, tn), lambda l, j: (l, 0, j)),
        ],
        out_specs=pl.BlockSpec((None, bsz, tn), lambda l, j: (l, 0, j)),
        out_shape=jax.ShapeDtypeStruct((depth, bsz, n6), F32),
        compiler_params=_cparams("arbitrary", "arbitrary"),
        name="adaln_mod",
    )(c, ada_w, ada_b.reshape(depth, 1, n6))


def _inproj_kernel(x_ref, nw_ref, sh_ref, sc_ref, w_ref, o_ref):
    h = _norm_mod(x_ref[...], nw_ref[...], sh_ref[...], sc_ref[...])
    o_ref[...] = jnp.dot(h.astype(BF16), w_ref[...], preferred_element_type=F32)


def _inproj(x2, nw, mod3, w_cat, seq):
    n, d = x2.shape
    tm = TM_INPROJ
    tpb = seq // tm
    return pl.pallas_call(
        _inproj_kernel,
        grid=(n // tm,),
        in_specs=[
            pl.BlockSpec((tm, d), lambda i: (i, 0)),
            pl.BlockSpec((1, d), lambda i: (0, 0)),
            pl.BlockSpec((None, 1, d), lambda i: ((i // tpb) * N_ADA + 0, 0, 0)),
            pl.BlockSpec((None, 1, d), lambda i: ((i // tpb) * N_ADA + 1, 0, 0)),
            pl.BlockSpec((d, U_WIDTH), lambda i: (0, 0), pipeline_mode=pl.Buffered(1)),
        ],
        out_specs=pl.BlockSpec((tm, U_WIDTH), lambda i: (i, 0)),
        out_shape=jax.ShapeDtypeStruct((n, U_WIDTH), F32),
        compiler_params=_cparams("arbitrary"),
        name="inproj",
    )(x2, nw, mod3, mod3, w_cat)


def _causal_conv(cur_ref, ext, tail, cw_ref, cb_ref):
    rows, width = cur_ref.shape
    cur = cur_ref[...]
    ext[0:SUBLANES, :] = tail[...]
    ext[SUBLANES:SUBLANES + rows, :] = cur
    tail[...] = cur[rows - SUBLANES:rows, :]
    groups = ext[...].reshape(rows // SUBLANES + 1, SUBLANES, width)
    sub = lax.broadcasted_iota(jnp.int32, (1, SUBLANES, width), 1)
    acc = cb_ref[...] + cur * cw_ref[CONV_WIDTH - 1:CONV_WIDTH, :]
    for d in range(1, CONV_WIDTH):
        rot = pltpu.roll(groups, d, 1)
        back = jnp.where(sub < d, rot[:-1], rot[1:]).reshape(rows, width)
        acc = acc + back * cw_ref[CONV_WIDTH - 1 - d:CONV_WIDTH - d, :]
    return acc


def _ssd_kernel(z_ref, xs_ref, bc_ref, dt_ref, cwx_ref, cbx_ref, cwb_ref, cbb_ref, dtb_ref, alog_ref,
                dful_ref, nw_ref, e_ref, o_ref, extx, extb, tailx, tailb, hstate):
    c = pl.program_id(1)
    q = SSD_CHUNK

    @pl.when(c == 0)
    def _():
        tailx[...] = jnp.zeros_like(tailx)
        tailb[...] = jnp.zeros_like(tailb)
        hstate[...] = jnp.zeros_like(hstate)

    xs = _silu(_causal_conv(xs_ref, extx, tailx, cwx_ref, cbx_ref))
    bc = _silu(_causal_conv(bc_ref, extb, tailb, cwb_ref, cbb_ref))

    dt = _softplus(dt_ref[...] + dtb_ref[...])
    a = dt * (-jnp.exp(alog_ref[...]))
    ri = lax.broadcasted_iota(jnp.int32, (q, q), 0)
    ci = lax.broadcasted_iota(jnp.int32, (q, q), 1)
    tril = ri >= ci
    tril_f = tril.astype(F32)
    tril_b = tril.astype(BF16)
    acum = jnp.dot(jnp.concatenate([tril_b] * 3, axis=1), jnp.concatenate(_split3(a), axis=0),
                   preferred_element_type=F32)
    acum_t = acum.T
    expand3 = e_ref[...]
    dt_full = jnp.dot(jnp.concatenate(_split3(dt), axis=1), expand3, preferred_element_type=F32)
    acum_full = jnp.dot(jnp.concatenate(_split3(acum), axis=1), expand3, preferred_element_type=F32)
    alast_full = acum_full[q - 1:q, :]

    xdt = xs * dt_full
    xdt_b = xdt.astype(BF16)
    exp_a = jnp.exp(acum_full)
    xd_b = (xdt * jnp.exp(alast_full - acum_full)).astype(BF16)
    state_decay = jnp.exp(alast_full)
    left = lax.broadcasted_iota(jnp.int32, (q, LANES), 1) < SSD_HEADDIM
    zero_b = jnp.zeros((q, LANES), BF16)

    ys = []
    for g in range(SSD_GROUPS):
        b_g = bc[:, g * SSD_STATE:(g + 1) * SSD_STATE]
        c_g = bc[:, (SSD_GROUPS + g) * SSD_STATE:(SSD_GROUPS + g + 1) * SSD_STATE]
        c_b = c_g.astype(BF16)
        cb = lax.dot_general(c_b, b_g.astype(BF16), (((1,), (1,)), ((), ())), preferred_element_type=F32)
        cb = cb * tril_f
        cs = slice(g * SSD_GROUP_W, (g + 1) * SSD_GROUP_W)
        h_g = hstate[:, cs]
        y_off = jnp.dot(c_b, h_g.astype(BF16), preferred_element_type=F32) * exp_a[:, cs]
        pieces = []
        for pr in range(SSD_HEADS // SSD_GROUPS // 2):
            h0 = g * (SSD_HEADS // SSD_GROUPS) + 2 * pr
            ms = []
            for h in (h0, h0 + 1):
                col = acum[:, h:h + 1]
                row = acum_t[h:h + 1, :]
                ms.append((cb * jnp.exp(jnp.minimum(col - row, 0.0))).astype(BF16))
            lhs = jnp.concatenate(ms, axis=1)
            xp = xdt_b[:, h0 * SSD_HEADDIM:(h0 + 2) * SSD_HEADDIM]
            rhs = jnp.concatenate([jnp.where(left, xp, zero_b), jnp.where(left, zero_b, xp)], axis=0)
            pieces.append(jnp.dot(lhs, rhs, preferred_element_type=F32))
        ys.append(jnp.concatenate(pieces, axis=1) + y_off)
        b_t = b_g.T.astype(BF16)
        hstate[:, cs] = h_g * state_decay[:, cs] + jnp.dot(b_t, xd_b[:, cs], preferred_element_type=F32)

    y = jnp.concatenate(ys, axis=1) + xs * dful_ref[...]
    y = y * _silu(z_ref[...])
    outs = []
    for g in range(SSD_GROUPS):
        cs = slice(g * SSD_GROUP_W, (g + 1) * SSD_GROUP_W)
        yg = y[:, cs]
        ms = jnp.mean(yg * yg, axis=-1, keepdims=True)
        outs.append(yg * lax.rsqrt(ms + EPS) * nw_ref[:, cs])
    o_ref[...] = jnp.concatenate(outs, axis=1).astype(o_ref.dtype)


def _ssd(u, p, bsz, seq):
    q = SSD_CHUNK
    nc = seq // q
    n = bsz * seq

    def rows(b, c):
        return b * nc + c

    def const(shape):
        return pl.BlockSpec(shape, lambda b, c: (0,) * len(shape))

    return pl.pallas_call(
        _ssd_kernel,
        grid=(bsz, nc),
        in_specs=[
            pl.BlockSpec((q, SSD_INNER), lambda b, c: (rows(b, c), COL_Z // SSD_INNER)),
            pl.BlockSpec((q, SSD_INNER), lambda b, c: (rows(b, c), COL_XS // SSD_INNER)),
            pl.BlockSpec((q, 512), lambda b, c: (rows(b, c), COL_BC // 512)),
            pl.BlockSpec((q, LANES), lambda b, c: (rows(b, c), COL_DT // LANES)),
            const((CONV_WIDTH, SSD_INNER)), const((1, SSD_INNER)),
            const((CONV_WIDTH, 512)), const((1, 512)),
            const((1, LANES)), const((1, LANES)),
            const((1, SSD_INNER)), const((1, SSD_INNER)),
            const((3 * LANES, SSD_INNER)),
        ],
        out_specs=pl.BlockSpec((q, SSD_INNER), lambda b, c: (rows(b, c), 0)),
        out_shape=jax.ShapeDtypeStruct((n, SSD_INNER), BF16),
        scratch_shapes=[
            pltpu.VMEM((q + SUBLANES, SSD_INNER), F32),
            pltpu.VMEM((q + SUBLANES, 512), F32),
            pltpu.VMEM((SUBLANES, SSD_INNER), F32),
            pltpu.VMEM((SUBLANES, 512), F32),
            pltpu.VMEM((SSD_STATE, SSD_INNER), F32),
        ],
        compiler_params=_cparams("arbitrary", "arbitrary"),
        name="ssd_mixer",
    )(u, u, u, u, p["cwx"], p["cbx"], p["cwb"], p["cbb"], p["dtb"], p["alog"], p["dful"], p["ssd_nw"], p["expand"])


def _boundary_rows(b, lvl):
    half = 1 << lvl
    parts = []
    for v in range(b.shape[0] // SUBLANES):
        r0 = v * SUBLANES
        if 2 * half >= SUBLANES:
            src = (r0 // (2 * half)) * (2 * half) + half - 1
            parts.append(jnp.broadcast_to(b[src:src + 1, :], (SUBLANES, b.shape[1])))
        else:
            sub = lax.broadcasted_iota(jnp.int32, (SUBLANES, b.shape[1]), 0)
            piece = None
            for g in range(SUBLANES // (2 * half)):
                src = r0 + g * 2 * half + half - 1
                cand = jnp.broadcast_to(b[src:src + 1, :], (SUBLANES, b.shape[1]))
                piece = cand if piece is None else jnp.where(sub >= g * 2 * half, cand, piece)
            parts.append(piece)
    return jnp.concatenate(parts, axis=0)


def _hgrn2_lru_kernel(layer, q_ref, f_ref, v_ref, g_ref, lb_ref, nw_ref, msk_ref,
                      lg_ref, lx_ref, cw_ref, cb_ref, wa_ref, ba_ref, wx_ref, bx_ref, lam_ref, lnw_ref,
                      o_ref, ol_ref, state_t, ext, tail, hcarry):
    t = pl.program_id(1)
    ch = HG_CHUNK

    @pl.when(t == 0)
    def _():
        state_t[...] = jnp.zeros_like(state_t)
        tail[...] = jnp.zeros_like(tail)
        hcarry[...] = jnp.zeros_like(hcarry)

    _lru_tile(lg_ref, lx_ref, cw_ref, cb_ref, wa_ref, ba_ref, wx_ref, bx_ref, lam_ref, lnw_ref, ol_ref,
              ext, tail, hcarry)

    lrows = [lb_ref[j:j + 1, :] for j in range(lb_ref.shape[0])]
    mx = functools.reduce(jnp.maximum, lrows)
    es = [jnp.exp(r - mx) for r in lrows]
    den = functools.reduce(lambda a_, b_: a_ + b_, es)
    lb = jnp.zeros_like(mx)
    for j in range(1, layer + 1):
        lb = lb + es[j] / den
    one_minus_lb = 1.0 - lb
    nw = nw_ref[...]

    ri = lax.broadcasted_iota(jnp.int32, (ch, ch), 0)
    ci = lax.broadcasted_iota(jnp.int32, (ch, ch), 1)
    tril3 = jnp.concatenate([(ri >= ci).astype(BF16)] * 3, axis=1)
    rowi = lax.broadcasted_iota(jnp.int32, (ch, HG_EXPAND), 0)
    tgt = [((rowi >> lvl) & 1) == 1 for lvl in range(HG_LEVELS)]

    def head_chunk(h, rs):
        cs = slice(h * HG_EXPAND, (h + 1) * HG_EXPAND)
        qq = _silu(q_ref[rs, cs])
        kk = one_minus_lb[:, cs] * _sigmoid(-f_ref[rs, cs])
        logf = jnp.log1p(-kk)
        vv = v_ref[rs, cs]
        vb = vv.astype(BF16)
        b = jnp.dot(tril3, jnp.concatenate(_split3(logf), axis=0),
                    preferred_element_type=F32)
        st = state_t[h]
        o = lax.dot_general((qq * jnp.exp(b)).astype(BF16), st.astype(BF16), (((1,), (1,)), ((), ())),
                            preferred_element_type=F32)
        attn = jnp.zeros((ch, ch), F32)
        for lvl in range(HG_LEVELS):
            if lvl == 0:
                qe = jnp.where(tgt[0], qq * (1.0 - kk), 0.0)
                ke = jnp.where(tgt[0], 0.0, kk)
            else:
                m = _boundary_rows(b, lvl)
                e = jnp.exp2((b - m) * jnp.where(tgt[lvl], LOG2E, -LOG2E))
                prod = jnp.where(tgt[lvl], qq, kk) * e
                qe = jnp.where(tgt[lvl], prod, 0.0)
                ke = jnp.where(tgt[lvl], 0.0, prod)
            prod = lax.dot_general(qe.astype(BF16), ke.astype(BF16), (((1,), (1,)), ((), ())),
                                   preferred_element_type=F32)
            attn = attn + msk_ref[lvl] * prod
        diag = jnp.sum(qq * kk, axis=-1, keepdims=True)
        o = o + jnp.dot(attn.astype(BF16), vb, preferred_element_type=F32) + diag * vv
        b_last = b[ch - 1:ch, :]
        kd = (kk * jnp.exp(b_last - b)).astype(BF16)
        state_t[h] = st * jnp.exp(b_last) + jnp.dot(vv.T.astype(BF16), kd, preferred_element_type=F32)
        ms = jnp.mean(o * o, axis=-1, keepdims=True)
        y = o * lax.rsqrt(ms + EPS) * nw[:, cs]
        o_ref[rs, cs] = (y * _silu(g_ref[rs, cs])).astype(o_ref.dtype)

    for j in range(q_ref.shape[0] // ch):
        for h in range(HG_HEADS):
            head_chunk(h, slice(j * ch, (j + 1) * ch))


def _lru_tile(g_ref, x_ref, cw_ref, cb_ref, wa_ref, ba_ref, wx_ref, bx_ref, lam_ref, nw_ref, o_ref,
              ext, tail, hcarry):
    rows = x_ref.shape[0]
    xb = _causal_conv(x_ref, ext, tail, cw_ref, cb_ref)
    xbb = xb.astype(BF16)
    npair = LRU_WIDTH // LANES
    ra = jnp.concatenate([jnp.dot(xbb[:, j * LANES:(j + 1) * LANES], wa_ref[j], preferred_element_type=F32)
                          for j in range(npair)], axis=1)
    rx = jnp.concatenate([jnp.dot(xbb[:, j * LANES:(j + 1) * LANES], wx_ref[j], preferred_element_type=F32)
                          for j in range(npair)], axis=1)
    r = _sigmoid(ra + ba_ref[...])
    i = _sigmoid(rx + bx_ref[...])
    log_a = -LRU_C * r * _softplus(-lam_ref[...])
    a = jnp.exp(log_a)
    th = jnp.tanh(log_a)
    s = -2.0 * th
    root = s * lax.rsqrt(jnp.maximum(s * (1.0 - th), F32_TINY))
    u = root * (i * xb)

    sub = lax.broadcasted_iota(jnp.int32, (rows, LRU_WIDTH), 0) & (SUBLANES - 1)
    acc_a, acc_u = a, u
    d = 1
    while d < SUBLANES:
        keep = sub >= d
        a_sh = jnp.where(keep, pltpu.roll(acc_a, d, 0), 1.0)
        u_sh = jnp.where(keep, pltpu.roll(acc_u, d, 0), 0.0)
        acc_u = acc_a * u_sh + acc_u
        acc_a = acc_a * a_sh
        d *= 2
    carry = hcarry[0:1, :]
    groups = []
    for g in range(rows // SUBLANES):
        gs = slice(g * SUBLANES, (g + 1) * SUBLANES)
        hg = acc_a[gs, :] * carry + acc_u[gs, :]
        groups.append(hg)
        carry = hg[SUBLANES - 1:SUBLANES, :]
    h = jnp.concatenate(groups, axis=0)
    hcarry[0:1, :] = carry

    gate = g_ref[...]
    gelu = 0.5 * gate * (1.0 + jnp.tanh(np.sqrt(2.0 / np.pi).astype(np.float32) * (gate + 0.044715 * (gate * gate * gate))))
    y = h * gelu
    ms = jnp.mean(y * y, axis=-1, keepdims=True)
    o_ref[...] = (y * lax.rsqrt(ms + EPS) * nw_ref[...]).astype(o_ref.dtype)


def _hgrn2_lru(u, p, layer, bsz, seq):
    r = R_HG
    nt = seq // r
    n = bsz * seq
    npair = LRU_WIDTH // LANES

    def col(base, width):
        return pl.BlockSpec((r, width), lambda b, t: (b * nt + t, base // width))

    def const(shape):
        return pl.BlockSpec(shape, lambda b, t: (0,) * len(shape))

    def out(width):
        return pl.BlockSpec((r, width), lambda b, t: (b * nt + t, 0))

    return pl.pallas_call(
        functools.partial(_hgrn2_lru_kernel, layer),
        grid=(bsz, nt),
        in_specs=[
            col(COL_HQ, HG_WIDTH), col(COL_HF, HG_WIDTH), col(COL_HV, HG_WIDTH), col(COL_HG, HG_WIDTH),
            const(p["hg_lb"].shape), const((1, HG_WIDTH)), const((HG_LEVELS, HG_CHUNK, HG_CHUNK)),
            col(COL_LG, LRU_WIDTH), col(COL_LX, LRU_WIDTH),
            const((CONV_WIDTH, LRU_WIDTH)), const((1, LRU_WIDTH)),
            const((npair, LANES, LANES)), const((1, LRU_WIDTH)),
            const((npair, LANES, LANES)), const((1, LRU_WIDTH)),
            const((1, LRU_WIDTH)), const((1, LRU_WIDTH)),
        ],
        out_specs=[out(HG_WIDTH), out(LRU_WIDTH)],
        out_shape=[jax.ShapeDtypeStruct((n, HG_WIDTH), BF16), jax.ShapeDtypeStruct((n, LRU_WIDTH), BF16)],
        scratch_shapes=[
            pltpu.VMEM((HG_HEADS, HG_EXPAND, HG_EXPAND), F32),
            pltpu.VMEM((r + SUBLANES, LRU_WIDTH), F32),
            pltpu.VMEM((SUBLANES, LRU_WIDTH), F32),
            pltpu.VMEM((SUBLANES, LRU_WIDTH), F32),
        ],
        compiler_params=_cparams("arbitrary", "arbitrary"),
        name="hgrn2_lru_mixer",
    )(u, u, u, u, p["hg_lb"], p["hg_nw"], p["hg_msk"],
      u, u, p["lru_cw"], p["lru_cb"], p["lru_wa"], p["lru_ba"], p["lru_wx"], p["lru_bx"], p["lru_lam"], p["lru_nw"])


def _outproj_kernel(ys_ref, yh_ref, yl_ref, x_ref, w_ref, g_ref, nw_ref, sh_ref, sc_ref, xo_ref, *h_refs):
    acc = jnp.dot(ys_ref[...], w_ref[0:SSD_INNER, :], preferred_element_type=F32)
    acc = acc + jnp.dot(yh_ref[...], w_ref[SSD_INNER:SSD_INNER + HG_WIDTH, :], preferred_element_type=F32)
    acc = acc + jnp.dot(yl_ref[...], w_ref[SSD_INNER + HG_WIDTH:, :], preferred_element_type=F32)
    xn = x_ref[...] + g_ref[...] * acc
    xo_ref[...] = xn
    _store_planes(h_refs, _norm_mod(xn, nw_ref[...], sh_ref[...], sc_ref[...]))


def _outproj(y_ssd, y_hg, y_lru, x2, w_out, nw, mod3, seq):
    n, d = x2.shape
    tm = TM_OUTPROJ
    tpb = seq // tm

    def modspec(j):
        return pl.BlockSpec((None, 1, d), lambda i: ((i // tpb) * N_ADA + j, 0, 0))

    outs = pl.pallas_call(
        _outproj_kernel,
        grid=(n // tm,),
        in_specs=[
            pl.BlockSpec((tm, SSD_INNER), lambda i: (i, 0)),
            pl.BlockSpec((tm, HG_WIDTH), lambda i: (i, 0)),
            pl.BlockSpec((tm, LRU_WIDTH), lambda i: (i, 0)),
            pl.BlockSpec((tm, d), lambda i: (i, 0)),
            pl.BlockSpec(w_out.shape, lambda i: (0, 0)),
            modspec(2),
            pl.BlockSpec((1, d), lambda i: (0, 0)),
            modspec(3), modspec(4),
        ],
        out_specs=[pl.BlockSpec((tm, d), lambda i: (i, 0))] + [_plane_spec(tm)] * N_PLANES,
        out_shape=[jax.ShapeDtypeStruct((n, d), F32)] + [jax.ShapeDtypeStruct((n, LANES), jnp.uint32)] * N_PLANES,
        compiler_params=_cparams("arbitrary"),
        name="outproj",
    )(y_ssd, y_hg, y_lru, x2, w_out, mod3, nw, mod3, mod3)
    return outs[0], tuple(outs[1:])


def _plane_spec(tm):
    return pl.BlockSpec((tm, LANES), lambda i: (i, 0))


def _router_kernel(h0, h1, h2, h3, rw_ref, rb_ref, eid_ref, rank_ref, wt_ref, cnt_ref, carry, wscr):
    tm = h0.shape[0]

    @pl.when(pl.program_id(0) == 0)
    def _():
        carry[...] = jnp.zeros_like(carry)

    hb = _load_planes((h0, h1, h2, h3)).astype(BF16)
    logit_t = sum(lax.dot_general(part, hb, (((1,), (1,)), ((), ())), preferred_element_type=F32)
                  for part in _split3(rw_ref[...]))
    score = _sigmoid(logit_t)
    sel = score + rb_ref[...]
    neg_inf = jnp.float32(-jnp.inf)
    io_g = lax.broadcasted_iota(jnp.int32, (E_PER_GROUP, tm), 0)
    blocks, gscore = [], []
    for g in range(N_EXPERT_GROUPS):
        blk = sel[g * E_PER_GROUP:(g + 1) * E_PER_GROUP, :]
        m1 = jnp.max(blk, axis=0, keepdims=True)
        i1 = jnp.min(jnp.where(blk == m1, io_g, E_PER_GROUP), axis=0, keepdims=True)
        m2 = jnp.max(jnp.where(io_g == i1, neg_inf, blk), axis=0, keepdims=True)
        blocks.append(blk)
        gscore.append(m1 + m2)
    masked = []
    for g in range(N_EXPERT_GROUPS):
        rank = jnp.zeros((1, tm), jnp.int32)
        for o in range(N_EXPERT_GROUPS):
            if o == g:
                continue
            beats = (gscore[o] > gscore[g]) | ((gscore[o] == gscore[g]) & (o < g))
            rank = rank + beats.astype(jnp.int32)
        masked.append(jnp.where(rank < TOPK_GROUPS, blocks[g], MASK_SCORE))
    val = jnp.concatenate(masked, axis=0)
    io_e = lax.broadcasted_iota(jnp.int32, (N_EXPERTS, tm), 0)
    chosen = jnp.zeros((N_EXPERTS, tm), jnp.bool_)
    picks = []
    for k in range(TOP_K):
        m = jnp.max(val, axis=0, keepdims=True)
        idx = jnp.min(jnp.where(val == m, io_e, N_EXPERTS), axis=0, keepdims=True)
        pick = io_e == idx
        picks.append(pick)
        eid_ref[k:k + 1, :] = idx
        chosen = chosen | pick
        val = jnp.where(pick, neg_inf, val)
    w = jnp.where(chosen, score, 0.0)
    w = w / jnp.sum(w, axis=0, keepdims=True) * ROUTED_SCALE

    chosen_f = chosen.astype(F32)
    earlier = (lax.broadcasted_iota(jnp.int32, (tm, tm), 0) < lax.broadcasted_iota(jnp.int32, (tm, tm), 1))
    before = jnp.dot(chosen_f.astype(BF16), earlier.astype(BF16), preferred_element_type=F32)
    grank = carry[:, 0:1] + before
    wscr[...] = jnp.zeros_like(wscr)
    for k in range(TOP_K):
        rank_ref[k:k + 1, :] = jnp.sum(jnp.where(picks[k], grank, 0.0), axis=0, keepdims=True).astype(jnp.int32)
        wscr[k:k + 1, :] = jnp.sum(jnp.where(picks[k], w, 0.0), axis=0, keepdims=True)
    wt_ref[...] = wscr[...].T
    carry[...] = carry[...] + jnp.sum(chosen_f, axis=1, keepdims=True)
    cnt_ref[...] = carry[...]


def _router(hp, rw_t, rb):
    n = hp[0].shape[0]
    tm = TM_ROUTER
    return pl.pallas_call(
        _router_kernel,
        grid=(n // tm,),
        in_specs=[_plane_spec(tm)] * N_PLANES + [
            pl.BlockSpec(rw_t.shape, lambda i: (0, 0)),
            pl.BlockSpec((N_EXPERTS, 1), lambda i: (0, 0)),
        ],
        out_specs=[
            pl.BlockSpec((TOP_K, tm), lambda i: (0, i)),
            pl.BlockSpec((TOP_K, tm), lambda i: (0, i)),
            pl.BlockSpec((tm, LANES), lambda i: (i, 0)),
            pl.BlockSpec((N_EXPERTS, LANES), lambda i: (0, 0)),
        ],
        out_shape=[
            jax.ShapeDtypeStruct((TOP_K, n), jnp.int32),
            jax.ShapeDtypeStruct((TOP_K, n), jnp.int32),
            jax.ShapeDtypeStruct((n, LANES), F32),
            jax.ShapeDtypeStruct((N_EXPERTS, LANES), F32),
        ],
        scratch_shapes=[pltpu.VMEM((N_EXPERTS, LANES), F32), pltpu.VMEM((LANES, tm), F32)],
        compiler_params=_cparams("arbitrary"),
        name="router",
    )(*hp, rw_t, rb)


def _plan_kernel(n_slots, cnt_ref, eid_ref, rank_ref, dest_ref, be_ref, nb_ref, pad_ref):
    cnt = cnt_ref[...].astype(jnp.int32)
    padded = ((cnt + (MOE_BLOCK - 1)) >> MOE_BLOCK_LOG2) << MOE_BLOCK_LOG2
    ri = lax.broadcasted_iota(jnp.int32, (N_EXPERTS, N_EXPERTS), 0)
    ci = lax.broadcasted_iota(jnp.int32, (N_EXPERTS, N_EXPERTS), 1)
    pad_end = jnp.dot((ri >= ci).astype(F32), padded.astype(F32), precision=HI,
                      preferred_element_type=F32).astype(jnp.int32)
    pad_start = pad_end - padded
    eid = eid_ref[...]
    dest = rank_ref[...]
    for e in range(N_EXPERTS):
        dest = dest + jnp.where(eid == e, pad_start[e:e + 1, 0:1], 0)
    dest_ref[...] = dest
    nbp = be_ref.shape[1]
    jpos = lax.broadcasted_iota(jnp.int32, (N_EXPERTS, nbp), 1) * MOE_BLOCK
    be = jnp.sum((pad_end[:, 0:1] <= jpos).astype(jnp.int32), axis=0, keepdims=True)
    be_ref[...] = jnp.minimum(be, N_EXPERTS - 1)
    nb_ref[...] = pad_end[N_EXPERTS - 1:N_EXPERTS, :] >> MOE_BLOCK_LOG2
    lane = lax.broadcasted_iota(jnp.int32, (N_EXPERTS, LANES), 1)
    for r in range(MOE_BLOCK // LANES):
        s = pad_start + cnt + (r * LANES + lane)
        pad_ref[r * N_EXPERTS:(r + 1) * N_EXPERTS, :] = jnp.where(s < pad_end, s, n_slots + lane)


def _plan(cnt, eid, rank, n_slots):
    n = eid.shape[1]
    nbp = -(-(n_slots // MOE_BLOCK) // LANES) * LANES
    pad_rows = MOE_BLOCK // LANES * N_EXPERTS
    return pl.pallas_call(
        functools.partial(_plan_kernel, n_slots),
        grid=(1,),
        in_specs=[
            pl.BlockSpec(cnt.shape, lambda i: (0, 0)),
            pl.BlockSpec(eid.shape, lambda i: (0, 0)),
            pl.BlockSpec(rank.shape, lambda i: (0, 0)),
        ],
        out_specs=[
            pl.BlockSpec((TOP_K, n), lambda i: (0, 0)),
            pl.BlockSpec((1, nbp), lambda i: (0, 0)),
            pl.BlockSpec((1, LANES), lambda i: (0, 0)),
            pl.BlockSpec((pad_rows, LANES), lambda i: (0, 0)),
        ],
        out_shape=[
            jax.ShapeDtypeStruct((TOP_K, n), jnp.int32),
            jax.ShapeDtypeStruct((1, nbp), jnp.int32),
            jax.ShapeDtypeStruct((1, LANES), jnp.int32),
            jax.ShapeDtypeStruct((pad_rows, LANES), jnp.int32),
        ],
        compiler_params=_cparams("arbitrary"),
        name="moe_plan",
    )(cnt, eid, rank)


def _sc_mesh():
    return plsc.VectorSubcoreMesh(core_axis_name="c", subcore_axis_name="s")


def _sc_worker():
    return lax.axis_index("c") * SC_SUBCORES + lax.axis_index("s")


def _sc_dispatch(hp, dest_rows, pad_rows, n_rows):
    n = hp[0].shape[0]
    tiles_per_worker = n // SC_WIN // SC_WORKERS
    pad_per_worker = pad_rows.shape[0] // SC_WORKERS
    zeros = jnp.zeros((SC_WIN, LANES), jnp.uint32)

    def body(*refs):
        h = refs[:N_PLANES]
        dest_hbm, pad_hbm, z_hbm = refs[N_PLANES:N_PLANES + 3]
        xs = refs[N_PLANES + 3:2 * N_PLANES + 3]
        bufs = refs[2 * N_PLANES + 3:3 * N_PLANES + 3]
        ibuf, pbuf, sem = refs[3 * N_PLANES + 3:]
        wid = _sc_worker()

        pltpu.sync_copy(z_hbm, bufs[0])
        pltpu.sync_copy(pad_hbm.at[pl.ds(wid * pad_per_worker, pad_per_worker)], pbuf)
        copies = [pltpu.async_copy(bufs[0], xs[c].at[pbuf.at[r]], sem)
                  for r in range(pad_per_worker) for c in range(N_PLANES)]
        for cp in copies:
            cp.wait()

        @pl.loop(0, tiles_per_worker)
        def _(i):
            tile = wid * tiles_per_worker + i
            pltpu.sync_copy(dest_hbm.at[pl.ds(tile * TOP_K, TOP_K)], ibuf)
            for c in range(N_PLANES):
                pltpu.sync_copy(h[c].at[pl.ds(tile * SC_WIN, SC_WIN)], bufs[c])
            scatters = [pltpu.async_copy(bufs[c], xs[c].at[ibuf.at[k]], sem)
                        for c in range(N_PLANES) for k in range(TOP_K)]
            for cp in scatters:
                cp.wait()

    out_type = tuple(jax.ShapeDtypeStruct((n_rows, LANES), jnp.uint32) for _ in range(N_PLANES))
    scratch = ([pltpu.VMEM((SC_WIN, LANES), jnp.uint32)] * N_PLANES
               + [pltpu.VMEM((TOP_K, LANES), jnp.int32), pltpu.VMEM((pad_per_worker, LANES), jnp.int32),
                  pltpu.SemaphoreType.DMA])
    return pl.kernel(body, out_type=out_type, mesh=_sc_mesh(), scratch_types=scratch,
                     name="moe_sc_dispatch")(*hp, dest_rows, pad_rows, zeros)


def _sc_gather(ysp, dest_rows, n):
    tiles_per_worker = n // SC_WIN // SC_WORKERS

    def body(*refs):
        ys = refs[:N_PLANES]
        dest_hbm, g_hbm = refs[N_PLANES:N_PLANES + 2]
        bufs = refs[N_PLANES + 2:N_PLANES + 2 + SC_GATHER_BUFS]
        ibuf, sem = refs[N_PLANES + 2 + SC_GATHER_BUFS:]
        wid = _sc_worker()

        @pl.loop(0, tiles_per_worker)
        def _(i):
            tile = wid * tiles_per_worker + i
            pltpu.sync_copy(dest_hbm.at[pl.ds(tile * TOP_K, TOP_K)], ibuf)
            for c in range(N_PLANES):
                for k0 in range(0, TOP_K, SC_GATHER_BUFS):
                    gathers = [pltpu.async_copy(ys[c].at[ibuf.at[k0 + j]], bufs[j], sem)
                               for j in range(SC_GATHER_BUFS)]
                    for cp in gathers:
                        cp.wait()
                    stores = [pltpu.async_copy(
                        bufs[j], g_hbm.at[pl.ds(((k0 + j) * N_PLANES + c) * n + tile * SC_WIN, SC_WIN)], sem)
                        for j in range(SC_GATHER_BUFS)]
                    for cp in stores:
                        cp.wait()

    scratch = ([pltpu.VMEM((SC_WIN, LANES), jnp.uint32)] * SC_GATHER_BUFS
               + [pltpu.VMEM((TOP_K, LANES), jnp.int32), pltpu.SemaphoreType.DMA])
    return pl.kernel(body, out_type=jax.ShapeDtypeStruct((TOP_K * N_PLANES * n, LANES), jnp.uint32),
                     mesh=_sc_mesh(), scratch_types=scratch, name="moe_sc_gather")(*ysp, dest_rows)


def _expert_kernel(be_ref, nb_ref, *refs):
    xs_refs = refs[:N_PLANES]
    wg_ref, wu_ref, wd_ref = refs[N_PLANES:N_PLANES + 3]
    ys_refs = refs[N_PLANES + 3:2 * N_PLANES + 3]
    wg_b, wu_b, wd_b = refs[2 * N_PLANES + 3:]
    j = pl.program_id(0)
    used = j < nb_ref[0]
    new_expert = (j == 0) | (be_ref[j] != be_ref[jnp.maximum(j - 1, 0)])

    @pl.when(used & new_expert)
    def _():
        wg_b[...] = wg_ref[...].astype(BF16)
        wu_b[...] = wu_ref[...].astype(BF16)
        wd_b[...] = wd_ref[...].astype(BF16)

    @pl.when(used)
    def _():
        x = _load_planes(xs_refs).astype(BF16)
        a = jnp.dot(x, wg_b[...], preferred_element_type=F32)
        u = jnp.dot(x, wu_b[...], preferred_element_type=F32)
        y = jnp.dot((_silu(a) * u).astype(BF16), wd_b[...], preferred_element_type=F32)
        _store_planes(ys_refs, y)

    @pl.when(jnp.logical_not(used))
    def _():
        for ref in ys_refs:
            ref[...] = jnp.zeros_like(ref)


def _experts(be, nb, xsp, wg, wu, wd, layer, n_slots):
    d = wg.shape[2]

    def blk(j, be_ref, nb_ref):
        return jnp.minimum(j, jnp.maximum(nb_ref[0] - 1, 0))

    def wspec(shape):
        return pl.BlockSpec((None, None) + shape,
                            lambda j, be_ref, nb_ref: (layer, be_ref[blk(j, be_ref, nb_ref)], 0, 0))

    grid_spec = pltpu.PrefetchScalarGridSpec(
        num_scalar_prefetch=2,
        grid=(n_slots // MOE_BLOCK,),
        in_specs=[pl.BlockSpec((MOE_BLOCK, LANES), lambda j, be_ref, nb_ref: (blk(j, be_ref, nb_ref), 0))] * N_PLANES
        + [wspec((d, D_EXPERT)), wspec((d, D_EXPERT)), wspec((D_EXPERT, d))],
        out_specs=[pl.BlockSpec((MOE_BLOCK, LANES), lambda j, be_ref, nb_ref: (j, 0))] * N_PLANES,
        scratch_shapes=[pltpu.VMEM((d, D_EXPERT), BF16), pltpu.VMEM((d, D_EXPERT), BF16),
                        pltpu.VMEM((D_EXPERT, d), BF16)],
    )
    return pl.pallas_call(
        _expert_kernel,
        grid_spec=grid_spec,
        out_shape=[jax.ShapeDtypeStruct((n_slots, LANES), jnp.uint32)] * N_PLANES,
        compiler_params=_cparams("arbitrary"),
        name="moe_experts",
    )(be, nb, *xsp, wg, wu, wd)


def _combine_kernel(final, g_ref, wt_ref, h0, h1, h2, h3, sg_ref, su_ref, sd_ref, x_ref, gate_ref, fw_ref, o_ref):
    hb = _load_planes((h0, h1, h2, h3)).astype(BF16)
    a = jnp.dot(hb, sg_ref[...], preferred_element_type=F32)
    u = jnp.dot(hb, su_ref[...], preferred_element_type=F32)
    acc = jnp.dot((_silu(a) * u).astype(BF16), sd_ref[...], preferred_element_type=F32)
    wt = wt_ref[...]
    for k in range(TOP_K):
        rows = _unpack_bf16_pairs(jnp.concatenate([g_ref[k * N_PLANES + c] for c in range(N_PLANES)], axis=1))
        acc = acc + wt[:, k:k + 1] * rows
    xn = x_ref[...] + gate_ref[...] * acc
    if final:
        ms = jnp.mean(xn * xn, axis=-1, keepdims=True)
        xn = xn * lax.rsqrt(ms + EPS) * fw_ref[...]
    o_ref[...] = xn


def _combine(g, wt, hp, sg, su, sd, x2, mod3, fw, seq, final):
    n, d = x2.shape
    tm = TM_COMBINE
    tpb = seq // tm
    return pl.pallas_call(
        functools.partial(_combine_kernel, final),
        grid=(n // tm,),
        in_specs=[
            pl.BlockSpec((TOP_K * N_PLANES, tm, LANES), lambda i: (0, i, 0)),
            pl.BlockSpec((tm, LANES), lambda i: (i, 0)),
        ] + [_plane_spec(tm)] * N_PLANES + [
            pl.BlockSpec(sg.shape, lambda i: (0, 0)),
            pl.BlockSpec(su.shape, lambda i: (0, 0)),
            pl.BlockSpec(sd.shape, lambda i: (0, 0)),
            pl.BlockSpec((tm, d), lambda i: (i, 0)),
            pl.BlockSpec((None, 1, d), lambda i: ((i // tpb) * N_ADA + 5, 0, 0)),
            pl.BlockSpec((1, d), lambda i: (0, 0)),
        ],
        out_specs=pl.BlockSpec((tm, d), lambda i: (i, 0)),
        out_shape=jax.ShapeDtypeStruct((n, d), F32),
        compiler_params=_cparams("arbitrary"),
        name="moe_combine",
    )(g.reshape(TOP_K * N_PLANES, n, LANES), wt, *hp, sg, su, sd, x2, mod3, fw)


def _blockdiag_pairs(w):
    z = jnp.zeros((LRU_BLOCK_W, LRU_BLOCK_W), w.dtype)
    tiles = []
    for j in range(LRU_BLOCKS // 2):
        top = jnp.concatenate([w[2 * j], z], axis=1)
        bot = jnp.concatenate([z, w[2 * j + 1]], axis=1)
        tiles.append(jnp.concatenate([top, bot], axis=0))
    return jnp.stack(tiles).astype(BF16)


def _hg_level_masks():
    ch = HG_CHUNK
    msk = np.zeros((HG_LEVELS, ch, ch), np.float32)
    for lvl in range(HG_LEVELS):
        half = 1 << lvl
        for t in range(ch):
            base = (t // (2 * half)) * (2 * half)
            if (t // half) % 2 == 1:
                msk[lvl, t, base:base + half] = 1.0
    return jnp.asarray(msk)


def _ssd_expand():
    e = np.zeros((LANES, SSD_INNER), np.float32)
    for h in range(SSD_HEADS):
        e[h, h * SSD_HEADDIM:(h + 1) * SSD_HEADDIM] = 1.0
    return jnp.asarray(np.concatenate([e] * 3, axis=0), dtype=BF16)


def _pad_lanes(v, width):
    return jnp.pad(v, (0, width - v.shape[0])).reshape(1, width)


def _layer_params(l, w_in, ssd_conv_w, ssd_conv_b, ssd_dt_bias, ssd_a_log, ssd_d, ssd_norm_w, hg_lower_bounds,
                  hg_norm_w, lru_conv_w, lru_conv_b, lru_wa, lru_ba, lru_wx, lru_bx, lru_lambda, lru_norm_w):
    wi = w_in[l]
    dt0 = SSD_INNER + SSD_INNER + 2 * SSD_GROUPS * SSD_STATE
    w_cat = jnp.concatenate([wi[:, :dt0], wi[:, dt0 + SSD_HEADS:], wi[:, dt0:dt0 + SSD_HEADS]], axis=1)
    w_cat = jnp.pad(w_cat, ((0, 0), (0, U_WIDTH - w_cat.shape[1]))).astype(BF16)
    msk = _hg_level_masks()
    return dict(
        w_cat=w_cat,
        cwx=ssd_conv_w[l][:, :SSD_INNER], cbx=ssd_conv_b[l][:SSD_INNER].reshape(1, -1),
        cwb=ssd_conv_w[l][:, SSD_INNER:], cbb=ssd_conv_b[l][SSD_INNER:].reshape(1, -1),
        dtb=_pad_lanes(ssd_dt_bias[l], LANES), alog=_pad_lanes(ssd_a_log[l], LANES),
        dful=jnp.repeat(ssd_d[l], SSD_HEADDIM).reshape(1, -1), ssd_nw=ssd_norm_w[l].reshape(1, -1),
        expand=_ssd_expand(),
        hg_lb=hg_lower_bounds, hg_nw=hg_norm_w[l].reshape(1, -1), hg_msk=msk,
        lru_cw=lru_conv_w[l], lru_cb=lru_conv_b[l].reshape(1, -1),
        lru_wa=_blockdiag_pairs(lru_wa[l]), lru_ba=lru_ba[l].reshape(1, -1),
        lru_wx=_blockdiag_pairs(lru_wx[l]), lru_bx=lru_bx[l].reshape(1, -1),
        lru_lam=lru_lambda[l].reshape(1, -1), lru_nw=lru_norm_w[l].reshape(1, -1),
    )


def kernel(x, c, ada_w, ada_b, norm_mix_w, norm_ffn_w, w_in, ssd_conv_w, ssd_conv_b, ssd_dt_bias, ssd_a_log, ssd_d, ssd_norm_w, hg_lower_bounds, hg_norm_w, lru_conv_w, lru_conv_b, lru_wa, lru_ba, lru_wx, lru_bx, lru_lambda, lru_norm_w, w_out, router_w, router_bias, exp_gate, exp_up, exp_down, sh_gate, sh_up, sh_down, final_norm_w):
    bsz, seq, d = x.shape
    depth = ada_w.shape[0]
    assert d == D_MODEL and seq % TM_OUTPROJ == 0 and seq % SSD_CHUNK == 0
    n = bsz * seq
    n_slots = n * TOP_K + N_EXPERTS * MOE_BLOCK
    x2 = x.reshape(n, d)
    mod = _adaln(c, ada_w, ada_b)
    fw = final_norm_w.reshape(1, d)
    for l in range(depth):
        p = _layer_params(l, w_in, ssd_conv_w, ssd_conv_b, ssd_dt_bias, ssd_a_log, ssd_d, ssd_norm_w,
                          hg_lower_bounds, hg_norm_w, lru_conv_w, lru_conv_b, lru_wa, lru_ba, lru_wx, lru_bx,
                          lru_lambda, lru_norm_w)
        mod3 = mod[l].reshape(bsz * N_ADA, 1, d)
        u = _inproj(x2, norm_mix_w[l].reshape(1, d), mod3, p["w_cat"], seq)
        y_ssd = _ssd(u, p, bsz, seq)
        y_hg, y_lru = _hgrn2_lru(u, p, l, bsz, seq)
        x2, hp = _outproj(y_ssd, y_hg, y_lru, x2, w_out[l].astype(BF16), norm_ffn_w[l].reshape(1, d), mod3, seq)
        eid, rank, wt, cnt = _router(hp, router_w[l].T, router_bias[l].reshape(N_EXPERTS, 1))
        dest, be, nb, pad_rows = _plan(cnt, eid, rank, n_slots)
        dest_rows = dest.reshape(TOP_K, n // LANES, LANES).transpose(1, 0, 2).reshape(n // LANES * TOP_K, LANES)
        xsp = _sc_dispatch(hp, dest_rows, pad_rows, n_slots + LANES)
        ysp = _experts(be.reshape(-1), nb[0, :1], xsp, exp_gate, exp_up, exp_down, l, n_slots)
        g = _sc_gather(ysp, dest_rows, n)
        x2 = _combine(g, wt, hp, sh_gate[l].astype(BF16), sh_up[l].astype(BF16), sh_down[l].astype(BF16),
                      x2, mod3, fw, seq, final=(l == depth - 1))
    return x2.reshape(bsz, seq, d)
```

```python
import functools

import jax
import jax.numpy as jnp
import numpy as np
from jax import lax
from jax.experimental import pallas as pl
from jax.experimental.pallas import tpu as pltpu
from jax.experimental.pallas import tpu_sc as plsc

F32 = jnp.float32
BF16 = jnp.bfloat16
HI = lax.Precision.HIGHEST
F32_TINY = float(np.finfo(np.float32).tiny)
LOG2E = float(np.log2(np.e))

LANES = 128
SUBLANES = 8
VMEM_LIMIT_BYTES = 56 * 1024 * 1024

D_MODEL = 1024
EPS = 1e-6
N_ADA = 6
CONV_WIDTH = 4
SSD_INNER = 1024
SSD_HEADDIM = 64
SSD_HEADS = 16
SSD_GROUPS = 2
SSD_STATE = 128
SSD_CHUNK = 128
SSD_GROUP_W = SSD_INNER // SSD_GROUPS
HG_WIDTH = 512
HG_EXPAND = 128
HG_HEADS = 4
HG_CHUNK = 128
HG_LEVELS = 7
LRU_WIDTH = 512
LRU_BLOCKS = 8
LRU_BLOCK_W = 64
LRU_C = 8.0
N_EXPERTS = 64
TOP_K = 8
N_EXPERT_GROUPS = 8
E_PER_GROUP = 8
TOPK_GROUPS = 4
D_EXPERT = 256
ROUTED_SCALE = 2.5
MASK_SCORE = -1.0e4

COL_Z = 0
COL_XS = 1024
COL_BC = 2048
COL_HQ = 2560
COL_HF = 3072
COL_HV = 3584
COL_HG = 4096
COL_LG = 4608
COL_LX = 5120
COL_DT = 5632
U_WIDTH = 5760

TM_INPROJ = 512
TM_OUTPROJ = 1024
TM_ROUTER = 512
TM_COMBINE = 512
SC_CORES = 2
SC_SUBCORES = 16
SC_WORKERS = SC_CORES * SC_SUBCORES
SC_WIN = LANES
SC_GATHER_BUFS = 4
MOE_BLOCK_LOG2 = 10
MOE_BLOCK = 1 << MOE_BLOCK_LOG2
XS_RING = 3
R_HG = 256


def _cparams(*sem):
    return pltpu.CompilerParams(dimension_semantics=sem, vmem_limit_bytes=VMEM_LIMIT_BYTES)


def _sigmoid(x):
    return 0.5 * jnp.tanh(0.5 * x) + 0.5


def _silu(x):
    h = 0.5 * x
    return h + h * jnp.tanh(h)


def _split3(x):
    x1 = x.astype(BF16)
    r1 = x - x1.astype(F32)
    x2 = r1.astype(BF16)
    x3 = (r1 - x2.astype(F32)).astype(BF16)
    return x1, x2, x3


def _softplus(x):
    return jnp.maximum(x, 0.0) + jnp.log1p(jnp.exp(-jnp.abs(x)))


def _norm_mod(x, nw, shift, scale):
    ms = jnp.mean(x * x, axis=-1, keepdims=True)
    y = x * lax.rsqrt(ms + EPS) * nw
    return y * (1.0 + scale) + shift


_HI16 = np.uint32(0xFFFF0000)


def _pack_bf16_pairs(x):
    half = x.shape[1] // 2
    bits = lax.bitcast_convert_type(x.astype(BF16).astype(F32), jnp.uint32)
    return (bits[:, :half] & _HI16) | (bits[:, half:] >> 16)


def _unpack_bf16_pairs(w):
    hi = lax.bitcast_convert_type(w & _HI16, F32)
    lo = lax.bitcast_convert_type(w << 16, F32)
    return jnp.concatenate([hi, lo], axis=1)


N_PLANES = D_MODEL // 2 // LANES


def _store_planes(refs, x):
    packed = _pack_bf16_pairs(x)
    for c, ref in enumerate(refs):
        ref[...] = packed[:, c * LANES:(c + 1) * LANES]


def _load_planes(refs):
    return _unpack_bf16_pairs(jnp.concatenate([ref[...] for ref in refs], axis=1))


def _ada_kernel(c_ref, w_ref, b_ref, o_ref):
    c = c_ref[...]
    o_ref[...] = jnp.dot(_silu(c), w_ref[...], precision=HI, preferred_element_type=F32) + b_ref[...]


def _adaln(c, ada_w, ada_b):
    depth, d, n6 = ada_w.shape
    bsz = c.shape[0]
    tn = 1536
    return pl.pallas_call(
        _ada_kernel,
        grid=(depth, n6 // tn),
        in_specs=[
            pl.BlockSpec((bsz, d), lambda l, j: (0, 0)),
            pl.BlockSpec((None, d, tn), lambda l, j: (l, 0, j)),
            pl.BlockSpec((None, 1, tn), lambda l, j: (l, 0, j)),
        ],
        out_specs=pl.BlockSpec((None, bsz, tn), lambda l, j: (l, 0, j)),
        out_shape=jax.ShapeDtypeStruct((depth, bsz, n6), F32),
        compiler_params=_cparams("arbitrary", "arbitrary"),
        name="adaln_mod",
    )(c, ada_w, ada_b.reshape(depth, 1, n6))


def _inproj_kernel(x_ref, nw_ref, sh_ref, sc_ref, w_ref, o_ref):
    h = _norm_mod(x_ref[...], nw_ref[...], sh_ref[...], sc_ref[...])
    o_ref[...] = jnp.dot(h.astype(BF16), w_ref[...], preferred_element_type=F32)


def _inproj(x2, nw, mod3, w_cat, seq):
    n, d = x2.shape
    tm = TM_INPROJ
    tpb = seq // tm
    return pl.pallas_call(
        _inproj_kernel,
        grid=(n // tm,),
        in_specs=[
            pl.BlockSpec((tm, d), lambda i: (i, 0)),
            pl.BlockSpec((1, d), lambda i: (0, 0)),
            pl.BlockSpec((None, 1, d), lambda i: ((i // tpb) * N_ADA + 0, 0, 0)),
            pl.BlockSpec((None, 1, d), lambda i: ((i // tpb) * N_ADA + 1, 0, 0)),
            pl.BlockSpec((d, U_WIDTH), lambda i: (0, 0), pipeline_mode=pl.Buffered(1)),
        ],
        out_specs=pl.BlockSpec((tm, U_WIDTH), lambda i: (i, 0)),
        out_shape=jax.ShapeDtypeStruct((n, U_WIDTH), F32),
        compiler_params=_cparams("arbitrary"),
        name="inproj",
    )(x2, nw, mod3, mod3, w_cat)


def _causal_conv(cur_ref, ext, tail, cw_ref, cb_ref):
    rows, width = cur_ref.shape
    cur = cur_ref[...]
    ext[0:SUBLANES, :] = tail[...]
    ext[SUBLANES:SUBLANES + rows, :] = cur
    tail[...] = cur[rows - SUBLANES:rows, :]
    groups = ext[...].reshape(rows // SUBLANES + 1, SUBLANES, width)
    sub = lax.broadcasted_iota(jnp.int32, (1, SUBLANES, width), 1)
    acc = cb_ref[...] + cur * cw_ref[CONV_WIDTH - 1:CONV_WIDTH, :]
    for d in range(1, CONV_WIDTH):
        rot = pltpu.roll(groups, d, 1)
        back = jnp.where(sub < d, rot[:-1], rot[1:]).reshape(rows, width)
        acc = acc + back * cw_ref[CONV_WIDTH - 1 - d:CONV_WIDTH - d, :]
    return acc


def _ssd_kernel(z_ref, xs_ref, bc_ref, dt_ref, cwx_ref, cbx_ref, cwb_ref, cbb_ref, dtb_ref, alog_ref,
                dful_ref, nw_ref, e_ref, o_ref, extx, extb, tailx, tailb, hstate):
    c = pl.program_id(1)
    q = SSD_CHUNK

    @pl.when(c == 0)
    def _():
        tailx[...] = jnp.zeros_like(tailx)
        tailb[...] = jnp.zeros_like(tailb)
        hstate[...] = jnp.zeros_like(hstate)

    xs = _silu(_causal_conv(xs_ref, extx, tailx, cwx_ref, cbx_ref))
    bc = _silu(_causal_conv(bc_ref, extb, tailb, cwb_ref, cbb_ref))

    dt = _softplus(dt_ref[...] + dtb_ref[...])
    a = dt * (-jnp.exp(alog_ref[...]))
    ri = lax.broadcasted_iota(jnp.int32, (q, q), 0)
    ci = lax.broadcasted_iota(jnp.int32, (q, q), 1)
    tril = ri >= ci
    tril_f = tril.astype(F32)
    tril_b = tril.astype(BF16)
    acum = jnp.dot(jnp.concatenate([tril_b] * 3, axis=1), jnp.concatenate(_split3(a), axis=0),
                   preferred_element_type=F32)
    acum_t = acum.T
    expand3 = e_ref[...]
    dt_full = jnp.dot(jnp.concatenate(_split3(dt), axis=1), expand3, preferred_element_type=F32)
    acum_full = jnp.dot(jnp.concatenate(_split3(acum), axis=1), expand3, preferred_element_type=F32)
    alast_full = acum_full[q - 1:q, :]

    xdt = xs * dt_full
    xdt_b = xdt.astype(BF16)
    exp_a = jnp.exp(acum_full)
    xd_b = (xdt * jnp.exp(alast_full - acum_full)).astype(BF16)
    state_decay = jnp.exp(alast_full)
    left = lax.broadcasted_iota(jnp.int32, (q, LANES), 1) < SSD_HEADDIM
    zero_b = jnp.zeros((q, LANES), BF16)

    ys = []
    for g in range(SSD_GROUPS):
        b_g = bc[:, g * SSD_STATE:(g + 1) * SSD_STATE]
        c_g = bc[:, (SSD_GROUPS + g) * SSD_STATE:(SSD_GROUPS + g + 1) * SSD_STATE]
        c_b = c_g.astype(BF16)
        cb = lax.dot_general(c_b, b_g.astype(BF16), (((1,), (1,)), ((), ())), preferred_element_type=F32)
        cb = cb * tril_f
        cs = slice(g * SSD_GROUP_W, (g + 1) * SSD_GROUP_W)
        h_g = hstate[:, cs]
        y_off = jnp.dot(c_b, h_g.astype(BF16), preferred_element_type=F32) * exp_a[:, cs]
        pieces = []
        for pr in range(SSD_HEADS // SSD_GROUPS // 2):
            h0 = g * (SSD_HEADS // SSD_GROUPS) + 2 * pr
            ms = []
            for h in (h0, h0 + 1):
                col = acum[:, h:h + 1]
                row = acum_t[h:h + 1, :]
                ms.append((cb * jnp.exp(jnp.minimum(col - row, 0.0))).astype(BF16))
            lhs = jnp.concatenate(ms, axis=1)
            xp = xdt_b[:, h0 * SSD_HEADDIM:(h0 + 2) * SSD_HEADDIM]
            rhs = jnp.concatenate([jnp.where(left, xp, zero_b), jnp.where(left, zero_b, xp)], axis=0)
            pieces.append(jnp.dot(lhs, rhs, preferred_element_type=F32))
        ys.append(jnp.concatenate(pieces, axis=1) + y_off)
        b_t = b_g.T.astype(BF16)
        hstate[:, cs] = h_g * state_decay[:, cs] + jnp.dot(b_t, xd_b[:, cs], preferred_element_type=F32)

    y = jnp.concatenate(ys, axis=1) + xs * dful_ref[...]
    y = y * _silu(z_ref[...])
    outs = []
    for g in range(SSD_GROUPS):
        cs = slice(g * SSD_GROUP_W, (g + 1) * SSD_GROUP_W)
        yg = y[:, cs]
        ms = jnp.mean(yg * yg, axis=-1, keepdims=True)
        outs.append(yg * lax.rsqrt(ms + EPS) * nw_ref[:, cs])
    o_ref[...] = jnp.concatenate(outs, axis=1).astype(o_ref.dtype)


def _ssd(u, p, bsz, seq):
    q = SSD_CHUNK
    nc = seq // q
    n = bsz * seq

    def rows(b, c):
        return b * nc + c

    def const(shape):
        return pl.BlockSpec(shape, lambda b, c: (0,) * len(shape))

    return pl.pallas_call(
        _ssd_kernel,
        grid=(bsz, nc),
        in_specs=[
            pl.BlockSpec((q, SSD_INNER), lambda b, c: (rows(b, c), COL_Z // SSD_INNER)),
            pl.BlockSpec((q, SSD_INNER), lambda b, c: (rows(b, c), COL_XS // SSD_INNER)),
            pl.BlockSpec((q, 512), lambda b, c: (rows(b, c), COL_BC // 512)),
            pl.BlockSpec((q, LANES), lambda b, c: (rows(b, c), COL_DT // LANES)),
            const((CONV_WIDTH, SSD_INNER)), const((1, SSD_INNER)),
            const((CONV_WIDTH, 512)), const((1, 512)),
            const((1, LANES)), const((1, LANES)),
            const((1, SSD_INNER)), const((1, SSD_INNER)),
            const((3 * LANES, SSD_INNER)),
        ],
        out_specs=pl.BlockSpec((q, SSD_INNER), lambda b, c: (rows(b, c), 0)),
        out_shape=jax.ShapeDtypeStruct((n, SSD_INNER), BF16),
        scratch_shapes=[
            pltpu.VMEM((q + SUBLANES, SSD_INNER), F32),
            pltpu.VMEM((q + SUBLANES, 512), F32),
            pltpu.VMEM((SUBLANES, SSD_INNER), F32),
            pltpu.VMEM((SUBLANES, 512), F32),
            pltpu.VMEM((SSD_STATE, SSD_INNER), F32),
        ],
        compiler_params=_cparams("arbitrary", "arbitrary"),
        name="ssd_mixer",
    )(u, u, u, u, p["cwx"], p["cbx"], p["cwb"], p["cbb"], p["dtb"], p["alog"], p["dful"], p["ssd_nw"], p["expand"])


def _boundary_rows(b, lvl):
    half = 1 << lvl
    parts = []
    for v in range(b.shape[0] // SUBLANES):
        r0 = v * SUBLANES
        if 2 * half >= SUBLANES:
            src = (r0 // (2 * half)) * (2 * half) + half - 1
            parts.append(jnp.broadcast_to(b[src:src + 1, :], (SUBLANES, b.shape[1])))
        else:
            sub = lax.broadcasted_iota(jnp.int32, (SUBLANES, b.shape[1]), 0)
            piece = None
            for g in range(SUBLANES // (2 * half)):
                src = r0 + g * 2 * half + half - 1
                cand = jnp.broadcast_to(b[src:src + 1, :], (SUBLANES, b.shape[1]))
                piece = cand if piece is None else jnp.where(sub >= g * 2 * half, cand, piece)
            parts.append(piece)
    return jnp.concatenate(parts, axis=0)


def _hgrn2_lru_kernel(layer, q_ref, f_ref, v_ref, g_ref, lb_ref, nw_ref, msk_ref, half_ref, sgn_ref,
                      lg_ref, lx_ref, cw_ref, cb_ref, wa_ref, ba_ref, wx_ref, bx_ref, lam_ref, lnw_ref,
                      o_ref, ol_ref, state_t, ext, tail, hcarry):
    t = pl.program_id(1)
    ch = HG_CHUNK

    @pl.when(t == 0)
    def _():
        state_t[...] = jnp.zeros_like(state_t)
        tail[...] = jnp.zeros_like(tail)
        hcarry[...] = jnp.zeros_like(hcarry)

    _lru_tile(lg_ref, lx_ref, cw_ref, cb_ref, wa_ref, ba_ref, wx_ref, bx_ref, lam_ref, lnw_ref, ol_ref,
              ext, tail, hcarry)

    lrows = [lb_ref[j:j + 1, :] for j in range(lb_ref.shape[0])]
    mx = functools.reduce(jnp.maximum, lrows)
    es = [jnp.exp(r - mx) for r in lrows]
    den = functools.reduce(lambda a_, b_: a_ + b_, es)
    lb = jnp.zeros_like(mx)
    for j in range(1, layer + 1):
        lb = lb + es[j] / den
    one_minus_lb = 1.0 - lb
    nw = nw_ref[...]

    ri = lax.broadcasted_iota(jnp.int32, (ch, ch), 0)
    ci = lax.broadcasted_iota(jnp.int32, (ch, ch), 1)
    tril3 = jnp.concatenate([(ri >= ci).astype(BF16)] * 3, axis=1)
    rowi = lax.broadcasted_iota(jnp.int32, (ch, HG_EXPAND), 0)
    tgt = [((rowi >> lvl) & 1) == 1 for lvl in range(HG_LEVELS)]

    def head_chunk(h, rs):
        cs = slice(h * HG_EXPAND, (h + 1) * HG_EXPAND)
        qq = _silu(q_ref[rs, cs])
        kk = one_minus_lb[:, cs] * _sigmoid(-f_ref[rs, cs])
        logf = jnp.log1p(-kk)
        vv = v_ref[rs, cs]
        vb = vv.astype(BF16)
        b = jnp.dot(tril3, jnp.concatenate(_split3(logf), axis=0),
                    preferred_element_type=F32)
        st = state_t[h]
        o = lax.dot_general((qq * jnp.exp(b)).astype(BF16), st.astype(BF16), (((1,), (1,)), ((), ())),
                            preferred_element_type=F32)
        attn = jnp.zeros((ch, ch), F32)
        for lvl in range(HG_LEVELS):
            if lvl == 0:
                qe = jnp.where(tgt[0], qq * (1.0 - kk), 0.0)
                ke = jnp.where(tgt[0], 0.0, kk)
            else:
                m = _boundary_rows(b, lvl)
                later = half_ref[lvl]
                e = jnp.exp2((b - m) * sgn_ref[lvl])
                prod = jnp.where(tgt[lvl], qq, kk) * e
                qe = prod * later
                ke = prod - qe
            prod = lax.dot_general(qe.astype(BF16), ke.astype(BF16), (((1,), (1,)), ((), ())),
                                   preferred_element_type=F32)
            attn = attn + msk_ref[lvl] * prod
        diag = jnp.sum(qq * kk, axis=-1, keepdims=True)
        o = o + jnp.dot(attn.astype(BF16), vb, preferred_element_type=F32) + diag * vv
        b_last = b[ch - 1:ch, :]
        kd = (kk * jnp.exp(b_last - b)).astype(BF16)
        state_t[h] = st * jnp.exp(b_last) + jnp.dot(vv.T.astype(BF16), kd, preferred_element_type=F32)
        ms = jnp.mean(o * o, axis=-1, keepdims=True)
        y = o * lax.rsqrt(ms + EPS) * nw[:, cs]
        o_ref[rs, cs] = (y * _silu(g_ref[rs, cs])).astype(o_ref.dtype)

    for j in range(q_ref.shape[0] // ch):
        for h in range(HG_HEADS):
            head_chunk(h, slice(j * ch, (j + 1) * ch))


def _lru_tile(g_ref, x_ref, cw_ref, cb_ref, wa_ref, ba_ref, wx_ref, bx_ref, lam_ref, nw_ref, o_ref,
              ext, tail, hcarry):
    rows = x_ref.shape[0]
    xb = _causal_conv(x_ref, ext, tail, cw_ref, cb_ref)
    xbb = xb.astype(BF16)
    npair = LRU_WIDTH // LANES
    ra = jnp.concatenate([jnp.dot(xbb[:, j * LANES:(j + 1) * LANES], wa_ref[j], preferred_element_type=F32)
                          for j in range(npair)], axis=1)
    rx = jnp.concatenate([jnp.dot(xbb[:, j * LANES:(j + 1) * LANES], wx_ref[j], preferred_element_type=F32)
                          for j in range(npair)], axis=1)
    r = _sigmoid(ra + ba_ref[...])
    i = _sigmoid(rx + bx_ref[...])
    log_a = -LRU_C * r * _softplus(-lam_ref[...])
    a = jnp.exp(log_a)
    th = jnp.tanh(log_a)
    s = -2.0 * th
    root = s * lax.rsqrt(jnp.maximum(s * (1.0 - th), F32_TINY))
    u = root * (i * xb)

    ngroups = rows // SUBLANES
    sub = lax.broadcasted_iota(jnp.int32, (1, SUBLANES, LRU_WIDTH), 1)
    acc_a = a.reshape(ngroups, SUBLANES, LRU_WIDTH)
    acc_u = u.reshape(ngroups, SUBLANES, LRU_WIDTH)
    d = 1
    while d < SUBLANES:
        keep = sub >= d
        a_sh = jnp.where(keep, pltpu.roll(acc_a, d, 1), 1.0)
        u_sh = jnp.where(keep, pltpu.roll(acc_u, d, 1), 0.0)
        acc_u = acc_a * u_sh + acc_u
        acc_a = acc_a * a_sh
        d *= 2
    acc_a = acc_a.reshape(rows, LRU_WIDTH)
    acc_u = acc_u.reshape(rows, LRU_WIDTH)
    carry = hcarry[0:1, :]
    groups = []
    for g in range(rows // SUBLANES):
        gs = slice(g * SUBLANES, (g + 1) * SUBLANES)
        hg = acc_a[gs, :] * carry + acc_u[gs, :]
        groups.append(hg)
        carry = hg[SUBLANES - 1:SUBLANES, :]
    h = jnp.concatenate(groups, axis=0)
    hcarry[0:1, :] = carry

    gate = g_ref[...]
    gelu = 0.5 * gate * (1.0 + jnp.tanh(np.sqrt(2.0 / np.pi).astype(np.float32) * (gate + 0.044715 * (gate * gate * gate))))
    y = h * gelu
    ms = jnp.mean(y * y, axis=-1, keepdims=True)
    o_ref[...] = (y * lax.rsqrt(ms + EPS) * nw_ref[...]).astype(o_ref.dtype)


def _hgrn2_lru(u, p, layer, bsz, seq):
    r = R_HG
    nt = seq // r
    n = bsz * seq
    npair = LRU_WIDTH // LANES

    def col(base, width):
        return pl.BlockSpec((r, width), lambda b, t: (b * nt + t, base // width))

    def const(shape):
        return pl.BlockSpec(shape, lambda b, t: (0,) * len(shape))

    def out(width):
        return pl.BlockSpec((r, width), lambda b, t: (b * nt + t, 0))

    return pl.pallas_call(
        functools.partial(_hgrn2_lru_kernel, layer),
        grid=(bsz, nt),
        in_specs=[
            col(COL_HQ, HG_WIDTH), col(COL_HF, HG_WIDTH), col(COL_HV, HG_WIDTH), col(COL_HG, HG_WIDTH),
            const(p["hg_lb"].shape), const((1, HG_WIDTH)), const((HG_LEVELS, HG_CHUNK, HG_CHUNK)),
            const((HG_LEVELS, HG_CHUNK, HG_EXPAND)), const((HG_LEVELS, HG_CHUNK, HG_EXPAND)),
            col(COL_LG, LRU_WIDTH), col(COL_LX, LRU_WIDTH),
            const((CONV_WIDTH, LRU_WIDTH)), const((1, LRU_WIDTH)),
            const((npair, LANES, LANES)), const((1, LRU_WIDTH)),
            const((npair, LANES, LANES)), const((1, LRU_WIDTH)),
            const((1, LRU_WIDTH)), const((1, LRU_WIDTH)),
        ],
        out_specs=[out(HG_WIDTH), out(LRU_WIDTH)],
        out_shape=[jax.ShapeDtypeStruct((n, HG_WIDTH), BF16), jax.ShapeDtypeStruct((n, LRU_WIDTH), BF16)],
        scratch_shapes=[
            pltpu.VMEM((HG_HEADS, HG_EXPAND, HG_EXPAND), F32),
            pltpu.VMEM((r + SUBLANES, LRU_WIDTH), F32),
            pltpu.VMEM((SUBLANES, LRU_WIDTH), F32),
            pltpu.VMEM((SUBLANES, LRU_WIDTH), F32),
        ],
        compiler_params=_cparams("arbitrary", "arbitrary"),
        name="hgrn2_lru_mixer",
    )(u, u, u, u, p["hg_lb"], p["hg_nw"], p["hg_msk"], p["hg_later"], p["hg_sgn"],
      u, u, p["lru_cw"], p["lru_cb"], p["lru_wa"], p["lru_ba"], p["lru_wx"], p["lru_bx"], p["lru_lam"], p["lru_nw"])


def _outproj_kernel(ys_ref, yh_ref, yl_ref, x_ref, w_ref, g_ref, nw_ref, sh_ref, sc_ref, xo_ref, *h_refs):
    acc = jnp.dot(ys_ref[...], w_ref[0:SSD_INNER, :], preferred_element_type=F32)
    acc = acc + jnp.dot(yh_ref[...], w_ref[SSD_INNER:SSD_INNER + HG_WIDTH, :], preferred_element_type=F32)
    acc = acc + jnp.dot(yl_ref[...], w_ref[SSD_INNER + HG_WIDTH:, :], preferred_element_type=F32)
    xn = x_ref[...] + g_ref[...] * acc
    xo_ref[...] = xn
    _store_planes(h_refs, _norm_mod(xn, nw_ref[...], sh_ref[...], sc_ref[...]))


def _outproj(y_ssd, y_hg, y_lru, x2, w_out, nw, mod3, seq):
    n, d = x2.shape
    tm = TM_OUTPROJ
    tpb = seq // tm

    def modspec(j):
        return pl.BlockSpec((None, 1, d), lambda i: ((i // tpb) * N_ADA + j, 0, 0))

    outs = pl.pallas_call(
        _outproj_kernel,
        grid=(n // tm,),
        in_specs=[
            pl.BlockSpec((tm, SSD_INNER), lambda i: (i, 0)),
            pl.BlockSpec((tm, HG_WIDTH), lambda i: (i, 0)),
            pl.BlockSpec((tm, LRU_WIDTH), lambda i: (i, 0)),
            pl.BlockSpec((tm, d), lambda i: (i, 0)),
            pl.BlockSpec(w_out.shape, lambda i: (0, 0)),
            modspec(2),
            pl.BlockSpec((1, d), lambda i: (0, 0)),
            modspec(3), modspec(4),
        ],
        out_specs=[pl.BlockSpec((tm, d), lambda i: (i, 0))] + [_plane_spec(tm)] * N_PLANES,
        out_shape=[jax.ShapeDtypeStruct((n, d), F32)] + [jax.ShapeDtypeStruct((n, LANES), jnp.uint32)] * N_PLANES,
        compiler_params=_cparams("arbitrary"),
        name="outproj",
    )(y_ssd, y_hg, y_lru, x2, w_out, mod3, nw, mod3, mod3)
    return outs[0], tuple(outs[1:])


def _plane_spec(tm):
    return pl.BlockSpec((tm, LANES), lambda i: (i, 0))


def _router_kernel(h0, h1, h2, h3, rw_ref, rb_ref, eid_ref, rank_ref, wt_ref, cnt_ref, carry, wscr):
    tm = h0.shape[0]

    @pl.when(pl.program_id(0) == 0)
    def _():
        carry[...] = jnp.zeros_like(carry)

    hb = _load_planes((h0, h1, h2, h3)).astype(BF16)
    logit_t = sum(lax.dot_general(part, hb, (((1,), (1,)), ((), ())), preferred_element_type=F32)
                  for part in _split3(rw_ref[...]))
    score = _sigmoid(logit_t)
    sel = score + rb_ref[...]
    neg_inf = jnp.float32(-jnp.inf)
    io_g = lax.broadcasted_iota(jnp.int32, (E_PER_GROUP, tm), 0)
    blocks, gscore = [], []
    for g in range(N_EXPERT_GROUPS):
        blk = sel[g * E_PER_GROUP:(g + 1) * E_PER_GROUP, :]
        m1 = jnp.max(blk, axis=0, keepdims=True)
        i1 = jnp.min(jnp.where(blk == m1, io_g, E_PER_GROUP), axis=0, keepdims=True)
        m2 = jnp.max(jnp.where(io_g == i1, neg_inf, blk), axis=0, keepdims=True)
        blocks.append(blk)
        gscore.append(m1 + m2)
    masked = []
    for g in range(N_EXPERT_GROUPS):
        rank = jnp.zeros((1, tm), jnp.int32)
        for o in range(N_EXPERT_GROUPS):
            if o == g:
                continue
            beats = (gscore[o] > gscore[g]) | ((gscore[o] == gscore[g]) & (o < g))
            rank = rank + beats.astype(jnp.int32)
        masked.append(jnp.where(rank < TOPK_GROUPS, blocks[g], MASK_SCORE))
    val = jnp.concatenate(masked, axis=0)
    io_e = lax.broadcasted_iota(jnp.int32, (N_EXPERTS, tm), 0)
    chosen = jnp.zeros((N_EXPERTS, tm), jnp.bool_)
    picks = []
    for k in range(TOP_K):
        m = jnp.max(val, axis=0, keepdims=True)
        idx = jnp.min(jnp.where(val == m, io_e, N_EXPERTS), axis=0, keepdims=True)
        pick = io_e == idx
        picks.append(pick)
        eid_ref[k:k + 1, :] = idx
        chosen = chosen | pick
        val = jnp.where(pick, neg_inf, val)
    w = jnp.where(chosen, score, 0.0)
    w = w / jnp.sum(w, axis=0, keepdims=True) * ROUTED_SCALE

    chosen_f = chosen.astype(F32)
    earlier = (lax.broadcasted_iota(jnp.int32, (tm, tm), 0) < lax.broadcasted_iota(jnp.int32, (tm, tm), 1))
    before = jnp.dot(chosen_f.astype(BF16), earlier.astype(BF16), preferred_element_type=F32)
    grank = carry[:, 0:1] + before
    wscr[...] = jnp.zeros_like(wscr)
    for k in range(TOP_K):
        rank_ref[k:k + 1, :] = jnp.sum(jnp.where(picks[k], grank, 0.0), axis=0, keepdims=True).astype(jnp.int32)
        wscr[k:k + 1, :] = jnp.sum(jnp.where(picks[k], w, 0.0), axis=0, keepdims=True)
    wt_ref[...] = wscr[...].T
    carry[...] = carry[...] + jnp.sum(chosen_f, axis=1, keepdims=True)
    cnt_ref[...] = carry[...]


def _router(hp, rw_t, rb):
    n = hp[0].shape[0]
    tm = TM_ROUTER
    return pl.pallas_call(
        _router_kernel,
        grid=(n // tm,),
        in_specs=[_plane_spec(tm)] * N_PLANES + [
            pl.BlockSpec(rw_t.shape, lambda i: (0, 0)),
            pl.BlockSpec((N_EXPERTS, 1), lambda i: (0, 0)),
        ],
        out_specs=[
            pl.BlockSpec((TOP_K, tm), lambda i: (0, i)),
            pl.BlockSpec((TOP_K, tm), lambda i: (0, i)),
            pl.BlockSpec((tm, LANES), lambda i: (i, 0)),
            pl.BlockSpec((N_EXPERTS, LANES), lambda i: (0, 0)),
        ],
        out_shape=[
            jax.ShapeDtypeStruct((TOP_K, n), jnp.int32),
            jax.ShapeDtypeStruct((TOP_K, n), jnp.int32),
            jax.ShapeDtypeStruct((n, LANES), F32),
            jax.ShapeDtypeStruct((N_EXPERTS, LANES), F32),
        ],
        scratch_shapes=[pltpu.VMEM((N_EXPERTS, LANES), F32), pltpu.VMEM((LANES, tm), F32)],
        compiler_params=_cparams("arbitrary"),
        name="router",
    )(*hp, rw_t, rb)


def _plan_kernel(n_slots, cnt_ref, eid_ref, rank_ref, dest_ref, be_ref, nb_ref, pad_ref):
    cnt = cnt_ref[...].astype(jnp.int32)
    padded = ((cnt + (MOE_BLOCK - 1)) >> MOE_BLOCK_LOG2) << MOE_BLOCK_LOG2
    ri = lax.broadcasted_iota(jnp.int32, (N_EXPERTS, N_EXPERTS), 0)
    ci = lax.broadcasted_iota(jnp.int32, (N_EXPERTS, N_EXPERTS), 1)
    pad_end = jnp.dot((ri >= ci).astype(F32), padded.astype(F32), precision=HI,
                      preferred_element_type=F32).astype(jnp.int32)
    pad_start = pad_end - padded
    eid = eid_ref[...]
    dest = rank_ref[...]
    for e in range(N_EXPERTS):
        dest = dest + jnp.where(eid == e, pad_start[e:e + 1, 0:1], 0)
    dest_ref[...] = dest
    nbp = be_ref.shape[1]
    jpos = lax.broadcasted_iota(jnp.int32, (N_EXPERTS, nbp), 1) * MOE_BLOCK
    be = jnp.sum((pad_end[:, 0:1] <= jpos).astype(jnp.int32), axis=0, keepdims=True)
    be_ref[...] = jnp.minimum(be, N_EXPERTS - 1)
    nb_ref[...] = pad_end[N_EXPERTS - 1:N_EXPERTS, :] >> MOE_BLOCK_LOG2
    lane = lax.broadcasted_iota(jnp.int32, (N_EXPERTS, LANES), 1)
    for r in range(MOE_BLOCK // LANES):
        s = pad_start + cnt + (r * LANES + lane)
        pad_ref[r * N_EXPERTS:(r + 1) * N_EXPERTS, :] = jnp.where(s < pad_end, s, n_slots + lane)


def _plan(cnt, eid, rank, n_slots):
    n = eid.shape[1]
    nbp = -(-(n_slots // MOE_BLOCK) // LANES) * LANES
    pad_rows = MOE_BLOCK // LANES * N_EXPERTS
    return pl.pallas_call(
        functools.partial(_plan_kernel, n_slots),
        grid=(1,),
        in_specs=[
            pl.BlockSpec(cnt.shape, lambda i: (0, 0)),
            pl.BlockSpec(eid.shape, lambda i: (0, 0)),
            pl.BlockSpec(rank.shape, lambda i: (0, 0)),
        ],
        out_specs=[
            pl.BlockSpec((TOP_K, n), lambda i: (0, 0)),
            pl.BlockSpec((1, nbp), lambda i: (0, 0)),
            pl.BlockSpec((1, LANES), lambda i: (0, 0)),
            pl.BlockSpec((pad_rows, LANES), lambda i: (0, 0)),
        ],
        out_shape=[
            jax.ShapeDtypeStruct((TOP_K, n), jnp.int32),
            jax.ShapeDtypeStruct((1, nbp), jnp.int32),
            jax.ShapeDtypeStruct((1, LANES), jnp.int32),
            jax.ShapeDtypeStruct((pad_rows, LANES), jnp.int32),
        ],
        compiler_params=_cparams("arbitrary"),
        name="moe_plan",
    )(cnt, eid, rank)


def _sc_mesh():
    return plsc.VectorSubcoreMesh(core_axis_name="c", subcore_axis_name="s")


def _sc_worker():
    return lax.axis_index("c") * SC_SUBCORES + lax.axis_index("s")


def _sc_dispatch(hp, dest_rows, pad_rows, n_rows):
    n = hp[0].shape[0]
    tiles_per_worker = n // SC_WIN // SC_WORKERS
    pad_per_worker = pad_rows.shape[0] // SC_WORKERS
    zeros = jnp.zeros((SC_WIN, LANES), jnp.uint32)

    def body(*refs):
        h = refs[:N_PLANES]
        dest_hbm, pad_hbm, z_hbm = refs[N_PLANES:N_PLANES + 3]
        xs = refs[N_PLANES + 3:2 * N_PLANES + 3]
        bufs = refs[2 * N_PLANES + 3:3 * N_PLANES + 3]
        ibuf, pbuf, sem = refs[3 * N_PLANES + 3:]
        wid = _sc_worker()

        pltpu.sync_copy(z_hbm, bufs[0])
        pltpu.sync_copy(pad_hbm.at[pl.ds(wid * pad_per_worker, pad_per_worker)], pbuf)
        copies = [pltpu.async_copy(bufs[0], xs[c].at[pbuf.at[r]], sem)
                  for r in range(pad_per_worker) for c in range(N_PLANES)]
        for cp in copies:
            cp.wait()

        @pl.loop(0, tiles_per_worker)
        def _(i):
            tile = wid * tiles_per_worker + i
            pltpu.sync_copy(dest_hbm.at[pl.ds(tile * TOP_K, TOP_K)], ibuf)
            for c in range(N_PLANES):
                pltpu.sync_copy(h[c].at[pl.ds(tile * SC_WIN, SC_WIN)], bufs[c])
            scatters = [pltpu.async_copy(bufs[c], xs[c].at[ibuf.at[k]], sem)
                        for c in range(N_PLANES) for k in range(TOP_K)]
            for cp in scatters:
                cp.wait()

    out_type = tuple(jax.ShapeDtypeStruct((n_rows, LANES), jnp.uint32) for _ in range(N_PLANES))
    scratch = ([pltpu.VMEM((SC_WIN, LANES), jnp.uint32)] * N_PLANES
               + [pltpu.VMEM((TOP_K, LANES), jnp.int32), pltpu.VMEM((pad_per_worker, LANES), jnp.int32),
                  pltpu.SemaphoreType.DMA])
    return pl.kernel(body, out_type=out_type, mesh=_sc_mesh(), scratch_types=scratch,
                     name="moe_sc_dispatch")(*hp, dest_rows, pad_rows, zeros)


def _sc_gather(ysp, dest_rows, n):
    tiles_per_worker = n // SC_WIN // SC_WORKERS

    def body(*refs):
        ys = refs[:N_PLANES]
        dest_hbm, g_hbm = refs[N_PLANES:N_PLANES + 2]
        bufs = refs[N_PLANES + 2:N_PLANES + 2 + SC_GATHER_BUFS]
        ibuf, sem = refs[N_PLANES + 2 + SC_GATHER_BUFS:]
        wid = _sc_worker()

        @pl.loop(0, tiles_per_worker)
        def _(i):
            tile = wid * tiles_per_worker + i
            pltpu.sync_copy(dest_hbm.at[pl.ds(tile * TOP_K, TOP_K)], ibuf)
            for c in range(N_PLANES):
                for k0 in range(0, TOP_K, SC_GATHER_BUFS):
                    gathers = [pltpu.async_copy(ys[c].at[ibuf.at[k0 + j]], bufs[j], sem)
                               for j in range(SC_GATHER_BUFS)]
                    for cp in gathers:
                        cp.wait()
                    stores = [pltpu.async_copy(
                        bufs[j], g_hbm.at[pl.ds(((k0 + j) * N_PLANES + c) * n + tile * SC_WIN, SC_WIN)], sem)
                        for j in range(SC_GATHER_BUFS)]
                    for cp in stores:
                        cp.wait()

    scratch = ([pltpu.VMEM((SC_WIN, LANES), jnp.uint32)] * SC_GATHER_BUFS
               + [pltpu.VMEM((TOP_K, LANES), jnp.int32), pltpu.SemaphoreType.DMA])
    return pl.kernel(body, out_type=jax.ShapeDtypeStruct((TOP_K * N_PLANES * n, LANES), jnp.uint32),
                     mesh=_sc_mesh(), scratch_types=scratch, name="moe_sc_gather")(*ysp, dest_rows)


def _expert_kernel(be_ref, nb_ref, *refs):
    xs_hbm = refs[:N_PLANES]
    wg_ref, wu_ref, wd_ref = refs[N_PLANES:N_PLANES + 3]
    ys_refs = refs[N_PLANES + 3:2 * N_PLANES + 3]
    wg_b, wu_b, wd_b, ring, sems = refs[2 * N_PLANES + 3:]
    j = pl.program_id(0)
    nb = nb_ref[0]
    used = j < nb
    new_expert = (j == 0) | (be_ref[j] != be_ref[jnp.maximum(j - 1, 0)])

    def fetch(step, slot):
        rows = pl.ds(pl.multiple_of(step * MOE_BLOCK, MOE_BLOCK), MOE_BLOCK)
        return [pltpu.make_async_copy(xs_hbm[c].at[rows], ring.at[slot, c], sems.at[slot])
                for c in range(N_PLANES)]

    for s in range(XS_RING - 1):
        @pl.when((j == 0) & (s < nb))
        def _():
            for cp in fetch(s, s):
                cp.start()

    ahead = j + (XS_RING - 1)

    @pl.when(used & (ahead < nb))
    def _():
        for cp in fetch(ahead, lax.rem(ahead, XS_RING)):
            cp.start()

    @pl.when(used & new_expert)
    def _():
        wg_b[...] = wg_ref[...].astype(BF16)
        wu_b[...] = wu_ref[...].astype(BF16)
        wd_b[...] = wd_ref[...].astype(BF16)

    @pl.when(used)
    def _():
        slot = lax.rem(j, XS_RING)
        for cp in fetch(j, slot):
            cp.wait()
        x = _unpack_bf16_pairs(jnp.concatenate([ring[slot, c] for c in range(N_PLANES)], axis=1)).astype(BF16)
        a = jnp.dot(x, wg_b[...], preferred_element_type=F32)
        u = jnp.dot(x, wu_b[...], preferred_element_type=F32)
        y = jnp.dot((_silu(a) * u).astype(BF16), wd_b[...], preferred_element_type=F32)
        _store_planes(ys_refs, y)

    @pl.when(jnp.logical_not(used))
    def _():
        for ref in ys_refs:
            ref[...] = jnp.zeros_like(ref)


def _experts(be, nb, xsp, wg, wu, wd, layer, n_slots):
    d = wg.shape[2]

    def blk(j, be_ref, nb_ref):
        return jnp.minimum(j, jnp.maximum(nb_ref[0] - 1, 0))

    def wspec(shape):
        return pl.BlockSpec((None, None) + shape,
                            lambda j, be_ref, nb_ref: (layer, be_ref[blk(j, be_ref, nb_ref)], 0, 0))

    grid_spec = pltpu.PrefetchScalarGridSpec(
        num_scalar_prefetch=2,
        grid=(n_slots // MOE_BLOCK,),
        in_specs=[pl.BlockSpec(memory_space=pl.ANY)] * N_PLANES
        + [wspec((d, D_EXPERT)), wspec((d, D_EXPERT)), wspec((D_EXPERT, d))],
        out_specs=[pl.BlockSpec((MOE_BLOCK, LANES), lambda j, be_ref, nb_ref: (j, 0))] * N_PLANES,
        scratch_shapes=[pltpu.VMEM((d, D_EXPERT), BF16), pltpu.VMEM((d, D_EXPERT), BF16),
                        pltpu.VMEM((D_EXPERT, d), BF16),
                        pltpu.VMEM((XS_RING, N_PLANES, MOE_BLOCK, LANES), jnp.uint32),
                        pltpu.SemaphoreType.DMA((XS_RING,))],
    )
    return pl.pallas_call(
        _expert_kernel,
        grid_spec=grid_spec,
        out_shape=[jax.ShapeDtypeStruct((n_slots, LANES), jnp.uint32)] * N_PLANES,
        compiler_params=_cparams("arbitrary"),
        name="moe_experts",
    )(be, nb, *xsp, wg, wu, wd)


def _combine_kernel(final, g_ref, wt_ref, h0, h1, h2, h3, sg_ref, su_ref, sd_ref, x_ref, gate_ref, fw_ref, o_ref):
    hb = _load_planes((h0, h1, h2, h3)).astype(BF16)
    a = jnp.dot(hb, sg_ref[...], preferred_element_type=F32)
    u = jnp.dot(hb, su_ref[...], preferred_element_type=F32)
    acc = jnp.dot((_silu(a) * u).astype(BF16), sd_ref[...], preferred_element_type=F32)
    wt = wt_ref[...]
    for k in range(TOP_K):
        rows = _unpack_bf16_pairs(jnp.concatenate([g_ref[k * N_PLANES + c] for c in range(N_PLANES)], axis=1))
        acc = acc + wt[:, k:k + 1] * rows
    xn = x_ref[...] + gate_ref[...] * acc
    if final:
        ms = jnp.mean(xn * xn, axis=-1, keepdims=True)
        xn = xn * lax.rsqrt(ms + EPS) * fw_ref[...]
    o_ref[...] = xn


def _combine(g, wt, hp, sg, su, sd, x2, mod3, fw, seq, final):
    n, d = x2.shape
    tm = TM_COMBINE
    tpb = seq // tm
    return pl.pallas_call(
        functools.partial(_combine_kernel, final),
        grid=(n // tm,),
        in_specs=[
            pl.BlockSpec((TOP_K * N_PLANES, tm, LANES), lambda i: (0, i, 0)),
            pl.BlockSpec((tm, LANES), lambda i: (i, 0)),
        ] + [_plane_spec(tm)] * N_PLANES + [
            pl.BlockSpec(sg.shape, lambda i: (0, 0)),
            pl.BlockSpec(su.shape, lambda i: (0, 0)),
            pl.BlockSpec(sd.shape, lambda i: (0, 0)),
            pl.BlockSpec((tm, d), lambda i: (i, 0)),
            pl.BlockSpec((None, 1, d), lambda i: ((i // tpb) * N_ADA + 5, 0, 0)),
            pl.BlockSpec((1, d), lambda i: (0, 0)),
        ],
        out_specs=pl.BlockSpec((tm, d), lambda i: (i, 0)),
        out_shape=jax.ShapeDtypeStruct((n, d), F32),
        compiler_params=_cparams("arbitrary"),
        name="moe_combine",
    )(g.reshape(TOP_K * N_PLANES, n, LANES), wt, *hp, sg, su, sd, x2, mod3, fw)


def _blockdiag_pairs(w):
    z = jnp.zeros((LRU_BLOCK_W, LRU_BLOCK_W), w.dtype)
    tiles = []
    for j in range(LRU_BLOCKS // 2):
        top = jnp.concatenate([w[2 * j], z], axis=1)
        bot = jnp.concatenate([z, w[2 * j + 1]], axis=1)
        tiles.append(jnp.concatenate([top, bot], axis=0))
    return jnp.stack(tiles).astype(BF16)


def _hg_level_masks():
    ch = HG_CHUNK
    msk = np.zeros((HG_LEVELS, ch, ch), np.float32)
    for lvl in range(HG_LEVELS):
        half = 1 << lvl
        for t in range(ch):
            base = (t // (2 * half)) * (2 * half)
            if (t // half) % 2 == 1:
                msk[lvl, t, base:base + half] = 1.0
    later = ((np.arange(ch)[None, :, None] >> np.arange(HG_LEVELS)[:, None, None]) & 1).astype(np.float32)
    later = np.broadcast_to(later, (HG_LEVELS, ch, HG_EXPAND))
    sgn = (2.0 * later - 1.0) * np.float32(LOG2E)
    return jnp.asarray(msk), jnp.asarray(later), jnp.asarray(sgn, dtype=F32)


def _ssd_expand():
    e = np.zeros((LANES, SSD_INNER), np.float32)
    for h in range(SSD_HEADS):
        e[h, h * SSD_HEADDIM:(h + 1) * SSD_HEADDIM] = 1.0
    return jnp.asarray(np.concatenate([e] * 3, axis=0), dtype=BF16)


def _pad_lanes(v, width):
    return jnp.pad(v, (0, width - v.shape[0])).reshape(1, width)


def _layer_params(l, w_in, ssd_conv_w, ssd_conv_b, ssd_dt_bias, ssd_a_log, ssd_d, ssd_norm_w, hg_lower_bounds,
                  hg_norm_w, lru_conv_w, lru_conv_b, lru_wa, lru_ba, lru_wx, lru_bx, lru_lambda, lru_norm_w):
    wi = w_in[l]
    dt0 = SSD_INNER + SSD_INNER + 2 * SSD_GROUPS * SSD_STATE
    w_cat = jnp.concatenate([wi[:, :dt0], wi[:, dt0 + SSD_HEADS:], wi[:, dt0:dt0 + SSD_HEADS]], axis=1)
    w_cat = jnp.pad(w_cat, ((0, 0), (0, U_WIDTH - w_cat.shape[1]))).astype(BF16)
    msk, later, sgn = _hg_level_masks()
    return dict(
        w_cat=w_cat,
        cwx=ssd_conv_w[l][:, :SSD_INNER], cbx=ssd_conv_b[l][:SSD_INNER].reshape(1, -1),
        cwb=ssd_conv_w[l][:, SSD_INNER:], cbb=ssd_conv_b[l][SSD_INNER:].reshape(1, -1),
        dtb=_pad_lanes(ssd_dt_bias[l], LANES), alog=_pad_lanes(ssd_a_log[l], LANES),
        dful=jnp.repeat(ssd_d[l], SSD_HEADDIM).reshape(1, -1), ssd_nw=ssd_norm_w[l].reshape(1, -1),
        expand=_ssd_expand(),
        hg_lb=hg_lower_bounds, hg_nw=hg_norm_w[l].reshape(1, -1), hg_msk=msk, hg_later=later, hg_sgn=sgn,
        lru_cw=lru_conv_w[l], lru_cb=lru_conv_b[l].reshape(1, -1),
        lru_wa=_blockdiag_pairs(lru_wa[l]), lru_ba=lru_ba[l].reshape(1, -1),
        lru_wx=_blockdiag_pairs(lru_wx[l]), lru_bx=lru_bx[l].reshape(1, -1),
        lru_lam=lru_lambda[l].reshape(1, -1), lru_nw=lru_norm_w[l].reshape(1, -1),
    )


def kernel(x, c, ada_w, ada_b, norm_mix_w, norm_ffn_w, w_in, ssd_conv_w, ssd_conv_b, ssd_dt_bias, ssd_a_log, ssd_d, ssd_norm_w, hg_lower_bounds, hg_norm_w, lru_conv_w, lru_conv_b, lru_wa, lru_ba, lru_wx, lru_bx, lru_lambda, lru_norm_w, w_out, router_w, router_bias, exp_gate, exp_up, exp_down, sh_gate, sh_up, sh_down, final_norm_w):
    bsz, seq, d = x.shape
    depth = ada_w.shape[0]
    assert d == D_MODEL and seq % TM_OUTPROJ == 0 and seq % SSD_CHUNK == 0
    n = bsz * seq
    n_slots = n * TOP_K + N_EXPERTS * MOE_BLOCK
    x2 = x.reshape(n, d)
    mod = _adaln(c, ada_w, ada_b)
    fw = final_norm_w.reshape(1, d)
    for l in range(depth):
        p = _layer_params(l, w_in, ssd_conv_w, ssd_conv_b, ssd_dt_bias, ssd_a_log, ssd_d, ssd_norm_w,
                          hg_lower_bounds, hg_norm_w, lru_conv_w, lru_conv_b, lru_wa, lru_ba, lru_wx, lru_bx,
                          lru_lambda, lru_norm_w)
        mod3 = mod[l].reshape(bsz * N_ADA, 1, d)
        u = _inproj(x2, norm_mix_w[l].reshape(1, d), mod3, p["w_cat"], seq)
        y_ssd = _ssd(u, p, bsz, seq)
        y_hg, y_lru = _hgrn2_lru(u, p, l, bsz, seq)
        x2, hp = _outproj(y_ssd, y_hg, y_lru, x2, w_out[l].astype(BF16), norm_ffn_w[l].reshape(1, d), mod3, seq)
        eid, rank, wt, cnt = _router(hp, router_w[l].T, router_bias[l].reshape(N_EXPERTS, 1))
        dest, be, nb, pad_rows = _plan(cnt, eid, rank, n_slots)
        dest_rows = dest.reshape(TOP_K, n // LANES, LANES).transpose(1, 0, 2).reshape(n // LANES * TOP_K, LANES)
        xsp = _sc_dispatch(hp, dest_rows, pad_rows, n_slots + LANES)
        ysp = _experts(be.reshape(-1), nb[0, :1], xsp, exp_gate, exp_up, exp_down, l, n_slots)
        g = _sc_gather(ysp, dest_rows, n)
        x2 = _combine(g, wt, hp, sh_gate[l].astype(BF16), sh_up[l].astype(BF16), sh_down[l].astype(BF16),
                      x2, mod3, fw, seq, final=(l == depth - 1))
    return x2.reshape(bsz, seq, d)
```

```python
import functools

import jax
import jax.numpy as jnp
import numpy as np
from jax import lax
from jax.experimental import pallas as pl
from jax.experimental.pallas import tpu as pltpu
from jax.experimental.pallas import tpu_sc as plsc

F32 = jnp.float32
BF16 = jnp.bfloat16
HI = lax.Precision.HIGHEST
F32_TINY = float(np.finfo(np.float32).tiny)
LOG2E = float(np.log2(np.e))

LANES = 128
SUBLANES = 8
VMEM_LIMIT_BYTES = 56 * 1024 * 1024

D_MODEL = 1024
EPS = 1e-6
N_ADA = 6
CONV_WIDTH = 4
SSD_INNER = 1024
SSD_HEADDIM = 64
SSD_HEADS = 16
SSD_GROUPS = 2
SSD_STATE = 128
SSD_CHUNK = 128
SSD_GROUP_W = SSD_INNER // SSD_GROUPS
HG_WIDTH = 512
HG_EXPAND = 128
HG_HEADS = 4
HG_CHUNK = 128
HG_LEVELS = 7
LRU_WIDTH = 512
LRU_BLOCKS = 8
LRU_BLOCK_W = 64
LRU_C = 8.0
N_EXPERTS = 64
TOP_K = 8
N_EXPERT_GROUPS = 8
E_PER_GROUP = 8
TOPK_GROUPS = 4
D_EXPERT = 256
ROUTED_SCALE = 2.5
MASK_SCORE = -1.0e4

COL_Z = 0
COL_XS = 1024
COL_BC = 2048
COL_HQ = 2560
COL_HF = 3072
COL_HV = 3584
COL_HG = 4096
COL_LG = 4608
COL_LX = 5120
COL_DT = 5632
U_WIDTH = 5760

TM_INPROJ = 512
TM_OUTPROJ = 1024
TM_ROUTER = 512
TM_COMBINE = 512
SC_CORES = 2
SC_SUBCORES = 16
SC_WORKERS = SC_CORES * SC_SUBCORES
SC_WIN = LANES
SC_GATHER_BUFS = 4
MOE_BLOCK_LOG2 = 9
MOE_BLOCK = 1 << MOE_BLOCK_LOG2
XS_RING = 3
R_HG = 256


def _cparams(*sem):
    return pltpu.CompilerParams(dimension_semantics=sem, vmem_limit_bytes=VMEM_LIMIT_BYTES)


def _sigmoid(x):
    return 0.5 * jnp.tanh(0.5 * x) + 0.5


def _silu(x):
    h = 0.5 * x
    return h + h * jnp.tanh(h)


def _split3(x):
    x1 = x.astype(BF16)
    r1 = x - x1.astype(F32)
    x2 = r1.astype(BF16)
    x3 = (r1 - x2.astype(F32)).astype(BF16)
    return x1, x2, x3


def _softplus(x):
    return jnp.maximum(x, 0.0) + jnp.log1p(jnp.exp(-jnp.abs(x)))


def _norm_mod(x, nw, shift, scale):
    ms = jnp.mean(x * x, axis=-1, keepdims=True)
    y = x * lax.rsqrt(ms + EPS) * nw
    return y * (1.0 + scale) + shift


_HI16 = np.uint32(0xFFFF0000)


def _pack_bf16_pairs(x):
    half = x.shape[1] // 2
    bits = lax.bitcast_convert_type(x.astype(BF16).astype(F32), jnp.uint32)
    return (bits[:, :half] & _HI16) | (bits[:, half:] >> 16)


def _unpack_bf16_pairs(w):
    hi = lax.bitcast_convert_type(w & _HI16, F32)
    lo = lax.bitcast_convert_type(w << 16, F32)
    return jnp.concatenate([hi, lo], axis=1)


N_PLANES = D_MODEL // 2 // LANES


def _store_planes(refs, x):
    packed = _pack_bf16_pairs(x)
    for c, ref in enumerate(refs):
        ref[...] = packed[:, c * LANES:(c + 1) * LANES]


def _load_planes(refs):
    return _unpack_bf16_pairs(jnp.concatenate([ref[...] for ref in refs], axis=1))


def _ada_kernel(c_ref, w_ref, b_ref, o_ref):
    c = c_ref[...]
    o_ref[...] = jnp.dot(_silu(c), w_ref[...], precision=HI, preferred_element_type=F32) + b_ref[...]


def _adaln(c, ada_w, ada_b):
    depth, d, n6 = ada_w.shape
    bsz = c.shape[0]
    tn = 1536
    return pl.pallas_call(
        _ada_kernel,
        grid=(depth, n6 // tn),
        in_specs=[
            pl.BlockSpec((bsz, d), lambda l, j: (0, 0)),
            pl.BlockSpec((None, d, tn), lambda l, j: (l, 0, j)),
            pl.BlockSpec((None, 1, tn), lambda l, j: (l, 0, j)),
        ],
        out_specs=pl.BlockSpec((None, bsz, tn), lambda l, j: (l, 0, j)),
        out_shape=jax.ShapeDtypeStruct((depth, bsz, n6), F32),
        compiler_params=_cparams("arbitrary", "arbitrary"),
        name="adaln_mod",
    )(c, ada_w, ada_b.reshape(depth, 1, n6))


def _inproj_kernel(x_ref, nw_ref, sh_ref, sc_ref, w_ref, o_ref):
    h = _norm_mod(x_ref[...], nw_ref[...], sh_ref[...], sc_ref[...])
    o_ref[...] = jnp.dot(h.astype(BF16), w_ref[...], preferred_element_type=F32)


def _inproj(x2, nw, mod3, w_cat, seq):
    n, d = x2.shape
    tm = TM_INPROJ
    tpb = seq // tm
    return pl.pallas_call(
        _inproj_kernel,
        grid=(n // tm,),
        in_specs=[
            pl.BlockSpec((tm, d), lambda i: (i, 0)),
            pl.BlockSpec((1, d), lambda i: (0, 0)),
            pl.BlockSpec((None, 1, d), lambda i: ((i // tpb) * N_ADA + 0, 0, 0)),
            pl.BlockSpec((None, 1, d), lambda i: ((i // tpb) * N_ADA + 1, 0, 0)),
            pl.BlockSpec((d, U_WIDTH), lambda i: (0, 0), pipeline_mode=pl.Buffered(1)),
        ],
        out_specs=pl.BlockSpec((tm, U_WIDTH), lambda i: (i, 0)),
        out_shape=jax.ShapeDtypeStruct((n, U_WIDTH), F32),
        compiler_params=_cparams("arbitrary"),
        name="inproj",
    )(x2, nw, mod3, mod3, w_cat)


def _causal_conv(cur_ref, ext, tail, cw_ref, cb_ref):
    rows, width = cur_ref.shape
    cur = cur_ref[...]
    ext[0:SUBLANES, :] = tail[...]
    ext[SUBLANES:SUBLANES + rows, :] = cur
    tail[...] = cur[rows - SUBLANES:rows, :]
    groups = ext[...].reshape(rows // SUBLANES + 1, SUBLANES, width)
    sub = lax.broadcasted_iota(jnp.int32, (1, SUBLANES, width), 1)
    acc = cb_ref[...] + cur * cw_ref[CONV_WIDTH - 1:CONV_WIDTH, :]
    for d in range(1, CONV_WIDTH):
        rot = pltpu.roll(groups, d, 1)
        back = jnp.where(sub < d, rot[:-1], rot[1:]).reshape(rows, width)
        acc = acc + back * cw_ref[CONV_WIDTH - 1 - d:CONV_WIDTH - d, :]
    return acc


def _ssd_kernel(z_ref, xs_ref, bc_ref, dt_ref, cwx_ref, cbx_ref, cwb_ref, cbb_ref, dtb_ref, alog_ref,
                dful_ref, nw_ref, e_ref, o_ref, extx, extb, tailx, tailb, hstate):
    c = pl.program_id(1)
    q = SSD_CHUNK

    @pl.when(c == 0)
    def _():
        tailx[...] = jnp.zeros_like(tailx)
        tailb[...] = jnp.zeros_like(tailb)
        hstate[...] = jnp.zeros_like(hstate)

    xs = _silu(_causal_conv(xs_ref, extx, tailx, cwx_ref, cbx_ref))
    bc = _silu(_causal_conv(bc_ref, extb, tailb, cwb_ref, cbb_ref))

    dt = _softplus(dt_ref[...] + dtb_ref[...])
    a = dt * (-jnp.exp(alog_ref[...]))
    ri = lax.broadcasted_iota(jnp.int32, (q, q), 0)
    ci = lax.broadcasted_iota(jnp.int32, (q, q), 1)
    tril = ri >= ci
    tril_f = tril.astype(F32)
    tril_b = tril.astype(BF16)
    acum = jnp.dot(jnp.concatenate([tril_b] * 3, axis=1), jnp.concatenate(_split3(a), axis=0),
                   preferred_element_type=F32)
    acum_t = acum.T
    expand3 = e_ref[...]
    dt_full = jnp.dot(jnp.concatenate(_split3(dt), axis=1), expand3, preferred_element_type=F32)
    acum_full = jnp.dot(jnp.concatenate(_split3(acum), axis=1), expand3, preferred_element_type=F32)
    alast_full = acum_full[q - 1:q, :]

    xdt = xs * dt_full
    xdt_b = xdt.astype(BF16)
    exp_a = jnp.exp(acum_full)
    xd_b = (xdt * jnp.exp(alast_full - acum_full)).astype(BF16)
    state_decay = jnp.exp(alast_full)
    left = lax.broadcasted_iota(jnp.int32, (q, LANES), 1) < SSD_HEADDIM
    zero_b = jnp.zeros((q, LANES), BF16)

    ys = []
    for g in range(SSD_GROUPS):
        b_g = bc[:, g * SSD_STATE:(g + 1) * SSD_STATE]
        c_g = bc[:, (SSD_GROUPS + g) * SSD_STATE:(SSD_GROUPS + g + 1) * SSD_STATE]
        c_b = c_g.astype(BF16)
        cb = lax.dot_general(c_b, b_g.astype(BF16), (((1,), (1,)), ((), ())), preferred_element_type=F32)
        cb = cb * tril_f
        cs = slice(g * SSD_GROUP_W, (g + 1) * SSD_GROUP_W)
        h_g = hstate[:, cs]
        y_off = jnp.dot(c_b, h_g.astype(BF16), preferred_element_type=F32) * exp_a[:, cs]
        pieces = []
        for pr in range(SSD_HEADS // SSD_GROUPS // 2):
            h0 = g * (SSD_HEADS // SSD_GROUPS) + 2 * pr
            ms = []
            for h in (h0, h0 + 1):
                col = acum[:, h:h + 1]
                row = acum_t[h:h + 1, :]
                ms.append((cb * jnp.exp(jnp.minimum(col - row, 0.0))).astype(BF16))
            lhs = jnp.concatenate(ms, axis=1)
            xp = xdt_b[:, h0 * SSD_HEADDIM:(h0 + 2) * SSD_HEADDIM]
            rhs = jnp.concatenate([jnp.where(left, xp, zero_b), jnp.where(left, zero_b, xp)], axis=0)
            pieces.append(jnp.dot(lhs, rhs, preferred_element_type=F32))
        ys.append(jnp.concatenate(pieces, axis=1) + y_off)
        b_t = b_g.T.astype(BF16)
        hstate[:, cs] = h_g * state_decay[:, cs] + jnp.dot(b_t, xd_b[:, cs], preferred_element_type=F32)

    y = jnp.concatenate(ys, axis=1) + xs * dful_ref[...]
    y = y * _silu(z_ref[...])
    outs = []
    for g in range(SSD_GROUPS):
        cs = slice(g * SSD_GROUP_W, (g + 1) * SSD_GROUP_W)
        yg = y[:, cs]
        ms = jnp.mean(yg * yg, axis=-1, keepdims=True)
        outs.append(yg * lax.rsqrt(ms + EPS) * nw_ref[:, cs])
    o_ref[...] = jnp.concatenate(outs, axis=1).astype(o_ref.dtype)


def _ssd(u, p, bsz, seq):
    q = SSD_CHUNK
    nc = seq // q
    n = bsz * seq

    def rows(b, c):
        return b * nc + c

    def const(shape):
        return pl.BlockSpec(shape, lambda b, c: (0,) * len(shape))

    return pl.pallas_call(
        _ssd_kernel,
        grid=(bsz, nc),
        in_specs=[
            pl.BlockSpec((q, SSD_INNER), lambda b, c: (rows(b, c), COL_Z // SSD_INNER)),
            pl.BlockSpec((q, SSD_INNER), lambda b, c: (rows(b, c), COL_XS // SSD_INNER)),
            pl.BlockSpec((q, 512), lambda b, c: (rows(b, c), COL_BC // 512)),
            pl.BlockSpec((q, LANES), lambda b, c: (rows(b, c), COL_DT // LANES)),
            const((CONV_WIDTH, SSD_INNER)), const((1, SSD_INNER)),
            const((CONV_WIDTH, 512)), const((1, 512)),
            const((1, LANES)), const((1, LANES)),
            const((1, SSD_INNER)), const((1, SSD_INNER)),
            const((3 * LANES, SSD_INNER)),
        ],
        out_specs=pl.BlockSpec((q, SSD_INNER), lambda b, c: (rows(b, c), 0)),
        out_shape=jax.ShapeDtypeStruct((n, SSD_INNER), BF16),
        scratch_shapes=[
            pltpu.VMEM((q + SUBLANES, SSD_INNER), F32),
            pltpu.VMEM((q + SUBLANES, 512), F32),
            pltpu.VMEM((SUBLANES, SSD_INNER), F32),
            pltpu.VMEM((SUBLANES, 512), F32),
            pltpu.VMEM((SSD_STATE, SSD_INNER), F32),
        ],
        compiler_params=_cparams("arbitrary", "arbitrary"),
        name="ssd_mixer",
    )(u, u, u, u, p["cwx"], p["cbx"], p["cwb"], p["cbb"], p["dtb"], p["alog"], p["dful"], p["ssd_nw"], p["expand"])


def _boundary_rows(b, lvl):
    half = 1 << lvl
    parts = []
    for v in range(b.shape[0] // SUBLANES):
        r0 = v * SUBLANES
        if 2 * half >= SUBLANES:
            src = (r0 // (2 * half)) * (2 * half) + half - 1
            parts.append(jnp.broadcast_to(b[src:src + 1, :], (SUBLANES, b.shape[1])))
        else:
            sub = lax.broadcasted_iota(jnp.int32, (SUBLANES, b.shape[1]), 0)
            piece = None
            for g in range(SUBLANES // (2 * half)):
                src = r0 + g * 2 * half + half - 1
                cand = jnp.broadcast_to(b[src:src + 1, :], (SUBLANES, b.shape[1]))
                piece = cand if piece is None else jnp.where(sub >= g * 2 * half, cand, piece)
            parts.append(piece)
    return jnp.concatenate(parts, axis=0)


def _hgrn2_lru_kernel(layer, q_ref, f_ref, v_ref, g_ref, lb_ref, nw_ref, msk_ref, half_ref, sgn_ref,
                      lg_ref, lx_ref, cw_ref, cb_ref, wa_ref, ba_ref, wx_ref, bx_ref, lam_ref, lnw_ref,
                      o_ref, ol_ref, state_t, ext, tail, hcarry):
    t = pl.program_id(1)
    ch = HG_CHUNK

    @pl.when(t == 0)
    def _():
        state_t[...] = jnp.zeros_like(state_t)
        tail[...] = jnp.zeros_like(tail)
        hcarry[...] = jnp.zeros_like(hcarry)

    _lru_tile(lg_ref, lx_ref, cw_ref, cb_ref, wa_ref, ba_ref, wx_ref, bx_ref, lam_ref, lnw_ref, ol_ref,
              ext, tail, hcarry)

    lrows = [lb_ref[j:j + 1, :] for j in range(lb_ref.shape[0])]
    mx = functools.reduce(jnp.maximum, lrows)
    es = [jnp.exp(r - mx) for r in lrows]
    den = functools.reduce(lambda a_, b_: a_ + b_, es)
    lb = jnp.zeros_like(mx)
    for j in range(1, layer + 1):
        lb = lb + es[j] / den
    one_minus_lb = 1.0 - lb
    nw = nw_ref[...]

    ri = lax.broadcasted_iota(jnp.int32, (ch, ch), 0)
    ci = lax.broadcasted_iota(jnp.int32, (ch, ch), 1)
    tril3 = jnp.concatenate([(ri >= ci).astype(BF16)] * 3, axis=1)
    rowi = lax.broadcasted_iota(jnp.int32, (ch, HG_EXPAND), 0)
    tgt = [((rowi >> lvl) & 1) == 1 for lvl in range(HG_LEVELS)]

    def head_chunk(h, rs):
        cs = slice(h * HG_EXPAND, (h + 1) * HG_EXPAND)
        qq = _silu(q_ref[rs, cs])
        kk = one_minus_lb[:, cs] * _sigmoid(-f_ref[rs, cs])
        logf = jnp.log1p(-kk)
        vv = v_ref[rs, cs]
        vb = vv.astype(BF16)
        b = jnp.dot(tril3, jnp.concatenate(_split3(logf), axis=0),
                    preferred_element_type=F32)
        st = state_t[h]
        o = lax.dot_general((qq * jnp.exp(b)).astype(BF16), st.astype(BF16), (((1,), (1,)), ((), ())),
                            preferred_element_type=F32)
        attn = jnp.zeros((ch, ch), F32)
        for lvl in range(HG_LEVELS):
            if lvl == 0:
                qe = jnp.where(tgt[0], qq * (1.0 - kk), 0.0)
                ke = jnp.where(tgt[0], 0.0, kk)
            else:
                m = _boundary_rows(b, lvl)
                later = half_ref[lvl]
                e = jnp.exp2((b - m) * sgn_ref[lvl])
                prod = jnp.where(tgt[lvl], qq, kk) * e
                qe = prod * later
                ke = prod - qe
            prod = lax.dot_general(qe.astype(BF16), ke.astype(BF16), (((1,), (1,)), ((), ())),
                                   preferred_element_type=F32)
            attn = attn + msk_ref[lvl] * prod
        diag = jnp.sum(qq * kk, axis=-1, keepdims=True)
        o = o + jnp.dot(attn.astype(BF16), vb, preferred_element_type=F32) + diag * vv
        b_last = b[ch - 1:ch, :]
        kd = (kk * jnp.exp(b_last - b)).astype(BF16)
        state_t[h] = st * jnp.exp(b_last) + jnp.dot(vv.T.astype(BF16), kd, preferred_element_type=F32)
        ms = jnp.mean(o * o, axis=-1, keepdims=True)
        y = o * lax.rsqrt(ms + EPS) * nw[:, cs]
        o_ref[rs, cs] = (y * _silu(g_ref[rs, cs])).astype(o_ref.dtype)

    for j in range(q_ref.shape[0] // ch):
        for h in range(HG_HEADS):
            head_chunk(h, slice(j * ch, (j + 1) * ch))


def _lru_tile(g_ref, x_ref, cw_ref, cb_ref, wa_ref, ba_ref, wx_ref, bx_ref, lam_ref, nw_ref, o_ref,
              ext, tail, hcarry):
    rows = x_ref.shape[0]
    xb = _causal_conv(x_ref, ext, tail, cw_ref, cb_ref)
    xbb = xb.astype(BF16)
    npair = LRU_WIDTH // LANES
    ra = jnp.concatenate([jnp.dot(xbb[:, j * LANES:(j + 1) * LANES], wa_ref[j], preferred_element_type=F32)
                          for j in range(npair)], axis=1)
    rx = jnp.concatenate([jnp.dot(xbb[:, j * LANES:(j + 1) * LANES], wx_ref[j], preferred_element_type=F32)
                          for j in range(npair)], axis=1)
    r = _sigmoid(ra + ba_ref[...])
    i = _sigmoid(rx + bx_ref[...])
    log_a = -LRU_C * r * _softplus(-lam_ref[...])
    a = jnp.exp(log_a)
    th = jnp.tanh(log_a)
    s = -2.0 * th
    root = s * lax.rsqrt(jnp.maximum(s * (1.0 - th), F32_TINY))
    u = root * (i * xb)

    ngroups = rows // SUBLANES
    sub = lax.broadcasted_iota(jnp.int32, (1, SUBLANES, LRU_WIDTH), 1)
    acc_a = a.reshape(ngroups, SUBLANES, LRU_WIDTH)
    acc_u = u.reshape(ngroups, SUBLANES, LRU_WIDTH)
    d = 1
    while d < SUBLANES:
        keep = sub >= d
        a_sh = jnp.where(keep, pltpu.roll(acc_a, d, 1), 1.0)
        u_sh = jnp.where(keep, pltpu.roll(acc_u, d, 1), 0.0)
        acc_u = acc_a * u_sh + acc_u
        acc_a = acc_a * a_sh
        d *= 2
    acc_a = acc_a.reshape(rows, LRU_WIDTH)
    acc_u = acc_u.reshape(rows, LRU_WIDTH)
    carry = hcarry[0:1, :]
    groups = []
    for g in range(rows // SUBLANES):
        gs = slice(g * SUBLANES, (g + 1) * SUBLANES)
        hg = acc_a[gs, :] * carry + acc_u[gs, :]
        groups.append(hg)
        carry = hg[SUBLANES - 1:SUBLANES, :]
    h = jnp.concatenate(groups, axis=0)
    hcarry[0:1, :] = carry

    gate = g_ref[...]
    gelu = 0.5 * gate * (1.0 + jnp.tanh(np.sqrt(2.0 / np.pi).astype(np.float32) * (gate + 0.044715 * (gate * gate * gate))))
    y = h * gelu
    ms = jnp.mean(y * y, axis=-1, keepdims=True)
    o_ref[...] = (y * lax.rsqrt(ms + EPS) * nw_ref[...]).astype(o_ref.dtype)


def _hgrn2_lru(u, p, layer, bsz, seq):
    r = R_HG
    nt = seq // r
    n = bsz * seq
    npair = LRU_WIDTH // LANES

    def col(base, width):
        return pl.BlockSpec((r, width), lambda b, t: (b * nt + t, base // width))

    def const(shape):
        return pl.BlockSpec(shape, lambda b, t: (0,) * len(shape))

    def out(width):
        return pl.BlockSpec((r, width), lambda b, t: (b * nt + t, 0))

    return pl.pallas_call(
        functools.partial(_hgrn2_lru_kernel, layer),
        grid=(bsz, nt),
        in_specs=[
            col(COL_HQ, HG_WIDTH), col(COL_HF, HG_WIDTH), col(COL_HV, HG_WIDTH), col(COL_HG, HG_WIDTH),
            const(p["hg_lb"].shape), const((1, HG_WIDTH)), const((HG_LEVELS, HG_CHUNK, HG_CHUNK)),
            const((HG_LEVELS, HG_CHUNK, HG_EXPAND)), const((HG_LEVELS, HG_CHUNK, HG_EXPAND)),
            col(COL_LG, LRU_WIDTH), col(COL_LX, LRU_WIDTH),
            const((CONV_WIDTH, LRU_WIDTH)), const((1, LRU_WIDTH)),
            const((npair, LANES, LANES)), const((1, LRU_WIDTH)),
            const((npair, LANES, LANES)), const((1, LRU_WIDTH)),
            const((1, LRU_WIDTH)), const((1, LRU_WIDTH)),
        ],
        out_specs=[out(HG_WIDTH), out(LRU_WIDTH)],
        out_shape=[jax.ShapeDtypeStruct((n, HG_WIDTH), BF16), jax.ShapeDtypeStruct((n, LRU_WIDTH), BF16)],
        scratch_shapes=[
            pltpu.VMEM((HG_HEADS, HG_EXPAND, HG_EXPAND), F32),
            pltpu.VMEM((r + SUBLANES, LRU_WIDTH), F32),
            pltpu.VMEM((SUBLANES, LRU_WIDTH), F32),
            pltpu.VMEM((SUBLANES, LRU_WIDTH), F32),
        ],
        compiler_params=_cparams("arbitrary", "arbitrary"),
        name="hgrn2_lru_mixer",
    )(u, u, u, u, p["hg_lb"], p["hg_nw"], p["hg_msk"], p["hg_later"], p["hg_sgn"],
      u, u, p["lru_cw"], p["lru_cb"], p["lru_wa"], p["lru_ba"], p["lru_wx"], p["lru_bx"], p["lru_lam"], p["lru_nw"])


def _outproj_kernel(ys_ref, yh_ref, yl_ref, x_ref, w_ref, g_ref, nw_ref, sh_ref, sc_ref, xo_ref, *h_refs):
    acc = jnp.dot(ys_ref[...], w_ref[0:SSD_INNER, :], preferred_element_type=F32)
    acc = acc + jnp.dot(yh_ref[...], w_ref[SSD_INNER:SSD_INNER + HG_WIDTH, :], preferred_element_type=F32)
    acc = acc + jnp.dot(yl_ref[...], w_ref[SSD_INNER + HG_WIDTH:, :], preferred_element_type=F32)
    xn = x_ref[...] + g_ref[...] * acc
    xo_ref[...] = xn
    _store_planes(h_refs, _norm_mod(xn, nw_ref[...], sh_ref[...], sc_ref[...]))


def _outproj(y_ssd, y_hg, y_lru, x2, w_out, nw, mod3, seq):
    n, d = x2.shape
    tm = TM_OUTPROJ
    tpb = seq // tm

    def modspec(j):
        return pl.BlockSpec((None, 1, d), lambda i: ((i // tpb) * N_ADA + j, 0, 0))

    outs = pl.pallas_call(
        _outproj_kernel,
        grid=(n // tm,),
        in_specs=[
            pl.BlockSpec((tm, SSD_INNER), lambda i: (i, 0)),
            pl.BlockSpec((tm, HG_WIDTH), lambda i: (i, 0)),
            pl.BlockSpec((tm, LRU_WIDTH), lambda i: (i, 0)),
            pl.BlockSpec((tm, d), lambda i: (i, 0)),
            pl.BlockSpec(w_out.shape, lambda i: (0, 0)),
            modspec(2),
            pl.BlockSpec((1, d), lambda i: (0, 0)),
            modspec(3), modspec(4),
        ],
        out_specs=[pl.BlockSpec((tm, d), lambda i: (i, 0))] + [_plane_spec(tm)] * N_PLANES,
        out_shape=[jax.ShapeDtypeStruct((n, d), F32)] + [jax.ShapeDtypeStruct((n, LANES), jnp.uint32)] * N_PLANES,
        compiler_params=_cparams("arbitrary"),
        name="outproj",
    )(y_ssd, y_hg, y_lru, x2, w_out, mod3, nw, mod3, mod3)
    return outs[0], tuple(outs[1:])


def _plane_spec(tm):
    return pl.BlockSpec((tm, LANES), lambda i: (i, 0))


def _router_kernel(h0, h1, h2, h3, rw_ref, rb_ref, eid_ref, rank_ref, wt_ref, cnt_ref, carry, wscr):
    tm = h0.shape[0]

    @pl.when(pl.program_id(0) == 0)
    def _():
        carry[...] = jnp.zeros_like(carry)

    hb = _load_planes((h0, h1, h2, h3)).astype(BF16)
    logit_t = sum(lax.dot_general(part, hb, (((1,), (1,)), ((), ())), preferred_element_type=F32)
                  for part in _split3(rw_ref[...]))
    score = _sigmoid(logit_t)
    sel = score + rb_ref[...]
    neg_inf = jnp.float32(-jnp.inf)
    io_g = lax.broadcasted_iota(jnp.int32, (E_PER_GROUP, tm), 0)
    blocks, gscore = [], []
    for g in range(N_EXPERT_GROUPS):
        blk = sel[g * E_PER_GROUP:(g + 1) * E_PER_GROUP, :]
        m1 = jnp.max(blk, axis=0, keepdims=True)
        i1 = jnp.min(jnp.where(blk == m1, io_g, E_PER_GROUP), axis=0, keepdims=True)
        m2 = jnp.max(jnp.where(io_g == i1, neg_inf, blk), axis=0, keepdims=True)
        blocks.append(blk)
        gscore.append(m1 + m2)
    masked = []
    for g in range(N_EXPERT_GROUPS):
        rank = jnp.zeros((1, tm), jnp.int32)
        for o in range(N_EXPERT_GROUPS):
            if o == g:
                continue
            beats = (gscore[o] > gscore[g]) | ((gscore[o] == gscore[g]) & (o < g))
            rank = rank + beats.astype(jnp.int32)
        masked.append(jnp.where(rank < TOPK_GROUPS, blocks[g], MASK_SCORE))
    val = jnp.concatenate(masked, axis=0)
    io_e = lax.broadcasted_iota(jnp.int32, (N_EXPERTS, tm), 0)
    chosen = jnp.zeros((N_EXPERTS, tm), jnp.bool_)
    picks = []
    for k in range(TOP_K):
        m = jnp.max(val, axis=0, keepdims=True)
        idx = jnp.min(jnp.where(val == m, io_e, N_EXPERTS), axis=0, keepdims=True)
        pick = io_e == idx
        picks.append(pick)
        eid_ref[k:k + 1, :] = idx
        chosen = chosen | pick
        val = jnp.where(pick, neg_inf, val)
    w = jnp.where(chosen, score, 0.0)
    w = w / jnp.sum(w, axis=0, keepdims=True) * ROUTED_SCALE

    chosen_f = chosen.astype(F32)
    earlier = (lax.broadcasted_iota(jnp.int32, (tm, tm), 0) < lax.broadcasted_iota(jnp.int32, (tm, tm), 1))
    before = jnp.dot(chosen_f.astype(BF16), earlier.astype(BF16), preferred_element_type=F32)
    grank = carry[:, 0:1] + before
    wscr[...] = jnp.zeros_like(wscr)
    for k in range(TOP_K):
        rank_ref[k:k + 1, :] = jnp.sum(jnp.where(picks[k], grank, 0.0), axis=0, keepdims=True).astype(jnp.int32)
        wscr[k:k + 1, :] = jnp.sum(jnp.where(picks[k], w, 0.0), axis=0, keepdims=True)
    wt_ref[...] = wscr[...].T
    carry[...] = carry[...] + jnp.sum(chosen_f, axis=1, keepdims=True)
    cnt_ref[...] = carry[...]


def _router(hp, rw_t, rb):
    n = hp[0].shape[0]
    tm = TM_ROUTER
    return pl.pallas_call(
        _router_kernel,
        grid=(n // tm,),
        in_specs=[_plane_spec(tm)] * N_PLANES + [
            pl.BlockSpec(rw_t.shape, lambda i: (0, 0)),
            pl.BlockSpec((N_EXPERTS, 1), lambda i: (0, 0)),
        ],
        out_specs=[
            pl.BlockSpec((TOP_K, tm), lambda i: (0, i)),
            pl.BlockSpec((TOP_K, tm), lambda i: (0, i)),
            pl.BlockSpec((tm, LANES), lambda i: (i, 0)),
            pl.BlockSpec((N_EXPERTS, LANES), lambda i: (0, 0)),
        ],
        out_shape=[
            jax.ShapeDtypeStruct((TOP_K, n), jnp.int32),
            jax.ShapeDtypeStruct((TOP_K, n), jnp.int32),
            jax.ShapeDtypeStruct((n, LANES), F32),
            jax.ShapeDtypeStruct((N_EXPERTS, LANES), F32),
        ],
        scratch_shapes=[pltpu.VMEM((N_EXPERTS, LANES), F32), pltpu.VMEM((LANES, tm), F32)],
        compiler_params=_cparams("arbitrary"),
        name="router",
    )(*hp, rw_t, rb)


def _plan_kernel(n_slots, cnt_ref, eid_ref, rank_ref, dest_ref, be_ref, nb_ref, pad_ref):
    cnt = cnt_ref[...].astype(jnp.int32)
    padded = ((cnt + (MOE_BLOCK - 1)) >> MOE_BLOCK_LOG2) << MOE_BLOCK_LOG2
    ri = lax.broadcasted_iota(jnp.int32, (N_EXPERTS, N_EXPERTS), 0)
    ci = lax.broadcasted_iota(jnp.int32, (N_EXPERTS, N_EXPERTS), 1)
    pad_end = jnp.dot((ri >= ci).astype(F32), padded.astype(F32), precision=HI,
                      preferred_element_type=F32).astype(jnp.int32)
    pad_start = pad_end - padded
    eid = eid_ref[...]
    dest = rank_ref[...]
    for e in range(N_EXPERTS):
        dest = dest + jnp.where(eid == e, pad_start[e:e + 1, 0:1], 0)
    dest_ref[...] = dest
    nbp = be_ref.shape[1]
    jpos = lax.broadcasted_iota(jnp.int32, (N_EXPERTS, nbp), 1) * MOE_BLOCK
    be = jnp.sum((pad_end[:, 0:1] <= jpos).astype(jnp.int32), axis=0, keepdims=True)
    be_ref[...] = jnp.minimum(be, N_EXPERTS - 1)
    nb_ref[...] = pad_end[N_EXPERTS - 1:N_EXPERTS, :] >> MOE_BLOCK_LOG2
    lane = lax.broadcasted_iota(jnp.int32, (N_EXPERTS, LANES), 1)
    for r in range(MOE_BLOCK // LANES):
        s = pad_start + cnt + (r * LANES + lane)
        pad_ref[r * N_EXPERTS:(r + 1) * N_EXPERTS, :] = jnp.where(s < pad_end, s, n_slots + lane)


def _plan(cnt, eid, rank, n_slots):
    n = eid.shape[1]
    nbp = -(-(n_slots // MOE_BLOCK) // LANES) * LANES
    pad_rows = MOE_BLOCK // LANES * N_EXPERTS
    return pl.pallas_call(
        functools.partial(_plan_kernel, n_slots),
        grid=(1,),
        in_specs=[
            pl.BlockSpec(cnt.shape, lambda i: (0, 0)),
            pl.BlockSpec(eid.shape, lambda i: (0, 0)),
            pl.BlockSpec(rank.shape, lambda i: (0, 0)),
        ],
        out_specs=[
            pl.BlockSpec((TOP_K, n), lambda i: (0, 0)),
            pl.BlockSpec((1, nbp), lambda i: (0, 0)),
            pl.BlockSpec((1, LANES), lambda i: (0, 0)),
            pl.BlockSpec((pad_rows, LANES), lambda i: (0, 0)),
        ],
        out_shape=[
            jax.ShapeDtypeStruct((TOP_K, n), jnp.int32),
            jax.ShapeDtypeStruct((1, nbp), jnp.int32),
            jax.ShapeDtypeStruct((1, LANES), jnp.int32),
            jax.ShapeDtypeStruct((pad_rows, LANES), jnp.int32),
        ],
        compiler_params=_cparams("arbitrary"),
        name="moe_plan",
    )(cnt, eid, rank)


def _sc_mesh():
    return plsc.VectorSubcoreMesh(core_axis_name="c", subcore_axis_name="s")


def _sc_worker():
    return lax.axis_index("c") * SC_SUBCORES + lax.axis_index("s")


def _sc_dispatch(hp, dest_rows, pad_rows, n_rows):
    n = hp[0].shape[0]
    tiles_per_worker = n // SC_WIN // SC_WORKERS
    pad_per_worker = pad_rows.shape[0] // SC_WORKERS
    zeros = jnp.zeros((SC_WIN, LANES), jnp.uint32)

    def body(*refs):
        h = refs[:N_PLANES]
        dest_hbm, pad_hbm, z_hbm = refs[N_PLANES:N_PLANES + 3]
        xs = refs[N_PLANES + 3:2 * N_PLANES + 3]
        bufs = refs[2 * N_PLANES + 3:3 * N_PLANES + 3]
        ibuf, pbuf, sem = refs[3 * N_PLANES + 3:]
        wid = _sc_worker()

        pltpu.sync_copy(z_hbm, bufs[0])
        pltpu.sync_copy(pad_hbm.at[pl.ds(wid * pad_per_worker, pad_per_worker)], pbuf)
        copies = [pltpu.async_copy(bufs[0], xs[c].at[pbuf.at[r]], sem)
                  for r in range(pad_per_worker) for c in range(N_PLANES)]
        for cp in copies:
            cp.wait()

        @pl.loop(0, tiles_per_worker)
        def _(i):
            tile = wid * tiles_per_worker + i
            pltpu.sync_copy(dest_hbm.at[pl.ds(tile * TOP_K, TOP_K)], ibuf)
            for c in range(N_PLANES):
                pltpu.sync_copy(h[c].at[pl.ds(tile * SC_WIN, SC_WIN)], bufs[c])
            scatters = [pltpu.async_copy(bufs[c], xs[c].at[ibuf.at[k]], sem)
                        for c in range(N_PLANES) for k in range(TOP_K)]
            for cp in scatters:
                cp.wait()

    out_type = tuple(jax.ShapeDtypeStruct((n_rows, LANES), jnp.uint32) for _ in range(N_PLANES))
    scratch = ([pltpu.VMEM((SC_WIN, LANES), jnp.uint32)] * N_PLANES
               + [pltpu.VMEM((TOP_K, LANES), jnp.int32), pltpu.VMEM((pad_per_worker, LANES), jnp.int32),
                  pltpu.SemaphoreType.DMA])
    return pl.kernel(body, out_type=out_type, mesh=_sc_mesh(), scratch_types=scratch,
                     name="moe_sc_dispatch")(*hp, dest_rows, pad_rows, zeros)


def _sc_gather(ysp, dest_rows, n):
    tiles_per_worker = n // SC_WIN // SC_WORKERS

    def body(*refs):
        ys = refs[:N_PLANES]
        dest_hbm, g_hbm = refs[N_PLANES:N_PLANES + 2]
        bufs = refs[N_PLANES + 2:N_PLANES + 2 + SC_GATHER_BUFS]
        ibuf, sem = refs[N_PLANES + 2 + SC_GATHER_BUFS:]
        wid = _sc_worker()

        @pl.loop(0, tiles_per_worker)
        def _(i):
            tile = wid * tiles_per_worker + i
            pltpu.sync_copy(dest_hbm.at[pl.ds(tile * TOP_K, TOP_K)], ibuf)
            for c in range(N_PLANES):
                for k0 in range(0, TOP_K, SC_GATHER_BUFS):
                    gathers = [pltpu.async_copy(ys[c].at[ibuf.at[k0 + j]], bufs[j], sem)
                               for j in range(SC_GATHER_BUFS)]
                    for cp in gathers:
                        cp.wait()
                    stores = [pltpu.async_copy(
                        bufs[j], g_hbm.at[pl.ds(((k0 + j) * N_PLANES + c) * n + tile * SC_WIN, SC_WIN)], sem)
                        for j in range(SC_GATHER_BUFS)]
                    for cp in stores:
                        cp.wait()

    scratch = ([pltpu.VMEM((SC_WIN, LANES), jnp.uint32)] * SC_GATHER_BUFS
               + [pltpu.VMEM((TOP_K, LANES), jnp.int32), pltpu.SemaphoreType.DMA])
    return pl.kernel(body, out_type=jax.ShapeDtypeStruct((TOP_K * N_PLANES * n, LANES), jnp.uint32),
                     mesh=_sc_mesh(), scratch_types=scratch, name="moe_sc_gather")(*ysp, dest_rows)


def _expert_kernel(be_ref, nb_ref, *refs):
    xs_hbm = refs[:N_PLANES]
    wg_ref, wu_ref, wd_ref = refs[N_PLANES:N_PLANES + 3]
    ys_refs = refs[N_PLANES + 3:2 * N_PLANES + 3]
    wg_b, wu_b, wd_b, ring, sems = refs[2 * N_PLANES + 3:]
    j = pl.program_id(0)
    nb = nb_ref[0]
    used = j < nb
    new_expert = (j == 0) | (be_ref[j] != be_ref[jnp.maximum(j - 1, 0)])

    def fetch(step, slot):
        rows = pl.ds(pl.multiple_of(step * MOE_BLOCK, MOE_BLOCK), MOE_BLOCK)
        return [pltpu.make_async_copy(xs_hbm[c].at[rows], ring.at[slot, c], sems.at[slot])
                for c in range(N_PLANES)]

    for s in range(XS_RING - 1):
        @pl.when((j == 0) & (s < nb))
        def _():
            for cp in fetch(s, s):
                cp.start()

    ahead = j + (XS_RING - 1)

    @pl.when(used & (ahead < nb))
    def _():
        for cp in fetch(ahead, lax.rem(ahead, XS_RING)):
            cp.start()

    @pl.when(used & new_expert)
    def _():
        wg_b[...] = wg_ref[...].astype(BF16)
        wu_b[...] = wu_ref[...].astype(BF16)
        wd_b[...] = wd_ref[...].astype(BF16)

    @pl.when(used)
    def _():
        slot = lax.rem(j, XS_RING)
        for cp in fetch(j, slot):
            cp.wait()
        x = _unpack_bf16_pairs(jnp.concatenate([ring[slot, c] for c in range(N_PLANES)], axis=1)).astype(BF16)
        a = jnp.dot(x, wg_b[...], preferred_element_type=F32)
        u = jnp.dot(x, wu_b[...], preferred_element_type=F32)
        y = jnp.dot((_silu(a) * u).astype(BF16), wd_b[...], preferred_element_type=F32)
        _store_planes(ys_refs, y)

    @pl.when(jnp.logical_not(used))
    def _():
        for ref in ys_refs:
            ref[...] = jnp.zeros_like(ref)


def _experts(be, nb, xsp, wg, wu, wd, layer, n_slots):
    d = wg.shape[2]

    def blk(j, be_ref, nb_ref):
        return jnp.minimum(j, jnp.maximum(nb_ref[0] - 1, 0))

    def wspec(shape):
        return pl.BlockSpec((None, None) + shape,
                            lambda j, be_ref, nb_ref: (layer, be_ref[blk(j, be_ref, nb_ref)], 0, 0))

    grid_spec = pltpu.PrefetchScalarGridSpec(
        num_scalar_prefetch=2,
        grid=(n_slots // MOE_BLOCK,),
        in_specs=[pl.BlockSpec(memory_space=pl.ANY)] * N_PLANES
        + [wspec((d, D_EXPERT)), wspec((d, D_EXPERT)), wspec((D_EXPERT, d))],
        out_specs=[pl.BlockSpec((MOE_BLOCK, LANES), lambda j, be_ref, nb_ref: (j, 0))] * N_PLANES,
        scratch_shapes=[pltpu.VMEM((d, D_EXPERT), BF16), pltpu.VMEM((d, D_EXPERT), BF16),
                        pltpu.VMEM((D_EXPERT, d), BF16),
                        pltpu.VMEM((XS_RING, N_PLANES, MOE_BLOCK, LANES), jnp.uint32),
                        pltpu.SemaphoreType.DMA((XS_RING,))],
    )
    return pl.pallas_call(
        _expert_kernel,
        grid_spec=grid_spec,
        out_shape=[jax.ShapeDtypeStruct((n_slots, LANES), jnp.uint32)] * N_PLANES,
        compiler_params=_cparams("arbitrary"),
        name="moe_experts",
    )(be, nb, *xsp, wg, wu, wd)


def _combine_kernel(final, g_ref, wt_ref, h0, h1, h2, h3, sg_ref, su_ref, sd_ref, x_ref, gate_ref, fw_ref, o_ref):
    hb = _load_planes((h0, h1, h2, h3)).astype(BF16)
    a = jnp.dot(hb, sg_ref[...], preferred_element_type=F32)
    u = jnp.dot(hb, su_ref[...], preferred_element_type=F32)
    acc = jnp.dot((_silu(a) * u).astype(BF16), sd_ref[...], preferred_element_type=F32)
    wt = wt_ref[...]
    for k in range(TOP_K):
        rows = _unpack_bf16_pairs(jnp.concatenate([g_ref[k * N_PLANES + c] for c in range(N_PLANES)], axis=1))
        acc = acc + wt[:, k:k + 1] * rows
    xn = x_ref[...] + gate_ref[...] * acc
    if final:
        ms = jnp.mean(xn * xn, axis=-1, keepdims=True)
        xn = xn * lax.rsqrt(ms + EPS) * fw_ref[...]
    o_ref[...] = xn


def _combine(g, wt, hp, sg, su, sd, x2, mod3, fw, seq, final):
    n, d = x2.shape
    tm = TM_COMBINE
    tpb = seq // tm
    return pl.pallas_call(
        functools.partial(_combine_kernel, final),
        grid=(n // tm,),
        in_specs=[
            pl.BlockSpec((TOP_K * N_PLANES, tm, LANES), lambda i: (0, i, 0)),
            pl.BlockSpec((tm, LANES), lambda i: (i, 0)),
        ] + [_plane_spec(tm)] * N_PLANES + [
            pl.BlockSpec(sg.shape, lambda i: (0, 0)),
            pl.BlockSpec(su.shape, lambda i: (0, 0)),
            pl.BlockSpec(sd.shape, lambda i: (0, 0)),
            pl.BlockSpec((tm, d), lambda i: (i, 0)),
            pl.BlockSpec((None, 1, d), lambda i: ((i // tpb) * N_ADA + 5, 0, 0)),
            pl.BlockSpec((1, d), lambda i: (0, 0)),
        ],
        out_specs=pl.BlockSpec((tm, d), lambda i: (i, 0)),
        out_shape=jax.ShapeDtypeStruct((n, d), F32),
        compiler_params=_cparams("arbitrary"),
        name="moe_combine",
    )(g.reshape(TOP_K * N_PLANES, n, LANES), wt, *hp, sg, su, sd, x2, mod3, fw)


def _blockdiag_pairs(w):
    z = jnp.zeros((LRU_BLOCK_W, LRU_BLOCK_W), w.dtype)
    tiles = []
    for j in range(LRU_BLOCKS // 2):
        top = jnp.concatenate([w[2 * j], z], axis=1)
        bot = jnp.concatenate([z, w[2 * j + 1]], axis=1)
        tiles.append(jnp.concatenate([top, bot], axis=0))
    return jnp.stack(tiles).astype(BF16)


def _hg_level_masks():
    ch = HG_CHUNK
    msk = np.zeros((HG_LEVELS, ch, ch), np.float32)
    for lvl in range(HG_LEVELS):
        half = 1 << lvl
        for t in range(ch):
            base = (t // (2 * half)) * (2 * half)
            if (t // half) % 2 == 1:
                msk[lvl, t, base:base + half] = 1.0
    later = ((np.arange(ch)[None, :, None] >> np.arange(HG_LEVELS)[:, None, None]) & 1).astype(np.float32)
    later = np.broadcast_to(later, (HG_LEVELS, ch, HG_EXPAND))
    sgn = (2.0 * later - 1.0) * np.float32(LOG2E)
    return jnp.asarray(msk), jnp.asarray(later), jnp.asarray(sgn, dtype=F32)


def _ssd_expand():
    e = np.zeros((LANES, SSD_INNER), np.float32)
    for h in range(SSD_HEADS):
        e[h, h * SSD_HEADDIM:(h + 1) * SSD_HEADDIM] = 1.0
    return jnp.asarray(np.concatenate([e] * 3, axis=0), dtype=BF16)


def _pad_lanes(v, width):
    return jnp.pad(v, (0, width - v.shape[0])).reshape(1, width)


def _layer_params(l, w_in, ssd_conv_w, ssd_conv_b, ssd_dt_bias, ssd_a_log, ssd_d, ssd_norm_w, hg_lower_bounds,
                  hg_norm_w, lru_conv_w, lru_conv_b, lru_wa, lru_ba, lru_wx, lru_bx, lru_lambda, lru_norm_w):
    wi = w_in[l]
    dt0 = SSD_INNER + SSD_INNER + 2 * SSD_GROUPS * SSD_STATE
    w_cat = jnp.concatenate([wi[:, :dt0], wi[:, dt0 + SSD_HEADS:], wi[:, dt0:dt0 + SSD_HEADS]], axis=1)
    w_cat = jnp.pad(w_cat, ((0, 0), (0, U_WIDTH - w_cat.shape[1]))).astype(BF16)
    msk, later, sgn = _hg_level_masks()
    return dict(
        w_cat=w_cat,
        cwx=ssd_conv_w[l][:, :SSD_INNER], cbx=ssd_conv_b[l][:SSD_INNER].reshape(1, -1),
        cwb=ssd_conv_w[l][:, SSD_INNER:], cbb=ssd_conv_b[l][SSD_INNER:].reshape(1, -1),
        dtb=_pad_lanes(ssd_dt_bias[l], LANES), alog=_pad_lanes(ssd_a_log[l], LANES),
        dful=jnp.repeat(ssd_d[l], SSD_HEADDIM).reshape(1, -1), ssd_nw=ssd_norm_w[l].reshape(1, -1),
        expand=_ssd_expand(),
        hg_lb=hg_lower_bounds, hg_nw=hg_norm_w[l].reshape(1, -1), hg_msk=msk, hg_later=later, hg_sgn=sgn,
        lru_cw=lru_conv_w[l], lru_cb=lru_conv_b[l].reshape(1, -1),
        lru_wa=_blockdiag_pairs(lru_wa[l]), lru_ba=lru_ba[l].reshape(1, -1),
        lru_wx=_blockdiag_pairs(lru_wx[l]), lru_bx=lru_bx[l].reshape(1, -1),
        lru_lam=lru_lambda[l].reshape(1, -1), lru_nw=lru_norm_w[l].reshape(1, -1),
    )


def kernel(x, c, ada_w, ada_b, norm_mix_w, norm_ffn_w, w_in, ssd_conv_w, ssd_conv_b, ssd_dt_bias, ssd_a_log, ssd_d, ssd_norm_w, hg_lower_bounds, hg_norm_w, lru_conv_w, lru_conv_b, lru_wa, lru_ba, lru_wx, lru_bx, lru_lambda, lru_norm_w, w_out, router_w, router_bias, exp_gate, exp_up, exp_down, sh_gate, sh_up, sh_down, final_norm_w):
    bsz, seq, d = x.shape
    depth = ada_w.shape[0]
    assert d == D_MODEL and seq % TM_OUTPROJ == 0 and seq % SSD_CHUNK == 0
    n = bsz * seq
    n_slots = n * TOP_K + N_EXPERTS * MOE_BLOCK
    x2 = x.reshape(n, d)
    mod = _adaln(c, ada_w, ada_b)
    fw = final_norm_w.reshape(1, d)
    for l in range(depth):
        p = _layer_params(l, w_in, ssd_conv_w, ssd_conv_b, ssd_dt_bias, ssd_a_log, ssd_d, ssd_norm_w,
                          hg_lower_bounds, hg_norm_w, lru_conv_w, lru_conv_b, lru_wa, lru_ba, lru_wx, lru_bx,
                          lru_lambda, lru_norm_w)
        mod3 = mod[l].reshape(bsz * N_ADA, 1, d)
        u = _inproj(x2, norm_mix_w[l].reshape(1, d), mod3, p["w_cat"], seq)
        y_ssd = _ssd(u, p, bsz, seq)
        y_hg, y_lru = _hgrn2_lru(u, p, l, bsz, seq)
        x2, hp = _outproj(y_ssd, y_hg, y_lru, x2, w_out[l].astype(BF16), norm_ffn_w[l].reshape(1, d), mod3, seq)
        eid, rank, wt, cnt = _router(hp, router_w[l].T, router_bias[l].reshape(N_EXPERTS, 1))
        dest, be, nb, pad_rows = _plan(cnt, eid, rank, n_slots)
        dest_rows = dest.reshape(TOP_K, n // LANES, LANES).transpose(1, 0, 2).reshape(n // LANES * TOP_K, LANES)
        xsp = _sc_dispatch(hp, dest_rows, pad_rows, n_slots + LANES)
        ysp = _experts(be.reshape(-1), nb[0, :1], xsp, exp_gate, exp_up, exp_down, l, n_slots)
        g = _sc_gather(ysp, dest_rows, n)
        x2 = _combine(g, wt, hp, sh_gate[l].astype(BF16), sh_up[l].astype(BF16), sh_down[l].astype(BF16),
                      x2, mod3, fw, seq, final=(l == depth - 1))
    return x2.reshape(bsz, seq, d)
```

```python
import functools

import jax
import jax.numpy as jnp
import numpy as np
from jax import lax
from jax.experimental import pallas as pl
from jax.experimental.pallas import tpu as pltpu
from jax.experimental.pallas import tpu_sc as plsc

F32 = jnp.float32
BF16 = jnp.bfloat16
HI = lax.Precision.HIGHEST
F32_TINY = float(np.finfo(np.float32).tiny)
LOG2E = float(np.log2(np.e))

LANES = 128
SUBLANES = 8
VMEM_LIMIT_BYTES = 56 * 1024 * 1024

D_MODEL = 1024
EPS = 1e-6
N_ADA = 6
CONV_WIDTH = 4
SSD_INNER = 1024
SSD_HEADDIM = 64
SSD_HEADS = 16
SSD_GROUPS = 2
SSD_STATE = 128
SSD_CHUNK = 128
SSD_GROUP_W = SSD_INNER // SSD_GROUPS
HG_WIDTH = 512
HG_EXPAND = 128
HG_HEADS = 4
HG_CHUNK = 128
HG_LEVELS = 7
LRU_WIDTH = 512
LRU_BLOCKS = 8
LRU_BLOCK_W = 64
LRU_C = 8.0
N_EXPERTS = 64
TOP_K = 8
N_EXPERT_GROUPS = 8
E_PER_GROUP = 8
TOPK_GROUPS = 4
D_EXPERT = 256
ROUTED_SCALE = 2.5
MASK_SCORE = -1.0e4

COL_Z = 0
COL_XS = 1024
COL_BC = 2048
COL_HQ = 2560
COL_HF = 3072
COL_HV = 3584
COL_HG = 4096
COL_LG = 4608
COL_LX = 5120
COL_DT = 5632
U_WIDTH = 5760

TM_INPROJ = 512
TM_OUTPROJ = 1024
TM_ROUTER = 512
TM_COMBINE = 512
SC_CORES = 2
SC_SUBCORES = 16
SC_WORKERS = SC_CORES * SC_SUBCORES
SC_WIN = LANES
SC_GATHER_BUFS = 4
MOE_BLOCK_LOG2 = 10
MOE_BLOCK = 1 << MOE_BLOCK_LOG2
XS_RING = 3
XS_PIECE_LOG2 = 7
XS_PIECE = 1 << XS_PIECE_LOG2
R_HG = 256


def _cparams(*sem):
    return pltpu.CompilerParams(dimension_semantics=sem, vmem_limit_bytes=VMEM_LIMIT_BYTES)


def _sigmoid(x):
    return 0.5 * jnp.tanh(0.5 * x) + 0.5


def _silu(x):
    h = 0.5 * x
    return h + h * jnp.tanh(h)


def _split3(x):
    x1 = x.astype(BF16)
    r1 = x - x1.astype(F32)
    x2 = r1.astype(BF16)
    x3 = (r1 - x2.astype(F32)).astype(BF16)
    return x1, x2, x3


def _softplus(x):
    return jnp.maximum(x, 0.0) + jnp.log1p(jnp.exp(-jnp.abs(x)))


def _norm_mod(x, nw, shift, scale):
    ms = jnp.mean(x * x, axis=-1, keepdims=True)
    y = x * lax.rsqrt(ms + EPS) * nw
    return y * (1.0 + scale) + shift


_HI16 = np.uint32(0xFFFF0000)


def _pack_bf16_pairs(x):
    half = x.shape[1] // 2
    bits = lax.bitcast_convert_type(x.astype(BF16).astype(F32), jnp.uint32)
    return (bits[:, :half] & _HI16) | (bits[:, half:] >> 16)


def _unpack_bf16_pairs(w):
    hi = lax.bitcast_convert_type(w & _HI16, F32)
    lo = lax.bitcast_convert_type(w << 16, F32)
    return jnp.concatenate([hi, lo], axis=1)


N_PLANES = D_MODEL // 2 // LANES


def _store_planes(refs, x):
    packed = _pack_bf16_pairs(x)
    for c, ref in enumerate(refs):
        ref[...] = packed[:, c * LANES:(c + 1) * LANES]


def _load_planes(refs):
    return _unpack_bf16_pairs(jnp.concatenate([ref[...] for ref in refs], axis=1))


def _ada_kernel(c_ref, w_ref, b_ref, o_ref):
    c = c_ref[...]
    o_ref[...] = jnp.dot(_silu(c), w_ref[...], precision=HI, preferred_element_type=F32) + b_ref[...]


def _adaln(c, ada_w, ada_b):
    depth, d, n6 = ada_w.shape
    bsz = c.shape[0]
    tn = 1536
    return pl.pallas_call(
        _ada_kernel,
        grid=(depth, n6 // tn),
        in_specs=[
            pl.BlockSpec((bsz, d), lambda l, j: (0, 0)),
            pl.BlockSpec((None, d, tn), lambda l, j: (l, 0, j)),
            pl.BlockSpec((None, 1, tn), lambda l, j: (l, 0, j)),
        ],
        out_specs=pl.BlockSpec((None, bsz, tn), lambda l, j: (l, 0, j)),
        out_shape=jax.ShapeDtypeStruct((depth, bsz, n6), F32),
        compiler_params=_cparams("arbitrary", "arbitrary"),
        name="adaln_mod",
    )(c, ada_w, ada_b.reshape(depth, 1, n6))


def _inproj_kernel(x_ref, nw_ref, sh_ref, sc_ref, w_ref, o_ref):
    h = _norm_mod(x_ref[...], nw_ref[...], sh_ref[...], sc_ref[...])
    o_ref[...] = jnp.dot(h.astype(BF16), w_ref[...], preferred_element_type=F32)


def _inproj(x2, nw, mod3, w_cat, seq):
    n, d = x2.shape
    tm = TM_INPROJ
    tpb = seq // tm
    return pl.pallas_call(
        _inproj_kernel,
        grid=(n // tm,),
        in_specs=[
            pl.BlockSpec((tm, d), lambda i: (i, 0)),
            pl.BlockSpec((1, d), lambda i: (0, 0)),
            pl.BlockSpec((None, 1, d), lambda i: ((i // tpb) * N_ADA + 0, 0, 0)),
            pl.BlockSpec((None, 1, d), lambda i: ((i // tpb) * N_ADA + 1, 0, 0)),
            pl.BlockSpec((d, U_WIDTH), lambda i: (0, 0), pipeline_mode=pl.Buffered(1)),
        ],
        out_specs=pl.BlockSpec((tm, U_WIDTH), lambda i: (i, 0)),
        out_shape=jax.ShapeDtypeStruct((n, U_WIDTH), F32),
        compiler_params=_cparams("arbitrary"),
        name="inproj",
    )(x2, nw, mod3, mod3, w_cat)


def _causal_conv(cur_ref, ext, tail, cw_ref, cb_ref):
    rows, width = cur_ref.shape
    cur = cur_ref[...]
    ext[0:SUBLANES, :] = tail[...]
    ext[SUBLANES:SUBLANES + rows, :] = cur
    tail[...] = cur[rows - SUBLANES:rows, :]
    groups = ext[...].reshape(rows // SUBLANES + 1, SUBLANES, width)
    sub = lax.broadcasted_iota(jnp.int32, (1, SUBLANES, width), 1)
    acc = cb_ref[...] + cur * cw_ref[CONV_WIDTH - 1:CONV_WIDTH, :]
    for d in range(1, CONV_WIDTH):
        rot = pltpu.roll(groups, d, 1)
        back = jnp.where(sub < d, rot[:-1], rot[1:]).reshape(rows, width)
        acc = acc + back * cw_ref[CONV_WIDTH - 1 - d:CONV_WIDTH - d, :]
    return acc


def _ssd_kernel(z_ref, xs_ref, bc_ref, dt_ref, cwx_ref, cbx_ref, cwb_ref, cbb_ref, dtb_ref, alog_ref,
                dful_ref, nw_ref, e_ref, o_ref, extx, extb, tailx, tailb, hstate):
    c = pl.program_id(1)
    q = SSD_CHUNK

    @pl.when(c == 0)
    def _():
        tailx[...] = jnp.zeros_like(tailx)
        tailb[...] = jnp.zeros_like(tailb)
        hstate[...] = jnp.zeros_like(hstate)

    xs = _silu(_causal_conv(xs_ref, extx, tailx, cwx_ref, cbx_ref))
    bc = _silu(_causal_conv(bc_ref, extb, tailb, cwb_ref, cbb_ref))

    dt = _softplus(dt_ref[...] + dtb_ref[...])
    a = dt * (-jnp.exp(alog_ref[...]))
    ri = lax.broadcasted_iota(jnp.int32, (q, q), 0)
    ci = lax.broadcasted_iota(jnp.int32, (q, q), 1)
    tril = ri >= ci
    tril_f = tril.astype(F32)
    tril_b = tril.astype(BF16)
    acum = jnp.dot(jnp.concatenate([tril_b] * 3, axis=1), jnp.concatenate(_split3(a), axis=0),
                   preferred_element_type=F32)
    acum_t = acum.T
    expand3 = e_ref[...]
    dt_full = jnp.dot(jnp.concatenate(_split3(dt), axis=1), expand3, preferred_element_type=F32)
    acum_full = jnp.dot(jnp.concatenate(_split3(acum), axis=1), expand3, preferred_element_type=F32)
    alast_full = acum_full[q - 1:q, :]

    xdt = xs * dt_full
    xdt_b = xdt.astype(BF16)
    exp_a = jnp.exp(acum_full)
    xd_b = (xdt * jnp.exp(alast_full - acum_full)).astype(BF16)
    state_decay = jnp.exp(alast_full)
    left = lax.broadcasted_iota(jnp.int32, (q, LANES), 1) < SSD_HEADDIM
    zero_b = jnp.zeros((q, LANES), BF16)

    ys = []
    for g in range(SSD_GROUPS):
        b_g = bc[:, g * SSD_STATE:(g + 1) * SSD_STATE]
        c_g = bc[:, (SSD_GROUPS + g) * SSD_STATE:(SSD_GROUPS + g + 1) * SSD_STATE]
        c_b = c_g.astype(BF16)
        cb = lax.dot_general(c_b, b_g.astype(BF16), (((1,), (1,)), ((), ())), preferred_element_type=F32)
        cb = cb * tril_f
        cs = slice(g * SSD_GROUP_W, (g + 1) * SSD_GROUP_W)
        h_g = hstate[:, cs]
        y_off = jnp.dot(c_b, h_g.astype(BF16), preferred_element_type=F32) * exp_a[:, cs]
        pieces = []
        for pr in range(SSD_HEADS // SSD_GROUPS // 2):
            h0 = g * (SSD_HEADS // SSD_GROUPS) + 2 * pr
            ms = []
            for h in (h0, h0 + 1):
                col = acum[:, h:h + 1]
                row = acum_t[h:h + 1, :]
                ms.append((cb * jnp.exp(jnp.minimum(col - row, 0.0))).astype(BF16))
            lhs = jnp.concatenate(ms, axis=1)
            xp = xdt_b[:, h0 * SSD_HEADDIM:(h0 + 2) * SSD_HEADDIM]
            rhs = jnp.concatenate([jnp.where(left, xp, zero_b), jnp.where(left, zero_b, xp)], axis=0)
            pieces.append(jnp.dot(lhs, rhs, preferred_element_type=F32))
        ys.append(jnp.concatenate(pieces, axis=1) + y_off)
        b_t = b_g.T.astype(BF16)
        hstate[:, cs] = h_g * state_decay[:, cs] + jnp.dot(b_t, xd_b[:, cs], preferred_element_type=F32)

    y = jnp.concatenate(ys, axis=1) + xs * dful_ref[...]
    y = y * _silu(z_ref[...])
    outs = []
    for g in range(SSD_GROUPS):
        cs = slice(g * SSD_GROUP_W, (g + 1) * SSD_GROUP_W)
        yg = y[:, cs]
        ms = jnp.mean(yg * yg, axis=-1, keepdims=True)
        outs.append(yg * lax.rsqrt(ms + EPS) * nw_ref[:, cs])
    o_ref[...] = jnp.concatenate(outs, axis=1).astype(o_ref.dtype)


def _ssd(u, p, bsz, seq):
    q = SSD_CHUNK
    nc = seq // q
    n = bsz * seq

    def rows(b, c):
        return b * nc + c

    def const(shape):
        return pl.BlockSpec(shape, lambda b, c: (0,) * len(shape))

    return pl.pallas_call(
        _ssd_kernel,
        grid=(bsz, nc),
        in_specs=[
            pl.BlockSpec((q, SSD_INNER), lambda b, c: (rows(b, c), COL_Z // SSD_INNER)),
            pl.BlockSpec((q, SSD_INNER), lambda b, c: (rows(b, c), COL_XS // SSD_INNER)),
            pl.BlockSpec((q, 512), lambda b, c: (rows(b, c), COL_BC // 512)),
            pl.BlockSpec((q, LANES), lambda b, c: (rows(b, c), COL_DT // LANES)),
            const((CONV_WIDTH, SSD_INNER)), const((1, SSD_INNER)),
            const((CONV_WIDTH, 512)), const((1, 512)),
            const((1, LANES)), const((1, LANES)),
            const((1, SSD_INNER)), const((1, SSD_INNER)),
            const((3 * LANES, SSD_INNER)),
        ],
        out_specs=pl.BlockSpec((q, SSD_INNER), lambda b, c: (rows(b, c), 0)),
        out_shape=jax.ShapeDtypeStruct((n, SSD_INNER), BF16),
        scratch_shapes=[
            pltpu.VMEM((q + SUBLANES, SSD_INNER), F32),
            pltpu.VMEM((q + SUBLANES, 512), F32),
            pltpu.VMEM((SUBLANES, SSD_INNER), F32),
            pltpu.VMEM((SUBLANES, 512), F32),
            pltpu.VMEM((SSD_STATE, SSD_INNER), F32),
        ],
        compiler_params=_cparams("arbitrary", "arbitrary"),
        name="ssd_mixer",
    )(u, u, u, u, p["cwx"], p["cbx"], p["cwb"], p["cbb"], p["dtb"], p["alog"], p["dful"], p["ssd_nw"], p["expand"])


def _boundary_rows(b, lvl):
    half = 1 << lvl
    parts = []
    for v in range(b.shape[0] // SUBLANES):
        r0 = v * SUBLANES
        if 2 * half >= SUBLANES:
            src = (r0 // (2 * half)) * (2 * half) + half - 1
            parts.append(jnp.broadcast_to(b[src:src + 1, :], (SUBLANES, b.shape[1])))
        else:
            sub = lax.broadcasted_iota(jnp.int32, (SUBLANES, b.shape[1]), 0)
            piece = None
            for g in range(SUBLANES // (2 * half)):
                src = r0 + g * 2 * half + half - 1
                cand = jnp.broadcast_to(b[src:src + 1, :], (SUBLANES, b.shape[1]))
                piece = cand if piece is None else jnp.where(sub >= g * 2 * half, cand, piece)
            parts.append(piece)
    return jnp.concatenate(parts, axis=0)


def _hgrn2_lru_kernel(layer, q_ref, f_ref, v_ref, g_ref, lb_ref, nw_ref, msk_ref, half_ref, sgn_ref,
                      lg_ref, lx_ref, cw_ref, cb_ref, wa_ref, ba_ref, wx_ref, bx_ref, lam_ref, lnw_ref,
                      o_ref, ol_ref, state_t, ext, tail, hcarry):
    t = pl.program_id(1)
    ch = HG_CHUNK

    @pl.when(t == 0)
    def _():
        state_t[...] = jnp.zeros_like(state_t)
        tail[...] = jnp.zeros_like(tail)
        hcarry[...] = jnp.zeros_like(hcarry)

    _lru_tile(lg_ref, lx_ref, cw_ref, cb_ref, wa_ref, ba_ref, wx_ref, bx_ref, lam_ref, lnw_ref, ol_ref,
              ext, tail, hcarry)

    lrows = [lb_ref[j:j + 1, :] for j in range(lb_ref.shape[0])]
    mx = functools.reduce(jnp.maximum, lrows)
    es = [jnp.exp(r - mx) for r in lrows]
    den = functools.reduce(lambda a_, b_: a_ + b_, es)
    lb = jnp.zeros_like(mx)
    for j in range(1, layer + 1):
        lb = lb + es[j] / den
    one_minus_lb = 1.0 - lb
    nw = nw_ref[...]

    ri = lax.broadcasted_iota(jnp.int32, (ch, ch), 0)
    ci = lax.broadcasted_iota(jnp.int32, (ch, ch), 1)
    tril3 = jnp.concatenate([(ri >= ci).astype(BF16)] * 3, axis=1)
    rowi = lax.broadcasted_iota(jnp.int32, (ch, HG_EXPAND), 0)
    tgt = [((rowi >> lvl) & 1) == 1 for lvl in range(HG_LEVELS)]

    def head_chunk(h, rs):
        cs = slice(h * HG_EXPAND, (h + 1) * HG_EXPAND)
        qq = _silu(q_ref[rs, cs])
        kk = one_minus_lb[:, cs] * _sigmoid(-f_ref[rs, cs])
        logf = jnp.log1p(-kk)
        vv = v_ref[rs, cs]
        vb = vv.astype(BF16)
        b = jnp.dot(tril3, jnp.concatenate(_split3(logf), axis=0),
                    preferred_element_type=F32)
        st = state_t[h]
        o = lax.dot_general((qq * jnp.exp(b)).astype(BF16), st.astype(BF16), (((1,), (1,)), ((), ())),
                            preferred_element_type=F32)
        attn = jnp.zeros((ch, ch), F32)
        for lvl in range(HG_LEVELS):
            if lvl == 0:
                qe = jnp.where(tgt[0], qq * (1.0 - kk), 0.0)
                ke = jnp.where(tgt[0], 0.0, kk)
            else:
                m = _boundary_rows(b, lvl)
                later = half_ref[lvl]
                e = jnp.exp2((b - m) * sgn_ref[lvl])
                prod = jnp.where(tgt[lvl], qq, kk) * e
                qe = prod * later
                ke = prod - qe
            prod = lax.dot_general(qe.astype(BF16), ke.astype(BF16), (((1,), (1,)), ((), ())),
                                   preferred_element_type=F32)
            attn = attn + msk_ref[lvl] * prod
        diag = jnp.sum(qq * kk, axis=-1, keepdims=True)
        o = o + jnp.dot(attn.astype(BF16), vb, preferred_element_type=F32) + diag * vv
        b_last = b[ch - 1:ch, :]
        kd = (kk * jnp.exp(b_last - b)).astype(BF16)
        state_t[h] = st * jnp.exp(b_last) + jnp.dot(vv.T.astype(BF16), kd, preferred_element_type=F32)
        ms = jnp.mean(o * o, axis=-1, keepdims=True)
        y = o * lax.rsqrt(ms + EPS) * nw[:, cs]
        o_ref[rs, cs] = (y * _silu(g_ref[rs, cs])).astype(o_ref.dtype)

    for j in range(q_ref.shape[0] // ch):
        for h in range(HG_HEADS):
            head_chunk(h, slice(j * ch, (j + 1) * ch))


def _lru_tile(g_ref, x_ref, cw_ref, cb_ref, wa_ref, ba_ref, wx_ref, bx_ref, lam_ref, nw_ref, o_ref,
              ext, tail, hcarry):
    rows = x_ref.shape[0]
    xb = _causal_conv(x_ref, ext, tail, cw_ref, cb_ref)
    xbb = xb.astype(BF16)
    npair = LRU_WIDTH // LANES
    ra = jnp.concatenate([jnp.dot(xbb[:, j * LANES:(j + 1) * LANES], wa_ref[j], preferred_element_type=F32)
                          for j in range(npair)], axis=1)
    rx = jnp.concatenate([jnp.dot(xbb[:, j * LANES:(j + 1) * LANES], wx_ref[j], preferred_element_type=F32)
                          for j in range(npair)], axis=1)
    r = _sigmoid(ra + ba_ref[...])
    i = _sigmoid(rx + bx_ref[...])
    log_a = -LRU_C * r * _softplus(-lam_ref[...])
    a = jnp.exp(log_a)
    th = jnp.tanh(log_a)
    s = -2.0 * th
    root = s * lax.rsqrt(jnp.maximum(s * (1.0 - th), F32_TINY))
    u = root * (i * xb)

    ngroups = rows // SUBLANES
    sub = lax.broadcasted_iota(jnp.int32, (1, SUBLANES, LRU_WIDTH), 1)
    acc_a = a.reshape(ngroups, SUBLANES, LRU_WIDTH)
    acc_u = u.reshape(ngroups, SUBLANES, LRU_WIDTH)
    d = 1
    while d < SUBLANES:
        keep = sub >= d
        a_sh = jnp.where(keep, pltpu.roll(acc_a, d, 1), 1.0)
        u_sh = jnp.where(keep, pltpu.roll(acc_u, d, 1), 0.0)
        acc_u = acc_a * u_sh + acc_u
        acc_a = acc_a * a_sh
        d *= 2
    acc_a = acc_a.reshape(rows, LRU_WIDTH)
    acc_u = acc_u.reshape(rows, LRU_WIDTH)
    carry = hcarry[0:1, :]
    groups = []
    for g in range(rows // SUBLANES):
        gs = slice(g * SUBLANES, (g + 1) * SUBLANES)
        hg = acc_a[gs, :] * carry + acc_u[gs, :]
        groups.append(hg)
        carry = hg[SUBLANES - 1:SUBLANES, :]
    h = jnp.concatenate(groups, axis=0)
    hcarry[0:1, :] = carry

    gate = g_ref[...]
    gelu = 0.5 * gate * (1.0 + jnp.tanh(np.sqrt(2.0 / np.pi).astype(np.float32) * (gate + 0.044715 * (gate * gate * gate))))
    y = h * gelu
    ms = jnp.mean(y * y, axis=-1, keepdims=True)
    o_ref[...] = (y * lax.rsqrt(ms + EPS) * nw_ref[...]).astype(o_ref.dtype)


def _hgrn2_lru(u, p, layer, bsz, seq):
    r = R_HG
    nt = seq // r
    n = bsz * seq
    npair = LRU_WIDTH // LANES

    def col(base, width):
        return pl.BlockSpec((r, width), lambda b, t: (b * nt + t, base // width))

    def const(shape):
        return pl.BlockSpec(shape, lambda b, t: (0,) * len(shape))

    def out(width):
        return pl.BlockSpec((r, width), lambda b, t: (b * nt + t, 0))

    return pl.pallas_call(
        functools.partial(_hgrn2_lru_kernel, layer),
        grid=(bsz, nt),
        in_specs=[
            col(COL_HQ, HG_WIDTH), col(COL_HF, HG_WIDTH), col(COL_HV, HG_WIDTH), col(COL_HG, HG_WIDTH),
            const(p["hg_lb"].shape), const((1, HG_WIDTH)), const((HG_LEVELS, HG_CHUNK, HG_CHUNK)),
            const((HG_LEVELS, HG_CHUNK, HG_EXPAND)), const((HG_LEVELS, HG_CHUNK, HG_EXPAND)),
            col(COL_LG, LRU_WIDTH), col(COL_LX, LRU_WIDTH),
            const((CONV_WIDTH, LRU_WIDTH)), const((1, LRU_WIDTH)),
            const((npair, LANES, LANES)), const((1, LRU_WIDTH)),
            const((npair, LANES, LANES)), const((1, LRU_WIDTH)),
            const((1, LRU_WIDTH)), const((1, LRU_WIDTH)),
        ],
        out_specs=[out(HG_WIDTH), out(LRU_WIDTH)],
        out_shape=[jax.ShapeDtypeStruct((n, HG_WIDTH), BF16), jax.ShapeDtypeStruct((n, LRU_WIDTH), BF16)],
        scratch_shapes=[
            pltpu.VMEM((HG_HEADS, HG_EXPAND, HG_EXPAND), F32),
            pltpu.VMEM((r + SUBLANES, LRU_WIDTH), F32),
            pltpu.VMEM((SUBLANES, LRU_WIDTH), F32),
            pltpu.VMEM((SUBLANES, LRU_WIDTH), F32),
        ],
        compiler_params=_cparams("arbitrary", "arbitrary"),
        name="hgrn2_lru_mixer",
    )(u, u, u, u, p["hg_lb"], p["hg_nw"], p["hg_msk"], p["hg_later"], p["hg_sgn"],
      u, u, p["lru_cw"], p["lru_cb"], p["lru_wa"], p["lru_ba"], p["lru_wx"], p["lru_bx"], p["lru_lam"], p["lru_nw"])


def _outproj_kernel(ys_ref, yh_ref, yl_ref, x_ref, w_ref, g_ref, nw_ref, sh_ref, sc_ref, xo_ref, *h_refs):
    acc = jnp.dot(ys_ref[...], w_ref[0:SSD_INNER, :], preferred_element_type=F32)
    acc = acc + jnp.dot(yh_ref[...], w_ref[SSD_INNER:SSD_INNER + HG_WIDTH, :], preferred_element_type=F32)
    acc = acc + jnp.dot(yl_ref[...], w_ref[SSD_INNER + HG_WIDTH:, :], preferred_element_type=F32)
    xn = x_ref[...] + g_ref[...] * acc
    xo_ref[...] = xn
    _store_planes(h_refs, _norm_mod(xn, nw_ref[...], sh_ref[...], sc_ref[...]))


def _outproj(y_ssd, y_hg, y_lru, x2, w_out, nw, mod3, seq):
    n, d = x2.shape
    tm = TM_OUTPROJ
    tpb = seq // tm

    def modspec(j):
        return pl.BlockSpec((None, 1, d), lambda i: ((i // tpb) * N_ADA + j, 0, 0))

    outs = pl.pallas_call(
        _outproj_kernel,
        grid=(n // tm,),
        in_specs=[
            pl.BlockSpec((tm, SSD_INNER), lambda i: (i, 0)),
            pl.BlockSpec((tm, HG_WIDTH), lambda i: (i, 0)),
            pl.BlockSpec((tm, LRU_WIDTH), lambda i: (i, 0)),
            pl.BlockSpec((tm, d), lambda i: (i, 0)),
            pl.BlockSpec(w_out.shape, lambda i: (0, 0)),
            modspec(2),
            pl.BlockSpec((1, d), lambda i: (0, 0)),
            modspec(3), modspec(4),
        ],
        out_specs=[pl.BlockSpec((tm, d), lambda i: (i, 0))] + [_plane_spec(tm)] * N_PLANES,
        out_shape=[jax.ShapeDtypeStruct((n, d), F32)] + [jax.ShapeDtypeStruct((n, LANES), jnp.uint32)] * N_PLANES,
        compiler_params=_cparams("arbitrary"),
        name="outproj",
    )(y_ssd, y_hg, y_lru, x2, w_out, mod3, nw, mod3, mod3)
    return outs[0], tuple(outs[1:])


def _plane_spec(tm):
    return pl.BlockSpec((tm, LANES), lambda i: (i, 0))


def _router_kernel(h0, h1, h2, h3, rw_ref, rb_ref, eid_ref, rank_ref, wt_ref, cnt_ref, carry, wscr):
    tm = h0.shape[0]

    @pl.when(pl.program_id(0) == 0)
    def _():
        carry[...] = jnp.zeros_like(carry)

    hb = _load_planes((h0, h1, h2, h3)).astype(BF16)
    logit_t = sum(lax.dot_general(part, hb, (((1,), (1,)), ((), ())), preferred_element_type=F32)
                  for part in _split3(rw_ref[...]))
    score = _sigmoid(logit_t)
    sel = score + rb_ref[...]
    neg_inf = jnp.float32(-jnp.inf)
    io_g = lax.broadcasted_iota(jnp.int32, (E_PER_GROUP, tm), 0)
    blocks, gscore = [], []
    for g in range(N_EXPERT_GROUPS):
        blk = sel[g * E_PER_GROUP:(g + 1) * E_PER_GROUP, :]
        m1 = jnp.max(blk, axis=0, keepdims=True)
        i1 = jnp.min(jnp.where(blk == m1, io_g, E_PER_GROUP), axis=0, keepdims=True)
        m2 = jnp.max(jnp.where(io_g == i1, neg_inf, blk), axis=0, keepdims=True)
        blocks.append(blk)
        gscore.append(m1 + m2)
    masked = []
    for g in range(N_EXPERT_GROUPS):
        rank = jnp.zeros((1, tm), jnp.int32)
        for o in range(N_EXPERT_GROUPS):
            if o == g:
                continue
            beats = (gscore[o] > gscore[g]) | ((gscore[o] == gscore[g]) & (o < g))
            rank = rank + beats.astype(jnp.int32)
        masked.append(jnp.where(rank < TOPK_GROUPS, blocks[g], MASK_SCORE))
    val = jnp.concatenate(masked, axis=0)
    io_e = lax.broadcasted_iota(jnp.int32, (N_EXPERTS, tm), 0)
    chosen = jnp.zeros((N_EXPERTS, tm), jnp.bool_)
    picks = []
    for k in range(TOP_K):
        m = jnp.max(val, axis=0, keepdims=True)
        idx = jnp.min(jnp.where(val == m, io_e, N_EXPERTS), axis=0, keepdims=True)
        pick = io_e == idx
        picks.append(pick)
        eid_ref[k:k + 1, :] = idx
        chosen = chosen | pick
        val = jnp.where(pick, neg_inf, val)
    w = jnp.where(chosen, score, 0.0)
    w = w / jnp.sum(w, axis=0, keepdims=True) * ROUTED_SCALE

    chosen_f = chosen.astype(F32)
    earlier = (lax.broadcasted_iota(jnp.int32, (tm, tm), 0) < lax.broadcasted_iota(jnp.int32, (tm, tm), 1))
    before = jnp.dot(chosen_f.astype(BF16), earlier.astype(BF16), preferred_element_type=F32)
    grank = carry[:, 0:1] + before
    wscr[...] = jnp.zeros_like(wscr)
    for k in range(TOP_K):
        rank_ref[k:k + 1, :] = jnp.sum(jnp.where(picks[k], grank, 0.0), axis=0, keepdims=True).astype(jnp.int32)
        wscr[k:k + 1, :] = jnp.sum(jnp.where(picks[k], w, 0.0), axis=0, keepdims=True)
    wt_ref[...] = wscr[...].T
    carry[...] = carry[...] + jnp.sum(chosen_f, axis=1, keepdims=True)
    cnt_ref[...] = carry[...]


def _router(hp, rw_t, rb):
    n = hp[0].shape[0]
    tm = TM_ROUTER
    return pl.pallas_call(
        _router_kernel,
        grid=(n // tm,),
        in_specs=[_plane_spec(tm)] * N_PLANES + [
            pl.BlockSpec(rw_t.shape, lambda i: (0, 0)),
            pl.BlockSpec((N_EXPERTS, 1), lambda i: (0, 0)),
        ],
        out_specs=[
            pl.BlockSpec((TOP_K, tm), lambda i: (0, i)),
            pl.BlockSpec((TOP_K, tm), lambda i: (0, i)),
            pl.BlockSpec((tm, LANES), lambda i: (i, 0)),
            pl.BlockSpec((N_EXPERTS, LANES), lambda i: (0, 0)),
        ],
        out_shape=[
            jax.ShapeDtypeStruct((TOP_K, n), jnp.int32),
            jax.ShapeDtypeStruct((TOP_K, n), jnp.int32),
            jax.ShapeDtypeStruct((n, LANES), F32),
            jax.ShapeDtypeStruct((N_EXPERTS, LANES), F32),
        ],
        scratch_shapes=[pltpu.VMEM((N_EXPERTS, LANES), F32), pltpu.VMEM((LANES, tm), F32)],
        compiler_params=_cparams("arbitrary"),
        name="router",
    )(*hp, rw_t, rb)


def _plan_kernel(n_slots, cnt_ref, eid_ref, rank_ref, dest_ref, be_ref, nb_ref, pad_ref, np_ref):
    cnt = cnt_ref[...].astype(jnp.int32)
    padded = ((cnt + (MOE_BLOCK - 1)) >> MOE_BLOCK_LOG2) << MOE_BLOCK_LOG2
    ri = lax.broadcasted_iota(jnp.int32, (N_EXPERTS, N_EXPERTS), 0)
    ci = lax.broadcasted_iota(jnp.int32, (N_EXPERTS, N_EXPERTS), 1)
    pad_end = jnp.dot((ri >= ci).astype(F32), padded.astype(F32), precision=HI,
                      preferred_element_type=F32).astype(jnp.int32)
    pad_start = pad_end - padded
    eid = eid_ref[...]
    dest = rank_ref[...]
    for e in range(N_EXPERTS):
        dest = dest + jnp.where(eid == e, pad_start[e:e + 1, 0:1], 0)
    dest_ref[...] = dest
    nbp = be_ref.shape[1]
    jpos = lax.broadcasted_iota(jnp.int32, (N_EXPERTS, nbp), 1) * MOE_BLOCK
    be = jnp.minimum(jnp.sum((pad_end[:, 0:1] <= jpos).astype(jnp.int32), axis=0, keepdims=True), N_EXPERTS - 1)
    be_ref[...] = be
    nb_ref[...] = pad_end[N_EXPERTS - 1:N_EXPERTS, :] >> MOE_BLOCK_LOG2
    fill_end = pad_start + cnt
    io_e = lax.broadcasted_iota(jnp.int32, (N_EXPERTS, nbp), 0)
    end_j = jnp.sum(jnp.where(io_e == be, fill_end[:, 0:1], 0), axis=0, keepdims=True)
    pieces = (end_j - jpos[0:1, :] + (XS_PIECE - 1)) >> XS_PIECE_LOG2
    np_ref[...] = jnp.clip(pieces, 0, MOE_BLOCK // XS_PIECE)
    lane = lax.broadcasted_iota(jnp.int32, (N_EXPERTS, LANES), 1)
    piece_end = ((fill_end + (XS_PIECE - 1)) >> XS_PIECE_LOG2) << XS_PIECE_LOG2
    s = fill_end + lane
    pad_ref[...] = jnp.where(s < piece_end, s, n_slots + lane)


def _plan(cnt, eid, rank, n_slots):
    n = eid.shape[1]
    nbp = -(-(n_slots // MOE_BLOCK) // LANES) * LANES
    pad_rows = N_EXPERTS
    return pl.pallas_call(
        functools.partial(_plan_kernel, n_slots),
        grid=(1,),
        in_specs=[
            pl.BlockSpec(cnt.shape, lambda i: (0, 0)),
            pl.BlockSpec(eid.shape, lambda i: (0, 0)),
            pl.BlockSpec(rank.shape, lambda i: (0, 0)),
        ],
        out_specs=[
            pl.BlockSpec((TOP_K, n), lambda i: (0, 0)),
            pl.BlockSpec((1, nbp), lambda i: (0, 0)),
            pl.BlockSpec((1, LANES), lambda i: (0, 0)),
            pl.BlockSpec((pad_rows, LANES), lambda i: (0, 0)),
            pl.BlockSpec((1, nbp), lambda i: (0, 0)),
        ],
        out_shape=[
            jax.ShapeDtypeStruct((TOP_K, n), jnp.int32),
            jax.ShapeDtypeStruct((1, nbp), jnp.int32),
            jax.ShapeDtypeStruct((1, LANES), jnp.int32),
            jax.ShapeDtypeStruct((pad_rows, LANES), jnp.int32),
            jax.ShapeDtypeStruct((1, nbp), jnp.int32),
        ],
        compiler_params=_cparams("arbitrary"),
        name="moe_plan",
    )(cnt, eid, rank)


def _sc_mesh():
    return plsc.VectorSubcoreMesh(core_axis_name="c", subcore_axis_name="s")


def _sc_worker():
    return lax.axis_index("c") * SC_SUBCORES + lax.axis_index("s")


def _sc_dispatch(hp, dest_rows, pad_rows, n_rows):
    n = hp[0].shape[0]
    tiles_per_worker = n // SC_WIN // SC_WORKERS
    pad_per_worker = pad_rows.shape[0] // SC_WORKERS
    zeros = jnp.zeros((SC_WIN, LANES), jnp.uint32)

    def body(*refs):
        h = refs[:N_PLANES]
        dest_hbm, pad_hbm, z_hbm = refs[N_PLANES:N_PLANES + 3]
        xs = refs[N_PLANES + 3:2 * N_PLANES + 3]
        bufs = refs[2 * N_PLANES + 3:3 * N_PLANES + 3]
        ibuf, pbuf, sem = refs[3 * N_PLANES + 3:]
        wid = _sc_worker()

        pltpu.sync_copy(z_hbm, bufs[0])
        pltpu.sync_copy(pad_hbm.at[pl.ds(wid * pad_per_worker, pad_per_worker)], pbuf)
        copies = [pltpu.async_copy(bufs[0], xs[c].at[pbuf.at[r]], sem)
                  for r in range(pad_per_worker) for c in range(N_PLANES)]
        for cp in copies:
            cp.wait()

        @pl.loop(0, tiles_per_worker)
        def _(i):
            tile = wid * tiles_per_worker + i
            pltpu.sync_copy(dest_hbm.at[pl.ds(tile * TOP_K, TOP_K)], ibuf)
            for c in range(N_PLANES):
                pltpu.sync_copy(h[c].at[pl.ds(tile * SC_WIN, SC_WIN)], bufs[c])
            scatters = [pltpu.async_copy(bufs[c], xs[c].at[ibuf.at[k]], sem)
                        for c in range(N_PLANES) for k in range(TOP_K)]
            for cp in scatters:
                cp.wait()

    out_type = tuple(jax.ShapeDtypeStruct((n_rows, LANES), jnp.uint32) for _ in range(N_PLANES))
    scratch = ([pltpu.VMEM((SC_WIN, LANES), jnp.uint32)] * N_PLANES
               + [pltpu.VMEM((TOP_K, LANES), jnp.int32), pltpu.VMEM((pad_per_worker, LANES), jnp.int32),
                  pltpu.SemaphoreType.DMA])
    return pl.kernel(body, out_type=out_type, mesh=_sc_mesh(), scratch_types=scratch,
                     name="moe_sc_dispatch")(*hp, dest_rows, pad_rows, zeros)


def _sc_gather(ysp, dest_rows, n):
    tiles_per_worker = n // SC_WIN // SC_WORKERS

    def body(*refs):
        ys = refs[:N_PLANES]
        dest_hbm, g_hbm = refs[N_PLANES:N_PLANES + 2]
        bufs = refs[N_PLANES + 2:N_PLANES + 2 + SC_GATHER_BUFS]
        ibuf, sem = refs[N_PLANES + 2 + SC_GATHER_BUFS:]
        wid = _sc_worker()

        @pl.loop(0, tiles_per_worker)
        def _(i):
            tile = wid * tiles_per_worker + i
            pltpu.sync_copy(dest_hbm.at[pl.ds(tile * TOP_K, TOP_K)], ibuf)
            for c in range(N_PLANES):
                for k0 in range(0, TOP_K, SC_GATHER_BUFS):
                    gathers = [pltpu.async_copy(ys[c].at[ibuf.at[k0 + j]], bufs[j], sem)
                               for j in range(SC_GATHER_BUFS)]
                    for cp in gathers:
                        cp.wait()
                    stores = [pltpu.async_copy(
                        bufs[j], g_hbm.at[pl.ds(((k0 + j) * N_PLANES + c) * n + tile * SC_WIN, SC_WIN)], sem)
                        for j in range(SC_GATHER_BUFS)]
                    for cp in stores:
                        cp.wait()

    scratch = ([pltpu.VMEM((SC_WIN, LANES), jnp.uint32)] * SC_GATHER_BUFS
               + [pltpu.VMEM((TOP_K, LANES), jnp.int32), pltpu.SemaphoreType.DMA])
    return pl.kernel(body, out_type=jax.ShapeDtypeStruct((TOP_K * N_PLANES * n, LANES), jnp.uint32),
                     mesh=_sc_mesh(), scratch_types=scratch, name="moe_sc_gather")(*ysp, dest_rows)


def _expert_kernel(be_ref, nb_ref, np_ref, *refs):
    xs_hbm = refs[:N_PLANES]
    wg_ref, wu_ref, wd_ref = refs[N_PLANES:N_PLANES + 3]
    ys_refs = refs[N_PLANES + 3:2 * N_PLANES + 3]
    wg_b, wu_b, wd_b, ring, sems = refs[2 * N_PLANES + 3:]
    j = pl.program_id(0)
    nb = nb_ref[0]
    used = j < nb
    new_expert = (j == 0) | (be_ref[j] != be_ref[jnp.maximum(j - 1, 0)])

    def fetch(step, slot, wait):
        for p in range(MOE_BLOCK // XS_PIECE):
            @pl.when(p < np_ref[step])
            def _():
                src = pl.ds(pl.multiple_of(step * MOE_BLOCK + p * XS_PIECE, XS_PIECE), XS_PIECE)
                for c in range(N_PLANES):
                    cp = pltpu.make_async_copy(xs_hbm[c].at[src], ring.at[slot, c, pl.ds(p * XS_PIECE, XS_PIECE)],
                                               sems.at[slot])
                    if wait:
                        cp.wait()
                    else:
                        cp.start()

    @pl.when(j == 0)
    def _():
        ring[...] = jnp.zeros_like(ring)

    for s in range(XS_RING - 1):
        @pl.when((j == 0) & (s < nb))
        def _():
            fetch(s, s, wait=False)

    ahead = j + (XS_RING - 1)

    @pl.when(used & (ahead < nb))
    def _():
        fetch(ahead, lax.rem(ahead, XS_RING), wait=False)

    @pl.when(used & new_expert)
    def _():
        wg_b[...] = wg_ref[...].astype(BF16)
        wu_b[...] = wu_ref[...].astype(BF16)
        wd_b[...] = wd_ref[...].astype(BF16)

    @pl.when(used)
    def _():
        slot = lax.rem(j, XS_RING)
        fetch(j, slot, wait=True)
        x = _unpack_bf16_pairs(jnp.concatenate([ring[slot, c] for c in range(N_PLANES)], axis=1)).astype(BF16)
        a = jnp.dot(x, wg_b[...], preferred_element_type=F32)
        u = jnp.dot(x, wu_b[...], preferred_element_type=F32)
        y = jnp.dot((_silu(a) * u).astype(BF16), wd_b[...], preferred_element_type=F32)
        _store_planes(ys_refs, y)

    @pl.when(jnp.logical_not(used))
    def _():
        for ref in ys_refs:
            ref[...] = jnp.zeros_like(ref)


def _experts(be, nb, npieces, xsp, wg, wu, wd, layer, n_slots):
    d = wg.shape[2]

    def wspec(shape):
        def index(j, be_ref, nb_ref, np_ref):
            return layer, be_ref[jnp.minimum(j, jnp.maximum(nb_ref[0] - 1, 0))], 0, 0
        return pl.BlockSpec((None, None) + shape, index)

    grid_spec = pltpu.PrefetchScalarGridSpec(
        num_scalar_prefetch=3,
        grid=(n_slots // MOE_BLOCK,),
        in_specs=[pl.BlockSpec(memory_space=pl.ANY)] * N_PLANES
        + [wspec((d, D_EXPERT)), wspec((d, D_EXPERT)), wspec((D_EXPERT, d))],
        out_specs=[pl.BlockSpec((MOE_BLOCK, LANES), lambda j, be_ref, nb_ref, np_ref: (j, 0))] * N_PLANES,
        scratch_shapes=[pltpu.VMEM((d, D_EXPERT), BF16), pltpu.VMEM((d, D_EXPERT), BF16),
                        pltpu.VMEM((D_EXPERT, d), BF16),
                        pltpu.VMEM((XS_RING, N_PLANES, MOE_BLOCK, LANES), jnp.uint32),
                        pltpu.SemaphoreType.DMA((XS_RING,))],
    )
    return pl.pallas_call(
        _expert_kernel,
        grid_spec=grid_spec,
        out_shape=[jax.ShapeDtypeStruct((n_slots, LANES), jnp.uint32)] * N_PLANES,
        compiler_params=_cparams("arbitrary"),
        name="moe_experts",
    )(be, nb, npieces, *xsp, wg, wu, wd)


def _combine_kernel(final, g_ref, wt_ref, h0, h1, h2, h3, sg_ref, su_ref, sd_ref, x_ref, gate_ref, fw_ref, o_ref):
    hb = _load_planes((h0, h1, h2, h3)).astype(BF16)
    a = jnp.dot(hb, sg_ref[...], preferred_element_type=F32)
    u = jnp.dot(hb, su_ref[...], preferred_element_type=F32)
    acc = jnp.dot((_silu(a) * u).astype(BF16), sd_ref[...], preferred_element_type=F32)
    wt = wt_ref[...]
    for k in range(TOP_K):
        rows = _unpack_bf16_pairs(jnp.concatenate([g_ref[k * N_PLANES + c] for c in range(N_PLANES)], axis=1))
        acc = acc + wt[:, k:k + 1] * rows
    xn = x_ref[...] + gate_ref[...] * acc
    if final:
        ms = jnp.mean(xn * xn, axis=-1, keepdims=True)
        xn = xn * lax.rsqrt(ms + EPS) * fw_ref[...]
    o_ref[...] = xn


def _combine(g, wt, hp, sg, su, sd, x2, mod3, fw, seq, final):
    n, d = x2.shape
    tm = TM_COMBINE
    tpb = seq // tm
    return pl.pallas_call(
        functools.partial(_combine_kernel, final),
        grid=(n // tm,),
        in_specs=[
            pl.BlockSpec((TOP_K * N_PLANES, tm, LANES), lambda i: (0, i, 0)),
            pl.BlockSpec((tm, LANES), lambda i: (i, 0)),
        ] + [_plane_spec(tm)] * N_PLANES + [
            pl.BlockSpec(sg.shape, lambda i: (0, 0)),
            pl.BlockSpec(su.shape, lambda i: (0, 0)),
            pl.BlockSpec(sd.shape, lambda i: (0, 0)),
            pl.BlockSpec((tm, d), lambda i: (i, 0)),
            pl.BlockSpec((None, 1, d), lambda i: ((i // tpb) * N_ADA + 5, 0, 0)),
            pl.BlockSpec((1, d), lambda i: (0, 0)),
        ],
        out_specs=pl.BlockSpec((tm, d), lambda i: (i, 0)),
        out_shape=jax.ShapeDtypeStruct((n, d), F32),
        compiler_params=_cparams("arbitrary"),
        name="moe_combine",
    )(g.reshape(TOP_K * N_PLANES, n, LANES), wt, *hp, sg, su, sd, x2, mod3, fw)


def _blockdiag_pairs(w):
    z = jnp.zeros((LRU_BLOCK_W, LRU_BLOCK_W), w.dtype)
    tiles = []
    for j in range(LRU_BLOCKS // 2):
        top = jnp.concatenate([w[2 * j], z], axis=1)
        bot = jnp.concatenate([z, w[2 * j + 1]], axis=1)
        tiles.append(jnp.concatenate([top, bot], axis=0))
    return jnp.stack(tiles).astype(BF16)


def _hg_level_masks():
    ch = HG_CHUNK
    msk = np.zeros((HG_LEVELS, ch, ch), np.float32)
    for lvl in range(HG_LEVELS):
        half = 1 << lvl
        for t in range(ch):
            base = (t // (2 * half)) * (2 * half)
            if (t // half) % 2 == 1:
                msk[lvl, t, base:base + half] = 1.0
    later = ((np.arange(ch)[None, :, None] >> np.arange(HG_LEVELS)[:, None, None]) & 1).astype(np.float32)
    later = np.broadcast_to(later, (HG_LEVELS, ch, HG_EXPAND))
    sgn = (2.0 * later - 1.0) * np.float32(LOG2E)
    return jnp.asarray(msk), jnp.asarray(later), jnp.asarray(sgn, dtype=F32)


def _ssd_expand():
    e = np.zeros((LANES, SSD_INNER), np.float32)
    for h in range(SSD_HEADS):
        e[h, h * SSD_HEADDIM:(h + 1) * SSD_HEADDIM] = 1.0
    return jnp.asarray(np.concatenate([e] * 3, axis=0), dtype=BF16)


def _pad_lanes(v, width):
    return jnp.pad(v, (0, width - v.shape[0])).reshape(1, width)


def _layer_params(l, w_in, ssd_conv_w, ssd_conv_b, ssd_dt_bias, ssd_a_log, ssd_d, ssd_norm_w, hg_lower_bounds,
                  hg_norm_w, lru_conv_w, lru_conv_b, lru_wa, lru_ba, lru_wx, lru_bx, lru_lambda, lru_norm_w):
    wi = w_in[l]
    dt0 = SSD_INNER + SSD_INNER + 2 * SSD_GROUPS * SSD_STATE
    w_cat = jnp.concatenate([wi[:, :dt0], wi[:, dt0 + SSD_HEADS:], wi[:, dt0:dt0 + SSD_HEADS]], axis=1)
    w_cat = jnp.pad(w_cat, ((0, 0), (0, U_WIDTH - w_cat.shape[1]))).astype(BF16)
    msk, later, sgn = _hg_level_masks()
    return dict(
        w_cat=w_cat,
        cwx=ssd_conv_w[l][:, :SSD_INNER], cbx=ssd_conv_b[l][:SSD_INNER].reshape(1, -1),
        cwb=ssd_conv_w[l][:, SSD_INNER:], cbb=ssd_conv_b[l][SSD_INNER:].reshape(1, -1),
        dtb=_pad_lanes(ssd_dt_bias[l], LANES), alog=_pad_lanes(ssd_a_log[l], LANES),
        dful=jnp.repeat(ssd_d[l], SSD_HEADDIM).reshape(1, -1), ssd_nw=ssd_norm_w[l].reshape(1, -1),
        expand=_ssd_expand(),
        hg_lb=hg_lower_bounds, hg_nw=hg_norm_w[l].reshape(1, -1), hg_msk=msk, hg_later=later, hg_sgn=sgn,
        lru_cw=lru_conv_w[l], lru_cb=lru_conv_b[l].reshape(1, -1),
        lru_wa=_blockdiag_pairs(lru_wa[l]), lru_ba=lru_ba[l].reshape(1, -1),
        lru_wx=_blockdiag_pairs(lru_wx[l]), lru_bx=lru_bx[l].reshape(1, -1),
        lru_lam=lru_lambda[l].reshape(1, -1), lru_nw=lru_norm_w[l].reshape(1, -1),
    )


def kernel(x, c, ada_w, ada_b, norm_mix_w, norm_ffn_w, w_in, ssd_conv_w, ssd_conv_b, ssd_dt_bias, ssd_a_log, ssd_d, ssd_norm_w, hg_lower_bounds, hg_norm_w, lru_conv_w, lru_conv_b, lru_wa, lru_ba, lru_wx, lru_bx, lru_lambda, lru_norm_w, w_out, router_w, router_bias, exp_gate, exp_up, exp_down, sh_gate, sh_up, sh_down, final_norm_w):
    bsz, seq, d = x.shape
    depth = ada_w.shape[0]
    assert d == D_MODEL and seq % TM_OUTPROJ == 0 and seq % SSD_CHUNK == 0
    n = bsz * seq
    n_slots = n * TOP_K + N_EXPERTS * MOE_BLOCK
    x2 = x.reshape(n, d)
    mod = _adaln(c, ada_w, ada_b)
    fw = final_norm_w.reshape(1, d)
    for l in range(depth):
        p = _layer_params(l, w_in, ssd_conv_w, ssd_conv_b, ssd_dt_bias, ssd_a_log, ssd_d, ssd_norm_w,
                          hg_lower_bounds, hg_norm_w, lru_conv_w, lru_conv_b, lru_wa, lru_ba, lru_wx, lru_bx,
                          lru_lambda, lru_norm_w)
        mod3 = mod[l].reshape(bsz * N_ADA, 1, d)
        u = _inproj(x2, norm_mix_w[l].reshape(1, d), mod3, p["w_cat"], seq)
        y_ssd = _ssd(u, p, bsz, seq)
        y_hg, y_lru = _hgrn2_lru(u, p, l, bsz, seq)
        x2, hp = _outproj(y_ssd, y_hg, y_lru, x2, w_out[l].astype(BF16), norm_ffn_w[l].reshape(1, d), mod3, seq)
        eid, rank, wt, cnt = _router(hp, router_w[l].T, router_bias[l].reshape(N_EXPERTS, 1))
        dest, be, nb, pad_rows, npieces = _plan(cnt, eid, rank, n_slots)
        dest_rows = dest.reshape(TOP_K, n // LANES, LANES).transpose(1, 0, 2).reshape(n // LANES * TOP_K, LANES)
        xsp = _sc_dispatch(hp, dest_rows, pad_rows, n_slots + LANES)
        ysp = _experts(be.reshape(-1), nb[0, :1], npieces.reshape(-1), xsp, exp_gate, exp_up, exp_down, l, n_slots)
        g = _sc_gather(ysp, dest_rows, n)
        x2 = _combine(g, wt, hp, sh_gate[l].astype(BF16), sh_up[l].astype(BF16), sh_down[l].astype(BF16),
                      x2, mod3, fw, seq, final=(l == depth - 1))
    return x2.reshape(bsz, seq, d)
```

```python
import functools

import jax
import jax.numpy as jnp
import numpy as np
from jax import lax
from jax.experimental import pallas as pl
from jax.experimental.pallas import tpu as pltpu
from jax.experimental.pallas import tpu_sc as plsc

F32 = jnp.float32
BF16 = jnp.bfloat16
HI = lax.Precision.HIGHEST
F32_TINY = float(np.finfo(np.float32).tiny)
LOG2E = float(np.log2(np.e))

LANES = 128
SUBLANES = 8
VMEM_LIMIT_BYTES = 56 * 1024 * 1024

D_MODEL = 1024
EPS = 1e-6
N_ADA = 6
CONV_WIDTH = 4
SSD_INNER = 1024
SSD_HEADDIM = 64
SSD_HEADS = 16
SSD_GROUPS = 2
SSD_STATE = 128
SSD_CHUNK = 128
SSD_GROUP_W = SSD_INNER // SSD_GROUPS
HG_WIDTH = 512
HG_EXPAND = 128
HG_HEADS = 4
HG_CHUNK = 128
HG_LEVELS = 7
LRU_WIDTH = 512
LRU_BLOCKS = 8
LRU_BLOCK_W = 64
LRU_C = 8.0
N_EXPERTS = 64
TOP_K = 8
N_EXPERT_GROUPS = 8
E_PER_GROUP = 8
TOPK_GROUPS = 4
D_EXPERT = 256
ROUTED_SCALE = 2.5
MASK_SCORE = -1.0e4

COL_Z = 0
COL_XS = 1024
COL_BC = 2048
COL_HQ = 2560
COL_HF = 3072
COL_HV = 3584
COL_HG = 4096
COL_LG = 4608
COL_LX = 5120
COL_DT = 5632
U_WIDTH = 5760

TM_INPROJ = 512
TM_OUTPROJ = 1024
OUTPROJ_PARTS = 4
TM_COMBINE = 512
SC_CORES = 2
SC_SUBCORES = 16
SC_WORKERS = SC_CORES * SC_SUBCORES
SC_WIN = LANES
SC_GATHER_BUFS = 4
MOE_BLOCK_LOG2 = 10
MOE_BLOCK = 1 << MOE_BLOCK_LOG2
XS_RING = 3
XS_PIECE_LOG2 = 7
XS_PIECE = 1 << XS_PIECE_LOG2
R_HG = 256


def _cparams(*sem):
    return pltpu.CompilerParams(dimension_semantics=sem, vmem_limit_bytes=VMEM_LIMIT_BYTES)


def _sigmoid(x):
    return 0.5 * jnp.tanh(0.5 * x) + 0.5


def _silu(x):
    h = 0.5 * x
    return h + h * jnp.tanh(h)


def _split3(x):
    x1 = x.astype(BF16)
    r1 = x - x1.astype(F32)
    x2 = r1.astype(BF16)
    x3 = (r1 - x2.astype(F32)).astype(BF16)
    return x1, x2, x3


def _softplus(x):
    return jnp.maximum(x, 0.0) + jnp.log1p(jnp.exp(-jnp.abs(x)))


def _norm_mod(x, nw, shift, scale):
    ms = jnp.mean(x * x, axis=-1, keepdims=True)
    y = x * lax.rsqrt(ms + EPS) * nw
    return y * (1.0 + scale) + shift


_HI16 = np.uint32(0xFFFF0000)


def _pack_bf16_pairs(x):
    half = x.shape[1] // 2
    bits = lax.bitcast_convert_type(x.astype(BF16).astype(F32), jnp.uint32)
    return (bits[:, :half] & _HI16) | (bits[:, half:] >> 16)


def _unpack_bf16_pairs(w):
    hi = lax.bitcast_convert_type(w & _HI16, F32)
    lo = lax.bitcast_convert_type(w << 16, F32)
    return jnp.concatenate([hi, lo], axis=1)


N_PLANES = D_MODEL // 2 // LANES


def _store_planes(refs, x):
    packed = _pack_bf16_pairs(x)
    for c, ref in enumerate(refs):
        ref[...] = packed[:, c * LANES:(c + 1) * LANES]


def _load_planes(refs):
    return _unpack_bf16_pairs(jnp.concatenate([ref[...] for ref in refs], axis=1))


def _ada_kernel(c_ref, w_ref, b_ref, o_ref):
    c = c_ref[...]
    o_ref[...] = jnp.dot(_silu(c), w_ref[...], precision=HI, preferred_element_type=F32) + b_ref[...]


def _adaln(c, ada_w, ada_b):
    depth, d, n6 = ada_w.shape
    bsz = c.shape[0]
    tn = 1536
    return pl.pallas_call(
        _ada_kernel,
        grid=(depth, n6 // tn),
        in_specs=[
            pl.BlockSpec((bsz, d), lambda l, j: (0, 0)),
            pl.BlockSpec((None, d, tn), lambda l, j: (l, 0, j)),
            pl.BlockSpec((None, 1, tn), lambda l, j: (l, 0, j)),
        ],
        out_specs=pl.BlockSpec((None, bsz, tn), lambda l, j: (l, 0, j)),
        out_shape=jax.ShapeDtypeStruct((depth, bsz, n6), F32),
        compiler_params=_cparams("arbitrary", "arbitrary"),
        name="adaln_mod",
    )(c, ada_w, ada_b.reshape(depth, 1, n6))


def _inproj_kernel(x_ref, nw_ref, sh_ref, sc_ref, w_ref, o_ref):
    h = _norm_mod(x_ref[...], nw_ref[...], sh_ref[...], sc_ref[...])
    o_ref[...] = jnp.dot(h.astype(BF16), w_ref[...], preferred_element_type=F32)


def _inproj(x2, nw, mod3, w_cat, seq):
    n, d = x2.shape
    tm = TM_INPROJ
    tpb = seq // tm
    return pl.pallas_call(
        _inproj_kernel,
        grid=(n // tm,),
        in_specs=[
            pl.BlockSpec((tm, d), lambda i: (i, 0)),
            pl.BlockSpec((1, d), lambda i: (0, 0)),
            pl.BlockSpec((None, 1, d), lambda i: ((i // tpb) * N_ADA + 0, 0, 0)),
            pl.BlockSpec((None, 1, d), lambda i: ((i // tpb) * N_ADA + 1, 0, 0)),
            pl.BlockSpec((d, U_WIDTH), lambda i: (0, 0), pipeline_mode=pl.Buffered(1)),
        ],
        out_specs=pl.BlockSpec((tm, U_WIDTH), lambda i: (i, 0)),
        out_shape=jax.ShapeDtypeStruct((n, U_WIDTH), F32),
        compiler_params=_cparams("arbitrary"),
        name="inproj",
    )(x2, nw, mod3, mod3, w_cat)


def _causal_conv(cur_ref, ext, tail, cw_ref, cb_ref):
    rows, width = cur_ref.shape
    cur = cur_ref[...]
    ext[0:SUBLANES, :] = tail[...]
    ext[SUBLANES:SUBLANES + rows, :] = cur
    tail[...] = cur[rows - SUBLANES:rows, :]
    groups = ext[...].reshape(rows // SUBLANES + 1, SUBLANES, width)
    sub = lax.broadcasted_iota(jnp.int32, (1, SUBLANES, width), 1)
    acc = cb_ref[...] + cur * cw_ref[CONV_WIDTH - 1:CONV_WIDTH, :]
    for d in range(1, CONV_WIDTH):
        rot = pltpu.roll(groups, d, 1)
        back = jnp.where(sub < d, rot[:-1], rot[1:]).reshape(rows, width)
        acc = acc + back * cw_ref[CONV_WIDTH - 1 - d:CONV_WIDTH - d, :]
    return acc


def _ssd_kernel(z_ref, xs_ref, bc_ref, dt_ref, cwx_ref, cbx_ref, cwb_ref, cbb_ref, dtb_ref, alog_ref,
                dful_ref, nw_ref, e_ref, o_ref, extx, extb, tailx, tailb, hstate):
    c = pl.program_id(1)
    q = SSD_CHUNK

    @pl.when(c == 0)
    def _():
        tailx[...] = jnp.zeros_like(tailx)
        tailb[...] = jnp.zeros_like(tailb)
        hstate[...] = jnp.zeros_like(hstate)

    xs = _silu(_causal_conv(xs_ref, extx, tailx, cwx_ref, cbx_ref))
    bc = _silu(_causal_conv(bc_ref, extb, tailb, cwb_ref, cbb_ref))

    dt = _softplus(dt_ref[...] + dtb_ref[...])
    a = dt * (-jnp.exp(alog_ref[...]))
    ri = lax.broadcasted_iota(jnp.int32, (q, q), 0)
    ci = lax.broadcasted_iota(jnp.int32, (q, q), 1)
    tril = ri >= ci
    tril_f = tril.astype(F32)
    tril_b = tril.astype(BF16)
    acum = jnp.dot(jnp.concatenate([tril_b] * 3, axis=1), jnp.concatenate(_split3(a), axis=0),
                   preferred_element_type=F32)
    acum_t = acum.T
    expand3 = e_ref[...]
    dt_full = jnp.dot(jnp.concatenate(_split3(dt), axis=1), expand3, preferred_element_type=F32)
    acum_full = jnp.dot(jnp.concatenate(_split3(acum), axis=1), expand3, preferred_element_type=F32)
    alast_full = acum_full[q - 1:q, :]

    xdt = xs * dt_full
    xdt_b = xdt.astype(BF16)
    exp_a = jnp.exp(acum_full)
    xd_b = (xdt * jnp.exp(alast_full - acum_full)).astype(BF16)
    state_decay = jnp.exp(alast_full)
    left = lax.broadcasted_iota(jnp.int32, (q, LANES), 1) < SSD_HEADDIM
    zero_b = jnp.zeros((q, LANES), BF16)

    ys = []
    for g in range(SSD_GROUPS):
        b_g = bc[:, g * SSD_STATE:(g + 1) * SSD_STATE]
        c_g = bc[:, (SSD_GROUPS + g) * SSD_STATE:(SSD_GROUPS + g + 1) * SSD_STATE]
        c_b = c_g.astype(BF16)
        cb = lax.dot_general(c_b, b_g.astype(BF16), (((1,), (1,)), ((), ())), preferred_element_type=F32)
        cb = cb * tril_f
        cs = slice(g * SSD_GROUP_W, (g + 1) * SSD_GROUP_W)
        h_g = hstate[:, cs]
        y_off = jnp.dot(c_b, h_g.astype(BF16), preferred_element_type=F32) * exp_a[:, cs]
        pieces = []
        for pr in range(SSD_HEADS // SSD_GROUPS // 2):
            h0 = g * (SSD_HEADS // SSD_GROUPS) + 2 * pr
            ms = []
            for h in (h0, h0 + 1):
                col = acum[:, h:h + 1]
                row = acum_t[h:h + 1, :]
                ms.append((cb * jnp.exp(jnp.minimum(col - row, 0.0))).astype(BF16))
            lhs = jnp.concatenate(ms, axis=1)
            xp = xdt_b[:, h0 * SSD_HEADDIM:(h0 + 2) * SSD_HEADDIM]
            rhs = jnp.concatenate([jnp.where(left, xp, zero_b), jnp.where(left, zero_b, xp)], axis=0)
            pieces.append(jnp.dot(lhs, rhs, preferred_element_type=F32))
        ys.append(jnp.concatenate(pieces, axis=1) + y_off)
        b_t = b_g.T.astype(BF16)
        hstate[:, cs] = h_g * state_decay[:, cs] + jnp.dot(b_t, xd_b[:, cs], preferred_element_type=F32)

    y = jnp.concatenate(ys, axis=1) + xs * dful_ref[...]
    y = y * _silu(z_ref[...])
    outs = []
    for g in range(SSD_GROUPS):
        cs = slice(g * SSD_GROUP_W, (g + 1) * SSD_GROUP_W)
        yg = y[:, cs]
        ms = jnp.mean(yg * yg, axis=-1, keepdims=True)
        outs.append(yg * lax.rsqrt(ms + EPS) * nw_ref[:, cs])
    o_ref[...] = jnp.concatenate(outs, axis=1).astype(o_ref.dtype)


def _ssd(u, p, bsz, seq):
    q = SSD_CHUNK
    nc = seq // q
    n = bsz * seq

    def rows(b, c):
        return b * nc + c

    def const(shape):
        return pl.BlockSpec(shape, lambda b, c: (0,) * len(shape))

    return pl.pallas_call(
        _ssd_kernel,
        grid=(bsz, nc),
        in_specs=[
            pl.BlockSpec((q, SSD_INNER), lambda b, c: (rows(b, c), COL_Z // SSD_INNER)),
            pl.BlockSpec((q, SSD_INNER), lambda b, c: (rows(b, c), COL_XS // SSD_INNER)),
            pl.BlockSpec((q, 512), lambda b, c: (rows(b, c), COL_BC // 512)),
            pl.BlockSpec((q, LANES), lambda b, c: (rows(b, c), COL_DT // LANES)),
            const((CONV_WIDTH, SSD_INNER)), const((1, SSD_INNER)),
            const((CONV_WIDTH, 512)), const((1, 512)),
            const((1, LANES)), const((1, LANES)),
            const((1, SSD_INNER)), const((1, SSD_INNER)),
            const((3 * LANES, SSD_INNER)),
        ],
        out_specs=pl.BlockSpec((q, SSD_INNER), lambda b, c: (rows(b, c), 0)),
        out_shape=jax.ShapeDtypeStruct((n, SSD_INNER), BF16),
        scratch_shapes=[
            pltpu.VMEM((q + SUBLANES, SSD_INNER), F32),
            pltpu.VMEM((q + SUBLANES, 512), F32),
            pltpu.VMEM((SUBLANES, SSD_INNER), F32),
            pltpu.VMEM((SUBLANES, 512), F32),
            pltpu.VMEM((SSD_STATE, SSD_INNER), F32),
        ],
        compiler_params=_cparams("arbitrary", "arbitrary"),
        name="ssd_mixer",
    )(u, u, u, u, p["cwx"], p["cbx"], p["cwb"], p["cbb"], p["dtb"], p["alog"], p["dful"], p["ssd_nw"], p["expand"])


def _boundary_rows(b, lvl):
    half = 1 << lvl
    parts = []
    for v in range(b.shape[0] // SUBLANES):
        r0 = v * SUBLANES
        if 2 * half >= SUBLANES:
            src = (r0 // (2 * half)) * (2 * half) + half - 1
            parts.append(jnp.broadcast_to(b[src:src + 1, :], (SUBLANES, b.shape[1])))
        else:
            sub = lax.broadcasted_iota(jnp.int32, (SUBLANES, b.shape[1]), 0)
            piece = None
            for g in range(SUBLANES // (2 * half)):
                src = r0 + g * 2 * half + half - 1
                cand = jnp.broadcast_to(b[src:src + 1, :], (SUBLANES, b.shape[1]))
                piece = cand if piece is None else jnp.where(sub >= g * 2 * half, cand, piece)
            parts.append(piece)
    return jnp.concatenate(parts, axis=0)


def _hgrn2_lru_kernel(layer, q_ref, f_ref, v_ref, g_ref, lb_ref, nw_ref, msk_ref, half_ref, sgn_ref,
                      lg_ref, lx_ref, cw_ref, cb_ref, wa_ref, ba_ref, wx_ref, bx_ref, lam_ref, lnw_ref,
                      o_ref, ol_ref, state_t, ext, tail, hcarry):
    t = pl.program_id(1)
    ch = HG_CHUNK

    @pl.when(t == 0)
    def _():
        state_t[...] = jnp.zeros_like(state_t)
        tail[...] = jnp.zeros_like(tail)
        hcarry[...] = jnp.zeros_like(hcarry)

    _lru_tile(lg_ref, lx_ref, cw_ref, cb_ref, wa_ref, ba_ref, wx_ref, bx_ref, lam_ref, lnw_ref, ol_ref,
              ext, tail, hcarry)

    lrows = [lb_ref[j:j + 1, :] for j in range(lb_ref.shape[0])]
    mx = functools.reduce(jnp.maximum, lrows)
    es = [jnp.exp(r - mx) for r in lrows]
    den = functools.reduce(lambda a_, b_: a_ + b_, es)
    lb = jnp.zeros_like(mx)
    for j in range(1, layer + 1):
        lb = lb + es[j] / den
    one_minus_lb = 1.0 - lb
    nw = nw_ref[...]

    ri = lax.broadcasted_iota(jnp.int32, (ch, ch), 0)
    ci = lax.broadcasted_iota(jnp.int32, (ch, ch), 1)
    tril3 = jnp.concatenate([(ri >= ci).astype(BF16)] * 3, axis=1)
    rowi = lax.broadcasted_iota(jnp.int32, (ch, HG_EXPAND), 0)
    tgt = [((rowi >> lvl) & 1) == 1 for lvl in range(HG_LEVELS)]

    def head_chunk(h, rs):
        cs = slice(h * HG_EXPAND, (h + 1) * HG_EXPAND)
        qq = _silu(q_ref[rs, cs])
        kk = one_minus_lb[:, cs] * _sigmoid(-f_ref[rs, cs])
        logf = jnp.log1p(-kk)
        vv = v_ref[rs, cs]
        vb = vv.astype(BF16)
        b = jnp.dot(tril3, jnp.concatenate(_split3(logf), axis=0),
                    preferred_element_type=F32)
        st = state_t[h]
        o = lax.dot_general((qq * jnp.exp(b)).astype(BF16), st.astype(BF16), (((1,), (1,)), ((), ())),
                            preferred_element_type=F32)
        attn = jnp.zeros((ch, ch), F32)
        for lvl in range(HG_LEVELS):
            if lvl == 0:
                qe = jnp.where(tgt[0], qq * (1.0 - kk), 0.0)
                ke = jnp.where(tgt[0], 0.0, kk)
            else:
                m = _boundary_rows(b, lvl)
                later = half_ref[lvl]
                e = jnp.exp2((b - m) * sgn_ref[lvl])
                prod = jnp.where(tgt[lvl], qq, kk) * e
                qe = prod * later
                ke = prod - qe
            prod = lax.dot_general(qe.astype(BF16), ke.astype(BF16), (((1,), (1,)), ((), ())),
                                   preferred_element_type=F32)
            attn = attn + msk_ref[lvl] * prod
        diag = jnp.sum(qq * kk, axis=-1, keepdims=True)
        o = o + jnp.dot(attn.astype(BF16), vb, preferred_element_type=F32) + diag * vv
        b_last = b[ch - 1:ch, :]
        kd = (kk * jnp.exp(b_last - b)).astype(BF16)
        state_t[h] = st * jnp.exp(b_last) + jnp.dot(vv.T.astype(BF16), kd, preferred_element_type=F32)
        ms = jnp.mean(o * o, axis=-1, keepdims=True)
        y = o * lax.rsqrt(ms + EPS) * nw[:, cs]
        o_ref[rs, cs] = (y * _silu(g_ref[rs, cs])).astype(o_ref.dtype)

    for j in range(q_ref.shape[0] // ch):
        for h in range(HG_HEADS):
            head_chunk(h, slice(j * ch, (j + 1) * ch))


def _lru_tile(g_ref, x_ref, cw_ref, cb_ref, wa_ref, ba_ref, wx_ref, bx_ref, lam_ref, nw_ref, o_ref,
              ext, tail, hcarry):
    rows = x_ref.shape[0]
    xb = _causal_conv(x_ref, ext, tail, cw_ref, cb_ref)
    xbb = xb.astype(BF16)
    npair = LRU_WIDTH // LANES
    ra = jnp.concatenate([jnp.dot(xbb[:, j * LANES:(j + 1) * LANES], wa_ref[j], preferred_element_type=F32)
                          for j in range(npair)], axis=1)
    rx = jnp.concatenate([jnp.dot(xbb[:, j * LANES:(j + 1) * LANES], wx_ref[j], preferred_element_type=F32)
                          for j in range(npair)], axis=1)
    r = _sigmoid(ra + ba_ref[...])
    i = _sigmoid(rx + bx_ref[...])
    log_a = -LRU_C * r * _softplus(-lam_ref[...])
    a = jnp.exp(log_a)
    th = jnp.tanh(log_a)
    s = -2.0 * th
    root = s * lax.rsqrt(jnp.maximum(s * (1.0 - th), F32_TINY))
    u = root * (i * xb)

    ngroups = rows // SUBLANES
    sub = lax.broadcasted_iota(jnp.int32, (1, SUBLANES, LRU_WIDTH), 1)
    acc_a = a.reshape(ngroups, SUBLANES, LRU_WIDTH)
    acc_u = u.reshape(ngroups, SUBLANES, LRU_WIDTH)
    d = 1
    while d < SUBLANES:
        keep = sub >= d
        a_sh = jnp.where(keep, pltpu.roll(acc_a, d, 1), 1.0)
        u_sh = jnp.where(keep, pltpu.roll(acc_u, d, 1), 0.0)
        acc_u = acc_a * u_sh + acc_u
        acc_a = acc_a * a_sh
        d *= 2
    acc_a = acc_a.reshape(rows, LRU_WIDTH)
    acc_u = acc_u.reshape(rows, LRU_WIDTH)
    carry = hcarry[0:1, :]
    groups = []
    for g in range(rows // SUBLANES):
        gs = slice(g * SUBLANES, (g + 1) * SUBLANES)
        hg = acc_a[gs, :] * carry + acc_u[gs, :]
        groups.append(hg)
        carry = hg[SUBLANES - 1:SUBLANES, :]
    h = jnp.concatenate(groups, axis=0)
    hcarry[0:1, :] = carry

    gate = g_ref[...]
    gelu = 0.5 * gate * (1.0 + jnp.tanh(np.sqrt(2.0 / np.pi).astype(np.float32) * (gate + 0.044715 * (gate * gate * gate))))
    y = h * gelu
    ms = jnp.mean(y * y, axis=-1, keepdims=True)
    o_ref[...] = (y * lax.rsqrt(ms + EPS) * nw_ref[...]).astype(o_ref.dtype)


def _hgrn2_lru(u, p, layer, bsz, seq):
    r = R_HG
    nt = seq // r
    n = bsz * seq
    npair = LRU_WIDTH // LANES

    def col(base, width):
        return pl.BlockSpec((r, width), lambda b, t: (b * nt + t, base // width))

    def const(shape):
        return pl.BlockSpec(shape, lambda b, t: (0,) * len(shape))

    def out(width):
        return pl.BlockSpec((r, width), lambda b, t: (b * nt + t, 0))

    return pl.pallas_call(
        functools.partial(_hgrn2_lru_kernel, layer),
        grid=(bsz, nt),
        in_specs=[
            col(COL_HQ, HG_WIDTH), col(COL_HF, HG_WIDTH), col(COL_HV, HG_WIDTH), col(COL_HG, HG_WIDTH),
            const(p["hg_lb"].shape), const((1, HG_WIDTH)), const((HG_LEVELS, HG_CHUNK, HG_CHUNK)),
            const((HG_LEVELS, HG_CHUNK, HG_EXPAND)), const((HG_LEVELS, HG_CHUNK, HG_EXPAND)),
            col(COL_LG, LRU_WIDTH), col(COL_LX, LRU_WIDTH),
            const((CONV_WIDTH, LRU_WIDTH)), const((1, LRU_WIDTH)),
            const((npair, LANES, LANES)), const((1, LRU_WIDTH)),
            const((npair, LANES, LANES)), const((1, LRU_WIDTH)),
            const((1, LRU_WIDTH)), const((1, LRU_WIDTH)),
        ],
        out_specs=[out(HG_WIDTH), out(LRU_WIDTH)],
        out_shape=[jax.ShapeDtypeStruct((n, HG_WIDTH), BF16), jax.ShapeDtypeStruct((n, LRU_WIDTH), BF16)],
        scratch_shapes=[
            pltpu.VMEM((HG_HEADS, HG_EXPAND, HG_EXPAND), F32),
            pltpu.VMEM((r + SUBLANES, LRU_WIDTH), F32),
            pltpu.VMEM((SUBLANES, LRU_WIDTH), F32),
            pltpu.VMEM((SUBLANES, LRU_WIDTH), F32),
        ],
        compiler_params=_cparams("arbitrary", "arbitrary"),
        name="hgrn2_lru_mixer",
    )(u, u, u, u, p["hg_lb"], p["hg_nw"], p["hg_msk"], p["hg_later"], p["hg_sgn"],
      u, u, p["lru_cw"], p["lru_cb"], p["lru_wa"], p["lru_ba"], p["lru_wx"], p["lru_bx"], p["lru_lam"], p["lru_nw"])


def _outproj_kernel(ys_ref, yh_ref, yl_ref, x_ref, w_ref, g_ref, nw_ref, sh_ref, sc_ref, rw_ref, rb_ref,
                    xo_ref, h0, h1, h2, h3, eid_ref, rank_ref, wt_ref, cnt_ref, carry, wscr, hprev):
    i = pl.program_id(0)

    @pl.when(i == 0)
    def _():
        carry[...] = jnp.zeros_like(carry)
        hprev[...] = jnp.zeros_like(hprev)

    stages = _route_stages(hprev[...], jnp.minimum(i, 1).astype(F32), rw_ref, rb_ref, eid_ref, rank_ref, wt_ref,
                           cnt_ref, carry, wscr)
    tm = x_ref.shape[0]
    rows_per_part = tm // OUTPROJ_PARTS
    for part in range(OUTPROJ_PARTS):
        next(stages, None)
        rs = slice(part * rows_per_part, (part + 1) * rows_per_part)
        acc = jnp.dot(ys_ref[rs, :], w_ref[0:SSD_INNER, :], preferred_element_type=F32)
        acc = acc + jnp.dot(yh_ref[rs, :], w_ref[SSD_INNER:SSD_INNER + HG_WIDTH, :], preferred_element_type=F32)
        acc = acc + jnp.dot(yl_ref[rs, :], w_ref[SSD_INNER + HG_WIDTH:, :], preferred_element_type=F32)
        xn = x_ref[rs, :] + g_ref[...] * acc
        xo_ref[rs, :] = xn
        h = _norm_mod(xn, nw_ref[...], sh_ref[...], sc_ref[...])
        packed = _pack_bf16_pairs(h)
        for c, ref in enumerate((h0, h1, h2, h3)):
            ref[rs, :] = packed[:, c * LANES:(c + 1) * LANES]
        hprev[rs, :] = h.astype(BF16)
    for _ in stages:
        pass


def _outproj_route(y_ssd, y_hg, y_lru, x2, w_out, nw, mod3, rw_t, rb, seq):
    n, d = x2.shape
    tm = TM_OUTPROJ
    tpb = seq // tm
    last = n // tm - 1

    def cur(i):
        return jnp.minimum(i, last)

    def prev(i):
        return jnp.maximum(i - 1, 0)

    def modspec(j):
        return pl.BlockSpec((None, 1, d), lambda i: ((cur(i) // tpb) * N_ADA + j, 0, 0))

    outs = pl.pallas_call(
        _outproj_kernel,
        grid=(n // tm + 1,),
        in_specs=[
            pl.BlockSpec((tm, SSD_INNER), lambda i: (cur(i), 0)),
            pl.BlockSpec((tm, HG_WIDTH), lambda i: (cur(i), 0)),
            pl.BlockSpec((tm, LRU_WIDTH), lambda i: (cur(i), 0)),
            pl.BlockSpec((tm, d), lambda i: (cur(i), 0)),
            pl.BlockSpec(w_out.shape, lambda i: (0, 0)),
            modspec(2),
            pl.BlockSpec((1, d), lambda i: (0, 0)),
            modspec(3), modspec(4),
            pl.BlockSpec(rw_t.shape, lambda i: (0, 0)),
            pl.BlockSpec((N_EXPERTS, 1), lambda i: (0, 0)),
        ],
        out_specs=[pl.BlockSpec((tm, d), lambda i: (cur(i), 0))]
        + [pl.BlockSpec((tm, LANES), lambda i: (cur(i), 0))] * N_PLANES + [
            pl.BlockSpec((TOP_K, tm), lambda i: (0, prev(i))),
            pl.BlockSpec((TOP_K, tm), lambda i: (0, prev(i))),
            pl.BlockSpec((tm, LANES), lambda i: (prev(i), 0)),
            pl.BlockSpec((N_EXPERTS, LANES), lambda i: (0, 0)),
        ],
        out_shape=[jax.ShapeDtypeStruct((n, d), F32)] + [jax.ShapeDtypeStruct((n, LANES), jnp.uint32)] * N_PLANES + [
            jax.ShapeDtypeStruct((TOP_K, n), jnp.int32),
            jax.ShapeDtypeStruct((TOP_K, n), jnp.int32),
            jax.ShapeDtypeStruct((n, LANES), F32),
            jax.ShapeDtypeStruct((N_EXPERTS, LANES), F32),
        ],
        scratch_shapes=[pltpu.VMEM((N_EXPERTS, LANES), F32), pltpu.VMEM((LANES, tm), F32),
                        pltpu.VMEM((tm, d), BF16)],
        compiler_params=_cparams("arbitrary"),
        name="outproj_router",
    )(y_ssd, y_hg, y_lru, x2, w_out, mod3, nw, mod3, mod3, rw_t, rb)
    x_new, hp = outs[0], tuple(outs[1:1 + N_PLANES])
    eid, rank, wt, cnt = outs[1 + N_PLANES:]
    return x_new, hp, eid, rank, wt, cnt


def _plane_spec(tm):
    return pl.BlockSpec((tm, LANES), lambda i: (i, 0))


def _route_stages(hb, counted, rw_ref, rb_ref, eid_ref, rank_ref, wt_ref, cnt_ref, carry, wscr):
    tm = hb.shape[0]
    logit_t = sum(lax.dot_general(part, hb, (((1,), (1,)), ((), ())), preferred_element_type=F32)
                  for part in _split3(rw_ref[...]))
    score = _sigmoid(logit_t)
    sel = score + rb_ref[...]
    neg_inf = jnp.float32(-jnp.inf)
    io_g = lax.broadcasted_iota(jnp.int32, (E_PER_GROUP, tm), 0)
    blocks, gscore = [], []
    for g in range(N_EXPERT_GROUPS):
        blk = sel[g * E_PER_GROUP:(g + 1) * E_PER_GROUP, :]
        m1 = jnp.max(blk, axis=0, keepdims=True)
        i1 = jnp.min(jnp.where(blk == m1, io_g, E_PER_GROUP), axis=0, keepdims=True)
        m2 = jnp.max(jnp.where(io_g == i1, neg_inf, blk), axis=0, keepdims=True)
        blocks.append(blk)
        gscore.append(m1 + m2)
    yield
    masked = []
    for g in range(N_EXPERT_GROUPS):
        rank = jnp.zeros((1, tm), jnp.int32)
        for o in range(N_EXPERT_GROUPS):
            if o == g:
                continue
            beats = (gscore[o] > gscore[g]) | ((gscore[o] == gscore[g]) & (o < g))
            rank = rank + beats.astype(jnp.int32)
        masked.append(jnp.where(rank < TOPK_GROUPS, blocks[g], MASK_SCORE))
    val = jnp.concatenate(masked, axis=0)
    io_e = lax.broadcasted_iota(jnp.int32, (N_EXPERTS, tm), 0)
    chosen = jnp.zeros((N_EXPERTS, tm), jnp.bool_)
    picks = []
    for k in range(TOP_K):
        m = jnp.max(val, axis=0, keepdims=True)
        idx = jnp.min(jnp.where(val == m, io_e, N_EXPERTS), axis=0, keepdims=True)
        pick = io_e == idx
        picks.append(pick)
        eid_ref[k:k + 1, :] = idx
        chosen = chosen | pick
        val = jnp.where(pick, neg_inf, val)
        if k == TOP_K // 2 - 1:
            yield
    w = jnp.where(chosen, score, 0.0)
    w = w / jnp.sum(w, axis=0, keepdims=True) * ROUTED_SCALE
    yield

    chosen_f = chosen.astype(F32)
    earlier = (lax.broadcasted_iota(jnp.int32, (tm, tm), 0) < lax.broadcasted_iota(jnp.int32, (tm, tm), 1))
    before = jnp.dot(chosen_f.astype(BF16), earlier.astype(BF16), preferred_element_type=F32)
    grank = carry[:, 0:1] + before
    wscr[...] = jnp.zeros_like(wscr)
    for k in range(TOP_K):
        rank_ref[k:k + 1, :] = jnp.sum(jnp.where(picks[k], grank, 0.0), axis=0, keepdims=True).astype(jnp.int32)
        wscr[k:k + 1, :] = jnp.sum(jnp.where(picks[k], w, 0.0), axis=0, keepdims=True)
    wt_ref[...] = wscr[...].T
    carry[...] = carry[...] + counted * jnp.sum(chosen_f, axis=1, keepdims=True)
    cnt_ref[...] = carry[...]


def _plan_kernel(n_slots, cnt_ref, eid_ref, rank_ref, dest_ref, be_ref, nb_ref, pad_ref, np_ref):
    cnt = cnt_ref[...].astype(jnp.int32)
    padded = ((cnt + (MOE_BLOCK - 1)) >> MOE_BLOCK_LOG2) << MOE_BLOCK_LOG2
    ri = lax.broadcasted_iota(jnp.int32, (N_EXPERTS, N_EXPERTS), 0)
    ci = lax.broadcasted_iota(jnp.int32, (N_EXPERTS, N_EXPERTS), 1)
    pad_end = jnp.dot((ri >= ci).astype(F32), padded.astype(F32), precision=HI,
                      preferred_element_type=F32).astype(jnp.int32)
    pad_start = pad_end - padded
    eid = eid_ref[...]
    dest = rank_ref[...]
    for e in range(N_EXPERTS):
        dest = dest + jnp.where(eid == e, pad_start[e:e + 1, 0:1], 0)
    dest_ref[...] = dest
    nbp = be_ref.shape[1]
    jpos = lax.broadcasted_iota(jnp.int32, (N_EXPERTS, nbp), 1) * MOE_BLOCK
    be = jnp.minimum(jnp.sum((pad_end[:, 0:1] <= jpos).astype(jnp.int32), axis=0, keepdims=True), N_EXPERTS - 1)
    be_ref[...] = be
    nb_ref[...] = pad_end[N_EXPERTS - 1:N_EXPERTS, :] >> MOE_BLOCK_LOG2
    fill_end = pad_start + cnt
    io_e = lax.broadcasted_iota(jnp.int32, (N_EXPERTS, nbp), 0)
    end_j = jnp.sum(jnp.where(io_e == be, fill_end[:, 0:1], 0), axis=0, keepdims=True)
    pieces = (end_j - jpos[0:1, :] + (XS_PIECE - 1)) >> XS_PIECE_LOG2
    np_ref[...] = jnp.clip(pieces, 0, MOE_BLOCK // XS_PIECE)
    lane = lax.broadcasted_iota(jnp.int32, (N_EXPERTS, LANES), 1)
    piece_end = ((fill_end + (XS_PIECE - 1)) >> XS_PIECE_LOG2) << XS_PIECE_LOG2
    s = fill_end + lane
    pad_ref[...] = jnp.where(s < piece_end, s, n_slots + lane)


def _plan(cnt, eid, rank, n_slots):
    n = eid.shape[1]
    nbp = -(-(n_slots // MOE_BLOCK) // LANES) * LANES
    pad_rows = N_EXPERTS
    return pl.pallas_call(
        functools.partial(_plan_kernel, n_slots),
        grid=(1,),
        in_specs=[
            pl.BlockSpec(cnt.shape, lambda i: (0, 0)),
            pl.BlockSpec(eid.shape, lambda i: (0, 0)),
            pl.BlockSpec(rank.shape, lambda i: (0, 0)),
        ],
        out_specs=[
            pl.BlockSpec((TOP_K, n), lambda i: (0, 0)),
            pl.BlockSpec((1, nbp), lambda i: (0, 0)),
            pl.BlockSpec((1, LANES), lambda i: (0, 0)),
            pl.BlockSpec((pad_rows, LANES), lambda i: (0, 0)),
            pl.BlockSpec((1, nbp), lambda i: (0, 0)),
        ],
        out_shape=[
            jax.ShapeDtypeStruct((TOP_K, n), jnp.int32),
            jax.ShapeDtypeStruct((1, nbp), jnp.int32),
            jax.ShapeDtypeStruct((1, LANES), jnp.int32),
            jax.ShapeDtypeStruct((pad_rows, LANES), jnp.int32),
            jax.ShapeDtypeStruct((1, nbp), jnp.int32),
        ],
        compiler_params=_cparams("arbitrary"),
        name="moe_plan",
    )(cnt, eid, rank)


def _sc_mesh():
    return plsc.VectorSubcoreMesh(core_axis_name="c", subcore_axis_name="s")


def _sc_worker():
    return lax.axis_index("c") * SC_SUBCORES + lax.axis_index("s")


def _sc_dispatch(hp, dest_rows, pad_rows, n_rows):
    n = hp[0].shape[0]
    tiles_per_worker = n // SC_WIN // SC_WORKERS
    pad_per_worker = pad_rows.shape[0] // SC_WORKERS
    zeros = jnp.zeros((SC_WIN, LANES), jnp.uint32)

    def body(*refs):
        h = refs[:N_PLANES]
        dest_hbm, pad_hbm, z_hbm = refs[N_PLANES:N_PLANES + 3]
        xs = refs[N_PLANES + 3:2 * N_PLANES + 3]
        bufs = refs[2 * N_PLANES + 3:3 * N_PLANES + 3]
        ibuf, pbuf, sem = refs[3 * N_PLANES + 3:]
        wid = _sc_worker()

        pltpu.sync_copy(z_hbm, bufs[0])
        pltpu.sync_copy(pad_hbm.at[pl.ds(wid * pad_per_worker, pad_per_worker)], pbuf)
        copies = [pltpu.async_copy(bufs[0], xs[c].at[pbuf.at[r]], sem)
                  for r in range(pad_per_worker) for c in range(N_PLANES)]
        for cp in copies:
            cp.wait()

        @pl.loop(0, tiles_per_worker)
        def _(i):
            tile = wid * tiles_per_worker + i
            pltpu.sync_copy(dest_hbm.at[pl.ds(tile * TOP_K, TOP_K)], ibuf)
            for c in range(N_PLANES):
                pltpu.sync_copy(h[c].at[pl.ds(tile * SC_WIN, SC_WIN)], bufs[c])
            scatters = [pltpu.async_copy(bufs[c], xs[c].at[ibuf.at[k]], sem)
                        for c in range(N_PLANES) for k in range(TOP_K)]
            for cp in scatters:
                cp.wait()

    out_type = tuple(jax.ShapeDtypeStruct((n_rows, LANES), jnp.uint32) for _ in range(N_PLANES))
    scratch = ([pltpu.VMEM((SC_WIN, LANES), jnp.uint32)] * N_PLANES
               + [pltpu.VMEM((TOP_K, LANES), jnp.int32), pltpu.VMEM((pad_per_worker, LANES), jnp.int32),
                  pltpu.SemaphoreType.DMA])
    return pl.kernel(body, out_type=out_type, mesh=_sc_mesh(), scratch_types=scratch,
                     name="moe_sc_dispatch")(*hp, dest_rows, pad_rows, zeros)


def _sc_gather(ysp, dest_rows, n):
    tiles_per_worker = n // SC_WIN // SC_WORKERS

    def body(*refs):
        ys = refs[:N_PLANES]
        dest_hbm, g_hbm = refs[N_PLANES:N_PLANES + 2]
        bufs = refs[N_PLANES + 2:N_PLANES + 2 + SC_GATHER_BUFS]
        ibuf, sem = refs[N_PLANES + 2 + SC_GATHER_BUFS:]
        wid = _sc_worker()

        @pl.loop(0, tiles_per_worker)
        def _(i):
            tile = wid * tiles_per_worker + i
            pltpu.sync_copy(dest_hbm.at[pl.ds(tile * TOP_K, TOP_K)], ibuf)
            for c in range(N_PLANES):
                for k0 in range(0, TOP_K, SC_GATHER_BUFS):
                    gathers = [pltpu.async_copy(ys[c].at[ibuf.at[k0 + j]], bufs[j], sem)
                               for j in range(SC_GATHER_BUFS)]
                    for cp in gathers:
                        cp.wait()
                    stores = [pltpu.async_copy(
                        bufs[j], g_hbm.at[pl.ds(((k0 + j) * N_PLANES + c) * n + tile * SC_WIN, SC_WIN)], sem)
                        for j in range(SC_GATHER_BUFS)]
                    for cp in stores:
                        cp.wait()

    scratch = ([pltpu.VMEM((SC_WIN, LANES), jnp.uint32)] * SC_GATHER_BUFS
               + [pltpu.VMEM((TOP_K, LANES), jnp.int32), pltpu.SemaphoreType.DMA])
    return pl.kernel(body, out_type=jax.ShapeDtypeStruct((TOP_K * N_PLANES * n, LANES), jnp.uint32),
                     mesh=_sc_mesh(), scratch_types=scratch, name="moe_sc_gather")(*ysp, dest_rows)


def _expert_kernel(be_ref, nb_ref, np_ref, *refs):
    xs_hbm = refs[:N_PLANES]
    wg_ref, wu_ref, wd_ref = refs[N_PLANES:N_PLANES + 3]
    ys_refs = refs[N_PLANES + 3:2 * N_PLANES + 3]
    wg_b, wu_b, wd_b, ring, sems = refs[2 * N_PLANES + 3:]
    j = pl.program_id(0)
    nb = nb_ref[0]
    used = j < nb
    new_expert = (j == 0) | (be_ref[j] != be_ref[jnp.maximum(j - 1, 0)])

    def fetch(step, slot, wait):
        for p in range(MOE_BLOCK // XS_PIECE):
            @pl.when(p < np_ref[step])
            def _():
                src = pl.ds(pl.multiple_of(step * MOE_BLOCK + p * XS_PIECE, XS_PIECE), XS_PIECE)
                for c in range(N_PLANES):
                    cp = pltpu.make_async_copy(xs_hbm[c].at[src], ring.at[slot, c, pl.ds(p * XS_PIECE, XS_PIECE)],
                                               sems.at[slot])
                    if wait:
                        cp.wait()
                    else:
                        cp.start()

    @pl.when(j == 0)
    def _():
        ring[...] = jnp.zeros_like(ring)

    for s in range(XS_RING - 1):
        @pl.when((j == 0) & (s < nb))
        def _():
            fetch(s, s, wait=False)

    ahead = j + (XS_RING - 1)

    @pl.when(used & (ahead < nb))
    def _():
        fetch(ahead, lax.rem(ahead, XS_RING), wait=False)

    @pl.when(used & new_expert)
    def _():
        wg_b[...] = wg_ref[...].astype(BF16)
        wu_b[...] = wu_ref[...].astype(BF16)
        wd_b[...] = wd_ref[...].astype(BF16)

    @pl.when(used)
    def _():
        slot = lax.rem(j, XS_RING)
        fetch(j, slot, wait=True)
        x = _unpack_bf16_pairs(jnp.concatenate([ring[slot, c] for c in range(N_PLANES)], axis=1)).astype(BF16)
        a = jnp.dot(x, wg_b[...], preferred_element_type=F32)
        u = jnp.dot(x, wu_b[...], preferred_element_type=F32)
        y = jnp.dot((_silu(a) * u).astype(BF16), wd_b[...], preferred_element_type=F32)
        _store_planes(ys_refs, y)

    @pl.when(jnp.logical_not(used))
    def _():
        for ref in ys_refs:
            ref[...] = jnp.zeros_like(ref)


def _experts(be, nb, npieces, xsp, wg, wu, wd, layer, n_slots):
    d = wg.shape[2]

    def wspec(shape):
        def index(j, be_ref, nb_ref, np_ref):
            return layer, be_ref[jnp.minimum(j, jnp.maximum(nb_ref[0] - 1, 0))], 0, 0
        return pl.BlockSpec((None, None) + shape, index)

    grid_spec = pltpu.PrefetchScalarGridSpec(
        num_scalar_prefetch=3,
        grid=(n_slots // MOE_BLOCK,),
        in_specs=[pl.BlockSpec(memory_space=pl.ANY)] * N_PLANES
        + [wspec((d, D_EXPERT)), wspec((d, D_EXPERT)), wspec((D_EXPERT, d))],
        out_specs=[pl.BlockSpec((MOE_BLOCK, LANES), lambda j, be_ref, nb_ref, np_ref: (j, 0))] * N_PLANES,
        scratch_shapes=[pltpu.VMEM((d, D_EXPERT), BF16), pltpu.VMEM((d, D_EXPERT), BF16),
                        pltpu.VMEM((D_EXPERT, d), BF16),
                        pltpu.VMEM((XS_RING, N_PLANES, MOE_BLOCK, LANES), jnp.uint32),
                        pltpu.SemaphoreType.DMA((XS_RING,))],
    )
    return pl.pallas_call(
        _expert_kernel,
        grid_spec=grid_spec,
        out_shape=[jax.ShapeDtypeStruct((n_slots, LANES), jnp.uint32)] * N_PLANES,
        compiler_params=_cparams("arbitrary"),
        name="moe_experts",
    )(be, nb, npieces, *xsp, wg, wu, wd)


def _combine_kernel(final, g_ref, wt_ref, h0, h1, h2, h3, sg_ref, su_ref, sd_ref, x_ref, gate_ref, fw_ref, o_ref):
    hb = _load_planes((h0, h1, h2, h3)).astype(BF16)
    a = jnp.dot(hb, sg_ref[...], preferred_element_type=F32)
    u = jnp.dot(hb, su_ref[...], preferred_element_type=F32)
    acc = jnp.dot((_silu(a) * u).astype(BF16), sd_ref[...], preferred_element_type=F32)
    wt = wt_ref[...]
    for k in range(TOP_K):
        rows = _unpack_bf16_pairs(jnp.concatenate([g_ref[k * N_PLANES + c] for c in range(N_PLANES)], axis=1))
        acc = acc + wt[:, k:k + 1] * rows
    xn = x_ref[...] + gate_ref[...] * acc
    if final:
        ms = jnp.mean(xn * xn, axis=-1, keepdims=True)
        xn = xn * lax.rsqrt(ms + EPS) * fw_ref[...]
    o_ref[...] = xn


def _combine(g, wt, hp, sg, su, sd, x2, mod3, fw, seq, final):
    n, d = x2.shape
    tm = TM_COMBINE
    tpb = seq // tm
    return pl.pallas_call(
        functools.partial(_combine_kernel, final),
        grid=(n // tm,),
        in_specs=[
            pl.BlockSpec((TOP_K * N_PLANES, tm, LANES), lambda i: (0, i, 0)),
            pl.BlockSpec((tm, LANES), lambda i: (i, 0)),
        ] + [_plane_spec(tm)] * N_PLANES + [
            pl.BlockSpec(sg.shape, lambda i: (0, 0)),
            pl.BlockSpec(su.shape, lambda i: (0, 0)),
            pl.BlockSpec(sd.shape, lambda i: (0, 0)),
            pl.BlockSpec((tm, d), lambda i: (i, 0)),
            pl.BlockSpec((None, 1, d), lambda i: ((i // tpb) * N_ADA + 5, 0, 0)),
            pl.BlockSpec((1, d), lambda i: (0, 0)),
        ],
        out_specs=pl.BlockSpec((tm, d), lambda i: (i, 0)),
        out_shape=jax.ShapeDtypeStruct((n, d), F32),
        compiler_params=_cparams("arbitrary"),
        name="moe_combine",
    )(g.reshape(TOP_K * N_PLANES, n, LANES), wt, *hp, sg, su, sd, x2, mod3, fw)


def _blockdiag_pairs(w):
    z = jnp.zeros((LRU_BLOCK_W, LRU_BLOCK_W), w.dtype)
    tiles = []
    for j in range(LRU_BLOCKS // 2):
        top = jnp.concatenate([w[2 * j], z], axis=1)
        bot = jnp.concatenate([z, w[2 * j + 1]], axis=1)
        tiles.append(jnp.concatenate([top, bot], axis=0))
    return jnp.stack(tiles).astype(BF16)


def _hg_level_masks():
    ch = HG_CHUNK
    msk = np.zeros((HG_LEVELS, ch, ch), np.float32)
    for lvl in range(HG_LEVELS):
        half = 1 << lvl
        for t in range(ch):
            base = (t // (2 * half)) * (2 * half)
            if (t // half) % 2 == 1:
                msk[lvl, t, base:base + half] = 1.0
    later = ((np.arange(ch)[None, :, None] >> np.arange(HG_LEVELS)[:, None, None]) & 1).astype(np.float32)
    later = np.broadcast_to(later, (HG_LEVELS, ch, HG_EXPAND))
    sgn = (2.0 * later - 1.0) * np.float32(LOG2E)
    return jnp.asarray(msk), jnp.asarray(later), jnp.asarray(sgn, dtype=F32)


def _ssd_expand():
    e = np.zeros((LANES, SSD_INNER), np.float32)
    for h in range(SSD_HEADS):
        e[h, h * SSD_HEADDIM:(h + 1) * SSD_HEADDIM] = 1.0
    return jnp.asarray(np.concatenate([e] * 3, axis=0), dtype=BF16)


def _pad_lanes(v, width):
    return jnp.pad(v, (0, width - v.shape[0])).reshape(1, width)


def _layer_params(l, w_in, ssd_conv_w, ssd_conv_b, ssd_dt_bias, ssd_a_log, ssd_d, ssd_norm_w, hg_lower_bounds,
                  hg_norm_w, lru_conv_w, lru_conv_b, lru_wa, lru_ba, lru_wx, lru_bx, lru_lambda, lru_norm_w):
    wi = w_in[l]
    dt0 = SSD_INNER + SSD_INNER + 2 * SSD_GROUPS * SSD_STATE
    w_cat = jnp.concatenate([wi[:, :dt0], wi[:, dt0 + SSD_HEADS:], wi[:, dt0:dt0 + SSD_HEADS]], axis=1)
    w_cat = jnp.pad(w_cat, ((0, 0), (0, U_WIDTH - w_cat.shape[1]))).astype(BF16)
    msk, later, sgn = _hg_level_masks()
    return dict(
        w_cat=w_cat,
        cwx=ssd_conv_w[l][:, :SSD_INNER], cbx=ssd_conv_b[l][:SSD_INNER].reshape(1, -1),
        cwb=ssd_conv_w[l][:, SSD_INNER:], cbb=ssd_conv_b[l][SSD_INNER:].reshape(1, -1),
        dtb=_pad_lanes(ssd_dt_bias[l], LANES), alog=_pad_lanes(ssd_a_log[l], LANES),
        dful=jnp.repeat(ssd_d[l], SSD_HEADDIM).reshape(1, -1), ssd_nw=ssd_norm_w[l].reshape(1, -1),
        expand=_ssd_expand(),
        hg_lb=hg_lower_bounds, hg_nw=hg_norm_w[l].reshape(1, -1), hg_msk=msk, hg_later=later, hg_sgn=sgn,
        lru_cw=lru_conv_w[l], lru_cb=lru_conv_b[l].reshape(1, -1),
        lru_wa=_blockdiag_pairs(lru_wa[l]), lru_ba=lru_ba[l].reshape(1, -1),
        lru_wx=_blockdiag_pairs(lru_wx[l]), lru_bx=lru_bx[l].reshape(1, -1),
        lru_lam=lru_lambda[l].reshape(1, -1), lru_nw=lru_norm_w[l].reshape(1, -1),
    )


def kernel(x, c, ada_w, ada_b, norm_mix_w, norm_ffn_w, w_in, ssd_conv_w, ssd_conv_b, ssd_dt_bias, ssd_a_log, ssd_d, ssd_norm_w, hg_lower_bounds, hg_norm_w, lru_conv_w, lru_conv_b, lru_wa, lru_ba, lru_wx, lru_bx, lru_lambda, lru_norm_w, w_out, router_w, router_bias, exp_gate, exp_up, exp_down, sh_gate, sh_up, sh_down, final_norm_w):
    bsz, seq, d = x.shape
    depth = ada_w.shape[0]
    assert d == D_MODEL and seq % TM_OUTPROJ == 0 and seq % SSD_CHUNK == 0
    n = bsz * seq
    n_slots = n * TOP_K + N_EXPERTS * MOE_BLOCK
    x2 = x.reshape(n, d)
    mod = _adaln(c, ada_w, ada_b)
    fw = final_norm_w.reshape(1, d)
    for l in range(depth):
        p = _layer_params(l, w_in, ssd_conv_w, ssd_conv_b, ssd_dt_bias, ssd_a_log, ssd_d, ssd_norm_w,
                          hg_lower_bounds, hg_norm_w, lru_conv_w, lru_conv_b, lru_wa, lru_ba, lru_wx, lru_bx,
                          lru_lambda, lru_norm_w)
        mod3 = mod[l].reshape(bsz * N_ADA, 1, d)
        u = _inproj(x2, norm_mix_w[l].reshape(1, d), mod3, p["w_cat"], seq)
        y_ssd = _ssd(u, p, bsz, seq)
        y_hg, y_lru = _hgrn2_lru(u, p, l, bsz, seq)
        x2, hp, eid, rank, wt, cnt = _outproj_route(
            y_ssd, y_hg, y_lru, x2, w_out[l].astype(BF16), norm_ffn_w[l].reshape(1, d), mod3,
            router_w[l].T, router_bias[l].reshape(N_EXPERTS, 1), seq)
        dest, be, nb, pad_rows, npieces = _plan(cnt, eid, rank, n_slots)
        dest_rows = dest.reshape(TOP_K, n // LANES, LANES).transpose(1, 0, 2).reshape(n // LANES * TOP_K, LANES)
        xsp = _sc_dispatch(hp, dest_rows, pad_rows, n_slots + LANES)
        ysp = _experts(be.reshape(-1), nb[0, :1], npieces.reshape(-1), xsp, exp_gate, exp_up, exp_down, l, n_slots)
        g = _sc_gather(ysp, dest_rows, n)
        x2 = _combine(g, wt, hp, sh_gate[l].astype(BF16), sh_up[l].astype(BF16), sh_down[l].astype(BF16),
                      x2, mod3, fw, seq, final=(l == depth - 1))
    return x2.reshape(bsz, seq, d)
```

```python
import functools

import jax
import jax.numpy as jnp
import numpy as np
from jax import lax
from jax.experimental import pallas as pl
from jax.experimental.pallas import tpu as pltpu
from jax.experimental.pallas import tpu_sc as plsc

F32 = jnp.float32
BF16 = jnp.bfloat16
HI = lax.Precision.HIGHEST
F32_TINY = float(np.finfo(np.float32).tiny)
LOG2E = float(np.log2(np.e))

LANES = 128
SUBLANES = 8
VMEM_LIMIT_BYTES = 56 * 1024 * 1024

D_MODEL = 1024
EPS = 1e-6
N_ADA = 6
CONV_WIDTH = 4
SSD_INNER = 1024
SSD_HEADDIM = 64
SSD_HEADS = 16
SSD_GROUPS = 2
SSD_STATE = 128
SSD_CHUNK = 128
SSD_GROUP_W = SSD_INNER // SSD_GROUPS
HG_WIDTH = 512
HG_EXPAND = 128
HG_HEADS = 4
HG_CHUNK = 128
HG_LEVELS = 7
LRU_WIDTH = 512
LRU_BLOCKS = 8
LRU_BLOCK_W = 64
LRU_C = 8.0
N_EXPERTS = 64
TOP_K = 8
N_EXPERT_GROUPS = 8
E_PER_GROUP = 8
TOPK_GROUPS = 4
D_EXPERT = 256
ROUTED_SCALE = 2.5
MASK_SCORE = -1.0e4

COL_Z = 0
COL_XS = 1024
COL_BC = 2048
COL_HQ = 2560
COL_HF = 3072
COL_HV = 3584
COL_HG = 4096
COL_LG = 4608
COL_LX = 5120
COL_DT = 5632
U_WIDTH = 5760

TM_INPROJ = 512
TM_OUTPROJ = 1024
TM_ROUTER = 512
TM_COMBINE = 512
SC_CORES = 2
SC_SUBCORES = 16
SC_WORKERS = SC_CORES * SC_SUBCORES
SC_WIN = LANES
SC_GATHER_BUFS = 4
MOE_BLOCK_LOG2 = 10
MOE_BLOCK = 1 << MOE_BLOCK_LOG2
XS_RING = 3
XS_PIECE_LOG2 = 7
XS_PIECE = 1 << XS_PIECE_LOG2
R_HG = 512


def _cparams(*sem):
    return pltpu.CompilerParams(dimension_semantics=sem, vmem_limit_bytes=VMEM_LIMIT_BYTES)


def _sigmoid(x):
    return 0.5 * jnp.tanh(0.5 * x) + 0.5


def _silu(x):
    h = 0.5 * x
    return h + h * jnp.tanh(h)


def _split3(x):
    x1 = x.astype(BF16)
    r1 = x - x1.astype(F32)
    x2 = r1.astype(BF16)
    x3 = (r1 - x2.astype(F32)).astype(BF16)
    return x1, x2, x3


def _softplus(x):
    return jnp.maximum(x, 0.0) + jnp.log1p(jnp.exp(-jnp.abs(x)))


def _norm_mod(x, nw, shift, scale):
    ms = jnp.mean(x * x, axis=-1, keepdims=True)
    y = x * lax.rsqrt(ms + EPS) * nw
    return y * (1.0 + scale) + shift


_HI16 = np.uint32(0xFFFF0000)


def _pack_bf16_pairs(x):
    half = x.shape[1] // 2
    bits = lax.bitcast_convert_type(x.astype(BF16).astype(F32), jnp.uint32)
    return (bits[:, :half] & _HI16) | (bits[:, half:] >> 16)


def _unpack_bf16_pairs(w):
    hi = lax.bitcast_convert_type(w & _HI16, F32)
    lo = lax.bitcast_convert_type(w << 16, F32)
    return jnp.concatenate([hi, lo], axis=1)


N_PLANES = D_MODEL // 2 // LANES


def _store_planes(refs, x):
    packed = _pack_bf16_pairs(x)
    for c, ref in enumerate(refs):
        ref[...] = packed[:, c * LANES:(c + 1) * LANES]


def _load_planes(refs):
    return _unpack_bf16_pairs(jnp.concatenate([ref[...] for ref in refs], axis=1))


def _ada_kernel(c_ref, w_ref, b_ref, o_ref):
    c = c_ref[...]
    o_ref[...] = jnp.dot(_silu(c), w_ref[...], precision=HI, preferred_element_type=F32) + b_ref[...]


def _adaln(c, ada_w, ada_b):
    depth, d, n6 = ada_w.shape
    bsz = c.shape[0]
    tn = 1536
    return pl.pallas_call(
        _ada_kernel,
        grid=(depth, n6 // tn),
        in_specs=[
            pl.BlockSpec((bsz, d), lambda l, j: (0, 0)),
            pl.BlockSpec((None, d, tn), lambda l, j: (l, 0, j)),
            pl.BlockSpec((None, 1, tn), lambda l, j: (l, 0, j)),
        ],
        out_specs=pl.BlockSpec((None, bsz, tn), lambda l, j: (l, 0, j)),
        out_shape=jax.ShapeDtypeStruct((depth, bsz, n6), F32),
        compiler_params=_cparams("arbitrary", "arbitrary"),
        name="adaln_mod",
    )(c, ada_w, ada_b.reshape(depth, 1, n6))


def _inproj_kernel(x_ref, nw_ref, sh_ref, sc_ref, w_ref, o_ref):
    h = _norm_mod(x_ref[...], nw_ref[...], sh_ref[...], sc_ref[...])
    o_ref[...] = jnp.dot(h.astype(BF16), w_ref[...], preferred_element_type=F32)


def _inproj(x2, nw, mod3, w_cat, seq):
    n, d = x2.shape
    tm = TM_INPROJ
    tpb = seq // tm
    return pl.pallas_call(
        _inproj_kernel,
        grid=(n // tm,),
        in_specs=[
            pl.BlockSpec((tm, d), lambda i: (i, 0)),
            pl.BlockSpec((1, d), lambda i: (0, 0)),
            pl.BlockSpec((None, 1, d), lambda i: ((i // tpb) * N_ADA + 0, 0, 0)),
            pl.BlockSpec((None, 1, d), lambda i: ((i // tpb) * N_ADA + 1, 0, 0)),
            pl.BlockSpec((d, U_WIDTH), lambda i: (0, 0), pipeline_mode=pl.Buffered(1)),
        ],
        out_specs=pl.BlockSpec((tm, U_WIDTH), lambda i: (i, 0)),
        out_shape=jax.ShapeDtypeStruct((n, U_WIDTH), F32),
        compiler_params=_cparams("arbitrary"),
        name="inproj",
    )(x2, nw, mod3, mod3, w_cat)


def _causal_conv(cur_ref, ext, tail, cw_ref, cb_ref):
    rows, width = cur_ref.shape
    cur = cur_ref[...]
    ext[0:SUBLANES, :] = tail[...]
    ext[SUBLANES:SUBLANES + rows, :] = cur
    tail[...] = cur[rows - SUBLANES:rows, :]
    groups = ext[...].reshape(rows // SUBLANES + 1, SUBLANES, width)
    sub = lax.broadcasted_iota(jnp.int32, (1, SUBLANES, width), 1)
    acc = cb_ref[...] + cur * cw_ref[CONV_WIDTH - 1:CONV_WIDTH, :]
    for d in range(1, CONV_WIDTH):
        rot = pltpu.roll(groups, d, 1)
        back = jnp.where(sub < d, rot[:-1], rot[1:]).reshape(rows, width)
        acc = acc + back * cw_ref[CONV_WIDTH - 1 - d:CONV_WIDTH - d, :]
    return acc


def _ssd_kernel(z_ref, xs_ref, bc_ref, dt_ref, cwx_ref, cbx_ref, cwb_ref, cbb_ref, dtb_ref, alog_ref,
                dful_ref, nw_ref, e_ref, o_ref, extx, extb, tailx, tailb, hstate):
    c = pl.program_id(1)
    q = SSD_CHUNK

    @pl.when(c == 0)
    def _():
        tailx[...] = jnp.zeros_like(tailx)
        tailb[...] = jnp.zeros_like(tailb)
        hstate[...] = jnp.zeros_like(hstate)

    xs = _silu(_causal_conv(xs_ref, extx, tailx, cwx_ref, cbx_ref))
    bc = _silu(_causal_conv(bc_ref, extb, tailb, cwb_ref, cbb_ref))

    dt = _softplus(dt_ref[...] + dtb_ref[...])
    a = dt * (-jnp.exp(alog_ref[...]))
    ri = lax.broadcasted_iota(jnp.int32, (q, q), 0)
    ci = lax.broadcasted_iota(jnp.int32, (q, q), 1)
    tril = ri >= ci
    tril_f = tril.astype(F32)
    tril_b = tril.astype(BF16)
    acum = jnp.dot(jnp.concatenate([tril_b] * 3, axis=1), jnp.concatenate(_split3(a), axis=0),
                   preferred_element_type=F32)
    acum_t = acum.T
    expand3 = e_ref[...]
    dt_full = jnp.dot(jnp.concatenate(_split3(dt), axis=1), expand3, preferred_element_type=F32)
    acum_full = jnp.dot(jnp.concatenate(_split3(acum), axis=1), expand3, preferred_element_type=F32)
    alast_full = acum_full[q - 1:q, :]

    xdt = xs * dt_full
    xdt_b = xdt.astype(BF16)
    exp_a = jnp.exp(acum_full)
    xd_b = (xdt * jnp.exp(alast_full - acum_full)).astype(BF16)
    state_decay = jnp.exp(alast_full)
    left = lax.broadcasted_iota(jnp.int32, (q, LANES), 1) < SSD_HEADDIM
    zero_b = jnp.zeros((q, LANES), BF16)

    ys = []
    for g in range(SSD_GROUPS):
        b_g = bc[:, g * SSD_STATE:(g + 1) * SSD_STATE]
        c_g = bc[:, (SSD_GROUPS + g) * SSD_STATE:(SSD_GROUPS + g + 1) * SSD_STATE]
        c_b = c_g.astype(BF16)
        cb = lax.dot_general(c_b, b_g.astype(BF16), (((1,), (1,)), ((), ())), preferred_element_type=F32)
        cb = cb * tril_f
        cs = slice(g * SSD_GROUP_W, (g + 1) * SSD_GROUP_W)
        h_g = hstate[:, cs]
        y_off = jnp.dot(c_b, h_g.astype(BF16), preferred_element_type=F32) * exp_a[:, cs]
        pieces = []
        for pr in range(SSD_HEADS // SSD_GROUPS // 2):
            h0 = g * (SSD_HEADS // SSD_GROUPS) + 2 * pr
            ms = []
            for h in (h0, h0 + 1):
                col = acum[:, h:h + 1]
                row = acum_t[h:h + 1, :]
                ms.append((cb * jnp.exp(jnp.minimum(col - row, 0.0))).astype(BF16))
            lhs = jnp.concatenate(ms, axis=1)
            xp = xdt_b[:, h0 * SSD_HEADDIM:(h0 + 2) * SSD_HEADDIM]
            rhs = jnp.concatenate([jnp.where(left, xp, zero_b), jnp.where(left, zero_b, xp)], axis=0)
            pieces.append(jnp.dot(lhs, rhs, preferred_element_type=F32))
        ys.append(jnp.concatenate(pieces, axis=1) + y_off)
        b_t = b_g.T.astype(BF16)
        hstate[:, cs] = h_g * state_decay[:, cs] + jnp.dot(b_t, xd_b[:, cs], preferred_element_type=F32)

    y = jnp.concatenate(ys, axis=1) + xs * dful_ref[...]
    y = y * _silu(z_ref[...])
    outs = []
    for g in range(SSD_GROUPS):
        cs = slice(g * SSD_GROUP_W, (g + 1) * SSD_GROUP_W)
        yg = y[:, cs]
        ms = jnp.mean(yg * yg, axis=-1, keepdims=True)
        outs.append(yg * lax.rsqrt(ms + EPS) * nw_ref[:, cs])
    o_ref[...] = jnp.concatenate(outs, axis=1).astype(o_ref.dtype)


def _ssd(u, p, bsz, seq):
    q = SSD_CHUNK
    nc = seq // q
    n = bsz * seq

    def rows(b, c):
        return b * nc + c

    def const(shape):
        return pl.BlockSpec(shape, lambda b, c: (0,) * len(shape))

    return pl.pallas_call(
        _ssd_kernel,
        grid=(bsz, nc),
        in_specs=[
            pl.BlockSpec((q, SSD_INNER), lambda b, c: (rows(b, c), COL_Z // SSD_INNER)),
            pl.BlockSpec((q, SSD_INNER), lambda b, c: (rows(b, c), COL_XS // SSD_INNER)),
            pl.BlockSpec((q, 512), lambda b, c: (rows(b, c), COL_BC // 512)),
            pl.BlockSpec((q, LANES), lambda b, c: (rows(b, c), COL_DT // LANES)),
            const((CONV_WIDTH, SSD_INNER)), const((1, SSD_INNER)),
            const((CONV_WIDTH, 512)), const((1, 512)),
            const((1, LANES)), const((1, LANES)),
            const((1, SSD_INNER)), const((1, SSD_INNER)),
            const((3 * LANES, SSD_INNER)),
        ],
        out_specs=pl.BlockSpec((q, SSD_INNER), lambda b, c: (rows(b, c), 0)),
        out_shape=jax.ShapeDtypeStruct((n, SSD_INNER), BF16),
        scratch_shapes=[
            pltpu.VMEM((q + SUBLANES, SSD_INNER), F32),
            pltpu.VMEM((q + SUBLANES, 512), F32),
            pltpu.VMEM((SUBLANES, SSD_INNER), F32),
            pltpu.VMEM((SUBLANES, 512), F32),
            pltpu.VMEM((SSD_STATE, SSD_INNER), F32),
        ],
        compiler_params=_cparams("arbitrary", "arbitrary"),
        name="ssd_mixer",
    )(u, u, u, u, p["cwx"], p["cbx"], p["cwb"], p["cbb"], p["dtb"], p["alog"], p["dful"], p["ssd_nw"], p["expand"])


def _boundary_rows(b, lvl):
    half = 1 << lvl
    parts = []
    for v in range(b.shape[0] // SUBLANES):
        r0 = v * SUBLANES
        if 2 * half >= SUBLANES:
            src = (r0 // (2 * half)) * (2 * half) + half - 1
            parts.append(jnp.broadcast_to(b[src:src + 1, :], (SUBLANES, b.shape[1])))
        else:
            sub = lax.broadcasted_iota(jnp.int32, (SUBLANES, b.shape[1]), 0)
            piece = None
            for g in range(SUBLANES // (2 * half)):
                src = r0 + g * 2 * half + half - 1
                cand = jnp.broadcast_to(b[src:src + 1, :], (SUBLANES, b.shape[1]))
                piece = cand if piece is None else jnp.where(sub >= g * 2 * half, cand, piece)
            parts.append(piece)
    return jnp.concatenate(parts, axis=0)


def _hgrn2_lru_kernel(layer, q_ref, f_ref, v_ref, g_ref, lb_ref, nw_ref, msk_ref, half_ref, sgn_ref,
                      lg_ref, lx_ref, cw_ref, cb_ref, wa_ref, ba_ref, wx_ref, bx_ref, lam_ref, lnw_ref,
                      o_ref, ol_ref, state_t, ext, tail, hcarry):
    t = pl.program_id(1)
    ch = HG_CHUNK

    @pl.when(t == 0)
    def _():
        state_t[...] = jnp.zeros_like(state_t)
        tail[...] = jnp.zeros_like(tail)
        hcarry[...] = jnp.zeros_like(hcarry)

    _lru_tile(lg_ref, lx_ref, cw_ref, cb_ref, wa_ref, ba_ref, wx_ref, bx_ref, lam_ref, lnw_ref, ol_ref,
              ext, tail, hcarry)

    lrows = [lb_ref[j:j + 1, :] for j in range(lb_ref.shape[0])]
    mx = functools.reduce(jnp.maximum, lrows)
    es = [jnp.exp(r - mx) for r in lrows]
    den = functools.reduce(lambda a_, b_: a_ + b_, es)
    lb = jnp.zeros_like(mx)
    for j in range(1, layer + 1):
        lb = lb + es[j] / den
    one_minus_lb = 1.0 - lb
    nw = nw_ref[...]

    ri = lax.broadcasted_iota(jnp.int32, (ch, ch), 0)
    ci = lax.broadcasted_iota(jnp.int32, (ch, ch), 1)
    tril3 = jnp.concatenate([(ri >= ci).astype(BF16)] * 3, axis=1)
    rowi = lax.broadcasted_iota(jnp.int32, (ch, HG_EXPAND), 0)
    tgt = [((rowi >> lvl) & 1) == 1 for lvl in range(HG_LEVELS)]

    def head_chunk(h, rs):
        cs = slice(h * HG_EXPAND, (h + 1) * HG_EXPAND)
        qq = _silu(q_ref[rs, cs])
        kk = one_minus_lb[:, cs] * _sigmoid(-f_ref[rs, cs])
        logf = jnp.log1p(-kk)
        vv = v_ref[rs, cs]
        vb = vv.astype(BF16)
        b = jnp.dot(tril3, jnp.concatenate(_split3(logf), axis=0),
                    preferred_element_type=F32)
        st = state_t[h]
        o = lax.dot_general((qq * jnp.exp(b)).astype(BF16), st.astype(BF16), (((1,), (1,)), ((), ())),
                            preferred_element_type=F32)
        attn = jnp.zeros((ch, ch), F32)
        for lvl in range(HG_LEVELS):
            if lvl == 0:
                qe = jnp.where(tgt[0], qq * (1.0 - kk), 0.0)
                ke = jnp.where(tgt[0], 0.0, kk)
            else:
                m = _boundary_rows(b, lvl)
                later = half_ref[lvl]
                e = jnp.exp2((b - m) * sgn_ref[lvl])
                prod = jnp.where(tgt[lvl], qq, kk) * e
                qe = prod * later
                ke = prod - qe
            prod = lax.dot_general(qe.astype(BF16), ke.astype(BF16), (((1,), (1,)), ((), ())),
                                   preferred_element_type=F32)
            attn = attn + msk_ref[lvl] * prod
        diag = jnp.sum(qq * kk, axis=-1, keepdims=True)
        o = o + jnp.dot(attn.astype(BF16), vb, preferred_element_type=F32) + diag * vv
        b_last = b[ch - 1:ch, :]
        kd = (kk * jnp.exp(b_last - b)).astype(BF16)
        state_t[h] = st * jnp.exp(b_last) + jnp.dot(vv.T.astype(BF16), kd, preferred_element_type=F32)
        ms = jnp.mean(o * o, axis=-1, keepdims=True)
        y = o * lax.rsqrt(ms + EPS) * nw[:, cs]
        o_ref[rs, cs] = (y * _silu(g_ref[rs, cs])).astype(o_ref.dtype)

    for j in range(q_ref.shape[0] // ch):
        for h in range(HG_HEADS):
            head_chunk(h, slice(j * ch, (j + 1) * ch))


def _lru_tile(g_ref, x_ref, cw_ref, cb_ref, wa_ref, ba_ref, wx_ref, bx_ref, lam_ref, nw_ref, o_ref,
              ext, tail, hcarry):
    rows = x_ref.shape[0]
    xb = _causal_conv(x_ref, ext, tail, cw_ref, cb_ref)
    xbb = xb.astype(BF16)
    npair = LRU_WIDTH // LANES
    ra = jnp.concatenate([jnp.dot(xbb[:, j * LANES:(j + 1) * LANES], wa_ref[j], preferred_element_type=F32)
                          for j in range(npair)], axis=1)
    rx = jnp.concatenate([jnp.dot(xbb[:, j * LANES:(j + 1) * LANES], wx_ref[j], preferred_element_type=F32)
                          for j in range(npair)], axis=1)
    r = _sigmoid(ra + ba_ref[...])
    i = _sigmoid(rx + bx_ref[...])
    log_a = -LRU_C * r * _softplus(-lam_ref[...])
    a = jnp.exp(log_a)
    th = jnp.tanh(log_a)
    s = -2.0 * th
    root = s * lax.rsqrt(jnp.maximum(s * (1.0 - th), F32_TINY))
    u = root * (i * xb)

    ngroups = rows // SUBLANES
    sub = lax.broadcasted_iota(jnp.int32, (1, SUBLANES, LRU_WIDTH), 1)
    acc_a = a.reshape(ngroups, SUBLANES, LRU_WIDTH)
    acc_u = u.reshape(ngroups, SUBLANES, LRU_WIDTH)
    d = 1
    while d < SUBLANES:
        keep = sub >= d
        a_sh = jnp.where(keep, pltpu.roll(acc_a, d, 1), 1.0)
        u_sh = jnp.where(keep, pltpu.roll(acc_u, d, 1), 0.0)
        acc_u = acc_a * u_sh + acc_u
        acc_a = acc_a * a_sh
        d *= 2
    acc_a = acc_a.reshape(rows, LRU_WIDTH)
    acc_u = acc_u.reshape(rows, LRU_WIDTH)
    carry = hcarry[0:1, :]
    groups = []
    for g in range(rows // SUBLANES):
        gs = slice(g * SUBLANES, (g + 1) * SUBLANES)
        hg = acc_a[gs, :] * carry + acc_u[gs, :]
        groups.append(hg)
        carry = hg[SUBLANES - 1:SUBLANES, :]
    h = jnp.concatenate(groups, axis=0)
    hcarry[0:1, :] = carry

    gate = g_ref[...]
    gelu = 0.5 * gate * (1.0 + jnp.tanh(np.sqrt(2.0 / np.pi).astype(np.float32) * (gate + 0.044715 * (gate * gate * gate))))
    y = h * gelu
    ms = jnp.mean(y * y, axis=-1, keepdims=True)
    o_ref[...] = (y * lax.rsqrt(ms + EPS) * nw_ref[...]).astype(o_ref.dtype)


def _hgrn2_lru(u, p, layer, bsz, seq):
    r = R_HG
    nt = seq // r
    n = bsz * seq
    npair = LRU_WIDTH // LANES

    def col(base, width):
        return pl.BlockSpec((r, width), lambda b, t: (b * nt + t, base // width))

    def const(shape):
        return pl.BlockSpec(shape, lambda b, t: (0,) * len(shape))

    def out(width):
        return pl.BlockSpec((r, width), lambda b, t: (b * nt + t, 0))

    return pl.pallas_call(
        functools.partial(_hgrn2_lru_kernel, layer),
        grid=(bsz, nt),
        in_specs=[
            col(COL_HQ, HG_WIDTH), col(COL_HF, HG_WIDTH), col(COL_HV, HG_WIDTH), col(COL_HG, HG_WIDTH),
            const(p["hg_lb"].shape), const((1, HG_WIDTH)), const((HG_LEVELS, HG_CHUNK, HG_CHUNK)),
            const((HG_LEVELS, HG_CHUNK, HG_EXPAND)), const((HG_LEVELS, HG_CHUNK, HG_EXPAND)),
            col(COL_LG, LRU_WIDTH), col(COL_LX, LRU_WIDTH),
            const((CONV_WIDTH, LRU_WIDTH)), const((1, LRU_WIDTH)),
            const((npair, LANES, LANES)), const((1, LRU_WIDTH)),
            const((npair, LANES, LANES)), const((1, LRU_WIDTH)),
            const((1, LRU_WIDTH)), const((1, LRU_WIDTH)),
        ],
        out_specs=[out(HG_WIDTH), out(LRU_WIDTH)],
        out_shape=[jax.ShapeDtypeStruct((n, HG_WIDTH), BF16), jax.ShapeDtypeStruct((n, LRU_WIDTH), BF16)],
        scratch_shapes=[
            pltpu.VMEM((HG_HEADS, HG_EXPAND, HG_EXPAND), F32),
            pltpu.VMEM((r + SUBLANES, LRU_WIDTH), F32),
            pltpu.VMEM((SUBLANES, LRU_WIDTH), F32),
            pltpu.VMEM((SUBLANES, LRU_WIDTH), F32),
        ],
        compiler_params=_cparams("arbitrary", "arbitrary"),
        name="hgrn2_lru_mixer",
    )(u, u, u, u, p["hg_lb"], p["hg_nw"], p["hg_msk"], p["hg_later"], p["hg_sgn"],
      u, u, p["lru_cw"], p["lru_cb"], p["lru_wa"], p["lru_ba"], p["lru_wx"], p["lru_bx"], p["lru_lam"], p["lru_nw"])


def _outproj_kernel(ys_ref, yh_ref, yl_ref, x_ref, w_ref, g_ref, nw_ref, sh_ref, sc_ref, xo_ref, *h_refs):
    acc = jnp.dot(ys_ref[...], w_ref[0:SSD_INNER, :], preferred_element_type=F32)
    acc = acc + jnp.dot(yh_ref[...], w_ref[SSD_INNER:SSD_INNER + HG_WIDTH, :], preferred_element_type=F32)
    acc = acc + jnp.dot(yl_ref[...], w_ref[SSD_INNER + HG_WIDTH:, :], preferred_element_type=F32)
    xn = x_ref[...] + g_ref[...] * acc
    xo_ref[...] = xn
    _store_planes(h_refs, _norm_mod(xn, nw_ref[...], sh_ref[...], sc_ref[...]))


def _outproj(y_ssd, y_hg, y_lru, x2, w_out, nw, mod3, seq):
    n, d = x2.shape
    tm = TM_OUTPROJ
    tpb = seq // tm

    def modspec(j):
        return pl.BlockSpec((None, 1, d), lambda i: ((i // tpb) * N_ADA + j, 0, 0))

    outs = pl.pallas_call(
        _outproj_kernel,
        grid=(n // tm,),
        in_specs=[
            pl.BlockSpec((tm, SSD_INNER), lambda i: (i, 0)),
            pl.BlockSpec((tm, HG_WIDTH), lambda i: (i, 0)),
            pl.BlockSpec((tm, LRU_WIDTH), lambda i: (i, 0)),
            pl.BlockSpec((tm, d), lambda i: (i, 0)),
            pl.BlockSpec(w_out.shape, lambda i: (0, 0)),
            modspec(2),
            pl.BlockSpec((1, d), lambda i: (0, 0)),
            modspec(3), modspec(4),
        ],
        out_specs=[pl.BlockSpec((tm, d), lambda i: (i, 0))] + [_plane_spec(tm)] * N_PLANES,
        out_shape=[jax.ShapeDtypeStruct((n, d), F32)] + [jax.ShapeDtypeStruct((n, LANES), jnp.uint32)] * N_PLANES,
        compiler_params=_cparams("arbitrary"),
        name="outproj",
    )(y_ssd, y_hg, y_lru, x2, w_out, mod3, nw, mod3, mod3)
    return outs[0], tuple(outs[1:])


def _plane_spec(tm):
    return pl.BlockSpec((tm, LANES), lambda i: (i, 0))


def _router_kernel(h0, h1, h2, h3, rw_ref, rb_ref, eid_ref, rank_ref, wt_ref, cnt_ref, carry, wscr):
    tm = h0.shape[0]

    @pl.when(pl.program_id(0) == 0)
    def _():
        carry[...] = jnp.zeros_like(carry)

    hb = _load_planes((h0, h1, h2, h3)).astype(BF16)
    logit_t = sum(lax.dot_general(part, hb, (((1,), (1,)), ((), ())), preferred_element_type=F32)
                  for part in _split3(rw_ref[...]))
    score = _sigmoid(logit_t)
    sel = score + rb_ref[...]
    neg_inf = jnp.float32(-jnp.inf)
    io_g = lax.broadcasted_iota(jnp.int32, (E_PER_GROUP, tm), 0)
    blocks, gscore = [], []
    for g in range(N_EXPERT_GROUPS):
        blk = sel[g * E_PER_GROUP:(g + 1) * E_PER_GROUP, :]
        m1 = jnp.max(blk, axis=0, keepdims=True)
        i1 = jnp.min(jnp.where(blk == m1, io_g, E_PER_GROUP), axis=0, keepdims=True)
        m2 = jnp.max(jnp.where(io_g == i1, neg_inf, blk), axis=0, keepdims=True)
        blocks.append(blk)
        gscore.append(m1 + m2)
    masked = []
    for g in range(N_EXPERT_GROUPS):
        rank = jnp.zeros((1, tm), jnp.int32)
        for o in range(N_EXPERT_GROUPS):
            if o == g:
                continue
            beats = (gscore[o] > gscore[g]) | ((gscore[o] == gscore[g]) & (o < g))
            rank = rank + beats.astype(jnp.int32)
        masked.append(jnp.where(rank < TOPK_GROUPS, blocks[g], MASK_SCORE))
    val = jnp.concatenate(masked, axis=0)
    io_e = lax.broadcasted_iota(jnp.int32, (N_EXPERTS, tm), 0)
    chosen = jnp.zeros((N_EXPERTS, tm), jnp.bool_)
    picks = []
    for k in range(TOP_K):
        m = jnp.max(val, axis=0, keepdims=True)
        idx = jnp.min(jnp.where(val == m, io_e, N_EXPERTS), axis=0, keepdims=True)
        pick = io_e == idx
        picks.append(pick)
        eid_ref[k:k + 1, :] = idx
        chosen = chosen | pick
        val = jnp.where(pick, neg_inf, val)
    w = jnp.where(chosen, score, 0.0)
    w = w / jnp.sum(w, axis=0, keepdims=True) * ROUTED_SCALE

    chosen_f = chosen.astype(F32)
    earlier = (lax.broadcasted_iota(jnp.int32, (tm, tm), 0) < lax.broadcasted_iota(jnp.int32, (tm, tm), 1))
    before = jnp.dot(chosen_f.astype(BF16), earlier.astype(BF16), preferred_element_type=F32)
    grank = carry[:, 0:1] + before
    wscr[...] = jnp.zeros_like(wscr)
    for k in range(TOP_K):
        rank_ref[k:k + 1, :] = jnp.sum(jnp.where(picks[k], grank, 0.0), axis=0, keepdims=True).astype(jnp.int32)
        wscr[k:k + 1, :] = jnp.sum(jnp.where(picks[k], w, 0.0), axis=0, keepdims=True)
    wt_ref[...] = wscr[...].T
    carry[...] = carry[...] + jnp.sum(chosen_f, axis=1, keepdims=True)
    cnt_ref[...] = carry[...]


def _router(hp, rw_t, rb):
    n = hp[0].shape[0]
    tm = TM_ROUTER
    return pl.pallas_call(
        _router_kernel,
        grid=(n // tm,),
        in_specs=[_plane_spec(tm)] * N_PLANES + [
            pl.BlockSpec(rw_t.shape, lambda i: (0, 0)),
            pl.BlockSpec((N_EXPERTS, 1), lambda i: (0, 0)),
        ],
        out_specs=[
            pl.BlockSpec((TOP_K, tm), lambda i: (0, i)),
            pl.BlockSpec((TOP_K, tm), lambda i: (0, i)),
            pl.BlockSpec((tm, LANES), lambda i: (i, 0)),
            pl.BlockSpec((N_EXPERTS, LANES), lambda i: (0, 0)),
        ],
        out_shape=[
            jax.ShapeDtypeStruct((TOP_K, n), jnp.int32),
            jax.ShapeDtypeStruct((TOP_K, n), jnp.int32),
            jax.ShapeDtypeStruct((n, LANES), F32),
            jax.ShapeDtypeStruct((N_EXPERTS, LANES), F32),
        ],
        scratch_shapes=[pltpu.VMEM((N_EXPERTS, LANES), F32), pltpu.VMEM((LANES, tm), F32)],
        compiler_params=_cparams("arbitrary"),
        name="router",
    )(*hp, rw_t, rb)


def _plan_kernel(n_slots, cnt_ref, eid_ref, rank_ref, dest_ref, be_ref, nb_ref, pad_ref, np_ref):
    cnt = cnt_ref[...].astype(jnp.int32)
    padded = ((cnt + (MOE_BLOCK - 1)) >> MOE_BLOCK_LOG2) << MOE_BLOCK_LOG2
    ri = lax.broadcasted_iota(jnp.int32, (N_EXPERTS, N_EXPERTS), 0)
    ci = lax.broadcasted_iota(jnp.int32, (N_EXPERTS, N_EXPERTS), 1)
    pad_end = jnp.dot((ri >= ci).astype(F32), padded.astype(F32), precision=HI,
                      preferred_element_type=F32).astype(jnp.int32)
    pad_start = pad_end - padded
    eid = eid_ref[...]
    dest = rank_ref[...]
    for e in range(N_EXPERTS):
        dest = dest + jnp.where(eid == e, pad_start[e:e + 1, 0:1], 0)
    dest_ref[...] = dest
    nbp = be_ref.shape[1]
    jpos = lax.broadcasted_iota(jnp.int32, (N_EXPERTS, nbp), 1) * MOE_BLOCK
    be = jnp.minimum(jnp.sum((pad_end[:, 0:1] <= jpos).astype(jnp.int32), axis=0, keepdims=True), N_EXPERTS - 1)
    be_ref[...] = be
    nb_ref[...] = pad_end[N_EXPERTS - 1:N_EXPERTS, :] >> MOE_BLOCK_LOG2
    fill_end = pad_start + cnt
    io_e = lax.broadcasted_iota(jnp.int32, (N_EXPERTS, nbp), 0)
    end_j = jnp.sum(jnp.where(io_e == be, fill_end[:, 0:1], 0), axis=0, keepdims=True)
    pieces = (end_j - jpos[0:1, :] + (XS_PIECE - 1)) >> XS_PIECE_LOG2
    np_ref[...] = jnp.clip(pieces, 0, MOE_BLOCK // XS_PIECE)
    lane = lax.broadcasted_iota(jnp.int32, (N_EXPERTS, LANES), 1)
    piece_end = ((fill_end + (XS_PIECE - 1)) >> XS_PIECE_LOG2) << XS_PIECE_LOG2
    s = fill_end + lane
    pad_ref[...] = jnp.where(s < piece_end, s, n_slots + lane)


def _plan(cnt, eid, rank, n_slots):
    n = eid.shape[1]
    nbp = -(-(n_slots // MOE_BLOCK) // LANES) * LANES
    pad_rows = N_EXPERTS
    return pl.pallas_call(
        functools.partial(_plan_kernel, n_slots),
        grid=(1,),
        in_specs=[
            pl.BlockSpec(cnt.shape, lambda i: (0, 0)),
            pl.BlockSpec(eid.shape, lambda i: (0, 0)),
            pl.BlockSpec(rank.shape, lambda i: (0, 0)),
        ],
        out_specs=[
            pl.BlockSpec((TOP_K, n), lambda i: (0, 0)),
            pl.BlockSpec((1, nbp), lambda i: (0, 0)),
            pl.BlockSpec((1, LANES), lambda i: (0, 0)),
            pl.BlockSpec((pad_rows, LANES), lambda i: (0, 0)),
            pl.BlockSpec((1, nbp), lambda i: (0, 0)),
        ],
        out_shape=[
            jax.ShapeDtypeStruct((TOP_K, n), jnp.int32),
            jax.ShapeDtypeStruct((1, nbp), jnp.int32),
            jax.ShapeDtypeStruct((1, LANES), jnp.int32),
            jax.ShapeDtypeStruct((pad_rows, LANES), jnp.int32),
            jax.ShapeDtypeStruct((1, nbp), jnp.int32),
        ],
        compiler_params=_cparams("arbitrary"),
        name="moe_plan",
    )(cnt, eid, rank)


def _sc_mesh():
    return plsc.VectorSubcoreMesh(core_axis_name="c", subcore_axis_name="s")


def _sc_worker():
    return lax.axis_index("c") * SC_SUBCORES + lax.axis_index("s")


def _sc_dispatch(hp, dest_rows, pad_rows, n_rows):
    n = hp[0].shape[0]
    tiles_per_worker = n // SC_WIN // SC_WORKERS
    pad_per_worker = pad_rows.shape[0] // SC_WORKERS
    zeros = jnp.zeros((SC_WIN, LANES), jnp.uint32)

    def body(*refs):
        h = refs[:N_PLANES]
        dest_hbm, pad_hbm, z_hbm = refs[N_PLANES:N_PLANES + 3]
        xs = refs[N_PLANES + 3:2 * N_PLANES + 3]
        bufs = refs[2 * N_PLANES + 3:3 * N_PLANES + 3]
        ibuf, pbuf, sem = refs[3 * N_PLANES + 3:]
        wid = _sc_worker()

        pltpu.sync_copy(z_hbm, bufs[0])
        pltpu.sync_copy(pad_hbm.at[pl.ds(wid * pad_per_worker, pad_per_worker)], pbuf)
        copies = [pltpu.async_copy(bufs[0], xs[c].at[pbuf.at[r]], sem)
                  for r in range(pad_per_worker) for c in range(N_PLANES)]
        for cp in copies:
            cp.wait()

        @pl.loop(0, tiles_per_worker)
        def _(i):
            tile = wid * tiles_per_worker + i
            pltpu.sync_copy(dest_hbm.at[pl.ds(tile * TOP_K, TOP_K)], ibuf)
            for c in range(N_PLANES):
                pltpu.sync_copy(h[c].at[pl.ds(tile * SC_WIN, SC_WIN)], bufs[c])
            scatters = [pltpu.async_copy(bufs[c], xs[c].at[ibuf.at[k]], sem)
                        for c in range(N_PLANES) for k in range(TOP_K)]
            for cp in scatters:
                cp.wait()

    out_type = tuple(jax.ShapeDtypeStruct((n_rows, LANES), jnp.uint32) for _ in range(N_PLANES))
    scratch = ([pltpu.VMEM((SC_WIN, LANES), jnp.uint32)] * N_PLANES
               + [pltpu.VMEM((TOP_K, LANES), jnp.int32), pltpu.VMEM((pad_per_worker, LANES), jnp.int32),
                  pltpu.SemaphoreType.DMA])
    return pl.kernel(body, out_type=out_type, mesh=_sc_mesh(), scratch_types=scratch,
                     name="moe_sc_dispatch")(*hp, dest_rows, pad_rows, zeros)


def _sc_gather(ysp, dest_rows, n):
    tiles_per_worker = n // SC_WIN // SC_WORKERS

    def body(*refs):
        ys = refs[:N_PLANES]
        dest_hbm, g_hbm = refs[N_PLANES:N_PLANES + 2]
        bufs = refs[N_PLANES + 2:N_PLANES + 2 + SC_GATHER_BUFS]
        ibuf, sem = refs[N_PLANES + 2 + SC_GATHER_BUFS:]
        wid = _sc_worker()

        @pl.loop(0, tiles_per_worker)
        def _(i):
            tile = wid * tiles_per_worker + i
            pltpu.sync_copy(dest_hbm.at[pl.ds(tile * TOP_K, TOP_K)], ibuf)
            for c in range(N_PLANES):
                for k0 in range(0, TOP_K, SC_GATHER_BUFS):
                    gathers = [pltpu.async_copy(ys[c].at[ibuf.at[k0 + j]], bufs[j], sem)
                               for j in range(SC_GATHER_BUFS)]
                    for cp in gathers:
                        cp.wait()
                    stores = [pltpu.async_copy(
                        bufs[j], g_hbm.at[pl.ds(((k0 + j) * N_PLANES + c) * n + tile * SC_WIN, SC_WIN)], sem)
                        for j in range(SC_GATHER_BUFS)]
                    for cp in stores:
                        cp.wait()

    scratch = ([pltpu.VMEM((SC_WIN, LANES), jnp.uint32)] * SC_GATHER_BUFS
               + [pltpu.VMEM((TOP_K, LANES), jnp.int32), pltpu.SemaphoreType.DMA])
    return pl.kernel(body, out_type=jax.ShapeDtypeStruct((TOP_K * N_PLANES * n, LANES), jnp.uint32),
                     mesh=_sc_mesh(), scratch_types=scratch, name="moe_sc_gather")(*ysp, dest_rows)


def _expert_kernel(be_ref, nb_ref, np_ref, *refs):
    xs_hbm = refs[:N_PLANES]
    wg_ref, wu_ref, wd_ref = refs[N_PLANES:N_PLANES + 3]
    ys_refs = refs[N_PLANES + 3:2 * N_PLANES + 3]
    wg_b, wu_b, wd_b, ring, sems = refs[2 * N_PLANES + 3:]
    j = pl.program_id(0)
    nb = nb_ref[0]
    used = j < nb
    new_expert = (j == 0) | (be_ref[j] != be_ref[jnp.maximum(j - 1, 0)])

    def fetch(step, slot, wait):
        for p in range(MOE_BLOCK // XS_PIECE):
            @pl.when(p < np_ref[step])
            def _():
                src = pl.ds(pl.multiple_of(step * MOE_BLOCK + p * XS_PIECE, XS_PIECE), XS_PIECE)
                for c in range(N_PLANES):
                    cp = pltpu.make_async_copy(xs_hbm[c].at[src], ring.at[slot, c, pl.ds(p * XS_PIECE, XS_PIECE)],
                                               sems.at[slot])
                    if wait:
                        cp.wait()
                    else:
                        cp.start()

    @pl.when(j == 0)
    def _():
        ring[...] = jnp.zeros_like(ring)

    for s in range(XS_RING - 1):
        @pl.when((j == 0) & (s < nb))
        def _():
            fetch(s, s, wait=False)

    ahead = j + (XS_RING - 1)

    @pl.when(used & (ahead < nb))
    def _():
        fetch(ahead, lax.rem(ahead, XS_RING), wait=False)

    @pl.when(used & new_expert)
    def _():
        wg_b[...] = wg_ref[...].astype(BF16)
        wu_b[...] = wu_ref[...].astype(BF16)
        wd_b[...] = wd_ref[...].astype(BF16)

    @pl.when(used)
    def _():
        slot = lax.rem(j, XS_RING)
        fetch(j, slot, wait=True)
        x = _unpack_bf16_pairs(jnp.concatenate([ring[slot, c] for c in range(N_PLANES)], axis=1)).astype(BF16)
        a = jnp.dot(x, wg_b[...], preferred_element_type=F32)
        u = jnp.dot(x, wu_b[...], preferred_element_type=F32)
        y = jnp.dot((_silu(a) * u).astype(BF16), wd_b[...], preferred_element_type=F32)
        _store_planes(ys_refs, y)

    @pl.when(jnp.logical_not(used))
    def _():
        for ref in ys_refs:
            ref[...] = jnp.zeros_like(ref)


def _experts(be, nb, npieces, xsp, wg, wu, wd, layer, n_slots):
    d = wg.shape[2]

    def wspec(shape):
        def index(j, be_ref, nb_ref, np_ref):
            return layer, be_ref[jnp.minimum(j, jnp.maximum(nb_ref[0] - 1, 0))], 0, 0
        return pl.BlockSpec((None, None) + shape, index)

    grid_spec = pltpu.PrefetchScalarGridSpec(
        num_scalar_prefetch=3,
        grid=(n_slots // MOE_BLOCK,),
        in_specs=[pl.BlockSpec(memory_space=pl.ANY)] * N_PLANES
        + [wspec((d, D_EXPERT)), wspec((d, D_EXPERT)), wspec((D_EXPERT, d))],
        out_specs=[pl.BlockSpec((MOE_BLOCK, LANES), lambda j, be_ref, nb_ref, np_ref: (j, 0))] * N_PLANES,
        scratch_shapes=[pltpu.VMEM((d, D_EXPERT), BF16), pltpu.VMEM((d, D_EXPERT), BF16),
                        pltpu.VMEM((D_EXPERT, d), BF16),
                        pltpu.VMEM((XS_RING, N_PLANES, MOE_BLOCK, LANES), jnp.uint32),
                        pltpu.SemaphoreType.DMA((XS_RING,))],
    )
    return pl.pallas_call(
        _expert_kernel,
        grid_spec=grid_spec,
        out_shape=[jax.ShapeDtypeStruct((n_slots, LANES), jnp.uint32)] * N_PLANES,
        compiler_params=_cparams("arbitrary"),
        name="moe_experts",
    )(be, nb, npieces, *xsp, wg, wu, wd)


def _combine_kernel(final, g_ref, wt_ref, h0, h1, h2, h3, sg_ref, su_ref, sd_ref, x_ref, gate_ref, fw_ref, o_ref):
    hb = _load_planes((h0, h1, h2, h3)).astype(BF16)
    a = jnp.dot(hb, sg_ref[...], preferred_element_type=F32)
    u = jnp.dot(hb, su_ref[...], preferred_element_type=F32)
    acc = jnp.dot((_silu(a) * u).astype(BF16), sd_ref[...], preferred_element_type=F32)
    wt = wt_ref[...]
    for k in range(TOP_K):
        rows = _unpack_bf16_pairs(jnp.concatenate([g_ref[k * N_PLANES + c] for c in range(N_PLANES)], axis=1))
        acc = acc + wt[:, k:k + 1] * rows
    xn = x_ref[...] + gate_ref[...] * acc
    if final:
        ms = jnp.mean(xn * xn, axis=-1, keepdims=True)
        xn = xn * lax.rsqrt(ms + EPS) * fw_ref[...]
    o_ref[...] = xn


def _combine(g, wt, hp, sg, su, sd, x2, mod3, fw, seq, final):
    n, d = x2.shape
    tm = TM_COMBINE
    tpb = seq // tm
    return pl.pallas_call(
        functools.partial(_combine_kernel, final),
        grid=(n // tm,),
        in_specs=[
            pl.BlockSpec((TOP_K * N_PLANES, tm, LANES), lambda i: (0, i, 0)),
            pl.BlockSpec((tm, LANES), lambda i: (i, 0)),
        ] + [_plane_spec(tm)] * N_PLANES + [
            pl.BlockSpec(sg.shape, lambda i: (0, 0)),
            pl.BlockSpec(su.shape, lambda i: (0, 0)),
            pl.BlockSpec(sd.shape, lambda i: (0, 0)),
            pl.BlockSpec((tm, d), lambda i: (i, 0)),
            pl.BlockSpec((None, 1, d), lambda i: ((i // tpb) * N_ADA + 5, 0, 0)),
            pl.BlockSpec((1, d), lambda i: (0, 0)),
        ],
        out_specs=pl.BlockSpec((tm, d), lambda i: (i, 0)),
        out_shape=jax.ShapeDtypeStruct((n, d), F32),
        compiler_params=_cparams("arbitrary"),
        name="moe_combine",
    )(g.reshape(TOP_K * N_PLANES, n, LANES), wt, *hp, sg, su, sd, x2, mod3, fw)


def _blockdiag_pairs(w):
    z = jnp.zeros((LRU_BLOCK_W, LRU_BLOCK_W), w.dtype)
    tiles = []
    for j in range(LRU_BLOCKS // 2):
        top = jnp.concatenate([w[2 * j], z], axis=1)
        bot = jnp.concatenate([z, w[2 * j + 1]], axis=1)
        tiles.append(jnp.concatenate([top, bot], axis=0))
    return jnp.stack(tiles).astype(BF16)


def _hg_level_masks():
    ch = HG_CHUNK
    msk = np.zeros((HG_LEVELS, ch, ch), np.float32)
    for lvl in range(HG_LEVELS):
        half = 1 << lvl
        for t in range(ch):
            base = (t // (2 * half)) * (2 * half)
            if (t // half) % 2 == 1:
                msk[lvl, t, base:base + half] = 1.0
    later = ((np.arange(ch)[None, :, None] >> np.arange(HG_LEVELS)[:, None, None]) & 1).astype(np.float32)
    later = np.broadcast_to(later, (HG_LEVELS, ch, HG_EXPAND))
    sgn = (2.0 * later - 1.0) * np.float32(LOG2E)
    return jnp.asarray(msk), jnp.asarray(later), jnp.asarray(sgn, dtype=F32)


def _ssd_expand():
    e = np.zeros((LANES, SSD_INNER), np.float32)
    for h in range(SSD_HEADS):
        e[h, h * SSD_HEADDIM:(h + 1) * SSD_HEADDIM] = 1.0
    return jnp.asarray(np.concatenate([e] * 3, axis=0), dtype=BF16)


def _pad_lanes(v, width):
    return jnp.pad(v, (0, width - v.shape[0])).reshape(1, width)


def _layer_params(l, w_in, ssd_conv_w, ssd_conv_b, ssd_dt_bias, ssd_a_log, ssd_d, ssd_norm_w, hg_lower_bounds,
                  hg_norm_w, lru_conv_w, lru_conv_b, lru_wa, lru_ba, lru_wx, lru_bx, lru_lambda, lru_norm_w):
    wi = w_in[l]
    dt0 = SSD_INNER + SSD_INNER + 2 * SSD_GROUPS * SSD_STATE
    w_cat = jnp.concatenate([wi[:, :dt0], wi[:, dt0 + SSD_HEADS:], wi[:, dt0:dt0 + SSD_HEADS]], axis=1)
    w_cat = jnp.pad(w_cat, ((0, 0), (0, U_WIDTH - w_cat.shape[1]))).astype(BF16)
    msk, later, sgn = _hg_level_masks()
    return dict(
        w_cat=w_cat,
        cwx=ssd_conv_w[l][:, :SSD_INNER], cbx=ssd_conv_b[l][:SSD_INNER].reshape(1, -1),
        cwb=ssd_conv_w[l][:, SSD_INNER:], cbb=ssd_conv_b[l][SSD_INNER:].reshape(1, -1),
        dtb=_pad_lanes(ssd_dt_bias[l], LANES), alog=_pad_lanes(ssd_a_log[l], LANES),
        dful=jnp.repeat(ssd_d[l], SSD_HEADDIM).reshape(1, -1), ssd_nw=ssd_norm_w[l].reshape(1, -1),
        expand=_ssd_expand(),
        hg_lb=hg_lower_bounds, hg_nw=hg_norm_w[l].reshape(1, -1), hg_msk=msk, hg_later=later, hg_sgn=sgn,
        lru_cw=lru_conv_w[l], lru_cb=lru_conv_b[l].reshape(1, -1),
        lru_wa=_blockdiag_pairs(lru_wa[l]), lru_ba=lru_ba[l].reshape(1, -1),
        lru_wx=_blockdiag_pairs(lru_wx[l]), lru_bx=lru_bx[l].reshape(1, -1),
        lru_lam=lru_lambda[l].reshape(1, -1), lru_nw=lru_norm_w[l].reshape(1, -1),
    )


def kernel(x, c, ada_w, ada_b, norm_mix_w, norm_ffn_w, w_in, ssd_conv_w, ssd_conv_b, ssd_dt_bias, ssd_a_log, ssd_d, ssd_norm_w, hg_lower_bounds, hg_norm_w, lru_conv_w, lru_conv_b, lru_wa, lru_ba, lru_wx, lru_bx, lru_lambda, lru_norm_w, w_out, router_w, router_bias, exp_gate, exp_up, exp_down, sh_gate, sh_up, sh_down, final_norm_w):
    bsz, seq, d = x.shape
    depth = ada_w.shape[0]
    assert d == D_MODEL and seq % TM_OUTPROJ == 0 and seq % SSD_CHUNK == 0 and seq % R_HG == 0
    n = bsz * seq
    assert n % (SC_WIN * SC_WORKERS) == 0
    n_slots = n * TOP_K + N_EXPERTS * MOE_BLOCK
    x2 = x.reshape(n, d)
    mod = _adaln(c, ada_w, ada_b)
    fw = final_norm_w.reshape(1, d)
    for l in range(depth):
        p = _layer_params(l, w_in, ssd_conv_w, ssd_conv_b, ssd_dt_bias, ssd_a_log, ssd_d, ssd_norm_w,
                          hg_lower_bounds, hg_norm_w, lru_conv_w, lru_conv_b, lru_wa, lru_ba, lru_wx, lru_bx,
                          lru_lambda, lru_norm_w)
        mod3 = mod[l].reshape(bsz * N_ADA, 1, d)
        u = _inproj(x2, norm_mix_w[l].reshape(1, d), mod3, p["w_cat"], seq)
        y_ssd = _ssd(u, p, bsz, seq)
        y_hg, y_lru = _hgrn2_lru(u, p, l, bsz, seq)
        x2, hp = _outproj(y_ssd, y_hg, y_lru, x2, w_out[l].astype(BF16), norm_ffn_w[l].reshape(1, d), mod3, seq)
        eid, rank, wt, cnt = _router(hp, router_w[l].T, router_bias[l].reshape(N_EXPERTS, 1))
        dest, be, nb, pad_rows, npieces = _plan(cnt, eid, rank, n_slots)
        dest_rows = dest.reshape(TOP_K, n // LANES, LANES).transpose(1, 0, 2).reshape(n // LANES * TOP_K, LANES)
        xsp = _sc_dispatch(hp, dest_rows, pad_rows, n_slots + LANES)
        ysp = _experts(be.reshape(-1), nb[0, :1], npieces.reshape(-1), xsp, exp_gate, exp_up, exp_down, l, n_slots)
        g = _sc_gather(ysp, dest_rows, n)
        x2 = _combine(g, wt, hp, sh_gate[l].astype(BF16), sh_up[l].astype(BF16), sh_down[l].astype(BF16),
                      x2, mod3, fw, seq, final=(l == depth - 1))
    return x2.reshape(bsz, seq, d)
```

```python
import functools

import jax
import jax.numpy as jnp
import numpy as np
from jax import lax
from jax.experimental import pallas as pl
from jax.experimental.pallas import tpu as pltpu
from jax.experimental.pallas import tpu_sc as plsc

F32 = jnp.float32
BF16 = jnp.bfloat16
HI = lax.Precision.HIGHEST
F32_TINY = float(np.finfo(np.float32).tiny)
LOG2E = float(np.log2(np.e))

LANES = 128
SUBLANES = 8
VMEM_LIMIT_BYTES = 56 * 1024 * 1024

D_MODEL = 1024
EPS = 1e-6
N_ADA = 6
CONV_WIDTH = 4
SSD_INNER = 1024
SSD_HEADDIM = 64
SSD_HEADS = 16
SSD_GROUPS = 2
SSD_STATE = 128
SSD_CHUNK = 128
SSD_GROUP_W = SSD_INNER // SSD_GROUPS
HG_WIDTH = 512
HG_EXPAND = 128
HG_HEADS = 4
HG_CHUNK = 128
HG_LEVELS = 7
LRU_WIDTH = 512
LRU_BLOCKS = 8
LRU_BLOCK_W = 64
LRU_C = 8.0
N_EXPERTS = 64
TOP_K = 8
N_EXPERT_GROUPS = 8
E_PER_GROUP = 8
TOPK_GROUPS = 4
D_EXPERT = 256
ROUTED_SCALE = 2.5
MASK_SCORE = -1.0e4

COL_Z = 0
COL_XS = 1024
COL_BC = 2048
COL_HQ = 2560
COL_HF = 3072
COL_HV = 3584
COL_HG = 4096
COL_LG = 4608
COL_LX = 5120
COL_DT = 5632
U_WIDTH = 5760

TM_INPROJ = 512
TM_OUTPROJ = 1024
TM_ROUTER = 512
TM_COMBINE = 512
SC_CORES = 2
SC_SUBCORES = 16
SC_WORKERS = SC_CORES * SC_SUBCORES
SC_WIN = LANES
SC_GATHER_BUFS = 4
MOE_BLOCK_LOG2 = 10
MOE_BLOCK = 1 << MOE_BLOCK_LOG2
XS_RING = 3
XS_PIECE_LOG2 = 7
XS_PIECE = 1 << XS_PIECE_LOG2
R_HG = 512


def _cparams(*sem):
    return pltpu.CompilerParams(dimension_semantics=sem, vmem_limit_bytes=VMEM_LIMIT_BYTES)


def _sigmoid(x):
    return 0.5 * jnp.tanh(0.5 * x) + 0.5


def _silu(x):
    h = 0.5 * x
    return h + h * jnp.tanh(h)


def _split3(x):
    x1 = x.astype(BF16)
    r1 = x - x1.astype(F32)
    x2 = r1.astype(BF16)
    x3 = (r1 - x2.astype(F32)).astype(BF16)
    return x1, x2, x3


def _softplus(x):
    return jnp.maximum(x, 0.0) + jnp.log1p(jnp.exp(-jnp.abs(x)))


def _norm_mod(x, nw, shift, scale):
    ms = jnp.mean(x * x, axis=-1, keepdims=True)
    y = x * lax.rsqrt(ms + EPS) * nw
    return y * (1.0 + scale) + shift


_HI16 = np.uint32(0xFFFF0000)


def _pack_bf16_pairs(x):
    half = x.shape[1] // 2
    bits = lax.bitcast_convert_type(x.astype(BF16).astype(F32), jnp.uint32)
    return (bits[:, :half] & _HI16) | (bits[:, half:] >> 16)


def _unpack_bf16_pairs(w):
    hi = lax.bitcast_convert_type(w & _HI16, F32)
    lo = lax.bitcast_convert_type(w << 16, F32)
    return jnp.concatenate([hi, lo], axis=1)


N_PLANES = D_MODEL // 2 // LANES


def _store_planes(refs, x):
    packed = _pack_bf16_pairs(x)
    for c, ref in enumerate(refs):
        ref[...] = packed[:, c * LANES:(c + 1) * LANES]


def _load_planes(refs):
    return _unpack_bf16_pairs(jnp.concatenate([ref[...] for ref in refs], axis=1))


def _ada_kernel(c_ref, w_ref, b_ref, o_ref):
    c = c_ref[...]
    o_ref[...] = jnp.dot(_silu(c), w_ref[...], precision=HI, preferred_element_type=F32) + b_ref[...]


def _adaln(c, ada_w, ada_b):
    depth, d, n6 = ada_w.shape
    bsz = c.shape[0]
    tn = 1536
    return pl.pallas_call(
        _ada_kernel,
        grid=(depth, n6 // tn),
        in_specs=[
            pl.BlockSpec((bsz, d), lambda l, j: (0, 0)),
            pl.BlockSpec((None, d, tn), lambda l, j: (l, 0, j)),
            pl.BlockSpec((None, 1, tn), lambda l, j: (l, 0, j)),
        ],
        out_specs=pl.BlockSpec((None, bsz, tn), lambda l, j: (l, 0, j)),
        out_shape=jax.ShapeDtypeStruct((depth, bsz, n6), F32),
        compiler_params=_cparams("arbitrary", "arbitrary"),
        name="adaln_mod",
    )(c, ada_w, ada_b.reshape(depth, 1, n6))


def _inproj_kernel(x_ref, nw_ref, sh_ref, sc_ref, w_ref, o_ref):
    h = _norm_mod(x_ref[...], nw_ref[...], sh_ref[...], sc_ref[...])
    o_ref[...] = jnp.dot(h.astype(BF16), w_ref[...], preferred_element_type=F32)


def _inproj(x2, nw, mod3, w_cat, seq):
    n, d = x2.shape
    tm = TM_INPROJ
    tpb = seq // tm
    return pl.pallas_call(
        _inproj_kernel,
        grid=(n // tm,),
        in_specs=[
            pl.BlockSpec((tm, d), lambda i: (i, 0)),
            pl.BlockSpec((1, d), lambda i: (0, 0)),
            pl.BlockSpec((None, 1, d), lambda i: ((i // tpb) * N_ADA + 0, 0, 0)),
            pl.BlockSpec((None, 1, d), lambda i: ((i // tpb) * N_ADA + 1, 0, 0)),
            pl.BlockSpec((d, U_WIDTH), lambda i: (0, 0), pipeline_mode=pl.Buffered(1)),
        ],
        out_specs=pl.BlockSpec((tm, U_WIDTH), lambda i: (i, 0)),
        out_shape=jax.ShapeDtypeStruct((n, U_WIDTH), F32),
        compiler_params=_cparams("arbitrary"),
        name="inproj",
    )(x2, nw, mod3, mod3, w_cat)


def _causal_conv(cur_ref, ext, tail, cw_ref, cb_ref):
    rows, width = cur_ref.shape
    cur = cur_ref[...]
    ext[0:SUBLANES, :] = tail[...]
    ext[SUBLANES:SUBLANES + rows, :] = cur
    tail[...] = cur[rows - SUBLANES:rows, :]
    groups = ext[...].reshape(rows // SUBLANES + 1, SUBLANES, width)
    sub = lax.broadcasted_iota(jnp.int32, (1, SUBLANES, width), 1)
    acc = cb_ref[...] + cur * cw_ref[CONV_WIDTH - 1:CONV_WIDTH, :]
    for d in range(1, CONV_WIDTH):
        rot = pltpu.roll(groups, d, 1)
        back = jnp.where(sub < d, rot[:-1], rot[1:]).reshape(rows, width)
        acc = acc + back * cw_ref[CONV_WIDTH - 1 - d:CONV_WIDTH - d, :]
    return acc


def _ssd_kernel(z_ref, xs_ref, bc_ref, dt_ref, cwx_ref, cbx_ref, cwb_ref, cbb_ref, dtb_ref, alog_ref,
                dful_ref, nw_ref, e_ref, o_ref, extx, extb, tailx, tailb, hstate):
    c = pl.program_id(1)
    q = SSD_CHUNK

    @pl.when(c == 0)
    def _():
        tailx[...] = jnp.zeros_like(tailx)
        tailb[...] = jnp.zeros_like(tailb)
        hstate[...] = jnp.zeros_like(hstate)

    xs = _silu(_causal_conv(xs_ref, extx, tailx, cwx_ref, cbx_ref))
    bc = _silu(_causal_conv(bc_ref, extb, tailb, cwb_ref, cbb_ref))

    dt = _softplus(dt_ref[...] + dtb_ref[...])
    a = dt * (-jnp.exp(alog_ref[...]))
    ri = lax.broadcasted_iota(jnp.int32, (q, q), 0)
    ci = lax.broadcasted_iota(jnp.int32, (q, q), 1)
    tril = ri >= ci
    tril_f = tril.astype(F32)
    tril_b = tril.astype(BF16)
    acum = jnp.dot(jnp.concatenate([tril_b] * 3, axis=1), jnp.concatenate(_split3(a), axis=0),
                   preferred_element_type=F32)
    acum_t = acum.T
    expand3 = e_ref[...]
    dt_full = jnp.dot(jnp.concatenate(_split3(dt), axis=1), expand3, preferred_element_type=F32)
    acum_full = jnp.dot(jnp.concatenate(_split3(acum), axis=1), expand3, preferred_element_type=F32)
    alast_full = acum_full[q - 1:q, :]

    xdt = xs * dt_full
    xdt_b = xdt.astype(BF16)
    exp_a = jnp.exp(acum_full)
    xd_b = (xdt * jnp.exp(alast_full - acum_full)).astype(BF16)
    state_decay = jnp.exp(alast_full)
    left = lax.broadcasted_iota(jnp.int32, (q, LANES), 1) < SSD_HEADDIM
    zero_b = jnp.zeros((q, LANES), BF16)

    ys = []
    for g in range(SSD_GROUPS):
        b_g = bc[:, g * SSD_STATE:(g + 1) * SSD_STATE]
        c_g = bc[:, (SSD_GROUPS + g) * SSD_STATE:(SSD_GROUPS + g + 1) * SSD_STATE]
        c_b = c_g.astype(BF16)
        cb = lax.dot_general(c_b, b_g.astype(BF16), (((1,), (1,)), ((), ())), preferred_element_type=F32)
        cb = cb * tril_f
        cs = slice(g * SSD_GROUP_W, (g + 1) * SSD_GROUP_W)
        h_g = hstate[:, cs]
        y_off = jnp.dot(c_b, h_g.astype(BF16), preferred_element_type=F32) * exp_a[:, cs]
        pieces = []
        for pr in range(SSD_HEADS // SSD_GROUPS // 2):
            h0 = g * (SSD_HEADS // SSD_GROUPS) + 2 * pr
            ms = []
            for h in (h0, h0 + 1):
                col = acum[:, h:h + 1]
                row = acum_t[h:h + 1, :]
                ms.append((cb * jnp.exp(jnp.minimum(col - row, 0.0))).astype(BF16))
            lhs = jnp.concatenate(ms, axis=1)
            xp = xdt_b[:, h0 * SSD_HEADDIM:(h0 + 2) * SSD_HEADDIM]
            rhs = jnp.concatenate([jnp.where(left, xp, zero_b), jnp.where(left, zero_b, xp)], axis=0)
            pieces.append(jnp.dot(lhs, rhs, preferred_element_type=F32))
        ys.append(jnp.concatenate(pieces, axis=1) + y_off)
        b_t = b_g.T.astype(BF16)
        hstate[:, cs] = h_g * state_decay[:, cs] + jnp.dot(b_t, xd_b[:, cs], preferred_element_type=F32)

    y = jnp.concatenate(ys, axis=1) + xs * dful_ref[...]
    y = y * _silu(z_ref[...])
    outs = []
    for g in range(SSD_GROUPS):
        cs = slice(g * SSD_GROUP_W, (g + 1) * SSD_GROUP_W)
        yg = y[:, cs]
        ms = jnp.mean(yg * yg, axis=-1, keepdims=True)
        outs.append(yg * lax.rsqrt(ms + EPS) * nw_ref[:, cs])
    o_ref[...] = jnp.concatenate(outs, axis=1).astype(o_ref.dtype)


def _ssd(u, p, bsz, seq):
    q = SSD_CHUNK
    nc = seq // q
    n = bsz * seq

    def rows(b, c):
        return b * nc + c

    def const(shape):
        return pl.BlockSpec(shape, lambda b, c: (0,) * len(shape))

    return pl.pallas_call(
        _ssd_kernel,
        grid=(bsz, nc),
        in_specs=[
            pl.BlockSpec((q, SSD_INNER), lambda b, c: (rows(b, c), COL_Z // SSD_INNER)),
            pl.BlockSpec((q, SSD_INNER), lambda b, c: (rows(b, c), COL_XS // SSD_INNER)),
            pl.BlockSpec((q, 512), lambda b, c: (rows(b, c), COL_BC // 512)),
            pl.BlockSpec((q, LANES), lambda b, c: (rows(b, c), COL_DT // LANES)),
            const((CONV_WIDTH, SSD_INNER)), const((1, SSD_INNER)),
            const((CONV_WIDTH, 512)), const((1, 512)),
            const((1, LANES)), const((1, LANES)),
            const((1, SSD_INNER)), const((1, SSD_INNER)),
            const((3 * LANES, SSD_INNER)),
        ],
        out_specs=pl.BlockSpec((q, SSD_INNER), lambda b, c: (rows(b, c), 0)),
        out_shape=jax.ShapeDtypeStruct((n, SSD_INNER), BF16),
        scratch_shapes=[
            pltpu.VMEM((q + SUBLANES, SSD_INNER), F32),
            pltpu.VMEM((q + SUBLANES, 512), F32),
            pltpu.VMEM((SUBLANES, SSD_INNER), F32),
            pltpu.VMEM((SUBLANES, 512), F32),
            pltpu.VMEM((SSD_STATE, SSD_INNER), F32),
        ],
        compiler_params=_cparams("arbitrary", "arbitrary"),
        name="ssd_mixer",
    )(u, u, u, u, p["cwx"], p["cbx"], p["cwb"], p["cbb"], p["dtb"], p["alog"], p["dful"], p["ssd_nw"], p["expand"])


def _boundary_rows(b, lvl):
    half = 1 << lvl
    parts = []
    for v in range(b.shape[0] // SUBLANES):
        r0 = v * SUBLANES
        if 2 * half >= SUBLANES:
            src = (r0 // (2 * half)) * (2 * half) + half - 1
            parts.append(jnp.broadcast_to(b[src:src + 1, :], (SUBLANES, b.shape[1])))
        else:
            sub = lax.broadcasted_iota(jnp.int32, (SUBLANES, b.shape[1]), 0)
            piece = None
            for g in range(SUBLANES // (2 * half)):
                src = r0 + g * 2 * half + half - 1
                cand = jnp.broadcast_to(b[src:src + 1, :], (SUBLANES, b.shape[1]))
                piece = cand if piece is None else jnp.where(sub >= g * 2 * half, cand, piece)
            parts.append(piece)
    return jnp.concatenate(parts, axis=0)


def _hgrn2_lru_kernel(layer, q_ref, f_ref, v_ref, g_ref, lb_ref, nw_ref, msk_ref, half_ref, sgn_ref,
                      lg_ref, lx_ref, cw_ref, cb_ref, wa_ref, ba_ref, wx_ref, bx_ref, lam_ref, lnw_ref,
                      o_ref, ol_ref, state_t, ext, tail, hcarry):
    t = pl.program_id(1)
    ch = HG_CHUNK

    @pl.when(t == 0)
    def _():
        state_t[...] = jnp.zeros_like(state_t)
        tail[...] = jnp.zeros_like(tail)
        hcarry[...] = jnp.zeros_like(hcarry)

    _lru_tile(lg_ref, lx_ref, cw_ref, cb_ref, wa_ref, ba_ref, wx_ref, bx_ref, lam_ref, lnw_ref, ol_ref,
              ext, tail, hcarry)

    lrows = [lb_ref[j:j + 1, :] for j in range(lb_ref.shape[0])]
    mx = functools.reduce(jnp.maximum, lrows)
    es = [jnp.exp(r - mx) for r in lrows]
    den = functools.reduce(lambda a_, b_: a_ + b_, es)
    lb = jnp.zeros_like(mx)
    for j in range(1, layer + 1):
        lb = lb + es[j] / den
    one_minus_lb = 1.0 - lb
    nw = nw_ref[...]

    ri = lax.broadcasted_iota(jnp.int32, (ch, ch), 0)
    ci = lax.broadcasted_iota(jnp.int32, (ch, ch), 1)
    tril3 = jnp.concatenate([(ri >= ci).astype(BF16)] * 3, axis=1)
    rowi = lax.broadcasted_iota(jnp.int32, (ch, HG_EXPAND), 0)
    tgt = [((rowi >> lvl) & 1) == 1 for lvl in range(HG_LEVELS)]

    def head_chunk(h, rs):
        cs = slice(h * HG_EXPAND, (h + 1) * HG_EXPAND)
        qq = _silu(q_ref[rs, cs])
        kk = one_minus_lb[:, cs] * _sigmoid(-f_ref[rs, cs])
        logf = jnp.log1p(-kk)
        vv = v_ref[rs, cs]
        vb = vv.astype(BF16)
        b = jnp.dot(tril3, jnp.concatenate(_split3(logf), axis=0),
                    preferred_element_type=F32)
        st = state_t[h]
        o = lax.dot_general((qq * jnp.exp(b)).astype(BF16), st.astype(BF16), (((1,), (1,)), ((), ())),
                            preferred_element_type=F32)
        attn = jnp.zeros((ch, ch), F32)
        for lvl in range(HG_LEVELS):
            if lvl == 0:
                qe = jnp.where(tgt[0], qq * (1.0 - kk), 0.0)
                ke = jnp.where(tgt[0], 0.0, kk)
            else:
                m = _boundary_rows(b, lvl)
                later = half_ref[lvl]
                e = jnp.exp2((b - m) * sgn_ref[lvl])
                prod = jnp.where(tgt[lvl], qq, kk) * e
                qe = prod * later
                ke = prod - qe
            prod = lax.dot_general(qe.astype(BF16), ke.astype(BF16), (((1,), (1,)), ((), ())),
                                   preferred_element_type=F32)
            attn = attn + msk_ref[lvl] * prod
        diag = jnp.sum(qq * kk, axis=-1, keepdims=True)
        o = o + jnp.dot(attn.astype(BF16), vb, preferred_element_type=F32) + diag * vv
        b_last = b[ch - 1:ch, :]
        kd = (kk * jnp.exp(b_last - b)).astype(BF16)
        state_t[h] = st * jnp.exp(b_last) + jnp.dot(vv.T.astype(BF16), kd, preferred_element_type=F32)
        ms = jnp.mean(o * o, axis=-1, keepdims=True)
        y = o * lax.rsqrt(ms + EPS) * nw[:, cs]
        o_ref[rs, cs] = (y * _silu(g_ref[rs, cs])).astype(o_ref.dtype)

    for j in range(q_ref.shape[0] // ch):
        for h in range(HG_HEADS):
            head_chunk(h, slice(j * ch, (j + 1) * ch))


def _lru_tile(g_ref, x_ref, cw_ref, cb_ref, wa_ref, ba_ref, wx_ref, bx_ref, lam_ref, nw_ref, o_ref,
              ext, tail, hcarry):
    rows = x_ref.shape[0]
    xb = _causal_conv(x_ref, ext, tail, cw_ref, cb_ref)
    xbb = xb.astype(BF16)
    npair = LRU_WIDTH // LANES
    ra = jnp.concatenate([jnp.dot(xbb[:, j * LANES:(j + 1) * LANES], wa_ref[j], preferred_element_type=F32)
                          for j in range(npair)], axis=1)
    rx = jnp.concatenate([jnp.dot(xbb[:, j * LANES:(j + 1) * LANES], wx_ref[j], preferred_element_type=F32)
                          for j in range(npair)], axis=1)
    r = _sigmoid(ra + ba_ref[...])
    i = _sigmoid(rx + bx_ref[...])
    log_a = -LRU_C * r * _softplus(-lam_ref[...])
    a = jnp.exp(log_a)
    th = jnp.tanh(log_a)
    s = -2.0 * th
    root = s * lax.rsqrt(jnp.maximum(s * (1.0 - th), F32_TINY))
    u = root * (i * xb)

    ngroups = rows // SUBLANES
    sub = lax.broadcasted_iota(jnp.int32, (1, SUBLANES, LRU_WIDTH), 1)
    acc_a = a.reshape(ngroups, SUBLANES, LRU_WIDTH)
    acc_u = u.reshape(ngroups, SUBLANES, LRU_WIDTH)
    d = 1
    while d < SUBLANES:
        keep = sub >= d
        a_sh = jnp.where(keep, pltpu.roll(acc_a, d, 1), 1.0)
        u_sh = jnp.where(keep, pltpu.roll(acc_u, d, 1), 0.0)
        acc_u = acc_a * u_sh + acc_u
        acc_a = acc_a * a_sh
        d *= 2
    acc_a = acc_a.reshape(rows, LRU_WIDTH)
    acc_u = acc_u.reshape(rows, LRU_WIDTH)
    carry = hcarry[0:1, :]
    groups = []
    for g in range(rows // SUBLANES):
        gs = slice(g * SUBLANES, (g + 1) * SUBLANES)
        hg = acc_a[gs, :] * carry + acc_u[gs, :]
        groups.append(hg)
        carry = hg[SUBLANES - 1:SUBLANES, :]
    h = jnp.concatenate(groups, axis=0)
    hcarry[0:1, :] = carry

    gate = g_ref[...]
    gelu = 0.5 * gate * (1.0 + jnp.tanh(np.sqrt(2.0 / np.pi).astype(np.float32) * (gate + 0.044715 * (gate * gate * gate))))
    y = h * gelu
    ms = jnp.mean(y * y, axis=-1, keepdims=True)
    o_ref[...] = (y * lax.rsqrt(ms + EPS) * nw_ref[...]).astype(o_ref.dtype)


def _hgrn2_lru(u, p, layer, bsz, seq):
    r = R_HG
    nt = seq // r
    n = bsz * seq
    npair = LRU_WIDTH // LANES

    def col(base, width):
        return pl.BlockSpec((r, width), lambda b, t: (b * nt + t, base // width))

    def const(shape):
        return pl.BlockSpec(shape, lambda b, t: (0,) * len(shape))

    def out(width):
        return pl.BlockSpec((r, width), lambda b, t: (b * nt + t, 0))

    return pl.pallas_call(
        functools.partial(_hgrn2_lru_kernel, layer),
        grid=(bsz, nt),
        in_specs=[
            col(COL_HQ, HG_WIDTH), col(COL_HF, HG_WIDTH), col(COL_HV, HG_WIDTH), col(COL_HG, HG_WIDTH),
            const(p["hg_lb"].shape), const((1, HG_WIDTH)), const((HG_LEVELS, HG_CHUNK, HG_CHUNK)),
            const((HG_LEVELS, HG_CHUNK, HG_EXPAND)), const((HG_LEVELS, HG_CHUNK, HG_EXPAND)),
            col(COL_LG, LRU_WIDTH), col(COL_LX, LRU_WIDTH),
            const((CONV_WIDTH, LRU_WIDTH)), const((1, LRU_WIDTH)),
            const((npair, LANES, LANES)), const((1, LRU_WIDTH)),
            const((npair, LANES, LANES)), const((1, LRU_WIDTH)),
            const((1, LRU_WIDTH)), const((1, LRU_WIDTH)),
        ],
        out_specs=[out(HG_WIDTH), out(LRU_WIDTH)],
        out_shape=[jax.ShapeDtypeStruct((n, HG_WIDTH), BF16), jax.ShapeDtypeStruct((n, LRU_WIDTH), BF16)],
        scratch_shapes=[
            pltpu.VMEM((HG_HEADS, HG_EXPAND, HG_EXPAND), F32),
            pltpu.VMEM((r + SUBLANES, LRU_WIDTH), F32),
            pltpu.VMEM((SUBLANES, LRU_WIDTH), F32),
            pltpu.VMEM((SUBLANES, LRU_WIDTH), F32),
        ],
        compiler_params=_cparams("arbitrary", "arbitrary"),
        name="hgrn2_lru_mixer",
    )(u, u, u, u, p["hg_lb"], p["hg_nw"], p["hg_msk"], p["hg_later"], p["hg_sgn"],
      u, u, p["lru_cw"], p["lru_cb"], p["lru_wa"], p["lru_ba"], p["lru_wx"], p["lru_bx"], p["lru_lam"], p["lru_nw"])


def _outproj_kernel(ys_ref, yh_ref, yl_ref, x_ref, w_ref, g_ref, nw_ref, sh_ref, sc_ref, xo_ref, h0, h1, h2, h3,
                    w_b):
    @pl.when(pl.program_id(0) == 0)
    def _():
        w_b[...] = w_ref[...].astype(BF16)

    acc = jnp.dot(ys_ref[...], w_b[0:SSD_INNER, :], preferred_element_type=F32)
    acc = acc + jnp.dot(yh_ref[...], w_b[SSD_INNER:SSD_INNER + HG_WIDTH, :], preferred_element_type=F32)
    acc = acc + jnp.dot(yl_ref[...], w_b[SSD_INNER + HG_WIDTH:, :], preferred_element_type=F32)
    xn = x_ref[...] + g_ref[...] * acc
    xo_ref[...] = xn
    _store_planes((h0, h1, h2, h3), _norm_mod(xn, nw_ref[...], sh_ref[...], sc_ref[...]))


def _outproj(y_ssd, y_hg, y_lru, x2, w_out, layer, nw, mod3, seq):
    n, d = x2.shape
    tm = TM_OUTPROJ
    tpb = seq // tm

    def modspec(j):
        return pl.BlockSpec((None, 1, d), lambda i: ((i // tpb) * N_ADA + j, 0, 0))

    outs = pl.pallas_call(
        _outproj_kernel,
        grid=(n // tm,),
        in_specs=[
            pl.BlockSpec((tm, SSD_INNER), lambda i: (i, 0)),
            pl.BlockSpec((tm, HG_WIDTH), lambda i: (i, 0)),
            pl.BlockSpec((tm, LRU_WIDTH), lambda i: (i, 0)),
            pl.BlockSpec((tm, d), lambda i: (i, 0)),
            pl.BlockSpec((None,) + w_out.shape[1:], lambda i: (layer, 0, 0), pipeline_mode=pl.Buffered(1)),
            modspec(2),
            pl.BlockSpec((1, d), lambda i: (0, 0)),
            modspec(3), modspec(4),
        ],
        out_specs=[pl.BlockSpec((tm, d), lambda i: (i, 0))] + [_plane_spec(tm)] * N_PLANES,
        out_shape=[jax.ShapeDtypeStruct((n, d), F32)] + [jax.ShapeDtypeStruct((n, LANES), jnp.uint32)] * N_PLANES,
        scratch_shapes=[pltpu.VMEM(w_out.shape[1:], BF16)],
        compiler_params=_cparams("arbitrary"),
        name="outproj",
    )(y_ssd, y_hg, y_lru, x2, w_out, mod3, nw, mod3, mod3)
    return outs[0], tuple(outs[1:])


def _plane_spec(tm):
    return pl.BlockSpec((tm, LANES), lambda i: (i, 0))


def _router_kernel(h0, h1, h2, h3, rw_ref, rb_ref, eid_ref, rank_ref, wt_ref, cnt_ref, carry, wscr):
    tm = h0.shape[0]

    @pl.when(pl.program_id(0) == 0)
    def _():
        carry[...] = jnp.zeros_like(carry)

    hb = _load_planes((h0, h1, h2, h3)).astype(BF16)
    logit_t = sum(lax.dot_general(part, hb, (((1,), (1,)), ((), ())), preferred_element_type=F32)
                  for part in _split3(rw_ref[...]))
    score = _sigmoid(logit_t)
    sel = score + rb_ref[...]
    neg_inf = jnp.float32(-jnp.inf)
    io_g = lax.broadcasted_iota(jnp.int32, (E_PER_GROUP, tm), 0)
    blocks, gscore = [], []
    for g in range(N_EXPERT_GROUPS):
        blk = sel[g * E_PER_GROUP:(g + 1) * E_PER_GROUP, :]
        m1 = jnp.max(blk, axis=0, keepdims=True)
        i1 = jnp.min(jnp.where(blk == m1, io_g, E_PER_GROUP), axis=0, keepdims=True)
        m2 = jnp.max(jnp.where(io_g == i1, neg_inf, blk), axis=0, keepdims=True)
        blocks.append(blk)
        gscore.append(m1 + m2)
    masked = []
    for g in range(N_EXPERT_GROUPS):
        rank = jnp.zeros((1, tm), jnp.int32)
        for o in range(N_EXPERT_GROUPS):
            if o == g:
                continue
            beats = (gscore[o] > gscore[g]) | ((gscore[o] == gscore[g]) & (o < g))
            rank = rank + beats.astype(jnp.int32)
        masked.append(jnp.where(rank < TOPK_GROUPS, blocks[g], MASK_SCORE))
    val = jnp.concatenate(masked, axis=0)
    io_e = lax.broadcasted_iota(jnp.int32, (N_EXPERTS, tm), 0)
    chosen = jnp.zeros((N_EXPERTS, tm), jnp.bool_)
    picks = []
    for k in range(TOP_K):
        m = jnp.max(val, axis=0, keepdims=True)
        idx = jnp.min(jnp.where(val == m, io_e, N_EXPERTS), axis=0, keepdims=True)
        pick = io_e == idx
        picks.append(pick)
        eid_ref[k:k + 1, :] = idx
        chosen = chosen | pick
        val = jnp.where(pick, neg_inf, val)
    w = jnp.where(chosen, score, 0.0)
    w = w / jnp.sum(w, axis=0, keepdims=True) * ROUTED_SCALE

    chosen_f = chosen.astype(F32)
    earlier = (lax.broadcasted_iota(jnp.int32, (tm, tm), 0) < lax.broadcasted_iota(jnp.int32, (tm, tm), 1))
    before = jnp.dot(chosen_f.astype(BF16), earlier.astype(BF16), preferred_element_type=F32)
    grank = carry[:, 0:1] + before
    wscr[...] = jnp.zeros_like(wscr)
    for k in range(TOP_K):
        rank_ref[k:k + 1, :] = jnp.sum(jnp.where(picks[k], grank, 0.0), axis=0, keepdims=True).astype(jnp.int32)
        wscr[k:k + 1, :] = jnp.sum(jnp.where(picks[k], w, 0.0), axis=0, keepdims=True)
    wt_ref[...] = wscr[...].T
    carry[...] = carry[...] + jnp.sum(chosen_f, axis=1, keepdims=True)
    cnt_ref[...] = carry[...]


def _router(hp, rw_t, rb):
    n = hp[0].shape[0]
    tm = TM_ROUTER
    return pl.pallas_call(
        _router_kernel,
        grid=(n // tm,),
        in_specs=[_plane_spec(tm)] * N_PLANES + [
            pl.BlockSpec(rw_t.shape, lambda i: (0, 0)),
            pl.BlockSpec((N_EXPERTS, 1), lambda i: (0, 0)),
        ],
        out_specs=[
            pl.BlockSpec((TOP_K, tm), lambda i: (0, i)),
            pl.BlockSpec((TOP_K, tm), lambda i: (0, i)),
            pl.BlockSpec((tm, LANES), lambda i: (i, 0)),
            pl.BlockSpec((N_EXPERTS, LANES), lambda i: (0, 0)),
        ],
        out_shape=[
            jax.ShapeDtypeStruct((TOP_K, n), jnp.int32),
            jax.ShapeDtypeStruct((TOP_K, n), jnp.int32),
            jax.ShapeDtypeStruct((n, LANES), F32),
            jax.ShapeDtypeStruct((N_EXPERTS, LANES), F32),
        ],
        scratch_shapes=[pltpu.VMEM((N_EXPERTS, LANES), F32), pltpu.VMEM((LANES, tm), F32)],
        compiler_params=_cparams("arbitrary"),
        name="router",
    )(*hp, rw_t, rb)


def _plan_kernel(n_slots, cnt_ref, eid_ref, rank_ref, dest_ref, be_ref, nb_ref, pad_ref, np_ref):
    cnt = cnt_ref[...].astype(jnp.int32)
    padded = ((cnt + (MOE_BLOCK - 1)) >> MOE_BLOCK_LOG2) << MOE_BLOCK_LOG2
    ri = lax.broadcasted_iota(jnp.int32, (N_EXPERTS, N_EXPERTS), 0)
    ci = lax.broadcasted_iota(jnp.int32, (N_EXPERTS, N_EXPERTS), 1)
    pad_end = jnp.dot((ri >= ci).astype(F32), padded.astype(F32), precision=HI,
                      preferred_element_type=F32).astype(jnp.int32)
    pad_start = pad_end - padded
    eid = eid_ref[...]
    dest = rank_ref[...]
    for e in range(N_EXPERTS):
        dest = dest + jnp.where(eid == e, pad_start[e:e + 1, 0:1], 0)
    dest_ref[...] = dest
    nbp = be_ref.shape[1]
    jpos = lax.broadcasted_iota(jnp.int32, (N_EXPERTS, nbp), 1) * MOE_BLOCK
    be = jnp.minimum(jnp.sum((pad_end[:, 0:1] <= jpos).astype(jnp.int32), axis=0, keepdims=True), N_EXPERTS - 1)
    be_ref[...] = be
    nb_ref[...] = pad_end[N_EXPERTS - 1:N_EXPERTS, :] >> MOE_BLOCK_LOG2
    fill_end = pad_start + cnt
    io_e = lax.broadcasted_iota(jnp.int32, (N_EXPERTS, nbp), 0)
    end_j = jnp.sum(jnp.where(io_e == be, fill_end[:, 0:1], 0), axis=0, keepdims=True)
    pieces = (end_j - jpos[0:1, :] + (XS_PIECE - 1)) >> XS_PIECE_LOG2
    np_ref[...] = jnp.clip(pieces, 0, MOE_BLOCK // XS_PIECE)
    lane = lax.broadcasted_iota(jnp.int32, (N_EXPERTS, LANES), 1)
    piece_end = ((fill_end + (XS_PIECE - 1)) >> XS_PIECE_LOG2) << XS_PIECE_LOG2
    s = fill_end + lane
    pad_ref[...] = jnp.where(s < piece_end, s, n_slots + lane)


def _plan(cnt, eid, rank, n_slots):
    n = eid.shape[1]
    nbp = -(-(n_slots // MOE_BLOCK) // LANES) * LANES
    pad_rows = N_EXPERTS
    return pl.pallas_call(
        functools.partial(_plan_kernel, n_slots),
        grid=(1,),
        in_specs=[
            pl.BlockSpec(cnt.shape, lambda i: (0, 0)),
            pl.BlockSpec(eid.shape, lambda i: (0, 0)),
            pl.BlockSpec(rank.shape, lambda i: (0, 0)),
        ],
        out_specs=[
            pl.BlockSpec((TOP_K, n), lambda i: (0, 0)),
            pl.BlockSpec((1, nbp), lambda i: (0, 0)),
            pl.BlockSpec((1, LANES), lambda i: (0, 0)),
            pl.BlockSpec((pad_rows, LANES), lambda i: (0, 0)),
            pl.BlockSpec((1, nbp), lambda i: (0, 0)),
        ],
        out_shape=[
            jax.ShapeDtypeStruct((TOP_K, n), jnp.int32),
            jax.ShapeDtypeStruct((1, nbp), jnp.int32),
            jax.ShapeDtypeStruct((1, LANES), jnp.int32),
            jax.ShapeDtypeStruct((pad_rows, LANES), jnp.int32),
            jax.ShapeDtypeStruct((1, nbp), jnp.int32),
        ],
        compiler_params=_cparams("arbitrary"),
        name="moe_plan",
    )(cnt, eid, rank)


def _sc_mesh():
    return plsc.VectorSubcoreMesh(core_axis_name="c", subcore_axis_name="s")


def _sc_worker():
    return lax.axis_index("c") * SC_SUBCORES + lax.axis_index("s")


def _sc_dispatch(hp, dest_rows, pad_rows, n_rows):
    n = hp[0].shape[0]
    tiles_per_worker = n // SC_WIN // SC_WORKERS
    pad_per_worker = pad_rows.shape[0] // SC_WORKERS
    zeros = jnp.zeros((SC_WIN, LANES), jnp.uint32)

    def body(*refs):
        h = refs[:N_PLANES]
        dest_hbm, pad_hbm, z_hbm = refs[N_PLANES:N_PLANES + 3]
        xs = refs[N_PLANES + 3:2 * N_PLANES + 3]
        bufs = refs[2 * N_PLANES + 3:3 * N_PLANES + 3]
        ibuf, pbuf, sem = refs[3 * N_PLANES + 3:]
        wid = _sc_worker()

        pltpu.sync_copy(z_hbm, bufs[0])
        pltpu.sync_copy(pad_hbm.at[pl.ds(wid * pad_per_worker, pad_per_worker)], pbuf)
        copies = [pltpu.async_copy(bufs[0], xs[c].at[pbuf.at[r]], sem)
                  for r in range(pad_per_worker) for c in range(N_PLANES)]
        for cp in copies:
            cp.wait()

        @pl.loop(0, tiles_per_worker)
        def _(i):
            tile = wid * tiles_per_worker + i
            pltpu.sync_copy(dest_hbm.at[pl.ds(tile * TOP_K, TOP_K)], ibuf)
            for c in range(N_PLANES):
                pltpu.sync_copy(h[c].at[pl.ds(tile * SC_WIN, SC_WIN)], bufs[c])
            scatters = [pltpu.async_copy(bufs[c], xs[c].at[ibuf.at[k]], sem)
                        for c in range(N_PLANES) for k in range(TOP_K)]
            for cp in scatters:
                cp.wait()

    out_type = tuple(jax.ShapeDtypeStruct((n_rows, LANES), jnp.uint32) for _ in range(N_PLANES))
    scratch = ([pltpu.VMEM((SC_WIN, LANES), jnp.uint32)] * N_PLANES
               + [pltpu.VMEM((TOP_K, LANES), jnp.int32), pltpu.VMEM((pad_per_worker, LANES), jnp.int32),
                  pltpu.SemaphoreType.DMA])
    return pl.kernel(body, out_type=out_type, mesh=_sc_mesh(), scratch_types=scratch,
                     name="moe_sc_dispatch")(*hp, dest_rows, pad_rows, zeros)


def _sc_gather(ysp, dest_rows, n):
    tiles_per_worker = n // SC_WIN // SC_WORKERS

    def body(*refs):
        ys = refs[:N_PLANES]
        dest_hbm, g_hbm = refs[N_PLANES:N_PLANES + 2]
        bufs = refs[N_PLANES + 2:N_PLANES + 2 + SC_GATHER_BUFS]
        ibuf, sem = refs[N_PLANES + 2 + SC_GATHER_BUFS:]
        wid = _sc_worker()

        @pl.loop(0, tiles_per_worker)
        def _(i):
            tile = wid * tiles_per_worker + i
            pltpu.sync_copy(dest_hbm.at[pl.ds(tile * TOP_K, TOP_K)], ibuf)
            for c in range(N_PLANES):
                for k0 in range(0, TOP_K, SC_GATHER_BUFS):
                    gathers = [pltpu.async_copy(ys[c].at[ibuf.at[k0 + j]], bufs[j], sem)
                               for j in range(SC_GATHER_BUFS)]
                    for cp in gathers:
                        cp.wait()
                    stores = [pltpu.async_copy(
                        bufs[j], g_hbm.at[pl.ds(((k0 + j) * N_PLANES + c) * n + tile * SC_WIN, SC_WIN)], sem)
                        for j in range(SC_GATHER_BUFS)]
                    for cp in stores:
                        cp.wait()

    scratch = ([pltpu.VMEM((SC_WIN, LANES), jnp.uint32)] * SC_GATHER_BUFS
               + [pltpu.VMEM((TOP_K, LANES), jnp.int32), pltpu.SemaphoreType.DMA])
    return pl.kernel(body, out_type=jax.ShapeDtypeStruct((TOP_K * N_PLANES * n, LANES), jnp.uint32),
                     mesh=_sc_mesh(), scratch_types=scratch, name="moe_sc_gather")(*ysp, dest_rows)


def _expert_kernel(be_ref, nb_ref, np_ref, *refs):
    xs_hbm = refs[:N_PLANES]
    wg_ref, wu_ref, wd_ref = refs[N_PLANES:N_PLANES + 3]
    ys_refs = refs[N_PLANES + 3:2 * N_PLANES + 3]
    wg_b, wu_b, wd_b, ring, sems = refs[2 * N_PLANES + 3:]
    j = pl.program_id(0)
    nb = nb_ref[0]
    used = j < nb
    new_expert = (j == 0) | (be_ref[j] != be_ref[jnp.maximum(j - 1, 0)])

    def fetch(step, slot, wait):
        for p in range(MOE_BLOCK // XS_PIECE):
            @pl.when(p < np_ref[step])
            def _():
                src = pl.ds(pl.multiple_of(step * MOE_BLOCK + p * XS_PIECE, XS_PIECE), XS_PIECE)
                for c in range(N_PLANES):
                    cp = pltpu.make_async_copy(xs_hbm[c].at[src], ring.at[slot, c, pl.ds(p * XS_PIECE, XS_PIECE)],
                                               sems.at[slot])
                    if wait:
                        cp.wait()
                    else:
                        cp.start()

    @pl.when(j == 0)
    def _():
        ring[...] = jnp.zeros_like(ring)

    for s in range(XS_RING - 1):
        @pl.when((j == 0) & (s < nb))
        def _():
            fetch(s, s, wait=False)

    ahead = j + (XS_RING - 1)

    @pl.when(used & (ahead < nb))
    def _():
        fetch(ahead, lax.rem(ahead, XS_RING), wait=False)

    @pl.when(used & new_expert)
    def _():
        wg_b[...] = wg_ref[...].astype(BF16)
        wu_b[...] = wu_ref[...].astype(BF16)
        wd_b[...] = wd_ref[...].astype(BF16)

    @pl.when(used)
    def _():
        slot = lax.rem(j, XS_RING)
        fetch(j, slot, wait=True)
        x = _unpack_bf16_pairs(jnp.concatenate([ring[slot, c] for c in range(N_PLANES)], axis=1)).astype(BF16)
        a = jnp.dot(x, wg_b[...], preferred_element_type=F32)
        u = jnp.dot(x, wu_b[...], preferred_element_type=F32)
        y = jnp.dot((_silu(a) * u).astype(BF16), wd_b[...], preferred_element_type=F32)
        _store_planes(ys_refs, y)

    @pl.when(jnp.logical_not(used))
    def _():
        for ref in ys_refs:
            ref[...] = jnp.zeros_like(ref)


def _experts(be, nb, npieces, xsp, wg, wu, wd, layer, n_slots):
    d = wg.shape[2]

    def wspec(shape):
        def index(j, be_ref, nb_ref, np_ref):
            return layer, be_ref[jnp.minimum(j, jnp.maximum(nb_ref[0] - 1, 0))], 0, 0
        return pl.BlockSpec((None, None) + shape, index)

    grid_spec = pltpu.PrefetchScalarGridSpec(
        num_scalar_prefetch=3,
        grid=(n_slots // MOE_BLOCK,),
        in_specs=[pl.BlockSpec(memory_space=pl.ANY)] * N_PLANES
        + [wspec((d, D_EXPERT)), wspec((d, D_EXPERT)), wspec((D_EXPERT, d))],
        out_specs=[pl.BlockSpec((MOE_BLOCK, LANES), lambda j, be_ref, nb_ref, np_ref: (j, 0))] * N_PLANES,
        scratch_shapes=[pltpu.VMEM((d, D_EXPERT), BF16), pltpu.VMEM((d, D_EXPERT), BF16),
                        pltpu.VMEM((D_EXPERT, d), BF16),
                        pltpu.VMEM((XS_RING, N_PLANES, MOE_BLOCK, LANES), jnp.uint32),
                        pltpu.SemaphoreType.DMA((XS_RING,))],
    )
    return pl.pallas_call(
        _expert_kernel,
        grid_spec=grid_spec,
        out_shape=[jax.ShapeDtypeStruct((n_slots, LANES), jnp.uint32)] * N_PLANES,
        compiler_params=_cparams("arbitrary"),
        name="moe_experts",
    )(be, nb, npieces, *xsp, wg, wu, wd)


def _combine_kernel(final, g_ref, wt_ref, h0, h1, h2, h3, sg_ref, su_ref, sd_ref, x_ref, gate_ref, fw_ref, o_ref,
                    sg_b, su_b, sd_b):
    @pl.when(pl.program_id(0) == 0)
    def _():
        sg_b[...] = sg_ref[...].astype(BF16)
        su_b[...] = su_ref[...].astype(BF16)
        sd_b[...] = sd_ref[...].astype(BF16)

    hb = _load_planes((h0, h1, h2, h3)).astype(BF16)
    a = jnp.dot(hb, sg_b[...], preferred_element_type=F32)
    u = jnp.dot(hb, su_b[...], preferred_element_type=F32)
    acc = jnp.dot((_silu(a) * u).astype(BF16), sd_b[...], preferred_element_type=F32)
    wt = wt_ref[...]
    for k in range(TOP_K):
        rows = _unpack_bf16_pairs(jnp.concatenate([g_ref[k * N_PLANES + c] for c in range(N_PLANES)], axis=1))
        acc = acc + wt[:, k:k + 1] * rows
    xn = x_ref[...] + gate_ref[...] * acc
    if final:
        ms = jnp.mean(xn * xn, axis=-1, keepdims=True)
        xn = xn * lax.rsqrt(ms + EPS) * fw_ref[...]
    o_ref[...] = xn


def _combine(g, wt, hp, sg, su, sd, layer, x2, mod3, fw, seq, final):
    n, d = x2.shape
    tm = TM_COMBINE
    tpb = seq // tm

    def wspec(w):
        return pl.BlockSpec((None,) + w.shape[1:], lambda i: (layer, 0, 0), pipeline_mode=pl.Buffered(1))

    return pl.pallas_call(
        functools.partial(_combine_kernel, final),
        grid=(n // tm,),
        in_specs=[
            pl.BlockSpec((TOP_K * N_PLANES, tm, LANES), lambda i: (0, i, 0)),
            pl.BlockSpec((tm, LANES), lambda i: (i, 0)),
        ] + [_plane_spec(tm)] * N_PLANES + [
            wspec(sg), wspec(su), wspec(sd),
            pl.BlockSpec((tm, d), lambda i: (i, 0)),
            pl.BlockSpec((None, 1, d), lambda i: ((i // tpb) * N_ADA + 5, 0, 0)),
            pl.BlockSpec((1, d), lambda i: (0, 0)),
        ],
        out_specs=pl.BlockSpec((tm, d), lambda i: (i, 0)),
        out_shape=jax.ShapeDtypeStruct((n, d), F32),
        scratch_shapes=[pltpu.VMEM(sg.shape[1:], BF16), pltpu.VMEM(su.shape[1:], BF16),
                        pltpu.VMEM(sd.shape[1:], BF16)],
        compiler_params=_cparams("arbitrary"),
        name="moe_combine",
    )(g.reshape(TOP_K * N_PLANES, n, LANES), wt, *hp, sg, su, sd, x2, mod3, fw)


def _blockdiag_pairs(w):
    z = jnp.zeros((LRU_BLOCK_W, LRU_BLOCK_W), w.dtype)
    tiles = []
    for j in range(LRU_BLOCKS // 2):
        top = jnp.concatenate([w[2 * j], z], axis=1)
        bot = jnp.concatenate([z, w[2 * j + 1]], axis=1)
        tiles.append(jnp.concatenate([top, bot], axis=0))
    return jnp.stack(tiles).astype(BF16)


def _hg_level_masks():
    ch = HG_CHUNK
    msk = np.zeros((HG_LEVELS, ch, ch), np.float32)
    for lvl in range(HG_LEVELS):
        half = 1 << lvl
        for t in range(ch):
            base = (t // (2 * half)) * (2 * half)
            if (t // half) % 2 == 1:
                msk[lvl, t, base:base + half] = 1.0
    later = ((np.arange(ch)[None, :, None] >> np.arange(HG_LEVELS)[:, None, None]) & 1).astype(np.float32)
    later = np.broadcast_to(later, (HG_LEVELS, ch, HG_EXPAND))
    sgn = (2.0 * later - 1.0) * np.float32(LOG2E)
    return jnp.asarray(msk), jnp.asarray(later), jnp.asarray(sgn, dtype=F32)


def _ssd_expand():
    e = np.zeros((LANES, SSD_INNER), np.float32)
    for h in range(SSD_HEADS):
        e[h, h * SSD_HEADDIM:(h + 1) * SSD_HEADDIM] = 1.0
    return jnp.asarray(np.concatenate([e] * 3, axis=0), dtype=BF16)


def _pad_lanes(v, width):
    return jnp.pad(v, (0, width - v.shape[0])).reshape(1, width)


def _layer_params(l, w_in, ssd_conv_w, ssd_conv_b, ssd_dt_bias, ssd_a_log, ssd_d, ssd_norm_w, hg_lower_bounds,
                  hg_norm_w, lru_conv_w, lru_conv_b, lru_wa, lru_ba, lru_wx, lru_bx, lru_lambda, lru_norm_w):
    wi = w_in[l]
    dt0 = SSD_INNER + SSD_INNER + 2 * SSD_GROUPS * SSD_STATE
    w_cat = jnp.concatenate([wi[:, :dt0], wi[:, dt0 + SSD_HEADS:], wi[:, dt0:dt0 + SSD_HEADS]], axis=1)
    w_cat = jnp.pad(w_cat, ((0, 0), (0, U_WIDTH - w_cat.shape[1]))).astype(BF16)
    msk, later, sgn = _hg_level_masks()
    return dict(
        w_cat=w_cat,
        cwx=ssd_conv_w[l][:, :SSD_INNER], cbx=ssd_conv_b[l][:SSD_INNER].reshape(1, -1),
        cwb=ssd_conv_w[l][:, SSD_INNER:], cbb=ssd_conv_b[l][SSD_INNER:].reshape(1, -1),
        dtb=_pad_lanes(ssd_dt_bias[l], LANES), alog=_pad_lanes(ssd_a_log[l], LANES),
        dful=jnp.repeat(ssd_d[l], SSD_HEADDIM).reshape(1, -1), ssd_nw=ssd_norm_w[l].reshape(1, -1),
        expand=_ssd_expand(),
        hg_lb=hg_lower_bounds, hg_nw=hg_norm_w[l].reshape(1, -1), hg_msk=msk, hg_later=later, hg_sgn=sgn,
        lru_cw=lru_conv_w[l], lru_cb=lru_conv_b[l].reshape(1, -1),
        lru_wa=_blockdiag_pairs(lru_wa[l]), lru_ba=lru_ba[l].reshape(1, -1),
        lru_wx=_blockdiag_pairs(lru_wx[l]), lru_bx=lru_bx[l].reshape(1, -1),
        lru_lam=lru_lambda[l].reshape(1, -1), lru_nw=lru_norm_w[l].reshape(1, -1),
    )


def kernel(x, c, ada_w, ada_b, norm_mix_w, norm_ffn_w, w_in, ssd_conv_w, ssd_conv_b, ssd_dt_bias, ssd_a_log, ssd_d, ssd_norm_w, hg_lower_bounds, hg_norm_w, lru_conv_w, lru_conv_b, lru_wa, lru_ba, lru_wx, lru_bx, lru_lambda, lru_norm_w, w_out, router_w, router_bias, exp_gate, exp_up, exp_down, sh_gate, sh_up, sh_down, final_norm_w):
    bsz, seq, d = x.shape
    depth = ada_w.shape[0]
    assert d == D_MODEL and seq % TM_OUTPROJ == 0 and seq % SSD_CHUNK == 0 and seq % R_HG == 0
    n = bsz * seq
    assert n % (SC_WIN * SC_WORKERS) == 0
    n_slots = n * TOP_K + N_EXPERTS * MOE_BLOCK
    x2 = x.reshape(n, d)
    mod = _adaln(c, ada_w, ada_b)
    fw = final_norm_w.reshape(1, d)
    for l in range(depth):
        p = _layer_params(l, w_in, ssd_conv_w, ssd_conv_b, ssd_dt_bias, ssd_a_log, ssd_d, ssd_norm_w,
                          hg_lower_bounds, hg_norm_w, lru_conv_w, lru_conv_b, lru_wa, lru_ba, lru_wx, lru_bx,
                          lru_lambda, lru_norm_w)
        mod3 = mod[l].reshape(bsz * N_ADA, 1, d)
        u = _inproj(x2, norm_mix_w[l].reshape(1, d), mod3, p["w_cat"], seq)
        y_ssd = _ssd(u, p, bsz, seq)
        y_hg, y_lru = _hgrn2_lru(u, p, l, bsz, seq)
        x2, hp = _outproj(y_ssd, y_hg, y_lru, x2, w_out, l, norm_ffn_w[l].reshape(1, d), mod3, seq)
        eid, rank, wt, cnt = _router(hp, router_w[l].T, router_bias[l].reshape(N_EXPERTS, 1))
        dest, be, nb, pad_rows, npieces = _plan(cnt, eid, rank, n_slots)
        dest_rows = dest.reshape(TOP_K, n // LANES, LANES).transpose(1, 0, 2).reshape(n // LANES * TOP_K, LANES)
        xsp = _sc_dispatch(hp, dest_rows, pad_rows, n_slots + LANES)
        ysp = _experts(be.reshape(-1), nb[0, :1], npieces.reshape(-1), xsp, exp_gate, exp_up, exp_down, l, n_slots)
        g = _sc_gather(ysp, dest_rows, n)
        x2 = _combine(g, wt, hp, sh_gate, sh_up, sh_down, l, x2, mod3, fw, seq, final=(l == depth - 1))
    return x2.reshape(bsz, seq, d)
```

```python
import functools

import jax
import jax.numpy as jnp
import numpy as np
from jax import lax
from jax.experimental import pallas as pl
from jax.experimental.pallas import tpu as pltpu
from jax.experimental.pallas import tpu_sc as plsc

F32 = jnp.float32
BF16 = jnp.bfloat16
HI = lax.Precision.HIGHEST
F32_TINY = float(np.finfo(np.float32).tiny)
LOG2E = float(np.log2(np.e))

LANES = 128
SUBLANES = 8
VMEM_LIMIT_BYTES = 56 * 1024 * 1024

D_MODEL = 1024
EPS = 1e-6
N_ADA = 6
CONV_WIDTH = 4
SSD_INNER = 1024
SSD_HEADDIM = 64
SSD_HEADS = 16
SSD_GROUPS = 2
SSD_STATE = 128
SSD_CHUNK = 128
SSD_GROUP_W = SSD_INNER // SSD_GROUPS
HG_WIDTH = 512
HG_EXPAND = 128
HG_HEADS = 4
HG_CHUNK = 128
HG_LEVELS = 7
LRU_WIDTH = 512
LRU_BLOCKS = 8
LRU_BLOCK_W = 64
LRU_C = 8.0
N_EXPERTS = 64
TOP_K = 8
N_EXPERT_GROUPS = 8
E_PER_GROUP = 8
TOPK_GROUPS = 4
D_EXPERT = 256
ROUTED_SCALE = 2.5
MASK_SCORE = -1.0e4

COL_Z = 0
COL_XS = 1024
COL_BC = 2048
COL_HQ = 2560
COL_HF = 3072
COL_HV = 3584
COL_HG = 4096
COL_LG = 4608
COL_LX = 5120
COL_DT = 5632
U_WIDTH = 5760

TM_INPROJ = 512
TM_OUTPROJ = 1024
TM_ROUTER = 512
TM_COMBINE = 512
SC_CORES = 2
SC_SUBCORES = 16
SC_WORKERS = SC_CORES * SC_SUBCORES
SC_WIN = LANES
SC_GATHER_BUFS = 4
MOE_BLOCK_LOG2 = 10
MOE_BLOCK = 1 << MOE_BLOCK_LOG2
XS_RING = 3
XS_PIECE_LOG2 = 7
XS_PIECE = 1 << XS_PIECE_LOG2
R_HG = 512


def _cparams(*sem):
    return pltpu.CompilerParams(dimension_semantics=sem, vmem_limit_bytes=VMEM_LIMIT_BYTES)


def _sigmoid(x):
    return 0.5 * jnp.tanh(0.5 * x) + 0.5


def _silu(x):
    h = 0.5 * x
    return h + h * jnp.tanh(h)


def _split3(x):
    x1 = x.astype(BF16)
    r1 = x - x1.astype(F32)
    x2 = r1.astype(BF16)
    x3 = (r1 - x2.astype(F32)).astype(BF16)
    return x1, x2, x3


def _softplus(x):
    return jnp.maximum(x, 0.0) + jnp.log1p(jnp.exp(-jnp.abs(x)))


def _norm_mod(x, nw, shift, scale):
    ms = jnp.mean(x * x, axis=-1, keepdims=True)
    y = x * lax.rsqrt(ms + EPS) * nw
    return y * (1.0 + scale) + shift


_HI16 = np.uint32(0xFFFF0000)


def _pack_bf16_pairs(x):
    half = x.shape[1] // 2
    bits = lax.bitcast_convert_type(x.astype(BF16).astype(F32), jnp.uint32)
    return (bits[:, :half] & _HI16) | (bits[:, half:] >> 16)


def _unpack_bf16_pairs(w):
    hi = lax.bitcast_convert_type(w & _HI16, F32)
    lo = lax.bitcast_convert_type(w << 16, F32)
    return jnp.concatenate([hi, lo], axis=1)


N_PLANES = D_MODEL // 2 // LANES


def _store_planes(refs, x):
    packed = _pack_bf16_pairs(x)
    for c, ref in enumerate(refs):
        ref[...] = packed[:, c * LANES:(c + 1) * LANES]


def _load_planes(refs):
    return _unpack_bf16_pairs(jnp.concatenate([ref[...] for ref in refs], axis=1))


def _ada_kernel(c_ref, w_ref, b_ref, o_ref):
    c = c_ref[...]
    o_ref[...] = jnp.dot(_silu(c), w_ref[...], precision=HI, preferred_element_type=F32) + b_ref[...]


def _adaln(c, ada_w, ada_b):
    depth, d, n6 = ada_w.shape
    bsz = c.shape[0]
    tn = 1536
    return pl.pallas_call(
        _ada_kernel,
        grid=(depth, n6 // tn),
        in_specs=[
            pl.BlockSpec((bsz, d), lambda l, j: (0, 0)),
            pl.BlockSpec((None, d, tn), lambda l, j: (l, 0, j)),
            pl.BlockSpec((None, 1, tn), lambda l, j: (l, 0, j)),
        ],
        out_specs=pl.BlockSpec((None, bsz, tn), lambda l, j: (l, 0, j)),
        out_shape=jax.ShapeDtypeStruct((depth, bsz, n6), F32),
        compiler_params=_cparams("arbitrary", "arbitrary"),
        name="adaln_mod",
    )(c, ada_w, ada_b.reshape(depth, 1, n6))


def _inproj_kernel(x_ref, nw_ref, sh_ref, sc_ref, w_ref, o_ref):
    h = _norm_mod(x_ref[...], nw_ref[...], sh_ref[...], sc_ref[...])
    o_ref[...] = jnp.dot(h.astype(BF16), w_ref[...], preferred_element_type=F32)


def _inproj(x2, nw, mod3, w_cat, layer, seq):
    n, d = x2.shape
    tm = TM_INPROJ
    tpb = seq // tm
    return pl.pallas_call(
        _inproj_kernel,
        grid=(n // tm,),
        in_specs=[
            pl.BlockSpec((tm, d), lambda i: (i, 0)),
            pl.BlockSpec((1, d), lambda i: (0, 0)),
            pl.BlockSpec((None, 1, d), lambda i: ((i // tpb) * N_ADA + 0, 0, 0)),
            pl.BlockSpec((None, 1, d), lambda i: ((i // tpb) * N_ADA + 1, 0, 0)),
            pl.BlockSpec((None, d, U_WIDTH), lambda i: (layer, 0, 0), pipeline_mode=pl.Buffered(1)),
        ],
        out_specs=pl.BlockSpec((tm, U_WIDTH), lambda i: (i, 0)),
        out_shape=jax.ShapeDtypeStruct((n, U_WIDTH), F32),
        compiler_params=_cparams("arbitrary"),
        name="inproj",
    )(x2, nw, mod3, mod3, w_cat)


def _causal_conv(cur_ref, ext, tail, cw_ref, cb_ref):
    rows, width = cur_ref.shape
    cur = cur_ref[...]
    ext[0:SUBLANES, :] = tail[...]
    ext[SUBLANES:SUBLANES + rows, :] = cur
    tail[...] = cur[rows - SUBLANES:rows, :]
    groups = ext[...].reshape(rows // SUBLANES + 1, SUBLANES, width)
    sub = lax.broadcasted_iota(jnp.int32, (1, SUBLANES, width), 1)
    acc = cb_ref[...] + cur * cw_ref[CONV_WIDTH - 1:CONV_WIDTH, :]
    for d in range(1, CONV_WIDTH):
        rot = pltpu.roll(groups, d, 1)
        back = jnp.where(sub < d, rot[:-1], rot[1:]).reshape(rows, width)
        acc = acc + back * cw_ref[CONV_WIDTH - 1 - d:CONV_WIDTH - d, :]
    return acc


def _ssd_kernel(z_ref, xs_ref, bc_ref, dt_ref, cwx_ref, cbx_ref, cwb_ref, cbb_ref, dtb_ref, alog_ref,
                dful_ref, nw_ref, e_ref, o_ref, extx, extb, tailx, tailb, hstate):
    c = pl.program_id(1)
    q = SSD_CHUNK

    @pl.when(c == 0)
    def _():
        tailx[...] = jnp.zeros_like(tailx)
        tailb[...] = jnp.zeros_like(tailb)
        hstate[...] = jnp.zeros_like(hstate)

    xs = _silu(_causal_conv(xs_ref, extx, tailx, cwx_ref, cbx_ref))
    bc = _silu(_causal_conv(bc_ref, extb, tailb, cwb_ref, cbb_ref))

    dt = _softplus(dt_ref[...] + dtb_ref[...])
    a = dt * (-jnp.exp(alog_ref[...]))
    ri = lax.broadcasted_iota(jnp.int32, (q, q), 0)
    ci = lax.broadcasted_iota(jnp.int32, (q, q), 1)
    tril = ri >= ci
    tril_f = tril.astype(F32)
    tril_b = tril.astype(BF16)
    acum = jnp.dot(jnp.concatenate([tril_b] * 3, axis=1), jnp.concatenate(_split3(a), axis=0),
                   preferred_element_type=F32)
    acum_t = acum.T
    expand3 = e_ref[...]
    dt_full = jnp.dot(jnp.concatenate(_split3(dt), axis=1), expand3, preferred_element_type=F32)
    acum_full = jnp.dot(jnp.concatenate(_split3(acum), axis=1), expand3, preferred_element_type=F32)
    alast_full = acum_full[q - 1:q, :]

    xdt = xs * dt_full
    xdt_b = xdt.astype(BF16)
    exp_a = jnp.exp(acum_full)
    xd_b = (xdt * jnp.exp(alast_full - acum_full)).astype(BF16)
    state_decay = jnp.exp(alast_full)
    left = lax.broadcasted_iota(jnp.int32, (q, LANES), 1) < SSD_HEADDIM
    zero_b = jnp.zeros((q, LANES), BF16)

    ys = []
    for g in range(SSD_GROUPS):
        b_g = bc[:, g * SSD_STATE:(g + 1) * SSD_STATE]
        c_g = bc[:, (SSD_GROUPS + g) * SSD_STATE:(SSD_GROUPS + g + 1) * SSD_STATE]
        c_b = c_g.astype(BF16)
        cb = lax.dot_general(c_b, b_g.astype(BF16), (((1,), (1,)), ((), ())), preferred_element_type=F32)
        cb = cb * tril_f
        cs = slice(g * SSD_GROUP_W, (g + 1) * SSD_GROUP_W)
        h_g = hstate[:, cs]
        y_off = jnp.dot(c_b, h_g.astype(BF16), preferred_element_type=F32) * exp_a[:, cs]
        pieces = []
        for pr in range(SSD_HEADS // SSD_GROUPS // 2):
            h0 = g * (SSD_HEADS // SSD_GROUPS) + 2 * pr
            ms = []
            for h in (h0, h0 + 1):
                col = acum[:, h:h + 1]
                row = acum_t[h:h + 1, :]
                ms.append((cb * jnp.exp(jnp.minimum(col - row, 0.0))).astype(BF16))
            lhs = jnp.concatenate(ms, axis=1)
            xp = xdt_b[:, h0 * SSD_HEADDIM:(h0 + 2) * SSD_HEADDIM]
            rhs = jnp.concatenate([jnp.where(left, xp, zero_b), jnp.where(left, zero_b, xp)], axis=0)
            pieces.append(jnp.dot(lhs, rhs, preferred_element_type=F32))
        ys.append(jnp.concatenate(pieces, axis=1) + y_off)
        b_t = b_g.T.astype(BF16)
        hstate[:, cs] = h_g * state_decay[:, cs] + jnp.dot(b_t, xd_b[:, cs], preferred_element_type=F32)

    y = jnp.concatenate(ys, axis=1) + xs * dful_ref[...]
    y = y * _silu(z_ref[...])
    outs = []
    for g in range(SSD_GROUPS):
        cs = slice(g * SSD_GROUP_W, (g + 1) * SSD_GROUP_W)
        yg = y[:, cs]
        ms = jnp.mean(yg * yg, axis=-1, keepdims=True)
        outs.append(yg * lax.rsqrt(ms + EPS) * nw_ref[:, cs])
    o_ref[...] = jnp.concatenate(outs, axis=1).astype(o_ref.dtype)


def _ssd(u, p, bsz, seq):
    q = SSD_CHUNK
    nc = seq // q
    n = bsz * seq

    def rows(b, c):
        return b * nc + c

    def const(shape):
        return pl.BlockSpec(shape, lambda b, c: (0,) * len(shape))

    return pl.pallas_call(
        _ssd_kernel,
        grid=(bsz, nc),
        in_specs=[
            pl.BlockSpec((q, SSD_INNER), lambda b, c: (rows(b, c), COL_Z // SSD_INNER)),
            pl.BlockSpec((q, SSD_INNER), lambda b, c: (rows(b, c), COL_XS // SSD_INNER)),
            pl.BlockSpec((q, 512), lambda b, c: (rows(b, c), COL_BC // 512)),
            pl.BlockSpec((q, LANES), lambda b, c: (rows(b, c), COL_DT // LANES)),
            const((CONV_WIDTH, SSD_INNER)), const((1, SSD_INNER)),
            const((CONV_WIDTH, 512)), const((1, 512)),
            const((1, LANES)), const((1, LANES)),
            const((1, SSD_INNER)), const((1, SSD_INNER)),
            const((3 * LANES, SSD_INNER)),
        ],
        out_specs=pl.BlockSpec((q, SSD_INNER), lambda b, c: (rows(b, c), 0)),
        out_shape=jax.ShapeDtypeStruct((n, SSD_INNER), BF16),
        scratch_shapes=[
            pltpu.VMEM((q + SUBLANES, SSD_INNER), F32),
            pltpu.VMEM((q + SUBLANES, 512), F32),
            pltpu.VMEM((SUBLANES, SSD_INNER), F32),
            pltpu.VMEM((SUBLANES, 512), F32),
            pltpu.VMEM((SSD_STATE, SSD_INNER), F32),
        ],
        compiler_params=_cparams("arbitrary", "arbitrary"),
        name="ssd_mixer",
    )(u, u, u, u, p["cwx"], p["cbx"], p["cwb"], p["cbb"], p["dtb"], p["alog"], p["dful"], p["ssd_nw"], p["expand"])


def _boundary_rows(b, lvl):
    half = 1 << lvl
    parts = []
    for v in range(b.shape[0] // SUBLANES):
        r0 = v * SUBLANES
        if 2 * half >= SUBLANES:
            src = (r0 // (2 * half)) * (2 * half) + half - 1
            parts.append(jnp.broadcast_to(b[src:src + 1, :], (SUBLANES, b.shape[1])))
        else:
            sub = lax.broadcasted_iota(jnp.int32, (SUBLANES, b.shape[1]), 0)
            piece = None
            for g in range(SUBLANES // (2 * half)):
                src = r0 + g * 2 * half + half - 1
                cand = jnp.broadcast_to(b[src:src + 1, :], (SUBLANES, b.shape[1]))
                piece = cand if piece is None else jnp.where(sub >= g * 2 * half, cand, piece)
            parts.append(piece)
    return jnp.concatenate(parts, axis=0)


def _hgrn2_lru_kernel(layer, q_ref, f_ref, v_ref, g_ref, lb_ref, nw_ref, msk_ref, half_ref, sgn_ref,
                      lg_ref, lx_ref, cw_ref, cb_ref, wa_ref, ba_ref, wx_ref, bx_ref, lam_ref, lnw_ref,
                      o_ref, ol_ref, state_t, ext, tail, hcarry):
    t = pl.program_id(1)
    ch = HG_CHUNK

    @pl.when(t == 0)
    def _():
        state_t[...] = jnp.zeros_like(state_t)
        tail[...] = jnp.zeros_like(tail)
        hcarry[...] = jnp.zeros_like(hcarry)

    _lru_tile(lg_ref, lx_ref, cw_ref, cb_ref, wa_ref, ba_ref, wx_ref, bx_ref, lam_ref, lnw_ref, ol_ref,
              ext, tail, hcarry)

    lrows = [lb_ref[j:j + 1, :] for j in range(lb_ref.shape[0])]
    mx = functools.reduce(jnp.maximum, lrows)
    es = [jnp.exp(r - mx) for r in lrows]
    den = functools.reduce(lambda a_, b_: a_ + b_, es)
    lb = jnp.zeros_like(mx)
    for j in range(1, layer + 1):
        lb = lb + es[j] / den
    one_minus_lb = 1.0 - lb
    nw = nw_ref[...]

    ri = lax.broadcasted_iota(jnp.int32, (ch, ch), 0)
    ci = lax.broadcasted_iota(jnp.int32, (ch, ch), 1)
    tril3 = jnp.concatenate([(ri >= ci).astype(BF16)] * 3, axis=1)
    rowi = lax.broadcasted_iota(jnp.int32, (ch, HG_EXPAND), 0)
    tgt = [((rowi >> lvl) & 1) == 1 for lvl in range(HG_LEVELS)]

    def head_chunk(h, rs):
        cs = slice(h * HG_EXPAND, (h + 1) * HG_EXPAND)
        qq = _silu(q_ref[rs, cs])
        kk = one_minus_lb[:, cs] * _sigmoid(-f_ref[rs, cs])
        logf = jnp.log1p(-kk)
        vv = v_ref[rs, cs]
        vb = vv.astype(BF16)
        b = jnp.dot(tril3, jnp.concatenate(_split3(logf), axis=0),
                    preferred_element_type=F32)
        st = state_t[h]
        o = lax.dot_general((qq * jnp.exp(b)).astype(BF16), st.astype(BF16), (((1,), (1,)), ((), ())),
                            preferred_element_type=F32)
        attn = jnp.zeros((ch, ch), F32)
        for lvl in range(HG_LEVELS):
            if lvl == 0:
                qe = jnp.where(tgt[0], qq * (1.0 - kk), 0.0)
                ke = jnp.where(tgt[0], 0.0, kk)
            else:
                m = _boundary_rows(b, lvl)
                later = half_ref[lvl]
                e = jnp.exp2((b - m) * sgn_ref[lvl])
                prod = jnp.where(tgt[lvl], qq, kk) * e
                qe = prod * later
                ke = prod - qe
            prod = lax.dot_general(qe.astype(BF16), ke.astype(BF16), (((1,), (1,)), ((), ())),
                                   preferred_element_type=F32)
            attn = attn + msk_ref[lvl] * prod
        diag = jnp.sum(qq * kk, axis=-1, keepdims=True)
        o = o + jnp.dot(attn.astype(BF16), vb, preferred_element_type=F32) + diag * vv
        b_last = b[ch - 1:ch, :]
        kd = (kk * jnp.exp(b_last - b)).astype(BF16)
        state_t[h] = st * jnp.exp(b_last) + jnp.dot(vv.T.astype(BF16), kd, preferred_element_type=F32)
        ms = jnp.mean(o * o, axis=-1, keepdims=True)
        y = o * lax.rsqrt(ms + EPS) * nw[:, cs]
        o_ref[rs, cs] = (y * _silu(g_ref[rs, cs])).astype(o_ref.dtype)

    for j in range(q_ref.shape[0] // ch):
        for h in range(HG_HEADS):
            head_chunk(h, slice(j * ch, (j + 1) * ch))


def _lru_tile(g_ref, x_ref, cw_ref, cb_ref, wa_ref, ba_ref, wx_ref, bx_ref, lam_ref, nw_ref, o_ref,
              ext, tail, hcarry):
    rows = x_ref.shape[0]
    xb = _causal_conv(x_ref, ext, tail, cw_ref, cb_ref)
    xbb = xb.astype(BF16)
    npair = LRU_WIDTH // LANES
    ra = jnp.concatenate([jnp.dot(xbb[:, j * LANES:(j + 1) * LANES], wa_ref[j], preferred_element_type=F32)
                          for j in range(npair)], axis=1)
    rx = jnp.concatenate([jnp.dot(xbb[:, j * LANES:(j + 1) * LANES], wx_ref[j], preferred_element_type=F32)
                          for j in range(npair)], axis=1)
    r = _sigmoid(ra + ba_ref[...])
    i = _sigmoid(rx + bx_ref[...])
    log_a = -LRU_C * r * _softplus(-lam_ref[...])
    a = jnp.exp(log_a)
    th = jnp.tanh(log_a)
    s = -2.0 * th
    root = s * lax.rsqrt(jnp.maximum(s * (1.0 - th), F32_TINY))
    u = root * (i * xb)

    ngroups = rows // SUBLANES
    sub = lax.broadcasted_iota(jnp.int32, (1, SUBLANES, LRU_WIDTH), 1)
    acc_a = a.reshape(ngroups, SUBLANES, LRU_WIDTH)
    acc_u = u.reshape(ngroups, SUBLANES, LRU_WIDTH)
    d = 1
    while d < SUBLANES:
        keep = sub >= d
        a_sh = jnp.where(keep, pltpu.roll(acc_a, d, 1), 1.0)
        u_sh = jnp.where(keep, pltpu.roll(acc_u, d, 1), 0.0)
        acc_u = acc_a * u_sh + acc_u
        acc_a = acc_a * a_sh
        d *= 2
    acc_a = acc_a.reshape(rows, LRU_WIDTH)
    acc_u = acc_u.reshape(rows, LRU_WIDTH)
    carry = hcarry[0:1, :]
    groups = []
    for g in range(rows // SUBLANES):
        gs = slice(g * SUBLANES, (g + 1) * SUBLANES)
        hg = acc_a[gs, :] * carry + acc_u[gs, :]
        groups.append(hg)
        carry = hg[SUBLANES - 1:SUBLANES, :]
    h = jnp.concatenate(groups, axis=0)
    hcarry[0:1, :] = carry

    gate = g_ref[...]
    gelu = 0.5 * gate * (1.0 + jnp.tanh(np.sqrt(2.0 / np.pi).astype(np.float32) * (gate + 0.044715 * (gate * gate * gate))))
    y = h * gelu
    ms = jnp.mean(y * y, axis=-1, keepdims=True)
    o_ref[...] = (y * lax.rsqrt(ms + EPS) * nw_ref[...]).astype(o_ref.dtype)


def _hgrn2_lru(u, p, layer, bsz, seq):
    r = R_HG
    nt = seq // r
    n = bsz * seq
    npair = LRU_WIDTH // LANES

    def col(base, width):
        return pl.BlockSpec((r, width), lambda b, t: (b * nt + t, base // width))

    def const(shape):
        return pl.BlockSpec(shape, lambda b, t: (0,) * len(shape))

    def out(width):
        return pl.BlockSpec((r, width), lambda b, t: (b * nt + t, 0))

    return pl.pallas_call(
        functools.partial(_hgrn2_lru_kernel, layer),
        grid=(bsz, nt),
        in_specs=[
            col(COL_HQ, HG_WIDTH), col(COL_HF, HG_WIDTH), col(COL_HV, HG_WIDTH), col(COL_HG, HG_WIDTH),
            const(p["hg_lb"].shape), const((1, HG_WIDTH)), const((HG_LEVELS, HG_CHUNK, HG_CHUNK)),
            const((HG_LEVELS, HG_CHUNK, HG_EXPAND)), const((HG_LEVELS, HG_CHUNK, HG_EXPAND)),
            col(COL_LG, LRU_WIDTH), col(COL_LX, LRU_WIDTH),
            const((CONV_WIDTH, LRU_WIDTH)), const((1, LRU_WIDTH)),
            const((npair, LANES, LANES)), const((1, LRU_WIDTH)),
            const((npair, LANES, LANES)), const((1, LRU_WIDTH)),
            const((1, LRU_WIDTH)), const((1, LRU_WIDTH)),
        ],
        out_specs=[out(HG_WIDTH), out(LRU_WIDTH)],
        out_shape=[jax.ShapeDtypeStruct((n, HG_WIDTH), BF16), jax.ShapeDtypeStruct((n, LRU_WIDTH), BF16)],
        scratch_shapes=[
            pltpu.VMEM((HG_HEADS, HG_EXPAND, HG_EXPAND), F32),
            pltpu.VMEM((r + SUBLANES, LRU_WIDTH), F32),
            pltpu.VMEM((SUBLANES, LRU_WIDTH), F32),
            pltpu.VMEM((SUBLANES, LRU_WIDTH), F32),
        ],
        compiler_params=_cparams("arbitrary", "arbitrary"),
        name="hgrn2_lru_mixer",
    )(u, u, u, u, p["hg_lb"], p["hg_nw"], p["hg_msk"], p["hg_later"], p["hg_sgn"],
      u, u, p["lru_cw"], p["lru_cb"], p["lru_wa"], p["lru_ba"], p["lru_wx"], p["lru_bx"], p["lru_lam"], p["lru_nw"])


def _outproj_kernel(ys_ref, yh_ref, yl_ref, x_ref, w_ref, g_ref, nw_ref, sh_ref, sc_ref, xo_ref, h0, h1, h2, h3,
                    w_b):
    @pl.when(pl.program_id(0) == 0)
    def _():
        w_b[...] = w_ref[...].astype(BF16)

    acc = jnp.dot(ys_ref[...], w_b[0:SSD_INNER, :], preferred_element_type=F32)
    acc = acc + jnp.dot(yh_ref[...], w_b[SSD_INNER:SSD_INNER + HG_WIDTH, :], preferred_element_type=F32)
    acc = acc + jnp.dot(yl_ref[...], w_b[SSD_INNER + HG_WIDTH:, :], preferred_element_type=F32)
    xn = x_ref[...] + g_ref[...] * acc
    xo_ref[...] = xn
    _store_planes((h0, h1, h2, h3), _norm_mod(xn, nw_ref[...], sh_ref[...], sc_ref[...]))


def _outproj(y_ssd, y_hg, y_lru, x2, w_out, layer, nw, mod3, seq):
    n, d = x2.shape
    tm = TM_OUTPROJ
    tpb = seq // tm

    def modspec(j):
        return pl.BlockSpec((None, 1, d), lambda i: ((i // tpb) * N_ADA + j, 0, 0))

    outs = pl.pallas_call(
        _outproj_kernel,
        grid=(n // tm,),
        in_specs=[
            pl.BlockSpec((tm, SSD_INNER), lambda i: (i, 0)),
            pl.BlockSpec((tm, HG_WIDTH), lambda i: (i, 0)),
            pl.BlockSpec((tm, LRU_WIDTH), lambda i: (i, 0)),
            pl.BlockSpec((tm, d), lambda i: (i, 0)),
            pl.BlockSpec((None,) + w_out.shape[1:], lambda i: (layer, 0, 0), pipeline_mode=pl.Buffered(1)),
            modspec(2),
            pl.BlockSpec((1, d), lambda i: (0, 0)),
            modspec(3), modspec(4),
        ],
        out_specs=[pl.BlockSpec((tm, d), lambda i: (i, 0))] + [_plane_spec(tm)] * N_PLANES,
        out_shape=[jax.ShapeDtypeStruct((n, d), F32)] + [jax.ShapeDtypeStruct((n, LANES), jnp.uint32)] * N_PLANES,
        scratch_shapes=[pltpu.VMEM(w_out.shape[1:], BF16)],
        compiler_params=_cparams("arbitrary"),
        name="outproj",
    )(y_ssd, y_hg, y_lru, x2, w_out, mod3, nw, mod3, mod3)
    return outs[0], tuple(outs[1:])


def _plane_spec(tm):
    return pl.BlockSpec((tm, LANES), lambda i: (i, 0))


def _router_kernel(h0, h1, h2, h3, rw_ref, rb_ref, eid_ref, rank_ref, wt_ref, cnt_ref, carry, wscr):
    tm = h0.shape[0]

    @pl.when(pl.program_id(0) == 0)
    def _():
        carry[...] = jnp.zeros_like(carry)

    hb = _load_planes((h0, h1, h2, h3)).astype(BF16)
    logit_t = sum(lax.dot_general(part, hb, (((1,), (1,)), ((), ())), preferred_element_type=F32)
                  for part in _split3(rw_ref[...]))
    score = _sigmoid(logit_t)
    sel = score + rb_ref[...]
    neg_inf = jnp.float32(-jnp.inf)
    io_g = lax.broadcasted_iota(jnp.int32, (E_PER_GROUP, tm), 0)
    blocks, gscore = [], []
    for g in range(N_EXPERT_GROUPS):
        blk = sel[g * E_PER_GROUP:(g + 1) * E_PER_GROUP, :]
        m1 = jnp.max(blk, axis=0, keepdims=True)
        i1 = jnp.min(jnp.where(blk == m1, io_g, E_PER_GROUP), axis=0, keepdims=True)
        m2 = jnp.max(jnp.where(io_g == i1, neg_inf, blk), axis=0, keepdims=True)
        blocks.append(blk)
        gscore.append(m1 + m2)
    masked = []
    for g in range(N_EXPERT_GROUPS):
        rank = jnp.zeros((1, tm), jnp.int32)
        for o in range(N_EXPERT_GROUPS):
            if o == g:
                continue
            beats = (gscore[o] > gscore[g]) | ((gscore[o] == gscore[g]) & (o < g))
            rank = rank + beats.astype(jnp.int32)
        masked.append(jnp.where(rank < TOPK_GROUPS, blocks[g], MASK_SCORE))
    val = jnp.concatenate(masked, axis=0)
    io_e = lax.broadcasted_iota(jnp.int32, (N_EXPERTS, tm), 0)
    chosen = jnp.zeros((N_EXPERTS, tm), jnp.bool_)
    picks = []
    for k in range(TOP_K):
        m = jnp.max(val, axis=0, keepdims=True)
        idx = jnp.min(jnp.where(val == m, io_e, N_EXPERTS), axis=0, keepdims=True)
        pick = io_e == idx
        picks.append(pick)
        eid_ref[k:k + 1, :] = idx
        chosen = chosen | pick
        val = jnp.where(pick, neg_inf, val)
    w = jnp.where(chosen, score, 0.0)
    w = w / jnp.sum(w, axis=0, keepdims=True) * ROUTED_SCALE

    chosen_f = chosen.astype(F32)
    earlier = (lax.broadcasted_iota(jnp.int32, (tm, tm), 0) < lax.broadcasted_iota(jnp.int32, (tm, tm), 1))
    before = jnp.dot(chosen_f.astype(BF16), earlier.astype(BF16), preferred_element_type=F32)
    grank = carry[:, 0:1] + before
    wscr[...] = jnp.zeros_like(wscr)
    for k in range(TOP_K):
        rank_ref[k:k + 1, :] = jnp.sum(jnp.where(picks[k], grank, 0.0), axis=0, keepdims=True).astype(jnp.int32)
        wscr[k:k + 1, :] = jnp.sum(jnp.where(picks[k], w, 0.0), axis=0, keepdims=True)
    wt_ref[...] = wscr[...].T
    carry[...] = carry[...] + jnp.sum(chosen_f, axis=1, keepdims=True)
    cnt_ref[...] = carry[...]


def _router(hp, rw_t, rb):
    n = hp[0].shape[0]
    tm = TM_ROUTER
    return pl.pallas_call(
        _router_kernel,
        grid=(n // tm,),
        in_specs=[_plane_spec(tm)] * N_PLANES + [
            pl.BlockSpec(rw_t.shape, lambda i: (0, 0)),
            pl.BlockSpec((N_EXPERTS, 1), lambda i: (0, 0)),
        ],
        out_specs=[
            pl.BlockSpec((TOP_K, tm), lambda i: (0, i)),
            pl.BlockSpec((TOP_K, tm), lambda i: (0, i)),
            pl.BlockSpec((tm, LANES), lambda i: (i, 0)),
            pl.BlockSpec((N_EXPERTS, LANES), lambda i: (0, 0)),
        ],
        out_shape=[
            jax.ShapeDtypeStruct((TOP_K, n), jnp.int32),
            jax.ShapeDtypeStruct((TOP_K, n), jnp.int32),
            jax.ShapeDtypeStruct((n, LANES), F32),
            jax.ShapeDtypeStruct((N_EXPERTS, LANES), F32),
        ],
        scratch_shapes=[pltpu.VMEM((N_EXPERTS, LANES), F32), pltpu.VMEM((LANES, tm), F32)],
        compiler_params=_cparams("arbitrary"),
        name="router",
    )(*hp, rw_t, rb)


def _plan_kernel(n_slots, cnt_ref, eid_ref, rank_ref, dest_ref, be_ref, nb_ref, pad_ref, np_ref):
    cnt = cnt_ref[...].astype(jnp.int32)
    padded = ((cnt + (MOE_BLOCK - 1)) >> MOE_BLOCK_LOG2) << MOE_BLOCK_LOG2
    ri = lax.broadcasted_iota(jnp.int32, (N_EXPERTS, N_EXPERTS), 0)
    ci = lax.broadcasted_iota(jnp.int32, (N_EXPERTS, N_EXPERTS), 1)
    pad_end = jnp.dot((ri >= ci).astype(F32), padded.astype(F32), precision=HI,
                      preferred_element_type=F32).astype(jnp.int32)
    pad_start = pad_end - padded
    eid = eid_ref[...]
    dest = rank_ref[...]
    for e in range(N_EXPERTS):
        dest = dest + jnp.where(eid == e, pad_start[e:e + 1, 0:1], 0)
    dest_ref[...] = dest
    nbp = be_ref.shape[1]
    jpos = lax.broadcasted_iota(jnp.int32, (N_EXPERTS, nbp), 1) * MOE_BLOCK
    be = jnp.minimum(jnp.sum((pad_end[:, 0:1] <= jpos).astype(jnp.int32), axis=0, keepdims=True), N_EXPERTS - 1)
    be_ref[...] = be
    nb_ref[...] = pad_end[N_EXPERTS - 1:N_EXPERTS, :] >> MOE_BLOCK_LOG2
    fill_end = pad_start + cnt
    io_e = lax.broadcasted_iota(jnp.int32, (N_EXPERTS, nbp), 0)
    end_j = jnp.sum(jnp.where(io_e == be, fill_end[:, 0:1], 0), axis=0, keepdims=True)
    pieces = (end_j - jpos[0:1, :] + (XS_PIECE - 1)) >> XS_PIECE_LOG2
    np_ref[...] = jnp.clip(pieces, 0, MOE_BLOCK // XS_PIECE)
    lane = lax.broadcasted_iota(jnp.int32, (N_EXPERTS, LANES), 1)
    piece_end = ((fill_end + (XS_PIECE - 1)) >> XS_PIECE_LOG2) << XS_PIECE_LOG2
    s = fill_end + lane
    pad_ref[...] = jnp.where(s < piece_end, s, n_slots + lane)


def _plan(cnt, eid, rank, n_slots):
    n = eid.shape[1]
    nbp = -(-(n_slots // MOE_BLOCK) // LANES) * LANES
    pad_rows = N_EXPERTS
    return pl.pallas_call(
        functools.partial(_plan_kernel, n_slots),
        grid=(1,),
        in_specs=[
            pl.BlockSpec(cnt.shape, lambda i: (0, 0)),
            pl.BlockSpec(eid.shape, lambda i: (0, 0)),
            pl.BlockSpec(rank.shape, lambda i: (0, 0)),
        ],
        out_specs=[
            pl.BlockSpec((TOP_K, n), lambda i: (0, 0)),
            pl.BlockSpec((1, nbp), lambda i: (0, 0)),
            pl.BlockSpec((1, LANES), lambda i: (0, 0)),
            pl.BlockSpec((pad_rows, LANES), lambda i: (0, 0)),
            pl.BlockSpec((1, nbp), lambda i: (0, 0)),
        ],
        out_shape=[
            jax.ShapeDtypeStruct((TOP_K, n), jnp.int32),
            jax.ShapeDtypeStruct((1, nbp), jnp.int32),
            jax.ShapeDtypeStruct((1, LANES), jnp.int32),
            jax.ShapeDtypeStruct((pad_rows, LANES), jnp.int32),
            jax.ShapeDtypeStruct((1, nbp), jnp.int32),
        ],
        compiler_params=_cparams("arbitrary"),
        name="moe_plan",
    )(cnt, eid, rank)


def _sc_mesh():
    return plsc.VectorSubcoreMesh(core_axis_name="c", subcore_axis_name="s")


def _sc_worker():
    return lax.axis_index("c") * SC_SUBCORES + lax.axis_index("s")


def _sc_dispatch(hp, dest_rows, pad_rows, n_rows):
    n = hp[0].shape[0]
    tiles_per_worker = n // SC_WIN // SC_WORKERS
    pad_per_worker = pad_rows.shape[0] // SC_WORKERS
    zeros = jnp.zeros((SC_WIN, LANES), jnp.uint32)

    def body(*refs):
        h = refs[:N_PLANES]
        dest_hbm, pad_hbm, z_hbm = refs[N_PLANES:N_PLANES + 3]
        xs = refs[N_PLANES + 3:2 * N_PLANES + 3]
        bufs = refs[2 * N_PLANES + 3:3 * N_PLANES + 3]
        ibuf, pbuf, sem = refs[3 * N_PLANES + 3:]
        wid = _sc_worker()

        pltpu.sync_copy(z_hbm, bufs[0])
        pltpu.sync_copy(pad_hbm.at[pl.ds(wid * pad_per_worker, pad_per_worker)], pbuf)
        copies = [pltpu.async_copy(bufs[0], xs[c].at[pbuf.at[r]], sem)
                  for r in range(pad_per_worker) for c in range(N_PLANES)]
        for cp in copies:
            cp.wait()

        @pl.loop(0, tiles_per_worker)
        def _(i):
            tile = wid * tiles_per_worker + i
            pltpu.sync_copy(dest_hbm.at[pl.ds(tile * TOP_K, TOP_K)], ibuf)
            for c in range(N_PLANES):
                pltpu.sync_copy(h[c].at[pl.ds(tile * SC_WIN, SC_WIN)], bufs[c])
            scatters = [pltpu.async_copy(bufs[c], xs[c].at[ibuf.at[k]], sem)
                        for c in range(N_PLANES) for k in range(TOP_K)]
            for cp in scatters:
                cp.wait()

    out_type = tuple(jax.ShapeDtypeStruct((n_rows, LANES), jnp.uint32) for _ in range(N_PLANES))
    scratch = ([pltpu.VMEM((SC_WIN, LANES), jnp.uint32)] * N_PLANES
               + [pltpu.VMEM((TOP_K, LANES), jnp.int32), pltpu.VMEM((pad_per_worker, LANES), jnp.int32),
                  pltpu.SemaphoreType.DMA])
    return pl.kernel(body, out_type=out_type, mesh=_sc_mesh(), scratch_types=scratch,
                     name="moe_sc_dispatch")(*hp, dest_rows, pad_rows, zeros)


def _sc_gather(ysp, dest_rows, n):
    tiles_per_worker = n // SC_WIN // SC_WORKERS

    def body(*refs):
        ys = refs[:N_PLANES]
        dest_hbm, g_hbm = refs[N_PLANES:N_PLANES + 2]
        bufs = refs[N_PLANES + 2:N_PLANES + 2 + SC_GATHER_BUFS]
        ibuf, sem = refs[N_PLANES + 2 + SC_GATHER_BUFS:]
        wid = _sc_worker()

        @pl.loop(0, tiles_per_worker)
        def _(i):
            tile = wid * tiles_per_worker + i
            pltpu.sync_copy(dest_hbm.at[pl.ds(tile * TOP_K, TOP_K)], ibuf)
            for c in range(N_PLANES):
                for k0 in range(0, TOP_K, SC_GATHER_BUFS):
                    gathers = [pltpu.async_copy(ys[c].at[ibuf.at[k0 + j]], bufs[j], sem)
                               for j in range(SC_GATHER_BUFS)]
                    for cp in gathers:
                        cp.wait()
                    stores = [pltpu.async_copy(
                        bufs[j], g_hbm.at[pl.ds(((k0 + j) * N_PLANES + c) * n + tile * SC_WIN, SC_WIN)], sem)
                        for j in range(SC_GATHER_BUFS)]
                    for cp in stores:
                        cp.wait()

    scratch = ([pltpu.VMEM((SC_WIN, LANES), jnp.uint32)] * SC_GATHER_BUFS
               + [pltpu.VMEM((TOP_K, LANES), jnp.int32), pltpu.SemaphoreType.DMA])
    return pl.kernel(body, out_type=jax.ShapeDtypeStruct((TOP_K * N_PLANES * n, LANES), jnp.uint32),
                     mesh=_sc_mesh(), scratch_types=scratch, name="moe_sc_gather")(*ysp, dest_rows)


def _expert_kernel(be_ref, nb_ref, np_ref, *refs):
    xs_hbm = refs[:N_PLANES]
    wg_ref, wu_ref, wd_ref = refs[N_PLANES:N_PLANES + 3]
    ys_refs = refs[N_PLANES + 3:2 * N_PLANES + 3]
    wg_b, wu_b, wd_b, ring, sems = refs[2 * N_PLANES + 3:]
    j = pl.program_id(0)
    nb = nb_ref[0]
    used = j < nb
    new_expert = (j == 0) | (be_ref[j] != be_ref[jnp.maximum(j - 1, 0)])

    def fetch(step, slot, wait):
        for p in range(MOE_BLOCK // XS_PIECE):
            @pl.when(p < np_ref[step])
            def _():
                src = pl.ds(pl.multiple_of(step * MOE_BLOCK + p * XS_PIECE, XS_PIECE), XS_PIECE)
                for c in range(N_PLANES):
                    cp = pltpu.make_async_copy(xs_hbm[c].at[src], ring.at[slot, c, pl.ds(p * XS_PIECE, XS_PIECE)],
                                               sems.at[slot])
                    if wait:
                        cp.wait()
                    else:
                        cp.start()

    @pl.when(j == 0)
    def _():
        ring[...] = jnp.zeros_like(ring)

    for s in range(XS_RING - 1):
        @pl.when((j == 0) & (s < nb))
        def _():
            fetch(s, s, wait=False)

    ahead = j + (XS_RING - 1)

    @pl.when(used & (ahead < nb))
    def _():
        fetch(ahead, lax.rem(ahead, XS_RING), wait=False)

    @pl.when(used & new_expert)
    def _():
        wg_b[...] = wg_ref[...].astype(BF16)
        wu_b[...] = wu_ref[...].astype(BF16)
        wd_b[...] = wd_ref[...].astype(BF16)

    @pl.when(used)
    def _():
        slot = lax.rem(j, XS_RING)
        fetch(j, slot, wait=True)
        x = _unpack_bf16_pairs(jnp.concatenate([ring[slot, c] for c in range(N_PLANES)], axis=1)).astype(BF16)
        a = jnp.dot(x, wg_b[...], preferred_element_type=F32)
        u = jnp.dot(x, wu_b[...], preferred_element_type=F32)
        y = jnp.dot((_silu(a) * u).astype(BF16), wd_b[...], preferred_element_type=F32)
        _store_planes(ys_refs, y)

    @pl.when(jnp.logical_not(used))
    def _():
        for ref in ys_refs:
            ref[...] = jnp.zeros_like(ref)


def _experts(be, nb, npieces, xsp, wg, wu, wd, layer, n_slots):
    d = wg.shape[2]

    def wspec(shape):
        def index(j, be_ref, nb_ref, np_ref):
            return layer, be_ref[jnp.minimum(j, jnp.maximum(nb_ref[0] - 1, 0))], 0, 0
        return pl.BlockSpec((None, None) + shape, index)

    grid_spec = pltpu.PrefetchScalarGridSpec(
        num_scalar_prefetch=3,
        grid=(n_slots // MOE_BLOCK,),
        in_specs=[pl.BlockSpec(memory_space=pl.ANY)] * N_PLANES
        + [wspec((d, D_EXPERT)), wspec((d, D_EXPERT)), wspec((D_EXPERT, d))],
        out_specs=[pl.BlockSpec((MOE_BLOCK, LANES), lambda j, be_ref, nb_ref, np_ref: (j, 0))] * N_PLANES,
        scratch_shapes=[pltpu.VMEM((d, D_EXPERT), BF16), pltpu.VMEM((d, D_EXPERT), BF16),
                        pltpu.VMEM((D_EXPERT, d), BF16),
                        pltpu.VMEM((XS_RING, N_PLANES, MOE_BLOCK, LANES), jnp.uint32),
                        pltpu.SemaphoreType.DMA((XS_RING,))],
    )
    return pl.pallas_call(
        _expert_kernel,
        grid_spec=grid_spec,
        out_shape=[jax.ShapeDtypeStruct((n_slots, LANES), jnp.uint32)] * N_PLANES,
        compiler_params=_cparams("arbitrary"),
        name="moe_experts",
    )(be, nb, npieces, *xsp, wg, wu, wd)


def _combine_kernel(final, g_ref, wt_ref, h0, h1, h2, h3, sg_ref, su_ref, sd_ref, x_ref, gate_ref, fw_ref, o_ref,
                    sg_b, su_b, sd_b):
    @pl.when(pl.program_id(0) == 0)
    def _():
        sg_b[...] = sg_ref[...].astype(BF16)
        su_b[...] = su_ref[...].astype(BF16)
        sd_b[...] = sd_ref[...].astype(BF16)

    hb = _load_planes((h0, h1, h2, h3)).astype(BF16)
    a = jnp.dot(hb, sg_b[...], preferred_element_type=F32)
    u = jnp.dot(hb, su_b[...], preferred_element_type=F32)
    acc = jnp.dot((_silu(a) * u).astype(BF16), sd_b[...], preferred_element_type=F32)
    wt = wt_ref[...]
    for k in range(TOP_K):
        rows = _unpack_bf16_pairs(jnp.concatenate([g_ref[k * N_PLANES + c] for c in range(N_PLANES)], axis=1))
        acc = acc + wt[:, k:k + 1] * rows
    xn = x_ref[...] + gate_ref[...] * acc
    if final:
        ms = jnp.mean(xn * xn, axis=-1, keepdims=True)
        xn = xn * lax.rsqrt(ms + EPS) * fw_ref[...]
    o_ref[...] = xn


def _combine(g, wt, hp, sg, su, sd, layer, x2, mod3, fw, seq, final):
    n, d = x2.shape
    tm = TM_COMBINE
    tpb = seq // tm

    def wspec(w):
        return pl.BlockSpec((None,) + w.shape[1:], lambda i: (layer, 0, 0), pipeline_mode=pl.Buffered(1))

    return pl.pallas_call(
        functools.partial(_combine_kernel, final),
        grid=(n // tm,),
        in_specs=[
            pl.BlockSpec((TOP_K * N_PLANES, tm, LANES), lambda i: (0, i, 0)),
            pl.BlockSpec((tm, LANES), lambda i: (i, 0)),
        ] + [_plane_spec(tm)] * N_PLANES + [
            wspec(sg), wspec(su), wspec(sd),
            pl.BlockSpec((tm, d), lambda i: (i, 0)),
            pl.BlockSpec((None, 1, d), lambda i: ((i // tpb) * N_ADA + 5, 0, 0)),
            pl.BlockSpec((1, d), lambda i: (0, 0)),
        ],
        out_specs=pl.BlockSpec((tm, d), lambda i: (i, 0)),
        out_shape=jax.ShapeDtypeStruct((n, d), F32),
        scratch_shapes=[pltpu.VMEM(sg.shape[1:], BF16), pltpu.VMEM(su.shape[1:], BF16),
                        pltpu.VMEM(sd.shape[1:], BF16)],
        compiler_params=_cparams("arbitrary"),
        name="moe_combine",
    )(g.reshape(TOP_K * N_PLANES, n, LANES), wt, *hp, sg, su, sd, x2, mod3, fw)


def _blockdiag_pairs(w):
    z = jnp.zeros((LRU_BLOCK_W, LRU_BLOCK_W), w.dtype)
    tiles = []
    for j in range(LRU_BLOCKS // 2):
        top = jnp.concatenate([w[2 * j], z], axis=1)
        bot = jnp.concatenate([z, w[2 * j + 1]], axis=1)
        tiles.append(jnp.concatenate([top, bot], axis=0))
    return jnp.stack(tiles).astype(BF16)


def _hg_level_masks():
    ch = HG_CHUNK
    msk = np.zeros((HG_LEVELS, ch, ch), np.float32)
    for lvl in range(HG_LEVELS):
        half = 1 << lvl
        for t in range(ch):
            base = (t // (2 * half)) * (2 * half)
            if (t // half) % 2 == 1:
                msk[lvl, t, base:base + half] = 1.0
    later = ((np.arange(ch)[None, :, None] >> np.arange(HG_LEVELS)[:, None, None]) & 1).astype(np.float32)
    later = np.broadcast_to(later, (HG_LEVELS, ch, HG_EXPAND))
    sgn = (2.0 * later - 1.0) * np.float32(LOG2E)
    return jnp.asarray(msk), jnp.asarray(later), jnp.asarray(sgn, dtype=F32)


def _ssd_expand():
    e = np.zeros((LANES, SSD_INNER), np.float32)
    for h in range(SSD_HEADS):
        e[h, h * SSD_HEADDIM:(h + 1) * SSD_HEADDIM] = 1.0
    return jnp.asarray(np.concatenate([e] * 3, axis=0), dtype=BF16)


def _pad_lanes(v, width):
    return jnp.pad(v, (0, width - v.shape[0])).reshape(1, width)


def _reorder_w_in(w_in):
    dt0 = SSD_INNER + SSD_INNER + 2 * SSD_GROUPS * SSD_STATE
    w_cat = jnp.concatenate([w_in[:, :, :dt0], w_in[:, :, dt0 + SSD_HEADS:], w_in[:, :, dt0:dt0 + SSD_HEADS]], axis=2)
    return jnp.pad(w_cat, ((0, 0), (0, 0), (0, U_WIDTH - w_cat.shape[2]))).astype(BF16)


def _layer_params(l, ssd_conv_w, ssd_conv_b, ssd_dt_bias, ssd_a_log, ssd_d, ssd_norm_w, hg_lower_bounds,
                  hg_norm_w, lru_conv_w, lru_conv_b, lru_wa, lru_ba, lru_wx, lru_bx, lru_lambda, lru_norm_w):
    msk, later, sgn = _hg_level_masks()
    return dict(
        cwx=ssd_conv_w[l][:, :SSD_INNER], cbx=ssd_conv_b[l][:SSD_INNER].reshape(1, -1),
        cwb=ssd_conv_w[l][:, SSD_INNER:], cbb=ssd_conv_b[l][SSD_INNER:].reshape(1, -1),
        dtb=_pad_lanes(ssd_dt_bias[l], LANES), alog=_pad_lanes(ssd_a_log[l], LANES),
        dful=jnp.repeat(ssd_d[l], SSD_HEADDIM).reshape(1, -1), ssd_nw=ssd_norm_w[l].reshape(1, -1),
        expand=_ssd_expand(),
        hg_lb=hg_lower_bounds, hg_nw=hg_norm_w[l].reshape(1, -1), hg_msk=msk, hg_later=later, hg_sgn=sgn,
        lru_cw=lru_conv_w[l], lru_cb=lru_conv_b[l].reshape(1, -1),
        lru_wa=_blockdiag_pairs(lru_wa[l]), lru_ba=lru_ba[l].reshape(1, -1),
        lru_wx=_blockdiag_pairs(lru_wx[l]), lru_bx=lru_bx[l].reshape(1, -1),
        lru_lam=lru_lambda[l].reshape(1, -1), lru_nw=lru_norm_w[l].reshape(1, -1),
    )


def kernel(x, c, ada_w, ada_b, norm_mix_w, norm_ffn_w, w_in, ssd_conv_w, ssd_conv_b, ssd_dt_bias, ssd_a_log, ssd_d, ssd_norm_w, hg_lower_bounds, hg_norm_w, lru_conv_w, lru_conv_b, lru_wa, lru_ba, lru_wx, lru_bx, lru_lambda, lru_norm_w, w_out, router_w, router_bias, exp_gate, exp_up, exp_down, sh_gate, sh_up, sh_down, final_norm_w):
    bsz, seq, d = x.shape
    depth = ada_w.shape[0]
    assert d == D_MODEL and seq % TM_OUTPROJ == 0 and seq % SSD_CHUNK == 0 and seq % R_HG == 0
    n = bsz * seq
    assert n % (SC_WIN * SC_WORKERS) == 0
    n_slots = n * TOP_K + N_EXPERTS * MOE_BLOCK
    x2 = x.reshape(n, d)
    mod = _adaln(c, ada_w, ada_b)
    fw = final_norm_w.reshape(1, d)
    w_cat = _reorder_w_in(w_in)
    for l in range(depth):
        p = _layer_params(l, ssd_conv_w, ssd_conv_b, ssd_dt_bias, ssd_a_log, ssd_d, ssd_norm_w,
                          hg_lower_bounds, hg_norm_w, lru_conv_w, lru_conv_b, lru_wa, lru_ba, lru_wx, lru_bx,
                          lru_lambda, lru_norm_w)
        mod3 = mod[l].reshape(bsz * N_ADA, 1, d)
        u = _inproj(x2, norm_mix_w[l].reshape(1, d), mod3, w_cat, l, seq)
        y_ssd = _ssd(u, p, bsz, seq)
        y_hg, y_lru = _hgrn2_lru(u, p, l, bsz, seq)
        x2, hp = _outproj(y_ssd, y_hg, y_lru, x2, w_out, l, norm_ffn_w[l].reshape(1, d), mod3, seq)
        eid, rank, wt, cnt = _router(hp, router_w[l].T, router_bias[l].reshape(N_EXPERTS, 1))
        dest, be, nb, pad_rows, npieces = _plan(cnt, eid, rank, n_slots)
        dest_rows = dest.reshape(TOP_K, n // LANES, LANES).transpose(1, 0, 2).reshape(n // LANES * TOP_K, LANES)
        xsp = _sc_dispatch(hp, dest_rows, pad_rows, n_slots + LANES)
        ysp = _experts(be.reshape(-1), nb[0, :1], npieces.reshape(-1), xsp, exp_gate, exp_up, exp_down, l, n_slots)
        g = _sc_gather(ysp, dest_rows, n)
        x2 = _combine(g, wt, hp, sh_gate, sh_up, sh_down, l, x2, mod3, fw, seq, final=(l == depth - 1))
    return x2.reshape(bsz, seq, d)
```

```python
import functools

import jax
import jax.numpy as jnp
import numpy as np
from jax import lax
from jax.experimental import pallas as pl
from jax.experimental.pallas import tpu as pltpu
from jax.experimental.pallas import tpu_sc as plsc

F32 = jnp.float32
BF16 = jnp.bfloat16
HI = lax.Precision.HIGHEST
F32_TINY = float(np.finfo(np.float32).tiny)
LOG2E = float(np.log2(np.e))

LANES = 128
SUBLANES = 8
VMEM_LIMIT_BYTES = 56 * 1024 * 1024

D_MODEL = 1024
EPS = 1e-6
N_ADA = 6
CONV_WIDTH = 4
SSD_INNER = 1024
SSD_HEADDIM = 64
SSD_HEADS = 16
SSD_GROUPS = 2
SSD_STATE = 128
SSD_CHUNK = 128
SSD_GROUP_W = SSD_INNER // SSD_GROUPS
HG_WIDTH = 512
HG_EXPAND = 128
HG_HEADS = 4
HG_CHUNK = 128
HG_LEVELS = 7
LRU_WIDTH = 512
LRU_BLOCKS = 8
LRU_BLOCK_W = 64
LRU_C = 8.0
N_EXPERTS = 64
TOP_K = 8
N_EXPERT_GROUPS = 8
E_PER_GROUP = 8
TOPK_GROUPS = 4
D_EXPERT = 256
ROUTED_SCALE = 2.5
MASK_SCORE = -1.0e4

COL_Z = 0
COL_XS = 1024
COL_BC = 2048
COL_HQ = 2560
COL_HF = 3072
COL_HV = 3584
COL_HG = 4096
COL_LG = 4608
COL_LX = 5120
COL_DT = 5632
U_WIDTH = 5760

TM_INPROJ = 512
TM_OUTPROJ = 1024
TM_ROUTER = 512
TM_COMBINE = 512
SC_CORES = 2
SC_SUBCORES = 16
SC_WORKERS = SC_CORES * SC_SUBCORES
SC_WIN = LANES
SC_GATHER_BUFS = 4
MOE_BLOCK_LOG2 = 10
MOE_BLOCK = 1 << MOE_BLOCK_LOG2
XS_RING = 3
XS_PIECE_LOG2 = 8
XS_PIECE = 1 << XS_PIECE_LOG2
R_HG = 512


def _cparams(*sem):
    return pltpu.CompilerParams(dimension_semantics=sem, vmem_limit_bytes=VMEM_LIMIT_BYTES)


def _sigmoid(x):
    return 0.5 * jnp.tanh(0.5 * x) + 0.5


def _silu(x):
    h = 0.5 * x
    return h + h * jnp.tanh(h)


def _split3(x):
    x1 = x.astype(BF16)
    r1 = x - x1.astype(F32)
    x2 = r1.astype(BF16)
    x3 = (r1 - x2.astype(F32)).astype(BF16)
    return x1, x2, x3


def _softplus(x):
    return jnp.maximum(x, 0.0) + jnp.log1p(jnp.exp(-jnp.abs(x)))


def _norm_mod(x, nw, shift, scale):
    ms = jnp.mean(x * x, axis=-1, keepdims=True)
    y = x * lax.rsqrt(ms + EPS) * nw
    return y * (1.0 + scale) + shift


_HI16 = np.uint32(0xFFFF0000)


def _pack_bf16_pairs(x):
    half = x.shape[1] // 2
    bits = lax.bitcast_convert_type(x.astype(BF16).astype(F32), jnp.uint32)
    return (bits[:, :half] & _HI16) | (bits[:, half:] >> 16)


def _unpack_bf16_pairs(w):
    hi = lax.bitcast_convert_type(w & _HI16, F32)
    lo = lax.bitcast_convert_type(w << 16, F32)
    return jnp.concatenate([hi, lo], axis=1)


N_PLANES = D_MODEL // 2 // LANES


def _store_planes(refs, x):
    packed = _pack_bf16_pairs(x)
    for c, ref in enumerate(refs):
        ref[...] = packed[:, c * LANES:(c + 1) * LANES]


def _load_planes(refs):
    return _unpack_bf16_pairs(jnp.concatenate([ref[...] for ref in refs], axis=1))


def _ada_kernel(c_ref, w_ref, b_ref, o_ref):
    c = c_ref[...]
    o_ref[...] = jnp.dot(_silu(c), w_ref[...], precision=HI, preferred_element_type=F32) + b_ref[...]


def _adaln(c, ada_w, ada_b):
    depth, d, n6 = ada_w.shape
    bsz = c.shape[0]
    tn = 1536
    return pl.pallas_call(
        _ada_kernel,
        grid=(depth, n6 // tn),
        in_specs=[
            pl.BlockSpec((bsz, d), lambda l, j: (0, 0)),
            pl.BlockSpec((None, d, tn), lambda l, j: (l, 0, j)),
            pl.BlockSpec((None, 1, tn), lambda l, j: (l, 0, j)),
        ],
        out_specs=pl.BlockSpec((None, bsz, tn), lambda l, j: (l, 0, j)),
        out_shape=jax.ShapeDtypeStruct((depth, bsz, n6), F32),
        compiler_params=_cparams("arbitrary", "arbitrary"),
        name="adaln_mod",
    )(c, ada_w, ada_b.reshape(depth, 1, n6))


def _inproj_kernel(x_ref, nw_ref, sh_ref, sc_ref, w_ref, o_ref):
    h = _norm_mod(x_ref[...], nw_ref[...], sh_ref[...], sc_ref[...])
    o_ref[...] = jnp.dot(h.astype(BF16), w_ref[...], preferred_element_type=F32)


def _inproj(x2, nw, mod3, w_cat, seq):
    n, d = x2.shape
    tm = TM_INPROJ
    tpb = seq // tm
    return pl.pallas_call(
        _inproj_kernel,
        grid=(n // tm,),
        in_specs=[
            pl.BlockSpec((tm, d), lambda i: (i, 0)),
            pl.BlockSpec((1, d), lambda i: (0, 0)),
            pl.BlockSpec((None, 1, d), lambda i: ((i // tpb) * N_ADA + 0, 0, 0)),
            pl.BlockSpec((None, 1, d), lambda i: ((i // tpb) * N_ADA + 1, 0, 0)),
            pl.BlockSpec((d, U_WIDTH), lambda i: (0, 0), pipeline_mode=pl.Buffered(1)),
        ],
        out_specs=pl.BlockSpec((tm, U_WIDTH), lambda i: (i, 0)),
        out_shape=jax.ShapeDtypeStruct((n, U_WIDTH), F32),
        compiler_params=_cparams("arbitrary"),
        name="inproj",
    )(x2, nw, mod3, mod3, w_cat)


def _causal_conv(cur_ref, ext, tail, cw_ref, cb_ref):
    rows, width = cur_ref.shape
    cur = cur_ref[...]
    ext[0:SUBLANES, :] = tail[...]
    ext[SUBLANES:SUBLANES + rows, :] = cur
    tail[...] = cur[rows - SUBLANES:rows, :]
    groups = ext[...].reshape(rows // SUBLANES + 1, SUBLANES, width)
    sub = lax.broadcasted_iota(jnp.int32, (1, SUBLANES, width), 1)
    acc = cb_ref[...] + cur * cw_ref[CONV_WIDTH - 1:CONV_WIDTH, :]
    for d in range(1, CONV_WIDTH):
        rot = pltpu.roll(groups, d, 1)
        back = jnp.where(sub < d, rot[:-1], rot[1:]).reshape(rows, width)
        acc = acc + back * cw_ref[CONV_WIDTH - 1 - d:CONV_WIDTH - d, :]
    return acc


def _ssd_kernel(z_ref, xs_ref, bc_ref, dt_ref, cwx_ref, cbx_ref, cwb_ref, cbb_ref, dtb_ref, alog_ref,
                dful_ref, nw_ref, e_ref, o_ref, extx, extb, tailx, tailb, hstate):
    c = pl.program_id(1)
    q = SSD_CHUNK

    @pl.when(c == 0)
    def _():
        tailx[...] = jnp.zeros_like(tailx)
        tailb[...] = jnp.zeros_like(tailb)
        hstate[...] = jnp.zeros_like(hstate)

    xs = _silu(_causal_conv(xs_ref, extx, tailx, cwx_ref, cbx_ref))
    bc = _silu(_causal_conv(bc_ref, extb, tailb, cwb_ref, cbb_ref))

    dt = _softplus(dt_ref[...] + dtb_ref[...])
    a = dt * (-jnp.exp(alog_ref[...]))
    ri = lax.broadcasted_iota(jnp.int32, (q, q), 0)
    ci = lax.broadcasted_iota(jnp.int32, (q, q), 1)
    tril = ri >= ci
    tril_f = tril.astype(F32)
    tril_b = tril.astype(BF16)
    acum = jnp.dot(jnp.concatenate([tril_b] * 3, axis=1), jnp.concatenate(_split3(a), axis=0),
                   preferred_element_type=F32)
    acum_t = acum.T
    expand3 = e_ref[...]
    dt_full = jnp.dot(jnp.concatenate(_split3(dt), axis=1), expand3, preferred_element_type=F32)
    acum_full = jnp.dot(jnp.concatenate(_split3(acum), axis=1), expand3, preferred_element_type=F32)
    alast_full = acum_full[q - 1:q, :]

    xdt = xs * dt_full
    xdt_b = xdt.astype(BF16)
    exp_a = jnp.exp(acum_full)
    xd_b = (xdt * jnp.exp(alast_full - acum_full)).astype(BF16)
    state_decay = jnp.exp(alast_full)
    left = lax.broadcasted_iota(jnp.int32, (q, LANES), 1) < SSD_HEADDIM
    zero_b = jnp.zeros((q, LANES), BF16)

    ys = []
    for g in range(SSD_GROUPS):
        b_g = bc[:, g * SSD_STATE:(g + 1) * SSD_STATE]
        c_g = bc[:, (SSD_GROUPS + g) * SSD_STATE:(SSD_GROUPS + g + 1) * SSD_STATE]
        c_b = c_g.astype(BF16)
        cb = lax.dot_general(c_b, b_g.astype(BF16), (((1,), (1,)), ((), ())), preferred_element_type=F32)
        cb = cb * tril_f
        cs = slice(g * SSD_GROUP_W, (g + 1) * SSD_GROUP_W)
        h_g = hstate[:, cs]
        y_off = jnp.dot(c_b, h_g.astype(BF16), preferred_element_type=F32) * exp_a[:, cs]
        pieces = []
        for pr in range(SSD_HEADS // SSD_GROUPS // 2):
            h0 = g * (SSD_HEADS // SSD_GROUPS) + 2 * pr
            ms = []
            for h in (h0, h0 + 1):
                col = acum[:, h:h + 1]
                row = acum_t[h:h + 1, :]
                ms.append((cb * jnp.exp(jnp.minimum(col - row, 0.0))).astype(BF16))
            lhs = jnp.concatenate(ms, axis=1)
            xp = xdt_b[:, h0 * SSD_HEADDIM:(h0 + 2) * SSD_HEADDIM]
            rhs = jnp.concatenate([jnp.where(left, xp, zero_b), jnp.where(left, zero_b, xp)], axis=0)
            pieces.append(jnp.dot(lhs, rhs, preferred_element_type=F32))
        ys.append(jnp.concatenate(pieces, axis=1) + y_off)
        b_t = b_g.T.astype(BF16)
        hstate[:, cs] = h_g * state_decay[:, cs] + jnp.dot(b_t, xd_b[:, cs], preferred_element_type=F32)

    y = jnp.concatenate(ys, axis=1) + xs * dful_ref[...]
    y = y * _silu(z_ref[...])
    outs = []
    for g in range(SSD_GROUPS):
        cs = slice(g * SSD_GROUP_W, (g + 1) * SSD_GROUP_W)
        yg = y[:, cs]
        ms = jnp.mean(yg * yg, axis=-1, keepdims=True)
        outs.append(yg * lax.rsqrt(ms + EPS) * nw_ref[:, cs])
    o_ref[...] = jnp.concatenate(outs, axis=1).astype(o_ref.dtype)


def _ssd(u, p, bsz, seq):
    q = SSD_CHUNK
    nc = seq // q
    n = bsz * seq

    def rows(b, c):
        return b * nc + c

    def const(shape):
        return pl.BlockSpec(shape, lambda b, c: (0,) * len(shape))

    return pl.pallas_call(
        _ssd_kernel,
        grid=(bsz, nc),
        in_specs=[
            pl.BlockSpec((q, SSD_INNER), lambda b, c: (rows(b, c), COL_Z // SSD_INNER)),
            pl.BlockSpec((q, SSD_INNER), lambda b, c: (rows(b, c), COL_XS // SSD_INNER)),
            pl.BlockSpec((q, 512), lambda b, c: (rows(b, c), COL_BC // 512)),
            pl.BlockSpec((q, LANES), lambda b, c: (rows(b, c), COL_DT // LANES)),
            const((CONV_WIDTH, SSD_INNER)), const((1, SSD_INNER)),
            const((CONV_WIDTH, 512)), const((1, 512)),
            const((1, LANES)), const((1, LANES)),
            const((1, SSD_INNER)), const((1, SSD_INNER)),
            const((3 * LANES, SSD_INNER)),
        ],
        out_specs=pl.BlockSpec((q, SSD_INNER), lambda b, c: (rows(b, c), 0)),
        out_shape=jax.ShapeDtypeStruct((n, SSD_INNER), BF16),
        scratch_shapes=[
            pltpu.VMEM((q + SUBLANES, SSD_INNER), F32),
            pltpu.VMEM((q + SUBLANES, 512), F32),
            pltpu.VMEM((SUBLANES, SSD_INNER), F32),
            pltpu.VMEM((SUBLANES, 512), F32),
            pltpu.VMEM((SSD_STATE, SSD_INNER), F32),
        ],
        compiler_params=_cparams("arbitrary", "arbitrary"),
        name="ssd_mixer",
    )(u, u, u, u, p["cwx"], p["cbx"], p["cwb"], p["cbb"], p["dtb"], p["alog"], p["dful"], p["ssd_nw"], p["expand"])


def _boundary_rows(b, lvl):
    half = 1 << lvl
    parts = []
    for v in range(b.shape[0] // SUBLANES):
        r0 = v * SUBLANES
        if 2 * half >= SUBLANES:
            src = (r0 // (2 * half)) * (2 * half) + half - 1
            parts.append(jnp.broadcast_to(b[src:src + 1, :], (SUBLANES, b.shape[1])))
        else:
            sub = lax.broadcasted_iota(jnp.int32, (SUBLANES, b.shape[1]), 0)
            piece = None
            for g in range(SUBLANES // (2 * half)):
                src = r0 + g * 2 * half + half - 1
                cand = jnp.broadcast_to(b[src:src + 1, :], (SUBLANES, b.shape[1]))
                piece = cand if piece is None else jnp.where(sub >= g * 2 * half, cand, piece)
            parts.append(piece)
    return jnp.concatenate(parts, axis=0)


def _hgrn2_lru_kernel(layer, q_ref, f_ref, v_ref, g_ref, lb_ref, nw_ref, msk_ref, half_ref, sgn_ref,
                      lg_ref, lx_ref, cw_ref, cb_ref, wa_ref, ba_ref, wx_ref, bx_ref, lam_ref, lnw_ref,
                      o_ref, ol_ref, state_t, ext, tail, hcarry):
    t = pl.program_id(1)
    ch = HG_CHUNK

    @pl.when(t == 0)
    def _():
        state_t[...] = jnp.zeros_like(state_t)
        tail[...] = jnp.zeros_like(tail)
        hcarry[...] = jnp.zeros_like(hcarry)

    _lru_tile(lg_ref, lx_ref, cw_ref, cb_ref, wa_ref, ba_ref, wx_ref, bx_ref, lam_ref, lnw_ref, ol_ref,
              ext, tail, hcarry)

    lrows = [lb_ref[j:j + 1, :] for j in range(lb_ref.shape[0])]
    mx = functools.reduce(jnp.maximum, lrows)
    es = [jnp.exp(r - mx) for r in lrows]
    den = functools.reduce(lambda a_, b_: a_ + b_, es)
    lb = jnp.zeros_like(mx)
    for j in range(1, layer + 1):
        lb = lb + es[j] / den
    one_minus_lb = 1.0 - lb
    nw = nw_ref[...]

    ri = lax.broadcasted_iota(jnp.int32, (ch, ch), 0)
    ci = lax.broadcasted_iota(jnp.int32, (ch, ch), 1)
    tril3 = jnp.concatenate([(ri >= ci).astype(BF16)] * 3, axis=1)
    rowi = lax.broadcasted_iota(jnp.int32, (ch, HG_EXPAND), 0)
    tgt = [((rowi >> lvl) & 1) == 1 for lvl in range(HG_LEVELS)]

    def head_chunk(h, rs):
        cs = slice(h * HG_EXPAND, (h + 1) * HG_EXPAND)
        qq = _silu(q_ref[rs, cs])
        kk = one_minus_lb[:, cs] * _sigmoid(-f_ref[rs, cs])
        logf = jnp.log1p(-kk)
        vv = v_ref[rs, cs]
        vb = vv.astype(BF16)
        b = jnp.dot(tril3, jnp.concatenate(_split3(logf), axis=0),
                    preferred_element_type=F32)
        st = state_t[h]
        o = lax.dot_general((qq * jnp.exp(b)).astype(BF16), st.astype(BF16), (((1,), (1,)), ((), ())),
                            preferred_element_type=F32)
        attn = jnp.zeros((ch, ch), F32)
        for lvl in range(HG_LEVELS):
            if lvl == 0:
                qe = jnp.where(tgt[0], qq * (1.0 - kk), 0.0)
                ke = jnp.where(tgt[0], 0.0, kk)
            else:
                m = _boundary_rows(b, lvl)
                later = half_ref[lvl]
                e = jnp.exp2((b - m) * sgn_ref[lvl])
                prod = jnp.where(tgt[lvl], qq, kk) * e
                qe = prod * later
                ke = prod - qe
            prod = lax.dot_general(qe.astype(BF16), ke.astype(BF16), (((1,), (1,)), ((), ())),
                                   preferred_element_type=F32)
            attn = attn + msk_ref[lvl] * prod
        diag = jnp.sum(qq * kk, axis=-1, keepdims=True)
        o = o + jnp.dot(attn.astype(BF16), vb, preferred_element_type=F32) + diag * vv
        b_last = b[ch - 1:ch, :]
        kd = (kk * jnp.exp(b_last - b)).astype(BF16)
        state_t[h] = st * jnp.exp(b_last) + jnp.dot(vv.T.astype(BF16), kd, preferred_element_type=F32)
        ms = jnp.mean(o * o, axis=-1, keepdims=True)
        y = o * lax.rsqrt(ms + EPS) * nw[:, cs]
        o_ref[rs, cs] = (y * _silu(g_ref[rs, cs])).astype(o_ref.dtype)

    for j in range(q_ref.shape[0] // ch):
        for h in range(HG_HEADS):
            head_chunk(h, slice(j * ch, (j + 1) * ch))


def _lru_tile(g_ref, x_ref, cw_ref, cb_ref, wa_ref, ba_ref, wx_ref, bx_ref, lam_ref, nw_ref, o_ref,
              ext, tail, hcarry):
    rows = x_ref.shape[0]
    xb = _causal_conv(x_ref, ext, tail, cw_ref, cb_ref)
    xbb = xb.astype(BF16)
    npair = LRU_WIDTH // LANES
    ra = jnp.concatenate([jnp.dot(xbb[:, j * LANES:(j + 1) * LANES], wa_ref[j], preferred_element_type=F32)
                          for j in range(npair)], axis=1)
    rx = jnp.concatenate([jnp.dot(xbb[:, j * LANES:(j + 1) * LANES], wx_ref[j], preferred_element_type=F32)
                          for j in range(npair)], axis=1)
    r = _sigmoid(ra + ba_ref[...])
    i = _sigmoid(rx + bx_ref[...])
    log_a = -LRU_C * r * _softplus(-lam_ref[...])
    a = jnp.exp(log_a)
    th = jnp.tanh(log_a)
    s = -2.0 * th
    root = s * lax.rsqrt(jnp.maximum(s * (1.0 - th), F32_TINY))
    u = root * (i * xb)

    ngroups = rows // SUBLANES
    sub = lax.broadcasted_iota(jnp.int32, (1, SUBLANES, LRU_WIDTH), 1)
    acc_a = a.reshape(ngroups, SUBLANES, LRU_WIDTH)
    acc_u = u.reshape(ngroups, SUBLANES, LRU_WIDTH)
    d = 1
    while d < SUBLANES:
        keep = sub >= d
        a_sh = jnp.where(keep, pltpu.roll(acc_a, d, 1), 1.0)
        u_sh = jnp.where(keep, pltpu.roll(acc_u, d, 1), 0.0)
        acc_u = acc_a * u_sh + acc_u
        acc_a = acc_a * a_sh
        d *= 2
    acc_a = acc_a.reshape(rows, LRU_WIDTH)
    acc_u = acc_u.reshape(rows, LRU_WIDTH)
    carry = hcarry[0:1, :]
    groups = []
    for g in range(rows // SUBLANES):
        gs = slice(g * SUBLANES, (g + 1) * SUBLANES)
        hg = acc_a[gs, :] * carry + acc_u[gs, :]
        groups.append(hg)
        carry = hg[SUBLANES - 1:SUBLANES, :]
    h = jnp.concatenate(groups, axis=0)
    hcarry[0:1, :] = carry

    gate = g_ref[...]
    gelu = 0.5 * gate * (1.0 + jnp.tanh(np.sqrt(2.0 / np.pi).astype(np.float32) * (gate + 0.044715 * (gate * gate * gate))))
    y = h * gelu
    ms = jnp.mean(y * y, axis=-1, keepdims=True)
    o_ref[...] = (y * lax.rsqrt(ms + EPS) * nw_ref[...]).astype(o_ref.dtype)


def _hgrn2_lru(u, p, layer, bsz, seq):
    r = R_HG
    nt = seq // r
    n = bsz * seq
    npair = LRU_WIDTH // LANES

    def col(base, width):
        return pl.BlockSpec((r, width), lambda b, t: (b * nt + t, base // width))

    def const(shape):
        return pl.BlockSpec(shape, lambda b, t: (0,) * len(shape))

    def out(width):
        return pl.BlockSpec((r, width), lambda b, t: (b * nt + t, 0))

    return pl.pallas_call(
        functools.partial(_hgrn2_lru_kernel, layer),
        grid=(bsz, nt),
        in_specs=[
            col(COL_HQ, HG_WIDTH), col(COL_HF, HG_WIDTH), col(COL_HV, HG_WIDTH), col(COL_HG, HG_WIDTH),
            const(p["hg_lb"].shape), const((1, HG_WIDTH)), const((HG_LEVELS, HG_CHUNK, HG_CHUNK)),
            const((HG_LEVELS, HG_CHUNK, HG_EXPAND)), const((HG_LEVELS, HG_CHUNK, HG_EXPAND)),
            col(COL_LG, LRU_WIDTH), col(COL_LX, LRU_WIDTH),
            const((CONV_WIDTH, LRU_WIDTH)), const((1, LRU_WIDTH)),
            const((npair, LANES, LANES)), const((1, LRU_WIDTH)),
            const((npair, LANES, LANES)), const((1, LRU_WIDTH)),
            const((1, LRU_WIDTH)), const((1, LRU_WIDTH)),
        ],
        out_specs=[out(HG_WIDTH), out(LRU_WIDTH)],
        out_shape=[jax.ShapeDtypeStruct((n, HG_WIDTH), BF16), jax.ShapeDtypeStruct((n, LRU_WIDTH), BF16)],
        scratch_shapes=[
            pltpu.VMEM((HG_HEADS, HG_EXPAND, HG_EXPAND), F32),
            pltpu.VMEM((r + SUBLANES, LRU_WIDTH), F32),
            pltpu.VMEM((SUBLANES, LRU_WIDTH), F32),
            pltpu.VMEM((SUBLANES, LRU_WIDTH), F32),
        ],
        compiler_params=_cparams("arbitrary", "arbitrary"),
        name="hgrn2_lru_mixer",
    )(u, u, u, u, p["hg_lb"], p["hg_nw"], p["hg_msk"], p["hg_later"], p["hg_sgn"],
      u, u, p["lru_cw"], p["lru_cb"], p["lru_wa"], p["lru_ba"], p["lru_wx"], p["lru_bx"], p["lru_lam"], p["lru_nw"])


def _outproj_kernel(ys_ref, yh_ref, yl_ref, x_ref, w_ref, g_ref, nw_ref, sh_ref, sc_ref, xo_ref, h0, h1, h2, h3,
                    w_b):
    @pl.when(pl.program_id(0) == 0)
    def _():
        w_b[...] = w_ref[...].astype(BF16)

    acc = jnp.dot(ys_ref[...], w_b[0:SSD_INNER, :], preferred_element_type=F32)
    acc = acc + jnp.dot(yh_ref[...], w_b[SSD_INNER:SSD_INNER + HG_WIDTH, :], preferred_element_type=F32)
    acc = acc + jnp.dot(yl_ref[...], w_b[SSD_INNER + HG_WIDTH:, :], preferred_element_type=F32)
    xn = x_ref[...] + g_ref[...] * acc
    xo_ref[...] = xn
    _store_planes((h0, h1, h2, h3), _norm_mod(xn, nw_ref[...], sh_ref[...], sc_ref[...]))


def _outproj(y_ssd, y_hg, y_lru, x2, w_out, layer, nw, mod3, seq):
    n, d = x2.shape
    tm = TM_OUTPROJ
    tpb = seq // tm

    def modspec(j):
        return pl.BlockSpec((None, 1, d), lambda i: ((i // tpb) * N_ADA + j, 0, 0))

    outs = pl.pallas_call(
        _outproj_kernel,
        grid=(n // tm,),
        in_specs=[
            pl.BlockSpec((tm, SSD_INNER), lambda i: (i, 0)),
            pl.BlockSpec((tm, HG_WIDTH), lambda i: (i, 0)),
            pl.BlockSpec((tm, LRU_WIDTH), lambda i: (i, 0)),
            pl.BlockSpec((tm, d), lambda i: (i, 0)),
            pl.BlockSpec((None,) + w_out.shape[1:], lambda i: (layer, 0, 0), pipeline_mode=pl.Buffered(1)),
            modspec(2),
            pl.BlockSpec((1, d), lambda i: (0, 0)),
            modspec(3), modspec(4),
        ],
        out_specs=[pl.BlockSpec((tm, d), lambda i: (i, 0))] + [_plane_spec(tm)] * N_PLANES,
        out_shape=[jax.ShapeDtypeStruct((n, d), F32)] + [jax.ShapeDtypeStruct((n, LANES), jnp.uint32)] * N_PLANES,
        scratch_shapes=[pltpu.VMEM(w_out.shape[1:], BF16)],
        compiler_params=_cparams("arbitrary"),
        name="outproj",
    )(y_ssd, y_hg, y_lru, x2, w_out, mod3, nw, mod3, mod3)
    return outs[0], tuple(outs[1:])


def _plane_spec(tm):
    return pl.BlockSpec((tm, LANES), lambda i: (i, 0))


def _router_kernel(h0, h1, h2, h3, rw_ref, rb_ref, eid_ref, rank_ref, wt_ref, cnt_ref, carry, wscr):
    tm = h0.shape[0]

    @pl.when(pl.program_id(0) == 0)
    def _():
        carry[...] = jnp.zeros_like(carry)

    hb = _load_planes((h0, h1, h2, h3)).astype(BF16)
    logit_t = sum(lax.dot_general(part, hb, (((1,), (1,)), ((), ())), preferred_element_type=F32)
                  for part in _split3(rw_ref[...]))
    score = _sigmoid(logit_t)
    sel = score + rb_ref[...]
    neg_inf = jnp.float32(-jnp.inf)
    io_g = lax.broadcasted_iota(jnp.int32, (E_PER_GROUP, tm), 0)
    blocks, gscore = [], []
    for g in range(N_EXPERT_GROUPS):
        blk = sel[g * E_PER_GROUP:(g + 1) * E_PER_GROUP, :]
        m1 = jnp.max(blk, axis=0, keepdims=True)
        i1 = jnp.min(jnp.where(blk == m1, io_g, E_PER_GROUP), axis=0, keepdims=True)
        m2 = jnp.max(jnp.where(io_g == i1, neg_inf, blk), axis=0, keepdims=True)
        blocks.append(blk)
        gscore.append(m1 + m2)
    masked = []
    for g in range(N_EXPERT_GROUPS):
        rank = jnp.zeros((1, tm), jnp.int32)
        for o in range(N_EXPERT_GROUPS):
            if o == g:
                continue
            beats = (gscore[o] > gscore[g]) | ((gscore[o] == gscore[g]) & (o < g))
            rank = rank + beats.astype(jnp.int32)
        masked.append(jnp.where(rank < TOPK_GROUPS, blocks[g], MASK_SCORE))
    val = jnp.concatenate(masked, axis=0)
    io_e = lax.broadcasted_iota(jnp.int32, (N_EXPERTS, tm), 0)
    chosen = jnp.zeros((N_EXPERTS, tm), jnp.bool_)
    picks = []
    for k in range(TOP_K):
        m = jnp.max(val, axis=0, keepdims=True)
        idx = jnp.min(jnp.where(val == m, io_e, N_EXPERTS), axis=0, keepdims=True)
        pick = io_e == idx
        picks.append(pick)
        eid_ref[k:k + 1, :] = idx
        chosen = chosen | pick
        val = jnp.where(pick, neg_inf, val)
    w = jnp.where(chosen, score, 0.0)
    w = w / jnp.sum(w, axis=0, keepdims=True) * ROUTED_SCALE

    chosen_f = chosen.astype(F32)
    earlier = (lax.broadcasted_iota(jnp.int32, (tm, tm), 0) < lax.broadcasted_iota(jnp.int32, (tm, tm), 1))
    before = jnp.dot(chosen_f.astype(BF16), earlier.astype(BF16), preferred_element_type=F32)
    grank = carry[:, 0:1] + before
    wscr[...] = jnp.zeros_like(wscr)
    for k in range(TOP_K):
        rank_ref[k:k + 1, :] = jnp.sum(jnp.where(picks[k], grank, 0.0), axis=0, keepdims=True).astype(jnp.int32)
        wscr[k:k + 1, :] = jnp.sum(jnp.where(picks[k], w, 0.0), axis=0, keepdims=True)
    wt_ref[...] = wscr[...].T
    carry[...] = carry[...] + jnp.sum(chosen_f, axis=1, keepdims=True)
    cnt_ref[...] = carry[...]


def _router(hp, rw_t, rb):
    n = hp[0].shape[0]
    tm = TM_ROUTER
    return pl.pallas_call(
        _router_kernel,
        grid=(n // tm,),
        in_specs=[_plane_spec(tm)] * N_PLANES + [
            pl.BlockSpec(rw_t.shape, lambda i: (0, 0)),
            pl.BlockSpec((N_EXPERTS, 1), lambda i: (0, 0)),
        ],
        out_specs=[
            pl.BlockSpec((TOP_K, tm), lambda i: (0, i)),
            pl.BlockSpec((TOP_K, tm), lambda i: (0, i)),
            pl.BlockSpec((tm, LANES), lambda i: (i, 0)),
            pl.BlockSpec((N_EXPERTS, LANES), lambda i: (0, 0)),
        ],
        out_shape=[
            jax.ShapeDtypeStruct((TOP_K, n), jnp.int32),
            jax.ShapeDtypeStruct((TOP_K, n), jnp.int32),
            jax.ShapeDtypeStruct((n, LANES), F32),
            jax.ShapeDtypeStruct((N_EXPERTS, LANES), F32),
        ],
        scratch_shapes=[pltpu.VMEM((N_EXPERTS, LANES), F32), pltpu.VMEM((LANES, tm), F32)],
        compiler_params=_cparams("arbitrary"),
        name="router",
    )(*hp, rw_t, rb)


def _plan_kernel(n_slots, cnt_ref, eid_ref, rank_ref, dest_ref, be_ref, nb_ref, pad_ref, np_ref):
    cnt = cnt_ref[...].astype(jnp.int32)
    padded = ((cnt + (MOE_BLOCK - 1)) >> MOE_BLOCK_LOG2) << MOE_BLOCK_LOG2
    ri = lax.broadcasted_iota(jnp.int32, (N_EXPERTS, N_EXPERTS), 0)
    ci = lax.broadcasted_iota(jnp.int32, (N_EXPERTS, N_EXPERTS), 1)
    pad_end = jnp.dot((ri >= ci).astype(F32), padded.astype(F32), precision=HI,
                      preferred_element_type=F32).astype(jnp.int32)
    pad_start = pad_end - padded
    eid = eid_ref[...]
    dest = rank_ref[...]
    for e in range(N_EXPERTS):
        dest = dest + jnp.where(eid == e, pad_start[e:e + 1, 0:1], 0)
    dest_ref[...] = dest
    nbp = be_ref.shape[1]
    jpos = lax.broadcasted_iota(jnp.int32, (N_EXPERTS, nbp), 1) * MOE_BLOCK
    be = jnp.minimum(jnp.sum((pad_end[:, 0:1] <= jpos).astype(jnp.int32), axis=0, keepdims=True), N_EXPERTS - 1)
    be_ref[...] = be
    nb_ref[...] = pad_end[N_EXPERTS - 1:N_EXPERTS, :] >> MOE_BLOCK_LOG2
    fill_end = pad_start + cnt
    io_e = lax.broadcasted_iota(jnp.int32, (N_EXPERTS, nbp), 0)
    end_j = jnp.sum(jnp.where(io_e == be, fill_end[:, 0:1], 0), axis=0, keepdims=True)
    pieces = (end_j - jpos[0:1, :] + (XS_PIECE - 1)) >> XS_PIECE_LOG2
    np_ref[...] = jnp.clip(pieces, 0, MOE_BLOCK // XS_PIECE)
    lane = lax.broadcasted_iota(jnp.int32, (N_EXPERTS, LANES), 1)
    piece_end = ((fill_end + (XS_PIECE - 1)) >> XS_PIECE_LOG2) << XS_PIECE_LOG2
    for r in range(XS_PIECE // LANES):
        s = fill_end + (r * LANES + lane)
        pad_ref[r * N_EXPERTS:(r + 1) * N_EXPERTS, :] = jnp.where(s < piece_end, s, n_slots + lane)


def _plan(cnt, eid, rank, n_slots):
    n = eid.shape[1]
    nbp = -(-(n_slots // MOE_BLOCK) // LANES) * LANES
    pad_rows = XS_PIECE // LANES * N_EXPERTS
    return pl.pallas_call(
        functools.partial(_plan_kernel, n_slots),
        grid=(1,),
        in_specs=[
            pl.BlockSpec(cnt.shape, lambda i: (0, 0)),
            pl.BlockSpec(eid.shape, lambda i: (0, 0)),
            pl.BlockSpec(rank.shape, lambda i: (0, 0)),
        ],
        out_specs=[
            pl.BlockSpec((TOP_K, n), lambda i: (0, 0)),
            pl.BlockSpec((1, nbp), lambda i: (0, 0)),
            pl.BlockSpec((1, LANES), lambda i: (0, 0)),
            pl.BlockSpec((pad_rows, LANES), lambda i: (0, 0)),
            pl.BlockSpec((1, nbp), lambda i: (0, 0)),
        ],
        out_shape=[
            jax.ShapeDtypeStruct((TOP_K, n), jnp.int32),
            jax.ShapeDtypeStruct((1, nbp), jnp.int32),
            jax.ShapeDtypeStruct((1, LANES), jnp.int32),
            jax.ShapeDtypeStruct((pad_rows, LANES), jnp.int32),
            jax.ShapeDtypeStruct((1, nbp), jnp.int32),
        ],
        compiler_params=_cparams("arbitrary"),
        name="moe_plan",
    )(cnt, eid, rank)


def _sc_mesh():
    return plsc.VectorSubcoreMesh(core_axis_name="c", subcore_axis_name="s")


def _sc_worker():
    return lax.axis_index("c") * SC_SUBCORES + lax.axis_index("s")


def _sc_dispatch(hp, dest_rows, pad_rows, n_rows):
    n = hp[0].shape[0]
    tiles_per_worker = n // SC_WIN // SC_WORKERS
    pad_per_worker = pad_rows.shape[0] // SC_WORKERS
    zeros = jnp.zeros((SC_WIN, LANES), jnp.uint32)

    def body(*refs):
        h = refs[:N_PLANES]
        dest_hbm, pad_hbm, z_hbm = refs[N_PLANES:N_PLANES + 3]
        xs = refs[N_PLANES + 3:2 * N_PLANES + 3]
        bufs = refs[2 * N_PLANES + 3:3 * N_PLANES + 3]
        ibuf, pbuf, sem = refs[3 * N_PLANES + 3:]
        wid = _sc_worker()

        pltpu.sync_copy(z_hbm, bufs[0])
        pltpu.sync_copy(pad_hbm.at[pl.ds(wid * pad_per_worker, pad_per_worker)], pbuf)
        copies = [pltpu.async_copy(bufs[0], xs[c].at[pbuf.at[r]], sem)
                  for r in range(pad_per_worker) for c in range(N_PLANES)]
        for cp in copies:
            cp.wait()

        @pl.loop(0, tiles_per_worker)
        def _(i):
            tile = wid * tiles_per_worker + i
            loads = [pltpu.async_copy(dest_hbm.at[pl.ds(tile * TOP_K, TOP_K)], ibuf, sem)]
            loads += [pltpu.async_copy(h[c].at[pl.ds(tile * SC_WIN, SC_WIN)], bufs[c], sem)
                      for c in range(N_PLANES)]
            for cp in loads:
                cp.wait()
            scatters = [pltpu.async_copy(bufs[c], xs[c].at[ibuf.at[k]], sem)
                        for c in range(N_PLANES) for k in range(TOP_K)]
            for cp in scatters:
                cp.wait()

    out_type = tuple(jax.ShapeDtypeStruct((n_rows, LANES), jnp.uint32) for _ in range(N_PLANES))
    scratch = ([pltpu.VMEM((SC_WIN, LANES), jnp.uint32)] * N_PLANES
               + [pltpu.VMEM((TOP_K, LANES), jnp.int32), pltpu.VMEM((pad_per_worker, LANES), jnp.int32),
                  pltpu.SemaphoreType.DMA])
    return pl.kernel(body, out_type=out_type, mesh=_sc_mesh(), scratch_types=scratch,
                     name="moe_sc_dispatch")(*hp, dest_rows, pad_rows, zeros)


def _sc_gather(ysp, dest_rows, n):
    tiles_per_worker = n // SC_WIN // SC_WORKERS

    def body(*refs):
        ys = refs[:N_PLANES]
        dest_hbm, g_hbm = refs[N_PLANES:N_PLANES + 2]
        bufs = refs[N_PLANES + 2:N_PLANES + 2 + SC_GATHER_BUFS]
        ibuf, sem = refs[N_PLANES + 2 + SC_GATHER_BUFS:]
        wid = _sc_worker()

        @pl.loop(0, tiles_per_worker)
        def _(i):
            tile = wid * tiles_per_worker + i
            pltpu.sync_copy(dest_hbm.at[pl.ds(tile * TOP_K, TOP_K)], ibuf)
            for c in range(N_PLANES):
                for k0 in range(0, TOP_K, SC_GATHER_BUFS):
                    gathers = [pltpu.async_copy(ys[c].at[ibuf.at[k0 + j]], bufs[j], sem)
                               for j in range(SC_GATHER_BUFS)]
                    for cp in gathers:
                        cp.wait()
                    stores = [pltpu.async_copy(
                        bufs[j], g_hbm.at[pl.ds(((k0 + j) * N_PLANES + c) * n + tile * SC_WIN, SC_WIN)], sem)
                        for j in range(SC_GATHER_BUFS)]
                    for cp in stores:
                        cp.wait()

    scratch = ([pltpu.VMEM((SC_WIN, LANES), jnp.uint32)] * SC_GATHER_BUFS
               + [pltpu.VMEM((TOP_K, LANES), jnp.int32), pltpu.SemaphoreType.DMA])
    return pl.kernel(body, out_type=jax.ShapeDtypeStruct((TOP_K * N_PLANES * n, LANES), jnp.uint32),
                     mesh=_sc_mesh(), scratch_types=scratch, name="moe_sc_gather")(*ysp, dest_rows)


def _expert_kernel(be_ref, nb_ref, np_ref, *refs):
    xs_hbm = refs[:N_PLANES]
    wg_ref, wu_ref, wd_ref = refs[N_PLANES:N_PLANES + 3]
    ys_refs = refs[N_PLANES + 3:2 * N_PLANES + 3]
    wg_b, wu_b, wd_b, ring, sems = refs[2 * N_PLANES + 3:]
    j = pl.program_id(0)
    nb = nb_ref[0]
    used = j < nb
    new_expert = (j == 0) | (be_ref[j] != be_ref[jnp.maximum(j - 1, 0)])

    def fetch(step, slot, wait):
        for p in range(MOE_BLOCK // XS_PIECE):
            @pl.when(p < np_ref[step])
            def _():
                src = pl.ds(pl.multiple_of(step * MOE_BLOCK + p * XS_PIECE, XS_PIECE), XS_PIECE)
                for c in range(N_PLANES):
                    cp = pltpu.make_async_copy(xs_hbm[c].at[src], ring.at[slot, c, pl.ds(p * XS_PIECE, XS_PIECE)],
                                               sems.at[slot])
                    if wait:
                        cp.wait()
                    else:
                        cp.start()

    @pl.when(j == 0)
    def _():
        ring[...] = jnp.zeros_like(ring)

    for s in range(XS_RING - 1):
        @pl.when((j == 0) & (s < nb))
        def _():
            fetch(s, s, wait=False)

    ahead = j + (XS_RING - 1)

    @pl.when(used & (ahead < nb))
    def _():
        fetch(ahead, lax.rem(ahead, XS_RING), wait=False)

    @pl.when(used & new_expert)
    def _():
        wg_b[...] = wg_ref[...].astype(BF16)
        wu_b[...] = wu_ref[...].astype(BF16)
        wd_b[...] = wd_ref[...].astype(BF16)

    @pl.when(used)
    def _():
        slot = lax.rem(j, XS_RING)
        fetch(j, slot, wait=True)
        x = _unpack_bf16_pairs(jnp.concatenate([ring[slot, c] for c in range(N_PLANES)], axis=1)).astype(BF16)
        a = jnp.dot(x, wg_b[...], preferred_element_type=F32)
        u = jnp.dot(x, wu_b[...], preferred_element_type=F32)
        y = jnp.dot((_silu(a) * u).astype(BF16), wd_b[...], preferred_element_type=F32)
        _store_planes(ys_refs, y)

    @pl.when(jnp.logical_not(used))
    def _():
        for ref in ys_refs:
            ref[...] = jnp.zeros_like(ref)


def _experts(be, nb, npieces, xsp, wg, wu, wd, layer, n_slots):
    d = wg.shape[2]

    def wspec(shape):
        def index(j, be_ref, nb_ref, np_ref):
            return layer, be_ref[jnp.minimum(j, jnp.maximum(nb_ref[0] - 1, 0))], 0, 0
        return pl.BlockSpec((None, None) + shape, index)

    grid_spec = pltpu.PrefetchScalarGridSpec(
        num_scalar_prefetch=3,
        grid=(n_slots // MOE_BLOCK,),
        in_specs=[pl.BlockSpec(memory_space=pl.ANY)] * N_PLANES
        + [wspec((d, D_EXPERT)), wspec((d, D_EXPERT)), wspec((D_EXPERT, d))],
        out_specs=[pl.BlockSpec((MOE_BLOCK, LANES), lambda j, be_ref, nb_ref, np_ref: (j, 0))] * N_PLANES,
        scratch_shapes=[pltpu.VMEM((d, D_EXPERT), BF16), pltpu.VMEM((d, D_EXPERT), BF16),
                        pltpu.VMEM((D_EXPERT, d), BF16),
                        pltpu.VMEM((XS_RING, N_PLANES, MOE_BLOCK, LANES), jnp.uint32),
                        pltpu.SemaphoreType.DMA((XS_RING,))],
    )
    return pl.pallas_call(
        _expert_kernel,
        grid_spec=grid_spec,
        out_shape=[jax.ShapeDtypeStruct((n_slots, LANES), jnp.uint32)] * N_PLANES,
        compiler_params=_cparams("arbitrary"),
        name="moe_experts",
    )(be, nb, npieces, *xsp, wg, wu, wd)


def _combine_kernel(final, g_ref, wt_ref, h0, h1, h2, h3, sg_ref, su_ref, sd_ref, x_ref, gate_ref, fw_ref, o_ref,
                    sg_b, su_b, sd_b):
    @pl.when(pl.program_id(0) == 0)
    def _():
        sg_b[...] = sg_ref[...].astype(BF16)
        su_b[...] = su_ref[...].astype(BF16)
        sd_b[...] = sd_ref[...].astype(BF16)

    hb = _load_planes((h0, h1, h2, h3)).astype(BF16)
    a = jnp.dot(hb, sg_b[...], preferred_element_type=F32)
    u = jnp.dot(hb, su_b[...], preferred_element_type=F32)
    acc = jnp.dot((_silu(a) * u).astype(BF16), sd_b[...], preferred_element_type=F32)
    wt = wt_ref[...]
    for k in range(TOP_K):
        rows = _unpack_bf16_pairs(jnp.concatenate([g_ref[k * N_PLANES + c] for c in range(N_PLANES)], axis=1))
        acc = acc + wt[:, k:k + 1] * rows
    xn = x_ref[...] + gate_ref[...] * acc
    if final:
        ms = jnp.mean(xn * xn, axis=-1, keepdims=True)
        xn = xn * lax.rsqrt(ms + EPS) * fw_ref[...]
    o_ref[...] = xn


def _combine(g, wt, hp, sg, su, sd, layer, x2, mod3, fw, seq, final):
    n, d = x2.shape
    tm = TM_COMBINE
    tpb = seq // tm

    def wspec(w):
        return pl.BlockSpec((None,) + w.shape[1:], lambda i: (layer, 0, 0), pipeline_mode=pl.Buffered(1))

    return pl.pallas_call(
        functools.partial(_combine_kernel, final),
        grid=(n // tm,),
        in_specs=[
            pl.BlockSpec((TOP_K * N_PLANES, tm, LANES), lambda i: (0, i, 0)),
            pl.BlockSpec((tm, LANES), lambda i: (i, 0)),
        ] + [_plane_spec(tm)] * N_PLANES + [
            wspec(sg), wspec(su), wspec(sd),
            pl.BlockSpec((tm, d), lambda i: (i, 0)),
            pl.BlockSpec((None, 1, d), lambda i: ((i // tpb) * N_ADA + 5, 0, 0)),
            pl.BlockSpec((1, d), lambda i: (0, 0)),
        ],
        out_specs=pl.BlockSpec((tm, d), lambda i: (i, 0)),
        out_shape=jax.ShapeDtypeStruct((n, d), F32),
        scratch_shapes=[pltpu.VMEM(sg.shape[1:], BF16), pltpu.VMEM(su.shape[1:], BF16),
                        pltpu.VMEM(sd.shape[1:], BF16)],
        compiler_params=_cparams("arbitrary"),
        name="moe_combine",
    )(g.reshape(TOP_K * N_PLANES, n, LANES), wt, *hp, sg, su, sd, x2, mod3, fw)


def _blockdiag_pairs(w):
    z = jnp.zeros((LRU_BLOCK_W, LRU_BLOCK_W), w.dtype)
    tiles = []
    for j in range(LRU_BLOCKS // 2):
        top = jnp.concatenate([w[2 * j], z], axis=1)
        bot = jnp.concatenate([z, w[2 * j + 1]], axis=1)
        tiles.append(jnp.concatenate([top, bot], axis=0))
    return jnp.stack(tiles).astype(BF16)


def _hg_level_masks():
    ch = HG_CHUNK
    msk = np.zeros((HG_LEVELS, ch, ch), np.float32)
    for lvl in range(HG_LEVELS):
        half = 1 << lvl
        for t in range(ch):
            base = (t // (2 * half)) * (2 * half)
            if (t // half) % 2 == 1:
                msk[lvl, t, base:base + half] = 1.0
    later = ((np.arange(ch)[None, :, None] >> np.arange(HG_LEVELS)[:, None, None]) & 1).astype(np.float32)
    later = np.broadcast_to(later, (HG_LEVELS, ch, HG_EXPAND))
    sgn = (2.0 * later - 1.0) * np.float32(LOG2E)
    return jnp.asarray(msk), jnp.asarray(later), jnp.asarray(sgn, dtype=F32)


def _ssd_expand():
    e = np.zeros((LANES, SSD_INNER), np.float32)
    for h in range(SSD_HEADS):
        e[h, h * SSD_HEADDIM:(h + 1) * SSD_HEADDIM] = 1.0
    return jnp.asarray(np.concatenate([e] * 3, axis=0), dtype=BF16)


def _pad_lanes(v, width):
    return jnp.pad(v, (0, width - v.shape[0])).reshape(1, width)


def _layer_params(l, w_in, ssd_conv_w, ssd_conv_b, ssd_dt_bias, ssd_a_log, ssd_d, ssd_norm_w, hg_lower_bounds,
                  hg_norm_w, lru_conv_w, lru_conv_b, lru_wa, lru_ba, lru_wx, lru_bx, lru_lambda, lru_norm_w):
    wi = w_in[l]
    dt0 = SSD_INNER + SSD_INNER + 2 * SSD_GROUPS * SSD_STATE
    w_cat = jnp.concatenate([wi[:, :dt0], wi[:, dt0 + SSD_HEADS:], wi[:, dt0:dt0 + SSD_HEADS]], axis=1)
    w_cat = jnp.pad(w_cat, ((0, 0), (0, U_WIDTH - w_cat.shape[1]))).astype(BF16)
    msk, later, sgn = _hg_level_masks()
    return dict(
        w_cat=w_cat,
        cwx=ssd_conv_w[l][:, :SSD_INNER], cbx=ssd_conv_b[l][:SSD_INNER].reshape(1, -1),
        cwb=ssd_conv_w[l][:, SSD_INNER:], cbb=ssd_conv_b[l][SSD_INNER:].reshape(1, -1),
        dtb=_pad_lanes(ssd_dt_bias[l], LANES), alog=_pad_lanes(ssd_a_log[l], LANES),
        dful=jnp.repeat(ssd_d[l], SSD_HEADDIM).reshape(1, -1), ssd_nw=ssd_norm_w[l].reshape(1, -1),
        expand=_ssd_expand(),
        hg_lb=hg_lower_bounds, hg_nw=hg_norm_w[l].reshape(1, -1), hg_msk=msk, hg_later=later, hg_sgn=sgn,
        lru_cw=lru_conv_w[l], lru_cb=lru_conv_b[l].reshape(1, -1),
        lru_wa=_blockdiag_pairs(lru_wa[l]), lru_ba=lru_ba[l].reshape(1, -1),
        lru_wx=_blockdiag_pairs(lru_wx[l]), lru_bx=lru_bx[l].reshape(1, -1),
        lru_lam=lru_lambda[l].reshape(1, -1), lru_nw=lru_norm_w[l].reshape(1, -1),
    )


def kernel(x, c, ada_w, ada_b, norm_mix_w, norm_ffn_w, w_in, ssd_conv_w, ssd_conv_b, ssd_dt_bias, ssd_a_log, ssd_d, ssd_norm_w, hg_lower_bounds, hg_norm_w, lru_conv_w, lru_conv_b, lru_wa, lru_ba, lru_wx, lru_bx, lru_lambda, lru_norm_w, w_out, router_w, router_bias, exp_gate, exp_up, exp_down, sh_gate, sh_up, sh_down, final_norm_w):
    bsz, seq, d = x.shape
    depth = ada_w.shape[0]
    assert d == D_MODEL and seq % TM_OUTPROJ == 0 and seq % SSD_CHUNK == 0 and seq % R_HG == 0
    n = bsz * seq
    assert n % (SC_WIN * SC_WORKERS) == 0
    n_slots = n * TOP_K + N_EXPERTS * MOE_BLOCK
    x2 = x.reshape(n, d)
    mod = _adaln(c, ada_w, ada_b)
    fw = final_norm_w.reshape(1, d)
    for l in range(depth):
        p = _layer_params(l, w_in, ssd_conv_w, ssd_conv_b, ssd_dt_bias, ssd_a_log, ssd_d, ssd_norm_w,
                          hg_lower_bounds, hg_norm_w, lru_conv_w, lru_conv_b, lru_wa, lru_ba, lru_wx, lru_bx,
                          lru_lambda, lru_norm_w)
        mod3 = mod[l].reshape(bsz * N_ADA, 1, d)
        u = _inproj(x2, norm_mix_w[l].reshape(1, d), mod3, p["w_cat"], seq)
        y_ssd = _ssd(u, p, bsz, seq)
        y_hg, y_lru = _hgrn2_lru(u, p, l, bsz, seq)
        x2, hp = _outproj(y_ssd, y_hg, y_lru, x2, w_out, l, norm_ffn_w[l].reshape(1, d), mod3, seq)
        eid, rank, wt, cnt = _router(hp, router_w[l].T, router_bias[l].reshape(N_EXPERTS, 1))
        dest, be, nb, pad_rows, npieces = _plan(cnt, eid, rank, n_slots)
        dest_rows = dest.reshape(TOP_K, n // LANES, LANES).transpose(1, 0, 2).reshape(n // LANES * TOP_K, LANES)
        xsp = _sc_dispatch(hp, dest_rows, pad_rows, n_slots + LANES)
        ysp = _experts(be.reshape(-1), nb[0, :1], npieces.reshape(-1), xsp, exp_gate, exp_up, exp_down, l, n_slots)
        g = _sc_gather(ysp, dest_rows, n)
        x2 = _combine(g, wt, hp, sh_gate, sh_up, sh_down, l, x2, mod3, fw, seq, final=(l == depth - 1))
    return x2.reshape(bsz, seq, d)
```

```python
import functools

import jax
import jax.numpy as jnp
import numpy as np
from jax import lax
from jax.experimental import pallas as pl
from jax.experimental.pallas import tpu as pltpu
from jax.experimental.pallas import tpu_sc as plsc

F32 = jnp.float32
BF16 = jnp.bfloat16
HI = lax.Precision.HIGHEST
F32_TINY = float(np.finfo(np.float32).tiny)
LOG2E = float(np.log2(np.e))

LANES = 128
SUBLANES = 8
VMEM_LIMIT_BYTES = 56 * 1024 * 1024

D_MODEL = 1024
EPS = 1e-6
N_ADA = 6
CONV_WIDTH = 4
SSD_INNER = 1024
SSD_HEADDIM = 64
SSD_HEADS = 16
SSD_GROUPS = 2
SSD_STATE = 128
SSD_CHUNK = 128
SSD_GROUP_W = SSD_INNER // SSD_GROUPS
HG_WIDTH = 512
HG_EXPAND = 128
HG_HEADS = 4
HG_CHUNK = 128
HG_LEVELS = 7
LRU_WIDTH = 512
LRU_BLOCKS = 8
LRU_BLOCK_W = 64
LRU_C = 8.0
N_EXPERTS = 64
TOP_K = 8
N_EXPERT_GROUPS = 8
E_PER_GROUP = 8
TOPK_GROUPS = 4
D_EXPERT = 256
ROUTED_SCALE = 2.5
MASK_SCORE = -1.0e4

COL_Z = 0
COL_XS = 1024
COL_BC = 2048
COL_HQ = 2560
COL_HF = 3072
COL_HV = 3584
COL_HG = 4096
COL_LG = 4608
COL_LX = 5120
COL_DT = 5632
U_WIDTH = 5760

TM_INPROJ = 512
TM_OUTPROJ = 1024
TM_ROUTER = 512
TM_COMBINE = 512
SC_CORES = 2
SC_SUBCORES = 16
SC_WORKERS = SC_CORES * SC_SUBCORES
SC_WIN = LANES
SC_GATHER_BUFS = 4
MOE_BLOCK_LOG2 = 10
MOE_BLOCK = 1 << MOE_BLOCK_LOG2
XS_RING = 3
XS_PIECE_LOG2 = 7
XS_PIECE = 1 << XS_PIECE_LOG2
R_HG = 512
R_SSD = 512


def _cparams(*sem):
    return pltpu.CompilerParams(dimension_semantics=sem, vmem_limit_bytes=VMEM_LIMIT_BYTES)


def _sigmoid(x):
    return 0.5 * jnp.tanh(0.5 * x) + 0.5


def _silu(x):
    h = 0.5 * x
    return h + h * jnp.tanh(h)


def _split3(x):
    x1 = x.astype(BF16)
    r1 = x - x1.astype(F32)
    x2 = r1.astype(BF16)
    x3 = (r1 - x2.astype(F32)).astype(BF16)
    return x1, x2, x3


def _softplus(x):
    return jnp.maximum(x, 0.0) + jnp.log1p(jnp.exp(-jnp.abs(x)))


def _norm_mod(x, nw, shift, scale):
    ms = jnp.mean(x * x, axis=-1, keepdims=True)
    y = x * lax.rsqrt(ms + EPS) * nw
    return y * (1.0 + scale) + shift


_HI16 = np.uint32(0xFFFF0000)


def _pack_bf16_pairs(x):
    half = x.shape[1] // 2
    bits = lax.bitcast_convert_type(x.astype(BF16).astype(F32), jnp.uint32)
    return (bits[:, :half] & _HI16) | (bits[:, half:] >> 16)


def _unpack_bf16_pairs(w):
    hi = lax.bitcast_convert_type(w & _HI16, F32)
    lo = lax.bitcast_convert_type(w << 16, F32)
    return jnp.concatenate([hi, lo], axis=1)


N_PLANES = D_MODEL // 2 // LANES


def _store_planes(refs, x):
    packed = _pack_bf16_pairs(x)
    for c, ref in enumerate(refs):
        ref[...] = packed[:, c * LANES:(c + 1) * LANES]


def _load_planes(refs):
    return _unpack_bf16_pairs(jnp.concatenate([ref[...] for ref in refs], axis=1))


def _ada_kernel(c_ref, w_ref, b_ref, o_ref):
    c = c_ref[...]
    o_ref[...] = jnp.dot(_silu(c), w_ref[...], precision=HI, preferred_element_type=F32) + b_ref[...]


def _adaln(c, ada_w, ada_b):
    depth, d, n6 = ada_w.shape
    bsz = c.shape[0]
    tn = 1536
    return pl.pallas_call(
        _ada_kernel,
        grid=(depth, n6 // tn),
        in_specs=[
            pl.BlockSpec((bsz, d), lambda l, j: (0, 0)),
            pl.BlockSpec((None, d, tn), lambda l, j: (l, 0, j)),
            pl.BlockSpec((None, 1, tn), lambda l, j: (l, 0, j)),
        ],
        out_specs=pl.BlockSpec((None, bsz, tn), lambda l, j: (l, 0, j)),
        out_shape=jax.ShapeDtypeStruct((depth, bsz, n6), F32),
        compiler_params=_cparams("arbitrary", "arbitrary"),
        name="adaln_mod",
    )(c, ada_w, ada_b.reshape(depth, 1, n6))


def _inproj_kernel(x_ref, nw_ref, sh_ref, sc_ref, w_ref, o_ref):
    h = _norm_mod(x_ref[...], nw_ref[...], sh_ref[...], sc_ref[...])
    o_ref[...] = jnp.dot(h.astype(BF16), w_ref[...], preferred_element_type=F32)


def _inproj(x2, nw, mod3, w_cat, seq):
    n, d = x2.shape
    tm = TM_INPROJ
    tpb = seq // tm
    return pl.pallas_call(
        _inproj_kernel,
        grid=(n // tm,),
        in_specs=[
            pl.BlockSpec((tm, d), lambda i: (i, 0)),
            pl.BlockSpec((1, d), lambda i: (0, 0)),
            pl.BlockSpec((None, 1, d), lambda i: ((i // tpb) * N_ADA + 0, 0, 0)),
            pl.BlockSpec((None, 1, d), lambda i: ((i // tpb) * N_ADA + 1, 0, 0)),
            pl.BlockSpec((d, U_WIDTH), lambda i: (0, 0), pipeline_mode=pl.Buffered(1)),
        ],
        out_specs=pl.BlockSpec((tm, U_WIDTH), lambda i: (i, 0)),
        out_shape=jax.ShapeDtypeStruct((n, U_WIDTH), F32),
        compiler_params=_cparams("arbitrary"),
        name="inproj",
    )(x2, nw, mod3, mod3, w_cat)


def _causal_conv(cur_ref, ext, tail, cw_ref, cb_ref):
    rows, width = cur_ref.shape
    cur = cur_ref[...]
    ext[0:SUBLANES, :] = tail[...]
    ext[SUBLANES:SUBLANES + rows, :] = cur
    tail[...] = cur[rows - SUBLANES:rows, :]
    groups = ext[...].reshape(rows // SUBLANES + 1, SUBLANES, width)
    sub = lax.broadcasted_iota(jnp.int32, (1, SUBLANES, width), 1)
    acc = cb_ref[...] + cur * cw_ref[CONV_WIDTH - 1:CONV_WIDTH, :]
    for d in range(1, CONV_WIDTH):
        rot = pltpu.roll(groups, d, 1)
        back = jnp.where(sub < d, rot[:-1], rot[1:]).reshape(rows, width)
        acc = acc + back * cw_ref[CONV_WIDTH - 1 - d:CONV_WIDTH - d, :]
    return acc


def _ssd_kernel(z_ref, xs_ref, bc_ref, dt_ref, cwx_ref, cbx_ref, cwb_ref, cbb_ref, dtb_ref, alog_ref,
                dful_ref, nw_ref, e_ref, o_ref, extx, extb, tailx, tailb, hstate, convx, convb):
    @pl.when(pl.program_id(1) == 0)
    def _():
        tailx[...] = jnp.zeros_like(tailx)
        tailb[...] = jnp.zeros_like(tailb)
        hstate[...] = jnp.zeros_like(hstate)

    convx[...] = _causal_conv(xs_ref, extx, tailx, cwx_ref, cbx_ref)
    convb[...] = _causal_conv(bc_ref, extb, tailb, cwb_ref, cbb_ref)
    for j in range(xs_ref.shape[0] // SSD_CHUNK):
        _ssd_chunk(slice(j * SSD_CHUNK, (j + 1) * SSD_CHUNK), z_ref, convx, convb, dt_ref, dtb_ref, alog_ref,
                   dful_ref, nw_ref, e_ref, o_ref, hstate)


def _ssd_chunk(rs, z_ref, convx, convb, dt_ref, dtb_ref, alog_ref, dful_ref, nw_ref, e_ref, o_ref, hstate):
    q = SSD_CHUNK
    xs = _silu(convx[rs, :])
    bc = _silu(convb[rs, :])

    dt = _softplus(dt_ref[rs, :] + dtb_ref[...])
    a = dt * (-jnp.exp(alog_ref[...]))
    ri = lax.broadcasted_iota(jnp.int32, (q, q), 0)
    ci = lax.broadcasted_iota(jnp.int32, (q, q), 1)
    tril = ri >= ci
    tril_f = tril.astype(F32)
    tril_b = tril.astype(BF16)
    acum = jnp.dot(jnp.concatenate([tril_b] * 3, axis=1), jnp.concatenate(_split3(a), axis=0),
                   preferred_element_type=F32)
    acum_t = acum.T
    expand3 = e_ref[...]
    dt_full = jnp.dot(jnp.concatenate(_split3(dt), axis=1), expand3, preferred_element_type=F32)
    acum_full = jnp.dot(jnp.concatenate(_split3(acum), axis=1), expand3, preferred_element_type=F32)
    alast_full = acum_full[q - 1:q, :]

    xdt = xs * dt_full
    xdt_b = xdt.astype(BF16)
    exp_a = jnp.exp(acum_full)
    xd_b = (xdt * jnp.exp(alast_full - acum_full)).astype(BF16)
    state_decay = jnp.exp(alast_full)
    left = lax.broadcasted_iota(jnp.int32, (q, LANES), 1) < SSD_HEADDIM
    zero_b = jnp.zeros((q, LANES), BF16)

    ys = []
    for g in range(SSD_GROUPS):
        b_g = bc[:, g * SSD_STATE:(g + 1) * SSD_STATE]
        c_g = bc[:, (SSD_GROUPS + g) * SSD_STATE:(SSD_GROUPS + g + 1) * SSD_STATE]
        c_b = c_g.astype(BF16)
        cb = lax.dot_general(c_b, b_g.astype(BF16), (((1,), (1,)), ((), ())), preferred_element_type=F32)
        cb = cb * tril_f
        cs = slice(g * SSD_GROUP_W, (g + 1) * SSD_GROUP_W)
        h_g = hstate[:, cs]
        y_off = jnp.dot(c_b, h_g.astype(BF16), preferred_element_type=F32) * exp_a[:, cs]
        pieces = []
        for pr in range(SSD_HEADS // SSD_GROUPS // 2):
            h0 = g * (SSD_HEADS // SSD_GROUPS) + 2 * pr
            ms = []
            for h in (h0, h0 + 1):
                col = acum[:, h:h + 1]
                row = acum_t[h:h + 1, :]
                ms.append((cb * jnp.exp(jnp.minimum(col - row, 0.0))).astype(BF16))
            lhs = jnp.concatenate(ms, axis=1)
            xp = xdt_b[:, h0 * SSD_HEADDIM:(h0 + 2) * SSD_HEADDIM]
            rhs = jnp.concatenate([jnp.where(left, xp, zero_b), jnp.where(left, zero_b, xp)], axis=0)
            pieces.append(jnp.dot(lhs, rhs, preferred_element_type=F32))
        ys.append(jnp.concatenate(pieces, axis=1) + y_off)
        b_t = b_g.T.astype(BF16)
        hstate[:, cs] = h_g * state_decay[:, cs] + jnp.dot(b_t, xd_b[:, cs], preferred_element_type=F32)

    y = jnp.concatenate(ys, axis=1) + xs * dful_ref[...]
    y = y * _silu(z_ref[rs, :])
    outs = []
    for g in range(SSD_GROUPS):
        cs = slice(g * SSD_GROUP_W, (g + 1) * SSD_GROUP_W)
        yg = y[:, cs]
        ms = jnp.mean(yg * yg, axis=-1, keepdims=True)
        outs.append(yg * lax.rsqrt(ms + EPS) * nw_ref[:, cs])
    o_ref[rs, :] = jnp.concatenate(outs, axis=1).astype(o_ref.dtype)


def _ssd(u, p, bsz, seq):
    q = R_SSD
    nc = seq // q
    n = bsz * seq

    def rows(b, c):
        return b * nc + c

    def const(shape):
        return pl.BlockSpec(shape, lambda b, c: (0,) * len(shape))

    return pl.pallas_call(
        _ssd_kernel,
        grid=(bsz, nc),
        in_specs=[
            pl.BlockSpec((q, SSD_INNER), lambda b, c: (rows(b, c), COL_Z // SSD_INNER)),
            pl.BlockSpec((q, SSD_INNER), lambda b, c: (rows(b, c), COL_XS // SSD_INNER)),
            pl.BlockSpec((q, 512), lambda b, c: (rows(b, c), COL_BC // 512)),
            pl.BlockSpec((q, LANES), lambda b, c: (rows(b, c), COL_DT // LANES)),
            const((CONV_WIDTH, SSD_INNER)), const((1, SSD_INNER)),
            const((CONV_WIDTH, 512)), const((1, 512)),
            const((1, LANES)), const((1, LANES)),
            const((1, SSD_INNER)), const((1, SSD_INNER)),
            const((3 * LANES, SSD_INNER)),
        ],
        out_specs=pl.BlockSpec((q, SSD_INNER), lambda b, c: (rows(b, c), 0)),
        out_shape=jax.ShapeDtypeStruct((n, SSD_INNER), BF16),
        scratch_shapes=[
            pltpu.VMEM((q + SUBLANES, SSD_INNER), F32),
            pltpu.VMEM((q + SUBLANES, 512), F32),
            pltpu.VMEM((SUBLANES, SSD_INNER), F32),
            pltpu.VMEM((SUBLANES, 512), F32),
            pltpu.VMEM((SSD_STATE, SSD_INNER), F32),
            pltpu.VMEM((q, SSD_INNER), F32),
            pltpu.VMEM((q, 512), F32),
        ],
        compiler_params=_cparams("arbitrary", "arbitrary"),
        name="ssd_mixer",
    )(u, u, u, u, p["cwx"], p["cbx"], p["cwb"], p["cbb"], p["dtb"], p["alog"], p["dful"], p["ssd_nw"], p["expand"])


def _boundary_rows(b, lvl):
    half = 1 << lvl
    parts = []
    for v in range(b.shape[0] // SUBLANES):
        r0 = v * SUBLANES
        if 2 * half >= SUBLANES:
            src = (r0 // (2 * half)) * (2 * half) + half - 1
            parts.append(jnp.broadcast_to(b[src:src + 1, :], (SUBLANES, b.shape[1])))
        else:
            sub = lax.broadcasted_iota(jnp.int32, (SUBLANES, b.shape[1]), 0)
            piece = None
            for g in range(SUBLANES // (2 * half)):
                src = r0 + g * 2 * half + half - 1
                cand = jnp.broadcast_to(b[src:src + 1, :], (SUBLANES, b.shape[1]))
                piece = cand if piece is None else jnp.where(sub >= g * 2 * half, cand, piece)
            parts.append(piece)
    return jnp.concatenate(parts, axis=0)


def _hgrn2_lru_kernel(layer, q_ref, f_ref, v_ref, g_ref, lb_ref, nw_ref, msk_ref, half_ref, sgn_ref,
                      lg_ref, lx_ref, cw_ref, cb_ref, wa_ref, ba_ref, wx_ref, bx_ref, lam_ref, lnw_ref,
                      o_ref, ol_ref, state_t, ext, tail, hcarry):
    t = pl.program_id(1)
    ch = HG_CHUNK

    @pl.when(t == 0)
    def _():
        state_t[...] = jnp.zeros_like(state_t)
        tail[...] = jnp.zeros_like(tail)
        hcarry[...] = jnp.zeros_like(hcarry)

    _lru_tile(lg_ref, lx_ref, cw_ref, cb_ref, wa_ref, ba_ref, wx_ref, bx_ref, lam_ref, lnw_ref, ol_ref,
              ext, tail, hcarry)

    lrows = [lb_ref[j:j + 1, :] for j in range(lb_ref.shape[0])]
    mx = functools.reduce(jnp.maximum, lrows)
    es = [jnp.exp(r - mx) for r in lrows]
    den = functools.reduce(lambda a_, b_: a_ + b_, es)
    lb = jnp.zeros_like(mx)
    for j in range(1, layer + 1):
        lb = lb + es[j] / den
    one_minus_lb = 1.0 - lb
    nw = nw_ref[...]

    ri = lax.broadcasted_iota(jnp.int32, (ch, ch), 0)
    ci = lax.broadcasted_iota(jnp.int32, (ch, ch), 1)
    tril3 = jnp.concatenate([(ri >= ci).astype(BF16)] * 3, axis=1)
    rowi = lax.broadcasted_iota(jnp.int32, (ch, HG_EXPAND), 0)
    tgt = [((rowi >> lvl) & 1) == 1 for lvl in range(HG_LEVELS)]

    def head_chunk(h, rs):
        cs = slice(h * HG_EXPAND, (h + 1) * HG_EXPAND)
        qq = _silu(q_ref[rs, cs])
        kk = one_minus_lb[:, cs] * _sigmoid(-f_ref[rs, cs])
        logf = jnp.log1p(-kk)
        vv = v_ref[rs, cs]
        vb = vv.astype(BF16)
        b = jnp.dot(tril3, jnp.concatenate(_split3(logf), axis=0),
                    preferred_element_type=F32)
        st = state_t[h]
        o = lax.dot_general((qq * jnp.exp(b)).astype(BF16), st.astype(BF16), (((1,), (1,)), ((), ())),
                            preferred_element_type=F32)
        attn = jnp.zeros((ch, ch), F32)
        for lvl in range(HG_LEVELS):
            if lvl == 0:
                qe = jnp.where(tgt[0], qq * (1.0 - kk), 0.0)
                ke = jnp.where(tgt[0], 0.0, kk)
            else:
                m = _boundary_rows(b, lvl)
                later = half_ref[lvl]
                e = jnp.exp2((b - m) * sgn_ref[lvl])
                prod = jnp.where(tgt[lvl], qq, kk) * e
                qe = prod * later
                ke = prod - qe
            prod = lax.dot_general(qe.astype(BF16), ke.astype(BF16), (((1,), (1,)), ((), ())),
                                   preferred_element_type=F32)
            attn = attn + msk_ref[lvl] * prod
        diag = jnp.sum(qq * kk, axis=-1, keepdims=True)
        o = o + jnp.dot(attn.astype(BF16), vb, preferred_element_type=F32) + diag * vv
        b_last = b[ch - 1:ch, :]
        kd = (kk * jnp.exp(b_last - b)).astype(BF16)
        state_t[h] = st * jnp.exp(b_last) + jnp.dot(vv.T.astype(BF16), kd, preferred_element_type=F32)
        ms = jnp.mean(o * o, axis=-1, keepdims=True)
        y = o * lax.rsqrt(ms + EPS) * nw[:, cs]
        o_ref[rs, cs] = (y * _silu(g_ref[rs, cs])).astype(o_ref.dtype)

    for j in range(q_ref.shape[0] // ch):
        for h in range(HG_HEADS):
            head_chunk(h, slice(j * ch, (j + 1) * ch))


def _lru_tile(g_ref, x_ref, cw_ref, cb_ref, wa_ref, ba_ref, wx_ref, bx_ref, lam_ref, nw_ref, o_ref,
              ext, tail, hcarry):
    rows = x_ref.shape[0]
    xb = _causal_conv(x_ref, ext, tail, cw_ref, cb_ref)
    xbb = xb.astype(BF16)
    npair = LRU_WIDTH // LANES
    ra = jnp.concatenate([jnp.dot(xbb[:, j * LANES:(j + 1) * LANES], wa_ref[j], preferred_element_type=F32)
                          for j in range(npair)], axis=1)
    rx = jnp.concatenate([jnp.dot(xbb[:, j * LANES:(j + 1) * LANES], wx_ref[j], preferred_element_type=F32)
                          for j in range(npair)], axis=1)
    r = _sigmoid(ra + ba_ref[...])
    i = _sigmoid(rx + bx_ref[...])
    log_a = -LRU_C * r * _softplus(-lam_ref[...])
    a = jnp.exp(log_a)
    th = jnp.tanh(log_a)
    s = -2.0 * th
    root = s * lax.rsqrt(jnp.maximum(s * (1.0 - th), F32_TINY))
    u = root * (i * xb)

    ngroups = rows // SUBLANES
    sub = lax.broadcasted_iota(jnp.int32, (1, SUBLANES, LRU_WIDTH), 1)
    acc_a = a.reshape(ngroups, SUBLANES, LRU_WIDTH)
    acc_u = u.reshape(ngroups, SUBLANES, LRU_WIDTH)
    d = 1
    while d < SUBLANES:
        keep = sub >= d
        a_sh = jnp.where(keep, pltpu.roll(acc_a, d, 1), 1.0)
        u_sh = jnp.where(keep, pltpu.roll(acc_u, d, 1), 0.0)
        acc_u = acc_a * u_sh + acc_u
        acc_a = acc_a * a_sh
        d *= 2
    acc_a = acc_a.reshape(rows, LRU_WIDTH)
    acc_u = acc_u.reshape(rows, LRU_WIDTH)
    carry = hcarry[0:1, :]
    groups = []
    for g in range(rows // SUBLANES):
        gs = slice(g * SUBLANES, (g + 1) * SUBLANES)
        hg = acc_a[gs, :] * carry + acc_u[gs, :]
        groups.append(hg)
        carry = hg[SUBLANES - 1:SUBLANES, :]
    h = jnp.concatenate(groups, axis=0)
    hcarry[0:1, :] = carry

    gate = g_ref[...]
    gelu = 0.5 * gate * (1.0 + jnp.tanh(np.sqrt(2.0 / np.pi).astype(np.float32) * (gate + 0.044715 * (gate * gate * gate))))
    y = h * gelu
    ms = jnp.mean(y * y, axis=-1, keepdims=True)
    o_ref[...] = (y * lax.rsqrt(ms + EPS) * nw_ref[...]).astype(o_ref.dtype)


def _hgrn2_lru(u, p, layer, bsz, seq):
    r = R_HG
    nt = seq // r
    n = bsz * seq
    npair = LRU_WIDTH // LANES

    def col(base, width):
        return pl.BlockSpec((r, width), lambda b, t: (b * nt + t, base // width))

    def const(shape):
        return pl.BlockSpec(shape, lambda b, t: (0,) * len(shape))

    def out(width):
        return pl.BlockSpec((r, width), lambda b, t: (b * nt + t, 0))

    return pl.pallas_call(
        functools.partial(_hgrn2_lru_kernel, layer),
        grid=(bsz, nt),
        in_specs=[
            col(COL_HQ, HG_WIDTH), col(COL_HF, HG_WIDTH), col(COL_HV, HG_WIDTH), col(COL_HG, HG_WIDTH),
            const(p["hg_lb"].shape), const((1, HG_WIDTH)), const((HG_LEVELS, HG_CHUNK, HG_CHUNK)),
            const((HG_LEVELS, HG_CHUNK, HG_EXPAND)), const((HG_LEVELS, HG_CHUNK, HG_EXPAND)),
            col(COL_LG, LRU_WIDTH), col(COL_LX, LRU_WIDTH),
            const((CONV_WIDTH, LRU_WIDTH)), const((1, LRU_WIDTH)),
            const((npair, LANES, LANES)), const((1, LRU_WIDTH)),
            const((npair, LANES, LANES)), const((1, LRU_WIDTH)),
            const((1, LRU_WIDTH)), const((1, LRU_WIDTH)),
        ],
        out_specs=[out(HG_WIDTH), out(LRU_WIDTH)],
        out_shape=[jax.ShapeDtypeStruct((n, HG_WIDTH), BF16), jax.ShapeDtypeStruct((n, LRU_WIDTH), BF16)],
        scratch_shapes=[
            pltpu.VMEM((HG_HEADS, HG_EXPAND, HG_EXPAND), F32),
            pltpu.VMEM((r + SUBLANES, LRU_WIDTH), F32),
            pltpu.VMEM((SUBLANES, LRU_WIDTH), F32),
            pltpu.VMEM((SUBLANES, LRU_WIDTH), F32),
        ],
        compiler_params=_cparams("arbitrary", "arbitrary"),
        name="hgrn2_lru_mixer",
    )(u, u, u, u, p["hg_lb"], p["hg_nw"], p["hg_msk"], p["hg_later"], p["hg_sgn"],
      u, u, p["lru_cw"], p["lru_cb"], p["lru_wa"], p["lru_ba"], p["lru_wx"], p["lru_bx"], p["lru_lam"], p["lru_nw"])


def _outproj_kernel(ys_ref, yh_ref, yl_ref, x_ref, w_ref, g_ref, nw_ref, sh_ref, sc_ref, xo_ref, h0, h1, h2, h3,
                    w_b):
    @pl.when(pl.program_id(0) == 0)
    def _():
        w_b[...] = w_ref[...].astype(BF16)

    acc = jnp.dot(ys_ref[...], w_b[0:SSD_INNER, :], preferred_element_type=F32)
    acc = acc + jnp.dot(yh_ref[...], w_b[SSD_INNER:SSD_INNER + HG_WIDTH, :], preferred_element_type=F32)
    acc = acc + jnp.dot(yl_ref[...], w_b[SSD_INNER + HG_WIDTH:, :], preferred_element_type=F32)
    xn = x_ref[...] + g_ref[...] * acc
    xo_ref[...] = xn
    _store_planes((h0, h1, h2, h3), _norm_mod(xn, nw_ref[...], sh_ref[...], sc_ref[...]))


def _outproj(y_ssd, y_hg, y_lru, x2, w_out, layer, nw, mod3, seq):
    n, d = x2.shape
    tm = TM_OUTPROJ
    tpb = seq // tm

    def modspec(j):
        return pl.BlockSpec((None, 1, d), lambda i: ((i // tpb) * N_ADA + j, 0, 0))

    outs = pl.pallas_call(
        _outproj_kernel,
        grid=(n // tm,),
        in_specs=[
            pl.BlockSpec((tm, SSD_INNER), lambda i: (i, 0)),
            pl.BlockSpec((tm, HG_WIDTH), lambda i: (i, 0)),
            pl.BlockSpec((tm, LRU_WIDTH), lambda i: (i, 0)),
            pl.BlockSpec((tm, d), lambda i: (i, 0)),
            pl.BlockSpec((None,) + w_out.shape[1:], lambda i: (layer, 0, 0), pipeline_mode=pl.Buffered(1)),
            modspec(2),
            pl.BlockSpec((1, d), lambda i: (0, 0)),
            modspec(3), modspec(4),
        ],
        out_specs=[pl.BlockSpec((tm, d), lambda i: (i, 0))] + [_plane_spec(tm)] * N_PLANES,
        out_shape=[jax.ShapeDtypeStruct((n, d), F32)] + [jax.ShapeDtypeStruct((n, LANES), jnp.uint32)] * N_PLANES,
        scratch_shapes=[pltpu.VMEM(w_out.shape[1:], BF16)],
        compiler_params=_cparams("arbitrary"),
        name="outproj",
    )(y_ssd, y_hg, y_lru, x2, w_out, mod3, nw, mod3, mod3)
    return outs[0], tuple(outs[1:])


def _plane_spec(tm):
    return pl.BlockSpec((tm, LANES), lambda i: (i, 0))


def _router_kernel(h0, h1, h2, h3, rw_ref, rb_ref, eid_ref, rank_ref, wt_ref, cnt_ref, carry, wscr):
    tm = h0.shape[0]

    @pl.when(pl.program_id(0) == 0)
    def _():
        carry[...] = jnp.zeros_like(carry)

    hb = _load_planes((h0, h1, h2, h3)).astype(BF16)
    logit_t = sum(lax.dot_general(part, hb, (((1,), (1,)), ((), ())), preferred_element_type=F32)
                  for part in _split3(rw_ref[...]))
    score = _sigmoid(logit_t)
    sel = score + rb_ref[...]
    neg_inf = jnp.float32(-jnp.inf)
    io_g = lax.broadcasted_iota(jnp.int32, (E_PER_GROUP, tm), 0)
    blocks, gscore = [], []
    for g in range(N_EXPERT_GROUPS):
        blk = sel[g * E_PER_GROUP:(g + 1) * E_PER_GROUP, :]
        m1 = jnp.max(blk, axis=0, keepdims=True)
        i1 = jnp.min(jnp.where(blk == m1, io_g, E_PER_GROUP), axis=0, keepdims=True)
        m2 = jnp.max(jnp.where(io_g == i1, neg_inf, blk), axis=0, keepdims=True)
        blocks.append(blk)
        gscore.append(m1 + m2)
    masked = []
    for g in range(N_EXPERT_GROUPS):
        rank = jnp.zeros((1, tm), jnp.int32)
        for o in range(N_EXPERT_GROUPS):
            if o == g:
                continue
            beats = (gscore[o] > gscore[g]) | ((gscore[o] == gscore[g]) & (o < g))
            rank = rank + beats.astype(jnp.int32)
        masked.append(jnp.where(rank < TOPK_GROUPS, blocks[g], MASK_SCORE))
    val = jnp.concatenate(masked, axis=0)
    io_e = lax.broadcasted_iota(jnp.int32, (N_EXPERTS, tm), 0)
    chosen = jnp.zeros((N_EXPERTS, tm), jnp.bool_)
    picks = []
    for k in range(TOP_K):
        m = jnp.max(val, axis=0, keepdims=True)
        idx = jnp.min(jnp.where(val == m, io_e, N_EXPERTS), axis=0, keepdims=True)
        pick = io_e == idx
        picks.append(pick)
        eid_ref[k:k + 1, :] = idx
        chosen = chosen | pick
        val = jnp.where(pick, neg_inf, val)
    w = jnp.where(chosen, score, 0.0)
    w = w / jnp.sum(w, axis=0, keepdims=True) * ROUTED_SCALE

    chosen_f = chosen.astype(F32)
    earlier = (lax.broadcasted_iota(jnp.int32, (tm, tm), 0) < lax.broadcasted_iota(jnp.int32, (tm, tm), 1))
    before = jnp.dot(chosen_f.astype(BF16), earlier.astype(BF16), preferred_element_type=F32)
    grank = carry[:, 0:1] + before
    wscr[...] = jnp.zeros_like(wscr)
    for k in range(TOP_K):
        rank_ref[k:k + 1, :] = jnp.sum(jnp.where(picks[k], grank, 0.0), axis=0, keepdims=True).astype(jnp.int32)
        wscr[k:k + 1, :] = jnp.sum(jnp.where(picks[k], w, 0.0), axis=0, keepdims=True)
    wt_ref[...] = wscr[...].T
    carry[...] = carry[...] + jnp.sum(chosen_f, axis=1, keepdims=True)
    cnt_ref[...] = carry[...]


def _router(hp, rw_t, rb):
    n = hp[0].shape[0]
    tm = TM_ROUTER
    return pl.pallas_call(
        _router_kernel,
        grid=(n // tm,),
        in_specs=[_plane_spec(tm)] * N_PLANES + [
            pl.BlockSpec(rw_t.shape, lambda i: (0, 0)),
            pl.BlockSpec((N_EXPERTS, 1), lambda i: (0, 0)),
        ],
        out_specs=[
            pl.BlockSpec((TOP_K, tm), lambda i: (0, i)),
            pl.BlockSpec((TOP_K, tm), lambda i: (0, i)),
            pl.BlockSpec((tm, LANES), lambda i: (i, 0)),
            pl.BlockSpec((N_EXPERTS, LANES), lambda i: (0, 0)),
        ],
        out_shape=[
            jax.ShapeDtypeStruct((TOP_K, n), jnp.int32),
            jax.ShapeDtypeStruct((TOP_K, n), jnp.int32),
            jax.ShapeDtypeStruct((n, LANES), F32),
            jax.ShapeDtypeStruct((N_EXPERTS, LANES), F32),
        ],
        scratch_shapes=[pltpu.VMEM((N_EXPERTS, LANES), F32), pltpu.VMEM((LANES, tm), F32)],
        compiler_params=_cparams("arbitrary"),
        name="router",
    )(*hp, rw_t, rb)


def _plan_kernel(n_slots, cnt_ref, eid_ref, rank_ref, dest_ref, be_ref, nb_ref, pad_ref, np_ref):
    cnt = cnt_ref[...].astype(jnp.int32)
    padded = ((cnt + (MOE_BLOCK - 1)) >> MOE_BLOCK_LOG2) << MOE_BLOCK_LOG2
    ri = lax.broadcasted_iota(jnp.int32, (N_EXPERTS, N_EXPERTS), 0)
    ci = lax.broadcasted_iota(jnp.int32, (N_EXPERTS, N_EXPERTS), 1)
    pad_end = jnp.dot((ri >= ci).astype(F32), padded.astype(F32), precision=HI,
                      preferred_element_type=F32).astype(jnp.int32)
    pad_start = pad_end - padded
    eid = eid_ref[...]
    dest = rank_ref[...]
    for e in range(N_EXPERTS):
        dest = dest + jnp.where(eid == e, pad_start[e:e + 1, 0:1], 0)
    dest_ref[...] = dest
    nbp = be_ref.shape[1]
    jpos = lax.broadcasted_iota(jnp.int32, (N_EXPERTS, nbp), 1) * MOE_BLOCK
    be = jnp.minimum(jnp.sum((pad_end[:, 0:1] <= jpos).astype(jnp.int32), axis=0, keepdims=True), N_EXPERTS - 1)
    be_ref[...] = be
    nb_ref[...] = pad_end[N_EXPERTS - 1:N_EXPERTS, :] >> MOE_BLOCK_LOG2
    fill_end = pad_start + cnt
    io_e = lax.broadcasted_iota(jnp.int32, (N_EXPERTS, nbp), 0)
    end_j = jnp.sum(jnp.where(io_e == be, fill_end[:, 0:1], 0), axis=0, keepdims=True)
    pieces = (end_j - jpos[0:1, :] + (XS_PIECE - 1)) >> XS_PIECE_LOG2
    np_ref[...] = jnp.clip(pieces, 0, MOE_BLOCK // XS_PIECE)
    lane = lax.broadcasted_iota(jnp.int32, (N_EXPERTS, LANES), 1)
    piece_end = ((fill_end + (XS_PIECE - 1)) >> XS_PIECE_LOG2) << XS_PIECE_LOG2
    s = fill_end + lane
    pad_ref[...] = jnp.where(s < piece_end, s, n_slots + lane)


def _plan(cnt, eid, rank, n_slots):
    n = eid.shape[1]
    nbp = -(-(n_slots // MOE_BLOCK) // LANES) * LANES
    pad_rows = N_EXPERTS
    return pl.pallas_call(
        functools.partial(_plan_kernel, n_slots),
        grid=(1,),
        in_specs=[
            pl.BlockSpec(cnt.shape, lambda i: (0, 0)),
            pl.BlockSpec(eid.shape, lambda i: (0, 0)),
            pl.BlockSpec(rank.shape, lambda i: (0, 0)),
        ],
        out_specs=[
            pl.BlockSpec((TOP_K, n), lambda i: (0, 0)),
            pl.BlockSpec((1, nbp), lambda i: (0, 0)),
            pl.BlockSpec((1, LANES), lambda i: (0, 0)),
            pl.BlockSpec((pad_rows, LANES), lambda i: (0, 0)),
            pl.BlockSpec((1, nbp), lambda i: (0, 0)),
        ],
        out_shape=[
            jax.ShapeDtypeStruct((TOP_K, n), jnp.int32),
            jax.ShapeDtypeStruct((1, nbp), jnp.int32),
            jax.ShapeDtypeStruct((1, LANES), jnp.int32),
            jax.ShapeDtypeStruct((pad_rows, LANES), jnp.int32),
            jax.ShapeDtypeStruct((1, nbp), jnp.int32),
        ],
        compiler_params=_cparams("arbitrary"),
        name="moe_plan",
    )(cnt, eid, rank)


def _sc_mesh():
    return plsc.VectorSubcoreMesh(core_axis_name="c", subcore_axis_name="s")


def _sc_worker():
    return lax.axis_index("c") * SC_SUBCORES + lax.axis_index("s")


def _sc_dispatch(hp, dest_rows, pad_rows, n_rows):
    n = hp[0].shape[0]
    tiles_per_worker = n // SC_WIN // SC_WORKERS
    pad_per_worker = pad_rows.shape[0] // SC_WORKERS
    zeros = jnp.zeros((SC_WIN, LANES), jnp.uint32)

    def body(*refs):
        h = refs[:N_PLANES]
        dest_hbm, pad_hbm, z_hbm = refs[N_PLANES:N_PLANES + 3]
        xs = refs[N_PLANES + 3:2 * N_PLANES + 3]
        bufs = refs[2 * N_PLANES + 3:3 * N_PLANES + 3]
        ibuf, pbuf, sem = refs[3 * N_PLANES + 3:]
        wid = _sc_worker()

        pltpu.sync_copy(z_hbm, bufs[0])
        pltpu.sync_copy(pad_hbm.at[pl.ds(wid * pad_per_worker, pad_per_worker)], pbuf)
        copies = [pltpu.async_copy(bufs[0], xs[c].at[pbuf.at[r]], sem)
                  for r in range(pad_per_worker) for c in range(N_PLANES)]
        for cp in copies:
            cp.wait()

        @pl.loop(0, tiles_per_worker)
        def _(i):
            tile = wid * tiles_per_worker + i
            pltpu.sync_copy(dest_hbm.at[pl.ds(tile * TOP_K, TOP_K)], ibuf)
            for c in range(N_PLANES):
                pltpu.sync_copy(h[c].at[pl.ds(tile * SC_WIN, SC_WIN)], bufs[c])
            scatters = [pltpu.async_copy(bufs[c], xs[c].at[ibuf.at[k]], sem)
                        for c in range(N_PLANES) for k in range(TOP_K)]
            for cp in scatters:
                cp.wait()

    out_type = tuple(jax.ShapeDtypeStruct((n_rows, LANES), jnp.uint32) for _ in range(N_PLANES))
    scratch = ([pltpu.VMEM((SC_WIN, LANES), jnp.uint32)] * N_PLANES
               + [pltpu.VMEM((TOP_K, LANES), jnp.int32), pltpu.VMEM((pad_per_worker, LANES), jnp.int32),
                  pltpu.SemaphoreType.DMA])
    return pl.kernel(body, out_type=out_type, mesh=_sc_mesh(), scratch_types=scratch,
                     name="moe_sc_dispatch")(*hp, dest_rows, pad_rows, zeros)


def _sc_gather(ysp, dest_rows, n):
    tiles_per_worker = n // SC_WIN // SC_WORKERS

    def body(*refs):
        ys = refs[:N_PLANES]
        dest_hbm, g_hbm = refs[N_PLANES:N_PLANES + 2]
        bufs = refs[N_PLANES + 2:N_PLANES + 2 + SC_GATHER_BUFS]
        ibuf, sem = refs[N_PLANES + 2 + SC_GATHER_BUFS:]
        wid = _sc_worker()

        @pl.loop(0, tiles_per_worker)
        def _(i):
            tile = wid * tiles_per_worker + i
            pltpu.sync_copy(dest_hbm.at[pl.ds(tile * TOP_K, TOP_K)], ibuf)
            for c in range(N_PLANES):
                for k0 in range(0, TOP_K, SC_GATHER_BUFS):
                    gathers = [pltpu.async_copy(ys[c].at[ibuf.at[k0 + j]], bufs[j], sem)
                               for j in range(SC_GATHER_BUFS)]
                    for cp in gathers:
                        cp.wait()
                    stores = [pltpu.async_copy(
                        bufs[j], g_hbm.at[pl.ds(((k0 + j) * N_PLANES + c) * n + tile * SC_WIN, SC_WIN)], sem)
                        for j in range(SC_GATHER_BUFS)]
                    for cp in stores:
                        cp.wait()

    scratch = ([pltpu.VMEM((SC_WIN, LANES), jnp.uint32)] * SC_GATHER_BUFS
               + [pltpu.VMEM((TOP_K, LANES), jnp.int32), pltpu.SemaphoreType.DMA])
    return pl.kernel(body, out_type=jax.ShapeDtypeStruct((TOP_K * N_PLANES * n, LANES), jnp.uint32),
                     mesh=_sc_mesh(), scratch_types=scratch, name="moe_sc_gather")(*ysp, dest_rows)


def _expert_kernel(be_ref, nb_ref, np_ref, *refs):
    xs_hbm = refs[:N_PLANES]
    wg_ref, wu_ref, wd_ref = refs[N_PLANES:N_PLANES + 3]
    ys_refs = refs[N_PLANES + 3:2 * N_PLANES + 3]
    wg_b, wu_b, wd_b, ring, sems = refs[2 * N_PLANES + 3:]
    j = pl.program_id(0)
    nb = nb_ref[0]
    used = j < nb
    new_expert = (j == 0) | (be_ref[j] != be_ref[jnp.maximum(j - 1, 0)])

    def fetch(step, slot, wait):
        for p in range(MOE_BLOCK // XS_PIECE):
            @pl.when(p < np_ref[step])
            def _():
                src = pl.ds(pl.multiple_of(step * MOE_BLOCK + p * XS_PIECE, XS_PIECE), XS_PIECE)
                for c in range(N_PLANES):
                    cp = pltpu.make_async_copy(xs_hbm[c].at[src], ring.at[slot, c, pl.ds(p * XS_PIECE, XS_PIECE)],
                                               sems.at[slot])
                    if wait:
                        cp.wait()
                    else:
                        cp.start()

    @pl.when(j == 0)
    def _():
        ring[...] = jnp.zeros_like(ring)

    for s in range(XS_RING - 1):
        @pl.when((j == 0) & (s < nb))
        def _():
            fetch(s, s, wait=False)

    ahead = j + (XS_RING - 1)

    @pl.when(used & (ahead < nb))
    def _():
        fetch(ahead, lax.rem(ahead, XS_RING), wait=False)

    @pl.when(used & new_expert)
    def _():
        wg_b[...] = wg_ref[...].astype(BF16)
        wu_b[...] = wu_ref[...].astype(BF16)
        wd_b[...] = wd_ref[...].astype(BF16)

    @pl.when(used)
    def _():
        slot = lax.rem(j, XS_RING)
        fetch(j, slot, wait=True)
        x = _unpack_bf16_pairs(jnp.concatenate([ring[slot, c] for c in range(N_PLANES)], axis=1)).astype(BF16)
        a = jnp.dot(x, wg_b[...], preferred_element_type=F32)
        u = jnp.dot(x, wu_b[...], preferred_element_type=F32)
        y = jnp.dot((_silu(a) * u).astype(BF16), wd_b[...], preferred_element_type=F32)
        _store_planes(ys_refs, y)

    @pl.when(jnp.logical_not(used))
    def _():
        for ref in ys_refs:
            ref[...] = jnp.zeros_like(ref)


def _experts(be, nb, npieces, xsp, wg, wu, wd, layer, n_slots):
    d = wg.shape[2]

    def wspec(shape):
        def index(j, be_ref, nb_ref, np_ref):
            return layer, be_ref[jnp.minimum(j, jnp.maximum(nb_ref[0] - 1, 0))], 0, 0
        return pl.BlockSpec((None, None) + shape, index)

    grid_spec = pltpu.PrefetchScalarGridSpec(
        num_scalar_prefetch=3,
        grid=(n_slots // MOE_BLOCK,),
        in_specs=[pl.BlockSpec(memory_space=pl.ANY)] * N_PLANES
        + [wspec((d, D_EXPERT)), wspec((d, D_EXPERT)), wspec((D_EXPERT, d))],
        out_specs=[pl.BlockSpec((MOE_BLOCK, LANES), lambda j, be_ref, nb_ref, np_ref: (j, 0))] * N_PLANES,
        scratch_shapes=[pltpu.VMEM((d, D_EXPERT), BF16), pltpu.VMEM((d, D_EXPERT), BF16),
                        pltpu.VMEM((D_EXPERT, d), BF16),
                        pltpu.VMEM((XS_RING, N_PLANES, MOE_BLOCK, LANES), jnp.uint32),
                        pltpu.SemaphoreType.DMA((XS_RING,))],
    )
    return pl.pallas_call(
        _expert_kernel,
        grid_spec=grid_spec,
        out_shape=[jax.ShapeDtypeStruct((n_slots, LANES), jnp.uint32)] * N_PLANES,
        compiler_params=_cparams("arbitrary"),
        name="moe_experts",
    )(be, nb, npieces, *xsp, wg, wu, wd)


def _combine_kernel(final, g_ref, wt_ref, h0, h1, h2, h3, sg_ref, su_ref, sd_ref, x_ref, gate_ref, fw_ref, o_ref,
                    sg_b, su_b, sd_b):
    @pl.when(pl.program_id(0) == 0)
    def _():
        sg_b[...] = sg_ref[...].astype(BF16)
        su_b[...] = su_ref[...].astype(BF16)
        sd_b[...] = sd_ref[...].astype(BF16)

    hb = _load_planes((h0, h1, h2, h3)).astype(BF16)
    a = jnp.dot(hb, sg_b[...], preferred_element_type=F32)
    u = jnp.dot(hb, su_b[...], preferred_element_type=F32)
    acc = jnp.dot((_silu(a) * u).astype(BF16), sd_b[...], preferred_element_type=F32)
    wt = wt_ref[...]
    for k in range(TOP_K):
        rows = _unpack_bf16_pairs(jnp.concatenate([g_ref[k * N_PLANES + c] for c in range(N_PLANES)], axis=1))
        acc = acc + wt[:, k:k + 1] * rows
    xn = x_ref[...] + gate_ref[...] * acc
    if final:
        ms = jnp.mean(xn * xn, axis=-1, keepdims=True)
        xn = xn * lax.rsqrt(ms + EPS) * fw_ref[...]
    o_ref[...] = xn


def _combine(g, wt, hp, sg, su, sd, layer, x2, mod3, fw, seq, final):
    n, d = x2.shape
    tm = TM_COMBINE
    tpb = seq // tm

    def wspec(w):
        return pl.BlockSpec((None,) + w.shape[1:], lambda i: (layer, 0, 0), pipeline_mode=pl.Buffered(1))

    return pl.pallas_call(
        functools.partial(_combine_kernel, final),
        grid=(n // tm,),
        in_specs=[
            pl.BlockSpec((TOP_K * N_PLANES, tm, LANES), lambda i: (0, i, 0)),
            pl.BlockSpec((tm, LANES), lambda i: (i, 0)),
        ] + [_plane_spec(tm)] * N_PLANES + [
            wspec(sg), wspec(su), wspec(sd),
            pl.BlockSpec((tm, d), lambda i: (i, 0)),
            pl.BlockSpec((None, 1, d), lambda i: ((i // tpb) * N_ADA + 5, 0, 0)),
            pl.BlockSpec((1, d), lambda i: (0, 0)),
        ],
        out_specs=pl.BlockSpec((tm, d), lambda i: (i, 0)),
        out_shape=jax.ShapeDtypeStruct((n, d), F32),
        scratch_shapes=[pltpu.VMEM(sg.shape[1:], BF16), pltpu.VMEM(su.shape[1:], BF16),
                        pltpu.VMEM(sd.shape[1:], BF16)],
        compiler_params=_cparams("arbitrary"),
        name="moe_combine",
    )(g.reshape(TOP_K * N_PLANES, n, LANES), wt, *hp, sg, su, sd, x2, mod3, fw)


def _blockdiag_pairs(w):
    z = jnp.zeros((LRU_BLOCK_W, LRU_BLOCK_W), w.dtype)
    tiles = []
    for j in range(LRU_BLOCKS // 2):
        top = jnp.concatenate([w[2 * j], z], axis=1)
        bot = jnp.concatenate([z, w[2 * j + 1]], axis=1)
        tiles.append(jnp.concatenate([top, bot], axis=0))
    return jnp.stack(tiles).astype(BF16)


def _hg_level_masks():
    ch = HG_CHUNK
    msk = np.zeros((HG_LEVELS, ch, ch), np.float32)
    for lvl in range(HG_LEVELS):
        half = 1 << lvl
        for t in range(ch):
            base = (t // (2 * half)) * (2 * half)
            if (t // half) % 2 == 1:
                msk[lvl, t, base:base + half] = 1.0
    later = ((np.arange(ch)[None, :, None] >> np.arange(HG_LEVELS)[:, None, None]) & 1).astype(np.float32)
    later = np.broadcast_to(later, (HG_LEVELS, ch, HG_EXPAND))
    sgn = (2.0 * later - 1.0) * np.float32(LOG2E)
    return jnp.asarray(msk), jnp.asarray(later), jnp.asarray(sgn, dtype=F32)


def _ssd_expand():
    e = np.zeros((LANES, SSD_INNER), np.float32)
    for h in range(SSD_HEADS):
        e[h, h * SSD_HEADDIM:(h + 1) * SSD_HEADDIM] = 1.0
    return jnp.asarray(np.concatenate([e] * 3, axis=0), dtype=BF16)


def _pad_lanes(v, width):
    return jnp.pad(v, (0, width - v.shape[0])).reshape(1, width)


def _layer_params(l, w_in, ssd_conv_w, ssd_conv_b, ssd_dt_bias, ssd_a_log, ssd_d, ssd_norm_w, hg_lower_bounds,
                  hg_norm_w, lru_conv_w, lru_conv_b, lru_wa, lru_ba, lru_wx, lru_bx, lru_lambda, lru_norm_w):
    wi = w_in[l]
    dt0 = SSD_INNER + SSD_INNER + 2 * SSD_GROUPS * SSD_STATE
    w_cat = jnp.concatenate([wi[:, :dt0], wi[:, dt0 + SSD_HEADS:], wi[:, dt0:dt0 + SSD_HEADS]], axis=1)
    w_cat = jnp.pad(w_cat, ((0, 0), (0, U_WIDTH - w_cat.shape[1]))).astype(BF16)
    msk, later, sgn = _hg_level_masks()
    return dict(
        w_cat=w_cat,
        cwx=ssd_conv_w[l][:, :SSD_INNER], cbx=ssd_conv_b[l][:SSD_INNER].reshape(1, -1),
        cwb=ssd_conv_w[l][:, SSD_INNER:], cbb=ssd_conv_b[l][SSD_INNER:].reshape(1, -1),
        dtb=_pad_lanes(ssd_dt_bias[l], LANES), alog=_pad_lanes(ssd_a_log[l], LANES),
        dful=jnp.repeat(ssd_d[l], SSD_HEADDIM).reshape(1, -1), ssd_nw=ssd_norm_w[l].reshape(1, -1),
        expand=_ssd_expand(),
        hg_lb=hg_lower_bounds, hg_nw=hg_norm_w[l].reshape(1, -1), hg_msk=msk, hg_later=later, hg_sgn=sgn,
        lru_cw=lru_conv_w[l], lru_cb=lru_conv_b[l].reshape(1, -1),
        lru_wa=_blockdiag_pairs(lru_wa[l]), lru_ba=lru_ba[l].reshape(1, -1),
        lru_wx=_blockdiag_pairs(lru_wx[l]), lru_bx=lru_bx[l].reshape(1, -1),
        lru_lam=lru_lambda[l].reshape(1, -1), lru_nw=lru_norm_w[l].reshape(1, -1),
    )


def kernel(x, c, ada_w, ada_b, norm_mix_w, norm_ffn_w, w_in, ssd_conv_w, ssd_conv_b, ssd_dt_bias, ssd_a_log, ssd_d, ssd_norm_w, hg_lower_bounds, hg_norm_w, lru_conv_w, lru_conv_b, lru_wa, lru_ba, lru_wx, lru_bx, lru_lambda, lru_norm_w, w_out, router_w, router_bias, exp_gate, exp_up, exp_down, sh_gate, sh_up, sh_down, final_norm_w):
    bsz, seq, d = x.shape
    depth = ada_w.shape[0]
    assert d == D_MODEL and seq % TM_OUTPROJ == 0 and seq % SSD_CHUNK == 0 and seq % R_HG == 0
    n = bsz * seq
    assert n % (SC_WIN * SC_WORKERS) == 0
    n_slots = n * TOP_K + N_EXPERTS * MOE_BLOCK
    x2 = x.reshape(n, d)
    mod = _adaln(c, ada_w, ada_b)
    fw = final_norm_w.reshape(1, d)
    for l in range(depth):
        p = _layer_params(l, w_in, ssd_conv_w, ssd_conv_b, ssd_dt_bias, ssd_a_log, ssd_d, ssd_norm_w,
                          hg_lower_bounds, hg_norm_w, lru_conv_w, lru_conv_b, lru_wa, lru_ba, lru_wx, lru_bx,
                          lru_lambda, lru_norm_w)
        mod3 = mod[l].reshape(bsz * N_ADA, 1, d)
        u = _inproj(x2, norm_mix_w[l].reshape(1, d), mod3, p["w_cat"], seq)
        y_ssd = _ssd(u, p, bsz, seq)
        y_hg, y_lru = _hgrn2_lru(u, p, l, bsz, seq)
        x2, hp = _outproj(y_ssd, y_hg, y_lru, x2, w_out, l, norm_ffn_w[l].reshape(1, d), mod3, seq)
        eid, rank, wt, cnt = _router(hp, router_w[l].T, router_bias[l].reshape(N_EXPERTS, 1))
        dest, be, nb, pad_rows, npieces = _plan(cnt, eid, rank, n_slots)
        dest_rows = dest.reshape(TOP_K, n // LANES, LANES).transpose(1, 0, 2).reshape(n // LANES * TOP_K, LANES)
        xsp = _sc_dispatch(hp, dest_rows, pad_rows, n_slots + LANES)
        ysp = _experts(be.reshape(-1), nb[0, :1], npieces.reshape(-1), xsp, exp_gate, exp_up, exp_down, l, n_slots)
        g = _sc_gather(ysp, dest_rows, n)
        x2 = _combine(g, wt, hp, sh_gate, sh_up, sh_down, l, x2, mod3, fw, seq, final=(l == depth - 1))
    return x2.reshape(bsz, seq, d)
```

```python
import functools

import jax
import jax.numpy as jnp
import numpy as np
from jax import lax
from jax.experimental import pallas as pl
from jax.experimental.pallas import tpu as pltpu
from jax.experimental.pallas import tpu_sc as plsc

F32 = jnp.float32
BF16 = jnp.bfloat16
HI = lax.Precision.HIGHEST
F32_TINY = float(np.finfo(np.float32).tiny)
LOG2E = float(np.log2(np.e))

LANES = 128
SUBLANES = 8
VMEM_LIMIT_BYTES = 56 * 1024 * 1024

D_MODEL = 1024
EPS = 1e-6
N_ADA = 6
CONV_WIDTH = 4
SSD_INNER = 1024
SSD_HEADDIM = 64
SSD_HEADS = 16
SSD_GROUPS = 2
SSD_STATE = 128
SSD_CHUNK = 128
SSD_GROUP_W = SSD_INNER // SSD_GROUPS
HG_WIDTH = 512
HG_EXPAND = 128
HG_HEADS = 4
HG_CHUNK = 128
HG_LEVELS = 7
LRU_WIDTH = 512
LRU_BLOCKS = 8
LRU_BLOCK_W = 64
LRU_C = 8.0
N_EXPERTS = 64
TOP_K = 8
N_EXPERT_GROUPS = 8
E_PER_GROUP = 8
TOPK_GROUPS = 4
D_EXPERT = 256
ROUTED_SCALE = 2.5
MASK_SCORE = -1.0e4

COL_Z = 0
COL_XS = 1024
COL_BC = 2048
COL_HQ = 2560
COL_HF = 3072
COL_HV = 3584
COL_HG = 4096
COL_LG = 4608
COL_LX = 5120
COL_DT = 5632
U_WIDTH = 5760

TM_INPROJ = 512
TM_OUTPROJ = 1024
TM_ROUTER = 512
TM_COMBINE = 512
SC_CORES = 2
SC_SUBCORES = 16
SC_WORKERS = SC_CORES * SC_SUBCORES
SC_WIN = LANES
SC_GATHER_BUFS = 4
MOE_BLOCK_LOG2 = 10
MOE_BLOCK = 1 << MOE_BLOCK_LOG2
XS_RING = 3
XS_PIECE_LOG2 = 7
XS_PIECE = 1 << XS_PIECE_LOG2
R_HG = 1024
R_SSD = 1024


def _cparams(*sem):
    return pltpu.CompilerParams(dimension_semantics=sem, vmem_limit_bytes=VMEM_LIMIT_BYTES)


def _sigmoid(x):
    return 0.5 * jnp.tanh(0.5 * x) + 0.5


def _silu(x):
    h = 0.5 * x
    return h + h * jnp.tanh(h)


def _split3(x):
    x1 = x.astype(BF16)
    r1 = x - x1.astype(F32)
    x2 = r1.astype(BF16)
    x3 = (r1 - x2.astype(F32)).astype(BF16)
    return x1, x2, x3


def _softplus(x):
    return jnp.maximum(x, 0.0) + jnp.log1p(jnp.exp(-jnp.abs(x)))


def _norm_mod(x, nw, shift, scale):
    ms = jnp.mean(x * x, axis=-1, keepdims=True)
    y = x * lax.rsqrt(ms + EPS) * nw
    return y * (1.0 + scale) + shift


_HI16 = np.uint32(0xFFFF0000)


def _pack_bf16_pairs(x):
    half = x.shape[1] // 2
    bits = lax.bitcast_convert_type(x.astype(BF16).astype(F32), jnp.uint32)
    return (bits[:, :half] & _HI16) | (bits[:, half:] >> 16)


def _unpack_bf16_pairs(w):
    hi = lax.bitcast_convert_type(w & _HI16, F32)
    lo = lax.bitcast_convert_type(w << 16, F32)
    return jnp.concatenate([hi, lo], axis=1)


N_PLANES = D_MODEL // 2 // LANES


def _store_planes(refs, x):
    packed = _pack_bf16_pairs(x)
    for c, ref in enumerate(refs):
        ref[...] = packed[:, c * LANES:(c + 1) * LANES]


def _load_planes(refs):
    return _unpack_bf16_pairs(jnp.concatenate([ref[...] for ref in refs], axis=1))


def _ada_kernel(c_ref, w_ref, b_ref, o_ref):
    c = c_ref[...]
    o_ref[...] = jnp.dot(_silu(c), w_ref[...], precision=HI, preferred_element_type=F32) + b_ref[...]


def _adaln(c, ada_w, ada_b):
    depth, d, n6 = ada_w.shape
    bsz = c.shape[0]
    tn = 1536
    return pl.pallas_call(
        _ada_kernel,
        grid=(depth, n6 // tn),
        in_specs=[
            pl.BlockSpec((bsz, d), lambda l, j: (0, 0)),
            pl.BlockSpec((None, d, tn), lambda l, j: (l, 0, j)),
            pl.BlockSpec((None, 1, tn), lambda l, j: (l, 0, j)),
        ],
        out_specs=pl.BlockSpec((None, bsz, tn), lambda l, j: (l, 0, j)),
        out_shape=jax.ShapeDtypeStruct((depth, bsz, n6), F32),
        compiler_params=_cparams("arbitrary", "arbitrary"),
        name="adaln_mod",
    )(c, ada_w, ada_b.reshape(depth, 1, n6))


def _inproj_kernel(x_ref, nw_ref, sh_ref, sc_ref, w_ref, o_ref):
    h = _norm_mod(x_ref[...], nw_ref[...], sh_ref[...], sc_ref[...])
    o_ref[...] = jnp.dot(h.astype(BF16), w_ref[...], preferred_element_type=F32)


def _inproj(x2, nw, mod3, w_cat, seq):
    n, d = x2.shape
    tm = TM_INPROJ
    tpb = seq // tm
    return pl.pallas_call(
        _inproj_kernel,
        grid=(n // tm,),
        in_specs=[
            pl.BlockSpec((tm, d), lambda i: (i, 0)),
            pl.BlockSpec((1, d), lambda i: (0, 0)),
            pl.BlockSpec((None, 1, d), lambda i: ((i // tpb) * N_ADA + 0, 0, 0)),
            pl.BlockSpec((None, 1, d), lambda i: ((i // tpb) * N_ADA + 1, 0, 0)),
            pl.BlockSpec((d, U_WIDTH), lambda i: (0, 0), pipeline_mode=pl.Buffered(1)),
        ],
        out_specs=pl.BlockSpec((tm, U_WIDTH), lambda i: (i, 0)),
        out_shape=jax.ShapeDtypeStruct((n, U_WIDTH), F32),
        compiler_params=_cparams("arbitrary"),
        name="inproj",
    )(x2, nw, mod3, mod3, w_cat)


def _causal_conv(cur_ref, ext, tail, cw_ref, cb_ref):
    rows, width = cur_ref.shape
    cur = cur_ref[...]
    ext[0:SUBLANES, :] = tail[...]
    ext[SUBLANES:SUBLANES + rows, :] = cur
    tail[...] = cur[rows - SUBLANES:rows, :]
    groups = ext[...].reshape(rows // SUBLANES + 1, SUBLANES, width)
    sub = lax.broadcasted_iota(jnp.int32, (1, SUBLANES, width), 1)
    acc = cb_ref[...] + cur * cw_ref[CONV_WIDTH - 1:CONV_WIDTH, :]
    for d in range(1, CONV_WIDTH):
        rot = pltpu.roll(groups, d, 1)
        back = jnp.where(sub < d, rot[:-1], rot[1:]).reshape(rows, width)
        acc = acc + back * cw_ref[CONV_WIDTH - 1 - d:CONV_WIDTH - d, :]
    return acc


def _ssd_kernel(z_ref, xs_ref, bc_ref, dt_ref, cwx_ref, cbx_ref, cwb_ref, cbb_ref, dtb_ref, alog_ref,
                dful_ref, nw_ref, e_ref, o_ref, extx, extb, tailx, tailb, hstate, convx, convb):
    @pl.when(pl.program_id(1) == 0)
    def _():
        tailx[...] = jnp.zeros_like(tailx)
        tailb[...] = jnp.zeros_like(tailb)
        hstate[...] = jnp.zeros_like(hstate)

    convx[...] = _causal_conv(xs_ref, extx, tailx, cwx_ref, cbx_ref)
    convb[...] = _causal_conv(bc_ref, extb, tailb, cwb_ref, cbb_ref)
    for j in range(xs_ref.shape[0] // SSD_CHUNK):
        _ssd_chunk(slice(j * SSD_CHUNK, (j + 1) * SSD_CHUNK), z_ref, convx, convb, dt_ref, dtb_ref, alog_ref,
                   dful_ref, nw_ref, e_ref, o_ref, hstate)


def _ssd_chunk(rs, z_ref, convx, convb, dt_ref, dtb_ref, alog_ref, dful_ref, nw_ref, e_ref, o_ref, hstate):
    q = SSD_CHUNK
    xs = _silu(convx[rs, :])
    bc = _silu(convb[rs, :])

    dt = _softplus(dt_ref[rs, :] + dtb_ref[...])
    a = dt * (-jnp.exp(alog_ref[...]))
    ri = lax.broadcasted_iota(jnp.int32, (q, q), 0)
    ci = lax.broadcasted_iota(jnp.int32, (q, q), 1)
    tril = ri >= ci
    tril_f = tril.astype(F32)
    tril_b = tril.astype(BF16)
    acum = jnp.dot(jnp.concatenate([tril_b] * 3, axis=1), jnp.concatenate(_split3(a), axis=0),
                   preferred_element_type=F32)
    acum_t = acum.T
    expand3 = e_ref[...]
    dt_full = jnp.dot(jnp.concatenate(_split3(dt), axis=1), expand3, preferred_element_type=F32)
    acum_full = jnp.dot(jnp.concatenate(_split3(acum), axis=1), expand3, preferred_element_type=F32)
    alast_full = acum_full[q - 1:q, :]

    xdt = xs * dt_full
    xdt_b = xdt.astype(BF16)
    exp_a = jnp.exp(acum_full)
    xd_b = (xdt * jnp.exp(alast_full - acum_full)).astype(BF16)
    state_decay = jnp.exp(alast_full)
    left = lax.broadcasted_iota(jnp.int32, (q, LANES), 1) < SSD_HEADDIM
    zero_b = jnp.zeros((q, LANES), BF16)

    ys = []
    for g in range(SSD_GROUPS):
        b_g = bc[:, g * SSD_STATE:(g + 1) * SSD_STATE]
        c_g = bc[:, (SSD_GROUPS + g) * SSD_STATE:(SSD_GROUPS + g + 1) * SSD_STATE]
        c_b = c_g.astype(BF16)
        cb = lax.dot_general(c_b, b_g.astype(BF16), (((1,), (1,)), ((), ())), preferred_element_type=F32)
        cb = cb * tril_f
        cs = slice(g * SSD_GROUP_W, (g + 1) * SSD_GROUP_W)
        h_g = hstate[:, cs]
        y_off = jnp.dot(c_b, h_g.astype(BF16), preferred_element_type=F32) * exp_a[:, cs]
        pieces = []
        for pr in range(SSD_HEADS // SSD_GROUPS // 2):
            h0 = g * (SSD_HEADS // SSD_GROUPS) + 2 * pr
            ms = []
            for h in (h0, h0 + 1):
                col = acum[:, h:h + 1]
                row = acum_t[h:h + 1, :]
                ms.append((cb * jnp.exp(jnp.minimum(col - row, 0.0))).astype(BF16))
            lhs = jnp.concatenate(ms, axis=1)
            xp = xdt_b[:, h0 * SSD_HEADDIM:(h0 + 2) * SSD_HEADDIM]
            rhs = jnp.concatenate([jnp.where(left, xp, zero_b), jnp.where(left, zero_b, xp)], axis=0)
            pieces.append(jnp.dot(lhs, rhs, preferred_element_type=F32))
        ys.append(jnp.concatenate(pieces, axis=1) + y_off)
        b_t = b_g.T.astype(BF16)
        hstate[:, cs] = h_g * state_decay[:, cs] + jnp.dot(b_t, xd_b[:, cs], preferred_element_type=F32)

    y = jnp.concatenate(ys, axis=1) + xs * dful_ref[...]
    y = y * _silu(z_ref[rs, :])
    outs = []
    for g in range(SSD_GROUPS):
        cs = slice(g * SSD_GROUP_W, (g + 1) * SSD_GROUP_W)
        yg = y[:, cs]
        ms = jnp.mean(yg * yg, axis=-1, keepdims=True)
        outs.append(yg * lax.rsqrt(ms + EPS) * nw_ref[:, cs])
    o_ref[rs, :] = jnp.concatenate(outs, axis=1).astype(o_ref.dtype)


def _ssd(u, p, bsz, seq):
    q = R_SSD
    nc = seq // q
    n = bsz * seq

    def rows(b, c):
        return b * nc + c

    def const(shape):
        return pl.BlockSpec(shape, lambda b, c: (0,) * len(shape))

    return pl.pallas_call(
        _ssd_kernel,
        grid=(bsz, nc),
        in_specs=[
            pl.BlockSpec((q, SSD_INNER), lambda b, c: (rows(b, c), COL_Z // SSD_INNER)),
            pl.BlockSpec((q, SSD_INNER), lambda b, c: (rows(b, c), COL_XS // SSD_INNER)),
            pl.BlockSpec((q, 512), lambda b, c: (rows(b, c), COL_BC // 512)),
            pl.BlockSpec((q, LANES), lambda b, c: (rows(b, c), COL_DT // LANES)),
            const((CONV_WIDTH, SSD_INNER)), const((1, SSD_INNER)),
            const((CONV_WIDTH, 512)), const((1, 512)),
            const((1, LANES)), const((1, LANES)),
            const((1, SSD_INNER)), const((1, SSD_INNER)),
            const((3 * LANES, SSD_INNER)),
        ],
        out_specs=pl.BlockSpec((q, SSD_INNER), lambda b, c: (rows(b, c), 0)),
        out_shape=jax.ShapeDtypeStruct((n, SSD_INNER), BF16),
        scratch_shapes=[
            pltpu.VMEM((q + SUBLANES, SSD_INNER), F32),
            pltpu.VMEM((q + SUBLANES, 512), F32),
            pltpu.VMEM((SUBLANES, SSD_INNER), F32),
            pltpu.VMEM((SUBLANES, 512), F32),
            pltpu.VMEM((SSD_STATE, SSD_INNER), F32),
            pltpu.VMEM((q, SSD_INNER), F32),
            pltpu.VMEM((q, 512), F32),
        ],
        compiler_params=_cparams("arbitrary", "arbitrary"),
        name="ssd_mixer",
    )(u, u, u, u, p["cwx"], p["cbx"], p["cwb"], p["cbb"], p["dtb"], p["alog"], p["dful"], p["ssd_nw"], p["expand"])


def _boundary_rows(b, lvl):
    half = 1 << lvl
    parts = []
    for v in range(b.shape[0] // SUBLANES):
        r0 = v * SUBLANES
        if 2 * half >= SUBLANES:
            src = (r0 // (2 * half)) * (2 * half) + half - 1
            parts.append(jnp.broadcast_to(b[src:src + 1, :], (SUBLANES, b.shape[1])))
        else:
            sub = lax.broadcasted_iota(jnp.int32, (SUBLANES, b.shape[1]), 0)
            piece = None
            for g in range(SUBLANES // (2 * half)):
                src = r0 + g * 2 * half + half - 1
                cand = jnp.broadcast_to(b[src:src + 1, :], (SUBLANES, b.shape[1]))
                piece = cand if piece is None else jnp.where(sub >= g * 2 * half, cand, piece)
            parts.append(piece)
    return jnp.concatenate(parts, axis=0)


def _hgrn2_lru_kernel(layer, q_ref, f_ref, v_ref, g_ref, lb_ref, nw_ref, msk_ref, half_ref, sgn_ref,
                      lg_ref, lx_ref, cw_ref, cb_ref, wa_ref, ba_ref, wx_ref, bx_ref, lam_ref, lnw_ref,
                      o_ref, ol_ref, state_t, ext, tail, hcarry):
    t = pl.program_id(1)
    ch = HG_CHUNK

    @pl.when(t == 0)
    def _():
        state_t[...] = jnp.zeros_like(state_t)
        tail[...] = jnp.zeros_like(tail)
        hcarry[...] = jnp.zeros_like(hcarry)

    _lru_tile(lg_ref, lx_ref, cw_ref, cb_ref, wa_ref, ba_ref, wx_ref, bx_ref, lam_ref, lnw_ref, ol_ref,
              ext, tail, hcarry)

    lrows = [lb_ref[j:j + 1, :] for j in range(lb_ref.shape[0])]
    mx = functools.reduce(jnp.maximum, lrows)
    es = [jnp.exp(r - mx) for r in lrows]
    den = functools.reduce(lambda a_, b_: a_ + b_, es)
    lb = jnp.zeros_like(mx)
    for j in range(1, layer + 1):
        lb = lb + es[j] / den
    one_minus_lb = 1.0 - lb
    nw = nw_ref[...]

    ri = lax.broadcasted_iota(jnp.int32, (ch, ch), 0)
    ci = lax.broadcasted_iota(jnp.int32, (ch, ch), 1)
    tril3 = jnp.concatenate([(ri >= ci).astype(BF16)] * 3, axis=1)
    rowi = lax.broadcasted_iota(jnp.int32, (ch, HG_EXPAND), 0)
    tgt = [((rowi >> lvl) & 1) == 1 for lvl in range(HG_LEVELS)]

    def head_chunk(h, rs):
        cs = slice(h * HG_EXPAND, (h + 1) * HG_EXPAND)
        qq = _silu(q_ref[rs, cs])
        kk = one_minus_lb[:, cs] * _sigmoid(-f_ref[rs, cs])
        logf = jnp.log1p(-kk)
        vv = v_ref[rs, cs]
        vb = vv.astype(BF16)
        b = jnp.dot(tril3, jnp.concatenate(_split3(logf), axis=0),
                    preferred_element_type=F32)
        st = state_t[h]
        o = lax.dot_general((qq * jnp.exp(b)).astype(BF16), st.astype(BF16), (((1,), (1,)), ((), ())),
                            preferred_element_type=F32)
        attn = jnp.zeros((ch, ch), F32)
        for lvl in range(HG_LEVELS):
            if lvl == 0:
                qe = jnp.where(tgt[0], qq * (1.0 - kk), 0.0)
                ke = jnp.where(tgt[0], 0.0, kk)
            else:
                m = _boundary_rows(b, lvl)
                later = half_ref[lvl]
                e = jnp.exp2((b - m) * sgn_ref[lvl])
                prod = jnp.where(tgt[lvl], qq, kk) * e
                qe = prod * later
                ke = prod - qe
            prod = lax.dot_general(qe.astype(BF16), ke.astype(BF16), (((1,), (1,)), ((), ())),
                                   preferred_element_type=F32)
            attn = attn + msk_ref[lvl] * prod
        diag = jnp.sum(qq * kk, axis=-1, keepdims=True)
        o = o + jnp.dot(attn.astype(BF16), vb, preferred_element_type=F32) + diag * vv
        b_last = b[ch - 1:ch, :]
        kd = (kk * jnp.exp(b_last - b)).astype(BF16)
        state_t[h] = st * jnp.exp(b_last) + jnp.dot(vv.T.astype(BF16), kd, preferred_element_type=F32)
        ms = jnp.mean(o * o, axis=-1, keepdims=True)
        y = o * lax.rsqrt(ms + EPS) * nw[:, cs]
        o_ref[rs, cs] = (y * _silu(g_ref[rs, cs])).astype(o_ref.dtype)

    for j in range(q_ref.shape[0] // ch):
        for h in range(HG_HEADS):
            head_chunk(h, slice(j * ch, (j + 1) * ch))


def _lru_tile(g_ref, x_ref, cw_ref, cb_ref, wa_ref, ba_ref, wx_ref, bx_ref, lam_ref, nw_ref, o_ref,
              ext, tail, hcarry):
    rows = x_ref.shape[0]
    xb = _causal_conv(x_ref, ext, tail, cw_ref, cb_ref)
    xbb = xb.astype(BF16)
    npair = LRU_WIDTH // LANES
    ra = jnp.concatenate([jnp.dot(xbb[:, j * LANES:(j + 1) * LANES], wa_ref[j], preferred_element_type=F32)
                          for j in range(npair)], axis=1)
    rx = jnp.concatenate([jnp.dot(xbb[:, j * LANES:(j + 1) * LANES], wx_ref[j], preferred_element_type=F32)
                          for j in range(npair)], axis=1)
    r = _sigmoid(ra + ba_ref[...])
    i = _sigmoid(rx + bx_ref[...])
    log_a = -LRU_C * r * _softplus(-lam_ref[...])
    a = jnp.exp(log_a)
    th = jnp.tanh(log_a)
    s = -2.0 * th
    root = s * lax.rsqrt(jnp.maximum(s * (1.0 - th), F32_TINY))
    u = root * (i * xb)

    ngroups = rows // SUBLANES
    sub = lax.broadcasted_iota(jnp.int32, (1, SUBLANES, LRU_WIDTH), 1)
    acc_a = a.reshape(ngroups, SUBLANES, LRU_WIDTH)
    acc_u = u.reshape(ngroups, SUBLANES, LRU_WIDTH)
    d = 1
    while d < SUBLANES:
        keep = sub >= d
        a_sh = jnp.where(keep, pltpu.roll(acc_a, d, 1), 1.0)
        u_sh = jnp.where(keep, pltpu.roll(acc_u, d, 1), 0.0)
        acc_u = acc_a * u_sh + acc_u
        acc_a = acc_a * a_sh
        d *= 2
    acc_a = acc_a.reshape(rows, LRU_WIDTH)
    acc_u = acc_u.reshape(rows, LRU_WIDTH)
    carry = hcarry[0:1, :]
    groups = []
    for g in range(rows // SUBLANES):
        gs = slice(g * SUBLANES, (g + 1) * SUBLANES)
        hg = acc_a[gs, :] * carry + acc_u[gs, :]
        groups.append(hg)
        carry = hg[SUBLANES - 1:SUBLANES, :]
    h = jnp.concatenate(groups, axis=0)
    hcarry[0:1, :] = carry

    gate = g_ref[...]
    gelu = 0.5 * gate * (1.0 + jnp.tanh(np.sqrt(2.0 / np.pi).astype(np.float32) * (gate + 0.044715 * (gate * gate * gate))))
    y = h * gelu
    ms = jnp.mean(y * y, axis=-1, keepdims=True)
    o_ref[...] = (y * lax.rsqrt(ms + EPS) * nw_ref[...]).astype(o_ref.dtype)


def _hgrn2_lru(u, p, layer, bsz, seq):
    r = R_HG
    nt = seq // r
    n = bsz * seq
    npair = LRU_WIDTH // LANES

    def col(base, width):
        return pl.BlockSpec((r, width), lambda b, t: (b * nt + t, base // width))

    def const(shape):
        return pl.BlockSpec(shape, lambda b, t: (0,) * len(shape))

    def out(width):
        return pl.BlockSpec((r, width), lambda b, t: (b * nt + t, 0))

    return pl.pallas_call(
        functools.partial(_hgrn2_lru_kernel, layer),
        grid=(bsz, nt),
        in_specs=[
            col(COL_HQ, HG_WIDTH), col(COL_HF, HG_WIDTH), col(COL_HV, HG_WIDTH), col(COL_HG, HG_WIDTH),
            const(p["hg_lb"].shape), const((1, HG_WIDTH)), const((HG_LEVELS, HG_CHUNK, HG_CHUNK)),
            const((HG_LEVELS, HG_CHUNK, HG_EXPAND)), const((HG_LEVELS, HG_CHUNK, HG_EXPAND)),
            col(COL_LG, LRU_WIDTH), col(COL_LX, LRU_WIDTH),
            const((CONV_WIDTH, LRU_WIDTH)), const((1, LRU_WIDTH)),
            const((npair, LANES, LANES)), const((1, LRU_WIDTH)),
            const((npair, LANES, LANES)), const((1, LRU_WIDTH)),
            const((1, LRU_WIDTH)), const((1, LRU_WIDTH)),
        ],
        out_specs=[out(HG_WIDTH), out(LRU_WIDTH)],
        out_shape=[jax.ShapeDtypeStruct((n, HG_WIDTH), BF16), jax.ShapeDtypeStruct((n, LRU_WIDTH), BF16)],
        scratch_shapes=[
            pltpu.VMEM((HG_HEADS, HG_EXPAND, HG_EXPAND), F32),
            pltpu.VMEM((r + SUBLANES, LRU_WIDTH), F32),
            pltpu.VMEM((SUBLANES, LRU_WIDTH), F32),
            pltpu.VMEM((SUBLANES, LRU_WIDTH), F32),
        ],
        compiler_params=_cparams("arbitrary", "arbitrary"),
        name="hgrn2_lru_mixer",
    )(u, u, u, u, p["hg_lb"], p["hg_nw"], p["hg_msk"], p["hg_later"], p["hg_sgn"],
      u, u, p["lru_cw"], p["lru_cb"], p["lru_wa"], p["lru_ba"], p["lru_wx"], p["lru_bx"], p["lru_lam"], p["lru_nw"])


def _outproj_kernel(ys_ref, yh_ref, yl_ref, x_ref, w_ref, g_ref, nw_ref, sh_ref, sc_ref, xo_ref, h0, h1, h2, h3,
                    w_b):
    @pl.when(pl.program_id(0) == 0)
    def _():
        w_b[...] = w_ref[...].astype(BF16)

    acc = jnp.dot(ys_ref[...], w_b[0:SSD_INNER, :], preferred_element_type=F32)
    acc = acc + jnp.dot(yh_ref[...], w_b[SSD_INNER:SSD_INNER + HG_WIDTH, :], preferred_element_type=F32)
    acc = acc + jnp.dot(yl_ref[...], w_b[SSD_INNER + HG_WIDTH:, :], preferred_element_type=F32)
    xn = x_ref[...] + g_ref[...] * acc
    xo_ref[...] = xn
    _store_planes((h0, h1, h2, h3), _norm_mod(xn, nw_ref[...], sh_ref[...], sc_ref[...]))


def _outproj(y_ssd, y_hg, y_lru, x2, w_out, layer, nw, mod3, seq):
    n, d = x2.shape
    tm = TM_OUTPROJ
    tpb = seq // tm

    def modspec(j):
        return pl.BlockSpec((None, 1, d), lambda i: ((i // tpb) * N_ADA + j, 0, 0))

    outs = pl.pallas_call(
        _outproj_kernel,
        grid=(n // tm,),
        in_specs=[
            pl.BlockSpec((tm, SSD_INNER), lambda i: (i, 0)),
            pl.BlockSpec((tm, HG_WIDTH), lambda i: (i, 0)),
            pl.BlockSpec((tm, LRU_WIDTH), lambda i: (i, 0)),
            pl.BlockSpec((tm, d), lambda i: (i, 0)),
            pl.BlockSpec((None,) + w_out.shape[1:], lambda i: (layer, 0, 0), pipeline_mode=pl.Buffered(1)),
            modspec(2),
            pl.BlockSpec((1, d), lambda i: (0, 0)),
            modspec(3), modspec(4),
        ],
        out_specs=[pl.BlockSpec((tm, d), lambda i: (i, 0))] + [_plane_spec(tm)] * N_PLANES,
        out_shape=[jax.ShapeDtypeStruct((n, d), F32)] + [jax.ShapeDtypeStruct((n, LANES), jnp.uint32)] * N_PLANES,
        scratch_shapes=[pltpu.VMEM(w_out.shape[1:], BF16)],
        compiler_params=_cparams("arbitrary"),
        name="outproj",
    )(y_ssd, y_hg, y_lru, x2, w_out, mod3, nw, mod3, mod3)
    return outs[0], tuple(outs[1:])


def _plane_spec(tm):
    return pl.BlockSpec((tm, LANES), lambda i: (i, 0))


def _router_kernel(h0, h1, h2, h3, rw_ref, rb_ref, eid_ref, rank_ref, wt_ref, cnt_ref, carry, wscr):
    tm = h0.shape[0]

    @pl.when(pl.program_id(0) == 0)
    def _():
        carry[...] = jnp.zeros_like(carry)

    hb = _load_planes((h0, h1, h2, h3)).astype(BF16)
    logit_t = sum(lax.dot_general(part, hb, (((1,), (1,)), ((), ())), preferred_element_type=F32)
                  for part in _split3(rw_ref[...]))
    score = _sigmoid(logit_t)
    sel = score + rb_ref[...]
    neg_inf = jnp.float32(-jnp.inf)
    io_g = lax.broadcasted_iota(jnp.int32, (E_PER_GROUP, tm), 0)
    blocks, gscore = [], []
    for g in range(N_EXPERT_GROUPS):
        blk = sel[g * E_PER_GROUP:(g + 1) * E_PER_GROUP, :]
        m1 = jnp.max(blk, axis=0, keepdims=True)
        i1 = jnp.min(jnp.where(blk == m1, io_g, E_PER_GROUP), axis=0, keepdims=True)
        m2 = jnp.max(jnp.where(io_g == i1, neg_inf, blk), axis=0, keepdims=True)
        blocks.append(blk)
        gscore.append(m1 + m2)
    masked = []
    for g in range(N_EXPERT_GROUPS):
        rank = jnp.zeros((1, tm), jnp.int32)
        for o in range(N_EXPERT_GROUPS):
            if o == g:
                continue
            beats = (gscore[o] > gscore[g]) | ((gscore[o] == gscore[g]) & (o < g))
            rank = rank + beats.astype(jnp.int32)
        masked.append(jnp.where(rank < TOPK_GROUPS, blocks[g], MASK_SCORE))
    val = jnp.concatenate(masked, axis=0)
    io_e = lax.broadcasted_iota(jnp.int32, (N_EXPERTS, tm), 0)
    chosen = jnp.zeros((N_EXPERTS, tm), jnp.bool_)
    picks = []
    for k in range(TOP_K):
        m = jnp.max(val, axis=0, keepdims=True)
        idx = jnp.min(jnp.where(val == m, io_e, N_EXPERTS), axis=0, keepdims=True)
        pick = io_e == idx
        picks.append(pick)
        eid_ref[k:k + 1, :] = idx
        chosen = chosen | pick
        val = jnp.where(pick, neg_inf, val)
    w = jnp.where(chosen, score, 0.0)
    w = w / jnp.sum(w, axis=0, keepdims=True) * ROUTED_SCALE

    chosen_f = chosen.astype(F32)
    earlier = (lax.broadcasted_iota(jnp.int32, (tm, tm), 0) < lax.broadcasted_iota(jnp.int32, (tm, tm), 1))
    before = jnp.dot(chosen_f.astype(BF16), earlier.astype(BF16), preferred_element_type=F32)
    grank = carry[:, 0:1] + before
    wscr[...] = jnp.zeros_like(wscr)
    for k in range(TOP_K):
        rank_ref[k:k + 1, :] = jnp.sum(jnp.where(picks[k], grank, 0.0), axis=0, keepdims=True).astype(jnp.int32)
        wscr[k:k + 1, :] = jnp.sum(jnp.where(picks[k], w, 0.0), axis=0, keepdims=True)
    wt_ref[...] = wscr[...].T
    carry[...] = carry[...] + jnp.sum(chosen_f, axis=1, keepdims=True)
    cnt_ref[...] = carry[...]


def _router(hp, rw_t, rb):
    n = hp[0].shape[0]
    tm = TM_ROUTER
    return pl.pallas_call(
        _router_kernel,
        grid=(n // tm,),
        in_specs=[_plane_spec(tm)] * N_PLANES + [
            pl.BlockSpec(rw_t.shape, lambda i: (0, 0)),
            pl.BlockSpec((N_EXPERTS, 1), lambda i: (0, 0)),
        ],
        out_specs=[
            pl.BlockSpec((TOP_K, tm), lambda i: (0, i)),
            pl.BlockSpec((TOP_K, tm), lambda i: (0, i)),
            pl.BlockSpec((tm, LANES), lambda i: (i, 0)),
            pl.BlockSpec((N_EXPERTS, LANES), lambda i: (0, 0)),
        ],
        out_shape=[
            jax.ShapeDtypeStruct((TOP_K, n), jnp.int32),
            jax.ShapeDtypeStruct((TOP_K, n), jnp.int32),
            jax.ShapeDtypeStruct((n, LANES), F32),
            jax.ShapeDtypeStruct((N_EXPERTS, LANES), F32),
        ],
        scratch_shapes=[pltpu.VMEM((N_EXPERTS, LANES), F32), pltpu.VMEM((LANES, tm), F32)],
        compiler_params=_cparams("arbitrary"),
        name="router",
    )(*hp, rw_t, rb)


def _plan_kernel(n_slots, cnt_ref, eid_ref, rank_ref, dest_ref, be_ref, nb_ref, pad_ref, np_ref):
    cnt = cnt_ref[...].astype(jnp.int32)
    padded = ((cnt + (MOE_BLOCK - 1)) >> MOE_BLOCK_LOG2) << MOE_BLOCK_LOG2
    ri = lax.broadcasted_iota(jnp.int32, (N_EXPERTS, N_EXPERTS), 0)
    ci = lax.broadcasted_iota(jnp.int32, (N_EXPERTS, N_EXPERTS), 1)
    pad_end = jnp.dot((ri >= ci).astype(F32), padded.astype(F32), precision=HI,
                      preferred_element_type=F32).astype(jnp.int32)
    pad_start = pad_end - padded
    eid = eid_ref[...]
    dest = rank_ref[...]
    for e in range(N_EXPERTS):
        dest = dest + jnp.where(eid == e, pad_start[e:e + 1, 0:1], 0)
    dest_ref[...] = dest
    nbp = be_ref.shape[1]
    jpos = lax.broadcasted_iota(jnp.int32, (N_EXPERTS, nbp), 1) * MOE_BLOCK
    be = jnp.minimum(jnp.sum((pad_end[:, 0:1] <= jpos).astype(jnp.int32), axis=0, keepdims=True), N_EXPERTS - 1)
    be_ref[...] = be
    nb_ref[...] = pad_end[N_EXPERTS - 1:N_EXPERTS, :] >> MOE_BLOCK_LOG2
    fill_end = pad_start + cnt
    io_e = lax.broadcasted_iota(jnp.int32, (N_EXPERTS, nbp), 0)
    end_j = jnp.sum(jnp.where(io_e == be, fill_end[:, 0:1], 0), axis=0, keepdims=True)
    pieces = (end_j - jpos[0:1, :] + (XS_PIECE - 1)) >> XS_PIECE_LOG2
    np_ref[...] = jnp.clip(pieces, 0, MOE_BLOCK // XS_PIECE)
    lane = lax.broadcasted_iota(jnp.int32, (N_EXPERTS, LANES), 1)
    piece_end = ((fill_end + (XS_PIECE - 1)) >> XS_PIECE_LOG2) << XS_PIECE_LOG2
    s = fill_end + lane
    pad_ref[...] = jnp.where(s < piece_end, s, n_slots + lane)


def _plan(cnt, eid, rank, n_slots):
    n = eid.shape[1]
    nbp = -(-(n_slots // MOE_BLOCK) // LANES) * LANES
    pad_rows = N_EXPERTS
    return pl.pallas_call(
        functools.partial(_plan_kernel, n_slots),
        grid=(1,),
        in_specs=[
            pl.BlockSpec(cnt.shape, lambda i: (0, 0)),
            pl.BlockSpec(eid.shape, lambda i: (0, 0)),
            pl.BlockSpec(rank.shape, lambda i: (0, 0)),
        ],
        out_specs=[
            pl.BlockSpec((TOP_K, n), lambda i: (0, 0)),
            pl.BlockSpec((1, nbp), lambda i: (0, 0)),
            pl.BlockSpec((1, LANES), lambda i: (0, 0)),
            pl.BlockSpec((pad_rows, LANES), lambda i: (0, 0)),
            pl.BlockSpec((1, nbp), lambda i: (0, 0)),
        ],
        out_shape=[
            jax.ShapeDtypeStruct((TOP_K, n), jnp.int32),
            jax.ShapeDtypeStruct((1, nbp), jnp.int32),
            jax.ShapeDtypeStruct((1, LANES), jnp.int32),
            jax.ShapeDtypeStruct((pad_rows, LANES), jnp.int32),
            jax.ShapeDtypeStruct((1, nbp), jnp.int32),
        ],
        compiler_params=_cparams("arbitrary"),
        name="moe_plan",
    )(cnt, eid, rank)


def _sc_mesh():
    return plsc.VectorSubcoreMesh(core_axis_name="c", subcore_axis_name="s")


def _sc_worker():
    return lax.axis_index("c") * SC_SUBCORES + lax.axis_index("s")


def _sc_dispatch(hp, dest_rows, pad_rows, n_rows):
    n = hp[0].shape[0]
    tiles_per_worker = n // SC_WIN // SC_WORKERS
    pad_per_worker = pad_rows.shape[0] // SC_WORKERS
    zeros = jnp.zeros((SC_WIN, LANES), jnp.uint32)

    def body(*refs):
        h = refs[:N_PLANES]
        dest_hbm, pad_hbm, z_hbm = refs[N_PLANES:N_PLANES + 3]
        xs = refs[N_PLANES + 3:2 * N_PLANES + 3]
        bufs = refs[2 * N_PLANES + 3:3 * N_PLANES + 3]
        ibuf, pbuf, sem = refs[3 * N_PLANES + 3:]
        wid = _sc_worker()

        pltpu.sync_copy(z_hbm, bufs[0])
        pltpu.sync_copy(pad_hbm.at[pl.ds(wid * pad_per_worker, pad_per_worker)], pbuf)
        copies = [pltpu.async_copy(bufs[0], xs[c].at[pbuf.at[r]], sem)
                  for r in range(pad_per_worker) for c in range(N_PLANES)]
        for cp in copies:
            cp.wait()

        @pl.loop(0, tiles_per_worker)
        def _(i):
            tile = wid * tiles_per_worker + i
            pltpu.sync_copy(dest_hbm.at[pl.ds(tile * TOP_K, TOP_K)], ibuf)
            for c in range(N_PLANES):
                pltpu.sync_copy(h[c].at[pl.ds(tile * SC_WIN, SC_WIN)], bufs[c])
            scatters = [pltpu.async_copy(bufs[c], xs[c].at[ibuf.at[k]], sem)
                        for c in range(N_PLANES) for k in range(TOP_K)]
            for cp in scatters:
                cp.wait()

    out_type = tuple(jax.ShapeDtypeStruct((n_rows, LANES), jnp.uint32) for _ in range(N_PLANES))
    scratch = ([pltpu.VMEM((SC_WIN, LANES), jnp.uint32)] * N_PLANES
               + [pltpu.VMEM((TOP_K, LANES), jnp.int32), pltpu.VMEM((pad_per_worker, LANES), jnp.int32),
                  pltpu.SemaphoreType.DMA])
    return pl.kernel(body, out_type=out_type, mesh=_sc_mesh(), scratch_types=scratch,
                     name="moe_sc_dispatch")(*hp, dest_rows, pad_rows, zeros)


def _sc_gather(ysp, dest_rows, n):
    tiles_per_worker = n // SC_WIN // SC_WORKERS

    def body(*refs):
        ys = refs[:N_PLANES]
        dest_hbm, g_hbm = refs[N_PLANES:N_PLANES + 2]
        bufs = refs[N_PLANES + 2:N_PLANES + 2 + SC_GATHER_BUFS]
        ibuf, sem = refs[N_PLANES + 2 + SC_GATHER_BUFS:]
        wid = _sc_worker()

        @pl.loop(0, tiles_per_worker)
        def _(i):
            tile = wid * tiles_per_worker + i
            pltpu.sync_copy(dest_hbm.at[pl.ds(tile * TOP_K, TOP_K)], ibuf)
            for c in range(N_PLANES):
                for k0 in range(0, TOP_K, SC_GATHER_BUFS):
                    gathers = [pltpu.async_copy(ys[c].at[ibuf.at[k0 + j]], bufs[j], sem)
                               for j in range(SC_GATHER_BUFS)]
                    for cp in gathers:
                        cp.wait()
                    stores = [pltpu.async_copy(
                        bufs[j], g_hbm.at[pl.ds(((k0 + j) * N_PLANES + c) * n + tile * SC_WIN, SC_WIN)], sem)
                        for j in range(SC_GATHER_BUFS)]
                    for cp in stores:
                        cp.wait()

    scratch = ([pltpu.VMEM((SC_WIN, LANES), jnp.uint32)] * SC_GATHER_BUFS
               + [pltpu.VMEM((TOP_K, LANES), jnp.int32), pltpu.SemaphoreType.DMA])
    return pl.kernel(body, out_type=jax.ShapeDtypeStruct((TOP_K * N_PLANES * n, LANES), jnp.uint32),
                     mesh=_sc_mesh(), scratch_types=scratch, name="moe_sc_gather")(*ysp, dest_rows)


def _expert_kernel(be_ref, nb_ref, np_ref, *refs):
    xs_hbm = refs[:N_PLANES]
    wg_ref, wu_ref, wd_ref = refs[N_PLANES:N_PLANES + 3]
    ys_refs = refs[N_PLANES + 3:2 * N_PLANES + 3]
    wg_b, wu_b, wd_b, ring, sems = refs[2 * N_PLANES + 3:]
    j = pl.program_id(0)
    nb = nb_ref[0]
    used = j < nb
    new_expert = (j == 0) | (be_ref[j] != be_ref[jnp.maximum(j - 1, 0)])

    def fetch(step, slot, wait):
        for p in range(MOE_BLOCK // XS_PIECE):
            @pl.when(p < np_ref[step])
            def _():
                src = pl.ds(pl.multiple_of(step * MOE_BLOCK + p * XS_PIECE, XS_PIECE), XS_PIECE)
                for c in range(N_PLANES):
                    cp = pltpu.make_async_copy(xs_hbm[c].at[src], ring.at[slot, c, pl.ds(p * XS_PIECE, XS_PIECE)],
                                               sems.at[slot])
                    if wait:
                        cp.wait()
                    else:
                        cp.start()

    @pl.when(j == 0)
    def _():
        ring[...] = jnp.zeros_like(ring)

    for s in range(XS_RING - 1):
        @pl.when((j == 0) & (s < nb))
        def _():
            fetch(s, s, wait=False)

    ahead = j + (XS_RING - 1)

    @pl.when(used & (ahead < nb))
    def _():
        fetch(ahead, lax.rem(ahead, XS_RING), wait=False)

    @pl.when(used & new_expert)
    def _():
        wg_b[...] = wg_ref[...].astype(BF16)
        wu_b[...] = wu_ref[...].astype(BF16)
        wd_b[...] = wd_ref[...].astype(BF16)

    @pl.when(used)
    def _():
        slot = lax.rem(j, XS_RING)
        fetch(j, slot, wait=True)
        x = _unpack_bf16_pairs(jnp.concatenate([ring[slot, c] for c in range(N_PLANES)], axis=1)).astype(BF16)
        a = jnp.dot(x, wg_b[...], preferred_element_type=F32)
        u = jnp.dot(x, wu_b[...], preferred_element_type=F32)
        y = jnp.dot((_silu(a) * u).astype(BF16), wd_b[...], preferred_element_type=F32)
        _store_planes(ys_refs, y)

    @pl.when(jnp.logical_not(used))
    def _():
        for ref in ys_refs:
            ref[...] = jnp.zeros_like(ref)


def _experts(be, nb, npieces, xsp, wg, wu, wd, layer, n_slots):
    d = wg.shape[2]

    def wspec(shape):
        def index(j, be_ref, nb_ref, np_ref):
            return layer, be_ref[jnp.minimum(j, jnp.maximum(nb_ref[0] - 1, 0))], 0, 0
        return pl.BlockSpec((None, None) + shape, index)

    grid_spec = pltpu.PrefetchScalarGridSpec(
        num_scalar_prefetch=3,
        grid=(n_slots // MOE_BLOCK,),
        in_specs=[pl.BlockSpec(memory_space=pl.ANY)] * N_PLANES
        + [wspec((d, D_EXPERT)), wspec((d, D_EXPERT)), wspec((D_EXPERT, d))],
        out_specs=[pl.BlockSpec((MOE_BLOCK, LANES), lambda j, be_ref, nb_ref, np_ref: (j, 0))] * N_PLANES,
        scratch_shapes=[pltpu.VMEM((d, D_EXPERT), BF16), pltpu.VMEM((d, D_EXPERT), BF16),
                        pltpu.VMEM((D_EXPERT, d), BF16),
                        pltpu.VMEM((XS_RING, N_PLANES, MOE_BLOCK, LANES), jnp.uint32),
                        pltpu.SemaphoreType.DMA((XS_RING,))],
    )
    return pl.pallas_call(
        _expert_kernel,
        grid_spec=grid_spec,
        out_shape=[jax.ShapeDtypeStruct((n_slots, LANES), jnp.uint32)] * N_PLANES,
        compiler_params=_cparams("arbitrary"),
        name="moe_experts",
    )(be, nb, npieces, *xsp, wg, wu, wd)


def _combine_kernel(final, g_ref, wt_ref, h0, h1, h2, h3, sg_ref, su_ref, sd_ref, x_ref, gate_ref, fw_ref, o_ref,
                    sg_b, su_b, sd_b):
    @pl.when(pl.program_id(0) == 0)
    def _():
        sg_b[...] = sg_ref[...].astype(BF16)
        su_b[...] = su_ref[...].astype(BF16)
        sd_b[...] = sd_ref[...].astype(BF16)

    hb = _load_planes((h0, h1, h2, h3)).astype(BF16)
    a = jnp.dot(hb, sg_b[...], preferred_element_type=F32)
    u = jnp.dot(hb, su_b[...], preferred_element_type=F32)
    acc = jnp.dot((_silu(a) * u).astype(BF16), sd_b[...], preferred_element_type=F32)
    wt = wt_ref[...]
    for k in range(TOP_K):
        rows = _unpack_bf16_pairs(jnp.concatenate([g_ref[k * N_PLANES + c] for c in range(N_PLANES)], axis=1))
        acc = acc + wt[:, k:k + 1] * rows
    xn = x_ref[...] + gate_ref[...] * acc
    if final:
        ms = jnp.mean(xn * xn, axis=-1, keepdims=True)
        xn = xn * lax.rsqrt(ms + EPS) * fw_ref[...]
    o_ref[...] = xn


def _combine(g, wt, hp, sg, su, sd, layer, x2, mod3, fw, seq, final):
    n, d = x2.shape
    tm = TM_COMBINE
    tpb = seq // tm

    def wspec(w):
        return pl.BlockSpec((None,) + w.shape[1:], lambda i: (layer, 0, 0), pipeline_mode=pl.Buffered(1))

    return pl.pallas_call(
        functools.partial(_combine_kernel, final),
        grid=(n // tm,),
        in_specs=[
            pl.BlockSpec((TOP_K * N_PLANES, tm, LANES), lambda i: (0, i, 0)),
            pl.BlockSpec((tm, LANES), lambda i: (i, 0)),
        ] + [_plane_spec(tm)] * N_PLANES + [
            wspec(sg), wspec(su), wspec(sd),
            pl.BlockSpec((tm, d), lambda i: (i, 0)),
            pl.BlockSpec((None, 1, d), lambda i: ((i // tpb) * N_ADA + 5, 0, 0)),
            pl.BlockSpec((1, d), lambda i: (0, 0)),
        ],
        out_specs=pl.BlockSpec((tm, d), lambda i: (i, 0)),
        out_shape=jax.ShapeDtypeStruct((n, d), F32),
        scratch_shapes=[pltpu.VMEM(sg.shape[1:], BF16), pltpu.VMEM(su.shape[1:], BF16),
                        pltpu.VMEM(sd.shape[1:], BF16)],
        compiler_params=_cparams("arbitrary"),
        name="moe_combine",
    )(g.reshape(TOP_K * N_PLANES, n, LANES), wt, *hp, sg, su, sd, x2, mod3, fw)


def _blockdiag_pairs(w):
    z = jnp.zeros((LRU_BLOCK_W, LRU_BLOCK_W), w.dtype)
    tiles = []
    for j in range(LRU_BLOCKS // 2):
        top = jnp.concatenate([w[2 * j], z], axis=1)
        bot = jnp.concatenate([z, w[2 * j + 1]], axis=1)
        tiles.append(jnp.concatenate([top, bot], axis=0))
    return jnp.stack(tiles).astype(BF16)


def _hg_level_masks():
    ch = HG_CHUNK
    msk = np.zeros((HG_LEVELS, ch, ch), np.float32)
    for lvl in range(HG_LEVELS):
        half = 1 << lvl
        for t in range(ch):
            base = (t // (2 * half)) * (2 * half)
            if (t // half) % 2 == 1:
                msk[lvl, t, base:base + half] = 1.0
    later = ((np.arange(ch)[None, :, None] >> np.arange(HG_LEVELS)[:, None, None]) & 1).astype(np.float32)
    later = np.broadcast_to(later, (HG_LEVELS, ch, HG_EXPAND))
    sgn = (2.0 * later - 1.0) * np.float32(LOG2E)
    return jnp.asarray(msk), jnp.asarray(later), jnp.asarray(sgn, dtype=F32)


def _ssd_expand():
    e = np.zeros((LANES, SSD_INNER), np.float32)
    for h in range(SSD_HEADS):
        e[h, h * SSD_HEADDIM:(h + 1) * SSD_HEADDIM] = 1.0
    return jnp.asarray(np.concatenate([e] * 3, axis=0), dtype=BF16)


def _pad_lanes(v, width):
    return jnp.pad(v, (0, width - v.shape[0])).reshape(1, width)


def _layer_params(l, w_in, ssd_conv_w, ssd_conv_b, ssd_dt_bias, ssd_a_log, ssd_d, ssd_norm_w, hg_lower_bounds,
                  hg_norm_w, lru_conv_w, lru_conv_b, lru_wa, lru_ba, lru_wx, lru_bx, lru_lambda, lru_norm_w):
    wi = w_in[l]
    dt0 = SSD_INNER + SSD_INNER + 2 * SSD_GROUPS * SSD_STATE
    w_cat = jnp.concatenate([wi[:, :dt0], wi[:, dt0 + SSD_HEADS:], wi[:, dt0:dt0 + SSD_HEADS]], axis=1)
    w_cat = jnp.pad(w_cat, ((0, 0), (0, U_WIDTH - w_cat.shape[1]))).astype(BF16)
    msk, later, sgn = _hg_level_masks()
    return dict(
        w_cat=w_cat,
        cwx=ssd_conv_w[l][:, :SSD_INNER], cbx=ssd_conv_b[l][:SSD_INNER].reshape(1, -1),
        cwb=ssd_conv_w[l][:, SSD_INNER:], cbb=ssd_conv_b[l][SSD_INNER:].reshape(1, -1),
        dtb=_pad_lanes(ssd_dt_bias[l], LANES), alog=_pad_lanes(ssd_a_log[l], LANES),
        dful=jnp.repeat(ssd_d[l], SSD_HEADDIM).reshape(1, -1), ssd_nw=ssd_norm_w[l].reshape(1, -1),
        expand=_ssd_expand(),
        hg_lb=hg_lower_bounds, hg_nw=hg_norm_w[l].reshape(1, -1), hg_msk=msk, hg_later=later, hg_sgn=sgn,
        lru_cw=lru_conv_w[l], lru_cb=lru_conv_b[l].reshape(1, -1),
        lru_wa=_blockdiag_pairs(lru_wa[l]), lru_ba=lru_ba[l].reshape(1, -1),
        lru_wx=_blockdiag_pairs(lru_wx[l]), lru_bx=lru_bx[l].reshape(1, -1),
        lru_lam=lru_lambda[l].reshape(1, -1), lru_nw=lru_norm_w[l].reshape(1, -1),
    )


def kernel(x, c, ada_w, ada_b, norm_mix_w, norm_ffn_w, w_in, ssd_conv_w, ssd_conv_b, ssd_dt_bias, ssd_a_log, ssd_d, ssd_norm_w, hg_lower_bounds, hg_norm_w, lru_conv_w, lru_conv_b, lru_wa, lru_ba, lru_wx, lru_bx, lru_lambda, lru_norm_w, w_out, router_w, router_bias, exp_gate, exp_up, exp_down, sh_gate, sh_up, sh_down, final_norm_w):
    bsz, seq, d = x.shape
    depth = ada_w.shape[0]
    assert d == D_MODEL and seq % TM_OUTPROJ == 0 and seq % SSD_CHUNK == 0 and seq % R_HG == 0
    n = bsz * seq
    assert n % (SC_WIN * SC_WORKERS) == 0
    n_slots = n * TOP_K + N_EXPERTS * MOE_BLOCK
    x2 = x.reshape(n, d)
    mod = _adaln(c, ada_w, ada_b)
    fw = final_norm_w.reshape(1, d)
    for l in range(depth):
        p = _layer_params(l, w_in, ssd_conv_w, ssd_conv_b, ssd_dt_bias, ssd_a_log, ssd_d, ssd_norm_w,
                          hg_lower_bounds, hg_norm_w, lru_conv_w, lru_conv_b, lru_wa, lru_ba, lru_wx, lru_bx,
                          lru_lambda, lru_norm_w)
        mod3 = mod[l].reshape(bsz * N_ADA, 1, d)
        u = _inproj(x2, norm_mix_w[l].reshape(1, d), mod3, p["w_cat"], seq)
        y_ssd = _ssd(u, p, bsz, seq)
        y_hg, y_lru = _hgrn2_lru(u, p, l, bsz, seq)
        x2, hp = _outproj(y_ssd, y_hg, y_lru, x2, w_out, l, norm_ffn_w[l].reshape(1, d), mod3, seq)
        eid, rank, wt, cnt = _router(hp, router_w[l].T, router_bias[l].reshape(N_EXPERTS, 1))
        dest, be, nb, pad_rows, npieces = _plan(cnt, eid, rank, n_slots)
        dest_rows = dest.reshape(TOP_K, n // LANES, LANES).transpose(1, 0, 2).reshape(n // LANES * TOP_K, LANES)
        xsp = _sc_dispatch(hp, dest_rows, pad_rows, n_slots + LANES)
        ysp = _experts(be.reshape(-1), nb[0, :1], npieces.reshape(-1), xsp, exp_gate, exp_up, exp_down, l, n_slots)
        g = _sc_gather(ysp, dest_rows, n)
        x2 = _combine(g, wt, hp, sh_gate, sh_up, sh_down, l, x2, mod3, fw, seq, final=(l == depth - 1))
    return x2.reshape(bsz, seq, d)
```

```python
import functools

import jax
import jax.numpy as jnp
import numpy as np
from jax import lax
from jax.experimental import pallas as pl
from jax.experimental.pallas import tpu as pltpu
from jax.experimental.pallas import tpu_sc as plsc

F32 = jnp.float32
BF16 = jnp.bfloat16
HI = lax.Precision.HIGHEST
F32_TINY = float(np.finfo(np.float32).tiny)
LOG2E = float(np.log2(np.e))

LANES = 128
SUBLANES = 8
VMEM_LIMIT_BYTES = 56 * 1024 * 1024

D_MODEL = 1024
EPS = 1e-6
N_ADA = 6
CONV_WIDTH = 4
SSD_INNER = 1024
SSD_HEADDIM = 64
SSD_HEADS = 16
SSD_GROUPS = 2
SSD_STATE = 128
SSD_CHUNK = 128
SSD_GROUP_W = SSD_INNER // SSD_GROUPS
HG_WIDTH = 512
HG_EXPAND = 128
HG_HEADS = 4
HG_CHUNK = 128
HG_LEVELS = 7
LRU_WIDTH = 512
LRU_BLOCKS = 8
LRU_BLOCK_W = 64
LRU_C = 8.0
N_EXPERTS = 64
TOP_K = 8
N_EXPERT_GROUPS = 8
E_PER_GROUP = 8
TOPK_GROUPS = 4
D_EXPERT = 256
ROUTED_SCALE = 2.5
MASK_SCORE = -1.0e4

COL_Z = 0
COL_XS = 1024
COL_BC = 2048
COL_HQ = 2560
COL_HF = 3072
COL_HV = 3584
COL_HG = 4096
COL_LG = 4608
COL_LX = 5120
COL_DT = 5632
U_WIDTH = 5760

TM_INPROJ = 512
TM_OUTPROJ = 1024
TM_ROUTER = 512
TM_COMBINE = 512
SC_CORES = 2
SC_SUBCORES = 16
SC_WORKERS = SC_CORES * SC_SUBCORES
SC_WIN = LANES
SC_GATHER_BUFS = 4
MOE_BLOCK_LOG2 = 10
MOE_BLOCK = 1 << MOE_BLOCK_LOG2
XS_RING = 3
XS_PIECE_LOG2 = 7
XS_PIECE = 1 << XS_PIECE_LOG2
R_HG = 1024
R_SSD = 1024


def _cparams(*sem):
    return pltpu.CompilerParams(dimension_semantics=sem, vmem_limit_bytes=VMEM_LIMIT_BYTES)


def _sigmoid(x):
    return 0.5 * jnp.tanh(0.5 * x) + 0.5


def _silu(x):
    h = 0.5 * x
    return h + h * jnp.tanh(h)


def _split3(x):
    x1 = x.astype(BF16)
    r1 = x - x1.astype(F32)
    x2 = r1.astype(BF16)
    x3 = (r1 - x2.astype(F32)).astype(BF16)
    return x1, x2, x3


def _softplus(x):
    return jnp.maximum(x, 0.0) + jnp.log1p(jnp.exp(-jnp.abs(x)))


def _norm_mod(x, nw, shift, scale):
    ms = jnp.mean(x * x, axis=-1, keepdims=True)
    y = x * lax.rsqrt(ms + EPS) * nw
    return y * (1.0 + scale) + shift


_HI16 = np.uint32(0xFFFF0000)


def _pack_bf16_pairs(x):
    half = x.shape[1] // 2
    bits = lax.bitcast_convert_type(x.astype(BF16).astype(F32), jnp.uint32)
    return (bits[:, :half] & _HI16) | (bits[:, half:] >> 16)


def _unpack_bf16_pairs(w):
    hi = lax.bitcast_convert_type(w & _HI16, F32)
    lo = lax.bitcast_convert_type(w << 16, F32)
    return jnp.concatenate([hi, lo], axis=1)


N_PLANES = D_MODEL // 2 // LANES


def _store_planes(refs, x):
    packed = _pack_bf16_pairs(x)
    for c, ref in enumerate(refs):
        ref[...] = packed[:, c * LANES:(c + 1) * LANES]


def _load_planes(refs):
    return _unpack_bf16_pairs(jnp.concatenate([ref[...] for ref in refs], axis=1))


def _ada_kernel(c_ref, w_ref, b_ref, o_ref):
    c = c_ref[...]
    o_ref[...] = jnp.dot(_silu(c), w_ref[...], precision=HI, preferred_element_type=F32) + b_ref[...]


def _adaln(c, ada_w, ada_b):
    depth, d, n6 = ada_w.shape
    bsz = c.shape[0]
    tn = 1536
    return pl.pallas_call(
        _ada_kernel,
        grid=(depth, n6 // tn),
        in_specs=[
            pl.BlockSpec((bsz, d), lambda l, j: (0, 0)),
            pl.BlockSpec((None, d, tn), lambda l, j: (l, 0, j)),
            pl.BlockSpec((None, 1, tn), lambda l, j: (l, 0, j)),
        ],
        out_specs=pl.BlockSpec((None, bsz, tn), lambda l, j: (l, 0, j)),
        out_shape=jax.ShapeDtypeStruct((depth, bsz, n6), F32),
        compiler_params=_cparams("arbitrary", "arbitrary"),
        name="adaln_mod",
    )(c, ada_w, ada_b.reshape(depth, 1, n6))


def _inproj_kernel(x_ref, nw_ref, sh_ref, sc_ref, w_ref, o_ref):
    h = _norm_mod(x_ref[...], nw_ref[...], sh_ref[...], sc_ref[...])
    o_ref[...] = jnp.dot(h.astype(BF16), w_ref[...], preferred_element_type=F32)


def _inproj(x2, nw, mod3, w_cat, seq):
    n, d = x2.shape
    tm = TM_INPROJ
    tpb = seq // tm
    return pl.pallas_call(
        _inproj_kernel,
        grid=(n // tm,),
        in_specs=[
            pl.BlockSpec((tm, d), lambda i: (i, 0)),
            pl.BlockSpec((1, d), lambda i: (0, 0)),
            pl.BlockSpec((None, 1, d), lambda i: ((i // tpb) * N_ADA + 0, 0, 0)),
            pl.BlockSpec((None, 1, d), lambda i: ((i // tpb) * N_ADA + 1, 0, 0)),
            pl.BlockSpec((d, U_WIDTH), lambda i: (0, 0), pipeline_mode=pl.Buffered(1)),
        ],
        out_specs=pl.BlockSpec((tm, U_WIDTH), lambda i: (i, 0)),
        out_shape=jax.ShapeDtypeStruct((n, U_WIDTH), F32),
        compiler_params=_cparams("arbitrary"),
        name="inproj",
    )(x2, nw, mod3, mod3, w_cat)


def _causal_conv(cur_ref, ext, tail, cw_ref, cb_ref):
    rows, width = cur_ref.shape
    cur = cur_ref[...]
    ext[0:SUBLANES, :] = tail[...]
    ext[SUBLANES:SUBLANES + rows, :] = cur
    tail[...] = cur[rows - SUBLANES:rows, :]
    groups = ext[...].reshape(rows // SUBLANES + 1, SUBLANES, width)
    sub = lax.broadcasted_iota(jnp.int32, (1, SUBLANES, width), 1)
    acc = cb_ref[...] + cur * cw_ref[CONV_WIDTH - 1:CONV_WIDTH, :]
    for d in range(1, CONV_WIDTH):
        rot = pltpu.roll(groups, d, 1)
        back = jnp.where(sub < d, rot[:-1], rot[1:]).reshape(rows, width)
        acc = acc + back * cw_ref[CONV_WIDTH - 1 - d:CONV_WIDTH - d, :]
    return acc


def _ssd_kernel(z_ref, xs_ref, bc_ref, dt_ref, cwx_ref, cbx_ref, cwb_ref, cbb_ref, dtb_ref, alog_ref,
                dful_ref, nw_ref, e_ref, o_ref, extx, extb, tailx, tailb, hstate, convx, convb):
    @pl.when(pl.program_id(1) == 0)
    def _():
        tailx[...] = jnp.zeros_like(tailx)
        tailb[...] = jnp.zeros_like(tailb)
        hstate[...] = jnp.zeros_like(hstate)

    convx[...] = _causal_conv(xs_ref, extx, tailx, cwx_ref, cbx_ref)
    convb[...] = _causal_conv(bc_ref, extb, tailb, cwb_ref, cbb_ref)
    for j in range(xs_ref.shape[0] // SSD_CHUNK):
        _ssd_chunk(slice(j * SSD_CHUNK, (j + 1) * SSD_CHUNK), z_ref, convx, convb, dt_ref, dtb_ref, alog_ref,
                   dful_ref, nw_ref, e_ref, o_ref, hstate)


def _ssd_chunk(rs, z_ref, convx, convb, dt_ref, dtb_ref, alog_ref, dful_ref, nw_ref, e_ref, o_ref, hstate):
    q = SSD_CHUNK
    xs = _silu(convx[rs, :])
    bc = _silu(convb[rs, :])

    dt = _softplus(dt_ref[rs, :] + dtb_ref[...])
    a = dt * (-LOG2E * jnp.exp(alog_ref[...]))
    ri = lax.broadcasted_iota(jnp.int32, (q, q), 0)
    ci = lax.broadcasted_iota(jnp.int32, (q, q), 1)
    tril = ri >= ci
    tril_f = tril.astype(F32)
    tril_b = tril.astype(BF16)
    acum = jnp.dot(jnp.concatenate([tril_b] * 3, axis=1), jnp.concatenate(_split3(a), axis=0),
                   preferred_element_type=F32)
    acum_t = acum.T
    expand3 = e_ref[...]
    dt_full = jnp.dot(jnp.concatenate(_split3(dt), axis=1), expand3, preferred_element_type=F32)
    acum_full = jnp.dot(jnp.concatenate(_split3(acum), axis=1), expand3, preferred_element_type=F32)
    alast_full = acum_full[q - 1:q, :]

    xdt = xs * dt_full
    xdt_b = xdt.astype(BF16)
    exp_a = jnp.exp2(acum_full)
    xd_b = (xdt * jnp.exp2(alast_full - acum_full)).astype(BF16)
    state_decay = jnp.exp2(alast_full)
    left = lax.broadcasted_iota(jnp.int32, (q, LANES), 1) < SSD_HEADDIM
    zero_b = jnp.zeros((q, LANES), BF16)

    ys = []
    for g in range(SSD_GROUPS):
        b_g = bc[:, g * SSD_STATE:(g + 1) * SSD_STATE]
        c_g = bc[:, (SSD_GROUPS + g) * SSD_STATE:(SSD_GROUPS + g + 1) * SSD_STATE]
        c_b = c_g.astype(BF16)
        cb = lax.dot_general(c_b, b_g.astype(BF16), (((1,), (1,)), ((), ())), preferred_element_type=F32)
        cb = cb * tril_f
        cs = slice(g * SSD_GROUP_W, (g + 1) * SSD_GROUP_W)
        h_g = hstate[:, cs]
        y_off = jnp.dot(c_b, h_g.astype(BF16), preferred_element_type=F32) * exp_a[:, cs]
        pieces = []
        for pr in range(SSD_HEADS // SSD_GROUPS // 2):
            h0 = g * (SSD_HEADS // SSD_GROUPS) + 2 * pr
            ms = []
            for h in (h0, h0 + 1):
                col = acum[:, h:h + 1]
                row = acum_t[h:h + 1, :]
                ms.append((cb * jnp.exp2(jnp.minimum(col - row, 0.0))).astype(BF16))
            lhs = jnp.concatenate(ms, axis=1)
            xp = xdt_b[:, h0 * SSD_HEADDIM:(h0 + 2) * SSD_HEADDIM]
            rhs = jnp.concatenate([jnp.where(left, xp, zero_b), jnp.where(left, zero_b, xp)], axis=0)
            pieces.append(jnp.dot(lhs, rhs, preferred_element_type=F32))
        ys.append(jnp.concatenate(pieces, axis=1) + y_off)
        b_t = b_g.T.astype(BF16)
        hstate[:, cs] = h_g * state_decay[:, cs] + jnp.dot(b_t, xd_b[:, cs], preferred_element_type=F32)

    y = jnp.concatenate(ys, axis=1) + xs * dful_ref[...]
    y = y * _silu(z_ref[rs, :])
    outs = []
    for g in range(SSD_GROUPS):
        cs = slice(g * SSD_GROUP_W, (g + 1) * SSD_GROUP_W)
        yg = y[:, cs]
        ms = jnp.mean(yg * yg, axis=-1, keepdims=True)
        outs.append(yg * lax.rsqrt(ms + EPS) * nw_ref[:, cs])
    o_ref[rs, :] = jnp.concatenate(outs, axis=1).astype(o_ref.dtype)


def _ssd(u, p, bsz, seq):
    q = R_SSD
    nc = seq // q
    n = bsz * seq

    def rows(b, c):
        return b * nc + c

    def const(shape):
        return pl.BlockSpec(shape, lambda b, c: (0,) * len(shape))

    return pl.pallas_call(
        _ssd_kernel,
        grid=(bsz, nc),
        in_specs=[
            pl.BlockSpec((q, SSD_INNER), lambda b, c: (rows(b, c), COL_Z // SSD_INNER)),
            pl.BlockSpec((q, SSD_INNER), lambda b, c: (rows(b, c), COL_XS // SSD_INNER)),
            pl.BlockSpec((q, 512), lambda b, c: (rows(b, c), COL_BC // 512)),
            pl.BlockSpec((q, LANES), lambda b, c: (rows(b, c), COL_DT // LANES)),
            const((CONV_WIDTH, SSD_INNER)), const((1, SSD_INNER)),
            const((CONV_WIDTH, 512)), const((1, 512)),
            const((1, LANES)), const((1, LANES)),
            const((1, SSD_INNER)), const((1, SSD_INNER)),
            const((3 * LANES, SSD_INNER)),
        ],
        out_specs=pl.BlockSpec((q, SSD_INNER), lambda b, c: (rows(b, c), 0)),
        out_shape=jax.ShapeDtypeStruct((n, SSD_INNER), BF16),
        scratch_shapes=[
            pltpu.VMEM((q + SUBLANES, SSD_INNER), F32),
            pltpu.VMEM((q + SUBLANES, 512), F32),
            pltpu.VMEM((SUBLANES, SSD_INNER), F32),
            pltpu.VMEM((SUBLANES, 512), F32),
            pltpu.VMEM((SSD_STATE, SSD_INNER), F32),
            pltpu.VMEM((q, SSD_INNER), F32),
            pltpu.VMEM((q, 512), F32),
        ],
        compiler_params=_cparams("arbitrary", "arbitrary"),
        name="ssd_mixer",
    )(u, u, u, u, p["cwx"], p["cbx"], p["cwb"], p["cbb"], p["dtb"], p["alog"], p["dful"], p["ssd_nw"], p["expand"])


def _boundary_rows(b, lvl):
    half = 1 << lvl
    parts = []
    for v in range(b.shape[0] // SUBLANES):
        r0 = v * SUBLANES
        if 2 * half >= SUBLANES:
            src = (r0 // (2 * half)) * (2 * half) + half - 1
            parts.append(jnp.broadcast_to(b[src:src + 1, :], (SUBLANES, b.shape[1])))
        else:
            sub = lax.broadcasted_iota(jnp.int32, (SUBLANES, b.shape[1]), 0)
            piece = None
            for g in range(SUBLANES // (2 * half)):
                src = r0 + g * 2 * half + half - 1
                cand = jnp.broadcast_to(b[src:src + 1, :], (SUBLANES, b.shape[1]))
                piece = cand if piece is None else jnp.where(sub >= g * 2 * half, cand, piece)
            parts.append(piece)
    return jnp.concatenate(parts, axis=0)


def _hgrn2_lru_kernel(layer, q_ref, f_ref, v_ref, g_ref, lb_ref, nw_ref, msk_ref, half_ref, sgn_ref,
                      lg_ref, lx_ref, cw_ref, cb_ref, wa_ref, ba_ref, wx_ref, bx_ref, lam_ref, lnw_ref,
                      o_ref, ol_ref, state_t, ext, tail, hcarry):
    t = pl.program_id(1)
    ch = HG_CHUNK

    @pl.when(t == 0)
    def _():
        state_t[...] = jnp.zeros_like(state_t)
        tail[...] = jnp.zeros_like(tail)
        hcarry[...] = jnp.zeros_like(hcarry)

    _lru_tile(lg_ref, lx_ref, cw_ref, cb_ref, wa_ref, ba_ref, wx_ref, bx_ref, lam_ref, lnw_ref, ol_ref,
              ext, tail, hcarry)

    lrows = [lb_ref[j:j + 1, :] for j in range(lb_ref.shape[0])]
    mx = functools.reduce(jnp.maximum, lrows)
    es = [jnp.exp(r - mx) for r in lrows]
    den = functools.reduce(lambda a_, b_: a_ + b_, es)
    lb = jnp.zeros_like(mx)
    for j in range(1, layer + 1):
        lb = lb + es[j] / den
    one_minus_lb = 1.0 - lb
    nw = nw_ref[...]

    ri = lax.broadcasted_iota(jnp.int32, (ch, ch), 0)
    ci = lax.broadcasted_iota(jnp.int32, (ch, ch), 1)
    tril3 = jnp.concatenate([(ri >= ci).astype(BF16)] * 3, axis=1)
    rowi = lax.broadcasted_iota(jnp.int32, (ch, HG_EXPAND), 0)
    tgt = [((rowi >> lvl) & 1) == 1 for lvl in range(HG_LEVELS)]

    def head_chunk(h, rs):
        cs = slice(h * HG_EXPAND, (h + 1) * HG_EXPAND)
        qq = _silu(q_ref[rs, cs])
        kk = one_minus_lb[:, cs] * _sigmoid(-f_ref[rs, cs])
        logf = jnp.log1p(-kk)
        vv = v_ref[rs, cs]
        vb = vv.astype(BF16)
        b = jnp.dot(tril3, jnp.concatenate(_split3(logf), axis=0),
                    preferred_element_type=F32)
        st = state_t[h]
        o = lax.dot_general((qq * jnp.exp(b)).astype(BF16), st.astype(BF16), (((1,), (1,)), ((), ())),
                            preferred_element_type=F32)
        attn = jnp.zeros((ch, ch), F32)
        for lvl in range(HG_LEVELS):
            if lvl == 0:
                qe = jnp.where(tgt[0], qq * (1.0 - kk), 0.0)
                ke = jnp.where(tgt[0], 0.0, kk)
            else:
                m = _boundary_rows(b, lvl)
                later = half_ref[lvl]
                e = jnp.exp2((b - m) * sgn_ref[lvl])
                prod = jnp.where(tgt[lvl], qq, kk) * e
                qe = prod * later
                ke = prod - qe
            prod = lax.dot_general(qe.astype(BF16), ke.astype(BF16), (((1,), (1,)), ((), ())),
                                   preferred_element_type=F32)
            attn = attn + msk_ref[lvl] * prod
        diag = jnp.sum(qq * kk, axis=-1, keepdims=True)
        o = o + jnp.dot(attn.astype(BF16), vb, preferred_element_type=F32) + diag * vv
        b_last = b[ch - 1:ch, :]
        kd = (kk * jnp.exp(b_last - b)).astype(BF16)
        state_t[h] = st * jnp.exp(b_last) + jnp.dot(vv.T.astype(BF16), kd, preferred_element_type=F32)
        ms = jnp.mean(o * o, axis=-1, keepdims=True)
        y = o * lax.rsqrt(ms + EPS) * nw[:, cs]
        o_ref[rs, cs] = (y * _silu(g_ref[rs, cs])).astype(o_ref.dtype)

    for j in range(q_ref.shape[0] // ch):
        for h in range(HG_HEADS):
            head_chunk(h, slice(j * ch, (j + 1) * ch))


def _lru_tile(g_ref, x_ref, cw_ref, cb_ref, wa_ref, ba_ref, wx_ref, bx_ref, lam_ref, nw_ref, o_ref,
              ext, tail, hcarry):
    rows = x_ref.shape[0]
    xb = _causal_conv(x_ref, ext, tail, cw_ref, cb_ref)
    xbb = xb.astype(BF16)
    npair = LRU_WIDTH // LANES
    ra = jnp.concatenate([jnp.dot(xbb[:, j * LANES:(j + 1) * LANES], wa_ref[j], preferred_element_type=F32)
                          for j in range(npair)], axis=1)
    rx = jnp.concatenate([jnp.dot(xbb[:, j * LANES:(j + 1) * LANES], wx_ref[j], preferred_element_type=F32)
                          for j in range(npair)], axis=1)
    r = _sigmoid(ra + ba_ref[...])
    i = _sigmoid(rx + bx_ref[...])
    log_a = -LRU_C * r * _softplus(-lam_ref[...])
    a = jnp.exp(log_a)
    th = jnp.tanh(log_a)
    s = -2.0 * th
    root = s * lax.rsqrt(jnp.maximum(s * (1.0 - th), F32_TINY))
    u = root * (i * xb)

    ngroups = rows // SUBLANES
    sub = lax.broadcasted_iota(jnp.int32, (1, SUBLANES, LRU_WIDTH), 1)
    acc_a = a.reshape(ngroups, SUBLANES, LRU_WIDTH)
    acc_u = u.reshape(ngroups, SUBLANES, LRU_WIDTH)
    d = 1
    while d < SUBLANES:
        keep = sub >= d
        a_sh = jnp.where(keep, pltpu.roll(acc_a, d, 1), 1.0)
        u_sh = jnp.where(keep, pltpu.roll(acc_u, d, 1), 0.0)
        acc_u = acc_a * u_sh + acc_u
        acc_a = acc_a * a_sh
        d *= 2
    acc_a = acc_a.reshape(rows, LRU_WIDTH)
    acc_u = acc_u.reshape(rows, LRU_WIDTH)
    carry = hcarry[0:1, :]
    groups = []
    for g in range(rows // SUBLANES):
        gs = slice(g * SUBLANES, (g + 1) * SUBLANES)
        hg = acc_a[gs, :] * carry + acc_u[gs, :]
        groups.append(hg)
        carry = hg[SUBLANES - 1:SUBLANES, :]
    h = jnp.concatenate(groups, axis=0)
    hcarry[0:1, :] = carry

    gate = g_ref[...]
    gelu = 0.5 * gate * (1.0 + jnp.tanh(np.sqrt(2.0 / np.pi).astype(np.float32) * (gate + 0.044715 * (gate * gate * gate))))
    y = h * gelu
    ms = jnp.mean(y * y, axis=-1, keepdims=True)
    o_ref[...] = (y * lax.rsqrt(ms + EPS) * nw_ref[...]).astype(o_ref.dtype)


def _hgrn2_lru(u, p, layer, bsz, seq):
    r = R_HG
    nt = seq // r
    n = bsz * seq
    npair = LRU_WIDTH // LANES

    def col(base, width):
        return pl.BlockSpec((r, width), lambda b, t: (b * nt + t, base // width))

    def const(shape):
        return pl.BlockSpec(shape, lambda b, t: (0,) * len(shape))

    def out(width):
        return pl.BlockSpec((r, width), lambda b, t: (b * nt + t, 0))

    return pl.pallas_call(
        functools.partial(_hgrn2_lru_kernel, layer),
        grid=(bsz, nt),
        in_specs=[
            col(COL_HQ, HG_WIDTH), col(COL_HF, HG_WIDTH), col(COL_HV, HG_WIDTH), col(COL_HG, HG_WIDTH),
            const(p["hg_lb"].shape), const((1, HG_WIDTH)), const((HG_LEVELS, HG_CHUNK, HG_CHUNK)),
            const((HG_LEVELS, HG_CHUNK, HG_EXPAND)), const((HG_LEVELS, HG_CHUNK, HG_EXPAND)),
            col(COL_LG, LRU_WIDTH), col(COL_LX, LRU_WIDTH),
            const((CONV_WIDTH, LRU_WIDTH)), const((1, LRU_WIDTH)),
            const((npair, LANES, LANES)), const((1, LRU_WIDTH)),
            const((npair, LANES, LANES)), const((1, LRU_WIDTH)),
            const((1, LRU_WIDTH)), const((1, LRU_WIDTH)),
        ],
        out_specs=[out(HG_WIDTH), out(LRU_WIDTH)],
        out_shape=[jax.ShapeDtypeStruct((n, HG_WIDTH), BF16), jax.ShapeDtypeStruct((n, LRU_WIDTH), BF16)],
        scratch_shapes=[
            pltpu.VMEM((HG_HEADS, HG_EXPAND, HG_EXPAND), F32),
            pltpu.VMEM((r + SUBLANES, LRU_WIDTH), F32),
            pltpu.VMEM((SUBLANES, LRU_WIDTH), F32),
            pltpu.VMEM((SUBLANES, LRU_WIDTH), F32),
        ],
        compiler_params=_cparams("arbitrary", "arbitrary"),
        name="hgrn2_lru_mixer",
    )(u, u, u, u, p["hg_lb"], p["hg_nw"], p["hg_msk"], p["hg_later"], p["hg_sgn"],
      u, u, p["lru_cw"], p["lru_cb"], p["lru_wa"], p["lru_ba"], p["lru_wx"], p["lru_bx"], p["lru_lam"], p["lru_nw"])


def _outproj_kernel(ys_ref, yh_ref, yl_ref, x_ref, w_ref, g_ref, nw_ref, sh_ref, sc_ref, xo_ref, h0, h1, h2, h3,
                    w_b):
    @pl.when(pl.program_id(0) == 0)
    def _():
        w_b[...] = w_ref[...].astype(BF16)

    acc = jnp.dot(ys_ref[...], w_b[0:SSD_INNER, :], preferred_element_type=F32)
    acc = acc + jnp.dot(yh_ref[...], w_b[SSD_INNER:SSD_INNER + HG_WIDTH, :], preferred_element_type=F32)
    acc = acc + jnp.dot(yl_ref[...], w_b[SSD_INNER + HG_WIDTH:, :], preferred_element_type=F32)
    xn = x_ref[...] + g_ref[...] * acc
    xo_ref[...] = xn
    _store_planes((h0, h1, h2, h3), _norm_mod(xn, nw_ref[...], sh_ref[...], sc_ref[...]))


def _outproj(y_ssd, y_hg, y_lru, x2, w_out, layer, nw, mod3, seq):
    n, d = x2.shape
    tm = TM_OUTPROJ
    tpb = seq // tm

    def modspec(j):
        return pl.BlockSpec((None, 1, d), lambda i: ((i // tpb) * N_ADA + j, 0, 0))

    outs = pl.pallas_call(
        _outproj_kernel,
        grid=(n // tm,),
        in_specs=[
            pl.BlockSpec((tm, SSD_INNER), lambda i: (i, 0)),
            pl.BlockSpec((tm, HG_WIDTH), lambda i: (i, 0)),
            pl.BlockSpec((tm, LRU_WIDTH), lambda i: (i, 0)),
            pl.BlockSpec((tm, d), lambda i: (i, 0)),
            pl.BlockSpec((None,) + w_out.shape[1:], lambda i: (layer, 0, 0), pipeline_mode=pl.Buffered(1)),
            modspec(2),
            pl.BlockSpec((1, d), lambda i: (0, 0)),
            modspec(3), modspec(4),
        ],
        out_specs=[pl.BlockSpec((tm, d), lambda i: (i, 0))] + [_plane_spec(tm)] * N_PLANES,
        out_shape=[jax.ShapeDtypeStruct((n, d), F32)] + [jax.ShapeDtypeStruct((n, LANES), jnp.uint32)] * N_PLANES,
        scratch_shapes=[pltpu.VMEM(w_out.shape[1:], BF16)],
        compiler_params=_cparams("arbitrary"),
        name="outproj",
    )(y_ssd, y_hg, y_lru, x2, w_out, mod3, nw, mod3, mod3)
    return outs[0], tuple(outs[1:])


def _plane_spec(tm):
    return pl.BlockSpec((tm, LANES), lambda i: (i, 0))


def _router_kernel(h0, h1, h2, h3, rw_ref, rb_ref, eid_ref, rank_ref, wt_ref, cnt_ref, carry, wscr):
    tm = h0.shape[0]

    @pl.when(pl.program_id(0) == 0)
    def _():
        carry[...] = jnp.zeros_like(carry)

    hb = _load_planes((h0, h1, h2, h3)).astype(BF16)
    logit_t = sum(lax.dot_general(part, hb, (((1,), (1,)), ((), ())), preferred_element_type=F32)
                  for part in _split3(rw_ref[...]))
    score = _sigmoid(logit_t)
    sel = score + rb_ref[...]
    neg_inf = jnp.float32(-jnp.inf)
    io_g = lax.broadcasted_iota(jnp.int32, (E_PER_GROUP, tm), 0)
    blocks, gscore = [], []
    for g in range(N_EXPERT_GROUPS):
        blk = sel[g * E_PER_GROUP:(g + 1) * E_PER_GROUP, :]
        m1 = jnp.max(blk, axis=0, keepdims=True)
        i1 = jnp.min(jnp.where(blk == m1, io_g, E_PER_GROUP), axis=0, keepdims=True)
        m2 = jnp.max(jnp.where(io_g == i1, neg_inf, blk), axis=0, keepdims=True)
        blocks.append(blk)
        gscore.append(m1 + m2)
    masked = []
    for g in range(N_EXPERT_GROUPS):
        rank = jnp.zeros((1, tm), jnp.int32)
        for o in range(N_EXPERT_GROUPS):
            if o == g:
                continue
            beats = (gscore[o] > gscore[g]) | ((gscore[o] == gscore[g]) & (o < g))
            rank = rank + beats.astype(jnp.int32)
        masked.append(jnp.where(rank < TOPK_GROUPS, blocks[g], MASK_SCORE))
    val = jnp.concatenate(masked, axis=0)
    io_e = lax.broadcasted_iota(jnp.int32, (N_EXPERTS, tm), 0)
    chosen = jnp.zeros((N_EXPERTS, tm), jnp.bool_)
    picks = []
    for k in range(TOP_K):
        m = jnp.max(val, axis=0, keepdims=True)
        idx = jnp.min(jnp.where(val == m, io_e, N_EXPERTS), axis=0, keepdims=True)
        pick = io_e == idx
        picks.append(pick)
        eid_ref[k:k + 1, :] = idx
        chosen = chosen | pick
        val = jnp.where(pick, neg_inf, val)
    w = jnp.where(chosen, score, 0.0)
    w = w / jnp.sum(w, axis=0, keepdims=True) * ROUTED_SCALE

    chosen_f = chosen.astype(F32)
    earlier = (lax.broadcasted_iota(jnp.int32, (tm, tm), 0) < lax.broadcasted_iota(jnp.int32, (tm, tm), 1))
    before = jnp.dot(chosen_f.astype(BF16), earlier.astype(BF16), preferred_element_type=F32)
    grank = carry[:, 0:1] + before
    wscr[...] = jnp.zeros_like(wscr)
    for k in range(TOP_K):
        rank_ref[k:k + 1, :] = jnp.sum(jnp.where(picks[k], grank, 0.0), axis=0, keepdims=True).astype(jnp.int32)
        wscr[k:k + 1, :] = jnp.sum(jnp.where(picks[k], w, 0.0), axis=0, keepdims=True)
    wt_ref[...] = wscr[...].T
    carry[...] = carry[...] + jnp.sum(chosen_f, axis=1, keepdims=True)
    cnt_ref[...] = carry[...]


def _router(hp, rw_t, rb):
    n = hp[0].shape[0]
    tm = TM_ROUTER
    return pl.pallas_call(
        _router_kernel,
        grid=(n // tm,),
        in_specs=[_plane_spec(tm)] * N_PLANES + [
            pl.BlockSpec(rw_t.shape, lambda i: (0, 0)),
            pl.BlockSpec((N_EXPERTS, 1), lambda i: (0, 0)),
        ],
        out_specs=[
            pl.BlockSpec((TOP_K, tm), lambda i: (0, i)),
            pl.BlockSpec((TOP_K, tm), lambda i: (0, i)),
            pl.BlockSpec((tm, LANES), lambda i: (i, 0)),
            pl.BlockSpec((N_EXPERTS, LANES), lambda i: (0, 0)),
        ],
        out_shape=[
            jax.ShapeDtypeStruct((TOP_K, n), jnp.int32),
            jax.ShapeDtypeStruct((TOP_K, n), jnp.int32),
            jax.ShapeDtypeStruct((n, LANES), F32),
            jax.ShapeDtypeStruct((N_EXPERTS, LANES), F32),
        ],
        scratch_shapes=[pltpu.VMEM((N_EXPERTS, LANES), F32), pltpu.VMEM((LANES, tm), F32)],
        compiler_params=_cparams("arbitrary"),
        name="router",
    )(*hp, rw_t, rb)


def _plan_kernel(n_slots, cnt_ref, eid_ref, rank_ref, dest_ref, be_ref, nb_ref, pad_ref, np_ref):
    cnt = cnt_ref[...].astype(jnp.int32)
    padded = ((cnt + (MOE_BLOCK - 1)) >> MOE_BLOCK_LOG2) << MOE_BLOCK_LOG2
    ri = lax.broadcasted_iota(jnp.int32, (N_EXPERTS, N_EXPERTS), 0)
    ci = lax.broadcasted_iota(jnp.int32, (N_EXPERTS, N_EXPERTS), 1)
    pad_end = jnp.dot((ri >= ci).astype(F32), padded.astype(F32), precision=HI,
                      preferred_element_type=F32).astype(jnp.int32)
    pad_start = pad_end - padded
    eid = eid_ref[...]
    dest = rank_ref[...]
    for e in range(N_EXPERTS):
        dest = dest + jnp.where(eid == e, pad_start[e:e + 1, 0:1], 0)
    dest_ref[...] = dest
    nbp = be_ref.shape[1]
    jpos = lax.broadcasted_iota(jnp.int32, (N_EXPERTS, nbp), 1) * MOE_BLOCK
    be = jnp.minimum(jnp.sum((pad_end[:, 0:1] <= jpos).astype(jnp.int32), axis=0, keepdims=True), N_EXPERTS - 1)
    be_ref[...] = be
    nb_ref[...] = pad_end[N_EXPERTS - 1:N_EXPERTS, :] >> MOE_BLOCK_LOG2
    fill_end = pad_start + cnt
    io_e = lax.broadcasted_iota(jnp.int32, (N_EXPERTS, nbp), 0)
    end_j = jnp.sum(jnp.where(io_e == be, fill_end[:, 0:1], 0), axis=0, keepdims=True)
    pieces = (end_j - jpos[0:1, :] + (XS_PIECE - 1)) >> XS_PIECE_LOG2
    np_ref[...] = jnp.clip(pieces, 0, MOE_BLOCK // XS_PIECE)
    lane = lax.broadcasted_iota(jnp.int32, (N_EXPERTS, LANES), 1)
    piece_end = ((fill_end + (XS_PIECE - 1)) >> XS_PIECE_LOG2) << XS_PIECE_LOG2
    s = fill_end + lane
    pad_ref[...] = jnp.where(s < piece_end, s, n_slots + lane)


def _plan(cnt, eid, rank, n_slots):
    n = eid.shape[1]
    nbp = -(-(n_slots // MOE_BLOCK) // LANES) * LANES
    pad_rows = N_EXPERTS
    return pl.pallas_call(
        functools.partial(_plan_kernel, n_slots),
        grid=(1,),
        in_specs=[
            pl.BlockSpec(cnt.shape, lambda i: (0, 0)),
            pl.BlockSpec(eid.shape, lambda i: (0, 0)),
            pl.BlockSpec(rank.shape, lambda i: (0, 0)),
        ],
        out_specs=[
            pl.BlockSpec((TOP_K, n), lambda i: (0, 0)),
            pl.BlockSpec((1, nbp), lambda i: (0, 0)),
            pl.BlockSpec((1, LANES), lambda i: (0, 0)),
            pl.BlockSpec((pad_rows, LANES), lambda i: (0, 0)),
            pl.BlockSpec((1, nbp), lambda i: (0, 0)),
        ],
        out_shape=[
            jax.ShapeDtypeStruct((TOP_K, n), jnp.int32),
            jax.ShapeDtypeStruct((1, nbp), jnp.int32),
            jax.ShapeDtypeStruct((1, LANES), jnp.int32),
            jax.ShapeDtypeStruct((pad_rows, LANES), jnp.int32),
            jax.ShapeDtypeStruct((1, nbp), jnp.int32),
        ],
        compiler_params=_cparams("arbitrary"),
        name="moe_plan",
    )(cnt, eid, rank)


def _sc_mesh():
    return plsc.VectorSubcoreMesh(core_axis_name="c", subcore_axis_name="s")


def _sc_worker():
    return lax.axis_index("c") * SC_SUBCORES + lax.axis_index("s")


def _sc_dispatch(hp, dest_rows, pad_rows, n_rows):
    n = hp[0].shape[0]
    tiles_per_worker = n // SC_WIN // SC_WORKERS
    pad_per_worker = pad_rows.shape[0] // SC_WORKERS
    zeros = jnp.zeros((SC_WIN, LANES), jnp.uint32)

    def body(*refs):
        h = refs[:N_PLANES]
        dest_hbm, pad_hbm, z_hbm = refs[N_PLANES:N_PLANES + 3]
        xs = refs[N_PLANES + 3:2 * N_PLANES + 3]
        bufs = refs[2 * N_PLANES + 3:3 * N_PLANES + 3]
        ibuf, pbuf, sem = refs[3 * N_PLANES + 3:]
        wid = _sc_worker()

        pltpu.sync_copy(z_hbm, bufs[0])
        pltpu.sync_copy(pad_hbm.at[pl.ds(wid * pad_per_worker, pad_per_worker)], pbuf)
        copies = [pltpu.async_copy(bufs[0], xs[c].at[pbuf.at[r]], sem)
                  for r in range(pad_per_worker) for c in range(N_PLANES)]
        for cp in copies:
            cp.wait()

        @pl.loop(0, tiles_per_worker)
        def _(i):
            tile = wid * tiles_per_worker + i
            pltpu.sync_copy(dest_hbm.at[pl.ds(tile * TOP_K, TOP_K)], ibuf)
            for c in range(N_PLANES):
                pltpu.sync_copy(h[c].at[pl.ds(tile * SC_WIN, SC_WIN)], bufs[c])
            scatters = [pltpu.async_copy(bufs[c], xs[c].at[ibuf.at[k]], sem)
                        for c in range(N_PLANES) for k in range(TOP_K)]
            for cp in scatters:
                cp.wait()

    out_type = tuple(jax.ShapeDtypeStruct((n_rows, LANES), jnp.uint32) for _ in range(N_PLANES))
    scratch = ([pltpu.VMEM((SC_WIN, LANES), jnp.uint32)] * N_PLANES
               + [pltpu.VMEM((TOP_K, LANES), jnp.int32), pltpu.VMEM((pad_per_worker, LANES), jnp.int32),
                  pltpu.SemaphoreType.DMA])
    return pl.kernel(body, out_type=out_type, mesh=_sc_mesh(), scratch_types=scratch,
                     name="moe_sc_dispatch")(*hp, dest_rows, pad_rows, zeros)


def _sc_gather(ysp, dest_rows, n):
    tiles_per_worker = n // SC_WIN // SC_WORKERS

    def body(*refs):
        ys = refs[:N_PLANES]
        dest_hbm, g_hbm = refs[N_PLANES:N_PLANES + 2]
        bufs = refs[N_PLANES + 2:N_PLANES + 2 + SC_GATHER_BUFS]
        ibuf, sem = refs[N_PLANES + 2 + SC_GATHER_BUFS:]
        wid = _sc_worker()

        @pl.loop(0, tiles_per_worker)
        def _(i):
            tile = wid * tiles_per_worker + i
            pltpu.sync_copy(dest_hbm.at[pl.ds(tile * TOP_K, TOP_K)], ibuf)
            for c in range(N_PLANES):
                for k0 in range(0, TOP_K, SC_GATHER_BUFS):
                    gathers = [pltpu.async_copy(ys[c].at[ibuf.at[k0 + j]], bufs[j], sem)
                               for j in range(SC_GATHER_BUFS)]
                    for cp in gathers:
                        cp.wait()
                    stores = [pltpu.async_copy(
                        bufs[j], g_hbm.at[pl.ds(((k0 + j) * N_PLANES + c) * n + tile * SC_WIN, SC_WIN)], sem)
                        for j in range(SC_GATHER_BUFS)]
                    for cp in stores:
                        cp.wait()

    scratch = ([pltpu.VMEM((SC_WIN, LANES), jnp.uint32)] * SC_GATHER_BUFS
               + [pltpu.VMEM((TOP_K, LANES), jnp.int32), pltpu.SemaphoreType.DMA])
    return pl.kernel(body, out_type=jax.ShapeDtypeStruct((TOP_K * N_PLANES * n, LANES), jnp.uint32),
                     mesh=_sc_mesh(), scratch_types=scratch, name="moe_sc_gather")(*ysp, dest_rows)


def _expert_kernel(be_ref, nb_ref, np_ref, *refs):
    xs_hbm = refs[:N_PLANES]
    wg_ref, wu_ref, wd_ref = refs[N_PLANES:N_PLANES + 3]
    ys_refs = refs[N_PLANES + 3:2 * N_PLANES + 3]
    wg_b, wu_b, wd_b, ring, sems = refs[2 * N_PLANES + 3:]
    j = pl.program_id(0)
    nb = nb_ref[0]
    used = j < nb
    new_expert = (j == 0) | (be_ref[j] != be_ref[jnp.maximum(j - 1, 0)])

    def fetch(step, slot, wait):
        for p in range(MOE_BLOCK // XS_PIECE):
            @pl.when(p < np_ref[step])
            def _():
                src = pl.ds(pl.multiple_of(step * MOE_BLOCK + p * XS_PIECE, XS_PIECE), XS_PIECE)
                for c in range(N_PLANES):
                    cp = pltpu.make_async_copy(xs_hbm[c].at[src], ring.at[slot, c, pl.ds(p * XS_PIECE, XS_PIECE)],
                                               sems.at[slot])
                    if wait:
                        cp.wait()
                    else:
                        cp.start()

    @pl.when(j == 0)
    def _():
        ring[...] = jnp.zeros_like(ring)

    for s in range(XS_RING - 1):
        @pl.when((j == 0) & (s < nb))
        def _():
            fetch(s, s, wait=False)

    ahead = j + (XS_RING - 1)

    @pl.when(used & (ahead < nb))
    def _():
        fetch(ahead, lax.rem(ahead, XS_RING), wait=False)

    @pl.when(used & new_expert)
    def _():
        wg_b[...] = wg_ref[...].astype(BF16)
        wu_b[...] = wu_ref[...].astype(BF16)
        wd_b[...] = wd_ref[...].astype(BF16)

    @pl.when(used)
    def _():
        slot = lax.rem(j, XS_RING)
        fetch(j, slot, wait=True)
        x = _unpack_bf16_pairs(jnp.concatenate([ring[slot, c] for c in range(N_PLANES)], axis=1)).astype(BF16)
        a = jnp.dot(x, wg_b[...], preferred_element_type=F32)
        u = jnp.dot(x, wu_b[...], preferred_element_type=F32)
        y = jnp.dot((_silu(a) * u).astype(BF16), wd_b[...], preferred_element_type=F32)
        _store_planes(ys_refs, y)

    @pl.when(jnp.logical_not(used))
    def _():
        for ref in ys_refs:
            ref[...] = jnp.zeros_like(ref)


def _experts(be, nb, npieces, xsp, wg, wu, wd, layer, n_slots):
    d = wg.shape[2]

    def wspec(shape):
        def index(j, be_ref, nb_ref, np_ref):
            return layer, be_ref[jnp.minimum(j, jnp.maximum(nb_ref[0] - 1, 0))], 0, 0
        return pl.BlockSpec((None, None) + shape, index)

    grid_spec = pltpu.PrefetchScalarGridSpec(
        num_scalar_prefetch=3,
        grid=(n_slots // MOE_BLOCK,),
        in_specs=[pl.BlockSpec(memory_space=pl.ANY)] * N_PLANES
        + [wspec((d, D_EXPERT)), wspec((d, D_EXPERT)), wspec((D_EXPERT, d))],
        out_specs=[pl.BlockSpec((MOE_BLOCK, LANES), lambda j, be_ref, nb_ref, np_ref: (j, 0))] * N_PLANES,
        scratch_shapes=[pltpu.VMEM((d, D_EXPERT), BF16), pltpu.VMEM((d, D_EXPERT), BF16),
                        pltpu.VMEM((D_EXPERT, d), BF16),
                        pltpu.VMEM((XS_RING, N_PLANES, MOE_BLOCK, LANES), jnp.uint32),
                        pltpu.SemaphoreType.DMA((XS_RING,))],
    )
    return pl.pallas_call(
        _expert_kernel,
        grid_spec=grid_spec,
        out_shape=[jax.ShapeDtypeStruct((n_slots, LANES), jnp.uint32)] * N_PLANES,
        compiler_params=_cparams("arbitrary"),
        name="moe_experts",
    )(be, nb, npieces, *xsp, wg, wu, wd)


def _combine_kernel(final, g_ref, wt_ref, h0, h1, h2, h3, sg_ref, su_ref, sd_ref, x_ref, gate_ref, fw_ref, o_ref,
                    sg_b, su_b, sd_b):
    @pl.when(pl.program_id(0) == 0)
    def _():
        sg_b[...] = sg_ref[...].astype(BF16)
        su_b[...] = su_ref[...].astype(BF16)
        sd_b[...] = sd_ref[...].astype(BF16)

    hb = _load_planes((h0, h1, h2, h3)).astype(BF16)
    a = jnp.dot(hb, sg_b[...], preferred_element_type=F32)
    u = jnp.dot(hb, su_b[...], preferred_element_type=F32)
    acc = jnp.dot((_silu(a) * u).astype(BF16), sd_b[...], preferred_element_type=F32)
    wt = wt_ref[...]
    for k in range(TOP_K):
        rows = _unpack_bf16_pairs(jnp.concatenate([g_ref[k * N_PLANES + c] for c in range(N_PLANES)], axis=1))
        acc = acc + wt[:, k:k + 1] * rows
    xn = x_ref[...] + gate_ref[...] * acc
    if final:
        ms = jnp.mean(xn * xn, axis=-1, keepdims=True)
        xn = xn * lax.rsqrt(ms + EPS) * fw_ref[...]
    o_ref[...] = xn


def _combine(g, wt, hp, sg, su, sd, layer, x2, mod3, fw, seq, final):
    n, d = x2.shape
    tm = TM_COMBINE
    tpb = seq // tm

    def wspec(w):
        return pl.BlockSpec((None,) + w.shape[1:], lambda i: (layer, 0, 0), pipeline_mode=pl.Buffered(1))

    return pl.pallas_call(
        functools.partial(_combine_kernel, final),
        grid=(n // tm,),
        in_specs=[
            pl.BlockSpec((TOP_K * N_PLANES, tm, LANES), lambda i: (0, i, 0)),
            pl.BlockSpec((tm, LANES), lambda i: (i, 0)),
        ] + [_plane_spec(tm)] * N_PLANES + [
            wspec(sg), wspec(su), wspec(sd),
            pl.BlockSpec((tm, d), lambda i: (i, 0)),
            pl.BlockSpec((None, 1, d), lambda i: ((i // tpb) * N_ADA + 5, 0, 0)),
            pl.BlockSpec((1, d), lambda i: (0, 0)),
        ],
        out_specs=pl.BlockSpec((tm, d), lambda i: (i, 0)),
        out_shape=jax.ShapeDtypeStruct((n, d), F32),
        scratch_shapes=[pltpu.VMEM(sg.shape[1:], BF16), pltpu.VMEM(su.shape[1:], BF16),
                        pltpu.VMEM(sd.shape[1:], BF16)],
        compiler_params=_cparams("arbitrary"),
        name="moe_combine",
    )(g.reshape(TOP_K * N_PLANES, n, LANES), wt, *hp, sg, su, sd, x2, mod3, fw)


def _blockdiag_pairs(w):
    z = jnp.zeros((LRU_BLOCK_W, LRU_BLOCK_W), w.dtype)
    tiles = []
    for j in range(LRU_BLOCKS // 2):
        top = jnp.concatenate([w[2 * j], z], axis=1)
        bot = jnp.concatenate([z, w[2 * j + 1]], axis=1)
        tiles.append(jnp.concatenate([top, bot], axis=0))
    return jnp.stack(tiles).astype(BF16)


def _hg_level_masks():
    ch = HG_CHUNK
    msk = np.zeros((HG_LEVELS, ch, ch), np.float32)
    for lvl in range(HG_LEVELS):
        half = 1 << lvl
        for t in range(ch):
            base = (t // (2 * half)) * (2 * half)
            if (t // half) % 2 == 1:
                msk[lvl, t, base:base + half] = 1.0
    later = ((np.arange(ch)[None, :, None] >> np.arange(HG_LEVELS)[:, None, None]) & 1).astype(np.float32)
    later = np.broadcast_to(later, (HG_LEVELS, ch, HG_EXPAND))
    sgn = (2.0 * later - 1.0) * np.float32(LOG2E)
    return jnp.asarray(msk), jnp.asarray(later), jnp.asarray(sgn, dtype=F32)


def _ssd_expand():
    e = np.zeros((LANES, SSD_INNER), np.float32)
    for h in range(SSD_HEADS):
        e[h, h * SSD_HEADDIM:(h + 1) * SSD_HEADDIM] = 1.0
    return jnp.asarray(np.concatenate([e] * 3, axis=0), dtype=BF16)


def _pad_lanes(v, width):
    return jnp.pad(v, (0, width - v.shape[0])).reshape(1, width)


def _layer_params(l, w_in, ssd_conv_w, ssd_conv_b, ssd_dt_bias, ssd_a_log, ssd_d, ssd_norm_w, hg_lower_bounds,
                  hg_norm_w, lru_conv_w, lru_conv_b, lru_wa, lru_ba, lru_wx, lru_bx, lru_lambda, lru_norm_w):
    wi = w_in[l]
    dt0 = SSD_INNER + SSD_INNER + 2 * SSD_GROUPS * SSD_STATE
    w_cat = jnp.concatenate([wi[:, :dt0], wi[:, dt0 + SSD_HEADS:], wi[:, dt0:dt0 + SSD_HEADS]], axis=1)
    w_cat = jnp.pad(w_cat, ((0, 0), (0, U_WIDTH - w_cat.shape[1]))).astype(BF16)
    msk, later, sgn = _hg_level_masks()
    return dict(
        w_cat=w_cat,
        cwx=ssd_conv_w[l][:, :SSD_INNER], cbx=ssd_conv_b[l][:SSD_INNER].reshape(1, -1),
        cwb=ssd_conv_w[l][:, SSD_INNER:], cbb=ssd_conv_b[l][SSD_INNER:].reshape(1, -1),
        dtb=_pad_lanes(ssd_dt_bias[l], LANES), alog=_pad_lanes(ssd_a_log[l], LANES),
        dful=jnp.repeat(ssd_d[l], SSD_HEADDIM).reshape(1, -1), ssd_nw=ssd_norm_w[l].reshape(1, -1),
        expand=_ssd_expand(),
        hg_lb=hg_lower_bounds, hg_nw=hg_norm_w[l].reshape(1, -1), hg_msk=msk, hg_later=later, hg_sgn=sgn,
        lru_cw=lru_conv_w[l], lru_cb=lru_conv_b[l].reshape(1, -1),
        lru_wa=_blockdiag_pairs(lru_wa[l]), lru_ba=lru_ba[l].reshape(1, -1),
        lru_wx=_blockdiag_pairs(lru_wx[l]), lru_bx=lru_bx[l].reshape(1, -1),
        lru_lam=lru_lambda[l].reshape(1, -1), lru_nw=lru_norm_w[l].reshape(1, -1),
    )


def kernel(x, c, ada_w, ada_b, norm_mix_w, norm_ffn_w, w_in, ssd_conv_w, ssd_conv_b, ssd_dt_bias, ssd_a_log, ssd_d, ssd_norm_w, hg_lower_bounds, hg_norm_w, lru_conv_w, lru_conv_b, lru_wa, lru_ba, lru_wx, lru_bx, lru_lambda, lru_norm_w, w_out, router_w, router_bias, exp_gate, exp_up, exp_down, sh_gate, sh_up, sh_down, final_norm_w):
    bsz, seq, d = x.shape
    depth = ada_w.shape[0]
    assert d == D_MODEL and seq % TM_OUTPROJ == 0 and seq % SSD_CHUNK == 0 and seq % R_HG == 0
    n = bsz * seq
    assert n % (SC_WIN * SC_WORKERS) == 0
    n_slots = n * TOP_K + N_EXPERTS * MOE_BLOCK
    x2 = x.reshape(n, d)
    mod = _adaln(c, ada_w, ada_b)
    fw = final_norm_w.reshape(1, d)
    for l in range(depth):
        p = _layer_params(l, w_in, ssd_conv_w, ssd_conv_b, ssd_dt_bias, ssd_a_log, ssd_d, ssd_norm_w,
                          hg_lower_bounds, hg_norm_w, lru_conv_w, lru_conv_b, lru_wa, lru_ba, lru_wx, lru_bx,
                          lru_lambda, lru_norm_w)
        mod3 = mod[l].reshape(bsz * N_ADA, 1, d)
        u = _inproj(x2, norm_mix_w[l].reshape(1, d), mod3, p["w_cat"], seq)
        y_ssd = _ssd(u, p, bsz, seq)
        y_hg, y_lru = _hgrn2_lru(u, p, l, bsz, seq)
        x2, hp = _outproj(y_ssd, y_hg, y_lru, x2, w_out, l, norm_ffn_w[l].reshape(1, d), mod3, seq)
        eid, rank, wt, cnt = _router(hp, router_w[l].T, router_bias[l].reshape(N_EXPERTS, 1))
        dest, be, nb, pad_rows, npieces = _plan(cnt, eid, rank, n_slots)
        dest_rows = dest.reshape(TOP_K, n // LANES, LANES).transpose(1, 0, 2).reshape(n // LANES * TOP_K, LANES)
        xsp = _sc_dispatch(hp, dest_rows, pad_rows, n_slots + LANES)
        ysp = _experts(be.reshape(-1), nb[0, :1], npieces.reshape(-1), xsp, exp_gate, exp_up, exp_down, l, n_slots)
        g = _sc_gather(ysp, dest_rows, n)
        x2 = _combine(g, wt, hp, sh_gate, sh_up, sh_down, l, x2, mod3, fw, seq, final=(l == depth - 1))
    return x2.reshape(bsz, seq, d)
```

```python
import functools

import jax
import jax.numpy as jnp
import numpy as np
from jax import lax
from jax.experimental import pallas as pl
from jax.experimental.pallas import tpu as pltpu
from jax.experimental.pallas import tpu_sc as plsc

F32 = jnp.float32
BF16 = jnp.bfloat16
HI = lax.Precision.HIGHEST
F32_TINY = float(np.finfo(np.float32).tiny)
LOG2E = float(np.log2(np.e))

LANES = 128
SUBLANES = 8
VMEM_LIMIT_BYTES = 56 * 1024 * 1024

D_MODEL = 1024
EPS = 1e-6
N_ADA = 6
CONV_WIDTH = 4
SSD_INNER = 1024
SSD_HEADDIM = 64
SSD_HEADS = 16
SSD_GROUPS = 2
SSD_STATE = 128
SSD_CHUNK = 128
SSD_GROUP_W = SSD_INNER // SSD_GROUPS
HG_WIDTH = 512
HG_EXPAND = 128
HG_HEADS = 4
HG_CHUNK = 128
HG_LEVELS = 7
LRU_WIDTH = 512
LRU_BLOCKS = 8
LRU_BLOCK_W = 64
LRU_C = 8.0
N_EXPERTS = 64
TOP_K = 8
N_EXPERT_GROUPS = 8
E_PER_GROUP = 8
TOPK_GROUPS = 4
D_EXPERT = 256
ROUTED_SCALE = 2.5
MASK_SCORE = -1.0e4

COL_Z = 0
COL_XS = 1024
COL_BC = 2048
COL_HQ = 2560
COL_HF = 3072
COL_HV = 3584
COL_HG = 4096
COL_LG = 4608
COL_LX = 5120
COL_DT = 5632
U_WIDTH = 5760

TM_INPROJ = 512
TM_OUTPROJ = 1024
TM_ROUTER = 512
TM_COMBINE = 512
SC_CORES = 2
SC_SUBCORES = 16
SC_WORKERS = SC_CORES * SC_SUBCORES
SC_WIN = LANES
SC_GATHER_BUFS = 4
MOE_BLOCK_LOG2 = 10
MOE_BLOCK = 1 << MOE_BLOCK_LOG2
XS_RING = 3
EXPERT_SUB = 256
XS_PIECE_LOG2 = 7
XS_PIECE = 1 << XS_PIECE_LOG2
R_HG = 1024
R_SSD = 1024


def _cparams(*sem):
    return pltpu.CompilerParams(dimension_semantics=sem, vmem_limit_bytes=VMEM_LIMIT_BYTES)


def _sigmoid(x):
    return 0.5 * jnp.tanh(0.5 * x) + 0.5


def _silu(x):
    h = 0.5 * x
    return h + h * jnp.tanh(h)


def _split3(x):
    x1 = x.astype(BF16)
    r1 = x - x1.astype(F32)
    x2 = r1.astype(BF16)
    x3 = (r1 - x2.astype(F32)).astype(BF16)
    return x1, x2, x3


def _softplus(x):
    return jnp.maximum(x, 0.0) + jnp.log1p(jnp.exp(-jnp.abs(x)))


def _norm_mod(x, nw, shift, scale):
    ms = jnp.mean(x * x, axis=-1, keepdims=True)
    y = x * lax.rsqrt(ms + EPS) * nw
    return y * (1.0 + scale) + shift


_HI16 = np.uint32(0xFFFF0000)


def _pack_bf16_pairs(x):
    half = x.shape[1] // 2
    bits = lax.bitcast_convert_type(x.astype(BF16).astype(F32), jnp.uint32)
    return (bits[:, :half] & _HI16) | (bits[:, half:] >> 16)


def _unpack_bf16_pairs(w):
    hi = lax.bitcast_convert_type(w & _HI16, F32)
    lo = lax.bitcast_convert_type(w << 16, F32)
    return jnp.concatenate([hi, lo], axis=1)


N_PLANES = D_MODEL // 2 // LANES


def _store_planes(refs, x):
    packed = _pack_bf16_pairs(x)
    for c, ref in enumerate(refs):
        ref[...] = packed[:, c * LANES:(c + 1) * LANES]


def _load_planes(refs):
    return _unpack_bf16_pairs(jnp.concatenate([ref[...] for ref in refs], axis=1))


def _ada_kernel(c_ref, w_ref, b_ref, o_ref):
    c = c_ref[...]
    o_ref[...] = jnp.dot(_silu(c), w_ref[...], precision=HI, preferred_element_type=F32) + b_ref[...]


def _adaln(c, ada_w, ada_b):
    depth, d, n6 = ada_w.shape
    bsz = c.shape[0]
    tn = 1536
    return pl.pallas_call(
        _ada_kernel,
        grid=(depth, n6 // tn),
        in_specs=[
            pl.BlockSpec((bsz, d), lambda l, j: (0, 0)),
            pl.BlockSpec((None, d, tn), lambda l, j: (l, 0, j)),
            pl.BlockSpec((None, 1, tn), lambda l, j: (l, 0, j)),
        ],
        out_specs=pl.BlockSpec((None, bsz, tn), lambda l, j: (l, 0, j)),
        out_shape=jax.ShapeDtypeStruct((depth, bsz, n6), F32),
        compiler_params=_cparams("arbitrary", "arbitrary"),
        name="adaln_mod",
    )(c, ada_w, ada_b.reshape(depth, 1, n6))


def _inproj_kernel(x_ref, nw_ref, sh_ref, sc_ref, w_ref, o_ref):
    h = _norm_mod(x_ref[...], nw_ref[...], sh_ref[...], sc_ref[...])
    o_ref[...] = jnp.dot(h.astype(BF16), w_ref[...], preferred_element_type=F32)


def _inproj(x2, nw, mod3, w_cat, seq):
    n, d = x2.shape
    tm = TM_INPROJ
    tpb = seq // tm
    return pl.pallas_call(
        _inproj_kernel,
        grid=(n // tm,),
        in_specs=[
            pl.BlockSpec((tm, d), lambda i: (i, 0)),
            pl.BlockSpec((1, d), lambda i: (0, 0)),
            pl.BlockSpec((None, 1, d), lambda i: ((i // tpb) * N_ADA + 0, 0, 0)),
            pl.BlockSpec((None, 1, d), lambda i: ((i // tpb) * N_ADA + 1, 0, 0)),
            pl.BlockSpec((d, U_WIDTH), lambda i: (0, 0), pipeline_mode=pl.Buffered(1)),
        ],
        out_specs=pl.BlockSpec((tm, U_WIDTH), lambda i: (i, 0)),
        out_shape=jax.ShapeDtypeStruct((n, U_WIDTH), F32),
        compiler_params=_cparams("arbitrary"),
        name="inproj",
    )(x2, nw, mod3, mod3, w_cat)


def _causal_conv(cur_ref, ext, tail, cw_ref, cb_ref):
    rows, width = cur_ref.shape
    cur = cur_ref[...]
    ext[0:SUBLANES, :] = tail[...]
    ext[SUBLANES:SUBLANES + rows, :] = cur
    tail[...] = cur[rows - SUBLANES:rows, :]
    groups = ext[...].reshape(rows // SUBLANES + 1, SUBLANES, width)
    sub = lax.broadcasted_iota(jnp.int32, (1, SUBLANES, width), 1)
    acc = cb_ref[...] + cur * cw_ref[CONV_WIDTH - 1:CONV_WIDTH, :]
    for d in range(1, CONV_WIDTH):
        rot = pltpu.roll(groups, d, 1)
        back = jnp.where(sub < d, rot[:-1], rot[1:]).reshape(rows, width)
        acc = acc + back * cw_ref[CONV_WIDTH - 1 - d:CONV_WIDTH - d, :]
    return acc


def _ssd_kernel(z_ref, xs_ref, bc_ref, dt_ref, cwx_ref, cbx_ref, cwb_ref, cbb_ref, dtb_ref, alog_ref,
                dful_ref, nw_ref, e_ref, o_ref, extx, extb, tailx, tailb, hstate, convx, convb):
    @pl.when(pl.program_id(1) == 0)
    def _():
        tailx[...] = jnp.zeros_like(tailx)
        tailb[...] = jnp.zeros_like(tailb)
        hstate[...] = jnp.zeros_like(hstate)

    convx[...] = _causal_conv(xs_ref, extx, tailx, cwx_ref, cbx_ref)
    convb[...] = _causal_conv(bc_ref, extb, tailb, cwb_ref, cbb_ref)
    for j in range(xs_ref.shape[0] // SSD_CHUNK):
        _ssd_chunk(slice(j * SSD_CHUNK, (j + 1) * SSD_CHUNK), z_ref, convx, convb, dt_ref, dtb_ref, alog_ref,
                   dful_ref, nw_ref, e_ref, o_ref, hstate)


def _ssd_chunk(rs, z_ref, convx, convb, dt_ref, dtb_ref, alog_ref, dful_ref, nw_ref, e_ref, o_ref, hstate):
    q = SSD_CHUNK
    xs = _silu(convx[rs, :])
    bc = _silu(convb[rs, :])

    dt = _softplus(dt_ref[rs, :] + dtb_ref[...])
    a = dt * (-LOG2E * jnp.exp(alog_ref[...]))
    ri = lax.broadcasted_iota(jnp.int32, (q, q), 0)
    ci = lax.broadcasted_iota(jnp.int32, (q, q), 1)
    tril = ri >= ci
    tril_f = tril.astype(F32)
    tril_b = tril.astype(BF16)
    acum = jnp.dot(jnp.concatenate([tril_b] * 3, axis=1), jnp.concatenate(_split3(a), axis=0),
                   preferred_element_type=F32)
    acum_t = acum.T
    expand3 = e_ref[...]
    dt_full = jnp.dot(jnp.concatenate(_split3(dt), axis=1), expand3, preferred_element_type=F32)
    acum_full = jnp.dot(jnp.concatenate(_split3(acum), axis=1), expand3, preferred_element_type=F32)
    alast_full = acum_full[q - 1:q, :]

    xdt = xs * dt_full
    xdt_b = xdt.astype(BF16)
    exp_a = jnp.exp2(acum_full)
    xd_b = (xdt * jnp.exp2(alast_full - acum_full)).astype(BF16)
    state_decay = jnp.exp2(alast_full)
    left = lax.broadcasted_iota(jnp.int32, (q, LANES), 1) < SSD_HEADDIM
    zero_b = jnp.zeros((q, LANES), BF16)

    ys = []
    for g in range(SSD_GROUPS):
        b_g = bc[:, g * SSD_STATE:(g + 1) * SSD_STATE]
        c_g = bc[:, (SSD_GROUPS + g) * SSD_STATE:(SSD_GROUPS + g + 1) * SSD_STATE]
        c_b = c_g.astype(BF16)
        cb = lax.dot_general(c_b, b_g.astype(BF16), (((1,), (1,)), ((), ())), preferred_element_type=F32)
        cb = cb * tril_f
        cs = slice(g * SSD_GROUP_W, (g + 1) * SSD_GROUP_W)
        h_g = hstate[:, cs]
        y_off = jnp.dot(c_b, h_g.astype(BF16), preferred_element_type=F32) * exp_a[:, cs]
        pieces = []
        for pr in range(SSD_HEADS // SSD_GROUPS // 2):
            h0 = g * (SSD_HEADS // SSD_GROUPS) + 2 * pr
            ms = []
            for h in (h0, h0 + 1):
                col = acum[:, h:h + 1]
                row = acum_t[h:h + 1, :]
                ms.append((cb * jnp.exp2(jnp.minimum(col - row, 0.0))).astype(BF16))
            lhs = jnp.concatenate(ms, axis=1)
            xp = xdt_b[:, h0 * SSD_HEADDIM:(h0 + 2) * SSD_HEADDIM]
            rhs = jnp.concatenate([jnp.where(left, xp, zero_b), jnp.where(left, zero_b, xp)], axis=0)
            pieces.append(jnp.dot(lhs, rhs, preferred_element_type=F32))
        ys.append(jnp.concatenate(pieces, axis=1) + y_off)
        b_t = b_g.T.astype(BF16)
        hstate[:, cs] = h_g * state_decay[:, cs] + jnp.dot(b_t, xd_b[:, cs], preferred_element_type=F32)

    y = jnp.concatenate(ys, axis=1) + xs * dful_ref[...]
    y = y * _silu(z_ref[rs, :])
    outs = []
    for g in range(SSD_GROUPS):
        cs = slice(g * SSD_GROUP_W, (g + 1) * SSD_GROUP_W)
        yg = y[:, cs]
        ms = jnp.mean(yg * yg, axis=-1, keepdims=True)
        outs.append(yg * lax.rsqrt(ms + EPS) * nw_ref[:, cs])
    o_ref[rs, :] = jnp.concatenate(outs, axis=1).astype(o_ref.dtype)


def _ssd(u, p, bsz, seq):
    q = R_SSD
    nc = seq // q
    n = bsz * seq

    def rows(b, c):
        return b * nc + c

    def const(shape):
        return pl.BlockSpec(shape, lambda b, c: (0,) * len(shape))

    return pl.pallas_call(
        _ssd_kernel,
        grid=(bsz, nc),
        in_specs=[
            pl.BlockSpec((q, SSD_INNER), lambda b, c: (rows(b, c), COL_Z // SSD_INNER)),
            pl.BlockSpec((q, SSD_INNER), lambda b, c: (rows(b, c), COL_XS // SSD_INNER)),
            pl.BlockSpec((q, 512), lambda b, c: (rows(b, c), COL_BC // 512)),
            pl.BlockSpec((q, LANES), lambda b, c: (rows(b, c), COL_DT // LANES)),
            const((CONV_WIDTH, SSD_INNER)), const((1, SSD_INNER)),
            const((CONV_WIDTH, 512)), const((1, 512)),
            const((1, LANES)), const((1, LANES)),
            const((1, SSD_INNER)), const((1, SSD_INNER)),
            const((3 * LANES, SSD_INNER)),
        ],
        out_specs=pl.BlockSpec((q, SSD_INNER), lambda b, c: (rows(b, c), 0)),
        out_shape=jax.ShapeDtypeStruct((n, SSD_INNER), BF16),
        scratch_shapes=[
            pltpu.VMEM((q + SUBLANES, SSD_INNER), F32),
            pltpu.VMEM((q + SUBLANES, 512), F32),
            pltpu.VMEM((SUBLANES, SSD_INNER), F32),
            pltpu.VMEM((SUBLANES, 512), F32),
            pltpu.VMEM((SSD_STATE, SSD_INNER), F32),
            pltpu.VMEM((q, SSD_INNER), F32),
            pltpu.VMEM((q, 512), F32),
        ],
        compiler_params=_cparams("arbitrary", "arbitrary"),
        name="ssd_mixer",
    )(u, u, u, u, p["cwx"], p["cbx"], p["cwb"], p["cbb"], p["dtb"], p["alog"], p["dful"], p["ssd_nw"], p["expand"])


def _boundary_rows(b, lvl):
    half = 1 << lvl
    parts = []
    for v in range(b.shape[0] // SUBLANES):
        r0 = v * SUBLANES
        if 2 * half >= SUBLANES:
            src = (r0 // (2 * half)) * (2 * half) + half - 1
            parts.append(jnp.broadcast_to(b[src:src + 1, :], (SUBLANES, b.shape[1])))
        else:
            sub = lax.broadcasted_iota(jnp.int32, (SUBLANES, b.shape[1]), 0)
            piece = None
            for g in range(SUBLANES // (2 * half)):
                src = r0 + g * 2 * half + half - 1
                cand = jnp.broadcast_to(b[src:src + 1, :], (SUBLANES, b.shape[1]))
                piece = cand if piece is None else jnp.where(sub >= g * 2 * half, cand, piece)
            parts.append(piece)
    return jnp.concatenate(parts, axis=0)


def _hgrn2_lru_kernel(layer, q_ref, f_ref, v_ref, g_ref, lb_ref, nw_ref, msk_ref, sgn_ref,
                      lg_ref, lx_ref, cw_ref, cb_ref, wa_ref, ba_ref, wx_ref, bx_ref, lam_ref, lnw_ref,
                      o_ref, ol_ref, state_t, ext, tail, hcarry):
    t = pl.program_id(1)
    ch = HG_CHUNK

    @pl.when(t == 0)
    def _():
        state_t[...] = jnp.zeros_like(state_t)
        tail[...] = jnp.zeros_like(tail)
        hcarry[...] = jnp.zeros_like(hcarry)

    _lru_tile(lg_ref, lx_ref, cw_ref, cb_ref, wa_ref, ba_ref, wx_ref, bx_ref, lam_ref, lnw_ref, ol_ref,
              ext, tail, hcarry)

    lrows = [lb_ref[j:j + 1, :] for j in range(lb_ref.shape[0])]
    mx = functools.reduce(jnp.maximum, lrows)
    es = [jnp.exp(r - mx) for r in lrows]
    den = functools.reduce(lambda a_, b_: a_ + b_, es)
    lb = jnp.zeros_like(mx)
    for j in range(1, layer + 1):
        lb = lb + es[j] / den
    one_minus_lb = 1.0 - lb
    nw = nw_ref[...]

    ri = lax.broadcasted_iota(jnp.int32, (ch, ch), 0)
    ci = lax.broadcasted_iota(jnp.int32, (ch, ch), 1)
    tril3 = jnp.concatenate([(ri >= ci).astype(BF16)] * 3, axis=1)
    rowi = lax.broadcasted_iota(jnp.int32, (ch, HG_EXPAND), 0)
    tgt = [((rowi >> lvl) & 1) == 1 for lvl in range(HG_LEVELS)]

    def head_chunk(h, rs):
        cs = slice(h * HG_EXPAND, (h + 1) * HG_EXPAND)
        qq = _silu(q_ref[rs, cs])
        kk = one_minus_lb[:, cs] * _sigmoid(-f_ref[rs, cs])
        logf = jnp.log1p(-kk)
        vv = v_ref[rs, cs]
        vb = vv.astype(BF16)
        b = jnp.dot(tril3, jnp.concatenate(_split3(logf), axis=0),
                    preferred_element_type=F32)
        st = state_t[h]
        o = lax.dot_general((qq * jnp.exp(b)).astype(BF16), st.astype(BF16), (((1,), (1,)), ((), ())),
                            preferred_element_type=F32)
        attn = jnp.zeros((ch, ch), F32)
        for lvl in range(HG_LEVELS):
            if lvl == 0:
                qk = jnp.where(tgt[0], qq * (1.0 - kk), kk)
            else:
                m = _boundary_rows(b, lvl)
                qk = jnp.where(tgt[lvl], qq, kk) * jnp.exp2((b - m) * sgn_ref[lvl])
            qk = qk.astype(BF16)
            pairs = lax.dot_general(qk, qk, (((1,), (1,)), ((), ())), preferred_element_type=F32)
            attn = attn + msk_ref[lvl] * pairs
        diag = jnp.sum(qq * kk, axis=-1, keepdims=True)
        o = o + jnp.dot(attn.astype(BF16), vb, preferred_element_type=F32) + diag * vv
        b_last = b[ch - 1:ch, :]
        kd = (kk * jnp.exp(b_last - b)).astype(BF16)
        state_t[h] = st * jnp.exp(b_last) + jnp.dot(vv.T.astype(BF16), kd, preferred_element_type=F32)
        ms = jnp.mean(o * o, axis=-1, keepdims=True)
        y = o * lax.rsqrt(ms + EPS) * nw[:, cs]
        o_ref[rs, cs] = (y * _silu(g_ref[rs, cs])).astype(o_ref.dtype)

    for j in range(q_ref.shape[0] // ch):
        for h in range(HG_HEADS):
            head_chunk(h, slice(j * ch, (j + 1) * ch))


def _lru_tile(g_ref, x_ref, cw_ref, cb_ref, wa_ref, ba_ref, wx_ref, bx_ref, lam_ref, nw_ref, o_ref,
              ext, tail, hcarry):
    rows = x_ref.shape[0]
    xb = _causal_conv(x_ref, ext, tail, cw_ref, cb_ref)
    xbb = xb.astype(BF16)
    npair = LRU_WIDTH // LANES
    ra = jnp.concatenate([jnp.dot(xbb[:, j * LANES:(j + 1) * LANES], wa_ref[j], preferred_element_type=F32)
                          for j in range(npair)], axis=1)
    rx = jnp.concatenate([jnp.dot(xbb[:, j * LANES:(j + 1) * LANES], wx_ref[j], preferred_element_type=F32)
                          for j in range(npair)], axis=1)
    r = _sigmoid(ra + ba_ref[...])
    i = _sigmoid(rx + bx_ref[...])
    log_a = -LRU_C * r * _softplus(-lam_ref[...])
    a = jnp.exp(log_a)
    th = jnp.tanh(log_a)
    s = -2.0 * th
    root = s * lax.rsqrt(jnp.maximum(s * (1.0 - th), F32_TINY))
    u = root * (i * xb)

    ngroups = rows // SUBLANES
    sub = lax.broadcasted_iota(jnp.int32, (1, SUBLANES, LRU_WIDTH), 1)
    acc_a = a.reshape(ngroups, SUBLANES, LRU_WIDTH)
    acc_u = u.reshape(ngroups, SUBLANES, LRU_WIDTH)
    d = 1
    while d < SUBLANES:
        keep = sub >= d
        a_sh = jnp.where(keep, pltpu.roll(acc_a, d, 1), 1.0)
        u_sh = jnp.where(keep, pltpu.roll(acc_u, d, 1), 0.0)
        acc_u = acc_a * u_sh + acc_u
        acc_a = acc_a * a_sh
        d *= 2
    acc_a = acc_a.reshape(rows, LRU_WIDTH)
    acc_u = acc_u.reshape(rows, LRU_WIDTH)
    carry = hcarry[0:1, :]
    groups = []
    for g in range(rows // SUBLANES):
        gs = slice(g * SUBLANES, (g + 1) * SUBLANES)
        hg = acc_a[gs, :] * carry + acc_u[gs, :]
        groups.append(hg)
        carry = hg[SUBLANES - 1:SUBLANES, :]
    h = jnp.concatenate(groups, axis=0)
    hcarry[0:1, :] = carry

    gate = g_ref[...]
    gelu = 0.5 * gate * (1.0 + jnp.tanh(np.sqrt(2.0 / np.pi).astype(np.float32) * (gate + 0.044715 * (gate * gate * gate))))
    y = h * gelu
    ms = jnp.mean(y * y, axis=-1, keepdims=True)
    o_ref[...] = (y * lax.rsqrt(ms + EPS) * nw_ref[...]).astype(o_ref.dtype)


def _hgrn2_lru(u, p, layer, bsz, seq):
    r = R_HG
    nt = seq // r
    n = bsz * seq
    npair = LRU_WIDTH // LANES

    def col(base, width):
        return pl.BlockSpec((r, width), lambda b, t: (b * nt + t, base // width))

    def const(shape):
        return pl.BlockSpec(shape, lambda b, t: (0,) * len(shape))

    def out(width):
        return pl.BlockSpec((r, width), lambda b, t: (b * nt + t, 0))

    return pl.pallas_call(
        functools.partial(_hgrn2_lru_kernel, layer),
        grid=(bsz, nt),
        in_specs=[
            col(COL_HQ, HG_WIDTH), col(COL_HF, HG_WIDTH), col(COL_HV, HG_WIDTH), col(COL_HG, HG_WIDTH),
            const(p["hg_lb"].shape), const((1, HG_WIDTH)), const((HG_LEVELS, HG_CHUNK, HG_CHUNK)),
            const((HG_LEVELS, HG_CHUNK, HG_EXPAND)),
            col(COL_LG, LRU_WIDTH), col(COL_LX, LRU_WIDTH),
            const((CONV_WIDTH, LRU_WIDTH)), const((1, LRU_WIDTH)),
            const((npair, LANES, LANES)), const((1, LRU_WIDTH)),
            const((npair, LANES, LANES)), const((1, LRU_WIDTH)),
            const((1, LRU_WIDTH)), const((1, LRU_WIDTH)),
        ],
        out_specs=[out(HG_WIDTH), out(LRU_WIDTH)],
        out_shape=[jax.ShapeDtypeStruct((n, HG_WIDTH), BF16), jax.ShapeDtypeStruct((n, LRU_WIDTH), BF16)],
        scratch_shapes=[
            pltpu.VMEM((HG_HEADS, HG_EXPAND, HG_EXPAND), F32),
            pltpu.VMEM((r + SUBLANES, LRU_WIDTH), F32),
            pltpu.VMEM((SUBLANES, LRU_WIDTH), F32),
            pltpu.VMEM((SUBLANES, LRU_WIDTH), F32),
        ],
        compiler_params=_cparams("arbitrary", "arbitrary"),
        name="hgrn2_lru_mixer",
    )(u, u, u, u, p["hg_lb"], p["hg_nw"], p["hg_msk"], p["hg_sgn"],
      u, u, p["lru_cw"], p["lru_cb"], p["lru_wa"], p["lru_ba"], p["lru_wx"], p["lru_bx"], p["lru_lam"], p["lru_nw"])


def _outproj_kernel(ys_ref, yh_ref, yl_ref, x_ref, w_ref, g_ref, nw_ref, sh_ref, sc_ref, xo_ref, h0, h1, h2, h3,
                    w_b):
    @pl.when(pl.program_id(0) == 0)
    def _():
        w_b[...] = w_ref[...].astype(BF16)

    acc = jnp.dot(ys_ref[...], w_b[0:SSD_INNER, :], preferred_element_type=F32)
    acc = acc + jnp.dot(yh_ref[...], w_b[SSD_INNER:SSD_INNER + HG_WIDTH, :], preferred_element_type=F32)
    acc = acc + jnp.dot(yl_ref[...], w_b[SSD_INNER + HG_WIDTH:, :], preferred_element_type=F32)
    xn = x_ref[...] + g_ref[...] * acc
    xo_ref[...] = xn
    _store_planes((h0, h1, h2, h3), _norm_mod(xn, nw_ref[...], sh_ref[...], sc_ref[...]))


def _outproj(y_ssd, y_hg, y_lru, x2, w_out, layer, nw, mod3, seq):
    n, d = x2.shape
    tm = TM_OUTPROJ
    tpb = seq // tm

    def modspec(j):
        return pl.BlockSpec((None, 1, d), lambda i: ((i // tpb) * N_ADA + j, 0, 0))

    outs = pl.pallas_call(
        _outproj_kernel,
        grid=(n // tm,),
        in_specs=[
            pl.BlockSpec((tm, SSD_INNER), lambda i: (i, 0)),
            pl.BlockSpec((tm, HG_WIDTH), lambda i: (i, 0)),
            pl.BlockSpec((tm, LRU_WIDTH), lambda i: (i, 0)),
            pl.BlockSpec((tm, d), lambda i: (i, 0)),
            pl.BlockSpec((None,) + w_out.shape[1:], lambda i: (layer, 0, 0), pipeline_mode=pl.Buffered(1)),
            modspec(2),
            pl.BlockSpec((1, d), lambda i: (0, 0)),
            modspec(3), modspec(4),
        ],
        out_specs=[pl.BlockSpec((tm, d), lambda i: (i, 0))] + [_plane_spec(tm)] * N_PLANES,
        out_shape=[jax.ShapeDtypeStruct((n, d), F32)] + [jax.ShapeDtypeStruct((n, LANES), jnp.uint32)] * N_PLANES,
        scratch_shapes=[pltpu.VMEM(w_out.shape[1:], BF16)],
        compiler_params=_cparams("arbitrary"),
        name="outproj",
    )(y_ssd, y_hg, y_lru, x2, w_out, mod3, nw, mod3, mod3)
    return outs[0], tuple(outs[1:])


def _plane_spec(tm):
    return pl.BlockSpec((tm, LANES), lambda i: (i, 0))


def _router_kernel(h0, h1, h2, h3, rw_ref, rb_ref, eid_ref, rank_ref, wt_ref, cnt_ref, carry, wscr):
    tm = h0.shape[0]

    @pl.when(pl.program_id(0) == 0)
    def _():
        carry[...] = jnp.zeros_like(carry)

    hb = _load_planes((h0, h1, h2, h3)).astype(BF16)
    logit_t = sum(lax.dot_general(part, hb, (((1,), (1,)), ((), ())), preferred_element_type=F32)
                  for part in _split3(rw_ref[...]))
    score = _sigmoid(logit_t)
    sel = score + rb_ref[...]
    neg_inf = jnp.float32(-jnp.inf)
    io_g = lax.broadcasted_iota(jnp.int32, (E_PER_GROUP, tm), 0)
    blocks, gscore = [], []
    for g in range(N_EXPERT_GROUPS):
        blk = sel[g * E_PER_GROUP:(g + 1) * E_PER_GROUP, :]
        m1 = jnp.max(blk, axis=0, keepdims=True)
        i1 = jnp.min(jnp.where(blk == m1, io_g, E_PER_GROUP), axis=0, keepdims=True)
        m2 = jnp.max(jnp.where(io_g == i1, neg_inf, blk), axis=0, keepdims=True)
        blocks.append(blk)
        gscore.append(m1 + m2)
    masked = []
    for g in range(N_EXPERT_GROUPS):
        rank = jnp.zeros((1, tm), jnp.int32)
        for o in range(N_EXPERT_GROUPS):
            if o == g:
                continue
            beats = (gscore[o] > gscore[g]) | ((gscore[o] == gscore[g]) & (o < g))
            rank = rank + beats.astype(jnp.int32)
        masked.append(jnp.where(rank < TOPK_GROUPS, blocks[g], MASK_SCORE))
    val = jnp.concatenate(masked, axis=0)
    io_e = lax.broadcasted_iota(jnp.int32, (N_EXPERTS, tm), 0)
    chosen = jnp.zeros((N_EXPERTS, tm), jnp.bool_)
    picks = []
    for k in range(TOP_K):
        m = jnp.max(val, axis=0, keepdims=True)
        idx = jnp.min(jnp.where(val == m, io_e, N_EXPERTS), axis=0, keepdims=True)
        pick = io_e == idx
        picks.append(pick)
        eid_ref[k:k + 1, :] = idx
        chosen = chosen | pick
        val = jnp.where(pick, neg_inf, val)
    w = jnp.where(chosen, score, 0.0)
    w = w / jnp.sum(w, axis=0, keepdims=True) * ROUTED_SCALE

    chosen_f = chosen.astype(F32)
    earlier = (lax.broadcasted_iota(jnp.int32, (tm, tm), 0) < lax.broadcasted_iota(jnp.int32, (tm, tm), 1))
    before = jnp.dot(chosen_f.astype(BF16), earlier.astype(BF16), preferred_element_type=F32)
    grank = carry[:, 0:1] + before
    wscr[...] = jnp.zeros_like(wscr)
    for k in range(TOP_K):
        rank_ref[k:k + 1, :] = jnp.sum(jnp.where(picks[k], grank, 0.0), axis=0, keepdims=True).astype(jnp.int32)
        wscr[k:k + 1, :] = jnp.sum(jnp.where(picks[k], w, 0.0), axis=0, keepdims=True)
    wt_ref[...] = wscr[...].T
    carry[...] = carry[...] + jnp.sum(chosen_f, axis=1, keepdims=True)
    cnt_ref[...] = carry[...]


def _router(hp, rw_t, rb):
    n = hp[0].shape[0]
    tm = TM_ROUTER
    return pl.pallas_call(
        _router_kernel,
        grid=(n // tm,),
        in_specs=[_plane_spec(tm)] * N_PLANES + [
            pl.BlockSpec(rw_t.shape, lambda i: (0, 0)),
            pl.BlockSpec((N_EXPERTS, 1), lambda i: (0, 0)),
        ],
        out_specs=[
            pl.BlockSpec((TOP_K, tm), lambda i: (0, i)),
            pl.BlockSpec((TOP_K, tm), lambda i: (0, i)),
            pl.BlockSpec((tm, LANES), lambda i: (i, 0)),
            pl.BlockSpec((N_EXPERTS, LANES), lambda i: (0, 0)),
        ],
        out_shape=[
            jax.ShapeDtypeStruct((TOP_K, n), jnp.int32),
            jax.ShapeDtypeStruct((TOP_K, n), jnp.int32),
            jax.ShapeDtypeStruct((n, LANES), F32),
            jax.ShapeDtypeStruct((N_EXPERTS, LANES), F32),
        ],
        scratch_shapes=[pltpu.VMEM((N_EXPERTS, LANES), F32), pltpu.VMEM((LANES, tm), F32)],
        compiler_params=_cparams("arbitrary"),
        name="router",
    )(*hp, rw_t, rb)


def _plan_kernel(n_slots, cnt_ref, eid_ref, rank_ref, dest_ref, be_ref, nb_ref, pad_ref, np_ref):
    cnt = cnt_ref[...].astype(jnp.int32)
    padded = ((cnt + (MOE_BLOCK - 1)) >> MOE_BLOCK_LOG2) << MOE_BLOCK_LOG2
    ri = lax.broadcasted_iota(jnp.int32, (N_EXPERTS, N_EXPERTS), 0)
    ci = lax.broadcasted_iota(jnp.int32, (N_EXPERTS, N_EXPERTS), 1)
    pad_end = jnp.dot((ri >= ci).astype(F32), padded.astype(F32), precision=HI,
                      preferred_element_type=F32).astype(jnp.int32)
    pad_start = pad_end - padded
    eid = eid_ref[...]
    dest = rank_ref[...]
    for e in range(N_EXPERTS):
        dest = dest + jnp.where(eid == e, pad_start[e:e + 1, 0:1], 0)
    dest_ref[...] = dest
    nbp = be_ref.shape[1]
    jpos = lax.broadcasted_iota(jnp.int32, (N_EXPERTS, nbp), 1) * MOE_BLOCK
    be = jnp.minimum(jnp.sum((pad_end[:, 0:1] <= jpos).astype(jnp.int32), axis=0, keepdims=True), N_EXPERTS - 1)
    be_ref[...] = be
    nb_ref[...] = pad_end[N_EXPERTS - 1:N_EXPERTS, :] >> MOE_BLOCK_LOG2
    fill_end = pad_start + cnt
    io_e = lax.broadcasted_iota(jnp.int32, (N_EXPERTS, nbp), 0)
    end_j = jnp.sum(jnp.where(io_e == be, fill_end[:, 0:1], 0), axis=0, keepdims=True)
    pieces = (end_j - jpos[0:1, :] + (XS_PIECE - 1)) >> XS_PIECE_LOG2
    np_ref[...] = jnp.clip(pieces, 0, MOE_BLOCK // XS_PIECE)
    lane = lax.broadcasted_iota(jnp.int32, (N_EXPERTS, LANES), 1)
    piece_end = ((fill_end + (XS_PIECE - 1)) >> XS_PIECE_LOG2) << XS_PIECE_LOG2
    s = fill_end + lane
    pad_ref[...] = jnp.where(s < piece_end, s, n_slots + lane)


def _plan(cnt, eid, rank, n_slots):
    n = eid.shape[1]
    nbp = -(-(n_slots // MOE_BLOCK) // LANES) * LANES
    pad_rows = N_EXPERTS
    return pl.pallas_call(
        functools.partial(_plan_kernel, n_slots),
        grid=(1,),
        in_specs=[
            pl.BlockSpec(cnt.shape, lambda i: (0, 0)),
            pl.BlockSpec(eid.shape, lambda i: (0, 0)),
            pl.BlockSpec(rank.shape, lambda i: (0, 0)),
        ],
        out_specs=[
            pl.BlockSpec((TOP_K, n), lambda i: (0, 0)),
            pl.BlockSpec((1, nbp), lambda i: (0, 0)),
            pl.BlockSpec((1, LANES), lambda i: (0, 0)),
            pl.BlockSpec((pad_rows, LANES), lambda i: (0, 0)),
            pl.BlockSpec((1, nbp), lambda i: (0, 0)),
        ],
        out_shape=[
            jax.ShapeDtypeStruct((TOP_K, n), jnp.int32),
            jax.ShapeDtypeStruct((1, nbp), jnp.int32),
            jax.ShapeDtypeStruct((1, LANES), jnp.int32),
            jax.ShapeDtypeStruct((pad_rows, LANES), jnp.int32),
            jax.ShapeDtypeStruct((1, nbp), jnp.int32),
        ],
        compiler_params=_cparams("arbitrary"),
        name="moe_plan",
    )(cnt, eid, rank)


def _sc_mesh():
    return plsc.VectorSubcoreMesh(core_axis_name="c", subcore_axis_name="s")


def _sc_worker():
    return lax.axis_index("c") * SC_SUBCORES + lax.axis_index("s")


def _sc_dispatch(hp, dest_rows, pad_rows, n_rows):
    n = hp[0].shape[0]
    tiles_per_worker = n // SC_WIN // SC_WORKERS
    pad_per_worker = pad_rows.shape[0] // SC_WORKERS
    zeros = jnp.zeros((SC_WIN, LANES), jnp.uint32)

    def body(*refs):
        h = refs[:N_PLANES]
        dest_hbm, pad_hbm, z_hbm = refs[N_PLANES:N_PLANES + 3]
        xs = refs[N_PLANES + 3:2 * N_PLANES + 3]
        bufs = refs[2 * N_PLANES + 3:3 * N_PLANES + 3]
        ibuf, pbuf, sem = refs[3 * N_PLANES + 3:]
        wid = _sc_worker()

        pltpu.sync_copy(z_hbm, bufs[0])
        pltpu.sync_copy(pad_hbm.at[pl.ds(wid * pad_per_worker, pad_per_worker)], pbuf)
        copies = [pltpu.async_copy(bufs[0], xs[c].at[pbuf.at[r]], sem)
                  for r in range(pad_per_worker) for c in range(N_PLANES)]
        for cp in copies:
            cp.wait()

        @pl.loop(0, tiles_per_worker)
        def _(i):
            tile = wid * tiles_per_worker + i
            pltpu.sync_copy(dest_hbm.at[pl.ds(tile * TOP_K, TOP_K)], ibuf)
            for c in range(N_PLANES):
                pltpu.sync_copy(h[c].at[pl.ds(tile * SC_WIN, SC_WIN)], bufs[c])
            scatters = [pltpu.async_copy(bufs[c], xs[c].at[ibuf.at[k]], sem)
                        for c in range(N_PLANES) for k in range(TOP_K)]
            for cp in scatters:
                cp.wait()

    out_type = tuple(jax.ShapeDtypeStruct((n_rows, LANES), jnp.uint32) for _ in range(N_PLANES))
    scratch = ([pltpu.VMEM((SC_WIN, LANES), jnp.uint32)] * N_PLANES
               + [pltpu.VMEM((TOP_K, LANES), jnp.int32), pltpu.VMEM((pad_per_worker, LANES), jnp.int32),
                  pltpu.SemaphoreType.DMA])
    return pl.kernel(body, out_type=out_type, mesh=_sc_mesh(), scratch_types=scratch,
                     name="moe_sc_dispatch")(*hp, dest_rows, pad_rows, zeros)


def _sc_gather(ysp, dest_rows, n):
    tiles_per_worker = n // SC_WIN // SC_WORKERS

    def body(*refs):
        ys = refs[:N_PLANES]
        dest_hbm, g_hbm = refs[N_PLANES:N_PLANES + 2]
        bufs = refs[N_PLANES + 2:N_PLANES + 2 + SC_GATHER_BUFS]
        ibuf, sem = refs[N_PLANES + 2 + SC_GATHER_BUFS:]
        wid = _sc_worker()

        @pl.loop(0, tiles_per_worker)
        def _(i):
            tile = wid * tiles_per_worker + i
            pltpu.sync_copy(dest_hbm.at[pl.ds(tile * TOP_K, TOP_K)], ibuf)
            for c in range(N_PLANES):
                for k0 in range(0, TOP_K, SC_GATHER_BUFS):
                    gathers = [pltpu.async_copy(ys[c].at[ibuf.at[k0 + j]], bufs[j], sem)
                               for j in range(SC_GATHER_BUFS)]
                    for cp in gathers:
                        cp.wait()
                    stores = [pltpu.async_copy(
                        bufs[j], g_hbm.at[pl.ds(((k0 + j) * N_PLANES + c) * n + tile * SC_WIN, SC_WIN)], sem)
                        for j in range(SC_GATHER_BUFS)]
                    for cp in stores:
                        cp.wait()

    scratch = ([pltpu.VMEM((SC_WIN, LANES), jnp.uint32)] * SC_GATHER_BUFS
               + [pltpu.VMEM((TOP_K, LANES), jnp.int32), pltpu.SemaphoreType.DMA])
    return pl.kernel(body, out_type=jax.ShapeDtypeStruct((TOP_K * N_PLANES * n, LANES), jnp.uint32),
                     mesh=_sc_mesh(), scratch_types=scratch, name="moe_sc_gather")(*ysp, dest_rows)


def _expert_kernel(be_ref, nb_ref, np_ref, *refs):
    xs_hbm = refs[:N_PLANES]
    wg_ref, wu_ref, wd_ref = refs[N_PLANES:N_PLANES + 3]
    ys_refs = refs[N_PLANES + 3:2 * N_PLANES + 3]
    wg_b, wu_b, wd_b, ring, sems = refs[2 * N_PLANES + 3:]
    j = pl.program_id(0)
    nb = nb_ref[0]
    used = j < nb
    new_expert = (j == 0) | (be_ref[j] != be_ref[jnp.maximum(j - 1, 0)])

    def fetch(step, slot, wait):
        for p in range(MOE_BLOCK // XS_PIECE):
            @pl.when(p < np_ref[step])
            def _():
                src = pl.ds(pl.multiple_of(step * MOE_BLOCK + p * XS_PIECE, XS_PIECE), XS_PIECE)
                for c in range(N_PLANES):
                    cp = pltpu.make_async_copy(xs_hbm[c].at[src], ring.at[slot, c, pl.ds(p * XS_PIECE, XS_PIECE)],
                                               sems.at[slot])
                    if wait:
                        cp.wait()
                    else:
                        cp.start()

    @pl.when(j == 0)
    def _():
        ring[...] = jnp.zeros_like(ring)

    for s in range(XS_RING - 1):
        @pl.when((j == 0) & (s < nb))
        def _():
            fetch(s, s, wait=False)

    ahead = j + (XS_RING - 1)

    @pl.when(used & (ahead < nb))
    def _():
        fetch(ahead, lax.rem(ahead, XS_RING), wait=False)

    @pl.when(used & new_expert)
    def _():
        wg_b[...] = wg_ref[...].astype(BF16)
        wu_b[...] = wu_ref[...].astype(BF16)
        wd_b[...] = wd_ref[...].astype(BF16)

    @pl.when(used)
    def _():
        slot = lax.rem(j, XS_RING)
        fetch(j, slot, wait=True)
        for s in range(MOE_BLOCK // EXPERT_SUB):
            rows = slice(s * EXPERT_SUB, (s + 1) * EXPERT_SUB)
            filled = np_ref[j] > s * (EXPERT_SUB // XS_PIECE)

            @pl.when(filled)
            def _():
                x = _unpack_bf16_pairs(
                    jnp.concatenate([ring[slot, c, rows, :] for c in range(N_PLANES)], axis=1)).astype(BF16)
                a = jnp.dot(x, wg_b[...], preferred_element_type=F32)
                u = jnp.dot(x, wu_b[...], preferred_element_type=F32)
                y = jnp.dot((_silu(a) * u).astype(BF16), wd_b[...], preferred_element_type=F32)
                packed = _pack_bf16_pairs(y)
                for c, ref in enumerate(ys_refs):
                    ref[rows, :] = packed[:, c * LANES:(c + 1) * LANES]

            @pl.when(jnp.logical_not(filled))
            def _():
                for ref in ys_refs:
                    ref[rows, :] = jnp.zeros((EXPERT_SUB, LANES), ref.dtype)

    @pl.when(jnp.logical_not(used))
    def _():
        for ref in ys_refs:
            ref[...] = jnp.zeros_like(ref)


def _experts(be, nb, npieces, xsp, wg, wu, wd, layer, n_slots):
    d = wg.shape[2]

    def wspec(shape):
        def index(j, be_ref, nb_ref, np_ref):
            return layer, be_ref[jnp.minimum(j, jnp.maximum(nb_ref[0] - 1, 0))], 0, 0
        return pl.BlockSpec((None, None) + shape, index)

    grid_spec = pltpu.PrefetchScalarGridSpec(
        num_scalar_prefetch=3,
        grid=(n_slots // MOE_BLOCK,),
        in_specs=[pl.BlockSpec(memory_space=pl.ANY)] * N_PLANES
        + [wspec((d, D_EXPERT)), wspec((d, D_EXPERT)), wspec((D_EXPERT, d))],
        out_specs=[pl.BlockSpec((MOE_BLOCK, LANES), lambda j, be_ref, nb_ref, np_ref: (j, 0))] * N_PLANES,
        scratch_shapes=[pltpu.VMEM((d, D_EXPERT), BF16), pltpu.VMEM((d, D_EXPERT), BF16),
                        pltpu.VMEM((D_EXPERT, d), BF16),
                        pltpu.VMEM((XS_RING, N_PLANES, MOE_BLOCK, LANES), jnp.uint32),
                        pltpu.SemaphoreType.DMA((XS_RING,))],
    )
    return pl.pallas_call(
        _expert_kernel,
        grid_spec=grid_spec,
        out_shape=[jax.ShapeDtypeStruct((n_slots, LANES), jnp.uint32)] * N_PLANES,
        compiler_params=_cparams("arbitrary"),
        name="moe_experts",
    )(be, nb, npieces, *xsp, wg, wu, wd)


def _combine_kernel(final, g_ref, wt_ref, h0, h1, h2, h3, sg_ref, su_ref, sd_ref, x_ref, gate_ref, fw_ref, o_ref,
                    sg_b, su_b, sd_b):
    @pl.when(pl.program_id(0) == 0)
    def _():
        sg_b[...] = sg_ref[...].astype(BF16)
        su_b[...] = su_ref[...].astype(BF16)
        sd_b[...] = sd_ref[...].astype(BF16)

    hb = _load_planes((h0, h1, h2, h3)).astype(BF16)
    a = jnp.dot(hb, sg_b[...], preferred_element_type=F32)
    u = jnp.dot(hb, su_b[...], preferred_element_type=F32)
    acc = jnp.dot((_silu(a) * u).astype(BF16), sd_b[...], preferred_element_type=F32)
    wt = wt_ref[...]
    for k in range(TOP_K):
        rows = _unpack_bf16_pairs(jnp.concatenate([g_ref[k * N_PLANES + c] for c in range(N_PLANES)], axis=1))
        acc = acc + wt[:, k:k + 1] * rows
    xn = x_ref[...] + gate_ref[...] * acc
    if final:
        ms = jnp.mean(xn * xn, axis=-1, keepdims=True)
        xn = xn * lax.rsqrt(ms + EPS) * fw_ref[...]
    o_ref[...] = xn


def _combine(g, wt, hp, sg, su, sd, layer, x2, mod3, fw, seq, final):
    n, d = x2.shape
    tm = TM_COMBINE
    tpb = seq // tm

    def wspec(w):
        return pl.BlockSpec((None,) + w.shape[1:], lambda i: (layer, 0, 0), pipeline_mode=pl.Buffered(1))

    return pl.pallas_call(
        functools.partial(_combine_kernel, final),
        grid=(n // tm,),
        in_specs=[
            pl.BlockSpec((TOP_K * N_PLANES, tm, LANES), lambda i: (0, i, 0)),
            pl.BlockSpec((tm, LANES), lambda i: (i, 0)),
        ] + [_plane_spec(tm)] * N_PLANES + [
            wspec(sg), wspec(su), wspec(sd),
            pl.BlockSpec((tm, d), lambda i: (i, 0)),
            pl.BlockSpec((None, 1, d), lambda i: ((i // tpb) * N_ADA + 5, 0, 0)),
            pl.BlockSpec((1, d), lambda i: (0, 0)),
        ],
        out_specs=pl.BlockSpec((tm, d), lambda i: (i, 0)),
        out_shape=jax.ShapeDtypeStruct((n, d), F32),
        scratch_shapes=[pltpu.VMEM(sg.shape[1:], BF16), pltpu.VMEM(su.shape[1:], BF16),
                        pltpu.VMEM(sd.shape[1:], BF16)],
        compiler_params=_cparams("arbitrary"),
        name="moe_combine",
    )(g.reshape(TOP_K * N_PLANES, n, LANES), wt, *hp, sg, su, sd, x2, mod3, fw)


def _blockdiag_pairs(w):
    z = jnp.zeros((LRU_BLOCK_W, LRU_BLOCK_W), w.dtype)
    tiles = []
    for j in range(LRU_BLOCKS // 2):
        top = jnp.concatenate([w[2 * j], z], axis=1)
        bot = jnp.concatenate([z, w[2 * j + 1]], axis=1)
        tiles.append(jnp.concatenate([top, bot], axis=0))
    return jnp.stack(tiles).astype(BF16)


def _hg_level_masks():
    ch = HG_CHUNK
    msk = np.zeros((HG_LEVELS, ch, ch), np.float32)
    for lvl in range(HG_LEVELS):
        half = 1 << lvl
        for t in range(ch):
            base = (t // (2 * half)) * (2 * half)
            if (t // half) % 2 == 1:
                msk[lvl, t, base:base + half] = 1.0
    later = ((np.arange(ch)[None, :, None] >> np.arange(HG_LEVELS)[:, None, None]) & 1).astype(np.float32)
    later = np.broadcast_to(later, (HG_LEVELS, ch, HG_EXPAND))
    sgn = (2.0 * later - 1.0) * np.float32(LOG2E)
    return jnp.asarray(msk), jnp.asarray(sgn, dtype=F32)


def _ssd_expand():
    e = np.zeros((LANES, SSD_INNER), np.float32)
    for h in range(SSD_HEADS):
        e[h, h * SSD_HEADDIM:(h + 1) * SSD_HEADDIM] = 1.0
    return jnp.asarray(np.concatenate([e] * 3, axis=0), dtype=BF16)


def _pad_lanes(v, width):
    return jnp.pad(v, (0, width - v.shape[0])).reshape(1, width)


def _layer_params(l, w_in, ssd_conv_w, ssd_conv_b, ssd_dt_bias, ssd_a_log, ssd_d, ssd_norm_w, hg_lower_bounds,
                  hg_norm_w, lru_conv_w, lru_conv_b, lru_wa, lru_ba, lru_wx, lru_bx, lru_lambda, lru_norm_w):
    wi = w_in[l]
    dt0 = SSD_INNER + SSD_INNER + 2 * SSD_GROUPS * SSD_STATE
    w_cat = jnp.concatenate([wi[:, :dt0], wi[:, dt0 + SSD_HEADS:], wi[:, dt0:dt0 + SSD_HEADS]], axis=1)
    w_cat = jnp.pad(w_cat, ((0, 0), (0, U_WIDTH - w_cat.shape[1]))).astype(BF16)
    msk, sgn = _hg_level_masks()
    return dict(
        w_cat=w_cat,
        cwx=ssd_conv_w[l][:, :SSD_INNER], cbx=ssd_conv_b[l][:SSD_INNER].reshape(1, -1),
        cwb=ssd_conv_w[l][:, SSD_INNER:], cbb=ssd_conv_b[l][SSD_INNER:].reshape(1, -1),
        dtb=_pad_lanes(ssd_dt_bias[l], LANES), alog=_pad_lanes(ssd_a_log[l], LANES),
        dful=jnp.repeat(ssd_d[l], SSD_HEADDIM).reshape(1, -1), ssd_nw=ssd_norm_w[l].reshape(1, -1),
        expand=_ssd_expand(),
        hg_lb=hg_lower_bounds, hg_nw=hg_norm_w[l].reshape(1, -1), hg_msk=msk, hg_sgn=sgn,
        lru_cw=lru_conv_w[l], lru_cb=lru_conv_b[l].reshape(1, -1),
        lru_wa=_blockdiag_pairs(lru_wa[l]), lru_ba=lru_ba[l].reshape(1, -1),
        lru_wx=_blockdiag_pairs(lru_wx[l]), lru_bx=lru_bx[l].reshape(1, -1),
        lru_lam=lru_lambda[l].reshape(1, -1), lru_nw=lru_norm_w[l].reshape(1, -1),
    )


def kernel(x, c, ada_w, ada_b, norm_mix_w, norm_ffn_w, w_in, ssd_conv_w, ssd_conv_b, ssd_dt_bias, ssd_a_log, ssd_d, ssd_norm_w, hg_lower_bounds, hg_norm_w, lru_conv_w, lru_conv_b, lru_wa, lru_ba, lru_wx, lru_bx, lru_lambda, lru_norm_w, w_out, router_w, router_bias, exp_gate, exp_up, exp_down, sh_gate, sh_up, sh_down, final_norm_w):
    bsz, seq, d = x.shape
    depth = ada_w.shape[0]
    assert d == D_MODEL and seq % TM_OUTPROJ == 0 and seq % SSD_CHUNK == 0 and seq % R_HG == 0
    n = bsz * seq
    assert n % (SC_WIN * SC_WORKERS) == 0
    n_slots = n * TOP_K + N_EXPERTS * MOE_BLOCK
    x2 = x.reshape(n, d)
    mod = _adaln(c, ada_w, ada_b)
    fw = final_norm_w.reshape(1, d)
    for l in range(depth):
        p = _layer_params(l, w_in, ssd_conv_w, ssd_conv_b, ssd_dt_bias, ssd_a_log, ssd_d, ssd_norm_w,
                          hg_lower_bounds, hg_norm_w, lru_conv_w, lru_conv_b, lru_wa, lru_ba, lru_wx, lru_bx,
                          lru_lambda, lru_norm_w)
        mod3 = mod[l].reshape(bsz * N_ADA, 1, d)
        u = _inproj(x2, norm_mix_w[l].reshape(1, d), mod3, p["w_cat"], seq)
        y_ssd = _ssd(u, p, bsz, seq)
        y_hg, y_lru = _hgrn2_lru(u, p, l, bsz, seq)
        x2, hp = _outproj(y_ssd, y_hg, y_lru, x2, w_out, l, norm_ffn_w[l].reshape(1, d), mod3, seq)
        eid, rank, wt, cnt = _router(hp, router_w[l].T, router_bias[l].reshape(N_EXPERTS, 1))
        dest, be, nb, pad_rows, npieces = _plan(cnt, eid, rank, n_slots)
        dest_rows = dest.reshape(TOP_K, n // LANES, LANES).transpose(1, 0, 2).reshape(n // LANES * TOP_K, LANES)
        xsp = _sc_dispatch(hp, dest_rows, pad_rows, n_slots + LANES)
        ysp = _experts(be.reshape(-1), nb[0, :1], npieces.reshape(-1), xsp, exp_gate, exp_up, exp_down, l, n_slots)
        g = _sc_gather(ysp, dest_rows, n)
        x2 = _combine(g, wt, hp, sh_gate, sh_up, sh_down, l, x2, mod3, fw, seq, final=(l == depth - 1))
    return x2.reshape(bsz, seq, d)
```
